```python
import jax, jax.numpy as jnp
from jax import lax
import numpy as np

D_MODEL = 1024
BATCH = 8
SEQ = 8192
DEPTH = 1

ATTN_WIDTH = D_MODEL // 2
POOL_WIDTH = D_MODEL - ATTN_WIDTH
HEAD_DIM = 64
N_Q_HEADS = ATTN_WIDTH // HEAD_DIM
N_KV_HEADS = 2
GQA_GROUP = N_Q_HEADS // N_KV_HEADS
KV_WIDTH = N_KV_HEADS * HEAD_DIM
WINDOW = 128
BLOCK = 128
ROPE_THETA = 10000.0
POOL_SIZES = (2, 4, 8, 16)
N_POOL_GROUPS = len(POOL_SIZES)
POOL_GROUP_WIDTH = POOL_WIDTH // N_POOL_GROUPS
IN_WIDTH = ATTN_WIDTH + 2 * KV_WIDTH + POOL_WIDTH
D_FF = -(-8 * D_MODEL // (3 * 256)) * 256
RMS_EPS = 1e-5

kernel_name = "hybrid_swa_sink_multiscale_pool_block"


def rmsnorm(x, g):
    xf = x.astype(jnp.float32)
    y = xf * lax.rsqrt(jnp.mean(xf * xf, axis=-1, keepdims=True) + RMS_EPS)
    return (y * g.astype(jnp.float32)).astype(x.dtype)


def rope_tables(seq):
    inv_freq = 1.0 / (ROPE_THETA ** (jnp.arange(0, HEAD_DIM, 2, dtype=jnp.float32) / HEAD_DIM))
    ang = jnp.arange(seq, dtype=jnp.float32)[:, None] * inv_freq[None, :]
    return jnp.cos(ang), jnp.sin(ang)


def apply_rope(t, cos, sin):
    t1, t2 = jnp.split(t.astype(jnp.float32), 2, axis=-1)
    c = cos[None, :, None, :]
    s = sin[None, :, None, :]
    return jnp.concatenate([t1 * c - t2 * s, t2 * c + t1 * s], axis=-1).astype(t.dtype)


def sliding_window_attention_with_sinks(q, k, v, sinks):
    b, s = q.shape[0], q.shape[1]
    nb = s // BLOCK
    qb = q.reshape(b, nb, BLOCK, N_KV_HEADS, GQA_GROUP, HEAD_DIM)

    def band(t):
        t = t.reshape(b, nb, BLOCK, N_KV_HEADS, HEAD_DIM)
        prev = jnp.pad(t, ((0, 0), (1, 0), (0, 0), (0, 0), (0, 0)))[:, :-1]
        return jnp.concatenate([prev, t], axis=2)

    kb, vb = band(k), band(v)
    scores = jnp.einsum('bnqkgd,bnskd->bnkgqs', qb, kb,
                        preferred_element_type=jnp.float32) * (HEAD_DIM ** -0.5)
    qi = jnp.arange(BLOCK)[:, None] + BLOCK
    sj = jnp.arange(2 * BLOCK)[None, :]
    delta = qi - sj
    in_window = (delta >= 0) & (delta < WINDOW)
    key_pos = jnp.arange(nb)[:, None] * BLOCK + sj - BLOCK
    mask = in_window[None] & (key_pos >= 0)[:, None, :]
    scores = jnp.where(mask[None, :, None, None], scores, -jnp.inf)
    sink = sinks.astype(jnp.float32).reshape(N_KV_HEADS, GQA_GROUP)[None, None, :, :, None, None]
    m = jnp.maximum(jnp.max(scores, axis=-1, keepdims=True), sink)
    p = jnp.exp(scores - m)
    p = p / (jnp.sum(p, axis=-1, keepdims=True) + jnp.exp(sink - m))
    out = jnp.einsum('bnkgqs,bnskd->bnqkgd', p.astype(v.dtype), vb)
    return out.reshape(b, s, N_Q_HEADS * HEAD_DIM)


def multiscale_causal_pool(u, w_pool, b_pool, pool_scale):
    b, s, _ = u.shape
    uf = u.astype(jnp.float32).reshape(b, s, N_POOL_GROUPS, POOL_GROUP_WIDTH)
    cs = jnp.pad(jnp.cumsum(uf, axis=1), ((0, 0), (1, 0), (0, 0), (0, 0)))
    t = jnp.arange(s)[:, None]
    sizes = jnp.array(POOL_SIZES, dtype=jnp.int32)[None, :]
    start = jnp.maximum(t + 1 - sizes, 0)
    g_idx = jnp.arange(N_POOL_GROUPS)[None, :]
    window_sum = cs[:, 1:] - cs[:, start, g_idx]
    count = (t + 1 - start).astype(jnp.float32)
    mixed = window_sum / count[None, :, :, None] - uf
    y = jnp.einsum('bsgc,gcd->bsgd', mixed.astype(u.dtype), w_pool) + b_pool
    y = y * pool_scale
    return y.reshape(b, s, POOL_WIDTH)


def _fwd_setup_inputs(seed: int = 0) -> dict:
    key = jax.random.key(seed)
    ks = jax.random.split(key, 16)
    f32 = jnp.float32
    nrm = lambda k, shape, scale: jax.random.normal(k, shape, f32) * scale
    return {
        "x": nrm(ks[0], (BATCH, SEQ, D_MODEL), 1.0),
        "g_mix": 1.0 + nrm(ks[1], (DEPTH, D_MODEL), 0.02),
        "w_in": nrm(ks[2], (DEPTH, D_MODEL, IN_WIDTH), D_MODEL ** -0.5),
        "b_in": nrm(ks[3], (DEPTH, IN_WIDTH), 0.02),
        "sinks": nrm(ks[4], (DEPTH, N_Q_HEADS), 1.0),
        "w_pool": nrm(ks[5], (DEPTH, N_POOL_GROUPS, POOL_GROUP_WIDTH, POOL_GROUP_WIDTH), POOL_GROUP_WIDTH ** -0.5),
        "b_pool": nrm(ks[6], (DEPTH, N_POOL_GROUPS, POOL_GROUP_WIDTH), 0.02),
        "pool_scale": 1.0 + nrm(ks[7], (DEPTH, N_POOL_GROUPS, POOL_GROUP_WIDTH), 0.1),
        "w_out": nrm(ks[8], (DEPTH, ATTN_WIDTH + POOL_WIDTH, D_MODEL), (ATTN_WIDTH + POOL_WIDTH) ** -0.5),
        "b_out": nrm(ks[9], (DEPTH, D_MODEL), 0.02),
        "g_ffn": 1.0 + nrm(ks[10], (DEPTH, D_MODEL), 0.02),
        "w_gate": nrm(ks[11], (DEPTH, D_MODEL, D_FF), D_MODEL ** -0.5),
        "w_up": nrm(ks[12], (DEPTH, D_MODEL, D_FF), D_MODEL ** -0.5),
        "w_down": nrm(ks[13], (DEPTH, D_FF, D_MODEL), D_FF ** -0.5),
        "g_final": 1.0 + nrm(ks[14], (D_MODEL,), 0.02),
    }


def _fwd_reference(x, g_mix, w_in, b_in, sinks, w_pool, b_pool, pool_scale, w_out, b_out,
              g_ffn, w_gate, w_up, w_down, g_final):
    b, s, _ = x.shape
    cos, sin = rope_tables(s)
    for i in range(DEPTH):
        h = rmsnorm(x, g_mix[i])
        z = h @ w_in[i] + b_in[i]
        q = z[..., :ATTN_WIDTH].reshape(b, s, N_Q_HEADS, HEAD_DIM)
        k = z[..., ATTN_WIDTH:ATTN_WIDTH + KV_WIDTH].reshape(b, s, N_KV_HEADS, HEAD_DIM)
        v = z[..., ATTN_WIDTH + KV_WIDTH:ATTN_WIDTH + 2 * KV_WIDTH].reshape(b, s, N_KV_HEADS, HEAD_DIM)
        u = z[..., ATTN_WIDTH + 2 * KV_WIDTH:]
        q = apply_rope(q, cos, sin)
        k = apply_rope(k, cos, sin)
        attn = sliding_window_attention_with_sinks(q, k, v, sinks[i])
        pool = multiscale_causal_pool(u, w_pool[i], b_pool[i], pool_scale[i])
        x = x + jnp.concatenate([attn, pool], axis=-1) @ w_out[i] + b_out[i]
        h = rmsnorm(x, g_ffn[i])
        x = x + (jax.nn.silu(h @ w_gate[i]) * (h @ w_up[i])) @ w_down[i]
    return rmsnorm(x, g_final)


import jax as _jax
import jax.numpy as _jnp

TWIN_FORMAT = 'train_step'
FWD_PARAMS = ['x', 'g_mix', 'w_in', 'b_in', 'sinks', 'w_pool', 'b_pool', 'pool_scale', 'w_out', 'b_out', 'g_ffn', 'w_gate', 'w_up', 'w_down', 'g_final']
TWIN_WEIGHTS = ['g_mix', 'w_in', 'b_in', 'sinks', 'w_pool', 'b_pool', 'pool_scale', 'w_out', 'b_out', 'g_ffn', 'w_gate', 'w_up', 'w_down', 'g_final']
TWIN_DIFF_INPUT = 'x'
TWIN_INPUTS = ['x', 'g_mix', 'w_in', 'b_in', 'sinks', 'w_pool', 'b_pool', 'pool_scale', 'w_out', 'b_out', 'g_ffn', 'w_gate', 'w_up', 'w_down', 'g_final', 'loss_target', 'm_g_mix', 'm_w_in', 'm_b_in', 'm_sinks', 'm_w_pool', 'm_b_pool', 'm_pool_scale', 'm_w_out', 'm_b_out', 'm_g_ffn', 'm_w_gate', 'm_w_up', 'm_w_down', 'm_g_final', 'v_g_mix', 'v_w_in', 'v_b_in', 'v_sinks', 'v_w_pool', 'v_b_pool', 'v_pool_scale', 'v_w_out', 'v_b_out', 'v_g_ffn', 'v_w_gate', 'v_w_up', 'v_w_down', 'v_g_final']
TWIN_OUTPUTS = ['loss', 'grad_x', 'grad_g_mix', 'grad_w_in', 'grad_b_in', 'grad_sinks', 'grad_w_pool', 'grad_b_pool', 'grad_pool_scale', 'grad_w_out', 'grad_b_out', 'grad_g_ffn', 'grad_w_gate', 'grad_w_up', 'grad_w_down', 'grad_g_final', 'delta_g_mix', 'delta_w_in', 'delta_b_in', 'delta_sinks', 'delta_w_pool', 'delta_b_pool', 'delta_pool_scale', 'delta_w_out', 'delta_b_out', 'delta_g_ffn', 'delta_w_gate', 'delta_w_up', 'delta_w_down', 'delta_g_final', 'new_m_g_mix', 'new_m_w_in', 'new_m_b_in', 'new_m_sinks', 'new_m_w_pool', 'new_m_b_pool', 'new_m_pool_scale', 'new_m_w_out', 'new_m_b_out', 'new_m_g_ffn', 'new_m_w_gate', 'new_m_w_up', 'new_m_w_down', 'new_m_g_final', 'new_v_g_mix', 'new_v_w_in', 'new_v_b_in', 'new_v_sinks', 'new_v_w_pool', 'new_v_b_pool', 'new_v_pool_scale', 'new_v_w_out', 'new_v_b_out', 'new_v_g_ffn', 'new_v_w_gate', 'new_v_w_up', 'new_v_w_down', 'new_v_g_final']
TWIN_LEAF_KINDS = {'loss': 'loss', 'grad_x': 'grad_x', 'grad_g_mix': 'grad_w', 'grad_w_in': 'grad_w', 'grad_b_in': 'grad_w', 'grad_sinks': 'grad_w', 'grad_w_pool': 'grad_w', 'grad_b_pool': 'grad_w', 'grad_pool_scale': 'grad_w', 'grad_w_out': 'grad_w', 'grad_b_out': 'grad_w', 'grad_g_ffn': 'grad_w', 'grad_w_gate': 'grad_w', 'grad_w_up': 'grad_w', 'grad_w_down': 'grad_w', 'grad_g_final': 'grad_w', 'delta_g_mix': 'delta_w', 'delta_w_in': 'delta_w', 'delta_b_in': 'delta_w', 'delta_sinks': 'delta_w', 'delta_w_pool': 'delta_w', 'delta_b_pool': 'delta_w', 'delta_pool_scale': 'delta_w', 'delta_w_out': 'delta_w', 'delta_b_out': 'delta_w', 'delta_g_ffn': 'delta_w', 'delta_w_gate': 'delta_w', 'delta_w_up': 'delta_w', 'delta_w_down': 'delta_w', 'delta_g_final': 'delta_w', 'new_m_g_mix': 'new_m', 'new_m_w_in': 'new_m', 'new_m_b_in': 'new_m', 'new_m_sinks': 'new_m', 'new_m_w_pool': 'new_m', 'new_m_b_pool': 'new_m', 'new_m_pool_scale': 'new_m', 'new_m_w_out': 'new_m', 'new_m_b_out': 'new_m', 'new_m_g_ffn': 'new_m', 'new_m_w_gate': 'new_m', 'new_m_w_up': 'new_m', 'new_m_w_down': 'new_m', 'new_m_g_final': 'new_m', 'new_v_g_mix': 'new_v', 'new_v_w_in': 'new_v', 'new_v_b_in': 'new_v', 'new_v_sinks': 'new_v', 'new_v_w_pool': 'new_v', 'new_v_b_pool': 'new_v', 'new_v_pool_scale': 'new_v', 'new_v_w_out': 'new_v', 'new_v_b_out': 'new_v', 'new_v_g_ffn': 'new_v', 'new_v_w_gate': 'new_v', 'new_v_w_up': 'new_v', 'new_v_w_down': 'new_v', 'new_v_g_final': 'new_v'}


def _forward(args):
    return _fwd_reference(*[args[k] for k in FWD_PARAMS])


def _output_shape():
    def fwd():
        inp = _fwd_setup_inputs(0)
        return _fwd_reference(*[inp[k] for k in FWD_PARAMS])
    out = _jax.eval_shape(fwd)
    return out.shape, out.dtype

N_MICROBATCH = 1
ADAM_LR = 0.001
ADAM_B1 = 0.9
ADAM_B2 = 0.999
ADAM_EPS = 1e-08
ADAM_WD = 0.01
ADAM_STEP = 10
PER_EXAMPLE_BATCH_AXIS = {'x': 0, 'loss_target': 0}
SHARED_INPUTS = []
_WEIGHT_DTYPES = {'g_mix': _jnp.float32, 'w_in': _jnp.float32, 'b_in': _jnp.float32, 'sinks': _jnp.float32, 'w_pool': _jnp.float32, 'b_pool': _jnp.float32, 'pool_scale': _jnp.float32, 'w_out': _jnp.float32, 'b_out': _jnp.float32, 'g_ffn': _jnp.float32, 'w_gate': _jnp.float32, 'w_up': _jnp.float32, 'w_down': _jnp.float32, 'g_final': _jnp.float32}
MOMENT_SCALE = {'g_mix': 1.560741e-01, 'w_in': 1.371114e-01, 'b_in': 2.278225e-01, 'sinks': 3.625088e-02, 'w_pool': 2.089111e-01, 'b_pool': 2.385726e-01, 'pool_scale': 2.141018e-01, 'w_out': 1.480890e-01, 'b_out': 2.652726e-01, 'g_ffn': 1.695457e-01, 'w_gate': 7.042329e-02, 'w_up': 6.862537e-02, 'w_down': 1.134030e-01, 'g_final': 6.403964e+01}


def _to_microbatches(a, axis):
    t = _jnp.moveaxis(a, axis, 0)
    t = t.reshape((N_MICROBATCH, t.shape[0] // N_MICROBATCH) + t.shape[1:])
    return _jnp.moveaxis(t, 1, axis + 1)


def setup_inputs(seed: int = 0) -> dict:
    inp = _fwd_setup_inputs(seed)
    key = _jax.random.fold_in(_jax.random.key(seed), 7919)
    shape, _ = _output_shape()
    out = dict(inp)
    out["loss_target"] = _jax.random.normal(_jax.random.fold_in(key, 0), shape, _jnp.float32)
    for i, name in enumerate(TWIN_WEIGHTS):
        w = inp[name].astype(_jnp.float32)
        if MOMENT_SCALE is None:
            s = _jnp.sqrt(_jnp.mean(_jnp.square(w)) + 1e-30)
        else:
            s = MOMENT_SCALE[name]
        km, kv = _jax.random.split(_jax.random.fold_in(key, i + 1))
        out[name] = w
        out["m_" + name] = s * _jax.random.normal(km, w.shape, _jnp.float32)
        out["v_" + name] = (s * s) * _jax.random.uniform(kv, w.shape, _jnp.float32, 0.5, 1.5)
    if N_MICROBATCH > 1:
        for name, axis in PER_EXAMPLE_BATCH_AXIS.items():
            out[name] = _to_microbatches(out[name], axis)
    return {'x': out['x'], 'g_mix': out['g_mix'], 'w_in': out['w_in'], 'b_in': out['b_in'], 'sinks': out['sinks'], 'w_pool': out['w_pool'], 'b_pool': out['b_pool'], 'pool_scale': out['pool_scale'], 'w_out': out['w_out'], 'b_out': out['b_out'], 'g_ffn': out['g_ffn'], 'w_gate': out['w_gate'], 'w_up': out['w_up'], 'w_down': out['w_down'], 'g_final': out['g_final'], 'loss_target': out['loss_target'], 'm_g_mix': out['m_g_mix'], 'm_w_in': out['m_w_in'], 'm_b_in': out['m_b_in'], 'm_sinks': out['m_sinks'], 'm_w_pool': out['m_w_pool'], 'm_b_pool': out['m_b_pool'], 'm_pool_scale': out['m_pool_scale'], 'm_w_out': out['m_w_out'], 'm_b_out': out['m_b_out'], 'm_g_ffn': out['m_g_ffn'], 'm_w_gate': out['m_w_gate'], 'm_w_up': out['m_w_up'], 'm_w_down': out['m_w_down'], 'm_g_final': out['m_g_final'], 'v_g_mix': out['v_g_mix'], 'v_w_in': out['v_w_in'], 'v_b_in': out['v_b_in'], 'v_sinks': out['v_sinks'], 'v_w_pool': out['v_w_pool'], 'v_b_pool': out['v_b_pool'], 'v_pool_scale': out['v_pool_scale'], 'v_w_out': out['v_w_out'], 'v_b_out': out['v_b_out'], 'v_g_ffn': out['v_g_ffn'], 'v_w_gate': out['v_w_gate'], 'v_w_up': out['v_w_up'], 'v_w_down': out['v_w_down'], 'v_g_final': out['v_g_final']}


def _loss(weights, diff, rest, loss_target):
    with _jax.named_scope("forward"):
        args = {**rest, TWIN_DIFF_INPUT: diff, **{k: w.astype(_WEIGHT_DTYPES[k]) for k, w in weights.items()}}
        y = _forward(args)
    with _jax.named_scope("loss_head"):
        err = _jnp.square(y.astype(_jnp.float32) - loss_target)
        return 0.5 * _jnp.sum(_jnp.mean(err, axis=-1)) if err.ndim else 0.5 * err


def _adamw(w, g, m, v):
    m = ADAM_B1 * m + (1.0 - ADAM_B1) * g
    v = ADAM_B2 * v + (1.0 - ADAM_B2) * _jnp.square(g)
    m_hat = m / (1.0 - ADAM_B1 ** ADAM_STEP)
    v_hat = v / (1.0 - ADAM_B2 ** ADAM_STEP)
    delta = -ADAM_LR * (m_hat / (_jnp.sqrt(v_hat) + ADAM_EPS) + ADAM_WD * w)
    return delta, m, v


def reference(x, g_mix, w_in, b_in, sinks, w_pool, b_pool, pool_scale, w_out, b_out, g_ffn, w_gate, w_up, w_down, g_final, loss_target, m_g_mix, m_w_in, m_b_in, m_sinks, m_w_pool, m_b_pool, m_pool_scale, m_w_out, m_b_out, m_g_ffn, m_w_gate, m_w_up, m_w_down, m_g_final, v_g_mix, v_w_in, v_b_in, v_sinks, v_w_pool, v_b_pool, v_pool_scale, v_w_out, v_b_out, v_g_ffn, v_w_gate, v_w_up, v_w_down, v_g_final):
    given = dict(x=x, g_mix=g_mix, w_in=w_in, b_in=b_in, sinks=sinks, w_pool=w_pool, b_pool=b_pool, pool_scale=pool_scale, w_out=w_out, b_out=b_out, g_ffn=g_ffn, w_gate=w_gate, w_up=w_up, w_down=w_down, g_final=g_final, loss_target=loss_target, m_g_mix=m_g_mix, m_w_in=m_w_in, m_b_in=m_b_in, m_sinks=m_sinks, m_w_pool=m_w_pool, m_b_pool=m_b_pool, m_pool_scale=m_pool_scale, m_w_out=m_w_out, m_b_out=m_b_out, m_g_ffn=m_g_ffn, m_w_gate=m_w_gate, m_w_up=m_w_up, m_w_down=m_w_down, m_g_final=m_g_final, v_g_mix=v_g_mix, v_w_in=v_w_in, v_b_in=v_b_in, v_sinks=v_sinks, v_w_pool=v_w_pool, v_b_pool=v_b_pool, v_pool_scale=v_pool_scale, v_w_out=v_w_out, v_b_out=v_b_out, v_g_ffn=v_g_ffn, v_w_gate=v_w_gate, v_w_up=v_w_up, v_w_down=v_w_down, v_g_final=v_g_final)
    weights = {n: given[n] for n in TWIN_WEIGHTS}
    shared = {n: given[n] for n in SHARED_INPUTS}
    per_example = {n: given[n] for n in ['x']}
    grad_fn = _jax.value_and_grad(_loss, argnums=(0, 1))

    def one_microbatch(ex, loss_target):
        ex = dict(ex)
        diff = ex.pop(TWIN_DIFF_INPUT)
        return grad_fn(weights, diff, {**shared, **ex}, loss_target)

    if N_MICROBATCH == 1:
        loss, (grad_w, grad_x) = one_microbatch(per_example, given["loss_target"])
    else:
        def body(carry, xs):
            loss_sum, grad_sum = carry
            l_k, (gw_k, gx_k) = one_microbatch(xs[0], xs[1])
            with _jax.named_scope("update"):
                return (loss_sum + l_k, _jax.tree.map(_jnp.add, grad_sum, gw_k)), gx_k

        init = (_jnp.zeros((), _jnp.float32), _jax.tree.map(_jnp.zeros_like, weights))
        (loss, grad_w), grad_x = _jax.lax.scan(body, init, (per_example, given["loss_target"]))
    with _jax.named_scope("update"):
        delta_w, new_m, new_v = {}, {}, {}
        for n in TWIN_WEIGHTS:
            delta_w[n], new_m[n], new_v[n] = _adamw(weights[n], grad_w[n], given["m_" + n], given["v_" + n])
    return (loss, grad_x, *[grad_w[n] for n in TWIN_WEIGHTS], *[delta_w[n] for n in TWIN_WEIGHTS],
            *[new_m[n] for n in TWIN_WEIGHTS], *[new_v[n] for n in TWIN_WEIGHTS])
```

```python
import functools

import jax
import jax.numpy as jnp
from jax import lax
from jax.experimental import pallas as pl
from jax.experimental.pallas import tpu as pltpu

F32 = jnp.float32
BF16 = jnp.bfloat16

D_MODEL = 1024
ATTN_WIDTH = 512
KV_WIDTH = 128
POOL_WIDTH = 512
IN_WIDTH = ATTN_WIDTH + 2 * KV_WIDTH + POOL_WIDTH
HEAD_DIM = 64
N_Q_HEADS = 8
BLOCK = 128
POOL_SIZES = (2, 4, 8, 16)
POOL_HALO = 16
ROPE_THETA = 10000.0
RMS_EPS = 1e-5
Q_SCALE = HEAD_DIM ** -0.5

ADAM_LR = 0.001
ADAM_B1 = 0.9
ADAM_B2 = 0.999
ADAM_EPS = 1e-08
ADAM_WD = 0.01
ADAM_STEP = 10

N_SHARDS = 4
LANES = 128
VMEM_LIMIT = 56 * 1024 * 1024

MESH = pl.DeviceIdType.MESH
HBM_SPEC = pl.BlockSpec(memory_space=pltpu.HBM)

SMALL_NAMES = ("g_mix", "b_in", "sinks", "w_pool", "b_pool", "pool_scale", "b_out", "g_ffn", "g_final")


def _dot(a, b):
    return jnp.dot(a, b, preferred_element_type=F32)


def _dot_nt(a, b):
    return lax.dot_general(a, b, (((1,), (1,)), ((), ())), preferred_element_type=F32)


def _dot_tn(a, b):
    return lax.dot_general(a, b, (((0,), (0,)), ((), ())), preferred_element_type=F32)


def _params(*semantics):
    return pltpu.CompilerParams(dimension_semantics=semantics, vmem_limit_bytes=VMEM_LIMIT)


def _resident(shape):
    return pl.BlockSpec(shape, lambda *_: (0,) * len(shape), pipeline_mode=pl.Buffered(1))


def _token_tile(t):
    for tm in (512, 256, 128):
        if t % tm == 0:
            return tm
    raise ValueError(f"sequence length {t} is not a multiple of 128")


def _rms_stats(xf):
    r = lax.rsqrt(jnp.mean(xf * xf, axis=-1, keepdims=True) + RMS_EPS)
    return xf * r, r


def _rms_bwd(dy, xh, r, g):
    dxh = dy * g
    return r * (dxh - xh * jnp.mean(dxh * xh, axis=-1, keepdims=True))


def _rope_partner(t, first):
    return jnp.where(first, pltpu.roll(t, LANES - HEAD_DIM // 2, 1), pltpu.roll(t, HEAD_DIM // 2, 1))


def _first_half_mask(shape):
    lane = lax.broadcasted_iota(jnp.int32, shape, 1)
    return (lane % HEAD_DIM) < (HEAD_DIM // 2)


def _rope_tables(t):
    inv_freq = 1.0 / (ROPE_THETA ** (jnp.arange(0, HEAD_DIM, 2, dtype=F32) / HEAD_DIM))
    ang = jnp.arange(t, dtype=F32)[:, None] * inv_freq[None, :]
    cos, sin = jnp.cos(ang), jnp.sin(ang)
    cos_t = jnp.tile(jnp.concatenate([cos, cos], axis=-1), (1, LANES // HEAD_DIM))
    sin_t = jnp.tile(jnp.concatenate([-sin, sin], axis=-1), (1, LANES // HEAD_DIM))
    return cos_t, sin_t


def _inproj_call(x, g_mix, w_in, b_in, cos_t, sin_t):
    t = x.shape[0]
    tm = _token_tile(t)

    def body(x_ref, g_ref, w_ref, b_ref, c_ref, s_ref, h_ref, q_ref, k_ref, v_ref, u_ref):
        xh, _ = _rms_stats(x_ref[...])
        h = (xh * g_ref[...]).astype(BF16)
        h_ref[...] = h
        z = _dot(h, w_ref[...]) + b_ref[...]
        cos, sin = c_ref[...], s_ref[...]
        first = _first_half_mask((tm, LANES))
        for j in range(ATTN_WIDTH // LANES):
            zj = z[:, LANES * j:LANES * (j + 1)]
            q_ref[:, LANES * j:LANES * (j + 1)] = ((zj * cos + _rope_partner(zj, first) * sin) * Q_SCALE).astype(BF16)
        zk = z[:, ATTN_WIDTH:ATTN_WIDTH + KV_WIDTH]
        k_ref[...] = (zk * cos + _rope_partner(zk, first) * sin).astype(BF16)
        v_ref[...] = z[:, ATTN_WIDTH + KV_WIDTH:ATTN_WIDTH + 2 * KV_WIDTH].astype(BF16)
        u_ref[...] = z[:, ATTN_WIDTH + 2 * KV_WIDTH:]

    row = lambda w: pl.BlockSpec((tm, w), lambda i: (i, 0))
    return pl.pallas_call(
        body, grid=(t // tm,),
        in_specs=[row(D_MODEL), _resident((1, D_MODEL)), _resident((D_MODEL, IN_WIDTH)), _resident((1, IN_WIDTH)),
                  row(LANES), row(LANES)],
        out_specs=[row(D_MODEL), row(ATTN_WIDTH), row(KV_WIDTH), row(KV_WIDTH), row(POOL_WIDTH)],
        out_shape=[jax.ShapeDtypeStruct((t, D_MODEL), BF16), jax.ShapeDtypeStruct((t, ATTN_WIDTH), BF16),
                   jax.ShapeDtypeStruct((t, KV_WIDTH), BF16), jax.ShapeDtypeStruct((t, KV_WIDTH), BF16),
                   jax.ShapeDtypeStruct((t, POOL_WIDTH), F32)],
        compiler_params=_params("parallel"), name="inproj_fwd",
    )(x, g_mix, w_in, b_in, cos_t, sin_t)


def _head_in_both_halves(band, kv_head):
    lane = lax.broadcasted_iota(jnp.int32, band.shape, 1)
    own = (lane < HEAD_DIM) if kv_head == 0 else (lane >= HEAD_DIM)
    return jnp.where(own, band, pltpu.roll(band, HEAD_DIM, 1)).astype(BF16)


def _stack_heads(ref, kv_head):
    low = lax.broadcasted_iota(jnp.int32, (BLOCK, LANES), 1) < HEAD_DIM
    parts = []
    for jj in range(2):
        slab = ref[:, LANES * (2 * kv_head + jj):LANES * (2 * kv_head + jj + 1)]
        zero = jnp.zeros_like(slab)
        parts += [jnp.where(low, slab, zero), jnp.where(low, zero, slab)]
    return jnp.concatenate(parts, axis=0)


def _unstack_heads(o):
    low = lax.broadcasted_iota(jnp.int32, (BLOCK, LANES), 1) < HEAD_DIM
    return [jnp.where(low, o[2 * BLOCK * jj:2 * BLOCK * jj + BLOCK], o[2 * BLOCK * jj + BLOCK:2 * BLOCK * (jj + 1)])
            for jj in range(2)]


def _band_visible(n):
    rows = 4 * BLOCK
    i = lax.broadcasted_iota(jnp.int32, (rows, 2 * BLOCK), 0) % BLOCK
    c = lax.broadcasted_iota(jnp.int32, (rows, 2 * BLOCK), 1)
    return (c > i) & (c <= i + BLOCK) & ((c >= BLOCK) | (n > 0))


def _sink_column(sink_ref, kv_head):
    return jnp.concatenate([jnp.full((BLOCK, 1), sink_ref[0, 4 * kv_head + g], F32) for g in range(4)], axis=0)


def _softmax_with_sink(s, sink):
    m = jnp.maximum(jnp.max(s, axis=-1, keepdims=True), sink)
    p = jnp.exp(s - m)
    e_sink = jnp.exp(sink - m)
    inv = 1.0 / (jnp.sum(p, axis=-1, keepdims=True) + e_sink)
    return p * inv, e_sink * inv


def _attn_fwd_call(q, k, v, sinks):
    t = q.shape[0]
    nb = t // BLOCK

    def body(sink_ref, q_ref, kc_ref, kp_ref, vc_ref, vp_ref, o_ref):
        n = pl.program_id(0)
        kband = jnp.concatenate([kp_ref[...], kc_ref[...]], axis=0).astype(F32)
        vband = jnp.concatenate([vp_ref[...], vc_ref[...]], axis=0).astype(F32)
        vis = _band_visible(n)
        for kh in range(2):
            k2 = _head_in_both_halves(kband, kh)
            v2 = _head_in_both_halves(vband, kh)
            s = jnp.where(vis, _dot_nt(_stack_heads(q_ref, kh), k2), -jnp.inf)
            p, _ = _softmax_with_sink(s, _sink_column(sink_ref, kh))
            o = _dot(p.astype(BF16), v2)
            for jj, slab in enumerate(_unstack_heads(o)):
                o_ref[:, LANES * (2 * kh + jj):LANES * (2 * kh + jj + 1)] = slab.astype(BF16)

    cur = lambda w: pl.BlockSpec((BLOCK, w), lambda n: (n, 0))
    prev = lambda w: pl.BlockSpec((BLOCK, w), lambda n: (jnp.maximum(n - 1, 0), 0))
    return pl.pallas_call(
        body, grid=(nb,),
        in_specs=[pl.BlockSpec(memory_space=pltpu.SMEM), cur(ATTN_WIDTH), cur(KV_WIDTH), prev(KV_WIDTH),
                  cur(KV_WIDTH), prev(KV_WIDTH)],
        out_specs=cur(ATTN_WIDTH),
        out_shape=jax.ShapeDtypeStruct((t, ATTN_WIDTH), BF16),
        compiler_params=_params("parallel"), name="attn_fwd",
    )(sinks, q, k, k, v, v)


def _pool_mixed(sc_ref, u_ref, halo_ref, i, tm):
    sc_ref[0:POOL_HALO, :] = jnp.where(i == 0, 0.0, halo_ref[...])
    sc_ref[POOL_HALO:, :] = u_ref[...]
    tok = i * tm + lax.broadcasted_iota(jnp.int32, (tm, 1), 0)
    mixed = []
    for g, size in enumerate(POOL_SIZES):
        lanes = slice(LANES * g, LANES * (g + 1))
        cur = sc_ref[pl.ds(POOL_HALO, tm), lanes]
        acc = cur
        for j in range(1, size):
            acc = acc + sc_ref[pl.ds(POOL_HALO - j, tm), lanes]
        count = jnp.minimum(tok + 1, size).astype(F32)
        mixed.append(acc / count - cur)
    return mixed


def _halo_before(tm, width):
    return pl.BlockSpec((POOL_HALO, width), lambda i: (jnp.maximum(i * (tm // POOL_HALO) - 1, 0), 0))


def _outproj_call(attn, u, x, w_pool, b_pool, pool_scale, w_out, b_out, g_ffn):
    t = x.shape[0]
    tm = _token_tile(t)

    def body(a_ref, u_ref, halo_ref, x_ref, wp_ref, bp_ref, ps_ref, wo_ref, bo_ref, g_ref,
             x1_ref, h1_ref, y_ref, sc_ref):
        i = pl.program_id(0)
        for g, mixed in enumerate(_pool_mixed(sc_ref, u_ref, halo_ref, i, tm)):
            lanes = slice(LANES * g, LANES * (g + 1))
            lin = _dot(mixed.astype(BF16), wp_ref[g]) + bp_ref[:, lanes]
            y_ref[:, lanes] = (lin * ps_ref[:, lanes]).astype(BF16)
        x1 = (x_ref[...] + _dot(a_ref[...], wo_ref[0:ATTN_WIDTH, :]) + _dot(y_ref[...], wo_ref[ATTN_WIDTH:, :])
              + bo_ref[...])
        x1_ref[...] = x1
        xh, _ = _rms_stats(x1)
        h1_ref[...] = (xh * g_ref[...]).astype(BF16)

    row = lambda w: pl.BlockSpec((tm, w), lambda i: (i, 0))
    return pl.pallas_call(
        body, grid=(t // tm,),
        in_specs=[row(ATTN_WIDTH), row(POOL_WIDTH), _halo_before(tm, POOL_WIDTH), row(D_MODEL),
                  _resident((4, LANES, LANES)), _resident((1, POOL_WIDTH)), _resident((1, POOL_WIDTH)),
                  _resident((D_MODEL, D_MODEL)), _resident((1, D_MODEL)), _resident((1, D_MODEL))],
        out_specs=[row(D_MODEL), row(D_MODEL), row(POOL_WIDTH)],
        out_shape=[jax.ShapeDtypeStruct((t, D_MODEL), F32), jax.ShapeDtypeStruct((t, D_MODEL), BF16),
                   jax.ShapeDtypeStruct((t, POOL_WIDTH), BF16)],
        scratch_shapes=[pltpu.VMEM((tm + POOL_HALO, POOL_WIDTH), F32)],
        compiler_params=_params("parallel"), name="outproj_fwd",
    )(attn, u, u, x, w_pool, b_pool, pool_scale, w_out, b_out, g_ffn)


def _ffn_fwd_call(h1, x1, target, w_gate, w_up, w_down, g_final):
    t = x1.shape[0]
    tm = _token_tile(t)
    ns, _, fs = w_gate.shape

    def body(h_ref, x1_ref, tgt_ref, wg_ref, wu_ref, wd_ref, gf_ref,
             gate_ref, up_ref, act_ref, dx2_ref, loss_ref, dgf_ref, acc_ref):
        i, s = pl.program_id(0), pl.program_id(1)

        @pl.when(s == 0)
        def _():
            acc_ref[...] = x1_ref[...]

        @pl.when((i == 0) & (s == 0))
        def _():
            loss_ref[...] = jnp.zeros_like(loss_ref)
            dgf_ref[...] = jnp.zeros_like(dgf_ref)

        h = h_ref[...]
        gate = _dot(h, wg_ref[...])
        up = _dot(h, wu_ref[...])
        act = (gate * jax.nn.sigmoid(gate) * up).astype(BF16)
        gate_ref[...] = gate.astype(BF16)
        up_ref[...] = up.astype(BF16)
        act_ref[...] = act
        acc_ref[...] += _dot(act, wd_ref[...])

        @pl.when(s == ns - 1)
        def _():
            xh, r = _rms_stats(acc_ref[...])
            gf = gf_ref[...]
            err = xh * gf - tgt_ref[...]
            loss_ref[...] += 0.5 * jnp.sum(err * err) / D_MODEL
            dy = err / D_MODEL
            dgf_ref[...] += jnp.sum(dy * xh, axis=0, keepdims=True)
            dx2_ref[...] = _rms_bwd(dy, xh, r, gf)

    row = lambda w: pl.BlockSpec((tm, w), lambda i, s: (i, 0))
    shard_act = pl.BlockSpec((None, tm, fs), lambda i, s: (s, i, 0))
    act_shape = jax.ShapeDtypeStruct((ns, t, fs), BF16)
    return pl.pallas_call(
        body, grid=(t // tm, ns),
        in_specs=[row(D_MODEL), row(D_MODEL), row(D_MODEL),
                  pl.BlockSpec((None, D_MODEL, fs), lambda i, s: (s, 0, 0)),
                  pl.BlockSpec((None, D_MODEL, fs), lambda i, s: (s, 0, 0)),
                  pl.BlockSpec((None, fs, D_MODEL), lambda i, s: (s, 0, 0)),
                  _resident((1, D_MODEL))],
        out_specs=[shard_act, shard_act, shard_act, row(D_MODEL),
                   pl.BlockSpec((1, LANES), lambda i, s: (0, 0)), pl.BlockSpec((1, D_MODEL), lambda i, s: (0, 0))],
        out_shape=[act_shape, act_shape, act_shape, jax.ShapeDtypeStruct((t, D_MODEL), F32),
                   jax.ShapeDtypeStruct((1, LANES), F32), jax.ShapeDtypeStruct((1, D_MODEL), F32)],
        scratch_shapes=[pltpu.VMEM((tm, D_MODEL), F32)],
        compiler_params=_params("arbitrary", "arbitrary"), name="ffn_fwd",
    )(h1, x1, target, w_gate, w_up, w_down, g_final)


def _ffn_bwd_act_call(dx2, gate, up, x1, w_gate, w_up, w_down, g_ffn):
    t = x1.shape[0]
    tm = _token_tile(t)
    ns, _, fs = w_gate.shape

    def body(dx2_ref, gate_ref, up_ref, x1_ref, wg_ref, wu_ref, wd_ref, g_ref,
             dgate_ref, dup_ref, dx1_ref, dg_ref, acc_ref, dx2b_ref):
        i, s = pl.program_id(0), pl.program_id(1)

        @pl.when(s == 0)
        def _():
            acc_ref[...] = jnp.zeros_like(acc_ref)
            dx2b_ref[...] = dx2_ref[...].astype(BF16)

        @pl.when((i == 0) & (s == 0))
        def _():
            dg_ref[...] = jnp.zeros_like(dg_ref)

        dact = _dot_nt(dx2b_ref[...], wd_ref[...])
        gate = gate_ref[...].astype(F32)
        upv = up_ref[...].astype(F32)
        sg = jax.nn.sigmoid(gate)
        dup = (dact * (gate * sg)).astype(BF16)
        dgate = (dact * upv * (sg * (1.0 + gate * (1.0 - sg)))).astype(BF16)
        dgate_ref[...] = dgate
        dup_ref[...] = dup
        acc_ref[...] += _dot_nt(dgate, wg_ref[...]) + _dot_nt(dup, wu_ref[...])

        @pl.when(s == ns - 1)
        def _():
            xh, r = _rms_stats(x1_ref[...])
            dh = acc_ref[...]
            dg_ref[...] += jnp.sum(dh * xh, axis=0, keepdims=True)
            dx1_ref[...] = dx2_ref[...] + _rms_bwd(dh, xh, r, g_ref[...])

    row = lambda w: pl.BlockSpec((tm, w), lambda i, s: (i, 0))
    shard_act = pl.BlockSpec((None, tm, fs), lambda i, s: (s, i, 0))
    act_shape = jax.ShapeDtypeStruct((ns, t, fs), BF16)
    return pl.pallas_call(
        body, grid=(t // tm, ns),
        in_specs=[row(D_MODEL), shard_act, shard_act, row(D_MODEL),
                  pl.BlockSpec((None, D_MODEL, fs), lambda i, s: (s, 0, 0)),
                  pl.BlockSpec((None, D_MODEL, fs), lambda i, s: (s, 0, 0)),
                  pl.BlockSpec((None, fs, D_MODEL), lambda i, s: (s, 0, 0)),
                  _resident((1, D_MODEL))],
        out_specs=[shard_act, shard_act, row(D_MODEL), pl.BlockSpec((1, D_MODEL), lambda i, s: (0, 0))],
        out_shape=[act_shape, act_shape, jax.ShapeDtypeStruct((t, D_MODEL), F32),
                   jax.ShapeDtypeStruct((1, D_MODEL), F32)],
        scratch_shapes=[pltpu.VMEM((tm, D_MODEL), F32), pltpu.VMEM((tm, D_MODEL), BF16)],
        compiler_params=_params("arbitrary", "arbitrary"), name="ffn_bwd_act",
    )(dx2, gate, up, x1, w_gate, w_up, w_down, g_ffn)


def _ffn_bwd_weights_call(h1, dgate, dup, act, dx2):
    t = h1.shape[0]
    tm = _token_tile(t)
    ns, _, fs = dgate.shape

    def body(h_ref, dgate_ref, dup_ref, act_ref, dx2_ref, gg_ref, gu_ref, gd_ref):
        @pl.when(pl.program_id(1) == 0)
        def _():
            gg_ref[...] = jnp.zeros_like(gg_ref)
            gu_ref[...] = jnp.zeros_like(gu_ref)
            gd_ref[...] = jnp.zeros_like(gd_ref)

        h = h_ref[...]
        gg_ref[...] += _dot_tn(h, dgate_ref[...])
        gu_ref[...] += _dot_tn(h, dup_ref[...])
        gd_ref[...] += _dot_tn(act_ref[...], dx2_ref[...].astype(BF16))

    row = lambda w: pl.BlockSpec((tm, w), lambda s, i: (i, 0))
    shard_act = pl.BlockSpec((None, tm, fs), lambda s, i: (s, i, 0))
    return pl.pallas_call(
        body, grid=(ns, t // tm),
        in_specs=[row(D_MODEL), shard_act, shard_act, shard_act, row(D_MODEL)],
        out_specs=[pl.BlockSpec((None, D_MODEL, fs), lambda s, i: (s, 0, 0)),
                   pl.BlockSpec((None, D_MODEL, fs), lambda s, i: (s, 0, 0)),
                   pl.BlockSpec((None, fs, D_MODEL), lambda s, i: (s, 0, 0))],
        out_shape=[jax.ShapeDtypeStruct((ns, D_MODEL, fs), F32), jax.ShapeDtypeStruct((ns, D_MODEL, fs), F32),
                   jax.ShapeDtypeStruct((ns, fs, D_MODEL), F32)],
        compiler_params=_params("parallel", "arbitrary"), name="ffn_bwd_weights",
    )(h1, dgate, dup, act, dx2)


def _outproj_bwd_call(dx1, attn, pool_y, u, w_pool, b_pool, pool_scale, w_out):
    t = dx1.shape[0]
    tm = _token_tile(t)

    def body(dx1_ref, a_ref, y_ref, u_ref, halo_ref, wp_ref, bp_ref, ps_ref, wo_ref,
             dattn_ref, dmixed_ref, gwo_ref, gbo_ref, gwp_ref, gbp_ref, gps_ref, sc_ref):
        i = pl.program_id(0)

        @pl.when(i == 0)
        def _():
            for ref in (gwo_ref, gbo_ref, gwp_ref, gbp_ref, gps_ref):
                ref[...] = jnp.zeros_like(ref)

        dx1 = dx1_ref[...]
        dx1b = dx1.astype(BF16)
        dcat = _dot_nt(dx1b, wo_ref[...])
        dattn_ref[...] = dcat[:, :ATTN_WIDTH].astype(BF16)
        for g, mixed in enumerate(_pool_mixed(sc_ref, u_ref, halo_ref, i, tm)):
            lanes = slice(LANES * g, LANES * (g + 1))
            mixed_b = mixed.astype(BF16)
            lin = _dot(mixed_b, wp_ref[g]) + bp_ref[:, lanes]
            dpool = dcat[:, ATTN_WIDTH + LANES * g:ATTN_WIDTH + LANES * (g + 1)]
            gps_ref[:, lanes] += jnp.sum(dpool * lin, axis=0, keepdims=True)
            dy = dpool * ps_ref[:, lanes]
            gbp_ref[:, lanes] += jnp.sum(dy, axis=0, keepdims=True)
            dy_b = dy.astype(BF16)
            gwp_ref[g] += _dot_tn(mixed_b, dy_b)
            dmixed_ref[:, lanes] = _dot_nt(dy_b, wp_ref[g])
        gwo_ref[0:ATTN_WIDTH, :] += _dot_tn(a_ref[...], dx1b)
        gwo_ref[ATTN_WIDTH:, :] += _dot_tn(y_ref[...], dx1b)
        gbo_ref[...] += jnp.sum(dx1, axis=0, keepdims=True)

    row = lambda w: pl.BlockSpec((tm, w), lambda i: (i, 0))
    const = lambda shape: pl.BlockSpec(shape, lambda i: (0,) * len(shape))
    return pl.pallas_call(
        body, grid=(t // tm,),
        in_specs=[row(D_MODEL), row(ATTN_WIDTH), row(POOL_WIDTH), row(POOL_WIDTH), _halo_before(tm, POOL_WIDTH),
                  _resident((4, LANES, LANES)), _resident((1, POOL_WIDTH)), _resident((1, POOL_WIDTH)),
                  _resident((D_MODEL, D_MODEL))],
        out_specs=[row(ATTN_WIDTH), row(POOL_WIDTH), const((D_MODEL, D_MODEL)), const((1, D_MODEL)),
                   const((4, LANES, LANES)), const((1, POOL_WIDTH)), const((1, POOL_WIDTH))],
        out_shape=[jax.ShapeDtypeStruct((t, ATTN_WIDTH), BF16), jax.ShapeDtypeStruct((t, POOL_WIDTH), F32),
                   jax.ShapeDtypeStruct((D_MODEL, D_MODEL), F32), jax.ShapeDtypeStruct((1, D_MODEL), F32),
                   jax.ShapeDtypeStruct((4, LANES, LANES), F32), jax.ShapeDtypeStruct((1, POOL_WIDTH), F32),
                   jax.ShapeDtypeStruct((1, POOL_WIDTH), F32)],
        scratch_shapes=[pltpu.VMEM((tm + POOL_HALO, POOL_WIDTH), F32)],
        compiler_params=_params("arbitrary"), name="outproj_bwd",
    )(dx1, attn, pool_y, u, u, w_pool, b_pool, pool_scale, w_out)


def _attn_bwd_call(q, k, v, dout, sinks, cos_t, sin_t):
    t = q.shape[0]
    nb = t // BLOCK

    def body(sink_ref, q_ref, kc_ref, kp_ref, vc_ref, vp_ref, do_ref, cq_ref, sq_ref, ck_ref, sk_ref,
             dq_ref, dk_ref, dv_ref, dsink_ref, bandk_ref, bandv_ref, carryk_ref, carryv_ref):
        n = pl.program_id(0)
        first = _first_half_mask((BLOCK, LANES))

        @pl.when(n == 0)
        def _():
            dsink_ref[...] = jnp.zeros_like(dsink_ref)

        @pl.when(n < nb)
        def _():
            kband = jnp.concatenate([kp_ref[...], kc_ref[...]], axis=0).astype(F32)
            vband = jnp.concatenate([vp_ref[...], vc_ref[...]], axis=0).astype(F32)
            vis = _band_visible(n)
            head_row = lax.broadcasted_iota(jnp.int32, dsink_ref.shape, 0)
            lane2 = lax.broadcasted_iota(jnp.int32, (2 * BLOCK, LANES), 1)
            dk_heads, dv_heads = [], []
            dsink = jnp.zeros(dsink_ref.shape, F32)
            for kh in range(2):
                k2 = _head_in_both_halves(kband, kh)
                v2 = _head_in_both_halves(vband, kh)
                qst = _stack_heads(q_ref, kh)
                dost = _stack_heads(do_ref, kh)
                s = jnp.where(vis, _dot_nt(qst, k2), -jnp.inf)
                p, p_sink = _softmax_with_sink(s, _sink_column(sink_ref, kh))
                dp = _dot_nt(dost, v2)
                delta = jnp.sum(p * dp, axis=-1, keepdims=True)
                ds = (p * (dp - delta)).astype(BF16)
                sink_term = p_sink * delta
                for g in range(4):
                    val = -jnp.sum(sink_term[BLOCK * g:BLOCK * (g + 1)])
                    dsink = dsink + jnp.where(head_row == 4 * kh + g, val, 0.0)
                dq2 = _dot(ds, k2)
                cq, sq = cq_ref[...], sq_ref[...]
                for jj, slab in enumerate(_unstack_heads(dq2)):
                    w = slab * sq
                    dq_ref[:, LANES * (2 * kh + jj):LANES * (2 * kh + jj + 1)] = (
                        (slab * cq + _rope_partner(w, first)) * Q_SCALE).astype(BF16)
                dk2 = _dot_tn(ds, qst)
                dv2 = _dot_tn(p.astype(BF16), dost)
                dk_heads.append(dk2 + pltpu.roll(dk2, HEAD_DIM, 1))
                dv_heads.append(dv2 + pltpu.roll(dv2, HEAD_DIM, 1))
            bandk_ref[...] = jnp.where(lane2 < HEAD_DIM, dk_heads[0], dk_heads[1])
            bandv_ref[...] = jnp.where(lane2 < HEAD_DIM, dv_heads[0], dv_heads[1])
            dsink_ref[...] += dsink

        @pl.when(n == nb)
        def _():
            bandk_ref[...] = jnp.zeros_like(bandk_ref)
            bandv_ref[...] = jnp.zeros_like(bandv_ref)

        @pl.when(n >= 1)
        def _():
            dk = carryk_ref[...] + bandk_ref[0:BLOCK, :]
            w = dk * sk_ref[...]
            dk_ref[...] = (dk * ck_ref[...] + _rope_partner(w, first)).astype(BF16)
            dv_ref[...] = (carryv_ref[...] + bandv_ref[0:BLOCK, :]).astype(BF16)

        carryk_ref[...] = bandk_ref[BLOCK:, :]
        carryv_ref[...] = bandv_ref[BLOCK:, :]

    cur = lambda w: pl.BlockSpec((BLOCK, w), lambda n: (jnp.minimum(n, nb - 1), 0))
    prev = lambda w: pl.BlockSpec((BLOCK, w), lambda n: (jnp.maximum(n - 1, 0), 0))
    return pl.pallas_call(
        body, grid=(nb + 1,),
        in_specs=[pl.BlockSpec(memory_space=pltpu.SMEM), cur(ATTN_WIDTH), cur(KV_WIDTH), prev(KV_WIDTH),
                  cur(KV_WIDTH), prev(KV_WIDTH), cur(ATTN_WIDTH), cur(LANES), cur(LANES), prev(LANES), prev(LANES)],
        out_specs=[cur(ATTN_WIDTH), prev(KV_WIDTH), prev(KV_WIDTH), pl.BlockSpec((8, LANES), lambda n: (0, 0))],
        out_shape=[jax.ShapeDtypeStruct((t, ATTN_WIDTH), BF16), jax.ShapeDtypeStruct((t, KV_WIDTH), BF16),
                   jax.ShapeDtypeStruct((t, KV_WIDTH), BF16), jax.ShapeDtypeStruct((8, LANES), F32)],
        scratch_shapes=[pltpu.VMEM((2 * BLOCK, LANES), F32), pltpu.VMEM((2 * BLOCK, LANES), F32),
                        pltpu.VMEM((BLOCK, LANES), F32), pltpu.VMEM((BLOCK, LANES), F32)],
        compiler_params=_params("arbitrary"), name="attn_bwd",
    )(sinks, q, k, k, v, v, dout, cos_t, sin_t, cos_t, sin_t)


def _inproj_bwd_call(dq, dk, dv, dmixed, x, h0, dx1, w_in, g_mix):
    t = x.shape[0]
    tm = _token_tile(t)
    n_tiles = t // tm

    def body(dq_ref, dk_ref, dv_ref, dm_ref, halo_ref, x_ref, h_ref, dx1_ref, w_ref, g_ref,
             dx_ref, gw_ref, gb_ref, gg_ref, sc_ref):
        i = pl.program_id(0)

        @pl.when(i == 0)
        def _():
            for ref in (gw_ref, gb_ref, gg_ref):
                ref[...] = jnp.zeros_like(ref)

        tok = i * tm + lax.broadcasted_iota(jnp.int32, (tm, 1), 0)
        halo = jnp.where(i == n_tiles - 1, 0.0, halo_ref[...])
        du = []
        for g, size in enumerate(POOL_SIZES):
            lanes = slice(LANES * g, LANES * (g + 1))
            dm = dm_ref[:, lanes]
            sc_ref[0:tm, lanes] = dm / jnp.minimum(tok + 1, size).astype(F32)
            sc_ref[tm:, lanes] = halo[:, lanes] / float(size)
            acc = -dm
            for j in range(size):
                acc = acc + sc_ref[pl.ds(j, tm), lanes]
            du.append(acc)
        dz32 = jnp.concatenate([dq_ref[...].astype(F32), dk_ref[...].astype(F32), dv_ref[...].astype(F32)] + du,
                               axis=1)
        dz = dz32.astype(BF16)
        gb_ref[...] += jnp.sum(dz32, axis=0, keepdims=True)
        gw_ref[...] += _dot_tn(h_ref[...], dz)
        dh = _dot_nt(dz, w_ref[...])
        xh, r = _rms_stats(x_ref[...])
        gg_ref[...] += jnp.sum(dh * xh, axis=0, keepdims=True)
        dx_ref[...] = dx1_ref[...] + _rms_bwd(dh, xh, r, g_ref[...])

    row = lambda w: pl.BlockSpec((tm, w), lambda i: (i, 0))
    const = lambda shape: pl.BlockSpec(shape, lambda i: (0,) * len(shape))
    last_halo = t // POOL_HALO - 1
    halo_after = pl.BlockSpec((POOL_HALO, POOL_WIDTH),
                              lambda i: (jnp.minimum((i + 1) * (tm // POOL_HALO), last_halo), 0))
    return pl.pallas_call(
        body, grid=(n_tiles,),
        in_specs=[row(ATTN_WIDTH), row(KV_WIDTH), row(KV_WIDTH), row(POOL_WIDTH), halo_after, row(D_MODEL),
                  row(D_MODEL), row(D_MODEL), _resident((D_MODEL, IN_WIDTH)), _resident((1, D_MODEL))],
        out_specs=[row(D_MODEL), const((D_MODEL, IN_WIDTH)), const((1, IN_WIDTH)), const((1, D_MODEL))],
        out_shape=[jax.ShapeDtypeStruct((t, D_MODEL), F32), jax.ShapeDtypeStruct((D_MODEL, IN_WIDTH), F32),
                   jax.ShapeDtypeStruct((1, IN_WIDTH), F32), jax.ShapeDtypeStruct((1, D_MODEL), F32)],
        scratch_shapes=[pltpu.VMEM((tm + POOL_HALO, POOL_WIDTH), F32)],
        compiler_params=_params("arbitrary"), name="inproj_bwd",
    )(dq, dk, dv, dmixed, dmixed, x, h0, dx1, w_in, g_mix)


def _local_step(x, target, small, w_in, w_out, w_gate, w_up, w_down):
    t = x.shape[0]
    cos_t, sin_t = _rope_tables(t)
    w_pool = small["w_pool"].astype(BF16)
    h0, q, k, v, u = _inproj_call(x, small["g_mix"], w_in, small["b_in"], cos_t, sin_t)
    attn = _attn_fwd_call(q, k, v, small["sinks"])
    x1, h1, pool_y = _outproj_call(attn, u, x, w_pool, small["b_pool"], small["pool_scale"], w_out,
                                   small["b_out"], small["g_ffn"])
    gate, up, act, dx2, loss, g_g_final = _ffn_fwd_call(h1, x1, target, w_gate, w_up, w_down, small["g_final"])
    dgate, dup, dx1, g_g_ffn = _ffn_bwd_act_call(dx2, gate, up, x1, w_gate, w_up, w_down, small["g_ffn"])
    g_w_gate, g_w_up, g_w_down = _ffn_bwd_weights_call(h1, dgate, dup, act, dx2)
    dattn, dmixed, g_w_out, g_b_out, g_w_pool, g_b_pool, g_pool_scale = _outproj_bwd_call(
        dx1, attn, pool_y, u, w_pool, small["b_pool"], small["pool_scale"], w_out)
    dq, dk, dv, g_sinks = _attn_bwd_call(q, k, v, dattn, small["sinks"], cos_t, sin_t)
    grad_x, g_w_in, g_b_in, g_g_mix = _inproj_bwd_call(dq, dk, dv, dmixed, x, h0, dx1, w_in, small["g_mix"])
    small_grads = dict(g_mix=g_g_mix, b_in=g_b_in, sinks=g_sinks[:, 0].reshape(1, N_Q_HEADS), w_pool=g_w_pool,
                       b_pool=g_b_pool, pool_scale=g_pool_scale, b_out=g_b_out, g_ffn=g_g_ffn, g_final=g_g_final)
    big_grads = dict(w_in=g_w_in, w_out=g_w_out, w_gate=g_w_gate, w_up=g_w_up, w_down=g_w_down)
    return loss, grad_x, small_grads, big_grads


def _mesh_place():
    x, y, c = lax.axis_index("x"), lax.axis_index("y"), lax.axis_index("c")
    other_chips = [(1 - x, y), (x, 1 - y), (1 - x, 1 - y)]
    return x, y, c, other_chips


def _half(ref, c, rows):
    return ref.at[pl.ds(pl.multiple_of(c * rows, 16), rows), :]


def _gather_call(shards):
    na = len(shards)

    def body(*refs):
        ins, outs = refs[:na], refs[na:2 * na]
        send_sems, recv_sems, local_sems = refs[2 * na:]
        x, y, c, other_chips = _mesh_place()
        me = 2 * x + y
        sibling = (x, y, 1 - c)

        def remote(idx, src, dst, to):
            return pltpu.make_async_remote_copy(src_ref=src, dst_ref=dst, send_sem=send_sems.at[idx],
                                                recv_sem=recv_sems.at[idx], device_id=to, device_id_type=MESH)

        local = [pltpu.make_async_copy(ins[a], outs[a].at[me], local_sems.at[a]) for a in range(na)]
        for cp in local:
            cp.start()
        sends = []
        for j, (cx, cy) in enumerate(other_chips):
            for a in range(na):
                rows = ins[a].shape[0] // 2
                cp = remote(j * na + a, _half(ins[a], c, rows), _half(outs[a].at[me], c, rows), (cx, cy, c))
                cp.start()
                sends.append(cp)
        for j, (cx, cy) in enumerate(other_chips):
            for a in range(na):
                rows = ins[a].shape[0] // 2
                piece = _half(outs[a].at[2 * cx + cy], c, rows)
                remote(j * na + a, piece, piece, sibling).wait_recv()
                cp = remote((3 + j) * na + a, piece, piece, sibling)
                cp.start()
                sends.append(cp)
        for j, (cx, cy) in enumerate(other_chips):
            for a in range(na):
                rows = ins[a].shape[0] // 2
                piece = _half(outs[a].at[2 * cx + cy], 1 - c, rows)
                remote((3 + j) * na + a, piece, piece, sibling).wait_recv()
        for cp in sends:
            cp.wait_send()
        for cp in local:
            cp.wait()

    return pl.pallas_call(
        body,
        in_specs=[HBM_SPEC] * na, out_specs=[HBM_SPEC] * na,
        out_shape=[jax.ShapeDtypeStruct((N_SHARDS,) + s.shape, s.dtype) for s in shards],
        scratch_shapes=[pltpu.SemaphoreType.DMA((6 * na,)), pltpu.SemaphoreType.DMA((6 * na,)),
                        pltpu.SemaphoreType.DMA((na,))],
        compiler_params=pltpu.CompilerParams(has_side_effects=True), name="gather_weights",
    )(*shards)


def _sibling_exchange_call(grads):
    na = len(grads)

    def body(*refs):
        ins, outs = refs[:na], refs[na:2 * na]
        send_sems, recv_sems = refs[2 * na:]
        x, y, c, _ = _mesh_place()
        sibling = (x, y, 1 - c)
        copies = []
        for a in range(na):
            rows = ins[a].shape[1] // 2
            src = ins[a].at[:, pl.ds(pl.multiple_of((1 - c) * rows, 8), rows), :]
            cp = pltpu.make_async_remote_copy(src_ref=src, dst_ref=outs[a], send_sem=send_sems.at[a],
                                              recv_sem=recv_sems.at[a], device_id=sibling, device_id_type=MESH)
            cp.start()
            copies.append(cp)
        for cp in copies:
            cp.wait_recv()
        for cp in copies:
            cp.wait_send()

    return pl.pallas_call(
        body,
        in_specs=[HBM_SPEC] * na, out_specs=[HBM_SPEC] * na,
        out_shape=[jax.ShapeDtypeStruct((g.shape[0], g.shape[1] // 2, g.shape[2]), g.dtype) for g in grads],
        scratch_shapes=[pltpu.SemaphoreType.DMA((na,)), pltpu.SemaphoreType.DMA((na,))],
        compiler_params=pltpu.CompilerParams(has_side_effects=True), name="reduce_sibling_exchange",
    )(*grads)


def _pair_sum_call(grads, landed, c):
    na = len(grads)
    split = 2

    def body(c_ref, *refs):
        ins, lands, outs = refs[:na], refs[na:2 * na], refs[2 * na:]
        for a in range(na):
            outs[a][...] = ins[a][...] + lands[a][...]

    in_specs, land_specs, out_specs, views = [], [], [], []
    for g in grads:
        ns, r, cols = g.shape
        rows = r // 2 // split
        views.append(g.reshape(ns, 2, r // 2, cols))
        in_specs.append(pl.BlockSpec((None, None, rows, cols), lambda s, i, c_ref: (s, c_ref[0], i, 0)))
        land_specs.append(pl.BlockSpec((None, rows, cols), lambda s, i, c_ref: (s, i, 0)))
        out_specs.append(pl.BlockSpec((None, rows, cols), lambda s, i, c_ref: (s, i, 0)))
    return pl.pallas_call(
        body,
        grid_spec=pltpu.PrefetchScalarGridSpec(num_scalar_prefetch=1, grid=(N_SHARDS, split),
                                               in_specs=in_specs + land_specs, out_specs=out_specs),
        out_shape=[jax.ShapeDtypeStruct(l.shape, F32) for l in landed],
        compiler_params=_params("parallel", "parallel"), name="reduce_pair_sum",
    )(c.reshape(1), *views, *landed)


def _chip_exchange_call(partials):
    na = len(partials)

    def body(*refs):
        ins, outs = refs[:na], refs[na:2 * na]
        send_sems, recv_sems, local_sems = refs[2 * na:]
        x, y, c, other_chips = _mesh_place()
        me = 2 * x + y
        local = [pltpu.make_async_copy(ins[a].at[me], outs[a].at[me], local_sems.at[a]) for a in range(na)]
        for cp in local:
            cp.start()
        sends = []
        for j, (cx, cy) in enumerate(other_chips):
            for a in range(na):
                cp = pltpu.make_async_remote_copy(
                    src_ref=ins[a].at[2 * cx + cy], dst_ref=outs[a].at[me], send_sem=send_sems.at[j * na + a],
                    recv_sem=recv_sems.at[j * na + a], device_id=(cx, cy, c), device_id_type=MESH)
                cp.start()
                sends.append(cp)
        for j, (cx, cy) in enumerate(other_chips):
            for a in range(na):
                slot = outs[a].at[2 * cx + cy]
                pltpu.make_async_remote_copy(
                    src_ref=slot, dst_ref=slot, send_sem=send_sems.at[j * na + a],
                    recv_sem=recv_sems.at[j * na + a], device_id=(cx, cy, c), device_id_type=MESH).wait_recv()
        for cp in sends:
            cp.wait_send()
        for cp in local:
            cp.wait()

    return pl.pallas_call(
        body,
        in_specs=[HBM_SPEC] * na, out_specs=[HBM_SPEC] * na,
        out_shape=[jax.ShapeDtypeStruct(p.shape, p.dtype) for p in partials],
        scratch_shapes=[pltpu.SemaphoreType.DMA((3 * na,)), pltpu.SemaphoreType.DMA((3 * na,)),
                        pltpu.SemaphoreType.DMA((na,))],
        compiler_params=pltpu.CompilerParams(has_side_effects=True), name="reduce_chip_exchange",
    )(*partials)


def _chip_sum_call(slots):
    na = len(slots)
    split = 4

    def body(*refs):
        ins, outs = refs[:na], refs[na:]
        for a in range(na):
            outs[a][...] = ((ins[a][0] + ins[a][1]) + ins[a][2]) + ins[a][3]

    in_specs, out_specs = [], []
    for s in slots:
        ns, h, cols = s.shape
        in_specs.append(pl.BlockSpec((ns, h // split, cols), lambda i: (0, i, 0)))
        out_specs.append(pl.BlockSpec((h // split, cols), lambda i: (i, 0)))
    return pl.pallas_call(
        body, grid=(split,), in_specs=in_specs, out_specs=out_specs,
        out_shape=[jax.ShapeDtypeStruct(s.shape[1:], F32) for s in slots],
        compiler_params=_params("parallel"), name="reduce_chip_sum",
    )(*slots)


def _sibling_allgather_call(halves):
    na = len(halves)

    def body(*refs):
        ins, outs = refs[:na], refs[na:2 * na]
        send_sems, recv_sems, local_sems = refs[2 * na:]
        x, y, c, _ = _mesh_place()
        sibling = (x, y, 1 - c)
        copies = []
        for a in range(na):
            rows = ins[a].shape[0]
            mine = outs[a].at[pl.ds(pl.multiple_of(c * rows, 8), rows), :]
            lc = pltpu.make_async_copy(ins[a], mine, local_sems.at[a])
            lc.start()
            cp = pltpu.make_async_remote_copy(src_ref=ins[a], dst_ref=mine, send_sem=send_sems.at[a],
                                              recv_sem=recv_sems.at[a], device_id=sibling, device_id_type=MESH)
            cp.start()
            copies.append((lc, cp))
        for a, (lc, cp) in enumerate(copies):
            rows = ins[a].shape[0]
            theirs = outs[a].at[pl.ds(pl.multiple_of((1 - c) * rows, 8), rows), :]
            pltpu.make_async_remote_copy(src_ref=theirs, dst_ref=theirs, send_sem=send_sems.at[a],
                                         recv_sem=recv_sems.at[a], device_id=sibling, device_id_type=MESH).wait_recv()
        for lc, cp in copies:
            cp.wait_send()
            lc.wait()

    return pl.pallas_call(
        body,
        in_specs=[HBM_SPEC] * na, out_specs=[HBM_SPEC] * na,
        out_shape=[jax.ShapeDtypeStruct((2 * h.shape[0], h.shape[1]), h.dtype) for h in halves],
        scratch_shapes=[pltpu.SemaphoreType.DMA((na,)), pltpu.SemaphoreType.DMA((na,)),
                        pltpu.SemaphoreType.DMA((na,))],
        compiler_params=pltpu.CompilerParams(has_side_effects=True), name="reduce_sibling_allgather",
    )(*halves)


def _adamw_call(ws, gs, ms, vs):
    na = len(ws)
    split = 4

    def body(*refs):
        w_refs, g_refs, m_refs, v_refs = (refs[k * na:(k + 1) * na] for k in range(4))
        d_refs, nm_refs, nv_refs = (refs[(4 + k) * na:(5 + k) * na] for k in range(3))
        for a in range(na):
            g = g_refs[a][...]
            m = ADAM_B1 * m_refs[a][...] + (1.0 - ADAM_B1) * g
            v = ADAM_B2 * v_refs[a][...] + (1.0 - ADAM_B2) * jnp.square(g)
            m_hat = m / (1.0 - ADAM_B1 ** ADAM_STEP)
            v_hat = v / (1.0 - ADAM_B2 ** ADAM_STEP)
            d_refs[a][...] = -ADAM_LR * (m_hat / (jnp.sqrt(v_hat) + ADAM_EPS) + ADAM_WD * w_refs[a][...])
            nm_refs[a][...] = m
            nv_refs[a][...] = v

    specs = [pl.BlockSpec((w.shape[0] // split, w.shape[1]), lambda i: (i, 0)) for w in ws]
    shapes = [jax.ShapeDtypeStruct(w.shape, F32) for w in ws]
    outs = pl.pallas_call(
        body, grid=(split,), in_specs=specs * 4, out_specs=specs * 3, out_shape=shapes * 3,
        compiler_params=_params("parallel"), name="adamw",
    )(*ws, *gs, *ms, *vs)
    return outs[:na], outs[na:2 * na], outs[2 * na:]


def _small_layout(shapes):
    layout, off = {}, 0
    for name in SMALL_NAMES:
        size = 1
        for d in shapes[name]:
            size *= d
        layout[name] = (off, size)
        off += size
    rows = -(-off // (LANES * 64)) * 64
    return layout, rows


def _pack_small(arrays, rows):
    flat = jnp.concatenate([arrays[name].reshape(-1) for name in SMALL_NAMES])
    return jnp.pad(flat, (0, rows * LANES - flat.shape[0])).reshape(rows, LANES)


def _unpack_small(packed, layout, shapes):
    flat = packed.reshape(-1)
    return {name: flat[off:off + size].reshape(shapes[name]) for name, (off, size) in layout.items()}


def kernel(x, g_mix, w_in, b_in, sinks, w_pool, b_pool, pool_scale, w_out, b_out, g_ffn, w_gate, w_up, w_down, g_final, loss_target, m_g_mix, m_w_in, m_b_in, m_sinks, m_w_pool, m_b_pool, m_pool_scale, m_w_out, m_b_out, m_g_ffn, m_w_gate, m_w_up, m_w_down, m_g_final, v_g_mix, v_w_in, v_b_in, v_sinks, v_w_pool, v_b_pool, v_pool_scale, v_w_out, v_b_out, v_g_ffn, v_w_gate, v_w_up, v_w_down, v_g_final):
    given = dict(locals())
    big_names = ("w_in", "w_out", "w_gate", "w_up", "w_down")
    out_shapes = {n: given[n].shape for n in SMALL_NAMES + big_names}
    t = x.shape[1]
    c = lax.axis_index("c")

    shards = [given[n][0].astype(BF16) for n in big_names]
    g_in, g_out, g_gate, g_up, g_down = _gather_call(shards)
    w_in_full = jnp.transpose(g_in, (1, 0, 2)).reshape(D_MODEL, IN_WIDTH)
    w_out_full = g_out.reshape(D_MODEL, D_MODEL)

    small = dict(g_mix=g_mix, b_in=b_in, sinks=sinks, w_pool=w_pool[0], b_pool=b_pool.reshape(1, POOL_WIDTH),
                 pool_scale=pool_scale.reshape(1, POOL_WIDTH), b_out=b_out, g_ffn=g_ffn,
                 g_final=g_final.reshape(1, D_MODEL))
    loss, grad_x, small_grads, big_grads = _local_step(
        x[0], loss_target[0], small, w_in_full, w_out_full, g_gate, g_up, g_down)
    loss = lax.psum(loss[0, 0], ("x", "y", "c"))

    layout, small_rows = _small_layout(out_shapes)
    packed = _pack_small(small_grads, small_rows)
    n_in = IN_WIDTH // N_SHARDS
    grads = [
        jnp.transpose(big_grads["w_in"].reshape(D_MODEL, N_SHARDS, n_in), (1, 0, 2)),
        big_grads["w_out"].reshape(N_SHARDS, D_MODEL // N_SHARDS, D_MODEL),
        big_grads["w_gate"], big_grads["w_up"], big_grads["w_down"],
        jnp.broadcast_to(packed[None], (N_SHARDS,) + packed.shape),
    ]
    landed = _sibling_exchange_call(grads)
    partials = _pair_sum_call(grads, landed, c)
    slots = _chip_exchange_call(partials)
    halves = _chip_sum_call(slots)
    totals = _sibling_allgather_call(halves)

    ws = [given[n][0] for n in big_names] + [_pack_small({n: given[n] for n in SMALL_NAMES}, small_rows)]
    ms = [given["m_" + n][0] for n in big_names] + [_pack_small({n: given["m_" + n] for n in SMALL_NAMES}, small_rows)]
    vs = [given["v_" + n][0] for n in big_names] + [_pack_small({n: given["v_" + n] for n in SMALL_NAMES}, small_rows)]
    deltas, new_ms, new_vs = _adamw_call(ws, totals, ms, vs)

    def unpack(arrays):
        out = _unpack_small(arrays[-1], layout, out_shapes)
        for n, a in zip(big_names, arrays[:-1]):
            out[n] = a.reshape(out_shapes[n])
        return out

    order = ("g_mix", "w_in", "b_in", "sinks", "w_pool", "b_pool", "pool_scale", "w_out", "b_out", "g_ffn",
             "w_gate", "w_up", "w_down", "g_final")
    results = [loss, grad_x.reshape(x.shape)]
    for group in (totals, deltas, new_ms, new_vs):
        named = unpack(list(group))
        results += [named[n] for n in order]
    return tuple(results)
```

```python
import functools

import jax
import jax.numpy as jnp
from jax import lax
from jax.experimental import pallas as pl
from jax.experimental.pallas import tpu as pltpu

F32 = jnp.float32
BF16 = jnp.bfloat16

D_MODEL = 1024
ATTN_WIDTH = 512
KV_WIDTH = 128
POOL_WIDTH = 512
IN_WIDTH = ATTN_WIDTH + 2 * KV_WIDTH + POOL_WIDTH
HEAD_DIM = 64
N_Q_HEADS = 8
BLOCK = 128
POOL_SIZES = (2, 4, 8, 16)
POOL_HALO = 16
ROPE_THETA = 10000.0
RMS_EPS = 1e-5
Q_SCALE = HEAD_DIM ** -0.5

ADAM_LR = 0.001
ADAM_B1 = 0.9
ADAM_B2 = 0.999
ADAM_EPS = 1e-08
ADAM_WD = 0.01
ADAM_STEP = 10

N_SHARDS = 4
LANES = 128
VMEM_LIMIT = 56 * 1024 * 1024

MESH = pl.DeviceIdType.MESH
HBM_SPEC = pl.BlockSpec(memory_space=pltpu.HBM)

SMALL_NAMES = ("g_mix", "b_in", "sinks", "w_pool", "b_pool", "pool_scale", "b_out", "g_ffn", "g_final")


def _dot(a, b):
    return jnp.dot(a, b, preferred_element_type=F32)


def _dot_nt(a, b):
    return lax.dot_general(a, b, (((1,), (1,)), ((), ())), preferred_element_type=F32)


def _dot_tn(a, b):
    return lax.dot_general(a, b, (((0,), (0,)), ((), ())), preferred_element_type=F32)


def _params(*semantics):
    return pltpu.CompilerParams(dimension_semantics=semantics, vmem_limit_bytes=VMEM_LIMIT)


def _resident(shape):
    return pl.BlockSpec(shape, lambda *_: (0,) * len(shape), pipeline_mode=pl.Buffered(1))


def _token_tile(t):
    for tm in (512, 256, 128):
        if t % tm == 0:
            return tm
    raise ValueError(f"sequence length {t} is not a multiple of 128")


def _rms_stats(xf):
    r = lax.rsqrt(jnp.mean(xf * xf, axis=-1, keepdims=True) + RMS_EPS)
    return xf * r, r


def _rms_bwd(dy, xh, r, g):
    dxh = dy * g
    return r * (dxh - xh * jnp.mean(dxh * xh, axis=-1, keepdims=True))


def _rope_partner(t, first):
    return jnp.where(first, pltpu.roll(t, LANES - HEAD_DIM // 2, 1), pltpu.roll(t, HEAD_DIM // 2, 1))


def _first_half_mask(shape):
    lane = lax.broadcasted_iota(jnp.int32, shape, 1)
    return (lane % HEAD_DIM) < (HEAD_DIM // 2)


def _rope_tables(t):
    inv_freq = 1.0 / (ROPE_THETA ** (jnp.arange(0, HEAD_DIM, 2, dtype=F32) / HEAD_DIM))
    ang = jnp.arange(t, dtype=F32)[:, None] * inv_freq[None, :]
    cos, sin = jnp.cos(ang), jnp.sin(ang)
    cos_t = jnp.tile(jnp.concatenate([cos, cos], axis=-1), (1, LANES // HEAD_DIM))
    sin_t = jnp.tile(jnp.concatenate([-sin, sin], axis=-1), (1, LANES // HEAD_DIM))
    return cos_t, sin_t


def _inproj_call(x, g_mix, w_in, b_in, cos_t, sin_t):
    t = x.shape[0]
    tm = _token_tile(t)

    def body(x_ref, g_ref, w_ref, b_ref, c_ref, s_ref, h_ref, q_ref, k_ref, v_ref, u_ref):
        xh, _ = _rms_stats(x_ref[...])
        h = (xh * g_ref[...]).astype(BF16)
        h_ref[...] = h
        z = _dot(h, w_ref[...]) + b_ref[...]
        cos, sin = c_ref[...], s_ref[...]
        first = _first_half_mask((tm, LANES))
        for j in range(ATTN_WIDTH // LANES):
            zj = z[:, LANES * j:LANES * (j + 1)]
            q_ref[:, LANES * j:LANES * (j + 1)] = ((zj * cos + _rope_partner(zj, first) * sin) * Q_SCALE).astype(BF16)
        zk = z[:, ATTN_WIDTH:ATTN_WIDTH + KV_WIDTH]
        k_ref[...] = (zk * cos + _rope_partner(zk, first) * sin).astype(BF16)
        v_ref[...] = z[:, ATTN_WIDTH + KV_WIDTH:ATTN_WIDTH + 2 * KV_WIDTH].astype(BF16)
        u_ref[...] = z[:, ATTN_WIDTH + 2 * KV_WIDTH:]

    row = lambda w: pl.BlockSpec((tm, w), lambda i: (i, 0))
    return pl.pallas_call(
        body, grid=(t // tm,),
        in_specs=[row(D_MODEL), _resident((1, D_MODEL)), _resident((D_MODEL, IN_WIDTH)), _resident((1, IN_WIDTH)),
                  row(LANES), row(LANES)],
        out_specs=[row(D_MODEL), row(ATTN_WIDTH), row(KV_WIDTH), row(KV_WIDTH), row(POOL_WIDTH)],
        out_shape=[jax.ShapeDtypeStruct((t, D_MODEL), BF16), jax.ShapeDtypeStruct((t, ATTN_WIDTH), BF16),
                   jax.ShapeDtypeStruct((t, KV_WIDTH), BF16), jax.ShapeDtypeStruct((t, KV_WIDTH), BF16),
                   jax.ShapeDtypeStruct((t, POOL_WIDTH), F32)],
        compiler_params=_params("parallel"), name="inproj_fwd",
    )(x, g_mix, w_in, b_in, cos_t, sin_t)


def _head_in_both_halves(band, kv_head):
    lane = lax.broadcasted_iota(jnp.int32, band.shape, 1)
    own = (lane < HEAD_DIM) if kv_head == 0 else (lane >= HEAD_DIM)
    return jnp.where(own, band, pltpu.roll(band, HEAD_DIM, 1)).astype(BF16)


def _stack_heads(ref, kv_head, row0):
    low = lax.broadcasted_iota(jnp.int32, (BLOCK, LANES), 1) < HEAD_DIM
    parts = []
    for jj in range(2):
        slab = ref[row0:row0 + BLOCK, LANES * (2 * kv_head + jj):LANES * (2 * kv_head + jj + 1)]
        zero = jnp.zeros_like(slab)
        parts += [jnp.where(low, slab, zero), jnp.where(low, zero, slab)]
    return jnp.concatenate(parts, axis=0)


def _unstack_heads(o):
    low = lax.broadcasted_iota(jnp.int32, (BLOCK, LANES), 1) < HEAD_DIM
    return [jnp.where(low, o[2 * BLOCK * jj:2 * BLOCK * jj + BLOCK], o[2 * BLOCK * jj + BLOCK:2 * BLOCK * (jj + 1)])
            for jj in range(2)]


def _band_biases():
    i = lax.broadcasted_iota(jnp.int32, (4 * BLOCK, 2 * BLOCK), 0) % BLOCK
    c = lax.broadcasted_iota(jnp.int32, (4 * BLOCK, 2 * BLOCK), 1)
    vis = (c > i) & (c <= i + BLOCK)
    return jnp.where(vis, 0.0, -jnp.inf).astype(F32), jnp.where(vis & (c >= BLOCK), 0.0, -jnp.inf).astype(F32)


def _blocks_per_step(nb):
    for g in (4, 2, 1):
        if nb % g == 0:
            return g


def _sink_column(sink_ref, kv_head):
    return jnp.concatenate([jnp.full((BLOCK, 1), sink_ref[0, 4 * kv_head + g], F32) for g in range(4)], axis=0)


def _softmax_with_sink(s, sink):
    m = jnp.maximum(jnp.max(s, axis=-1, keepdims=True), sink)
    p = jnp.exp(s - m)
    e_sink = jnp.exp(sink - m)
    inv = 1.0 / (jnp.sum(p, axis=-1, keepdims=True) + e_sink)
    return p * inv, e_sink * inv


def _kv_band(cur_ref, prev_ref, g):
    r0 = g * BLOCK
    prev = prev_ref[...] if g == 0 else cur_ref[r0 - BLOCK:r0, :]
    return jnp.concatenate([prev, cur_ref[r0:r0 + BLOCK, :]], axis=0).astype(F32)


def _attn_fwd_call(q, k, v, sinks, bias, bias_first):
    t = q.shape[0]
    nb = t // BLOCK
    gb = _blocks_per_step(nb)
    qb = gb * BLOCK

    def body(sink_ref, q_ref, kc_ref, kp_ref, vc_ref, vp_ref, b_ref, bf_ref, o_ref):
        n = pl.program_id(0)
        for g in range(gb):
            r0 = g * BLOCK
            kband = _kv_band(kc_ref, kp_ref, g)
            vband = _kv_band(vc_ref, vp_ref, g)
            bias_g = jnp.where(n == 0, bf_ref[...], b_ref[...]) if g == 0 else b_ref[...]
            for kh in range(2):
                k2 = _head_in_both_halves(kband, kh)
                v2 = _head_in_both_halves(vband, kh)
                s = _dot_nt(_stack_heads(q_ref, kh, r0), k2) + bias_g
                p, _ = _softmax_with_sink(s, _sink_column(sink_ref, kh))
                o = _dot(p.astype(BF16), v2)
                for jj, slab in enumerate(_unstack_heads(o)):
                    o_ref[r0:r0 + BLOCK, LANES * (2 * kh + jj):LANES * (2 * kh + jj + 1)] = slab.astype(BF16)

    cur = lambda w: pl.BlockSpec((qb, w), lambda n: (n, 0))
    prev = lambda w: pl.BlockSpec((BLOCK, w), lambda n: (jnp.maximum(n * gb - 1, 0), 0))
    return pl.pallas_call(
        body, grid=(nb // gb,),
        in_specs=[pl.BlockSpec(memory_space=pltpu.SMEM), cur(ATTN_WIDTH), cur(KV_WIDTH), prev(KV_WIDTH),
                  cur(KV_WIDTH), prev(KV_WIDTH), _resident(bias.shape), _resident(bias.shape)],
        out_specs=cur(ATTN_WIDTH),
        out_shape=jax.ShapeDtypeStruct((t, ATTN_WIDTH), BF16),
        compiler_params=_params("parallel"), name="attn_fwd",
    )(sinks, q, k, k, v, v, bias, bias_first)


def _pool_mixed(sc_ref, u_ref, halo_ref, i, tm):
    sc_ref[0:POOL_HALO, :] = jnp.where(i == 0, 0.0, halo_ref[...])
    sc_ref[POOL_HALO:, :] = u_ref[...]
    tok = i * tm + lax.broadcasted_iota(jnp.int32, (tm, 1), 0)
    mixed = []
    for g, size in enumerate(POOL_SIZES):
        lanes = slice(LANES * g, LANES * (g + 1))
        cur = sc_ref[pl.ds(POOL_HALO, tm), lanes]
        acc = cur
        for j in range(1, size):
            acc = acc + sc_ref[pl.ds(POOL_HALO - j, tm), lanes]
        count = jnp.minimum(tok + 1, size).astype(F32)
        mixed.append(acc / count - cur)
    return mixed


def _halo_before(tm, width):
    return pl.BlockSpec((POOL_HALO, width), lambda i: (jnp.maximum(i * (tm // POOL_HALO) - 1, 0), 0))


def _outproj_call(attn, u, x, w_pool, b_pool, pool_scale, w_out, b_out, g_ffn):
    t = x.shape[0]
    tm = _token_tile(t)

    def body(a_ref, u_ref, halo_ref, x_ref, wp_ref, bp_ref, ps_ref, wo_ref, bo_ref, g_ref,
             x1_ref, h1_ref, y_ref, sc_ref):
        i = pl.program_id(0)
        for g, mixed in enumerate(_pool_mixed(sc_ref, u_ref, halo_ref, i, tm)):
            lanes = slice(LANES * g, LANES * (g + 1))
            lin = _dot(mixed.astype(BF16), wp_ref[g]) + bp_ref[:, lanes]
            y_ref[:, lanes] = (lin * ps_ref[:, lanes]).astype(BF16)
        x1 = (x_ref[...] + _dot(a_ref[...], wo_ref[0:ATTN_WIDTH, :]) + _dot(y_ref[...], wo_ref[ATTN_WIDTH:, :])
              + bo_ref[...])
        x1_ref[...] = x1
        xh, _ = _rms_stats(x1)
        h1_ref[...] = (xh * g_ref[...]).astype(BF16)

    row = lambda w: pl.BlockSpec((tm, w), lambda i: (i, 0))
    return pl.pallas_call(
        body, grid=(t // tm,),
        in_specs=[row(ATTN_WIDTH), row(POOL_WIDTH), _halo_before(tm, POOL_WIDTH), row(D_MODEL),
                  _resident((4, LANES, LANES)), _resident((1, POOL_WIDTH)), _resident((1, POOL_WIDTH)),
                  _resident((D_MODEL, D_MODEL)), _resident((1, D_MODEL)), _resident((1, D_MODEL))],
        out_specs=[row(D_MODEL), row(D_MODEL), row(POOL_WIDTH)],
        out_shape=[jax.ShapeDtypeStruct((t, D_MODEL), F32), jax.ShapeDtypeStruct((t, D_MODEL), BF16),
                   jax.ShapeDtypeStruct((t, POOL_WIDTH), BF16)],
        scratch_shapes=[pltpu.VMEM((tm + POOL_HALO, POOL_WIDTH), F32)],
        compiler_params=_params("parallel"), name="outproj_fwd",
    )(attn, u, u, x, w_pool, b_pool, pool_scale, w_out, b_out, g_ffn)


def _ffn_fwd_call(h1, x1, target, w_gate, w_up, w_down, g_final):
    t = x1.shape[0]
    tm = _token_tile(t)
    ns, _, fs = w_gate.shape

    def body(h_ref, x1_ref, tgt_ref, wg_ref, wu_ref, wd_ref, gf_ref,
             gate_ref, up_ref, act_ref, dx2_ref, loss_ref, dgf_ref, acc_ref):
        i, s = pl.program_id(0), pl.program_id(1)

        @pl.when(s == 0)
        def _():
            acc_ref[...] = x1_ref[...]

        @pl.when((i == 0) & (s == 0))
        def _():
            loss_ref[...] = jnp.zeros_like(loss_ref)
            dgf_ref[...] = jnp.zeros_like(dgf_ref)

        h = h_ref[...]
        gate = _dot(h, wg_ref[...])
        up = _dot(h, wu_ref[...])
        act = (gate * jax.nn.sigmoid(gate) * up).astype(BF16)
        gate_ref[...] = gate.astype(BF16)
        up_ref[...] = up.astype(BF16)
        act_ref[...] = act
        acc_ref[...] += _dot(act, wd_ref[...])

        @pl.when(s == ns - 1)
        def _():
            xh, r = _rms_stats(acc_ref[...])
            gf = gf_ref[...]
            err = xh * gf - tgt_ref[...]
            loss_ref[...] += 0.5 * jnp.sum(err * err) / D_MODEL
            dy = err / D_MODEL
            dgf_ref[...] += jnp.sum(dy * xh, axis=0, keepdims=True)
            dx2_ref[...] = _rms_bwd(dy, xh, r, gf)

    row = lambda w: pl.BlockSpec((tm, w), lambda i, s: (i, 0))
    shard_act = pl.BlockSpec((None, tm, fs), lambda i, s: (s, i, 0))
    act_shape = jax.ShapeDtypeStruct((ns, t, fs), BF16)
    return pl.pallas_call(
        body, grid=(t // tm, ns),
        in_specs=[row(D_MODEL), row(D_MODEL), row(D_MODEL),
                  pl.BlockSpec((None, D_MODEL, fs), lambda i, s: (s, 0, 0)),
                  pl.BlockSpec((None, D_MODEL, fs), lambda i, s: (s, 0, 0)),
                  pl.BlockSpec((None, fs, D_MODEL), lambda i, s: (s, 0, 0)),
                  _resident((1, D_MODEL))],
        out_specs=[shard_act, shard_act, shard_act, row(D_MODEL),
                   pl.BlockSpec((1, LANES), lambda i, s: (0, 0)), pl.BlockSpec((1, D_MODEL), lambda i, s: (0, 0))],
        out_shape=[act_shape, act_shape, act_shape, jax.ShapeDtypeStruct((t, D_MODEL), F32),
                   jax.ShapeDtypeStruct((1, LANES), F32), jax.ShapeDtypeStruct((1, D_MODEL), F32)],
        scratch_shapes=[pltpu.VMEM((tm, D_MODEL), F32)],
        compiler_params=_params("arbitrary", "arbitrary"), name="ffn_fwd",
    )(h1, x1, target, w_gate, w_up, w_down, g_final)


def _ffn_bwd_act_call(dx2, gate, up, x1, w_gate, w_up, w_down, g_ffn):
    t = x1.shape[0]
    tm = _token_tile(t)
    ns, _, fs = w_gate.shape

    def body(dx2_ref, gate_ref, up_ref, x1_ref, wg_ref, wu_ref, wd_ref, g_ref,
             dgate_ref, dup_ref, dx1_ref, dg_ref, acc_ref, dx2b_ref):
        i, s = pl.program_id(0), pl.program_id(1)

        @pl.when(s == 0)
        def _():
            acc_ref[...] = jnp.zeros_like(acc_ref)
            dx2b_ref[...] = dx2_ref[...].astype(BF16)

        @pl.when((i == 0) & (s == 0))
        def _():
            dg_ref[...] = jnp.zeros_like(dg_ref)

        dact = _dot_nt(dx2b_ref[...], wd_ref[...])
        gate = gate_ref[...].astype(F32)
        upv = up_ref[...].astype(F32)
        sg = jax.nn.sigmoid(gate)
        dup = (dact * (gate * sg)).astype(BF16)
        dgate = (dact * upv * (sg * (1.0 + gate * (1.0 - sg)))).astype(BF16)
        dgate_ref[...] = dgate
        dup_ref[...] = dup
        acc_ref[...] += _dot_nt(dgate, wg_ref[...]) + _dot_nt(dup, wu_ref[...])

        @pl.when(s == ns - 1)
        def _():
            xh, r = _rms_stats(x1_ref[...])
            dh = acc_ref[...]
            dg_ref[...] += jnp.sum(dh * xh, axis=0, keepdims=True)
            dx1_ref[...] = dx2_ref[...] + _rms_bwd(dh, xh, r, g_ref[...])

    row = lambda w: pl.BlockSpec((tm, w), lambda i, s: (i, 0))
    shard_act = pl.BlockSpec((None, tm, fs), lambda i, s: (s, i, 0))
    act_shape = jax.ShapeDtypeStruct((ns, t, fs), BF16)
    return pl.pallas_call(
        body, grid=(t // tm, ns),
        in_specs=[row(D_MODEL), shard_act, shard_act, row(D_MODEL),
                  pl.BlockSpec((None, D_MODEL, fs), lambda i, s: (s, 0, 0)),
                  pl.BlockSpec((None, D_MODEL, fs), lambda i, s: (s, 0, 0)),
                  pl.BlockSpec((None, fs, D_MODEL), lambda i, s: (s, 0, 0)),
                  _resident((1, D_MODEL))],
        out_specs=[shard_act, shard_act, row(D_MODEL), pl.BlockSpec((1, D_MODEL), lambda i, s: (0, 0))],
        out_shape=[act_shape, act_shape, jax.ShapeDtypeStruct((t, D_MODEL), F32),
                   jax.ShapeDtypeStruct((1, D_MODEL), F32)],
        scratch_shapes=[pltpu.VMEM((tm, D_MODEL), F32), pltpu.VMEM((tm, D_MODEL), BF16)],
        compiler_params=_params("arbitrary", "arbitrary"), name="ffn_bwd_act",
    )(dx2, gate, up, x1, w_gate, w_up, w_down, g_ffn)


def _ffn_bwd_weights_call(h1, dgate, dup, act, dx2):
    t = h1.shape[0]
    tm = _token_tile(t)
    ns, _, fs = dgate.shape

    def body(h_ref, dgate_ref, dup_ref, act_ref, dx2_ref, gg_ref, gu_ref, gd_ref):
        @pl.when(pl.program_id(1) == 0)
        def _():
            gg_ref[...] = jnp.zeros_like(gg_ref)
            gu_ref[...] = jnp.zeros_like(gu_ref)
            gd_ref[...] = jnp.zeros_like(gd_ref)

        h = h_ref[...]
        gg_ref[...] += _dot_tn(h, dgate_ref[...])
        gu_ref[...] += _dot_tn(h, dup_ref[...])
        gd_ref[...] += _dot_tn(act_ref[...], dx2_ref[...].astype(BF16))

    row = lambda w: pl.BlockSpec((tm, w), lambda s, i: (i, 0))
    shard_act = pl.BlockSpec((None, tm, fs), lambda s, i: (s, i, 0))
    return pl.pallas_call(
        body, grid=(ns, t // tm),
        in_specs=[row(D_MODEL), shard_act, shard_act, shard_act, row(D_MODEL)],
        out_specs=[pl.BlockSpec((None, D_MODEL, fs), lambda s, i: (s, 0, 0)),
                   pl.BlockSpec((None, D_MODEL, fs), lambda s, i: (s, 0, 0)),
                   pl.BlockSpec((None, fs, D_MODEL), lambda s, i: (s, 0, 0))],
        out_shape=[jax.ShapeDtypeStruct((ns, D_MODEL, fs), F32), jax.ShapeDtypeStruct((ns, D_MODEL, fs), F32),
                   jax.ShapeDtypeStruct((ns, fs, D_MODEL), F32)],
        compiler_params=_params("parallel", "arbitrary"), name="ffn_bwd_weights",
    )(h1, dgate, dup, act, dx2)


def _outproj_bwd_call(dx1, attn, pool_y, u, w_pool, b_pool, pool_scale, w_out):
    t = dx1.shape[0]
    tm = _token_tile(t)

    def body(dx1_ref, a_ref, y_ref, u_ref, halo_ref, wp_ref, bp_ref, ps_ref, wo_ref,
             dattn_ref, dmixed_ref, gwo_ref, gbo_ref, gwp_ref, gbp_ref, gps_ref, sc_ref):
        i = pl.program_id(0)

        @pl.when(i == 0)
        def _():
            for ref in (gwo_ref, gbo_ref, gwp_ref, gbp_ref, gps_ref):
                ref[...] = jnp.zeros_like(ref)

        dx1 = dx1_ref[...]
        dx1b = dx1.astype(BF16)
        dcat = _dot_nt(dx1b, wo_ref[...])
        dattn_ref[...] = dcat[:, :ATTN_WIDTH].astype(BF16)
        for g, mixed in enumerate(_pool_mixed(sc_ref, u_ref, halo_ref, i, tm)):
            lanes = slice(LANES * g, LANES * (g + 1))
            mixed_b = mixed.astype(BF16)
            lin = _dot(mixed_b, wp_ref[g]) + bp_ref[:, lanes]
            dpool = dcat[:, ATTN_WIDTH + LANES * g:ATTN_WIDTH + LANES * (g + 1)]
            gps_ref[:, lanes] += jnp.sum(dpool * lin, axis=0, keepdims=True)
            dy = dpool * ps_ref[:, lanes]
            gbp_ref[:, lanes] += jnp.sum(dy, axis=0, keepdims=True)
            dy_b = dy.astype(BF16)
            gwp_ref[g] += _dot_tn(mixed_b, dy_b)
            dmixed_ref[:, lanes] = _dot_nt(dy_b, wp_ref[g])
        gwo_ref[0:ATTN_WIDTH, :] += _dot_tn(a_ref[...], dx1b)
        gwo_ref[ATTN_WIDTH:, :] += _dot_tn(y_ref[...], dx1b)
        gbo_ref[...] += jnp.sum(dx1, axis=0, keepdims=True)

    row = lambda w: pl.BlockSpec((tm, w), lambda i: (i, 0))
    const = lambda shape: pl.BlockSpec(shape, lambda i: (0,) * len(shape))
    return pl.pallas_call(
        body, grid=(t // tm,),
        in_specs=[row(D_MODEL), row(ATTN_WIDTH), row(POOL_WIDTH), row(POOL_WIDTH), _halo_before(tm, POOL_WIDTH),
                  _resident((4, LANES, LANES)), _resident((1, POOL_WIDTH)), _resident((1, POOL_WIDTH)),
                  _resident((D_MODEL, D_MODEL))],
        out_specs=[row(ATTN_WIDTH), row(POOL_WIDTH), const((D_MODEL, D_MODEL)), const((1, D_MODEL)),
                   const((4, LANES, LANES)), const((1, POOL_WIDTH)), const((1, POOL_WIDTH))],
        out_shape=[jax.ShapeDtypeStruct((t, ATTN_WIDTH), BF16), jax.ShapeDtypeStruct((t, POOL_WIDTH), F32),
                   jax.ShapeDtypeStruct((D_MODEL, D_MODEL), F32), jax.ShapeDtypeStruct((1, D_MODEL), F32),
                   jax.ShapeDtypeStruct((4, LANES, LANES), F32), jax.ShapeDtypeStruct((1, POOL_WIDTH), F32),
                   jax.ShapeDtypeStruct((1, POOL_WIDTH), F32)],
        scratch_shapes=[pltpu.VMEM((tm + POOL_HALO, POOL_WIDTH), F32)],
        compiler_params=_params("arbitrary"), name="outproj_bwd",
    )(dx1, attn, pool_y, u, u, w_pool, b_pool, pool_scale, w_out)


def _attn_bwd_call(q, k, v, dout, sinks, cos_t, sin_t, bias, bias_first):
    t = q.shape[0]
    nb = t // BLOCK
    gb = _blocks_per_step(nb)
    qb = gb * BLOCK
    steps = nb // gb

    def body(sink_ref, q_ref, kc_ref, kp_ref, vc_ref, vp_ref, do_ref, cq_ref, sq_ref, ck_ref, sk_ref, b_ref, bf_ref,
             dq_ref, dk_ref, dv_ref, dsink_ref, newk_ref, newv_ref, carryk_ref, carryv_ref, headk_ref, headv_ref):
        n = pl.program_id(0)
        first = _first_half_mask((BLOCK, LANES))

        @pl.when(n == 0)
        def _():
            dsink_ref[...] = jnp.zeros_like(dsink_ref)

        @pl.when(n < steps)
        def _():
            head_row = lax.broadcasted_iota(jnp.int32, dsink_ref.shape, 0)
            lane2 = lax.broadcasted_iota(jnp.int32, (2 * BLOCK, LANES), 1)
            dsink = jnp.zeros(dsink_ref.shape, F32)
            for g in range(gb):
                r0 = g * BLOCK
                kband = _kv_band(kc_ref, kp_ref, g)
                vband = _kv_band(vc_ref, vp_ref, g)
                bias_g = jnp.where(n == 0, bf_ref[...], b_ref[...]) if g == 0 else b_ref[...]
                cq, sq = cq_ref[r0:r0 + BLOCK, :], sq_ref[r0:r0 + BLOCK, :]
                dk_heads, dv_heads = [], []
                for kh in range(2):
                    k2 = _head_in_both_halves(kband, kh)
                    v2 = _head_in_both_halves(vband, kh)
                    qst = _stack_heads(q_ref, kh, r0)
                    dost = _stack_heads(do_ref, kh, r0)
                    p, p_sink = _softmax_with_sink(_dot_nt(qst, k2) + bias_g, _sink_column(sink_ref, kh))
                    dp = _dot_nt(dost, v2)
                    delta = jnp.sum(p * dp, axis=-1, keepdims=True)
                    ds = (p * (dp - delta)).astype(BF16)
                    sink_term = p_sink * delta
                    for h in range(4):
                        val = -jnp.sum(sink_term[BLOCK * h:BLOCK * (h + 1)])
                        dsink = dsink + jnp.where(head_row == 4 * kh + h, val, 0.0)
                    for jj, slab in enumerate(_unstack_heads(_dot(ds, k2))):
                        dq_ref[r0:r0 + BLOCK, LANES * (2 * kh + jj):LANES * (2 * kh + jj + 1)] = (
                            (slab * cq + _rope_partner(slab * sq, first)) * Q_SCALE).astype(BF16)
                    dk2 = _dot_tn(ds, qst)
                    dv2 = _dot_tn(p.astype(BF16), dost)
                    dk_heads.append(dk2 + pltpu.roll(dk2, HEAD_DIM, 1))
                    dv_heads.append(dv2 + pltpu.roll(dv2, HEAD_DIM, 1))
                bandk = jnp.where(lane2 < HEAD_DIM, dk_heads[0], dk_heads[1])
                bandv = jnp.where(lane2 < HEAD_DIM, dv_heads[0], dv_heads[1])
                if g == 0:
                    headk_ref[...] = bandk[0:BLOCK]
                    headv_ref[...] = bandv[0:BLOCK]
                else:
                    newk_ref[r0 - BLOCK:r0, :] += bandk[0:BLOCK]
                    newv_ref[r0 - BLOCK:r0, :] += bandv[0:BLOCK]
                newk_ref[r0:r0 + BLOCK, :] = bandk[BLOCK:]
                newv_ref[r0:r0 + BLOCK, :] = bandv[BLOCK:]
            dsink_ref[...] += dsink

        @pl.when(n == steps)
        def _():
            headk_ref[...] = jnp.zeros_like(headk_ref)
            headv_ref[...] = jnp.zeros_like(headv_ref)

        @pl.when(n >= 1)
        def _():
            last = qb - BLOCK
            carryk_ref[last:, :] += headk_ref[...]
            carryv_ref[last:, :] += headv_ref[...]
            for g in range(gb):
                rows = slice(g * BLOCK, (g + 1) * BLOCK)
                dk = carryk_ref[rows, :]
                dk_ref[rows, :] = (dk * ck_ref[rows, :] + _rope_partner(dk * sk_ref[rows, :], first)).astype(BF16)
            dv_ref[...] = carryv_ref[...].astype(BF16)

        @pl.when(n < steps)
        def _():
            carryk_ref[...] = newk_ref[...]
            carryv_ref[...] = newv_ref[...]

    cur = lambda w: pl.BlockSpec((qb, w), lambda n: (jnp.minimum(n, steps - 1), 0))
    late = lambda w: pl.BlockSpec((qb, w), lambda n: (jnp.maximum(n - 1, 0), 0))
    prev = lambda w: pl.BlockSpec((BLOCK, w), lambda n: (jnp.maximum(jnp.minimum(n, steps - 1) * gb - 1, 0), 0))
    return pl.pallas_call(
        body, grid=(steps + 1,),
        in_specs=[pl.BlockSpec(memory_space=pltpu.SMEM), cur(ATTN_WIDTH), cur(KV_WIDTH), prev(KV_WIDTH),
                  cur(KV_WIDTH), prev(KV_WIDTH), cur(ATTN_WIDTH), cur(LANES), cur(LANES), late(LANES), late(LANES),
                  _resident(bias.shape), _resident(bias.shape)],
        out_specs=[cur(ATTN_WIDTH), late(KV_WIDTH), late(KV_WIDTH), pl.BlockSpec((8, LANES), lambda n: (0, 0))],
        out_shape=[jax.ShapeDtypeStruct((t, ATTN_WIDTH), BF16), jax.ShapeDtypeStruct((t, KV_WIDTH), BF16),
                   jax.ShapeDtypeStruct((t, KV_WIDTH), BF16), jax.ShapeDtypeStruct((8, LANES), F32)],
        scratch_shapes=[pltpu.VMEM((qb, LANES), F32), pltpu.VMEM((qb, LANES), F32),
                        pltpu.VMEM((qb, LANES), F32), pltpu.VMEM((qb, LANES), F32),
                        pltpu.VMEM((BLOCK, LANES), F32), pltpu.VMEM((BLOCK, LANES), F32)],
        compiler_params=_params("arbitrary"), name="attn_bwd",
    )(sinks, q, k, k, v, v, dout, cos_t, sin_t, cos_t, sin_t, bias, bias_first)


def _inproj_bwd_call(dq, dk, dv, dmixed, x, h0, dx1, w_in, g_mix):
    t = x.shape[0]
    tm = _token_tile(t)
    n_tiles = t // tm

    def body(dq_ref, dk_ref, dv_ref, dm_ref, halo_ref, x_ref, h_ref, dx1_ref, w_ref, g_ref,
             dx_ref, gw_ref, gb_ref, gg_ref, sc_ref):
        i = pl.program_id(0)

        @pl.when(i == 0)
        def _():
            for ref in (gw_ref, gb_ref, gg_ref):
                ref[...] = jnp.zeros_like(ref)

        tok = i * tm + lax.broadcasted_iota(jnp.int32, (tm, 1), 0)
        halo = jnp.where(i == n_tiles - 1, 0.0, halo_ref[...])
        du = []
        for g, size in enumerate(POOL_SIZES):
            lanes = slice(LANES * g, LANES * (g + 1))
            dm = dm_ref[:, lanes]
            sc_ref[0:tm, lanes] = dm / jnp.minimum(tok + 1, size).astype(F32)
            sc_ref[tm:, lanes] = halo[:, lanes] / float(size)
            acc = -dm
            for j in range(size):
                acc = acc + sc_ref[pl.ds(j, tm), lanes]
            du.append(acc)
        dz32 = jnp.concatenate([dq_ref[...].astype(F32), dk_ref[...].astype(F32), dv_ref[...].astype(F32)] + du,
                               axis=1)
        dz = dz32.astype(BF16)
        gb_ref[...] += jnp.sum(dz32, axis=0, keepdims=True)
        gw_ref[...] += _dot_tn(h_ref[...], dz)
        dh = _dot_nt(dz, w_ref[...])
        xh, r = _rms_stats(x_ref[...])
        gg_ref[...] += jnp.sum(dh * xh, axis=0, keepdims=True)
        dx_ref[...] = dx1_ref[...] + _rms_bwd(dh, xh, r, g_ref[...])

    row = lambda w: pl.BlockSpec((tm, w), lambda i: (i, 0))
    const = lambda shape: pl.BlockSpec(shape, lambda i: (0,) * len(shape))
    last_halo = t // POOL_HALO - 1
    halo_after = pl.BlockSpec((POOL_HALO, POOL_WIDTH),
                              lambda i: (jnp.minimum((i + 1) * (tm // POOL_HALO), last_halo), 0))
    return pl.pallas_call(
        body, grid=(n_tiles,),
        in_specs=[row(ATTN_WIDTH), row(KV_WIDTH), row(KV_WIDTH), row(POOL_WIDTH), halo_after, row(D_MODEL),
                  row(D_MODEL), row(D_MODEL), _resident((D_MODEL, IN_WIDTH)), _resident((1, D_MODEL))],
        out_specs=[row(D_MODEL), const((D_MODEL, IN_WIDTH)), const((1, IN_WIDTH)), const((1, D_MODEL))],
        out_shape=[jax.ShapeDtypeStruct((t, D_MODEL), F32), jax.ShapeDtypeStruct((D_MODEL, IN_WIDTH), F32),
                   jax.ShapeDtypeStruct((1, IN_WIDTH), F32), jax.ShapeDtypeStruct((1, D_MODEL), F32)],
        scratch_shapes=[pltpu.VMEM((tm + POOL_HALO, POOL_WIDTH), F32)],
        compiler_params=_params("arbitrary"), name="inproj_bwd",
    )(dq, dk, dv, dmixed, dmixed, x, h0, dx1, w_in, g_mix)


def _local_step(x, target, small, w_in, w_out, w_gate, w_up, w_down):
    t = x.shape[0]
    cos_t, sin_t = _rope_tables(t)
    w_pool = small["w_pool"].astype(BF16)
    h0, q, k, v, u = _inproj_call(x, small["g_mix"], w_in, small["b_in"], cos_t, sin_t)
    bias, bias_first = _band_biases()
    attn = _attn_fwd_call(q, k, v, small["sinks"], bias, bias_first)
    x1, h1, pool_y = _outproj_call(attn, u, x, w_pool, small["b_pool"], small["pool_scale"], w_out,
                                   small["b_out"], small["g_ffn"])
    gate, up, act, dx2, loss, g_g_final = _ffn_fwd_call(h1, x1, target, w_gate, w_up, w_down, small["g_final"])
    dgate, dup, dx1, g_g_ffn = _ffn_bwd_act_call(dx2, gate, up, x1, w_gate, w_up, w_down, small["g_ffn"])
    g_w_gate, g_w_up, g_w_down = _ffn_bwd_weights_call(h1, dgate, dup, act, dx2)
    dattn, dmixed, g_w_out, g_b_out, g_w_pool, g_b_pool, g_pool_scale = _outproj_bwd_call(
        dx1, attn, pool_y, u, w_pool, small["b_pool"], small["pool_scale"], w_out)
    dq, dk, dv, g_sinks = _attn_bwd_call(q, k, v, dattn, small["sinks"], cos_t, sin_t, bias, bias_first)
    grad_x, g_w_in, g_b_in, g_g_mix = _inproj_bwd_call(dq, dk, dv, dmixed, x, h0, dx1, w_in, small["g_mix"])
    small_grads = dict(g_mix=g_g_mix, b_in=g_b_in, sinks=g_sinks[:, 0].reshape(1, N_Q_HEADS), w_pool=g_w_pool,
                       b_pool=g_b_pool, pool_scale=g_pool_scale, b_out=g_b_out, g_ffn=g_g_ffn, g_final=g_g_final)
    big_grads = dict(w_in=g_w_in, w_out=g_w_out, w_gate=g_w_gate, w_up=g_w_up, w_down=g_w_down)
    return loss, grad_x, small_grads, big_grads


def _mesh_place():
    x, y, c = lax.axis_index("x"), lax.axis_index("y"), lax.axis_index("c")
    other_chips = [(1 - x, y), (x, 1 - y), (1 - x, 1 - y)]
    return x, y, c, other_chips


def _half(ref, c, rows):
    return ref.at[pl.ds(pl.multiple_of(c * rows, 16), rows), :]


def _cast_place_call(shards, me):
    na = len(shards)
    split = 2

    def body(me_ref, *refs):
        for a in range(na):
            refs[na + a][...] = refs[a][...].astype(BF16)

    in_specs = [pl.BlockSpec((s.shape[0] // split, s.shape[1]), lambda i, me_ref: (i, 0)) for s in shards]
    out_specs = [pl.BlockSpec((None, s.shape[0] // split, s.shape[1]), lambda i, me_ref: (me_ref[0], i, 0))
                 for s in shards]
    return pl.pallas_call(
        body,
        grid_spec=pltpu.PrefetchScalarGridSpec(num_scalar_prefetch=1, grid=(split,), in_specs=in_specs,
                                               out_specs=out_specs),
        out_shape=[jax.ShapeDtypeStruct((N_SHARDS,) + s.shape, BF16) for s in shards],
        compiler_params=_params("parallel"), name="cast_place_weights",
    )(me.reshape(1), *shards)


def _gather_call(bufs):
    na = len(bufs)

    def body(*refs):
        outs = refs[na:2 * na]
        send_sems, recv_sems = refs[2 * na:]
        x, y, c, other_chips = _mesh_place()
        me = 2 * x + y
        sibling = (x, y, 1 - c)

        def remote(idx, piece, to):
            return pltpu.make_async_remote_copy(src_ref=piece, dst_ref=piece, send_sem=send_sems.at[idx],
                                                recv_sem=recv_sems.at[idx], device_id=to, device_id_type=MESH)

        sends = []
        for j, (cx, cy) in enumerate(other_chips):
            for a in range(na):
                rows = outs[a].shape[1] // 2
                cp = remote(j * na + a, _half(outs[a].at[me], c, rows), (cx, cy, c))
                cp.start()
                sends.append(cp)
        for j, (cx, cy) in enumerate(other_chips):
            for a in range(na):
                rows = outs[a].shape[1] // 2
                piece = _half(outs[a].at[2 * cx + cy], c, rows)
                remote(j * na + a, piece, sibling).wait_recv()
                cp = remote((3 + j) * na + a, piece, sibling)
                cp.start()
                sends.append(cp)
        for j, (cx, cy) in enumerate(other_chips):
            for a in range(na):
                rows = outs[a].shape[1] // 2
                remote((3 + j) * na + a, _half(outs[a].at[2 * cx + cy], 1 - c, rows), sibling).wait_recv()
        for cp in sends:
            cp.wait_send()

    return pl.pallas_call(
        body,
        in_specs=[HBM_SPEC] * na, out_specs=[HBM_SPEC] * na,
        out_shape=[jax.ShapeDtypeStruct(b.shape, b.dtype) for b in bufs],
        input_output_aliases={a: a for a in range(na)},
        scratch_shapes=[pltpu.SemaphoreType.DMA((6 * na,)), pltpu.SemaphoreType.DMA((6 * na,))],
        compiler_params=pltpu.CompilerParams(has_side_effects=True), name="gather_weights",
    )(*bufs)


def _sibling_exchange_call(grads):
    na = len(grads)

    def body(*refs):
        ins, outs = refs[:na], refs[na:2 * na]
        send_sems, recv_sems = refs[2 * na:]
        x, y, c, _ = _mesh_place()
        sibling = (x, y, 1 - c)
        copies = []
        for a in range(na):
            rows = ins[a].shape[1] // 2
            src = ins[a].at[:, pl.ds(pl.multiple_of((1 - c) * rows, 8), rows), :]
            cp = pltpu.make_async_remote_copy(src_ref=src, dst_ref=outs[a], send_sem=send_sems.at[a],
                                              recv_sem=recv_sems.at[a], device_id=sibling, device_id_type=MESH)
            cp.start()
            copies.append(cp)
        for cp in copies:
            cp.wait_recv()
        for cp in copies:
            cp.wait_send()

    return pl.pallas_call(
        body,
        in_specs=[HBM_SPEC] * na, out_specs=[HBM_SPEC] * na,
        out_shape=[jax.ShapeDtypeStruct((g.shape[0], g.shape[1] // 2, g.shape[2]), g.dtype) for g in grads],
        scratch_shapes=[pltpu.SemaphoreType.DMA((na,)), pltpu.SemaphoreType.DMA((na,))],
        compiler_params=pltpu.CompilerParams(has_side_effects=True), name="reduce_sibling_exchange",
    )(*grads)


def _pair_sum_call(grads, landed, c, wire_dtypes):
    na = len(grads)
    split = 2

    def body(c_ref, *refs):
        ins, lands, outs = refs[:na], refs[na:2 * na], refs[2 * na:]
        for a in range(na):
            outs[a][...] = (ins[a][...] + lands[a][...]).astype(outs[a].dtype)

    in_specs, land_specs, out_specs, views = [], [], [], []
    for g in grads:
        ns, r, cols = g.shape
        rows = r // 2 // split
        views.append(g.reshape(ns, 2, r // 2, cols))
        in_specs.append(pl.BlockSpec((None, None, rows, cols), lambda s, i, c_ref: (s, c_ref[0], i, 0)))
        land_specs.append(pl.BlockSpec((None, rows, cols), lambda s, i, c_ref: (s, i, 0)))
        out_specs.append(pl.BlockSpec((None, rows, cols), lambda s, i, c_ref: (s, i, 0)))
    return pl.pallas_call(
        body,
        grid_spec=pltpu.PrefetchScalarGridSpec(num_scalar_prefetch=1, grid=(N_SHARDS, split),
                                               in_specs=in_specs + land_specs, out_specs=out_specs),
        out_shape=[jax.ShapeDtypeStruct(l.shape, dt) for l, dt in zip(landed, wire_dtypes)],
        compiler_params=_params("parallel", "parallel"), name="reduce_pair_sum",
    )(c.reshape(1), *views, *landed)


def _chip_exchange_call(partials):
    na = len(partials)

    def body(*refs):
        ins, outs = refs[:na], refs[na:2 * na]
        send_sems, recv_sems = refs[2 * na:]
        x, y, c, other_chips = _mesh_place()
        me = 2 * x + y
        sends = []
        for j, (cx, cy) in enumerate(other_chips):
            for a in range(na):
                cp = pltpu.make_async_remote_copy(
                    src_ref=ins[a].at[2 * cx + cy], dst_ref=outs[a].at[me], send_sem=send_sems.at[j * na + a],
                    recv_sem=recv_sems.at[j * na + a], device_id=(cx, cy, c), device_id_type=MESH)
                cp.start()
                sends.append(cp)
        for j, (cx, cy) in enumerate(other_chips):
            for a in range(na):
                slot = outs[a].at[2 * cx + cy]
                pltpu.make_async_remote_copy(
                    src_ref=slot, dst_ref=slot, send_sem=send_sems.at[j * na + a],
                    recv_sem=recv_sems.at[j * na + a], device_id=(cx, cy, c), device_id_type=MESH).wait_recv()
        for cp in sends:
            cp.wait_send()

    return pl.pallas_call(
        body,
        in_specs=[HBM_SPEC] * na, out_specs=[HBM_SPEC] * na,
        out_shape=[jax.ShapeDtypeStruct(p.shape, p.dtype) for p in partials],
        scratch_shapes=[pltpu.SemaphoreType.DMA((3 * na,)), pltpu.SemaphoreType.DMA((3 * na,))],
        compiler_params=pltpu.CompilerParams(has_side_effects=True), name="reduce_chip_exchange",
    )(*partials)


def _chip_sum_call(partials, slots, me, c):
    na = len(slots)
    split = 2

    def body(place_ref, *refs):
        mine, lands, outs = refs[:na], refs[na:2 * na], refs[2 * na:]
        for j in range(N_SHARDS):
            @pl.when(place_ref[0] == j)
            def _():
                for a in range(na):
                    terms = [mine[a][...] if k == j else lands[a][k] for k in range(N_SHARDS)]
                    t0, t1, t2, t3 = (term.astype(F32) for term in terms)
                    outs[a][...] = ((t0 + t1) + t2) + t3

    mine_specs, land_specs, out_specs = [], [], []
    for s in slots:
        ns, h, cols = s.shape
        mine_specs.append(pl.BlockSpec((None, h // split, cols), lambda i, place: (place[0], i, 0)))
        land_specs.append(pl.BlockSpec((ns, h // split, cols), lambda i, place: (0, i, 0)))
        out_specs.append(pl.BlockSpec((h // split, cols), lambda i, place: (place[1] * split + i, 0)))
    return pl.pallas_call(
        body,
        grid_spec=pltpu.PrefetchScalarGridSpec(num_scalar_prefetch=1, grid=(split,),
                                               in_specs=mine_specs + land_specs, out_specs=out_specs),
        out_shape=[jax.ShapeDtypeStruct((2 * s.shape[1], s.shape[2]), F32) for s in slots],
        compiler_params=_params("parallel"), name="reduce_chip_sum",
    )(jnp.stack([me, c]), *partials, *slots)


def _sibling_allgather_call(bufs):
    na = len(bufs)

    def body(*refs):
        outs = refs[na:2 * na]
        send_sems, recv_sems = refs[2 * na:]
        x, y, c, _ = _mesh_place()
        sibling = (x, y, 1 - c)

        def remote(a, half):
            rows = outs[a].shape[0] // 2
            piece = outs[a].at[pl.ds(pl.multiple_of(half * rows, 8), rows), :]
            return pltpu.make_async_remote_copy(src_ref=piece, dst_ref=piece, send_sem=send_sems.at[a],
                                                recv_sem=recv_sems.at[a], device_id=sibling, device_id_type=MESH)

        sends = [remote(a, c) for a in range(na)]
        for cp in sends:
            cp.start()
        for a in range(na):
            remote(a, 1 - c).wait_recv()
        for cp in sends:
            cp.wait_send()

    return pl.pallas_call(
        body,
        in_specs=[HBM_SPEC] * na, out_specs=[HBM_SPEC] * na,
        out_shape=[jax.ShapeDtypeStruct(b.shape, b.dtype) for b in bufs],
        input_output_aliases={a: a for a in range(na)},
        scratch_shapes=[pltpu.SemaphoreType.DMA((na,)), pltpu.SemaphoreType.DMA((na,))],
        compiler_params=pltpu.CompilerParams(has_side_effects=True), name="reduce_sibling_allgather",
    )(*bufs)


def _adamw_call(ws, gs, ms, vs):
    na = len(ws)
    split = 4

    def body(*refs):
        w_refs, g_refs, m_refs, v_refs = (refs[k * na:(k + 1) * na] for k in range(4))
        d_refs, nm_refs, nv_refs = (refs[(4 + k) * na:(5 + k) * na] for k in range(3))
        for a in range(na):
            g = g_refs[a][...]
            m = ADAM_B1 * m_refs[a][...] + (1.0 - ADAM_B1) * g
            v = ADAM_B2 * v_refs[a][...] + (1.0 - ADAM_B2) * jnp.square(g)
            m_hat = m / (1.0 - ADAM_B1 ** ADAM_STEP)
            v_hat = v / (1.0 - ADAM_B2 ** ADAM_STEP)
            d_refs[a][...] = -ADAM_LR * (m_hat / (jnp.sqrt(v_hat) + ADAM_EPS) + ADAM_WD * w_refs[a][...])
            nm_refs[a][...] = m
            nv_refs[a][...] = v

    specs = [pl.BlockSpec((w.shape[0] // split, w.shape[1]), lambda i: (i, 0)) for w in ws]
    shapes = [jax.ShapeDtypeStruct(w.shape, F32) for w in ws]
    outs = pl.pallas_call(
        body, grid=(split,), in_specs=specs * 4, out_specs=specs * 3, out_shape=shapes * 3,
        compiler_params=_params("parallel"), name="adamw",
    )(*ws, *gs, *ms, *vs)
    return outs[:na], outs[na:2 * na], outs[2 * na:]


def _small_layout(shapes):
    layout, off = {}, 0
    for name in SMALL_NAMES:
        size = 1
        for d in shapes[name]:
            size *= d
        layout[name] = (off, size)
        off += size
    rows = -(-off // (LANES * 64)) * 64
    return layout, rows


def _pack_small(arrays, rows):
    flat = jnp.concatenate([arrays[name].reshape(-1) for name in SMALL_NAMES])
    return jnp.pad(flat, (0, rows * LANES - flat.shape[0])).reshape(rows, LANES)


def _unpack_small(packed, layout, shapes):
    flat = packed.reshape(-1)
    return {name: flat[off:off + size].reshape(shapes[name]) for name, (off, size) in layout.items()}


def kernel(x, g_mix, w_in, b_in, sinks, w_pool, b_pool, pool_scale, w_out, b_out, g_ffn, w_gate, w_up, w_down, g_final, loss_target, m_g_mix, m_w_in, m_b_in, m_sinks, m_w_pool, m_b_pool, m_pool_scale, m_w_out, m_b_out, m_g_ffn, m_w_gate, m_w_up, m_w_down, m_g_final, v_g_mix, v_w_in, v_b_in, v_sinks, v_w_pool, v_b_pool, v_pool_scale, v_w_out, v_b_out, v_g_ffn, v_w_gate, v_w_up, v_w_down, v_g_final):
    given = dict(locals())
    big_names = ("w_in", "w_out", "w_gate", "w_up", "w_down")
    out_shapes = {n: given[n].shape for n in SMALL_NAMES + big_names}
    t = x.shape[1]
    c = lax.axis_index("c")

    me = 2 * lax.axis_index("x") + lax.axis_index("y")
    g_in, g_out, g_gate, g_up, g_down = _gather_call(_cast_place_call([given[n][0] for n in big_names], me))
    w_in_full = jnp.transpose(g_in, (1, 0, 2)).reshape(D_MODEL, IN_WIDTH)
    w_out_full = g_out.reshape(D_MODEL, D_MODEL)

    small = dict(g_mix=g_mix, b_in=b_in, sinks=sinks, w_pool=w_pool[0], b_pool=b_pool.reshape(1, POOL_WIDTH),
                 pool_scale=pool_scale.reshape(1, POOL_WIDTH), b_out=b_out, g_ffn=g_ffn,
                 g_final=g_final.reshape(1, D_MODEL))
    loss, grad_x, small_grads, big_grads = _local_step(
        x[0], loss_target[0], small, w_in_full, w_out_full, g_gate, g_up, g_down)
    loss = lax.psum(loss[0, 0], ("x", "y", "c"))

    layout, small_rows = _small_layout(out_shapes)
    packed = _pack_small(small_grads, small_rows)
    n_in = IN_WIDTH // N_SHARDS
    grads = [
        jnp.transpose(big_grads["w_in"].reshape(D_MODEL, N_SHARDS, n_in), (1, 0, 2)),
        big_grads["w_out"].reshape(N_SHARDS, D_MODEL // N_SHARDS, D_MODEL),
        big_grads["w_gate"], big_grads["w_up"], big_grads["w_down"],
        jnp.broadcast_to(packed[None], (N_SHARDS,) + packed.shape),
    ]
    landed = _sibling_exchange_call(grads)
    partials = _pair_sum_call(grads, landed, c, [BF16] * len(big_names) + [F32])
    slots = _chip_exchange_call(partials)
    totals = _sibling_allgather_call(_chip_sum_call(partials, slots, me, c))

    ws = [given[n][0] for n in big_names] + [_pack_small({n: given[n] for n in SMALL_NAMES}, small_rows)]
    ms = [given["m_" + n][0] for n in big_names] + [_pack_small({n: given["m_" + n] for n in SMALL_NAMES}, small_rows)]
    vs = [given["v_" + n][0] for n in big_names] + [_pack_small({n: given["v_" + n] for n in SMALL_NAMES}, small_rows)]
    deltas, new_ms, new_vs = _adamw_call(ws, totals, ms, vs)

    def unpack(arrays):
        out = _unpack_small(arrays[-1], layout, out_shapes)
        for n, a in zip(big_names, arrays[:-1]):
            out[n] = a.reshape(out_shapes[n])
        return out

    order = ("g_mix", "w_in", "b_in", "sinks", "w_pool", "b_pool", "pool_scale", "w_out", "b_out", "g_ffn",
             "w_gate", "w_up", "w_down", "g_final")
    results = [loss, grad_x.reshape(x.shape)]
    for group in (totals, deltas, new_ms, new_vs):
        named = unpack(list(group))
        results += [named[n] for n in order]
    return tuple(results)
```

```python
import functools

import jax
import jax.numpy as jnp
from jax import lax
from jax.experimental import pallas as pl
from jax.experimental.pallas import tpu as pltpu

F32 = jnp.float32
BF16 = jnp.bfloat16

D_MODEL = 1024
ATTN_WIDTH = 512
KV_WIDTH = 128
POOL_WIDTH = 512
IN_WIDTH = ATTN_WIDTH + 2 * KV_WIDTH + POOL_WIDTH
HEAD_DIM = 64
N_Q_HEADS = 8
BLOCK = 128
POOL_SIZES = (2, 4, 8, 16)
POOL_HALO = 16
ROPE_THETA = 10000.0
RMS_EPS = 1e-5
Q_SCALE = HEAD_DIM ** -0.5

ADAM_LR = 0.001
ADAM_B1 = 0.9
ADAM_B2 = 0.999
ADAM_EPS = 1e-08
ADAM_WD = 0.01
ADAM_STEP = 10

N_SHARDS = 4
LANES = 128
VMEM_LIMIT = 56 * 1024 * 1024

MESH = pl.DeviceIdType.MESH
HBM_SPEC = pl.BlockSpec(memory_space=pltpu.HBM)

SMALL_NAMES = ("g_mix", "b_in", "sinks", "w_pool", "b_pool", "pool_scale", "b_out", "g_ffn", "g_final")


def _dot(a, b):
    return jnp.dot(a, b, preferred_element_type=F32)


def _dot_nt(a, b):
    return lax.dot_general(a, b, (((1,), (1,)), ((), ())), preferred_element_type=F32)


def _dot_tn(a, b):
    return lax.dot_general(a, b, (((0,), (0,)), ((), ())), preferred_element_type=F32)


def _params(*semantics):
    return pltpu.CompilerParams(dimension_semantics=semantics, vmem_limit_bytes=VMEM_LIMIT)


def _resident(shape):
    return pl.BlockSpec(shape, lambda *_: (0,) * len(shape), pipeline_mode=pl.Buffered(1))


def _token_tile(t, largest=512):
    for tm in (1024, 512, 256, 128):
        if tm <= largest and t % tm == 0:
            return tm
    raise ValueError(f"sequence length {t} is not a multiple of 128")


def _rms_stats(xf):
    r = lax.rsqrt(jnp.mean(xf * xf, axis=-1, keepdims=True) + RMS_EPS)
    return xf * r, r


def _rms_bwd(dy, xh, r, g):
    dxh = dy * g
    return r * (dxh - xh * jnp.mean(dxh * xh, axis=-1, keepdims=True))


def _rope_partner(t, first):
    return jnp.where(first, pltpu.roll(t, LANES - HEAD_DIM // 2, 1), pltpu.roll(t, HEAD_DIM // 2, 1))


def _first_half_mask(shape):
    lane = lax.broadcasted_iota(jnp.int32, shape, 1)
    return (lane % HEAD_DIM) < (HEAD_DIM // 2)


def _rope_tables(t):
    inv_freq = 1.0 / (ROPE_THETA ** (jnp.arange(0, HEAD_DIM, 2, dtype=F32) / HEAD_DIM))
    ang = jnp.arange(t, dtype=F32)[:, None] * inv_freq[None, :]
    cos, sin = jnp.cos(ang), jnp.sin(ang)
    cos_t = jnp.tile(jnp.concatenate([cos, cos], axis=-1), (1, LANES // HEAD_DIM))
    sin_t = jnp.tile(jnp.concatenate([-sin, sin], axis=-1), (1, LANES // HEAD_DIM))
    return cos_t, sin_t


def _inproj_call(x, g_mix, w_in, b_in, cos_t, sin_t):
    t = x.shape[0]
    tm = _token_tile(t)

    def body(x_ref, g_ref, w_ref, b_ref, c_ref, s_ref, h_ref, q_ref, k_ref, v_ref, u_ref):
        xh, _ = _rms_stats(x_ref[...])
        h = (xh * g_ref[...]).astype(BF16)
        h_ref[...] = h
        z = _dot_nt(h, w_ref[...]) + b_ref[...]
        cos, sin = c_ref[...], s_ref[...]
        first = _first_half_mask((tm, LANES))
        for j in range(ATTN_WIDTH // LANES):
            zj = z[:, LANES * j:LANES * (j + 1)]
            q_ref[:, LANES * j:LANES * (j + 1)] = ((zj * cos + _rope_partner(zj, first) * sin) * Q_SCALE).astype(BF16)
        zk = z[:, ATTN_WIDTH:ATTN_WIDTH + KV_WIDTH]
        k_ref[...] = (zk * cos + _rope_partner(zk, first) * sin).astype(BF16)
        v_ref[...] = z[:, ATTN_WIDTH + KV_WIDTH:ATTN_WIDTH + 2 * KV_WIDTH].astype(BF16)
        u_ref[...] = z[:, ATTN_WIDTH + 2 * KV_WIDTH:]

    row = lambda w: pl.BlockSpec((tm, w), lambda i: (i, 0))
    return pl.pallas_call(
        body, grid=(t // tm,),
        in_specs=[row(D_MODEL), _resident((1, D_MODEL)), _resident((IN_WIDTH, D_MODEL)), _resident((1, IN_WIDTH)),
                  row(LANES), row(LANES)],
        out_specs=[row(D_MODEL), row(ATTN_WIDTH), row(KV_WIDTH), row(KV_WIDTH), row(POOL_WIDTH)],
        out_shape=[jax.ShapeDtypeStruct((t, D_MODEL), BF16), jax.ShapeDtypeStruct((t, ATTN_WIDTH), BF16),
                   jax.ShapeDtypeStruct((t, KV_WIDTH), BF16), jax.ShapeDtypeStruct((t, KV_WIDTH), BF16),
                   jax.ShapeDtypeStruct((t, POOL_WIDTH), F32)],
        compiler_params=_params("parallel"), name="inproj_fwd",
    )(x, g_mix, w_in, b_in, cos_t, sin_t)


def _head_in_both_halves(band, kv_head):
    lane = lax.broadcasted_iota(jnp.int32, band.shape, 1)
    own = (lane < HEAD_DIM) if kv_head == 0 else (lane >= HEAD_DIM)
    return jnp.where(own, band, pltpu.roll(band, HEAD_DIM, 1)).astype(BF16)


def _stack_heads(ref, kv_head, row0):
    low = lax.broadcasted_iota(jnp.int32, (BLOCK, LANES), 1) < HEAD_DIM
    parts = []
    for jj in range(2):
        slab = ref[row0:row0 + BLOCK, LANES * (2 * kv_head + jj):LANES * (2 * kv_head + jj + 1)]
        zero = jnp.zeros_like(slab)
        parts += [jnp.where(low, slab, zero), jnp.where(low, zero, slab)]
    return jnp.concatenate(parts, axis=0)


def _band_biases():
    c = lax.broadcasted_iota(jnp.int32, (2 * BLOCK, 4 * BLOCK), 0)
    i = lax.broadcasted_iota(jnp.int32, (2 * BLOCK, 4 * BLOCK), 1) % BLOCK
    vis = (c > i) & (c <= i + BLOCK)
    return jnp.where(vis, 0.0, -jnp.inf).astype(F32), jnp.where(vis & (c >= BLOCK), 0.0, -jnp.inf).astype(F32)


def _blocks_per_step(nb):
    for g in (4, 2, 1):
        if nb % g == 0:
            return g


def _sink_row(sink_ref, kv_head):
    return jnp.concatenate([jnp.full((1, BLOCK), sink_ref[0, 4 * kv_head + g], F32) for g in range(4)], axis=1)


def _kv_band(cur_ref, prev_ref, g):
    r0 = g * BLOCK
    prev = prev_ref[...] if g == 0 else cur_ref[r0 - BLOCK:r0, :]
    return jnp.concatenate([prev, cur_ref[r0:r0 + BLOCK, :]], axis=0).astype(F32)


def _attn_fwd_call(q, k, v, sinks, bias, bias_first):
    t = q.shape[0]
    nb = t // BLOCK
    gb = _blocks_per_step(nb)
    qb = gb * BLOCK

    def body(sink_ref, q_ref, kc_ref, kp_ref, vc_ref, vp_ref, b_ref, bf_ref, o_ref):
        n = pl.program_id(0)
        row_low = lax.broadcasted_iota(jnp.int32, (LANES, BLOCK), 0) < HEAD_DIM
        for g in range(gb):
            r0 = g * BLOCK
            kband = _kv_band(kc_ref, kp_ref, g)
            vband_t = _kv_band(vc_ref, vp_ref, g).T
            bias_g = jnp.where(n == 0, bf_ref[...], b_ref[...]) if g == 0 else b_ref[...]
            for kh in range(2):
                k2 = _head_in_both_halves(kband, kh)
                vt_head = vband_t[HEAD_DIM * kh:HEAD_DIM * (kh + 1), :]
                v2_t = jnp.concatenate([vt_head, vt_head], axis=0).astype(BF16)
                s_t = _dot_nt(k2, _stack_heads(q_ref, kh, r0)) + bias_g
                sink = _sink_row(sink_ref, kh)
                m = jnp.maximum(jnp.max(s_t, axis=0, keepdims=True), sink)
                e = jnp.exp(s_t - m)
                inv = 1.0 / (jnp.sum(e, axis=0, keepdims=True) + jnp.exp(sink - m))
                o_t = _dot(v2_t, (e * inv).astype(BF16))
                for jj in range(2):
                    slab = jnp.where(row_low, o_t[:, BLOCK * 2 * jj:BLOCK * (2 * jj + 1)],
                                     o_t[:, BLOCK * (2 * jj + 1):BLOCK * (2 * jj + 2)]).T
                    o_ref[r0:r0 + BLOCK, LANES * (2 * kh + jj):LANES * (2 * kh + jj + 1)] = slab.astype(BF16)

    cur = lambda w: pl.BlockSpec((qb, w), lambda n: (n, 0))
    prev = lambda w: pl.BlockSpec((BLOCK, w), lambda n: (jnp.maximum(n * gb - 1, 0), 0))
    return pl.pallas_call(
        body, grid=(nb // gb,),
        in_specs=[pl.BlockSpec(memory_space=pltpu.SMEM), cur(ATTN_WIDTH), cur(KV_WIDTH), prev(KV_WIDTH),
                  cur(KV_WIDTH), prev(KV_WIDTH), _resident(bias.shape), _resident(bias.shape)],
        out_specs=cur(ATTN_WIDTH),
        out_shape=jax.ShapeDtypeStruct((t, ATTN_WIDTH), BF16),
        compiler_params=_params("parallel"), name="attn_fwd",
    )(sinks, q, k, k, v, v, bias, bias_first)


def _pool_mixed(sc_ref, u_ref, halo_ref, i, tm):
    sc_ref[0:POOL_HALO, :] = jnp.where(i == 0, 0.0, halo_ref[...])
    sc_ref[POOL_HALO:, :] = u_ref[...]
    tok = i * tm + lax.broadcasted_iota(jnp.int32, (tm, 1), 0)
    mixed = []
    for g, size in enumerate(POOL_SIZES):
        lanes = slice(LANES * g, LANES * (g + 1))
        cur = sc_ref[pl.ds(POOL_HALO, tm), lanes]
        acc = cur
        for j in range(1, size):
            acc = acc + sc_ref[pl.ds(POOL_HALO - j, tm), lanes]
        count = jnp.minimum(tok + 1, size).astype(F32)
        mixed.append(acc / count - cur)
    return mixed


def _halo_before(tm, width):
    return pl.BlockSpec((POOL_HALO, width), lambda i: (jnp.maximum(i * (tm // POOL_HALO) - 1, 0), 0))


def _outproj_call(attn, u, x, w_pool, b_pool, pool_scale, w_out, b_out, g_ffn):
    t = x.shape[0]
    tm = _token_tile(t)

    def body(a_ref, u_ref, halo_ref, x_ref, wp_ref, bp_ref, ps_ref, wo_ref, bo_ref, g_ref,
             x1_ref, h1_ref, y_ref, sc_ref):
        i = pl.program_id(0)
        for g, mixed in enumerate(_pool_mixed(sc_ref, u_ref, halo_ref, i, tm)):
            lanes = slice(LANES * g, LANES * (g + 1))
            lin = _dot(mixed.astype(BF16), wp_ref[g]) + bp_ref[:, lanes]
            y_ref[:, lanes] = (lin * ps_ref[:, lanes]).astype(BF16)
        x1 = (x_ref[...] + _dot(a_ref[...], wo_ref[0:ATTN_WIDTH, :]) + _dot(y_ref[...], wo_ref[ATTN_WIDTH:, :])
              + bo_ref[...])
        x1_ref[...] = x1
        xh, _ = _rms_stats(x1)
        h1_ref[...] = (xh * g_ref[...]).astype(BF16)

    row = lambda w: pl.BlockSpec((tm, w), lambda i: (i, 0))
    return pl.pallas_call(
        body, grid=(t // tm,),
        in_specs=[row(ATTN_WIDTH), row(POOL_WIDTH), _halo_before(tm, POOL_WIDTH), row(D_MODEL),
                  _resident((4, LANES, LANES)), _resident((1, POOL_WIDTH)), _resident((1, POOL_WIDTH)),
                  _resident((D_MODEL, D_MODEL)), _resident((1, D_MODEL)), _resident((1, D_MODEL))],
        out_specs=[row(D_MODEL), row(D_MODEL), row(POOL_WIDTH)],
        out_shape=[jax.ShapeDtypeStruct((t, D_MODEL), F32), jax.ShapeDtypeStruct((t, D_MODEL), BF16),
                   jax.ShapeDtypeStruct((t, POOL_WIDTH), BF16)],
        scratch_shapes=[pltpu.VMEM((tm + POOL_HALO, POOL_WIDTH), F32)],
        compiler_params=_params("parallel"), name="outproj_fwd",
    )(attn, u, u, x, w_pool, b_pool, pool_scale, w_out, b_out, g_ffn)


def _ffn_fwd_call(h1, x1, target, w_gate, w_up, w_down, g_final):
    t = x1.shape[0]
    tm = _token_tile(t)
    ns, fs, _ = w_gate.shape

    def body(h_ref, x1_ref, tgt_ref, wg_ref, wu_ref, wd_ref, gf_ref,
             gate_ref, up_ref, act_ref, dx2_ref, loss_ref, dgf_ref, acc_ref):
        i, s = pl.program_id(0), pl.program_id(1)

        @pl.when(s == 0)
        def _():
            acc_ref[...] = x1_ref[...]

        @pl.when((i == 0) & (s == 0))
        def _():
            loss_ref[...] = jnp.zeros_like(loss_ref)
            dgf_ref[...] = jnp.zeros_like(dgf_ref)

        h = h_ref[...]
        gate = _dot_nt(h, wg_ref[...])
        up = _dot_nt(h, wu_ref[...])
        act = (gate * jax.nn.sigmoid(gate) * up).astype(BF16)
        gate_ref[...] = gate.astype(BF16)
        up_ref[...] = up.astype(BF16)
        act_ref[...] = act
        acc_ref[...] += _dot(act, wd_ref[...])

        @pl.when(s == ns - 1)
        def _():
            xh, r = _rms_stats(acc_ref[...])
            gf = gf_ref[...]
            err = xh * gf - tgt_ref[...]
            loss_ref[...] += 0.5 * jnp.sum(err * err) / D_MODEL
            dy = err / D_MODEL
            dgf_ref[...] += jnp.sum(dy * xh, axis=0, keepdims=True)
            dx2_ref[...] = _rms_bwd(dy, xh, r, gf)

    row = lambda w: pl.BlockSpec((tm, w), lambda i, s: (i, 0))
    shard_act = pl.BlockSpec((None, tm, fs), lambda i, s: (s, i, 0))
    act_shape = jax.ShapeDtypeStruct((ns, t, fs), BF16)
    return pl.pallas_call(
        body, grid=(t // tm, ns),
        in_specs=[row(D_MODEL), row(D_MODEL), row(D_MODEL),
                  pl.BlockSpec((None, fs, D_MODEL), lambda i, s: (s, 0, 0)),
                  pl.BlockSpec((None, fs, D_MODEL), lambda i, s: (s, 0, 0)),
                  pl.BlockSpec((None, fs, D_MODEL), lambda i, s: (s, 0, 0)),
                  _resident((1, D_MODEL))],
        out_specs=[shard_act, shard_act, shard_act, row(D_MODEL),
                   pl.BlockSpec((1, LANES), lambda i, s: (0, 0)), pl.BlockSpec((1, D_MODEL), lambda i, s: (0, 0))],
        out_shape=[act_shape, act_shape, act_shape, jax.ShapeDtypeStruct((t, D_MODEL), F32),
                   jax.ShapeDtypeStruct((1, LANES), F32), jax.ShapeDtypeStruct((1, D_MODEL), F32)],
        scratch_shapes=[pltpu.VMEM((tm, D_MODEL), F32)],
        compiler_params=_params("arbitrary", "arbitrary"), name="ffn_fwd",
    )(h1, x1, target, w_gate, w_up, w_down, g_final)


def _ffn_bwd_act_call(dx2, gate, up, x1, w_gate, w_up, w_down, g_ffn):
    t = x1.shape[0]
    tm = _token_tile(t)
    ns, fs, _ = w_gate.shape

    def body(dx2_ref, gate_ref, up_ref, x1_ref, wg_ref, wu_ref, wd_ref, g_ref,
             dgate_ref, dup_ref, dx1_ref, dg_ref, acc_ref, dx2b_ref):
        i, s = pl.program_id(0), pl.program_id(1)

        @pl.when(s == 0)
        def _():
            acc_ref[...] = jnp.zeros_like(acc_ref)
            dx2b_ref[...] = dx2_ref[...].astype(BF16)

        @pl.when((i == 0) & (s == 0))
        def _():
            dg_ref[...] = jnp.zeros_like(dg_ref)

        dact = _dot_nt(dx2b_ref[...], wd_ref[...])
        gate = gate_ref[...].astype(F32)
        upv = up_ref[...].astype(F32)
        sg = jax.nn.sigmoid(gate)
        dup = (dact * (gate * sg)).astype(BF16)
        dgate = (dact * upv * (sg * (1.0 + gate * (1.0 - sg)))).astype(BF16)
        dgate_ref[...] = dgate
        dup_ref[...] = dup
        acc_ref[...] += _dot(dgate, wg_ref[...]) + _dot(dup, wu_ref[...])

        @pl.when(s == ns - 1)
        def _():
            xh, r = _rms_stats(x1_ref[...])
            dh = acc_ref[...]
            dg_ref[...] += jnp.sum(dh * xh, axis=0, keepdims=True)
            dx1_ref[...] = dx2_ref[...] + _rms_bwd(dh, xh, r, g_ref[...])

    row = lambda w: pl.BlockSpec((tm, w), lambda i, s: (i, 0))
    shard_act = pl.BlockSpec((None, tm, fs), lambda i, s: (s, i, 0))
    act_shape = jax.ShapeDtypeStruct((ns, t, fs), BF16)
    return pl.pallas_call(
        body, grid=(t // tm, ns),
        in_specs=[row(D_MODEL), shard_act, shard_act, row(D_MODEL),
                  pl.BlockSpec((None, fs, D_MODEL), lambda i, s: (s, 0, 0)),
                  pl.BlockSpec((None, fs, D_MODEL), lambda i, s: (s, 0, 0)),
                  pl.BlockSpec((None, fs, D_MODEL), lambda i, s: (s, 0, 0)),
                  _resident((1, D_MODEL))],
        out_specs=[shard_act, shard_act, row(D_MODEL), pl.BlockSpec((1, D_MODEL), lambda i, s: (0, 0))],
        out_shape=[act_shape, act_shape, jax.ShapeDtypeStruct((t, D_MODEL), F32),
                   jax.ShapeDtypeStruct((1, D_MODEL), F32)],
        scratch_shapes=[pltpu.VMEM((tm, D_MODEL), F32), pltpu.VMEM((tm, D_MODEL), BF16)],
        compiler_params=_params("arbitrary", "arbitrary"), name="ffn_bwd_act",
    )(dx2, gate, up, x1, w_gate, w_up, w_down, g_ffn)


def _ffn_bwd_weights_call(h1, dgate, dup, act, dx2):
    t = h1.shape[0]
    tm = _token_tile(t, 1024)
    ns, _, fs = dgate.shape

    def body(h_ref, dgate_ref, dup_ref, act_ref, dx2_ref, gg_ref, gu_ref, gd_ref):
        @pl.when(pl.program_id(1) == 0)
        def _():
            gg_ref[...] = jnp.zeros_like(gg_ref)
            gu_ref[...] = jnp.zeros_like(gu_ref)
            gd_ref[...] = jnp.zeros_like(gd_ref)

        h = h_ref[...]
        gg_ref[...] += _dot_tn(dgate_ref[...], h)
        gu_ref[...] += _dot_tn(dup_ref[...], h)
        gd_ref[...] += _dot_tn(act_ref[...], dx2_ref[...].astype(BF16))

    row = lambda w: pl.BlockSpec((tm, w), lambda s, i: (i, 0))
    shard_act = pl.BlockSpec((None, tm, fs), lambda s, i: (s, i, 0))
    return pl.pallas_call(
        body, grid=(ns, t // tm),
        in_specs=[row(D_MODEL), shard_act, shard_act, shard_act, row(D_MODEL)],
        out_specs=[pl.BlockSpec((None, fs, D_MODEL), lambda s, i: (s, 0, 0))] * 3,
        out_shape=[jax.ShapeDtypeStruct((ns, fs, D_MODEL), F32)] * 3,
        compiler_params=_params("parallel", "arbitrary"), name="ffn_bwd_weights",
    )(h1, dgate, dup, act, dx2)


def _outproj_bwd_call(dx1, attn, pool_y, u, w_pool, b_pool, pool_scale, w_out):
    t = dx1.shape[0]
    tm = _token_tile(t)

    def body(dx1_ref, a_ref, y_ref, u_ref, halo_ref, wp_ref, bp_ref, ps_ref, wo_ref,
             dattn_ref, dmixed_ref, gwo_ref, gbo_ref, gwp_ref, gbp_ref, gps_ref, sc_ref):
        i = pl.program_id(0)

        @pl.when(i == 0)
        def _():
            for ref in (gwo_ref, gbo_ref, gwp_ref, gbp_ref, gps_ref):
                ref[...] = jnp.zeros_like(ref)

        dx1 = dx1_ref[...]
        dx1b = dx1.astype(BF16)
        dcat = _dot_nt(dx1b, wo_ref[...])
        dattn_ref[...] = dcat[:, :ATTN_WIDTH].astype(BF16)
        for g, mixed in enumerate(_pool_mixed(sc_ref, u_ref, halo_ref, i, tm)):
            lanes = slice(LANES * g, LANES * (g + 1))
            mixed_b = mixed.astype(BF16)
            lin = _dot(mixed_b, wp_ref[g]) + bp_ref[:, lanes]
            dpool = dcat[:, ATTN_WIDTH + LANES * g:ATTN_WIDTH + LANES * (g + 1)]
            gps_ref[:, lanes] += jnp.sum(dpool * lin, axis=0, keepdims=True)
            dy = dpool * ps_ref[:, lanes]
            gbp_ref[:, lanes] += jnp.sum(dy, axis=0, keepdims=True)
            dy_b = dy.astype(BF16)
            gwp_ref[g] += _dot_tn(mixed_b, dy_b)
            dmixed_ref[:, lanes] = _dot_nt(dy_b, wp_ref[g])
        gwo_ref[0:ATTN_WIDTH, :] += _dot_tn(a_ref[...], dx1b)
        gwo_ref[ATTN_WIDTH:, :] += _dot_tn(y_ref[...], dx1b)
        gbo_ref[...] += jnp.sum(dx1, axis=0, keepdims=True)

    row = lambda w: pl.BlockSpec((tm, w), lambda i: (i, 0))
    const = lambda shape: pl.BlockSpec(shape, lambda i: (0,) * len(shape))
    return pl.pallas_call(
        body, grid=(t // tm,),
        in_specs=[row(D_MODEL), row(ATTN_WIDTH), row(POOL_WIDTH), row(POOL_WIDTH), _halo_before(tm, POOL_WIDTH),
                  _resident((4, LANES, LANES)), _resident((1, POOL_WIDTH)), _resident((1, POOL_WIDTH)),
                  _resident((D_MODEL, D_MODEL))],
        out_specs=[row(ATTN_WIDTH), row(POOL_WIDTH), const((D_MODEL, D_MODEL)), const((1, D_MODEL)),
                   const((4, LANES, LANES)), const((1, POOL_WIDTH)), const((1, POOL_WIDTH))],
        out_shape=[jax.ShapeDtypeStruct((t, ATTN_WIDTH), BF16), jax.ShapeDtypeStruct((t, POOL_WIDTH), F32),
                   jax.ShapeDtypeStruct((D_MODEL, D_MODEL), F32), jax.ShapeDtypeStruct((1, D_MODEL), F32),
                   jax.ShapeDtypeStruct((4, LANES, LANES), F32), jax.ShapeDtypeStruct((1, POOL_WIDTH), F32),
                   jax.ShapeDtypeStruct((1, POOL_WIDTH), F32)],
        scratch_shapes=[pltpu.VMEM((tm + POOL_HALO, POOL_WIDTH), F32)],
        compiler_params=_params("arbitrary"), name="outproj_bwd",
    )(dx1, attn, pool_y, u, u, w_pool, b_pool, pool_scale, w_out)


def _attn_bwd_call(q, k, v, dout, sinks, cos_t, sin_t, bias, bias_first):
    t = q.shape[0]
    nb = t // BLOCK
    gb = _blocks_per_step(nb)
    qb = gb * BLOCK
    steps = nb // gb

    def body(sink_ref, q_ref, kc_ref, kp_ref, vc_ref, vp_ref, do_ref, cq_ref, sq_ref, ck_ref, sk_ref, b_ref, bf_ref,
             dq_ref, dk_ref, dv_ref, dsink_ref, newk_ref, newv_ref, carryk_ref, carryv_ref, headk_ref, headv_ref):
        n = pl.program_id(0)
        first = _first_half_mask((BLOCK, LANES))

        @pl.when(n == 0)
        def _():
            dsink_ref[...] = jnp.zeros_like(dsink_ref)

        @pl.when(n < steps)
        def _():
            head_row = lax.broadcasted_iota(jnp.int32, dsink_ref.shape, 0)
            lane2 = lax.broadcasted_iota(jnp.int32, (2 * BLOCK, LANES), 1)
            row_low = lax.broadcasted_iota(jnp.int32, (LANES, BLOCK), 0) < HEAD_DIM
            dsink = jnp.zeros(dsink_ref.shape, F32)
            for g in range(gb):
                r0 = g * BLOCK
                kband = _kv_band(kc_ref, kp_ref, g)
                vband = _kv_band(vc_ref, vp_ref, g)
                kband_t = kband.T
                bias_g = jnp.where(n == 0, bf_ref[...], b_ref[...]) if g == 0 else b_ref[...]
                cq, sq = cq_ref[r0:r0 + BLOCK, :], sq_ref[r0:r0 + BLOCK, :]
                dk_heads, dv_heads = [], []
                for kh in range(2):
                    k2 = _head_in_both_halves(kband, kh)
                    v2 = _head_in_both_halves(vband, kh)
                    kt_head = kband_t[HEAD_DIM * kh:HEAD_DIM * (kh + 1), :]
                    k2_t = jnp.concatenate([kt_head, kt_head], axis=0).astype(BF16)
                    qst = _stack_heads(q_ref, kh, r0)
                    dost = _stack_heads(do_ref, kh, r0)
                    s_t = _dot_nt(k2, qst) + bias_g
                    sink = _sink_row(sink_ref, kh)
                    m = jnp.maximum(jnp.max(s_t, axis=0, keepdims=True), sink)
                    e = jnp.exp(s_t - m)
                    e_sink = jnp.exp(sink - m)
                    inv = 1.0 / (jnp.sum(e, axis=0, keepdims=True) + e_sink)
                    p_t = e * inv
                    dp_t = _dot_nt(v2, dost)
                    delta = jnp.sum(p_t * dp_t, axis=0, keepdims=True)
                    ds_t = (p_t * (dp_t - delta)).astype(BF16)
                    sink_term = e_sink * inv * delta
                    for h in range(4):
                        val = -jnp.sum(sink_term[:, BLOCK * h:BLOCK * (h + 1)])
                        dsink = dsink + jnp.where(head_row == 4 * kh + h, val, 0.0)
                    dq_t = _dot(k2_t, ds_t)
                    for jj in range(2):
                        slab = jnp.where(row_low, dq_t[:, BLOCK * 2 * jj:BLOCK * (2 * jj + 1)],
                                         dq_t[:, BLOCK * (2 * jj + 1):BLOCK * (2 * jj + 2)]).T
                        dq_ref[r0:r0 + BLOCK, LANES * (2 * kh + jj):LANES * (2 * kh + jj + 1)] = (
                            (slab * cq + _rope_partner(slab * sq, first)) * Q_SCALE).astype(BF16)
                    dk2 = _dot(ds_t, qst)
                    dv2 = _dot(p_t.astype(BF16), dost)
                    dk_heads.append(dk2 + pltpu.roll(dk2, HEAD_DIM, 1))
                    dv_heads.append(dv2 + pltpu.roll(dv2, HEAD_DIM, 1))
                bandk = jnp.where(lane2 < HEAD_DIM, dk_heads[0], dk_heads[1])
                bandv = jnp.where(lane2 < HEAD_DIM, dv_heads[0], dv_heads[1])
                if g == 0:
                    headk_ref[...] = bandk[0:BLOCK]
                    headv_ref[...] = bandv[0:BLOCK]
                else:
                    newk_ref[r0 - BLOCK:r0, :] += bandk[0:BLOCK]
                    newv_ref[r0 - BLOCK:r0, :] += bandv[0:BLOCK]
                newk_ref[r0:r0 + BLOCK, :] = bandk[BLOCK:]
                newv_ref[r0:r0 + BLOCK, :] = bandv[BLOCK:]
            dsink_ref[...] += dsink

        @pl.when(n == steps)
        def _():
            headk_ref[...] = jnp.zeros_like(headk_ref)
            headv_ref[...] = jnp.zeros_like(headv_ref)

        @pl.when(n >= 1)
        def _():
            last = qb - BLOCK
            carryk_ref[last:, :] += headk_ref[...]
            carryv_ref[last:, :] += headv_ref[...]
            for g in range(gb):
                rows = slice(g * BLOCK, (g + 1) * BLOCK)
                dk = carryk_ref[rows, :]
                dk_ref[rows, :] = (dk * ck_ref[rows, :] + _rope_partner(dk * sk_ref[rows, :], first)).astype(BF16)
            dv_ref[...] = carryv_ref[...].astype(BF16)

        @pl.when(n < steps)
        def _():
            carryk_ref[...] = newk_ref[...]
            carryv_ref[...] = newv_ref[...]

    cur = lambda w: pl.BlockSpec((qb, w), lambda n: (jnp.minimum(n, steps - 1), 0))
    late = lambda w: pl.BlockSpec((qb, w), lambda n: (jnp.maximum(n - 1, 0), 0))
    prev = lambda w: pl.BlockSpec((BLOCK, w), lambda n: (jnp.maximum(jnp.minimum(n, steps - 1) * gb - 1, 0), 0))
    return pl.pallas_call(
        body, grid=(steps + 1,),
        in_specs=[pl.BlockSpec(memory_space=pltpu.SMEM), cur(ATTN_WIDTH), cur(KV_WIDTH), prev(KV_WIDTH),
                  cur(KV_WIDTH), prev(KV_WIDTH), cur(ATTN_WIDTH), cur(LANES), cur(LANES), late(LANES), late(LANES),
                  _resident(bias.shape), _resident(bias.shape)],
        out_specs=[cur(ATTN_WIDTH), late(KV_WIDTH), late(KV_WIDTH), pl.BlockSpec((8, LANES), lambda n: (0, 0))],
        out_shape=[jax.ShapeDtypeStruct((t, ATTN_WIDTH), BF16), jax.ShapeDtypeStruct((t, KV_WIDTH), BF16),
                   jax.ShapeDtypeStruct((t, KV_WIDTH), BF16), jax.ShapeDtypeStruct((8, LANES), F32)],
        scratch_shapes=[pltpu.VMEM((qb, LANES), F32), pltpu.VMEM((qb, LANES), F32),
                        pltpu.VMEM((qb, LANES), F32), pltpu.VMEM((qb, LANES), F32),
                        pltpu.VMEM((BLOCK, LANES), F32), pltpu.VMEM((BLOCK, LANES), F32)],
        compiler_params=_params("arbitrary"), name="attn_bwd",
    )(sinks, q, k, k, v, v, dout, cos_t, sin_t, cos_t, sin_t, bias, bias_first)


def _inproj_bwd_call(dq, dk, dv, dmixed, x, h0, dx1, w_in, g_mix):
    t = x.shape[0]
    tm = _token_tile(t)
    n_tiles = t // tm

    def body(dq_ref, dk_ref, dv_ref, dm_ref, halo_ref, x_ref, h_ref, dx1_ref, w_ref, g_ref,
             dx_ref, gw_ref, gb_ref, gg_ref, sc_ref):
        i = pl.program_id(0)

        @pl.when(i == 0)
        def _():
            for ref in (gw_ref, gb_ref, gg_ref):
                ref[...] = jnp.zeros_like(ref)

        tok = i * tm + lax.broadcasted_iota(jnp.int32, (tm, 1), 0)
        halo = jnp.where(i == n_tiles - 1, 0.0, halo_ref[...])
        du = []
        for g, size in enumerate(POOL_SIZES):
            lanes = slice(LANES * g, LANES * (g + 1))
            dm = dm_ref[:, lanes]
            sc_ref[0:tm, lanes] = dm / jnp.minimum(tok + 1, size).astype(F32)
            sc_ref[tm:, lanes] = halo[:, lanes] / float(size)
            acc = -dm
            for j in range(size):
                acc = acc + sc_ref[pl.ds(j, tm), lanes]
            du.append(acc)
        dz32 = jnp.concatenate([dq_ref[...].astype(F32), dk_ref[...].astype(F32), dv_ref[...].astype(F32)] + du,
                               axis=1)
        dz = dz32.astype(BF16)
        gb_ref[...] += jnp.sum(dz32, axis=0, keepdims=True)
        gw_ref[...] += _dot_tn(dz, h_ref[...])
        dh = _dot(dz, w_ref[...])
        xh, r = _rms_stats(x_ref[...])
        gg_ref[...] += jnp.sum(dh * xh, axis=0, keepdims=True)
        dx_ref[...] = dx1_ref[...] + _rms_bwd(dh, xh, r, g_ref[...])

    row = lambda w: pl.BlockSpec((tm, w), lambda i: (i, 0))
    const = lambda shape: pl.BlockSpec(shape, lambda i: (0,) * len(shape))
    last_halo = t // POOL_HALO - 1
    halo_after = pl.BlockSpec((POOL_HALO, POOL_WIDTH),
                              lambda i: (jnp.minimum((i + 1) * (tm // POOL_HALO), last_halo), 0))
    return pl.pallas_call(
        body, grid=(n_tiles,),
        in_specs=[row(ATTN_WIDTH), row(KV_WIDTH), row(KV_WIDTH), row(POOL_WIDTH), halo_after, row(D_MODEL),
                  row(D_MODEL), row(D_MODEL), _resident((IN_WIDTH, D_MODEL)), _resident((1, D_MODEL))],
        out_specs=[row(D_MODEL), const((IN_WIDTH, D_MODEL)), const((1, IN_WIDTH)), const((1, D_MODEL))],
        out_shape=[jax.ShapeDtypeStruct((t, D_MODEL), F32), jax.ShapeDtypeStruct((IN_WIDTH, D_MODEL), F32),
                   jax.ShapeDtypeStruct((1, IN_WIDTH), F32), jax.ShapeDtypeStruct((1, D_MODEL), F32)],
        scratch_shapes=[pltpu.VMEM((tm + POOL_HALO, POOL_WIDTH), F32)],
        compiler_params=_params("arbitrary"), name="inproj_bwd",
    )(dq, dk, dv, dmixed, dmixed, x, h0, dx1, w_in, g_mix)


def _local_step(x, target, small, w_in, w_out, w_gate, w_up, w_down):
    t = x.shape[0]
    cos_t, sin_t = _rope_tables(t)
    w_pool = small["w_pool"].astype(BF16)
    h0, q, k, v, u = _inproj_call(x, small["g_mix"], w_in, small["b_in"], cos_t, sin_t)
    bias_t, bias_first_t = _band_biases()
    attn = _attn_fwd_call(q, k, v, small["sinks"], bias_t, bias_first_t)
    x1, h1, pool_y = _outproj_call(attn, u, x, w_pool, small["b_pool"], small["pool_scale"], w_out,
                                   small["b_out"], small["g_ffn"])
    gate, up, act, dx2, loss, g_g_final = _ffn_fwd_call(h1, x1, target, w_gate, w_up, w_down, small["g_final"])
    dgate, dup, dx1, g_g_ffn = _ffn_bwd_act_call(dx2, gate, up, x1, w_gate, w_up, w_down, small["g_ffn"])
    g_w_gate, g_w_up, g_w_down = _ffn_bwd_weights_call(h1, dgate, dup, act, dx2)
    dattn, dmixed, g_w_out, g_b_out, g_w_pool, g_b_pool, g_pool_scale = _outproj_bwd_call(
        dx1, attn, pool_y, u, w_pool, small["b_pool"], small["pool_scale"], w_out)
    dq, dk, dv, g_sinks = _attn_bwd_call(q, k, v, dattn, small["sinks"], cos_t, sin_t, bias_t, bias_first_t)
    grad_x, g_w_in, g_b_in, g_g_mix = _inproj_bwd_call(dq, dk, dv, dmixed, x, h0, dx1, w_in, small["g_mix"])
    small_grads = dict(g_mix=g_g_mix, b_in=g_b_in, sinks=g_sinks[:, 0].reshape(1, N_Q_HEADS), w_pool=g_w_pool,
                       b_pool=g_b_pool, pool_scale=g_pool_scale, b_out=g_b_out, g_ffn=g_g_ffn, g_final=g_g_final)
    big_grads = dict(w_in=g_w_in, w_out=g_w_out, w_gate=g_w_gate, w_up=g_w_up, w_down=g_w_down)
    return loss, grad_x, small_grads, big_grads


def _mesh_place():
    x, y, c = lax.axis_index("x"), lax.axis_index("y"), lax.axis_index("c")
    other_chips = [(1 - x, y), (x, 1 - y), (1 - x, 1 - y)]
    return x, y, c, other_chips


def _half(ref, c, rows):
    return ref.at[pl.ds(pl.multiple_of(c * rows, 16), rows), :]


def _cast_place_call(shards, me):
    na = len(shards)
    split = 2

    def body(me_ref, *refs):
        for a in range(na):
            refs[na + a][...] = refs[a][...].astype(BF16)

    in_specs = [pl.BlockSpec((s.shape[0] // split, s.shape[1]), lambda i, me_ref: (i, 0)) for s in shards]
    out_specs = [pl.BlockSpec((None, s.shape[0] // split, s.shape[1]), lambda i, me_ref: (me_ref[0], i, 0))
                 for s in shards]
    return pl.pallas_call(
        body,
        grid_spec=pltpu.PrefetchScalarGridSpec(num_scalar_prefetch=1, grid=(split,), in_specs=in_specs,
                                               out_specs=out_specs),
        out_shape=[jax.ShapeDtypeStruct((N_SHARDS,) + s.shape, BF16) for s in shards],
        compiler_params=_params("parallel"), name="cast_place_weights",
    )(me.reshape(1), *shards)


def _gather_call(bufs):
    na = len(bufs)

    def body(*refs):
        outs = refs[na:2 * na]
        send_sems, recv_sems = refs[2 * na:]
        x, y, c, other_chips = _mesh_place()
        me = 2 * x + y
        sibling = (x, y, 1 - c)

        def remote(idx, piece, to):
            return pltpu.make_async_remote_copy(src_ref=piece, dst_ref=piece, send_sem=send_sems.at[idx],
                                                recv_sem=recv_sems.at[idx], device_id=to, device_id_type=MESH)

        sends = []
        for j, (cx, cy) in enumerate(other_chips):
            for a in range(na):
                rows = outs[a].shape[1] // 2
                cp = remote(j * na + a, _half(outs[a].at[me], c, rows), (cx, cy, c))
                cp.start()
                sends.append(cp)
        for j, (cx, cy) in enumerate(other_chips):
            for a in range(na):
                rows = outs[a].shape[1] // 2
                piece = _half(outs[a].at[2 * cx + cy], c, rows)
                remote(j * na + a, piece, sibling).wait_recv()
                cp = remote((3 + j) * na + a, piece, sibling)
                cp.start()
                sends.append(cp)
        for j, (cx, cy) in enumerate(other_chips):
            for a in range(na):
                rows = outs[a].shape[1] // 2
                remote((3 + j) * na + a, _half(outs[a].at[2 * cx + cy], 1 - c, rows), sibling).wait_recv()
        for cp in sends:
            cp.wait_send()

    return pl.pallas_call(
        body,
        in_specs=[HBM_SPEC] * na, out_specs=[HBM_SPEC] * na,
        out_shape=[jax.ShapeDtypeStruct(b.shape, b.dtype) for b in bufs],
        input_output_aliases={a: a for a in range(na)},
        scratch_shapes=[pltpu.SemaphoreType.DMA((6 * na,)), pltpu.SemaphoreType.DMA((6 * na,))],
        compiler_params=pltpu.CompilerParams(has_side_effects=True), name="gather_weights",
    )(*bufs)


def _sibling_exchange_call(grads):
    na = len(grads)

    def body(*refs):
        ins, outs = refs[:na], refs[na:2 * na]
        send_sems, recv_sems = refs[2 * na:]
        x, y, c, _ = _mesh_place()
        sibling = (x, y, 1 - c)
        copies = []
        for a in range(na):
            rows = ins[a].shape[1] // 2
            src = ins[a].at[:, pl.ds(pl.multiple_of((1 - c) * rows, 8), rows), :]
            cp = pltpu.make_async_remote_copy(src_ref=src, dst_ref=outs[a], send_sem=send_sems.at[a],
                                              recv_sem=recv_sems.at[a], device_id=sibling, device_id_type=MESH)
            cp.start()
            copies.append(cp)
        for cp in copies:
            cp.wait_recv()
        for cp in copies:
            cp.wait_send()

    return pl.pallas_call(
        body,
        in_specs=[HBM_SPEC] * na, out_specs=[HBM_SPEC] * na,
        out_shape=[jax.ShapeDtypeStruct((g.shape[0], g.shape[1] // 2, g.shape[2]), g.dtype) for g in grads],
        scratch_shapes=[pltpu.SemaphoreType.DMA((na,)), pltpu.SemaphoreType.DMA((na,))],
        compiler_params=pltpu.CompilerParams(has_side_effects=True), name="reduce_sibling_exchange",
    )(*grads)


def _pair_sum_call(grads, landed, c, wire_dtypes):
    na = len(grads)
    split = 2

    def body(c_ref, *refs):
        ins, lands, outs = refs[:na], refs[na:2 * na], refs[2 * na:]
        for a in range(na):
            outs[a][...] = (ins[a][...] + lands[a][...]).astype(outs[a].dtype)

    in_specs, land_specs, out_specs, views = [], [], [], []
    for g in grads:
        ns, r, cols = g.shape
        rows = r // 2 // split
        views.append(g.reshape(ns, 2, r // 2, cols))
        in_specs.append(pl.BlockSpec((None, None, rows, cols), lambda s, i, c_ref: (s, c_ref[0], i, 0)))
        land_specs.append(pl.BlockSpec((None, rows, cols), lambda s, i, c_ref: (s, i, 0)))
        out_specs.append(pl.BlockSpec((None, rows, cols), lambda s, i, c_ref: (s, i, 0)))
    return pl.pallas_call(
        body,
        grid_spec=pltpu.PrefetchScalarGridSpec(num_scalar_prefetch=1, grid=(N_SHARDS, split),
                                               in_specs=in_specs + land_specs, out_specs=out_specs),
        out_shape=[jax.ShapeDtypeStruct(l.shape, dt) for l, dt in zip(landed, wire_dtypes)],
        compiler_params=_params("parallel", "parallel"), name="reduce_pair_sum",
    )(c.reshape(1), *views, *landed)


def _chip_exchange_call(partials):
    na = len(partials)

    def body(*refs):
        ins, outs = refs[:na], refs[na:2 * na]
        send_sems, recv_sems = refs[2 * na:]
        x, y, c, other_chips = _mesh_place()
        me = 2 * x + y
        sends = []
        for j, (cx, cy) in enumerate(other_chips):
            for a in range(na):
                cp = pltpu.make_async_remote_copy(
                    src_ref=ins[a].at[2 * cx + cy], dst_ref=outs[a].at[me], send_sem=send_sems.at[j * na + a],
                    recv_sem=recv_sems.at[j * na + a], device_id=(cx, cy, c), device_id_type=MESH)
                cp.start()
                sends.append(cp)
        for j, (cx, cy) in enumerate(other_chips):
            for a in range(na):
                slot = outs[a].at[2 * cx + cy]
                pltpu.make_async_remote_copy(
                    src_ref=slot, dst_ref=slot, send_sem=send_sems.at[j * na + a],
                    recv_sem=recv_sems.at[j * na + a], device_id=(cx, cy, c), device_id_type=MESH).wait_recv()
        for cp in sends:
            cp.wait_send()

    return pl.pallas_call(
        body,
        in_specs=[HBM_SPEC] * na, out_specs=[HBM_SPEC] * na,
        out_shape=[jax.ShapeDtypeStruct(p.shape, p.dtype) for p in partials],
        scratch_shapes=[pltpu.SemaphoreType.DMA((3 * na,)), pltpu.SemaphoreType.DMA((3 * na,))],
        compiler_params=pltpu.CompilerParams(has_side_effects=True), name="reduce_chip_exchange",
    )(*partials)


def _chip_sum_call(partials, slots, me, c):
    na = len(slots)
    split = 2

    def body(place_ref, *refs):
        mine, lands, outs = refs[:na], refs[na:2 * na], refs[2 * na:]
        for j in range(N_SHARDS):
            @pl.when(place_ref[0] == j)
            def _():
                for a in range(na):
                    terms = [mine[a][...] if k == j else lands[a][k] for k in range(N_SHARDS)]
                    t0, t1, t2, t3 = (term.astype(F32) for term in terms)
                    outs[a][...] = ((t0 + t1) + t2) + t3

    mine_specs, land_specs, out_specs = [], [], []
    for s in slots:
        ns, h, cols = s.shape
        mine_specs.append(pl.BlockSpec((None, h // split, cols), lambda i, place: (place[0], i, 0)))
        land_specs.append(pl.BlockSpec((ns, h // split, cols), lambda i, place: (0, i, 0)))
        out_specs.append(pl.BlockSpec((h // split, cols), lambda i, place: (place[1] * split + i, 0)))
    return pl.pallas_call(
        body,
        grid_spec=pltpu.PrefetchScalarGridSpec(num_scalar_prefetch=1, grid=(split,),
                                               in_specs=mine_specs + land_specs, out_specs=out_specs),
        out_shape=[jax.ShapeDtypeStruct((2 * s.shape[1], s.shape[2]), F32) for s in slots],
        compiler_params=_params("parallel"), name="reduce_chip_sum",
    )(jnp.stack([me, c]), *partials, *slots)


def _sibling_allgather_call(bufs):
    na = len(bufs)

    def body(*refs):
        outs = refs[na:2 * na]
        send_sems, recv_sems = refs[2 * na:]
        x, y, c, _ = _mesh_place()
        sibling = (x, y, 1 - c)

        def remote(a, half):
            rows = outs[a].shape[0] // 2
            piece = outs[a].at[pl.ds(pl.multiple_of(half * rows, 8), rows), :]
            return pltpu.make_async_remote_copy(src_ref=piece, dst_ref=piece, send_sem=send_sems.at[a],
                                                recv_sem=recv_sems.at[a], device_id=sibling, device_id_type=MESH)

        sends = [remote(a, c) for a in range(na)]
        for cp in sends:
            cp.start()
        for a in range(na):
            remote(a, 1 - c).wait_recv()
        for cp in sends:
            cp.wait_send()

    return pl.pallas_call(
        body,
        in_specs=[HBM_SPEC] * na, out_specs=[HBM_SPEC] * na,
        out_shape=[jax.ShapeDtypeStruct(b.shape, b.dtype) for b in bufs],
        input_output_aliases={a: a for a in range(na)},
        scratch_shapes=[pltpu.SemaphoreType.DMA((na,)), pltpu.SemaphoreType.DMA((na,))],
        compiler_params=pltpu.CompilerParams(has_side_effects=True), name="reduce_sibling_allgather",
    )(*bufs)


def _adamw_call(ws, gs, ms, vs):
    na = len(ws)
    split = 4

    def body(*refs):
        w_refs, g_refs, m_refs, v_refs = (refs[k * na:(k + 1) * na] for k in range(4))
        d_refs, nm_refs, nv_refs = (refs[(4 + k) * na:(5 + k) * na] for k in range(3))
        for a in range(na):
            g = g_refs[a][...]
            m = ADAM_B1 * m_refs[a][...] + (1.0 - ADAM_B1) * g
            v = ADAM_B2 * v_refs[a][...] + (1.0 - ADAM_B2) * jnp.square(g)
            m_hat = m / (1.0 - ADAM_B1 ** ADAM_STEP)
            v_hat = v / (1.0 - ADAM_B2 ** ADAM_STEP)
            d_refs[a][...] = -ADAM_LR * (m_hat / (jnp.sqrt(v_hat) + ADAM_EPS) + ADAM_WD * w_refs[a][...])
            nm_refs[a][...] = m
            nv_refs[a][...] = v

    specs = [pl.BlockSpec((w.shape[0] // split, w.shape[1]), lambda i: (i, 0)) for w in ws]
    shapes = [jax.ShapeDtypeStruct(w.shape, F32) for w in ws]
    outs = pl.pallas_call(
        body, grid=(split,), in_specs=specs * 4, out_specs=specs * 3, out_shape=shapes * 3,
        compiler_params=_params("parallel"), name="adamw",
    )(*ws, *gs, *ms, *vs)
    return outs[:na], outs[na:2 * na], outs[2 * na:]


def _small_layout(shapes):
    layout, off = {}, 0
    for name in SMALL_NAMES:
        size = 1
        for d in shapes[name]:
            size *= d
        layout[name] = (off, size)
        off += size
    rows = -(-off // (LANES * 64)) * 64
    return layout, rows


def _pack_small(arrays, rows):
    flat = jnp.concatenate([arrays[name].reshape(-1) for name in SMALL_NAMES])
    return jnp.pad(flat, (0, rows * LANES - flat.shape[0])).reshape(rows, LANES)


def _unpack_small(packed, layout, shapes):
    flat = packed.reshape(-1)
    return {name: flat[off:off + size].reshape(shapes[name]) for name, (off, size) in layout.items()}


def kernel(x, g_mix, w_in, b_in, sinks, w_pool, b_pool, pool_scale, w_out, b_out, g_ffn, w_gate, w_up, w_down, g_final, loss_target, m_g_mix, m_w_in, m_b_in, m_sinks, m_w_pool, m_b_pool, m_pool_scale, m_w_out, m_b_out, m_g_ffn, m_w_gate, m_w_up, m_w_down, m_g_final, v_g_mix, v_w_in, v_b_in, v_sinks, v_w_pool, v_b_pool, v_pool_scale, v_w_out, v_b_out, v_g_ffn, v_w_gate, v_w_up, v_w_down, v_g_final):
    given = dict(locals())
    big_names = ("w_in", "w_out", "w_gate", "w_up", "w_down")
    out_shapes = {n: given[n].shape for n in SMALL_NAMES + big_names}
    t = x.shape[1]
    c = lax.axis_index("c")

    me = 2 * lax.axis_index("x") + lax.axis_index("y")
    transposed = ("w_in", "w_gate", "w_up")

    def shard_view(a, name):
        return a[0].T if name in transposed else a[0]

    g_in, g_out, g_gate, g_up, g_down = _gather_call(
        _cast_place_call([shard_view(given[n], n) for n in big_names], me))
    w_in_full = g_in.reshape(IN_WIDTH, D_MODEL)
    w_out_full = g_out.reshape(D_MODEL, D_MODEL)

    small = dict(g_mix=g_mix, b_in=b_in, sinks=sinks, w_pool=w_pool[0], b_pool=b_pool.reshape(1, POOL_WIDTH),
                 pool_scale=pool_scale.reshape(1, POOL_WIDTH), b_out=b_out, g_ffn=g_ffn,
                 g_final=g_final.reshape(1, D_MODEL))
    loss, grad_x, small_grads, big_grads = _local_step(
        x[0], loss_target[0], small, w_in_full, w_out_full, g_gate, g_up, g_down)
    loss = lax.psum(loss[0, 0], ("x", "y", "c"))

    layout, small_rows = _small_layout(out_shapes)
    packed = _pack_small(small_grads, small_rows)
    grads = [
        big_grads["w_in"].reshape(N_SHARDS, IN_WIDTH // N_SHARDS, D_MODEL),
        big_grads["w_out"].reshape(N_SHARDS, D_MODEL // N_SHARDS, D_MODEL),
        big_grads["w_gate"], big_grads["w_up"], big_grads["w_down"],
        jnp.broadcast_to(packed[None], (N_SHARDS,) + packed.shape),
    ]
    landed = _sibling_exchange_call(grads)
    partials = _pair_sum_call(grads, landed, c, [BF16] * len(big_names) + [F32])
    slots = _chip_exchange_call(partials)
    totals = _sibling_allgather_call(_chip_sum_call(partials, slots, me, c))

    def owned(prefix):
        return ([shard_view(given[prefix + n], n) for n in big_names]
                + [_pack_small({n: given[prefix + n] for n in SMALL_NAMES}, small_rows)])

    ws, ms, vs = owned(""), owned("m_"), owned("v_")
    deltas, new_ms, new_vs = _adamw_call(ws, totals, ms, vs)

    def unpack(arrays):
        out = _unpack_small(arrays[-1], layout, out_shapes)
        for n, a in zip(big_names, arrays[:-1]):
            out[n] = (a.T if n in transposed else a).reshape(out_shapes[n])
        return out

    order = ("g_mix", "w_in", "b_in", "sinks", "w_pool", "b_pool", "pool_scale", "w_out", "b_out", "g_ffn",
             "w_gate", "w_up", "w_down", "g_final")
    results = [loss, grad_x.reshape(x.shape)]
    for group in (totals, deltas, new_ms, new_vs):
        named = unpack(list(group))
        results += [named[n] for n in order]
    return tuple(results)
```

```python
import functools
from typing import Any, Callable, Mapping, NamedTuple, Sequence

import jax
import jax.numpy as jnp
from jax import lax
from jax.experimental import pallas as pl
from jax.experimental.pallas import tpu as pltpu

F32 = jnp.float32
BF16 = jnp.bfloat16

D_MODEL = 1024
ATTN_WIDTH = 512
KV_WIDTH = 128
POOL_WIDTH = 512
IN_WIDTH = ATTN_WIDTH + 2 * KV_WIDTH + POOL_WIDTH
HEAD_DIM = 64
N_Q_HEADS = 8
BLOCK = 128
POOL_SIZES = (2, 4, 8, 16)
POOL_HALO = 16
ROPE_THETA = 10000.0
RMS_EPS = 1e-5
Q_SCALE = HEAD_DIM ** -0.5

ADAM_LR = 0.001
ADAM_B1 = 0.9
ADAM_B2 = 0.999
ADAM_EPS = 1e-08
ADAM_WD = 0.01
ADAM_STEP = 10

N_SHARDS = 4
LANES = 128
VMEM_LIMIT = 56 * 1024 * 1024

MESH = pl.DeviceIdType.MESH
HBM_SPEC = pl.BlockSpec(memory_space=pltpu.HBM)

SMALL_NAMES = ("g_mix", "b_in", "sinks", "w_pool", "b_pool", "pool_scale", "b_out", "g_ffn", "g_final")


def _dot(a, b):
    return jnp.dot(a, b, preferred_element_type=F32)


def _dot_nt(a, b):
    return lax.dot_general(a, b, (((1,), (1,)), ((), ())), preferred_element_type=F32)


def _dot_tn(a, b):
    return lax.dot_general(a, b, (((0,), (0,)), ((), ())), preferred_element_type=F32)


def _params(*semantics):
    return pltpu.CompilerParams(dimension_semantics=semantics, vmem_limit_bytes=VMEM_LIMIT)


def _resident(shape):
    return pl.BlockSpec(shape, lambda *_: (0,) * len(shape), pipeline_mode=pl.Buffered(1))


def _token_tile(t, largest=512):
    for tm in (1024, 512, 256, 128):
        if tm <= largest and t % tm == 0:
            return tm
    raise ValueError(f"sequence length {t} is not a multiple of 128")


def _rms_stats(xf):
    r = lax.rsqrt(jnp.mean(xf * xf, axis=-1, keepdims=True) + RMS_EPS)
    return xf * r, r


def _rms_bwd(dy, xh, r, g):
    dxh = dy * g
    return r * (dxh - xh * jnp.mean(dxh * xh, axis=-1, keepdims=True))


def _rope_partner(t, first):
    return jnp.where(first, pltpu.roll(t, LANES - HEAD_DIM // 2, 1), pltpu.roll(t, HEAD_DIM // 2, 1))


def _first_half_mask(shape):
    lane = lax.broadcasted_iota(jnp.int32, shape, 1)
    return (lane % HEAD_DIM) < (HEAD_DIM // 2)


def _rope_tables(t):
    inv_freq = 1.0 / (ROPE_THETA ** (jnp.arange(0, HEAD_DIM, 2, dtype=F32) / HEAD_DIM))
    ang = jnp.arange(t, dtype=F32)[:, None] * inv_freq[None, :]
    cos, sin = jnp.cos(ang), jnp.sin(ang)
    cos_t = jnp.tile(jnp.concatenate([cos, cos], axis=-1), (1, LANES // HEAD_DIM))
    sin_t = jnp.tile(jnp.concatenate([-sin, sin], axis=-1), (1, LANES // HEAD_DIM))
    return cos_t, sin_t


class _Exchange(NamedTuple):
    inputs: Sequence[Any]
    out_shapes: Sequence[Any]
    aliases: Mapping[int, int]
    sems: Sequence[Any]
    start: Callable[..., None]
    finish: Callable[..., None]


def _compute_call(body, operands, *, grid, in_specs, out_specs, out_shape, scratch_shapes=(), semantics, name,
                  exchange=None):
    n_in, n_out, n_scr = len(in_specs), len(out_specs), len(scratch_shapes)
    if exchange is None:
        return pl.pallas_call(body, grid=grid, in_specs=in_specs, out_specs=out_specs, out_shape=out_shape,
                              scratch_shapes=scratch_shapes, compiler_params=_params(*semantics),
                              name=name)(*operands), []
    k_in, k_out = len(exchange.inputs), len(exchange.out_shapes)

    def hosting(*refs):
        ins, rest = refs[:n_in], refs[n_in:]
        ex_ins, rest = rest[:k_in], rest[k_in:]
        outs, rest = rest[:n_out], rest[n_out:]
        ex_outs, rest = rest[:k_out], rest[k_out:]
        scratch, sems = rest[:n_scr], rest[n_scr:]
        steps = [pl.program_id(d) for d in range(len(grid))]
        is_first = functools.reduce(jnp.logical_and, [s == 0 for s in steps])
        is_last = functools.reduce(jnp.logical_and, [s == g - 1 for s, g in zip(steps, grid)])

        @pl.when(is_first)
        def _():
            exchange.start(ex_ins, ex_outs, sems)

        body(*ins, *outs, *scratch)

        @pl.when(is_last)
        def _():
            exchange.finish(ex_ins, ex_outs, sems)

    results = pl.pallas_call(
        hosting, grid=grid,
        in_specs=list(in_specs) + [HBM_SPEC] * k_in, out_specs=list(out_specs) + [HBM_SPEC] * k_out,
        out_shape=list(out_shape) + list(exchange.out_shapes),
        scratch_shapes=list(scratch_shapes) + list(exchange.sems),
        input_output_aliases={n_in + i: n_out + j for i, j in exchange.aliases.items()},
        compiler_params=_params(*semantics), name=name,
    )(*operands, *exchange.inputs)
    return results[:n_out], results[n_out:]


def _exchange_call(exchange, name):
    n_in, n_out = len(exchange.inputs), len(exchange.out_shapes)

    def body(*refs):
        ins, outs, sems = refs[:n_in], refs[n_in:n_in + n_out], refs[n_in + n_out:]
        exchange.start(ins, outs, sems)
        exchange.finish(ins, outs, sems)

    return pl.pallas_call(
        body, in_specs=[HBM_SPEC] * n_in, out_specs=[HBM_SPEC] * n_out, out_shape=list(exchange.out_shapes),
        input_output_aliases=dict(exchange.aliases), scratch_shapes=list(exchange.sems),
        compiler_params=pltpu.CompilerParams(has_side_effects=True), name=name,
    )(*exchange.inputs)


def _mesh_place():
    x, y, c = lax.axis_index("x"), lax.axis_index("y"), lax.axis_index("c")
    other_chips = [(1 - x, y), (x, 1 - y), (1 - x, 1 - y)]
    return x, y, c, other_chips


def _half(ref, c, rows):
    return ref.at[pl.ds(pl.multiple_of(c * rows, 16), rows), :]


def _remote(src, dst, send_sems, recv_sems, idx, to):
    return functools.partial(pltpu.make_async_remote_copy, src_ref=src, dst_ref=dst, send_sem=send_sems.at[idx],
                             recv_sem=recv_sems.at[idx], device_id=to, device_id_type=MESH)


def _gather_exchange(bufs):
    na = len(bufs)

    def copies(outs, sems):
        send_sems, recv_sems = sems
        x, y, c, other_chips = _mesh_place()
        me, sibling = 2 * x + y, (x, y, 1 - c)
        ici_out, ici_in, fwd_out, fwd_in = [], [], [], []
        for j, (cx, cy) in enumerate(other_chips):
            for a in range(na):
                rows = outs[a].shape[1] // 2
                mine = _half(outs[a].at[me], c, rows)
                landed = _half(outs[a].at[2 * cx + cy], c, rows)
                passed = _half(outs[a].at[2 * cx + cy], 1 - c, rows)
                ici_out.append(_remote(mine, mine, send_sems, recv_sems, j * na + a, (cx, cy, c)))
                ici_in.append(_remote(landed, landed, send_sems, recv_sems, j * na + a, (cx, cy, c)))
                fwd_out.append(_remote(landed, landed, send_sems, recv_sems, (3 + j) * na + a, sibling))
                fwd_in.append(_remote(passed, passed, send_sems, recv_sems, (3 + j) * na + a, sibling))
        return ici_out, ici_in, fwd_out, fwd_in

    def start(ins, outs, sems):
        for cp in copies(outs, sems)[0]:
            cp().start()

    def finish(ins, outs, sems):
        ici_out, ici_in, fwd_out, fwd_in = copies(outs, sems)
        for arrived, forward in zip(ici_in, fwd_out):
            arrived().wait_recv()
            forward().start()
        for cp in fwd_in:
            cp().wait_recv()
        for cp in ici_out + fwd_out:
            cp().wait_send()

    return _Exchange(inputs=bufs, out_shapes=[jax.ShapeDtypeStruct(b.shape, b.dtype) for b in bufs],
                     aliases={a: a for a in range(na)},
                     sems=[pltpu.SemaphoreType.DMA((6 * na,)), pltpu.SemaphoreType.DMA((6 * na,))],
                     start=start, finish=finish)


def _sibling_exchange(grads):
    na = len(grads)

    def copies(ins, outs, sems):
        x, y, c, _ = _mesh_place()
        out = []
        for a in range(na):
            rows = ins[a].shape[1] // 2
            src = ins[a].at[:, pl.ds(pl.multiple_of((1 - c) * rows, 8), rows), :]
            out.append(_remote(src, outs[a], sems[0], sems[1], a, (x, y, 1 - c)))
        return out

    def start(ins, outs, sems):
        for cp in copies(ins, outs, sems):
            cp().start()

    def finish(ins, outs, sems):
        for cp in copies(ins, outs, sems):
            cp().wait_recv()
            cp().wait_send()

    return _Exchange(inputs=grads,
                     out_shapes=[jax.ShapeDtypeStruct((g.shape[0], g.shape[1] // 2, g.shape[2]), g.dtype)
                                 for g in grads],
                     aliases={}, sems=[pltpu.SemaphoreType.DMA((na,)), pltpu.SemaphoreType.DMA((na,))],
                     start=start, finish=finish)


def _chip_exchange(partials):
    na = len(partials)

    def copies(ins, outs, sems):
        x, y, c, other_chips = _mesh_place()
        me = 2 * x + y
        sends, arrivals = [], []
        for j, (cx, cy) in enumerate(other_chips):
            for a in range(na):
                slot = outs[a].at[2 * cx + cy]
                sends.append(_remote(ins[a].at[2 * cx + cy], outs[a].at[me], sems[0], sems[1], j * na + a, (cx, cy, c)))
                arrivals.append(_remote(slot, slot, sems[0], sems[1], j * na + a, (cx, cy, c)))
        return sends, arrivals

    def start(ins, outs, sems):
        for cp in copies(ins, outs, sems)[0]:
            cp().start()

    def finish(ins, outs, sems):
        sends, arrivals = copies(ins, outs, sems)
        for cp in arrivals:
            cp().wait_recv()
        for cp in sends:
            cp().wait_send()

    return _Exchange(inputs=partials, out_shapes=[jax.ShapeDtypeStruct(p.shape, p.dtype) for p in partials],
                     aliases={}, sems=[pltpu.SemaphoreType.DMA((3 * na,)), pltpu.SemaphoreType.DMA((3 * na,))],
                     start=start, finish=finish)


def _sibling_allgather(bufs):
    na = len(bufs)

    def piece(outs, sems, a, half):
        x, y, c, _ = _mesh_place()
        rows = outs[a].shape[0] // 2
        rows_ref = outs[a].at[pl.ds(pl.multiple_of(half * rows, 8), rows), :]
        return _remote(rows_ref, rows_ref, sems[0], sems[1], a, (x, y, 1 - c))

    def start(ins, outs, sems):
        c = lax.axis_index("c")
        for a in range(na):
            piece(outs, sems, a, c)().start()

    def finish(ins, outs, sems):
        c = lax.axis_index("c")
        for a in range(na):
            piece(outs, sems, a, 1 - c)().wait_recv()
        for a in range(na):
            piece(outs, sems, a, c)().wait_send()

    return _Exchange(inputs=bufs, out_shapes=[jax.ShapeDtypeStruct(b.shape, b.dtype) for b in bufs],
                     aliases={a: a for a in range(na)},
                     sems=[pltpu.SemaphoreType.DMA((na,)), pltpu.SemaphoreType.DMA((na,))],
                     start=start, finish=finish)


def _inproj_call(x, g_mix, w_in, b_in, cos_t, sin_t, exchange=None):
    t = x.shape[0]
    tm = _token_tile(t)

    def body(x_ref, g_ref, w_ref, b_ref, c_ref, s_ref, h_ref, q_ref, k_ref, v_ref, u_ref):
        xh, _ = _rms_stats(x_ref[...])
        h = (xh * g_ref[...]).astype(BF16)
        h_ref[...] = h
        z = _dot_nt(h, w_ref[...]) + b_ref[...]
        cos, sin = c_ref[...], s_ref[...]
        first = _first_half_mask((tm, LANES))
        for j in range(ATTN_WIDTH // LANES):
            zj = z[:, LANES * j:LANES * (j + 1)]
            q_ref[:, LANES * j:LANES * (j + 1)] = ((zj * cos + _rope_partner(zj, first) * sin) * Q_SCALE).astype(BF16)
        zk = z[:, ATTN_WIDTH:ATTN_WIDTH + KV_WIDTH]
        k_ref[...] = (zk * cos + _rope_partner(zk, first) * sin).astype(BF16)
        v_ref[...] = z[:, ATTN_WIDTH + KV_WIDTH:ATTN_WIDTH + 2 * KV_WIDTH].astype(BF16)
        u_ref[...] = z[:, ATTN_WIDTH + 2 * KV_WIDTH:]

    row = lambda w: pl.BlockSpec((tm, w), lambda i: (i, 0))
    return _compute_call(
        body, (x, g_mix, w_in, b_in, cos_t, sin_t), grid=(t // tm,),
        in_specs=[row(D_MODEL), _resident((1, D_MODEL)), _resident((IN_WIDTH, D_MODEL)), _resident((1, IN_WIDTH)),
                  row(LANES), row(LANES)],
        out_specs=[row(D_MODEL), row(ATTN_WIDTH), row(KV_WIDTH), row(KV_WIDTH), row(POOL_WIDTH)],
        out_shape=[jax.ShapeDtypeStruct((t, D_MODEL), BF16), jax.ShapeDtypeStruct((t, ATTN_WIDTH), BF16),
                   jax.ShapeDtypeStruct((t, KV_WIDTH), BF16), jax.ShapeDtypeStruct((t, KV_WIDTH), BF16),
                   jax.ShapeDtypeStruct((t, POOL_WIDTH), F32)],
        semantics=("parallel",), name="inproj_fwd", exchange=exchange)


def _head_in_both_halves(band, kv_head):
    lane = lax.broadcasted_iota(jnp.int32, band.shape, 1)
    own = (lane < HEAD_DIM) if kv_head == 0 else (lane >= HEAD_DIM)
    return jnp.where(own, band, pltpu.roll(band, HEAD_DIM, 1)).astype(BF16)


def _stack_heads(ref, kv_head, row0):
    low = lax.broadcasted_iota(jnp.int32, (BLOCK, LANES), 1) < HEAD_DIM
    parts = []
    for jj in range(2):
        slab = ref[row0:row0 + BLOCK, LANES * (2 * kv_head + jj):LANES * (2 * kv_head + jj + 1)]
        zero = jnp.zeros_like(slab)
        parts += [jnp.where(low, slab, zero), jnp.where(low, zero, slab)]
    return jnp.concatenate(parts, axis=0)


def _band_biases():
    c = lax.broadcasted_iota(jnp.int32, (2 * BLOCK, 4 * BLOCK), 0)
    i = lax.broadcasted_iota(jnp.int32, (2 * BLOCK, 4 * BLOCK), 1) % BLOCK
    vis = (c > i) & (c <= i + BLOCK)
    return jnp.where(vis, 0.0, -jnp.inf).astype(F32), jnp.where(vis & (c >= BLOCK), 0.0, -jnp.inf).astype(F32)


def _blocks_per_step(nb):
    for g in (4, 2, 1):
        if nb % g == 0:
            return g


def _sink_row(sink_ref, kv_head):
    return jnp.concatenate([jnp.full((1, BLOCK), sink_ref[0, 4 * kv_head + g], F32) for g in range(4)], axis=1)


def _kv_band(cur_ref, prev_ref, g):
    r0 = g * BLOCK
    prev = prev_ref[...] if g == 0 else cur_ref[r0 - BLOCK:r0, :]
    return jnp.concatenate([prev, cur_ref[r0:r0 + BLOCK, :]], axis=0).astype(F32)


def _attn_fwd_call(q, k, v, sinks, bias, bias_first, exchange=None):
    t = q.shape[0]
    nb = t // BLOCK
    gb = _blocks_per_step(nb)
    qb = gb * BLOCK

    def body(sink_ref, q_ref, kc_ref, kp_ref, vc_ref, vp_ref, b_ref, bf_ref, o_ref):
        n = pl.program_id(0)
        row_low = lax.broadcasted_iota(jnp.int32, (LANES, BLOCK), 0) < HEAD_DIM
        for g in range(gb):
            r0 = g * BLOCK
            kband = _kv_band(kc_ref, kp_ref, g)
            vband_t = _kv_band(vc_ref, vp_ref, g).T
            bias_g = jnp.where(n == 0, bf_ref[...], b_ref[...]) if g == 0 else b_ref[...]
            for kh in range(2):
                k2 = _head_in_both_halves(kband, kh)
                vt_head = vband_t[HEAD_DIM * kh:HEAD_DIM * (kh + 1), :]
                v2_t = jnp.concatenate([vt_head, vt_head], axis=0).astype(BF16)
                s_t = _dot_nt(k2, _stack_heads(q_ref, kh, r0)) + bias_g
                sink = _sink_row(sink_ref, kh)
                m = jnp.maximum(jnp.max(s_t, axis=0, keepdims=True), sink)
                e = jnp.exp(s_t - m)
                inv = 1.0 / (jnp.sum(e, axis=0, keepdims=True) + jnp.exp(sink - m))
                o_t = _dot(v2_t, (e * inv).astype(BF16))
                for jj in range(2):
                    slab = jnp.where(row_low, o_t[:, BLOCK * 2 * jj:BLOCK * (2 * jj + 1)],
                                     o_t[:, BLOCK * (2 * jj + 1):BLOCK * (2 * jj + 2)]).T
                    o_ref[r0:r0 + BLOCK, LANES * (2 * kh + jj):LANES * (2 * kh + jj + 1)] = slab.astype(BF16)

    cur = lambda w: pl.BlockSpec((qb, w), lambda n: (n, 0))
    prev = lambda w: pl.BlockSpec((BLOCK, w), lambda n: (jnp.maximum(n * gb - 1, 0), 0))
    return _compute_call(
        body, (sinks, q, k, k, v, v, bias, bias_first), grid=(nb // gb,),
        in_specs=[pl.BlockSpec(memory_space=pltpu.SMEM), cur(ATTN_WIDTH), cur(KV_WIDTH), prev(KV_WIDTH),
                  cur(KV_WIDTH), prev(KV_WIDTH), _resident(bias.shape), _resident(bias.shape)],
        out_specs=[cur(ATTN_WIDTH)],
        out_shape=[jax.ShapeDtypeStruct((t, ATTN_WIDTH), BF16)],
        semantics=("parallel",), name="attn_fwd", exchange=exchange)


def _pool_mixed(sc_ref, u_ref, halo_ref, i, tm):
    sc_ref[0:POOL_HALO, :] = jnp.where(i == 0, 0.0, halo_ref[...])
    sc_ref[POOL_HALO:, :] = u_ref[...]
    tok = i * tm + lax.broadcasted_iota(jnp.int32, (tm, 1), 0)
    mixed = []
    for g, size in enumerate(POOL_SIZES):
        lanes = slice(LANES * g, LANES * (g + 1))
        cur = sc_ref[pl.ds(POOL_HALO, tm), lanes]
        acc = cur
        for j in range(1, size):
            acc = acc + sc_ref[pl.ds(POOL_HALO - j, tm), lanes]
        count = jnp.minimum(tok + 1, size).astype(F32)
        mixed.append(acc / count - cur)
    return mixed


def _halo_before(tm, width):
    return pl.BlockSpec((POOL_HALO, width), lambda i: (jnp.maximum(i * (tm // POOL_HALO) - 1, 0), 0))


def _outproj_call(attn, u, x, w_pool, b_pool, pool_scale, w_out, b_out, g_ffn, exchange=None):
    t = x.shape[0]
    tm = _token_tile(t)

    def body(a_ref, u_ref, halo_ref, x_ref, wp_ref, bp_ref, ps_ref, wo_ref, bo_ref, g_ref,
             x1_ref, h1_ref, y_ref, sc_ref):
        i = pl.program_id(0)
        for g, mixed in enumerate(_pool_mixed(sc_ref, u_ref, halo_ref, i, tm)):
            lanes = slice(LANES * g, LANES * (g + 1))
            lin = _dot(mixed.astype(BF16), wp_ref[g]) + bp_ref[:, lanes]
            y_ref[:, lanes] = (lin * ps_ref[:, lanes]).astype(BF16)
        x1 = (x_ref[...] + _dot(a_ref[...], wo_ref[0:ATTN_WIDTH, :]) + _dot(y_ref[...], wo_ref[ATTN_WIDTH:, :])
              + bo_ref[...])
        x1_ref[...] = x1
        xh, _ = _rms_stats(x1)
        h1_ref[...] = (xh * g_ref[...]).astype(BF16)

    row = lambda w: pl.BlockSpec((tm, w), lambda i: (i, 0))
    return _compute_call(
        body, (attn, u, u, x, w_pool, b_pool, pool_scale, w_out, b_out, g_ffn), grid=(t // tm,),
        in_specs=[row(ATTN_WIDTH), row(POOL_WIDTH), _halo_before(tm, POOL_WIDTH), row(D_MODEL),
                  _resident((4, LANES, LANES)), _resident((1, POOL_WIDTH)), _resident((1, POOL_WIDTH)),
                  _resident((D_MODEL, D_MODEL)), _resident((1, D_MODEL)), _resident((1, D_MODEL))],
        out_specs=[row(D_MODEL), row(D_MODEL), row(POOL_WIDTH)],
        out_shape=[jax.ShapeDtypeStruct((t, D_MODEL), F32), jax.ShapeDtypeStruct((t, D_MODEL), BF16),
                   jax.ShapeDtypeStruct((t, POOL_WIDTH), BF16)],
        scratch_shapes=[pltpu.VMEM((tm + POOL_HALO, POOL_WIDTH), F32)],
        semantics=("parallel",), name="outproj_fwd", exchange=exchange)


def _ffn_fwd_call(h1, x1, target, w_gate, w_up, w_down, g_final):
    t = x1.shape[0]
    tm = _token_tile(t)
    ns, fs, _ = w_gate.shape

    def body(h_ref, x1_ref, tgt_ref, wg_ref, wu_ref, wd_ref, gf_ref,
             gate_ref, up_ref, act_ref, dx2_ref, loss_ref, dgf_ref):
        i, s = pl.program_id(0), pl.program_id(1)
        acc_ref = dx2_ref

        @pl.when(s == 0)
        def _():
            acc_ref[...] = x1_ref[...]

        @pl.when((i == 0) & (s == 0))
        def _():
            loss_ref[...] = jnp.zeros_like(loss_ref)
            dgf_ref[...] = jnp.zeros_like(dgf_ref)

        h = h_ref[...]
        gate = _dot_nt(h, wg_ref[...])
        up = _dot_nt(h, wu_ref[...])
        act = (gate * jax.nn.sigmoid(gate) * up).astype(BF16)
        gate_ref[...] = gate.astype(BF16)
        up_ref[...] = up.astype(BF16)
        act_ref[...] = act
        acc_ref[...] += _dot(act, wd_ref[...])

        @pl.when(s == ns - 1)
        def _():
            xh, r = _rms_stats(acc_ref[...])
            gf = gf_ref[...]
            err = xh * gf - tgt_ref[...]
            loss_ref[...] += 0.5 * jnp.sum(err * err) / D_MODEL
            dy = err / D_MODEL
            dgf_ref[...] += jnp.sum(dy * xh, axis=0, keepdims=True)
            dx2_ref[...] = _rms_bwd(dy, xh, r, gf)

    row = lambda w: pl.BlockSpec((tm, w), lambda i, s: (i, 0))
    shard_act = pl.BlockSpec((None, tm, fs), lambda i, s: (s, i, 0))
    act_shape = jax.ShapeDtypeStruct((ns, t, fs), BF16)
    return pl.pallas_call(
        body, grid=(t // tm, ns),
        in_specs=[row(D_MODEL), row(D_MODEL), row(D_MODEL),
                  pl.BlockSpec((None, fs, D_MODEL), lambda i, s: (s, 0, 0)),
                  pl.BlockSpec((None, fs, D_MODEL), lambda i, s: (s, 0, 0)),
                  pl.BlockSpec((None, fs, D_MODEL), lambda i, s: (s, 0, 0)),
                  _resident((1, D_MODEL))],
        out_specs=[shard_act, shard_act, shard_act, row(D_MODEL),
                   pl.BlockSpec((1, LANES), lambda i, s: (0, 0)), pl.BlockSpec((1, D_MODEL), lambda i, s: (0, 0))],
        out_shape=[act_shape, act_shape, act_shape, jax.ShapeDtypeStruct((t, D_MODEL), F32),
                   jax.ShapeDtypeStruct((1, LANES), F32), jax.ShapeDtypeStruct((1, D_MODEL), F32)],
        compiler_params=_params("arbitrary", "arbitrary"), name="ffn_fwd",
    )(h1, x1, target, w_gate, w_up, w_down, g_final)


def _ffn_bwd_act_call(dx2, gate, up, x1, w_gate, w_up, w_down, g_ffn):
    t = x1.shape[0]
    tm = _token_tile(t)
    ns, fs, _ = w_gate.shape

    def body(dx2_ref, gate_ref, up_ref, x1_ref, wg_ref, wu_ref, wd_ref, g_ref,
             dgate_ref, dup_ref, dx1_ref, dg_ref, dx2b_ref):
        i, s = pl.program_id(0), pl.program_id(1)
        acc_ref = dx1_ref

        @pl.when(s == 0)
        def _():
            acc_ref[...] = jnp.zeros_like(acc_ref)
            dx2b_ref[...] = dx2_ref[...].astype(BF16)

        @pl.when((i == 0) & (s == 0))
        def _():
            dg_ref[...] = jnp.zeros_like(dg_ref)

        dact = _dot_nt(dx2b_ref[...], wd_ref[...])
        gate = gate_ref[...].astype(F32)
        upv = up_ref[...].astype(F32)
        sg = jax.nn.sigmoid(gate)
        dup = (dact * (gate * sg)).astype(BF16)
        dgate = (dact * upv * (sg * (1.0 + gate * (1.0 - sg)))).astype(BF16)
        dgate_ref[...] = dgate
        dup_ref[...] = dup
        acc_ref[...] += _dot(dgate, wg_ref[...]) + _dot(dup, wu_ref[...])

        @pl.when(s == ns - 1)
        def _():
            xh, r = _rms_stats(x1_ref[...])
            dh = acc_ref[...]
            dg_ref[...] += jnp.sum(dh * xh, axis=0, keepdims=True)
            dx1_ref[...] = dx2_ref[...] + _rms_bwd(dh, xh, r, g_ref[...])

    row = lambda w: pl.BlockSpec((tm, w), lambda i, s: (i, 0))
    shard_act = pl.BlockSpec((None, tm, fs), lambda i, s: (s, i, 0))
    act_shape = jax.ShapeDtypeStruct((ns, t, fs), BF16)
    return pl.pallas_call(
        body, grid=(t // tm, ns),
        in_specs=[row(D_MODEL), shard_act, shard_act, row(D_MODEL),
                  pl.BlockSpec((None, fs, D_MODEL), lambda i, s: (s, 0, 0)),
                  pl.BlockSpec((None, fs, D_MODEL), lambda i, s: (s, 0, 0)),
                  pl.BlockSpec((None, fs, D_MODEL), lambda i, s: (s, 0, 0)),
                  _resident((1, D_MODEL))],
        out_specs=[shard_act, shard_act, row(D_MODEL), pl.BlockSpec((1, D_MODEL), lambda i, s: (0, 0))],
        out_shape=[act_shape, act_shape, jax.ShapeDtypeStruct((t, D_MODEL), F32),
                   jax.ShapeDtypeStruct((1, D_MODEL), F32)],
        scratch_shapes=[pltpu.VMEM((tm, D_MODEL), BF16)],
        compiler_params=_params("arbitrary", "arbitrary"), name="ffn_bwd_act",
    )(dx2, gate, up, x1, w_gate, w_up, w_down, g_ffn)


def _ffn_bwd_weights_call(h1, dgate, dup, act, dx2):
    t = h1.shape[0]
    tm = _token_tile(t, 1024)
    ns, _, fs = dgate.shape

    def body(h_ref, dgate_ref, dup_ref, act_ref, dx2_ref, gg_ref, gu_ref, gd_ref):
        @pl.when(pl.program_id(1) == 0)
        def _():
            gg_ref[...] = jnp.zeros_like(gg_ref)
            gu_ref[...] = jnp.zeros_like(gu_ref)
            gd_ref[...] = jnp.zeros_like(gd_ref)

        h = h_ref[...]
        gg_ref[...] += _dot_tn(dgate_ref[...], h)
        gu_ref[...] += _dot_tn(dup_ref[...], h)
        gd_ref[...] += _dot_tn(act_ref[...], dx2_ref[...].astype(BF16))

    row = lambda w: pl.BlockSpec((tm, w), lambda s, i: (i, 0))
    shard_act = pl.BlockSpec((None, tm, fs), lambda s, i: (s, i, 0))
    return pl.pallas_call(
        body, grid=(ns, t // tm),
        in_specs=[row(D_MODEL), shard_act, shard_act, shard_act, row(D_MODEL)],
        out_specs=[pl.BlockSpec((None, fs, D_MODEL), lambda s, i: (s, 0, 0))] * 3,
        out_shape=[jax.ShapeDtypeStruct((ns, fs, D_MODEL), F32)] * 3,
        compiler_params=_params("parallel", "arbitrary"), name="ffn_bwd_weights",
    )(h1, dgate, dup, act, dx2)


def _outproj_bwd_call(dx1, attn, pool_y, u, w_pool, b_pool, pool_scale, w_out, exchange=None):
    t = dx1.shape[0]
    tm = _token_tile(t)

    def body(dx1_ref, a_ref, y_ref, u_ref, halo_ref, wp_ref, bp_ref, ps_ref, wo_ref,
             dattn_ref, dmixed_ref, gwo_ref, gbo_ref, gwp_ref, gbp_ref, gps_ref, sc_ref):
        i = pl.program_id(0)

        @pl.when(i == 0)
        def _():
            for ref in (gwo_ref, gbo_ref, gwp_ref, gbp_ref, gps_ref):
                ref[...] = jnp.zeros_like(ref)

        dx1 = dx1_ref[...]
        dx1b = dx1.astype(BF16)
        dcat = _dot_nt(dx1b, wo_ref[...])
        dattn_ref[...] = dcat[:, :ATTN_WIDTH].astype(BF16)
        for g, mixed in enumerate(_pool_mixed(sc_ref, u_ref, halo_ref, i, tm)):
            lanes = slice(LANES * g, LANES * (g + 1))
            mixed_b = mixed.astype(BF16)
            lin = _dot(mixed_b, wp_ref[g]) + bp_ref[:, lanes]
            dpool = dcat[:, ATTN_WIDTH + LANES * g:ATTN_WIDTH + LANES * (g + 1)]
            gps_ref[:, lanes] += jnp.sum(dpool * lin, axis=0, keepdims=True)
            dy = dpool * ps_ref[:, lanes]
            gbp_ref[:, lanes] += jnp.sum(dy, axis=0, keepdims=True)
            dy_b = dy.astype(BF16)
            gwp_ref[g] += _dot_tn(mixed_b, dy_b)
            dmixed_ref[:, lanes] = _dot_nt(dy_b, wp_ref[g])
        gwo_ref[0:ATTN_WIDTH, :] += _dot_tn(a_ref[...], dx1b)
        gwo_ref[ATTN_WIDTH:, :] += _dot_tn(y_ref[...], dx1b)
        gbo_ref[...] += jnp.sum(dx1, axis=0, keepdims=True)

    row = lambda w: pl.BlockSpec((tm, w), lambda i: (i, 0))
    const = lambda shape: pl.BlockSpec(shape, lambda i: (0,) * len(shape))
    return _compute_call(
        body, (dx1, attn, pool_y, u, u, w_pool, b_pool, pool_scale, w_out), grid=(t // tm,),
        in_specs=[row(D_MODEL), row(ATTN_WIDTH), row(POOL_WIDTH), row(POOL_WIDTH), _halo_before(tm, POOL_WIDTH),
                  _resident((4, LANES, LANES)), _resident((1, POOL_WIDTH)), _resident((1, POOL_WIDTH)),
                  _resident((D_MODEL, D_MODEL))],
        out_specs=[row(ATTN_WIDTH), row(POOL_WIDTH), const((D_MODEL, D_MODEL)), const((1, D_MODEL)),
                   const((4, LANES, LANES)), const((1, POOL_WIDTH)), const((1, POOL_WIDTH))],
        out_shape=[jax.ShapeDtypeStruct((t, ATTN_WIDTH), BF16), jax.ShapeDtypeStruct((t, POOL_WIDTH), F32),
                   jax.ShapeDtypeStruct((D_MODEL, D_MODEL), F32), jax.ShapeDtypeStruct((1, D_MODEL), F32),
                   jax.ShapeDtypeStruct((4, LANES, LANES), F32), jax.ShapeDtypeStruct((1, POOL_WIDTH), F32),
                   jax.ShapeDtypeStruct((1, POOL_WIDTH), F32)],
        scratch_shapes=[pltpu.VMEM((tm + POOL_HALO, POOL_WIDTH), F32)],
        semantics=("arbitrary",), name="outproj_bwd", exchange=exchange)


def _attn_bwd_call(q, k, v, dout, sinks, cos_t, sin_t, bias, bias_first, exchange=None):
    t = q.shape[0]
    nb = t // BLOCK
    gb = _blocks_per_step(nb)
    qb = gb * BLOCK
    steps = nb // gb

    def body(sink_ref, q_ref, kc_ref, kp_ref, vc_ref, vp_ref, do_ref, cq_ref, sq_ref, ck_ref, sk_ref, b_ref, bf_ref,
             dq_ref, dk_ref, dv_ref, dsink_ref, newk_ref, newv_ref, carryk_ref, carryv_ref, headk_ref, headv_ref):
        n = pl.program_id(0)
        first = _first_half_mask((BLOCK, LANES))

        @pl.when(n == 0)
        def _():
            dsink_ref[...] = jnp.zeros_like(dsink_ref)

        @pl.when(n < steps)
        def _():
            head_row = lax.broadcasted_iota(jnp.int32, dsink_ref.shape, 0)
            lane2 = lax.broadcasted_iota(jnp.int32, (2 * BLOCK, LANES), 1)
            row_low = lax.broadcasted_iota(jnp.int32, (LANES, BLOCK), 0) < HEAD_DIM
            dsink = jnp.zeros(dsink_ref.shape, F32)
            for g in range(gb):
                r0 = g * BLOCK
                kband = _kv_band(kc_ref, kp_ref, g)
                vband = _kv_band(vc_ref, vp_ref, g)
                kband_t = kband.T
                bias_g = jnp.where(n == 0, bf_ref[...], b_ref[...]) if g == 0 else b_ref[...]
                cq, sq = cq_ref[r0:r0 + BLOCK, :], sq_ref[r0:r0 + BLOCK, :]
                dk_heads, dv_heads = [], []
                for kh in range(2):
                    k2 = _head_in_both_halves(kband, kh)
                    v2 = _head_in_both_halves(vband, kh)
                    kt_head = kband_t[HEAD_DIM * kh:HEAD_DIM * (kh + 1), :]
                    k2_t = jnp.concatenate([kt_head, kt_head], axis=0).astype(BF16)
                    qst = _stack_heads(q_ref, kh, r0)
                    dost = _stack_heads(do_ref, kh, r0)
                    s_t = _dot_nt(k2, qst) + bias_g
                    sink = _sink_row(sink_ref, kh)
                    m = jnp.maximum(jnp.max(s_t, axis=0, keepdims=True), sink)
                    e = jnp.exp(s_t - m)
                    e_sink = jnp.exp(sink - m)
                    inv = 1.0 / (jnp.sum(e, axis=0, keepdims=True) + e_sink)
                    p_t = e * inv
                    dp_t = _dot_nt(v2, dost)
                    delta = jnp.sum(p_t * dp_t, axis=0, keepdims=True)
                    ds_t = (p_t * (dp_t - delta)).astype(BF16)
                    sink_term = e_sink * inv * delta
                    for h in range(4):
                        val = -jnp.sum(sink_term[:, BLOCK * h:BLOCK * (h + 1)])
                        dsink = dsink + jnp.where(head_row == 4 * kh + h, val, 0.0)
                    dq_t = _dot(k2_t, ds_t)
                    for jj in range(2):
                        slab = jnp.where(row_low, dq_t[:, BLOCK * 2 * jj:BLOCK * (2 * jj + 1)],
                                         dq_t[:, BLOCK * (2 * jj + 1):BLOCK * (2 * jj + 2)]).T
                        dq_ref[r0:r0 + BLOCK, LANES * (2 * kh + jj):LANES * (2 * kh + jj + 1)] = (
                            (slab * cq + _rope_partner(slab * sq, first)) * Q_SCALE).astype(BF16)
                    dk2 = _dot(ds_t, qst)
                    dv2 = _dot(p_t.astype(BF16), dost)
                    dk_heads.append(dk2 + pltpu.roll(dk2, HEAD_DIM, 1))
                    dv_heads.append(dv2 + pltpu.roll(dv2, HEAD_DIM, 1))
                bandk = jnp.where(lane2 < HEAD_DIM, dk_heads[0], dk_heads[1])
                bandv = jnp.where(lane2 < HEAD_DIM, dv_heads[0], dv_heads[1])
                if g == 0:
                    headk_ref[...] = bandk[0:BLOCK]
                    headv_ref[...] = bandv[0:BLOCK]
                else:
                    newk_ref[r0 - BLOCK:r0, :] += bandk[0:BLOCK]
                    newv_ref[r0 - BLOCK:r0, :] += bandv[0:BLOCK]
                newk_ref[r0:r0 + BLOCK, :] = bandk[BLOCK:]
                newv_ref[r0:r0 + BLOCK, :] = bandv[BLOCK:]
            dsink_ref[...] += dsink

        @pl.when(n == steps)
        def _():
            headk_ref[...] = jnp.zeros_like(headk_ref)
            headv_ref[...] = jnp.zeros_like(headv_ref)

        @pl.when(n >= 1)
        def _():
            last = qb - BLOCK
            carryk_ref[last:, :] += headk_ref[...]
            carryv_ref[last:, :] += headv_ref[...]
            for g in range(gb):
                rows = slice(g * BLOCK, (g + 1) * BLOCK)
                dk = carryk_ref[rows, :]
                dk_ref[rows, :] = (dk * ck_ref[rows, :] + _rope_partner(dk * sk_ref[rows, :], first)).astype(BF16)
            dv_ref[...] = carryv_ref[...].astype(BF16)

        @pl.when(n < steps)
        def _():
            carryk_ref[...] = newk_ref[...]
            carryv_ref[...] = newv_ref[...]

    cur = lambda w: pl.BlockSpec((qb, w), lambda n: (jnp.minimum(n, steps - 1), 0))
    late = lambda w: pl.BlockSpec((qb, w), lambda n: (jnp.maximum(n - 1, 0), 0))
    prev = lambda w: pl.BlockSpec((BLOCK, w), lambda n: (jnp.maximum(jnp.minimum(n, steps - 1) * gb - 1, 0), 0))
    return _compute_call(
        body, (sinks, q, k, k, v, v, dout, cos_t, sin_t, cos_t, sin_t, bias, bias_first), grid=(steps + 1,),
        in_specs=[pl.BlockSpec(memory_space=pltpu.SMEM), cur(ATTN_WIDTH), cur(KV_WIDTH), prev(KV_WIDTH),
                  cur(KV_WIDTH), prev(KV_WIDTH), cur(ATTN_WIDTH), cur(LANES), cur(LANES), late(LANES), late(LANES),
                  _resident(bias.shape), _resident(bias.shape)],
        out_specs=[cur(ATTN_WIDTH), late(KV_WIDTH), late(KV_WIDTH), pl.BlockSpec((8, LANES), lambda n: (0, 0))],
        out_shape=[jax.ShapeDtypeStruct((t, ATTN_WIDTH), BF16), jax.ShapeDtypeStruct((t, KV_WIDTH), BF16),
                   jax.ShapeDtypeStruct((t, KV_WIDTH), BF16), jax.ShapeDtypeStruct((8, LANES), F32)],
        scratch_shapes=[pltpu.VMEM((qb, LANES), F32), pltpu.VMEM((qb, LANES), F32),
                        pltpu.VMEM((qb, LANES), F32), pltpu.VMEM((qb, LANES), F32),
                        pltpu.VMEM((BLOCK, LANES), F32), pltpu.VMEM((BLOCK, LANES), F32)],
        semantics=("arbitrary",), name="attn_bwd", exchange=exchange)


def _inproj_bwd_call(dq, dk, dv, dmixed, x, h0, dx1, w_in, g_mix, exchange=None):
    t = x.shape[0]
    tm = _token_tile(t)
    n_tiles = t // tm

    def body(dq_ref, dk_ref, dv_ref, dm_ref, halo_ref, x_ref, h_ref, dx1_ref, w_ref, g_ref,
             dx_ref, gw_ref, gb_ref, gg_ref, sc_ref):
        i = pl.program_id(0)

        @pl.when(i == 0)
        def _():
            for ref in (gw_ref, gb_ref, gg_ref):
                ref[...] = jnp.zeros_like(ref)

        tok = i * tm + lax.broadcasted_iota(jnp.int32, (tm, 1), 0)
        halo = jnp.where(i == n_tiles - 1, 0.0, halo_ref[...])
        du = []
        for g, size in enumerate(POOL_SIZES):
            lanes = slice(LANES * g, LANES * (g + 1))
            dm = dm_ref[:, lanes]
            sc_ref[0:tm, lanes] = dm / jnp.minimum(tok + 1, size).astype(F32)
            sc_ref[tm:, lanes] = halo[:, lanes] / float(size)
            acc = -dm
            for j in range(size):
                acc = acc + sc_ref[pl.ds(j, tm), lanes]
            du.append(acc)
        dz32 = jnp.concatenate([dq_ref[...].astype(F32), dk_ref[...].astype(F32), dv_ref[...].astype(F32)] + du,
                               axis=1)
        dz = dz32.astype(BF16)
        gb_ref[...] += jnp.sum(dz32, axis=0, keepdims=True)
        gw_ref[...] += _dot_tn(dz, h_ref[...])
        dh = _dot(dz, w_ref[...])
        xh, r = _rms_stats(x_ref[...])
        gg_ref[...] += jnp.sum(dh * xh, axis=0, keepdims=True)
        dx_ref[...] = dx1_ref[...] + _rms_bwd(dh, xh, r, g_ref[...])

    row = lambda w: pl.BlockSpec((tm, w), lambda i: (i, 0))
    const = lambda shape: pl.BlockSpec(shape, lambda i: (0,) * len(shape))
    last_halo = t // POOL_HALO - 1
    halo_after = pl.BlockSpec((POOL_HALO, POOL_WIDTH),
                              lambda i: (jnp.minimum((i + 1) * (tm // POOL_HALO), last_halo), 0))
    return _compute_call(
        body, (dq, dk, dv, dmixed, dmixed, x, h0, dx1, w_in, g_mix), grid=(n_tiles,),
        in_specs=[row(ATTN_WIDTH), row(KV_WIDTH), row(KV_WIDTH), row(POOL_WIDTH), halo_after, row(D_MODEL),
                  row(D_MODEL), row(D_MODEL), _resident((IN_WIDTH, D_MODEL)), _resident((1, D_MODEL))],
        out_specs=[row(D_MODEL), const((IN_WIDTH, D_MODEL)), const((1, IN_WIDTH)), const((1, D_MODEL))],
        out_shape=[jax.ShapeDtypeStruct((t, D_MODEL), F32), jax.ShapeDtypeStruct((IN_WIDTH, D_MODEL), F32),
                   jax.ShapeDtypeStruct((1, IN_WIDTH), F32), jax.ShapeDtypeStruct((1, D_MODEL), F32)],
        scratch_shapes=[pltpu.VMEM((tm + POOL_HALO, POOL_WIDTH), F32)],
        semantics=("arbitrary",), name="inproj_bwd", exchange=exchange)


def _reduce_to_owner(grads, place, hosts, wire_dtypes):
    me, c = place
    landed = hosts[0](_sibling_exchange(grads))
    partials = _pair_sum_call(grads, landed, c, wire_dtypes)
    slots = hosts[1](_chip_exchange(partials))
    return hosts[2](_sibling_allgather(_chip_sum_call(partials, slots, me, c)))


def _local_step(x, target, small, w_in, w_out, w_ffn, place=None):
    t = x.shape[0]
    on_mesh = place is not None
    cos_t, sin_t = _rope_tables(t)
    w_pool = small["w_pool"].astype(BF16)
    w_ffn = list(w_ffn)

    def gathering(i):
        return _gather_exchange([w_ffn[i]]) if on_mesh else None

    def gathered(i, results):
        if on_mesh:
            w_ffn[i] = results[0]

    (h0, q, k, v, u), got = _inproj_call(x, small["g_mix"], w_in, small["b_in"], cos_t, sin_t, exchange=gathering(0))
    gathered(0, got)
    bias_t, bias_first_t = _band_biases()
    (attn,), got = _attn_fwd_call(q, k, v, small["sinks"], bias_t, bias_first_t, exchange=gathering(1))
    gathered(1, got)
    (x1, h1, pool_y), got = _outproj_call(attn, u, x, w_pool, small["b_pool"], small["pool_scale"], w_out,
                                          small["b_out"], small["g_ffn"], exchange=gathering(2))
    gathered(2, got)
    w_gate, w_up, w_down = w_ffn
    gate, up, act, dx2, loss, g_g_final = _ffn_fwd_call(h1, x1, target, w_gate, w_up, w_down, small["g_final"])
    dgate, dup, dx1, g_g_ffn = _ffn_bwd_act_call(dx2, gate, up, x1, w_gate, w_up, w_down, small["g_ffn"])
    ffn_grads = list(_ffn_bwd_weights_call(h1, dgate, dup, act, dx2))
    kept = {}

    def outproj_bwd(exchange=None):
        kept["outproj"], results = _outproj_bwd_call(dx1, attn, pool_y, u, w_pool, small["b_pool"],
                                                     small["pool_scale"], w_out, exchange=exchange)
        return results

    def attn_bwd(exchange=None):
        kept["attn"], results = _attn_bwd_call(q, k, v, kept["outproj"][0], small["sinks"], cos_t, sin_t, bias_t,
                                               bias_first_t, exchange=exchange)
        return results

    def inproj_bwd(exchange=None):
        dq, dk, dv, _ = kept["attn"]
        kept["inproj"], results = _inproj_bwd_call(dq, dk, dv, kept["outproj"][1], x, h0, dx1, w_in, small["g_mix"],
                                                   exchange=exchange)
        return results

    if on_mesh:
        ffn_grads = _reduce_to_owner(ffn_grads, place, (outproj_bwd, attn_bwd, inproj_bwd), [BF16] * 3)
    else:
        outproj_bwd(), attn_bwd(), inproj_bwd()
    _, _, g_w_out, g_b_out, g_w_pool, g_b_pool, g_pool_scale = kept["outproj"]
    g_sinks = kept["attn"][3]
    grad_x, g_w_in, g_b_in, g_g_mix = kept["inproj"]
    small_grads = dict(g_mix=g_g_mix, b_in=g_b_in, sinks=g_sinks[:, 0].reshape(1, N_Q_HEADS), w_pool=g_w_pool,
                       b_pool=g_b_pool, pool_scale=g_pool_scale, b_out=g_b_out, g_ffn=g_g_ffn, g_final=g_g_final)
    big_grads = dict(w_in=g_w_in, w_out=g_w_out, w_gate=ffn_grads[0], w_up=ffn_grads[1], w_down=ffn_grads[2])
    return loss, grad_x, small_grads, big_grads


def _cast_place_call(shards, me):
    na = len(shards)
    split = 2

    def body(me_ref, *refs):
        for a in range(na):
            refs[na + a][...] = refs[a][...].astype(BF16)

    in_specs = [pl.BlockSpec((s.shape[0] // split, s.shape[1]), lambda i, me_ref: (i, 0)) for s in shards]
    out_specs = [pl.BlockSpec((None, s.shape[0] // split, s.shape[1]), lambda i, me_ref: (me_ref[0], i, 0))
                 for s in shards]
    return pl.pallas_call(
        body,
        grid_spec=pltpu.PrefetchScalarGridSpec(num_scalar_prefetch=1, grid=(split,), in_specs=in_specs,
                                               out_specs=out_specs),
        out_shape=[jax.ShapeDtypeStruct((N_SHARDS,) + s.shape, BF16) for s in shards],
        compiler_params=_params("parallel"), name="cast_place_weights",
    )(me.reshape(1), *shards)


def _pair_sum_call(grads, landed, c, wire_dtypes):
    na = len(grads)
    split = 2

    def body(c_ref, *refs):
        ins, lands, outs = refs[:na], refs[na:2 * na], refs[2 * na:]
        for a in range(na):
            outs[a][...] = (ins[a][...] + lands[a][...]).astype(outs[a].dtype)

    in_specs, land_specs, out_specs, views = [], [], [], []
    for g in grads:
        ns, r, cols = g.shape
        rows = r // 2 // split
        views.append(g.reshape(ns, 2, r // 2, cols))
        in_specs.append(pl.BlockSpec((None, None, rows, cols), lambda s, i, c_ref: (s, c_ref[0], i, 0)))
        land_specs.append(pl.BlockSpec((None, rows, cols), lambda s, i, c_ref: (s, i, 0)))
        out_specs.append(pl.BlockSpec((None, rows, cols), lambda s, i, c_ref: (s, i, 0)))
    return pl.pallas_call(
        body,
        grid_spec=pltpu.PrefetchScalarGridSpec(num_scalar_prefetch=1, grid=(N_SHARDS, split),
                                               in_specs=in_specs + land_specs, out_specs=out_specs),
        out_shape=[jax.ShapeDtypeStruct(l.shape, dt) for l, dt in zip(landed, wire_dtypes)],
        compiler_params=_params("parallel", "parallel"), name="reduce_pair_sum",
    )(c.reshape(1), *views, *landed)


def _chip_sum_call(partials, slots, me, c):
    na = len(slots)
    split = 2

    def body(place_ref, *refs):
        mine, lands, outs = refs[:na], refs[na:2 * na], refs[2 * na:]
        for j in range(N_SHARDS):
            @pl.when(place_ref[0] == j)
            def _():
                for a in range(na):
                    terms = [mine[a][...] if k == j else lands[a][k] for k in range(N_SHARDS)]
                    t0, t1, t2, t3 = (term.astype(F32) for term in terms)
                    outs[a][...] = ((t0 + t1) + t2) + t3

    mine_specs, land_specs, out_specs = [], [], []
    for s in slots:
        ns, h, cols = s.shape
        mine_specs.append(pl.BlockSpec((None, h // split, cols), lambda i, place: (place[0], i, 0)))
        land_specs.append(pl.BlockSpec((ns, h // split, cols), lambda i, place: (0, i, 0)))
        out_specs.append(pl.BlockSpec((h // split, cols), lambda i, place: (place[1] * split + i, 0)))
    return pl.pallas_call(
        body,
        grid_spec=pltpu.PrefetchScalarGridSpec(num_scalar_prefetch=1, grid=(split,),
                                               in_specs=mine_specs + land_specs, out_specs=out_specs),
        out_shape=[jax.ShapeDtypeStruct((2 * s.shape[1], s.shape[2]), F32) for s in slots],
        compiler_params=_params("parallel"), name="reduce_chip_sum",
    )(jnp.stack([me, c]), *partials, *slots)


def _adamw_call(ws, gs, ms, vs):
    na = len(ws)
    split = 4

    def body(*refs):
        w_refs, g_refs, m_refs, v_refs = (refs[k * na:(k + 1) * na] for k in range(4))
        d_refs, nm_refs, nv_refs = (refs[(4 + k) * na:(5 + k) * na] for k in range(3))
        for a in range(na):
            g = g_refs[a][...]
            m = ADAM_B1 * m_refs[a][...] + (1.0 - ADAM_B1) * g
            v = ADAM_B2 * v_refs[a][...] + (1.0 - ADAM_B2) * jnp.square(g)
            m_hat = m / (1.0 - ADAM_B1 ** ADAM_STEP)
            v_hat = v / (1.0 - ADAM_B2 ** ADAM_STEP)
            d_refs[a][...] = -ADAM_LR * (m_hat / (jnp.sqrt(v_hat) + ADAM_EPS) + ADAM_WD * w_refs[a][...])
            nm_refs[a][...] = m
            nv_refs[a][...] = v

    specs = [pl.BlockSpec((w.shape[0] // split, w.shape[1]), lambda i: (i, 0)) for w in ws]
    shapes = [jax.ShapeDtypeStruct(w.shape, F32) for w in ws]
    outs = pl.pallas_call(
        body, grid=(split,), in_specs=specs * 4, out_specs=specs * 3, out_shape=shapes * 3,
        compiler_params=_params("parallel"), name="adamw",
    )(*ws, *gs, *ms, *vs)
    return outs[:na], outs[na:2 * na], outs[2 * na:]


def _small_layout(shapes):
    layout, off = {}, 0
    for name in SMALL_NAMES:
        size = 1
        for d in shapes[name]:
            size *= d
        layout[name] = (off, size)
        off += size
    rows = -(-off // (LANES * 64)) * 64
    return layout, rows


def _pack_small(arrays, rows):
    flat = jnp.concatenate([arrays[name].reshape(-1) for name in SMALL_NAMES])
    return jnp.pad(flat, (0, rows * LANES - flat.shape[0])).reshape(rows, LANES)


def _unpack_small(packed, layout, shapes):
    flat = packed.reshape(-1)
    return {name: flat[off:off + size].reshape(shapes[name]) for name, (off, size) in layout.items()}


def kernel(x, g_mix, w_in, b_in, sinks, w_pool, b_pool, pool_scale, w_out, b_out, g_ffn, w_gate, w_up, w_down, g_final, loss_target, m_g_mix, m_w_in, m_b_in, m_sinks, m_w_pool, m_b_pool, m_pool_scale, m_w_out, m_b_out, m_g_ffn, m_w_gate, m_w_up, m_w_down, m_g_final, v_g_mix, v_w_in, v_b_in, v_sinks, v_w_pool, v_b_pool, v_pool_scale, v_w_out, v_b_out, v_g_ffn, v_w_gate, v_w_up, v_w_down, v_g_final):
    given = dict(locals())
    big_names = ("w_in", "w_out", "w_gate", "w_up", "w_down")
    out_shapes = {n: given[n].shape for n in SMALL_NAMES + big_names}
    t = x.shape[1]
    c = lax.axis_index("c")

    me = 2 * lax.axis_index("x") + lax.axis_index("y")
    transposed = ("w_in", "w_gate", "w_up")

    def shard_view(a, name):
        return a[0].T if name in transposed else a[0]

    placed = _cast_place_call([shard_view(given[n], n) for n in big_names], me)
    g_in, g_out = _exchange_call(_gather_exchange(placed[:2]), "gather_attention_weights")
    w_in_full = g_in.reshape(IN_WIDTH, D_MODEL)
    w_out_full = g_out.reshape(D_MODEL, D_MODEL)

    small = dict(g_mix=g_mix, b_in=b_in, sinks=sinks, w_pool=w_pool[0], b_pool=b_pool.reshape(1, POOL_WIDTH),
                 pool_scale=pool_scale.reshape(1, POOL_WIDTH), b_out=b_out, g_ffn=g_ffn,
                 g_final=g_final.reshape(1, D_MODEL))
    loss, grad_x, small_grads, big_grads = _local_step(
        x[0], loss_target[0], small, w_in_full, w_out_full, placed[2:], place=(me, c))
    loss = lax.psum(loss[0, 0], ("x", "y", "c"))

    layout, small_rows = _small_layout(out_shapes)
    packed = _pack_small(small_grads, small_rows)
    grads = [
        big_grads["w_in"].reshape(N_SHARDS, IN_WIDTH // N_SHARDS, D_MODEL),
        big_grads["w_out"].reshape(N_SHARDS, D_MODEL // N_SHARDS, D_MODEL),
        jnp.broadcast_to(packed[None], (N_SHARDS,) + packed.shape),
    ]
    alone = [functools.partial(_exchange_call, name=name) for name in
             ("reduce_sibling_exchange", "reduce_chip_exchange", "reduce_sibling_allgather")]
    t_in, t_out, t_small = _reduce_to_owner(grads, (me, c), alone, [BF16, BF16, F32])
    totals = [t_in, t_out, big_grads["w_gate"], big_grads["w_up"], big_grads["w_down"], t_small]

    def owned(prefix):
        return ([shard_view(given[prefix + n], n) for n in big_names]
                + [_pack_small({n: given[prefix + n] for n in SMALL_NAMES}, small_rows)])

    ws, ms, vs = owned(""), owned("m_"), owned("v_")
    deltas, new_ms, new_vs = _adamw_call(ws, totals, ms, vs)

    def unpack(arrays):
        out = _unpack_small(arrays[-1], layout, out_shapes)
        for n, a in zip(big_names, arrays[:-1]):
            out[n] = (a.T if n in transposed else a).reshape(out_shapes[n])
        return out

    order = ("g_mix", "w_in", "b_in", "sinks", "w_pool", "b_pool", "pool_scale", "w_out", "b_out", "g_ffn",
             "w_gate", "w_up", "w_down", "g_final")
    results = [loss, grad_x.reshape(x.shape)]
    for group in (totals, deltas, new_ms, new_vs):
        named = unpack(list(group))
        results += [named[n] for n in order]
    return tuple(results)
```

```python
import functools
from typing import Any, Callable, Mapping, NamedTuple, Sequence

import jax
import jax.numpy as jnp
from jax import lax
from jax.experimental import pallas as pl
from jax.experimental.pallas import tpu as pltpu

F32 = jnp.float32
BF16 = jnp.bfloat16

D_MODEL = 1024
ATTN_WIDTH = 512
KV_WIDTH = 128
POOL_WIDTH = 512
IN_WIDTH = ATTN_WIDTH + 2 * KV_WIDTH + POOL_WIDTH
HEAD_DIM = 64
N_Q_HEADS = 8
BLOCK = 128
POOL_SIZES = (2, 4, 8, 16)
POOL_HALO = 16
ROPE_THETA = 10000.0
RMS_EPS = 1e-5
Q_SCALE = HEAD_DIM ** -0.5

ADAM_LR = 0.001
ADAM_B1 = 0.9
ADAM_B2 = 0.999
ADAM_EPS = 1e-08
ADAM_WD = 0.01
ADAM_STEP = 10

N_SHARDS = 4
LANES = 128
VMEM_LIMIT = 56 * 1024 * 1024
FFN_TOKEN_TILE = 256

MESH = pl.DeviceIdType.MESH
HBM_SPEC = pl.BlockSpec(memory_space=pltpu.HBM)

SMALL_NAMES = ("g_mix", "b_in", "sinks", "w_pool", "b_pool", "pool_scale", "b_out", "g_ffn", "g_final")


def _dot(a, b):
    return jnp.dot(a, b, preferred_element_type=F32)


def _dot_nt(a, b):
    return lax.dot_general(a, b, (((1,), (1,)), ((), ())), preferred_element_type=F32)


def _dot_tn(a, b):
    return lax.dot_general(a, b, (((0,), (0,)), ((), ())), preferred_element_type=F32)


def _params(*semantics):
    return pltpu.CompilerParams(dimension_semantics=semantics, vmem_limit_bytes=VMEM_LIMIT)


def _resident(shape):
    return pl.BlockSpec(shape, lambda *_: (0,) * len(shape), pipeline_mode=pl.Buffered(1))


def _token_tile(t, largest=512):
    for tm in (1024, 512, 256, 128):
        if tm <= largest and t % tm == 0:
            return tm
    raise ValueError(f"sequence length {t} is not a multiple of 128")


def _rms_stats(xf):
    r = lax.rsqrt(jnp.mean(xf * xf, axis=-1, keepdims=True) + RMS_EPS)
    return xf * r, r


def _rms_bwd(dy, xh, r, g):
    dxh = dy * g
    return r * (dxh - xh * jnp.mean(dxh * xh, axis=-1, keepdims=True))


def _rope_partner(t, first):
    return jnp.where(first, pltpu.roll(t, LANES - HEAD_DIM // 2, 1), pltpu.roll(t, HEAD_DIM // 2, 1))


def _first_half_mask(shape):
    lane = lax.broadcasted_iota(jnp.int32, shape, 1)
    return (lane % HEAD_DIM) < (HEAD_DIM // 2)


def _rope_tables(t):
    inv_freq = 1.0 / (ROPE_THETA ** (jnp.arange(0, HEAD_DIM, 2, dtype=F32) / HEAD_DIM))
    ang = jnp.arange(t, dtype=F32)[:, None] * inv_freq[None, :]
    cos, sin = jnp.cos(ang), jnp.sin(ang)
    cos_t = jnp.tile(jnp.concatenate([cos, cos], axis=-1), (1, LANES // HEAD_DIM))
    sin_t = jnp.tile(jnp.concatenate([-sin, sin], axis=-1), (1, LANES // HEAD_DIM))
    return cos_t, sin_t


class _Exchange(NamedTuple):
    inputs: Sequence[Any]
    out_shapes: Sequence[Any]
    aliases: Mapping[int, int]
    sems: Sequence[Any]
    start: Callable[..., None]
    finish: Callable[..., None]


def _compute_call(body, operands, *, grid, in_specs, out_specs, out_shape, scratch_shapes=(), semantics, name,
                  exchange=None):
    n_in, n_out, n_scr = len(in_specs), len(out_specs), len(scratch_shapes)
    if exchange is None:
        return pl.pallas_call(body, grid=grid, in_specs=in_specs, out_specs=out_specs, out_shape=out_shape,
                              scratch_shapes=scratch_shapes, compiler_params=_params(*semantics),
                              name=name)(*operands), []
    k_in, k_out = len(exchange.inputs), len(exchange.out_shapes)

    def hosting(*refs):
        ins, rest = refs[:n_in], refs[n_in:]
        ex_ins, rest = rest[:k_in], rest[k_in:]
        outs, rest = rest[:n_out], rest[n_out:]
        ex_outs, rest = rest[:k_out], rest[k_out:]
        scratch, sems = rest[:n_scr], rest[n_scr:]
        steps = [pl.program_id(d) for d in range(len(grid))]
        is_first = functools.reduce(jnp.logical_and, [s == 0 for s in steps])
        is_last = functools.reduce(jnp.logical_and, [s == g - 1 for s, g in zip(steps, grid)])

        @pl.when(is_first)
        def _():
            exchange.start(ex_ins, ex_outs, sems)

        body(*ins, *outs, *scratch)

        @pl.when(is_last)
        def _():
            exchange.finish(ex_ins, ex_outs, sems)

    results = pl.pallas_call(
        hosting, grid=grid,
        in_specs=list(in_specs) + [HBM_SPEC] * k_in, out_specs=list(out_specs) + [HBM_SPEC] * k_out,
        out_shape=list(out_shape) + list(exchange.out_shapes),
        scratch_shapes=list(scratch_shapes) + list(exchange.sems),
        input_output_aliases={n_in + i: n_out + j for i, j in exchange.aliases.items()},
        compiler_params=_params(*semantics), name=name,
    )(*operands, *exchange.inputs)
    return results[:n_out], results[n_out:]


def _exchange_call(exchange, name):
    n_in, n_out = len(exchange.inputs), len(exchange.out_shapes)

    def body(*refs):
        ins, outs, sems = refs[:n_in], refs[n_in:n_in + n_out], refs[n_in + n_out:]
        exchange.start(ins, outs, sems)
        exchange.finish(ins, outs, sems)

    return pl.pallas_call(
        body, in_specs=[HBM_SPEC] * n_in, out_specs=[HBM_SPEC] * n_out, out_shape=list(exchange.out_shapes),
        input_output_aliases=dict(exchange.aliases), scratch_shapes=list(exchange.sems),
        compiler_params=pltpu.CompilerParams(has_side_effects=True), name=name,
    )(*exchange.inputs)


def _mesh_place():
    x, y, c = lax.axis_index("x"), lax.axis_index("y"), lax.axis_index("c")
    other_chips = [(1 - x, y), (x, 1 - y), (1 - x, 1 - y)]
    return x, y, c, other_chips


def _half(ref, c, rows):
    return ref.at[pl.ds(pl.multiple_of(c * rows, 16), rows), :]


def _remote(src, dst, send_sems, recv_sems, idx, to):
    return functools.partial(pltpu.make_async_remote_copy, src_ref=src, dst_ref=dst, send_sem=send_sems.at[idx],
                             recv_sem=recv_sems.at[idx], device_id=to, device_id_type=MESH)


def _gather_exchange(bufs):
    na = len(bufs)

    def copies(outs, sems):
        send_sems, recv_sems = sems
        x, y, c, other_chips = _mesh_place()
        me, sibling = 2 * x + y, (x, y, 1 - c)
        ici_out, ici_in, fwd_out, fwd_in = [], [], [], []
        for j, (cx, cy) in enumerate(other_chips):
            for a in range(na):
                rows = outs[a].shape[1] // 2
                mine = _half(outs[a].at[me], c, rows)
                landed = _half(outs[a].at[2 * cx + cy], c, rows)
                passed = _half(outs[a].at[2 * cx + cy], 1 - c, rows)
                ici_out.append(_remote(mine, mine, send_sems, recv_sems, j * na + a, (cx, cy, c)))
                ici_in.append(_remote(landed, landed, send_sems, recv_sems, j * na + a, (cx, cy, c)))
                fwd_out.append(_remote(landed, landed, send_sems, recv_sems, (3 + j) * na + a, sibling))
                fwd_in.append(_remote(passed, passed, send_sems, recv_sems, (3 + j) * na + a, sibling))
        return ici_out, ici_in, fwd_out, fwd_in

    def start(ins, outs, sems):
        for cp in copies(outs, sems)[0]:
            cp().start()

    def finish(ins, outs, sems):
        ici_out, ici_in, fwd_out, fwd_in = copies(outs, sems)
        for arrived, forward in zip(ici_in, fwd_out):
            arrived().wait_recv()
            forward().start()
        for cp in fwd_in:
            cp().wait_recv()
        for cp in ici_out + fwd_out:
            cp().wait_send()

    return _Exchange(inputs=bufs, out_shapes=[jax.ShapeDtypeStruct(b.shape, b.dtype) for b in bufs],
                     aliases={a: a for a in range(na)},
                     sems=[pltpu.SemaphoreType.DMA((6 * na,)), pltpu.SemaphoreType.DMA((6 * na,))],
                     start=start, finish=finish)


def _sibling_exchange(grads):
    na = len(grads)

    def copies(ins, outs, sems):
        x, y, c, _ = _mesh_place()
        out = []
        for a in range(na):
            rows = ins[a].shape[1] // 2
            src = ins[a].at[:, pl.ds(pl.multiple_of((1 - c) * rows, 8), rows), :]
            out.append(_remote(src, outs[a], sems[0], sems[1], a, (x, y, 1 - c)))
        return out

    def start(ins, outs, sems):
        for cp in copies(ins, outs, sems):
            cp().start()

    def finish(ins, outs, sems):
        for cp in copies(ins, outs, sems):
            cp().wait_recv()
            cp().wait_send()

    return _Exchange(inputs=grads,
                     out_shapes=[jax.ShapeDtypeStruct((g.shape[0], g.shape[1] // 2, g.shape[2]), g.dtype)
                                 for g in grads],
                     aliases={}, sems=[pltpu.SemaphoreType.DMA((na,)), pltpu.SemaphoreType.DMA((na,))],
                     start=start, finish=finish)


def _chip_exchange(partials):
    na = len(partials)

    def copies(ins, outs, sems):
        x, y, c, other_chips = _mesh_place()
        me = 2 * x + y
        sends, arrivals = [], []
        for j, (cx, cy) in enumerate(other_chips):
            for a in range(na):
                slot = outs[a].at[2 * cx + cy]
                sends.append(_remote(ins[a].at[2 * cx + cy], outs[a].at[me], sems[0], sems[1], j * na + a, (cx, cy, c)))
                arrivals.append(_remote(slot, slot, sems[0], sems[1], j * na + a, (cx, cy, c)))
        return sends, arrivals

    def start(ins, outs, sems):
        for cp in copies(ins, outs, sems)[0]:
            cp().start()

    def finish(ins, outs, sems):
        sends, arrivals = copies(ins, outs, sems)
        for cp in arrivals:
            cp().wait_recv()
        for cp in sends:
            cp().wait_send()

    return _Exchange(inputs=partials, out_shapes=[jax.ShapeDtypeStruct(p.shape, p.dtype) for p in partials],
                     aliases={}, sems=[pltpu.SemaphoreType.DMA((3 * na,)), pltpu.SemaphoreType.DMA((3 * na,))],
                     start=start, finish=finish)


def _sibling_allgather(bufs):
    na = len(bufs)

    def piece(outs, sems, a, half):
        x, y, c, _ = _mesh_place()
        rows = outs[a].shape[0] // 2
        rows_ref = outs[a].at[pl.ds(pl.multiple_of(half * rows, 8), rows), :]
        return _remote(rows_ref, rows_ref, sems[0], sems[1], a, (x, y, 1 - c))

    def start(ins, outs, sems):
        c = lax.axis_index("c")
        for a in range(na):
            piece(outs, sems, a, c)().start()

    def finish(ins, outs, sems):
        c = lax.axis_index("c")
        for a in range(na):
            piece(outs, sems, a, 1 - c)().wait_recv()
        for a in range(na):
            piece(outs, sems, a, c)().wait_send()

    return _Exchange(inputs=bufs, out_shapes=[jax.ShapeDtypeStruct(b.shape, b.dtype) for b in bufs],
                     aliases={a: a for a in range(na)},
                     sems=[pltpu.SemaphoreType.DMA((na,)), pltpu.SemaphoreType.DMA((na,))],
                     start=start, finish=finish)


def _inproj_call(x, g_mix, w_in, b_in, cos_t, sin_t, exchange=None):
    t = x.shape[0]
    tm = _token_tile(t)

    def body(x_ref, g_ref, w_ref, b_ref, c_ref, s_ref, h_ref, q_ref, k_ref, v_ref, u_ref):
        xh, _ = _rms_stats(x_ref[...])
        h = (xh * g_ref[...]).astype(BF16)
        h_ref[...] = h
        z = _dot_nt(h, w_ref[...]) + b_ref[...]
        cos, sin = c_ref[...], s_ref[...]
        first = _first_half_mask((tm, LANES))
        for j in range(ATTN_WIDTH // LANES):
            zj = z[:, LANES * j:LANES * (j + 1)]
            q_ref[:, LANES * j:LANES * (j + 1)] = ((zj * cos + _rope_partner(zj, first) * sin) * Q_SCALE).astype(BF16)
        zk = z[:, ATTN_WIDTH:ATTN_WIDTH + KV_WIDTH]
        k_ref[...] = (zk * cos + _rope_partner(zk, first) * sin).astype(BF16)
        v_ref[...] = z[:, ATTN_WIDTH + KV_WIDTH:ATTN_WIDTH + 2 * KV_WIDTH].astype(BF16)
        u_ref[...] = z[:, ATTN_WIDTH + 2 * KV_WIDTH:]

    row = lambda w: pl.BlockSpec((tm, w), lambda i: (i, 0))
    return _compute_call(
        body, (x, g_mix, w_in, b_in, cos_t, sin_t), grid=(t // tm,),
        in_specs=[row(D_MODEL), _resident((1, D_MODEL)), _resident((IN_WIDTH, D_MODEL)), _resident((1, IN_WIDTH)),
                  row(LANES), row(LANES)],
        out_specs=[row(D_MODEL), row(ATTN_WIDTH), row(KV_WIDTH), row(KV_WIDTH), row(POOL_WIDTH)],
        out_shape=[jax.ShapeDtypeStruct((t, D_MODEL), BF16), jax.ShapeDtypeStruct((t, ATTN_WIDTH), BF16),
                   jax.ShapeDtypeStruct((t, KV_WIDTH), BF16), jax.ShapeDtypeStruct((t, KV_WIDTH), BF16),
                   jax.ShapeDtypeStruct((t, POOL_WIDTH), F32)],
        semantics=("parallel",), name="inproj_fwd", exchange=exchange)


def _head_in_both_halves(band, kv_head):
    lane = lax.broadcasted_iota(jnp.int32, band.shape, 1)
    own = (lane < HEAD_DIM) if kv_head == 0 else (lane >= HEAD_DIM)
    return jnp.where(own, band, pltpu.roll(band, HEAD_DIM, 1)).astype(BF16)


def _stack_heads(ref, kv_head, row0):
    low = lax.broadcasted_iota(jnp.int32, (BLOCK, LANES), 1) < HEAD_DIM
    parts = []
    for jj in range(2):
        slab = ref[row0:row0 + BLOCK, LANES * (2 * kv_head + jj):LANES * (2 * kv_head + jj + 1)]
        zero = jnp.zeros_like(slab)
        parts += [jnp.where(low, slab, zero), jnp.where(low, zero, slab)]
    return jnp.concatenate(parts, axis=0)


def _band_biases():
    c = lax.broadcasted_iota(jnp.int32, (2 * BLOCK, 4 * BLOCK), 0)
    i = lax.broadcasted_iota(jnp.int32, (2 * BLOCK, 4 * BLOCK), 1) % BLOCK
    vis = (c > i) & (c <= i + BLOCK)
    return jnp.where(vis, 0.0, -jnp.inf).astype(F32), jnp.where(vis & (c >= BLOCK), 0.0, -jnp.inf).astype(F32)


def _blocks_per_step(nb):
    for g in (4, 2, 1):
        if nb % g == 0:
            return g


def _sink_row(sink_ref, kv_head):
    return jnp.concatenate([jnp.full((1, BLOCK), sink_ref[0, 4 * kv_head + g], F32) for g in range(4)], axis=1)


def _kv_band(cur_ref, prev_ref, g):
    r0 = g * BLOCK
    prev = prev_ref[...] if g == 0 else cur_ref[r0 - BLOCK:r0, :]
    return jnp.concatenate([prev, cur_ref[r0:r0 + BLOCK, :]], axis=0).astype(F32)


def _attn_fwd_call(q, k, v, sinks, bias, bias_first, exchange=None):
    t = q.shape[0]
    nb = t // BLOCK
    gb = _blocks_per_step(nb)
    qb = gb * BLOCK

    def body(sink_ref, q_ref, kc_ref, kp_ref, vc_ref, vp_ref, b_ref, bf_ref, o_ref):
        n = pl.program_id(0)
        row_low = lax.broadcasted_iota(jnp.int32, (LANES, BLOCK), 0) < HEAD_DIM
        for g in range(gb):
            r0 = g * BLOCK
            kband = _kv_band(kc_ref, kp_ref, g)
            vband_t = _kv_band(vc_ref, vp_ref, g).T
            bias_g = jnp.where(n == 0, bf_ref[...], b_ref[...]) if g == 0 else b_ref[...]
            for kh in range(2):
                k2 = _head_in_both_halves(kband, kh)
                vt_head = vband_t[HEAD_DIM * kh:HEAD_DIM * (kh + 1), :]
                v2_t = jnp.concatenate([vt_head, vt_head], axis=0).astype(BF16)
                s_t = _dot_nt(k2, _stack_heads(q_ref, kh, r0)) + bias_g
                sink = _sink_row(sink_ref, kh)
                m = jnp.maximum(jnp.max(s_t, axis=0, keepdims=True), sink)
                e = jnp.exp(s_t - m)
                inv = 1.0 / (jnp.sum(e, axis=0, keepdims=True) + jnp.exp(sink - m))
                o_t = _dot(v2_t, (e * inv).astype(BF16))
                for jj in range(2):
                    slab = jnp.where(row_low, o_t[:, BLOCK * 2 * jj:BLOCK * (2 * jj + 1)],
                                     o_t[:, BLOCK * (2 * jj + 1):BLOCK * (2 * jj + 2)]).T
                    o_ref[r0:r0 + BLOCK, LANES * (2 * kh + jj):LANES * (2 * kh + jj + 1)] = slab.astype(BF16)

    cur = lambda w: pl.BlockSpec((qb, w), lambda n: (n, 0))
    prev = lambda w: pl.BlockSpec((BLOCK, w), lambda n: (jnp.maximum(n * gb - 1, 0), 0))
    return _compute_call(
        body, (sinks, q, k, k, v, v, bias, bias_first), grid=(nb // gb,),
        in_specs=[pl.BlockSpec(memory_space=pltpu.SMEM), cur(ATTN_WIDTH), cur(KV_WIDTH), prev(KV_WIDTH),
                  cur(KV_WIDTH), prev(KV_WIDTH), _resident(bias.shape), _resident(bias.shape)],
        out_specs=[cur(ATTN_WIDTH)],
        out_shape=[jax.ShapeDtypeStruct((t, ATTN_WIDTH), BF16)],
        semantics=("parallel",), name="attn_fwd", exchange=exchange)


def _pool_mixed(sc_ref, u_ref, halo_ref, i, tm):
    sc_ref[0:POOL_HALO, :] = jnp.where(i == 0, 0.0, halo_ref[...])
    sc_ref[POOL_HALO:, :] = u_ref[...]
    tok = i * tm + lax.broadcasted_iota(jnp.int32, (tm, 1), 0)
    mixed = []
    for g, size in enumerate(POOL_SIZES):
        lanes = slice(LANES * g, LANES * (g + 1))
        cur = sc_ref[pl.ds(POOL_HALO, tm), lanes]
        acc = cur
        for j in range(1, size):
            acc = acc + sc_ref[pl.ds(POOL_HALO - j, tm), lanes]
        count = jnp.minimum(tok + 1, size).astype(F32)
        mixed.append(acc / count - cur)
    return mixed


def _halo_before(tm, width):
    return pl.BlockSpec((POOL_HALO, width), lambda i: (jnp.maximum(i * (tm // POOL_HALO) - 1, 0), 0))


def _outproj_call(attn, u, x, w_pool, b_pool, pool_scale, w_out, b_out, g_ffn, exchange=None):
    t = x.shape[0]
    tm = _token_tile(t)

    def body(a_ref, u_ref, halo_ref, x_ref, wp_ref, bp_ref, ps_ref, wo_ref, bo_ref, g_ref,
             x1_ref, h1_ref, y_ref, sc_ref):
        i = pl.program_id(0)
        for g, mixed in enumerate(_pool_mixed(sc_ref, u_ref, halo_ref, i, tm)):
            lanes = slice(LANES * g, LANES * (g + 1))
            lin = _dot(mixed.astype(BF16), wp_ref[g]) + bp_ref[:, lanes]
            y_ref[:, lanes] = (lin * ps_ref[:, lanes]).astype(BF16)
        x1 = (x_ref[...] + _dot(a_ref[...], wo_ref[0:ATTN_WIDTH, :]) + _dot(y_ref[...], wo_ref[ATTN_WIDTH:, :])
              + bo_ref[...])
        x1_ref[...] = x1
        xh, _ = _rms_stats(x1)
        h1_ref[...] = (xh * g_ref[...]).astype(BF16)

    row = lambda w: pl.BlockSpec((tm, w), lambda i: (i, 0))
    return _compute_call(
        body, (attn, u, u, x, w_pool, b_pool, pool_scale, w_out, b_out, g_ffn), grid=(t // tm,),
        in_specs=[row(ATTN_WIDTH), row(POOL_WIDTH), _halo_before(tm, POOL_WIDTH), row(D_MODEL),
                  _resident((4, LANES, LANES)), _resident((1, POOL_WIDTH)), _resident((1, POOL_WIDTH)),
                  _resident((D_MODEL, D_MODEL)), _resident((1, D_MODEL)), _resident((1, D_MODEL))],
        out_specs=[row(D_MODEL), row(D_MODEL), row(POOL_WIDTH)],
        out_shape=[jax.ShapeDtypeStruct((t, D_MODEL), F32), jax.ShapeDtypeStruct((t, D_MODEL), BF16),
                   jax.ShapeDtypeStruct((t, POOL_WIDTH), BF16)],
        scratch_shapes=[pltpu.VMEM((tm + POOL_HALO, POOL_WIDTH), F32)],
        semantics=("parallel",), name="outproj_fwd", exchange=exchange)


def _ffn_fwd_call(h1, x1, target, w_gate, w_up, w_down, g_final):
    t = x1.shape[0]
    tm = _token_tile(t, FFN_TOKEN_TILE)
    ns, fs, _ = w_gate.shape

    def body(h_ref, x1_ref, tgt_ref, wg_ref, wu_ref, wd_ref, gf_ref,
             gate_ref, up_ref, act_ref, dx2_ref, loss_ref, dgf_ref):
        @pl.when(pl.program_id(0) == 0)
        def _():
            loss_ref[...] = jnp.zeros_like(loss_ref)
            dgf_ref[...] = jnp.zeros_like(dgf_ref)

        h = h_ref[...]
        x2 = x1_ref[...]
        for s in range(ns):
            gate = _dot_nt(h, wg_ref[s])
            up = _dot_nt(h, wu_ref[s])
            act = (gate * jax.nn.sigmoid(gate) * up).astype(BF16)
            gate_ref[s] = gate.astype(BF16)
            up_ref[s] = up.astype(BF16)
            act_ref[s] = act
            x2 = x2 + _dot(act, wd_ref[s])
        xh, r = _rms_stats(x2)
        gf = gf_ref[...]
        err = xh * gf - tgt_ref[...]
        loss_ref[...] += 0.5 * jnp.sum(err * err) / D_MODEL
        dy = err / D_MODEL
        dgf_ref[...] += jnp.sum(dy * xh, axis=0, keepdims=True)
        dx2_ref[...] = _rms_bwd(dy, xh, r, gf)

    row = lambda w: pl.BlockSpec((tm, w), lambda i: (i, 0))
    shard_act = pl.BlockSpec((ns, tm, fs), lambda i: (0, i, 0))
    act_shape = jax.ShapeDtypeStruct((ns, t, fs), BF16)
    return pl.pallas_call(
        body, grid=(t // tm,),
        in_specs=[row(D_MODEL), row(D_MODEL), row(D_MODEL), _resident((ns, fs, D_MODEL)),
                  _resident((ns, fs, D_MODEL)), _resident((ns, fs, D_MODEL)), _resident((1, D_MODEL))],
        out_specs=[shard_act, shard_act, shard_act, row(D_MODEL),
                   pl.BlockSpec((1, LANES), lambda i: (0, 0)), pl.BlockSpec((1, D_MODEL), lambda i: (0, 0))],
        out_shape=[act_shape, act_shape, act_shape, jax.ShapeDtypeStruct((t, D_MODEL), F32),
                   jax.ShapeDtypeStruct((1, LANES), F32), jax.ShapeDtypeStruct((1, D_MODEL), F32)],
        compiler_params=_params("arbitrary"), name="ffn_fwd",
    )(h1, x1, target, w_gate, w_up, w_down, g_final)


def _ffn_bwd_act_call(dx2, gate, up, x1, w_gate, w_up, w_down, g_ffn):
    t = x1.shape[0]
    tm = _token_tile(t, FFN_TOKEN_TILE)
    ns, fs, _ = w_gate.shape

    def body(dx2_ref, gate_ref, up_ref, x1_ref, wg_ref, wu_ref, wd_ref, g_ref,
             dgate_ref, dup_ref, dx1_ref, dg_ref):
        @pl.when(pl.program_id(0) == 0)
        def _():
            dg_ref[...] = jnp.zeros_like(dg_ref)

        dx2 = dx2_ref[...]
        dx2b = dx2.astype(BF16)
        dh = None
        for s in range(ns):
            dact = _dot_nt(dx2b, wd_ref[s])
            gate = gate_ref[s].astype(F32)
            sg = jax.nn.sigmoid(gate)
            dup = (dact * (gate * sg)).astype(BF16)
            dgate = (dact * up_ref[s].astype(F32) * (sg * (1.0 + gate * (1.0 - sg)))).astype(BF16)
            dgate_ref[s] = dgate
            dup_ref[s] = dup
            part = _dot(dgate, wg_ref[s]) + _dot(dup, wu_ref[s])
            dh = part if dh is None else dh + part
        xh, r = _rms_stats(x1_ref[...])
        dg_ref[...] += jnp.sum(dh * xh, axis=0, keepdims=True)
        dx1_ref[...] = dx2 + _rms_bwd(dh, xh, r, g_ref[...])

    row = lambda w: pl.BlockSpec((tm, w), lambda i: (i, 0))
    shard_act = pl.BlockSpec((ns, tm, fs), lambda i: (0, i, 0))
    act_shape = jax.ShapeDtypeStruct((ns, t, fs), BF16)
    return pl.pallas_call(
        body, grid=(t // tm,),
        in_specs=[row(D_MODEL), shard_act, shard_act, row(D_MODEL), _resident((ns, fs, D_MODEL)),
                  _resident((ns, fs, D_MODEL)), _resident((ns, fs, D_MODEL)), _resident((1, D_MODEL))],
        out_specs=[shard_act, shard_act, row(D_MODEL), pl.BlockSpec((1, D_MODEL), lambda i: (0, 0))],
        out_shape=[act_shape, act_shape, jax.ShapeDtypeStruct((t, D_MODEL), F32),
                   jax.ShapeDtypeStruct((1, D_MODEL), F32)],
        compiler_params=_params("arbitrary"), name="ffn_bwd_act",
    )(dx2, gate, up, x1, w_gate, w_up, w_down, g_ffn)


def _ffn_bwd_weights_call(h1, dgate, dup, act, dx2):
    t = h1.shape[0]
    tm = _token_tile(t, 1024)
    ns, _, fs = dgate.shape

    def body(h_ref, dgate_ref, dup_ref, act_ref, dx2_ref, gg_ref, gu_ref, gd_ref):
        @pl.when(pl.program_id(1) == 0)
        def _():
            gg_ref[...] = jnp.zeros_like(gg_ref)
            gu_ref[...] = jnp.zeros_like(gu_ref)
            gd_ref[...] = jnp.zeros_like(gd_ref)

        h = h_ref[...]
        gg_ref[...] += _dot_tn(dgate_ref[...], h)
        gu_ref[...] += _dot_tn(dup_ref[...], h)
        gd_ref[...] += _dot_tn(act_ref[...], dx2_ref[...].astype(BF16))

    row = lambda w: pl.BlockSpec((tm, w), lambda s, i: (i, 0))
    shard_act = pl.BlockSpec((None, tm, fs), lambda s, i: (s, i, 0))
    return pl.pallas_call(
        body, grid=(ns, t // tm),
        in_specs=[row(D_MODEL), shard_act, shard_act, shard_act, row(D_MODEL)],
        out_specs=[pl.BlockSpec((None, fs, D_MODEL), lambda s, i: (s, 0, 0))] * 3,
        out_shape=[jax.ShapeDtypeStruct((ns, fs, D_MODEL), F32)] * 3,
        compiler_params=_params("parallel", "arbitrary"), name="ffn_bwd_weights",
    )(h1, dgate, dup, act, dx2)


def _outproj_bwd_call(dx1, attn, pool_y, u, w_pool, b_pool, pool_scale, w_out, exchange=None):
    t = dx1.shape[0]
    tm = _token_tile(t)

    def body(dx1_ref, a_ref, y_ref, u_ref, halo_ref, wp_ref, bp_ref, ps_ref, wo_ref,
             dattn_ref, dmixed_ref, gwo_ref, gbo_ref, gwp_ref, gbp_ref, gps_ref, sc_ref):
        i = pl.program_id(0)

        @pl.when(i == 0)
        def _():
            for ref in (gwo_ref, gbo_ref, gwp_ref, gbp_ref, gps_ref):
                ref[...] = jnp.zeros_like(ref)

        dx1 = dx1_ref[...]
        dx1b = dx1.astype(BF16)
        dcat = _dot_nt(dx1b, wo_ref[...])
        dattn_ref[...] = dcat[:, :ATTN_WIDTH].astype(BF16)
        for g, mixed in enumerate(_pool_mixed(sc_ref, u_ref, halo_ref, i, tm)):
            lanes = slice(LANES * g, LANES * (g + 1))
            mixed_b = mixed.astype(BF16)
            lin = _dot(mixed_b, wp_ref[g]) + bp_ref[:, lanes]
            dpool = dcat[:, ATTN_WIDTH + LANES * g:ATTN_WIDTH + LANES * (g + 1)]
            gps_ref[:, lanes] += jnp.sum(dpool * lin, axis=0, keepdims=True)
            dy = dpool * ps_ref[:, lanes]
            gbp_ref[:, lanes] += jnp.sum(dy, axis=0, keepdims=True)
            dy_b = dy.astype(BF16)
            gwp_ref[g] += _dot_tn(mixed_b, dy_b)
            dmixed_ref[:, lanes] = _dot_nt(dy_b, wp_ref[g])
        gwo_ref[0:ATTN_WIDTH, :] += _dot_tn(a_ref[...], dx1b)
        gwo_ref[ATTN_WIDTH:, :] += _dot_tn(y_ref[...], dx1b)
        gbo_ref[...] += jnp.sum(dx1, axis=0, keepdims=True)

    row = lambda w: pl.BlockSpec((tm, w), lambda i: (i, 0))
    const = lambda shape: pl.BlockSpec(shape, lambda i: (0,) * len(shape))
    return _compute_call(
        body, (dx1, attn, pool_y, u, u, w_pool, b_pool, pool_scale, w_out), grid=(t // tm,),
        in_specs=[row(D_MODEL), row(ATTN_WIDTH), row(POOL_WIDTH), row(POOL_WIDTH), _halo_before(tm, POOL_WIDTH),
                  _resident((4, LANES, LANES)), _resident((1, POOL_WIDTH)), _resident((1, POOL_WIDTH)),
                  _resident((D_MODEL, D_MODEL))],
        out_specs=[row(ATTN_WIDTH), row(POOL_WIDTH), const((D_MODEL, D_MODEL)), const((1, D_MODEL)),
                   const((4, LANES, LANES)), const((1, POOL_WIDTH)), const((1, POOL_WIDTH))],
        out_shape=[jax.ShapeDtypeStruct((t, ATTN_WIDTH), BF16), jax.ShapeDtypeStruct((t, POOL_WIDTH), F32),
                   jax.ShapeDtypeStruct((D_MODEL, D_MODEL), F32), jax.ShapeDtypeStruct((1, D_MODEL), F32),
                   jax.ShapeDtypeStruct((4, LANES, LANES), F32), jax.ShapeDtypeStruct((1, POOL_WIDTH), F32),
                   jax.ShapeDtypeStruct((1, POOL_WIDTH), F32)],
        scratch_shapes=[pltpu.VMEM((tm + POOL_HALO, POOL_WIDTH), F32)],
        semantics=("arbitrary",), name="outproj_bwd", exchange=exchange)


def _attn_bwd_call(q, k, v, dout, sinks, cos_t, sin_t, bias, bias_first, exchange=None):
    t = q.shape[0]
    nb = t // BLOCK
    gb = _blocks_per_step(nb)
    qb = gb * BLOCK
    steps = nb // gb

    def body(sink_ref, q_ref, kc_ref, kp_ref, vc_ref, vp_ref, do_ref, cq_ref, sq_ref, ck_ref, sk_ref, b_ref, bf_ref,
             dq_ref, dk_ref, dv_ref, dsink_ref, newk_ref, newv_ref, carryk_ref, carryv_ref, headk_ref, headv_ref):
        n = pl.program_id(0)
        first = _first_half_mask((BLOCK, LANES))

        @pl.when(n == 0)
        def _():
            dsink_ref[...] = jnp.zeros_like(dsink_ref)

        @pl.when(n < steps)
        def _():
            head_row = lax.broadcasted_iota(jnp.int32, dsink_ref.shape, 0)
            lane2 = lax.broadcasted_iota(jnp.int32, (2 * BLOCK, LANES), 1)
            row_low = lax.broadcasted_iota(jnp.int32, (LANES, BLOCK), 0) < HEAD_DIM
            dsink = jnp.zeros(dsink_ref.shape, F32)
            for g in range(gb):
                r0 = g * BLOCK
                kband = _kv_band(kc_ref, kp_ref, g)
                vband = _kv_band(vc_ref, vp_ref, g)
                kband_t = kband.T
                bias_g = jnp.where(n == 0, bf_ref[...], b_ref[...]) if g == 0 else b_ref[...]
                cq, sq = cq_ref[r0:r0 + BLOCK, :], sq_ref[r0:r0 + BLOCK, :]
                dk_heads, dv_heads = [], []
                for kh in range(2):
                    k2 = _head_in_both_halves(kband, kh)
                    v2 = _head_in_both_halves(vband, kh)
                    kt_head = kband_t[HEAD_DIM * kh:HEAD_DIM * (kh + 1), :]
                    k2_t = jnp.concatenate([kt_head, kt_head], axis=0).astype(BF16)
                    qst = _stack_heads(q_ref, kh, r0)
                    dost = _stack_heads(do_ref, kh, r0)
                    s_t = _dot_nt(k2, qst) + bias_g
                    sink = _sink_row(sink_ref, kh)
                    m = jnp.maximum(jnp.max(s_t, axis=0, keepdims=True), sink)
                    e = jnp.exp(s_t - m)
                    e_sink = jnp.exp(sink - m)
                    inv = 1.0 / (jnp.sum(e, axis=0, keepdims=True) + e_sink)
                    p_t = e * inv
                    dp_t = _dot_nt(v2, dost)
                    delta = jnp.sum(p_t * dp_t, axis=0, keepdims=True)
                    ds_t = (p_t * (dp_t - delta)).astype(BF16)
                    sink_term = e_sink * inv * delta
                    for h in range(4):
                        val = -jnp.sum(sink_term[:, BLOCK * h:BLOCK * (h + 1)])
                        dsink = dsink + jnp.where(head_row == 4 * kh + h, val, 0.0)
                    dq_t = _dot(k2_t, ds_t)
                    for jj in range(2):
                        slab = jnp.where(row_low, dq_t[:, BLOCK * 2 * jj:BLOCK * (2 * jj + 1)],
                                         dq_t[:, BLOCK * (2 * jj + 1):BLOCK * (2 * jj + 2)]).T
                        dq_ref[r0:r0 + BLOCK, LANES * (2 * kh + jj):LANES * (2 * kh + jj + 1)] = (
                            (slab * cq + _rope_partner(slab * sq, first)) * Q_SCALE).astype(BF16)
                    dk2 = _dot(ds_t, qst)
                    dv2 = _dot(p_t.astype(BF16), dost)
                    dk_heads.append(dk2 + pltpu.roll(dk2, HEAD_DIM, 1))
                    dv_heads.append(dv2 + pltpu.roll(dv2, HEAD_DIM, 1))
                bandk = jnp.where(lane2 < HEAD_DIM, dk_heads[0], dk_heads[1])
                bandv = jnp.where(lane2 < HEAD_DIM, dv_heads[0], dv_heads[1])
                if g == 0:
                    headk_ref[...] = bandk[0:BLOCK]
                    headv_ref[...] = bandv[0:BLOCK]
                else:
                    newk_ref[r0 - BLOCK:r0, :] += bandk[0:BLOCK]
                    newv_ref[r0 - BLOCK:r0, :] += bandv[0:BLOCK]
                newk_ref[r0:r0 + BLOCK, :] = bandk[BLOCK:]
                newv_ref[r0:r0 + BLOCK, :] = bandv[BLOCK:]
            dsink_ref[...] += dsink

        @pl.when(n == steps)
        def _():
            headk_ref[...] = jnp.zeros_like(headk_ref)
            headv_ref[...] = jnp.zeros_like(headv_ref)

        @pl.when(n >= 1)
        def _():
            last = qb - BLOCK
            carryk_ref[last:, :] += headk_ref[...]
            carryv_ref[last:, :] += headv_ref[...]
            for g in range(gb):
                rows = slice(g * BLOCK, (g + 1) * BLOCK)
                dk = carryk_ref[rows, :]
                dk_ref[rows, :] = (dk * ck_ref[rows, :] + _rope_partner(dk * sk_ref[rows, :], first)).astype(BF16)
            dv_ref[...] = carryv_ref[...].astype(BF16)

        @pl.when(n < steps)
        def _():
            carryk_ref[...] = newk_ref[...]
            carryv_ref[...] = newv_ref[...]

    cur = lambda w: pl.BlockSpec((qb, w), lambda n: (jnp.minimum(n, steps - 1), 0))
    late = lambda w: pl.BlockSpec((qb, w), lambda n: (jnp.maximum(n - 1, 0), 0))
    prev = lambda w: pl.BlockSpec((BLOCK, w), lambda n: (jnp.maximum(jnp.minimum(n, steps - 1) * gb - 1, 0), 0))
    return _compute_call(
        body, (sinks, q, k, k, v, v, dout, cos_t, sin_t, cos_t, sin_t, bias, bias_first), grid=(steps + 1,),
        in_specs=[pl.BlockSpec(memory_space=pltpu.SMEM), cur(ATTN_WIDTH), cur(KV_WIDTH), prev(KV_WIDTH),
                  cur(KV_WIDTH), prev(KV_WIDTH), cur(ATTN_WIDTH), cur(LANES), cur(LANES), late(LANES), late(LANES),
                  _resident(bias.shape), _resident(bias.shape)],
        out_specs=[cur(ATTN_WIDTH), late(KV_WIDTH), late(KV_WIDTH), pl.BlockSpec((8, LANES), lambda n: (0, 0))],
        out_shape=[jax.ShapeDtypeStruct((t, ATTN_WIDTH), BF16), jax.ShapeDtypeStruct((t, KV_WIDTH), BF16),
                   jax.ShapeDtypeStruct((t, KV_WIDTH), BF16), jax.ShapeDtypeStruct((8, LANES), F32)],
        scratch_shapes=[pltpu.VMEM((qb, LANES), F32), pltpu.VMEM((qb, LANES), F32),
                        pltpu.VMEM((qb, LANES), F32), pltpu.VMEM((qb, LANES), F32),
                        pltpu.VMEM((BLOCK, LANES), F32), pltpu.VMEM((BLOCK, LANES), F32)],
        semantics=("arbitrary",), name="attn_bwd", exchange=exchange)


def _inproj_bwd_call(dq, dk, dv, dmixed, x, h0, dx1, w_in, g_mix, exchange=None):
    t = x.shape[0]
    tm = _token_tile(t)
    n_tiles = t // tm

    def body(dq_ref, dk_ref, dv_ref, dm_ref, halo_ref, x_ref, h_ref, dx1_ref, w_ref, g_ref,
             dx_ref, gw_ref, gb_ref, gg_ref, sc_ref):
        i = pl.program_id(0)

        @pl.when(i == 0)
        def _():
            for ref in (gw_ref, gb_ref, gg_ref):
                ref[...] = jnp.zeros_like(ref)

        tok = i * tm + lax.broadcasted_iota(jnp.int32, (tm, 1), 0)
        halo = jnp.where(i == n_tiles - 1, 0.0, halo_ref[...])
        du = []
        for g, size in enumerate(POOL_SIZES):
            lanes = slice(LANES * g, LANES * (g + 1))
            dm = dm_ref[:, lanes]
            sc_ref[0:tm, lanes] = dm / jnp.minimum(tok + 1, size).astype(F32)
            sc_ref[tm:, lanes] = halo[:, lanes] / float(size)
            acc = -dm
            for j in range(size):
                acc = acc + sc_ref[pl.ds(j, tm), lanes]
            du.append(acc)
        dz32 = jnp.concatenate([dq_ref[...].astype(F32), dk_ref[...].astype(F32), dv_ref[...].astype(F32)] + du,
                               axis=1)
        dz = dz32.astype(BF16)
        gb_ref[...] += jnp.sum(dz32, axis=0, keepdims=True)
        gw_ref[...] += _dot_tn(dz, h_ref[...])
        dh = _dot(dz, w_ref[...])
        xh, r = _rms_stats(x_ref[...])
        gg_ref[...] += jnp.sum(dh * xh, axis=0, keepdims=True)
        dx_ref[...] = dx1_ref[...] + _rms_bwd(dh, xh, r, g_ref[...])

    row = lambda w: pl.BlockSpec((tm, w), lambda i: (i, 0))
    const = lambda shape: pl.BlockSpec(shape, lambda i: (0,) * len(shape))
    last_halo = t // POOL_HALO - 1
    halo_after = pl.BlockSpec((POOL_HALO, POOL_WIDTH),
                              lambda i: (jnp.minimum((i + 1) * (tm // POOL_HALO), last_halo), 0))
    return _compute_call(
        body, (dq, dk, dv, dmixed, dmixed, x, h0, dx1, w_in, g_mix), grid=(n_tiles,),
        in_specs=[row(ATTN_WIDTH), row(KV_WIDTH), row(KV_WIDTH), row(POOL_WIDTH), halo_after, row(D_MODEL),
                  row(D_MODEL), row(D_MODEL), _resident((IN_WIDTH, D_MODEL)), _resident((1, D_MODEL))],
        out_specs=[row(D_MODEL), const((IN_WIDTH, D_MODEL)), const((1, IN_WIDTH)), const((1, D_MODEL))],
        out_shape=[jax.ShapeDtypeStruct((t, D_MODEL), F32), jax.ShapeDtypeStruct((IN_WIDTH, D_MODEL), F32),
                   jax.ShapeDtypeStruct((1, IN_WIDTH), F32), jax.ShapeDtypeStruct((1, D_MODEL), F32)],
        scratch_shapes=[pltpu.VMEM((tm + POOL_HALO, POOL_WIDTH), F32)],
        semantics=("arbitrary",), name="inproj_bwd", exchange=exchange)


def _reduce_to_owner(grads, place, hosts, wire_dtypes):
    me, c = place
    landed = hosts[0](_sibling_exchange(grads))
    partials = _pair_sum_call(grads, landed, c, wire_dtypes)
    slots = hosts[1](_chip_exchange(partials))
    return _chip_sum_call(partials, slots, me, c)


def _local_step(x, target, small, w_in, w_out, w_ffn, place=None):
    t = x.shape[0]
    on_mesh = place is not None
    cos_t, sin_t = _rope_tables(t)
    w_pool = small["w_pool"].astype(BF16)
    w_ffn = list(w_ffn)

    def gathering(i):
        return _gather_exchange([w_ffn[i]]) if on_mesh else None

    def gathered(i, results):
        if on_mesh:
            w_ffn[i] = results[-1]

    first = _gather_exchange([w_out, w_ffn[0]]) if on_mesh else None
    (h0, q, k, v, u), got = _inproj_call(x, small["g_mix"], w_in, small["b_in"], cos_t, sin_t, exchange=first)
    gathered(0, got)
    if on_mesh:
        w_out = got[0].reshape(D_MODEL, D_MODEL)
    bias_t, bias_first_t = _band_biases()
    (attn,), got = _attn_fwd_call(q, k, v, small["sinks"], bias_t, bias_first_t, exchange=gathering(1))
    gathered(1, got)
    (x1, h1, pool_y), got = _outproj_call(attn, u, x, w_pool, small["b_pool"], small["pool_scale"], w_out,
                                          small["b_out"], small["g_ffn"], exchange=gathering(2))
    gathered(2, got)
    w_gate, w_up, w_down = w_ffn
    gate, up, act, dx2, loss, g_g_final = _ffn_fwd_call(h1, x1, target, w_gate, w_up, w_down, small["g_final"])
    dgate, dup, dx1, g_g_ffn = _ffn_bwd_act_call(dx2, gate, up, x1, w_gate, w_up, w_down, small["g_ffn"])
    ffn_grads = list(_ffn_bwd_weights_call(h1, dgate, dup, act, dx2))
    kept = {}

    def outproj_bwd(exchange=None):
        kept["outproj"], results = _outproj_bwd_call(dx1, attn, pool_y, u, w_pool, small["b_pool"],
                                                     small["pool_scale"], w_out, exchange=exchange)
        return results

    def attn_bwd(exchange=None):
        kept["attn"], results = _attn_bwd_call(q, k, v, kept["outproj"][0], small["sinks"], cos_t, sin_t, bias_t,
                                               bias_first_t, exchange=exchange)
        return results

    def inproj_bwd(exchange=None):
        dq, dk, dv, _ = kept["attn"]
        kept["inproj"], results = _inproj_bwd_call(dq, dk, dv, kept["outproj"][1], x, h0, dx1, w_in, small["g_mix"],
                                                   exchange=exchange)
        return results

    if on_mesh:
        ffn_grads = _reduce_to_owner(ffn_grads, place, (outproj_bwd, attn_bwd), [BF16] * 3)
    else:
        outproj_bwd(), attn_bwd()
    inproj_bwd()
    _, _, g_w_out, g_b_out, g_w_pool, g_b_pool, g_pool_scale = kept["outproj"]
    g_sinks = kept["attn"][3]
    grad_x, g_w_in, g_b_in, g_g_mix = kept["inproj"]
    small_grads = dict(g_mix=g_g_mix, b_in=g_b_in, sinks=g_sinks[:, 0].reshape(1, N_Q_HEADS), w_pool=g_w_pool,
                       b_pool=g_b_pool, pool_scale=g_pool_scale, b_out=g_b_out, g_ffn=g_g_ffn, g_final=g_g_final)
    big_grads = dict(w_in=g_w_in, w_out=g_w_out, w_gate=ffn_grads[0], w_up=ffn_grads[1], w_down=ffn_grads[2])
    return loss, grad_x, small_grads, big_grads


def _cast_place_call(shards, me):
    na = len(shards)
    split = 2

    def body(me_ref, *refs):
        for a in range(na):
            refs[na + a][...] = refs[a][...].astype(BF16)

    in_specs = [pl.BlockSpec((s.shape[0] // split, s.shape[1]), lambda i, me_ref: (i, 0)) for s in shards]
    out_specs = [pl.BlockSpec((None, s.shape[0] // split, s.shape[1]), lambda i, me_ref: (me_ref[0], i, 0))
                 for s in shards]
    return pl.pallas_call(
        body,
        grid_spec=pltpu.PrefetchScalarGridSpec(num_scalar_prefetch=1, grid=(split,), in_specs=in_specs,
                                               out_specs=out_specs),
        out_shape=[jax.ShapeDtypeStruct((N_SHARDS,) + s.shape, BF16) for s in shards],
        compiler_params=_params("parallel"), name="cast_place_weights",
    )(me.reshape(1), *shards)


def _pair_sum_call(grads, landed, c, wire_dtypes):
    na = len(grads)
    split = 2

    def body(c_ref, *refs):
        ins, lands, outs = refs[:na], refs[na:2 * na], refs[2 * na:]
        for a in range(na):
            outs[a][...] = (ins[a][...] + lands[a][...]).astype(outs[a].dtype)

    in_specs, land_specs, out_specs, views = [], [], [], []
    for g in grads:
        ns, r, cols = g.shape
        rows = r // 2 // split
        views.append(g.reshape(ns, 2, r // 2, cols))
        in_specs.append(pl.BlockSpec((None, None, rows, cols), lambda s, i, c_ref: (s, c_ref[0], i, 0)))
        land_specs.append(pl.BlockSpec((None, rows, cols), lambda s, i, c_ref: (s, i, 0)))
        out_specs.append(pl.BlockSpec((None, rows, cols), lambda s, i, c_ref: (s, i, 0)))
    return pl.pallas_call(
        body,
        grid_spec=pltpu.PrefetchScalarGridSpec(num_scalar_prefetch=1, grid=(N_SHARDS, split),
                                               in_specs=in_specs + land_specs, out_specs=out_specs),
        out_shape=[jax.ShapeDtypeStruct(l.shape, dt) for l, dt in zip(landed, wire_dtypes)],
        compiler_params=_params("parallel", "parallel"), name="reduce_pair_sum",
    )(c.reshape(1), *views, *landed)


def _chip_sum_call(partials, slots, me, c):
    na = len(slots)
    split = 2

    def body(place_ref, *refs):
        mine, lands, outs = refs[:na], refs[na:2 * na], refs[2 * na:]
        for j in range(N_SHARDS):
            @pl.when(place_ref[0] == j)
            def _():
                for a in range(na):
                    terms = [mine[a][...] if k == j else lands[a][k] for k in range(N_SHARDS)]
                    t0, t1, t2, t3 = (term.astype(F32) for term in terms)
                    outs[a][...] = ((t0 + t1) + t2) + t3

    mine_specs, land_specs, out_specs = [], [], []
    for s in slots:
        ns, h, cols = s.shape
        mine_specs.append(pl.BlockSpec((None, h // split, cols), lambda i, place: (place[0], i, 0)))
        land_specs.append(pl.BlockSpec((ns, h // split, cols), lambda i, place: (0, i, 0)))
        out_specs.append(pl.BlockSpec((h // split, cols), lambda i, place: (place[1] * split + i, 0)))
    return pl.pallas_call(
        body,
        grid_spec=pltpu.PrefetchScalarGridSpec(num_scalar_prefetch=1, grid=(split,),
                                               in_specs=mine_specs + land_specs, out_specs=out_specs),
        out_shape=[jax.ShapeDtypeStruct((2 * s.shape[1], s.shape[2]), F32) for s in slots],
        compiler_params=_params("parallel"), name="reduce_chip_sum",
    )(jnp.stack([me, c]), *partials, *slots)


def _adamw_call(ws, gs, ms, vs):
    na = len(ws)
    split = 4

    def body(*refs):
        w_refs, g_refs, m_refs, v_refs = (refs[k * na:(k + 1) * na] for k in range(4))
        d_refs, nm_refs, nv_refs = (refs[(4 + k) * na:(5 + k) * na] for k in range(3))
        for a in range(na):
            g = g_refs[a][...]
            m = ADAM_B1 * m_refs[a][...] + (1.0 - ADAM_B1) * g
            v = ADAM_B2 * v_refs[a][...] + (1.0 - ADAM_B2) * jnp.square(g)
            m_hat = m / (1.0 - ADAM_B1 ** ADAM_STEP)
            v_hat = v / (1.0 - ADAM_B2 ** ADAM_STEP)
            d_refs[a][...] = -ADAM_LR * (m_hat / (jnp.sqrt(v_hat) + ADAM_EPS) + ADAM_WD * w_refs[a][...])
            nm_refs[a][...] = m
            nv_refs[a][...] = v

    specs = [pl.BlockSpec((w.shape[0] // split, w.shape[1]), lambda i: (i, 0)) for w in ws]
    shapes = [jax.ShapeDtypeStruct(w.shape, F32) for w in ws]
    outs = pl.pallas_call(
        body, grid=(split,), in_specs=specs * 4, out_specs=specs * 3, out_shape=shapes * 3,
        compiler_params=_params("parallel"), name="adamw",
    )(*ws, *gs, *ms, *vs)
    return outs[:na], outs[na:2 * na], outs[2 * na:]


def _small_layout(shapes):
    layout, off = {}, 0
    for name in SMALL_NAMES:
        size = 1
        for d in shapes[name]:
            size *= d
        layout[name] = (off, size)
        off += size
    rows = -(-(off + 1) // (LANES * 64)) * 64
    return layout, off, rows


def _pack_small(arrays, rows, extra=None):
    parts = [arrays[name].reshape(-1) for name in SMALL_NAMES] + ([] if extra is None else [extra.reshape(-1)[:1]])
    flat = jnp.concatenate(parts)
    return jnp.pad(flat, (0, rows * LANES - flat.shape[0])).reshape(rows, LANES)


def _unpack_small(packed, layout, shapes):
    flat = packed.reshape(-1)
    return {name: flat[off:off + size].reshape(shapes[name]) for name, (off, size) in layout.items()}


def kernel(x, g_mix, w_in, b_in, sinks, w_pool, b_pool, pool_scale, w_out, b_out, g_ffn, w_gate, w_up, w_down, g_final, loss_target, m_g_mix, m_w_in, m_b_in, m_sinks, m_w_pool, m_b_pool, m_pool_scale, m_w_out, m_b_out, m_g_ffn, m_w_gate, m_w_up, m_w_down, m_g_final, v_g_mix, v_w_in, v_b_in, v_sinks, v_w_pool, v_b_pool, v_pool_scale, v_w_out, v_b_out, v_g_ffn, v_w_gate, v_w_up, v_w_down, v_g_final):
    given = dict(locals())
    big_names = ("w_in", "w_out", "w_gate", "w_up", "w_down")
    out_shapes = {n: given[n].shape for n in SMALL_NAMES + big_names}
    t = x.shape[1]
    c = lax.axis_index("c")

    me = 2 * lax.axis_index("x") + lax.axis_index("y")
    transposed = ("w_in", "w_gate", "w_up")

    def shard_view(a, name):
        return a[0].T if name in transposed else a[0]

    placed = _cast_place_call([shard_view(given[n], n) for n in big_names], me)
    (g_in,) = _exchange_call(_gather_exchange(placed[:1]), "gather_input_weight")
    w_in_full = g_in.reshape(IN_WIDTH, D_MODEL)

    small = dict(g_mix=g_mix, b_in=b_in, sinks=sinks, w_pool=w_pool[0], b_pool=b_pool.reshape(1, POOL_WIDTH),
                 pool_scale=pool_scale.reshape(1, POOL_WIDTH), b_out=b_out, g_ffn=g_ffn,
                 g_final=g_final.reshape(1, D_MODEL))
    loss, grad_x, small_grads, big_grads = _local_step(
        x[0], loss_target[0], small, w_in_full, placed[1], placed[2:], place=(me, c))

    layout, loss_at, small_rows = _small_layout(out_shapes)
    packed = _pack_small(small_grads, small_rows, loss)
    grads = [
        big_grads["w_in"].reshape(N_SHARDS, IN_WIDTH // N_SHARDS, D_MODEL),
        big_grads["w_out"].reshape(N_SHARDS, D_MODEL // N_SHARDS, D_MODEL),
        jnp.broadcast_to(packed[None], (N_SHARDS,) + packed.shape),
    ]
    alone = [functools.partial(_exchange_call, name=name) for name in
             ("reduce_sibling_exchange", "reduce_chip_exchange")]
    h_in, h_out, h_small = _reduce_to_owner(grads, (me, c), alone, [BF16, BF16, F32])
    totals = _exchange_call(_sibling_allgather(
        [h_in, h_out, big_grads["w_gate"], big_grads["w_up"], big_grads["w_down"], h_small]), "reduce_sibling_allgather")
    loss = totals[-1].reshape(-1)[loss_at]

    def owned(prefix):
        return ([shard_view(given[prefix + n], n) for n in big_names]
                + [_pack_small({n: given[prefix + n] for n in SMALL_NAMES}, small_rows)])

    ws, ms, vs = owned(""), owned("m_"), owned("v_")
    deltas, new_ms, new_vs = _adamw_call(ws, totals, ms, vs)

    def unpack(arrays):
        out = _unpack_small(arrays[-1], layout, out_shapes)
        for n, a in zip(big_names, arrays[:-1]):
            out[n] = (a.T if n in transposed else a).reshape(out_shapes[n])
        return out

    order = ("g_mix", "w_in", "b_in", "sinks", "w_pool", "b_pool", "pool_scale", "w_out", "b_out", "g_ffn",
             "w_gate", "w_up", "w_down", "g_final")
    results = [loss, grad_x.reshape(x.shape)]
    for group in (totals, deltas, new_ms, new_vs):
        named = unpack(list(group))
        results += [named[n] for n in order]
    return tuple(results)
```

```python
import functools
from typing import Any, Callable, Mapping, NamedTuple, Sequence

import jax
import jax.numpy as jnp
from jax import lax
from jax.experimental import pallas as pl
from jax.experimental.pallas import tpu as pltpu

F32 = jnp.float32
BF16 = jnp.bfloat16

D_MODEL = 1024
ATTN_WIDTH = 512
KV_WIDTH = 128
POOL_WIDTH = 512
IN_WIDTH = ATTN_WIDTH + 2 * KV_WIDTH + POOL_WIDTH
HEAD_DIM = 64
N_Q_HEADS = 8
BLOCK = 128
POOL_SIZES = (2, 4, 8, 16)
POOL_HALO = 16
ROPE_THETA = 10000.0
RMS_EPS = 1e-5
Q_SCALE = HEAD_DIM ** -0.5

ADAM_LR = 0.001
ADAM_B1 = 0.9
ADAM_B2 = 0.999
ADAM_EPS = 1e-08
ADAM_WD = 0.01
ADAM_STEP = 10

N_SHARDS = 4
LANES = 128
VMEM_LIMIT = 56 * 1024 * 1024
FFN_TOKEN_TILE = 256

MESH = pl.DeviceIdType.MESH
HBM_SPEC = pl.BlockSpec(memory_space=pltpu.HBM)

SMALL_VIEWS = dict(w_pool=(4 * LANES, LANES), b_pool=(4, LANES), pool_scale=(4, LANES), g_mix=(1, D_MODEL),
                   b_in=(1, IN_WIDTH), b_out=(1, D_MODEL), g_ffn=(1, D_MODEL), g_final=(1, D_MODEL),
                   sinks=(1, N_Q_HEADS))
SMALL_NAMES = tuple(SMALL_VIEWS)


def _dot(a, b):
    return jnp.dot(a, b, preferred_element_type=F32)


def _dot_nt(a, b):
    return lax.dot_general(a, b, (((1,), (1,)), ((), ())), preferred_element_type=F32)


def _dot_tn(a, b):
    return lax.dot_general(a, b, (((0,), (0,)), ((), ())), preferred_element_type=F32)


def _params(*semantics):
    return pltpu.CompilerParams(dimension_semantics=semantics, vmem_limit_bytes=VMEM_LIMIT)


def _resident(shape):
    return pl.BlockSpec(shape, lambda *_: (0,) * len(shape), pipeline_mode=pl.Buffered(1))


def _token_tile(t, largest=512):
    for tm in (1024, 512, 256, 128):
        if tm <= largest and t % tm == 0:
            return tm
    raise ValueError(f"sequence length {t} is not a multiple of 128")


def _rms_stats(xf):
    r = lax.rsqrt(jnp.mean(xf * xf, axis=-1, keepdims=True) + RMS_EPS)
    return xf * r, r


def _rms_bwd(dy, xh, r, g):
    dxh = dy * g
    return r * (dxh - xh * jnp.mean(dxh * xh, axis=-1, keepdims=True))


def _rope_partner(t, first):
    return jnp.where(first, pltpu.roll(t, LANES - HEAD_DIM // 2, 1), pltpu.roll(t, HEAD_DIM // 2, 1))


def _first_half_mask(shape):
    lane = lax.broadcasted_iota(jnp.int32, shape, 1)
    return (lane % HEAD_DIM) < (HEAD_DIM // 2)


def _rope_tables(t):
    inv_freq = 1.0 / (ROPE_THETA ** (jnp.arange(0, HEAD_DIM, 2, dtype=F32) / HEAD_DIM))
    ang = jnp.arange(t, dtype=F32)[:, None] * inv_freq[None, :]
    cos, sin = jnp.cos(ang), jnp.sin(ang)
    cos_t = jnp.tile(jnp.concatenate([cos, cos], axis=-1), (1, LANES // HEAD_DIM))
    sin_t = jnp.tile(jnp.concatenate([-sin, sin], axis=-1), (1, LANES // HEAD_DIM))
    return cos_t, sin_t


class _Exchange(NamedTuple):
    inputs: Sequence[Any]
    out_shapes: Sequence[Any]
    aliases: Mapping[int, int]
    sems: Sequence[Any]
    start: Callable[..., None]
    finish: Callable[..., None]


def _compute_call(body, operands, *, grid, in_specs, out_specs, out_shape, scratch_shapes=(), semantics, name,
                  exchange=None):
    n_in, n_out, n_scr = len(in_specs), len(out_specs), len(scratch_shapes)
    if exchange is None:
        return pl.pallas_call(body, grid=grid, in_specs=in_specs, out_specs=out_specs, out_shape=out_shape,
                              scratch_shapes=scratch_shapes, compiler_params=_params(*semantics),
                              name=name)(*operands), []
    k_in, k_out = len(exchange.inputs), len(exchange.out_shapes)

    def hosting(*refs):
        ins, rest = refs[:n_in], refs[n_in:]
        ex_ins, rest = rest[:k_in], rest[k_in:]
        outs, rest = rest[:n_out], rest[n_out:]
        ex_outs, rest = rest[:k_out], rest[k_out:]
        scratch, sems = rest[:n_scr], rest[n_scr:]
        steps = [pl.program_id(d) for d in range(len(grid))]
        is_first = functools.reduce(jnp.logical_and, [s == 0 for s in steps])
        is_last = functools.reduce(jnp.logical_and, [s == g - 1 for s, g in zip(steps, grid)])

        @pl.when(is_first)
        def _():
            exchange.start(ex_ins, ex_outs, sems)

        body(*ins, *outs, *scratch)

        @pl.when(is_last)
        def _():
            exchange.finish(ex_ins, ex_outs, sems)

    results = pl.pallas_call(
        hosting, grid=grid,
        in_specs=list(in_specs) + [HBM_SPEC] * k_in, out_specs=list(out_specs) + [HBM_SPEC] * k_out,
        out_shape=list(out_shape) + list(exchange.out_shapes),
        scratch_shapes=list(scratch_shapes) + list(exchange.sems),
        input_output_aliases={n_in + i: n_out + j for i, j in exchange.aliases.items()},
        compiler_params=_params(*semantics), name=name,
    )(*operands, *exchange.inputs)
    return results[:n_out], results[n_out:]


def _exchange_call(exchange, name):
    n_in, n_out = len(exchange.inputs), len(exchange.out_shapes)

    def body(*refs):
        ins, outs, sems = refs[:n_in], refs[n_in:n_in + n_out], refs[n_in + n_out:]
        exchange.start(ins, outs, sems)
        exchange.finish(ins, outs, sems)

    return pl.pallas_call(
        body, in_specs=[HBM_SPEC] * n_in, out_specs=[HBM_SPEC] * n_out, out_shape=list(exchange.out_shapes),
        input_output_aliases=dict(exchange.aliases), scratch_shapes=list(exchange.sems),
        compiler_params=pltpu.CompilerParams(has_side_effects=True), name=name,
    )(*exchange.inputs)


def _mesh_place():
    x, y, c = lax.axis_index("x"), lax.axis_index("y"), lax.axis_index("c")
    other_chips = [(1 - x, y), (x, 1 - y), (1 - x, 1 - y)]
    return x, y, c, other_chips


def _half(ref, c, rows):
    return ref.at[pl.ds(pl.multiple_of(c * rows, 16), rows), :]


def _remote(src, dst, send_sems, recv_sems, idx, to):
    return functools.partial(pltpu.make_async_remote_copy, src_ref=src, dst_ref=dst, send_sem=send_sems.at[idx],
                             recv_sem=recv_sems.at[idx], device_id=to, device_id_type=MESH)


def _gather_exchange(bufs):
    na = len(bufs)

    def copies(outs, sems):
        send_sems, recv_sems = sems
        x, y, c, other_chips = _mesh_place()
        me, sibling = 2 * x + y, (x, y, 1 - c)
        ici_out, ici_in, fwd_out, fwd_in = [], [], [], []
        for j, (cx, cy) in enumerate(other_chips):
            for a in range(na):
                rows = outs[a].shape[1] // 2
                mine = _half(outs[a].at[me], c, rows)
                landed = _half(outs[a].at[2 * cx + cy], c, rows)
                passed = _half(outs[a].at[2 * cx + cy], 1 - c, rows)
                ici_out.append(_remote(mine, mine, send_sems, recv_sems, j * na + a, (cx, cy, c)))
                ici_in.append(_remote(landed, landed, send_sems, recv_sems, j * na + a, (cx, cy, c)))
                fwd_out.append(_remote(landed, landed, send_sems, recv_sems, (3 + j) * na + a, sibling))
                fwd_in.append(_remote(passed, passed, send_sems, recv_sems, (3 + j) * na + a, sibling))
        return ici_out, ici_in, fwd_out, fwd_in

    def start(ins, outs, sems):
        for cp in copies(outs, sems)[0]:
            cp().start()

    def finish(ins, outs, sems):
        ici_out, ici_in, fwd_out, fwd_in = copies(outs, sems)
        for arrived, forward in zip(ici_in, fwd_out):
            arrived().wait_recv()
            forward().start()
        for cp in fwd_in:
            cp().wait_recv()
        for cp in ici_out + fwd_out:
            cp().wait_send()

    return _Exchange(inputs=bufs, out_shapes=[jax.ShapeDtypeStruct(b.shape, b.dtype) for b in bufs],
                     aliases={a: a for a in range(na)},
                     sems=[pltpu.SemaphoreType.DMA((6 * na,)), pltpu.SemaphoreType.DMA((6 * na,))],
                     start=start, finish=finish)


def _sibling_exchange(grads):
    na = len(grads)

    def copies(ins, outs, sems):
        x, y, c, _ = _mesh_place()
        out = []
        for a in range(na):
            rows = ins[a].shape[1] // 2
            src = ins[a].at[:, pl.ds(pl.multiple_of((1 - c) * rows, 8), rows), :]
            out.append(_remote(src, outs[a], sems[0], sems[1], a, (x, y, 1 - c)))
        return out

    def start(ins, outs, sems):
        for cp in copies(ins, outs, sems):
            cp().start()

    def finish(ins, outs, sems):
        for cp in copies(ins, outs, sems):
            cp().wait_recv()
            cp().wait_send()

    return _Exchange(inputs=grads,
                     out_shapes=[jax.ShapeDtypeStruct((g.shape[0], g.shape[1] // 2, g.shape[2]), g.dtype)
                                 for g in grads],
                     aliases={}, sems=[pltpu.SemaphoreType.DMA((na,)), pltpu.SemaphoreType.DMA((na,))],
                     start=start, finish=finish)


def _chip_exchange(partials):
    na = len(partials)

    def copies(ins, outs, sems):
        x, y, c, other_chips = _mesh_place()
        me = 2 * x + y
        sends, arrivals = [], []
        for j, (cx, cy) in enumerate(other_chips):
            for a in range(na):
                slot = outs[a].at[2 * cx + cy]
                sends.append(_remote(ins[a].at[2 * cx + cy], outs[a].at[me], sems[0], sems[1], j * na + a, (cx, cy, c)))
                arrivals.append(_remote(slot, slot, sems[0], sems[1], j * na + a, (cx, cy, c)))
        return sends, arrivals

    def start(ins, outs, sems):
        for cp in copies(ins, outs, sems)[0]:
            cp().start()

    def finish(ins, outs, sems):
        sends, arrivals = copies(ins, outs, sems)
        for cp in arrivals:
            cp().wait_recv()
        for cp in sends:
            cp().wait_send()

    return _Exchange(inputs=partials, out_shapes=[jax.ShapeDtypeStruct(p.shape, p.dtype) for p in partials],
                     aliases={}, sems=[pltpu.SemaphoreType.DMA((3 * na,)), pltpu.SemaphoreType.DMA((3 * na,))],
                     start=start, finish=finish)


def _sibling_allgather(bufs):
    na = len(bufs)

    def piece(outs, sems, a, half):
        x, y, c, _ = _mesh_place()
        rows = outs[a].shape[0] // 2
        rows_ref = outs[a].at[pl.ds(pl.multiple_of(half * rows, 8), rows), :]
        return _remote(rows_ref, rows_ref, sems[0], sems[1], a, (x, y, 1 - c))

    def start(ins, outs, sems):
        c = lax.axis_index("c")
        for a in range(na):
            piece(outs, sems, a, c)().start()

    def finish(ins, outs, sems):
        c = lax.axis_index("c")
        for a in range(na):
            piece(outs, sems, a, 1 - c)().wait_recv()
        for a in range(na):
            piece(outs, sems, a, c)().wait_send()

    return _Exchange(inputs=bufs, out_shapes=[jax.ShapeDtypeStruct(b.shape, b.dtype) for b in bufs],
                     aliases={a: a for a in range(na)},
                     sems=[pltpu.SemaphoreType.DMA((na,)), pltpu.SemaphoreType.DMA((na,))],
                     start=start, finish=finish)


def _norm_call(x, g_mix, exchange=None):
    t = x.shape[0]
    tm = _token_tile(t)

    def body(x_ref, g_ref, h_ref):
        xh, _ = _rms_stats(x_ref[...])
        h_ref[...] = (xh * g_ref[...]).astype(BF16)

    row = pl.BlockSpec((tm, D_MODEL), lambda i: (i, 0))
    return _compute_call(body, (x, g_mix), grid=(t // tm,), in_specs=[row, _resident((1, D_MODEL))],
                         out_specs=[row], out_shape=[jax.ShapeDtypeStruct((t, D_MODEL), BF16)],
                         semantics=("parallel",), name="norm_fwd", exchange=exchange)


def _inproj_call(h0, w_in, b_in, cos_t, sin_t, exchange=None):
    t = h0.shape[0]
    tm = _token_tile(t)

    def body(h_ref, w_ref, b_ref, c_ref, s_ref, q_ref, k_ref, v_ref, u_ref):
        z = _dot_nt(h_ref[...], w_ref[...]) + b_ref[...]
        cos, sin = c_ref[...], s_ref[...]
        first = _first_half_mask((tm, LANES))
        for j in range(ATTN_WIDTH // LANES):
            zj = z[:, LANES * j:LANES * (j + 1)]
            q_ref[:, LANES * j:LANES * (j + 1)] = ((zj * cos + _rope_partner(zj, first) * sin) * Q_SCALE).astype(BF16)
        zk = z[:, ATTN_WIDTH:ATTN_WIDTH + KV_WIDTH]
        k_ref[...] = (zk * cos + _rope_partner(zk, first) * sin).astype(BF16)
        v_ref[...] = z[:, ATTN_WIDTH + KV_WIDTH:ATTN_WIDTH + 2 * KV_WIDTH].astype(BF16)
        u_ref[...] = z[:, ATTN_WIDTH + 2 * KV_WIDTH:]

    row = lambda w: pl.BlockSpec((tm, w), lambda i: (i, 0))
    return _compute_call(
        body, (h0, w_in, b_in, cos_t, sin_t), grid=(t // tm,),
        in_specs=[row(D_MODEL), _resident((IN_WIDTH, D_MODEL)), _resident((1, IN_WIDTH)), row(LANES), row(LANES)],
        out_specs=[row(ATTN_WIDTH), row(KV_WIDTH), row(KV_WIDTH), row(POOL_WIDTH)],
        out_shape=[jax.ShapeDtypeStruct((t, ATTN_WIDTH), BF16),
                   jax.ShapeDtypeStruct((t, KV_WIDTH), BF16), jax.ShapeDtypeStruct((t, KV_WIDTH), BF16),
                   jax.ShapeDtypeStruct((t, POOL_WIDTH), F32)],
        semantics=("parallel",), name="inproj_fwd", exchange=exchange)


def _head_in_both_halves(band, kv_head):
    lane = lax.broadcasted_iota(jnp.int32, band.shape, 1)
    own = (lane < HEAD_DIM) if kv_head == 0 else (lane >= HEAD_DIM)
    return jnp.where(own, band, pltpu.roll(band, HEAD_DIM, 1)).astype(BF16)


def _stack_heads(ref, kv_head, row0):
    low = lax.broadcasted_iota(jnp.int32, (BLOCK, LANES), 1) < HEAD_DIM
    parts = []
    for jj in range(2):
        slab = ref[row0:row0 + BLOCK, LANES * (2 * kv_head + jj):LANES * (2 * kv_head + jj + 1)]
        zero = jnp.zeros_like(slab)
        parts += [jnp.where(low, slab, zero), jnp.where(low, zero, slab)]
    return jnp.concatenate(parts, axis=0)


def _band_biases():
    c = lax.broadcasted_iota(jnp.int32, (2 * BLOCK, 4 * BLOCK), 0)
    i = lax.broadcasted_iota(jnp.int32, (2 * BLOCK, 4 * BLOCK), 1) % BLOCK
    vis = (c > i) & (c <= i + BLOCK)
    return jnp.where(vis, 0.0, -jnp.inf).astype(F32), jnp.where(vis & (c >= BLOCK), 0.0, -jnp.inf).astype(F32)


def _blocks_per_step(nb):
    for g in (4, 2, 1):
        if nb % g == 0:
            return g


def _sink_row(sink_ref, kv_head):
    return jnp.concatenate([jnp.full((1, BLOCK), sink_ref[0, 4 * kv_head + g], F32) for g in range(4)], axis=1)


def _kv_band(cur_ref, prev_ref, g):
    r0 = g * BLOCK
    prev = prev_ref[...] if g == 0 else cur_ref[r0 - BLOCK:r0, :]
    return jnp.concatenate([prev, cur_ref[r0:r0 + BLOCK, :]], axis=0).astype(F32)


def _attn_fwd_call(q, k, v, sinks, bias, bias_first, exchange=None):
    t = q.shape[0]
    nb = t // BLOCK
    gb = _blocks_per_step(nb)
    qb = gb * BLOCK

    def body(sink_ref, q_ref, kc_ref, kp_ref, vc_ref, vp_ref, b_ref, bf_ref, o_ref):
        n = pl.program_id(0)
        row_low = lax.broadcasted_iota(jnp.int32, (LANES, BLOCK), 0) < HEAD_DIM
        for g in range(gb):
            r0 = g * BLOCK
            kband = _kv_band(kc_ref, kp_ref, g)
            vband_t = _kv_band(vc_ref, vp_ref, g).T
            bias_g = jnp.where(n == 0, bf_ref[...], b_ref[...]) if g == 0 else b_ref[...]
            for kh in range(2):
                k2 = _head_in_both_halves(kband, kh)
                vt_head = vband_t[HEAD_DIM * kh:HEAD_DIM * (kh + 1), :]
                v2_t = jnp.concatenate([vt_head, vt_head], axis=0).astype(BF16)
                s_t = _dot_nt(k2, _stack_heads(q_ref, kh, r0)) + bias_g
                sink = _sink_row(sink_ref, kh)
                m = jnp.maximum(jnp.max(s_t, axis=0, keepdims=True), sink)
                e = jnp.exp(s_t - m)
                inv = 1.0 / (jnp.sum(e, axis=0, keepdims=True) + jnp.exp(sink - m))
                o_t = _dot(v2_t, (e * inv).astype(BF16))
                for jj in range(2):
                    slab = jnp.where(row_low, o_t[:, BLOCK * 2 * jj:BLOCK * (2 * jj + 1)],
                                     o_t[:, BLOCK * (2 * jj + 1):BLOCK * (2 * jj + 2)]).T
                    o_ref[r0:r0 + BLOCK, LANES * (2 * kh + jj):LANES * (2 * kh + jj + 1)] = slab.astype(BF16)

    cur = lambda w: pl.BlockSpec((qb, w), lambda n: (n, 0))
    prev = lambda w: pl.BlockSpec((BLOCK, w), lambda n: (jnp.maximum(n * gb - 1, 0), 0))
    return _compute_call(
        body, (sinks, q, k, k, v, v, bias, bias_first), grid=(nb // gb,),
        in_specs=[pl.BlockSpec(memory_space=pltpu.SMEM), cur(ATTN_WIDTH), cur(KV_WIDTH), prev(KV_WIDTH),
                  cur(KV_WIDTH), prev(KV_WIDTH), _resident(bias.shape), _resident(bias.shape)],
        out_specs=[cur(ATTN_WIDTH)],
        out_shape=[jax.ShapeDtypeStruct((t, ATTN_WIDTH), BF16)],
        semantics=("parallel",), name="attn_fwd", exchange=exchange)


def _pool_mixed(sc_ref, u_ref, halo_ref, i, tm):
    sc_ref[0:POOL_HALO, :] = jnp.where(i == 0, 0.0, halo_ref[...])
    sc_ref[POOL_HALO:, :] = u_ref[...]
    tok = i * tm + lax.broadcasted_iota(jnp.int32, (tm, 1), 0)
    mixed = []
    for g, size in enumerate(POOL_SIZES):
        lanes = slice(LANES * g, LANES * (g + 1))
        cur = sc_ref[pl.ds(POOL_HALO, tm), lanes]
        acc = cur
        for j in range(1, size):
            acc = acc + sc_ref[pl.ds(POOL_HALO - j, tm), lanes]
        count = jnp.minimum(tok + 1, size).astype(F32)
        mixed.append(acc / count - cur)
    return mixed


def _halo_before(tm, width):
    return pl.BlockSpec((POOL_HALO, width), lambda i: (jnp.maximum(i * (tm // POOL_HALO) - 1, 0), 0))


def _outproj_call(attn, u, x, w_pool, b_pool, pool_scale, w_out, b_out, g_ffn, exchange=None):
    t = x.shape[0]
    tm = _token_tile(t)

    def body(a_ref, u_ref, halo_ref, x_ref, wp_ref, bp_ref, ps_ref, wo_ref, bo_ref, g_ref,
             x1_ref, h1_ref, y_ref, sc_ref):
        i = pl.program_id(0)
        for g, mixed in enumerate(_pool_mixed(sc_ref, u_ref, halo_ref, i, tm)):
            lanes = slice(LANES * g, LANES * (g + 1))
            lin = _dot(mixed.astype(BF16), wp_ref[g]) + bp_ref[:, lanes]
            y_ref[:, lanes] = (lin * ps_ref[:, lanes]).astype(BF16)
        x1 = (x_ref[...] + _dot(a_ref[...], wo_ref[0:ATTN_WIDTH, :]) + _dot(y_ref[...], wo_ref[ATTN_WIDTH:, :])
              + bo_ref[...])
        x1_ref[...] = x1
        xh, _ = _rms_stats(x1)
        h1_ref[...] = (xh * g_ref[...]).astype(BF16)

    row = lambda w: pl.BlockSpec((tm, w), lambda i: (i, 0))
    return _compute_call(
        body, (attn, u, u, x, w_pool, b_pool, pool_scale, w_out, b_out, g_ffn), grid=(t // tm,),
        in_specs=[row(ATTN_WIDTH), row(POOL_WIDTH), _halo_before(tm, POOL_WIDTH), row(D_MODEL),
                  _resident((4, LANES, LANES)), _resident((1, POOL_WIDTH)), _resident((1, POOL_WIDTH)),
                  _resident((D_MODEL, D_MODEL)), _resident((1, D_MODEL)), _resident((1, D_MODEL))],
        out_specs=[row(D_MODEL), row(D_MODEL), row(POOL_WIDTH)],
        out_shape=[jax.ShapeDtypeStruct((t, D_MODEL), F32), jax.ShapeDtypeStruct((t, D_MODEL), BF16),
                   jax.ShapeDtypeStruct((t, POOL_WIDTH), BF16)],
        scratch_shapes=[pltpu.VMEM((tm + POOL_HALO, POOL_WIDTH), F32)],
        semantics=("parallel",), name="outproj_fwd", exchange=exchange)


def _ffn_fwd_call(h1, x1, target, w_gate, w_up, w_down, g_final):
    t = x1.shape[0]
    tm = _token_tile(t, FFN_TOKEN_TILE)
    ns, fs, _ = w_gate.shape

    def body(h_ref, x1_ref, tgt_ref, wg_ref, wu_ref, wd_ref, gf_ref,
             gate_ref, up_ref, act_ref, dx2_ref, loss_ref, dgf_ref):
        @pl.when(pl.program_id(0) == 0)
        def _():
            loss_ref[...] = jnp.zeros_like(loss_ref)
            dgf_ref[...] = jnp.zeros_like(dgf_ref)

        h = h_ref[...]
        x2 = x1_ref[...]
        for s in range(ns):
            gate = _dot_nt(h, wg_ref[s])
            up = _dot_nt(h, wu_ref[s])
            act = (gate * jax.nn.sigmoid(gate) * up).astype(BF16)
            gate_ref[s] = gate.astype(BF16)
            up_ref[s] = up.astype(BF16)
            act_ref[s] = act
            x2 = x2 + _dot(act, wd_ref[s])
        xh, r = _rms_stats(x2)
        gf = gf_ref[...]
        err = xh * gf - tgt_ref[...]
        loss_ref[...] += 0.5 * jnp.sum(err * err) / D_MODEL
        dy = err / D_MODEL
        dgf_ref[...] += jnp.sum(dy * xh, axis=0, keepdims=True)
        dx2_ref[...] = _rms_bwd(dy, xh, r, gf)

    row = lambda w: pl.BlockSpec((tm, w), lambda i: (i, 0))
    shard_act = pl.BlockSpec((ns, tm, fs), lambda i: (0, i, 0))
    act_shape = jax.ShapeDtypeStruct((ns, t, fs), BF16)
    return pl.pallas_call(
        body, grid=(t // tm,),
        in_specs=[row(D_MODEL), row(D_MODEL), row(D_MODEL), _resident((ns, fs, D_MODEL)),
                  _resident((ns, fs, D_MODEL)), _resident((ns, fs, D_MODEL)), _resident((1, D_MODEL))],
        out_specs=[shard_act, shard_act, shard_act, row(D_MODEL),
                   pl.BlockSpec((1, LANES), lambda i: (0, 0)), pl.BlockSpec((1, D_MODEL), lambda i: (0, 0))],
        out_shape=[act_shape, act_shape, act_shape, jax.ShapeDtypeStruct((t, D_MODEL), F32),
                   jax.ShapeDtypeStruct((1, LANES), F32), jax.ShapeDtypeStruct((1, D_MODEL), F32)],
        compiler_params=_params("arbitrary"), name="ffn_fwd",
    )(h1, x1, target, w_gate, w_up, w_down, g_final)


def _ffn_bwd_act_call(dx2, gate, up, x1, w_gate, w_up, w_down, g_ffn):
    t = x1.shape[0]
    tm = _token_tile(t, FFN_TOKEN_TILE)
    ns, fs, _ = w_gate.shape

    def body(dx2_ref, gate_ref, up_ref, x1_ref, wg_ref, wu_ref, wd_ref, g_ref,
             dgate_ref, dup_ref, dx1_ref, dx2b_ref, dg_ref):
        @pl.when(pl.program_id(0) == 0)
        def _():
            dg_ref[...] = jnp.zeros_like(dg_ref)

        dx2 = dx2_ref[...]
        dx2b = dx2.astype(BF16)
        dx2b_ref[...] = dx2b
        dh = None
        for s in range(ns):
            dact = _dot_nt(dx2b, wd_ref[s])
            gate = gate_ref[s].astype(F32)
            sg = jax.nn.sigmoid(gate)
            dup = (dact * (gate * sg)).astype(BF16)
            dgate = (dact * up_ref[s].astype(F32) * (sg * (1.0 + gate * (1.0 - sg)))).astype(BF16)
            dgate_ref[s] = dgate
            dup_ref[s] = dup
            part = _dot(dgate, wg_ref[s]) + _dot(dup, wu_ref[s])
            dh = part if dh is None else dh + part
        xh, r = _rms_stats(x1_ref[...])
        dg_ref[...] += jnp.sum(dh * xh, axis=0, keepdims=True)
        dx1_ref[...] = dx2 + _rms_bwd(dh, xh, r, g_ref[...])

    row = lambda w: pl.BlockSpec((tm, w), lambda i: (i, 0))
    shard_act = pl.BlockSpec((ns, tm, fs), lambda i: (0, i, 0))
    act_shape = jax.ShapeDtypeStruct((ns, t, fs), BF16)
    return pl.pallas_call(
        body, grid=(t // tm,),
        in_specs=[row(D_MODEL), shard_act, shard_act, row(D_MODEL), _resident((ns, fs, D_MODEL)),
                  _resident((ns, fs, D_MODEL)), _resident((ns, fs, D_MODEL)), _resident((1, D_MODEL))],
        out_specs=[shard_act, shard_act, row(D_MODEL), row(D_MODEL), pl.BlockSpec((1, D_MODEL), lambda i: (0, 0))],
        out_shape=[act_shape, act_shape, jax.ShapeDtypeStruct((t, D_MODEL), F32),
                   jax.ShapeDtypeStruct((t, D_MODEL), BF16), jax.ShapeDtypeStruct((1, D_MODEL), F32)],
        compiler_params=_params("arbitrary"), name="ffn_bwd_act",
    )(dx2, gate, up, x1, w_gate, w_up, w_down, g_ffn)


def _ffn_bwd_weights_call(h1, dgate, dup, act, dx2):
    t = h1.shape[0]
    tm = _token_tile(t, 1024)
    ns, _, fs = dgate.shape

    def body(h_ref, dgate_ref, dup_ref, act_ref, dx2_ref, gg_ref, gu_ref, gd_ref):
        @pl.when(pl.program_id(1) == 0)
        def _():
            gg_ref[...] = jnp.zeros_like(gg_ref)
            gu_ref[...] = jnp.zeros_like(gu_ref)
            gd_ref[...] = jnp.zeros_like(gd_ref)

        h = h_ref[...]
        gg_ref[...] += _dot_tn(dgate_ref[...], h)
        gu_ref[...] += _dot_tn(dup_ref[...], h)
        gd_ref[...] += _dot_tn(act_ref[...], dx2_ref[...])

    row = lambda w: pl.BlockSpec((tm, w), lambda s, i: (i, 0))
    shard_act = pl.BlockSpec((None, tm, fs), lambda s, i: (s, i, 0))
    return pl.pallas_call(
        body, grid=(ns, t // tm),
        in_specs=[row(D_MODEL), shard_act, shard_act, shard_act, row(D_MODEL)],
        out_specs=[pl.BlockSpec((None, fs, D_MODEL), lambda s, i: (s, 0, 0))] * 3,
        out_shape=[jax.ShapeDtypeStruct((ns, fs, D_MODEL), F32)] * 3,
        compiler_params=_params("parallel", "arbitrary"), name="ffn_bwd_weights",
    )(h1, dgate, dup, act, dx2)


def _outproj_bwd_call(dx1, attn, pool_y, u, w_pool, b_pool, pool_scale, w_out, exchange=None):
    t = dx1.shape[0]
    tm = _token_tile(t)

    def body(dx1_ref, a_ref, y_ref, u_ref, halo_ref, wp_ref, bp_ref, ps_ref, wo_ref,
             dattn_ref, dmixed_ref, gwo_ref, gbo_ref, gwp_ref, gbp_ref, gps_ref, sc_ref):
        i = pl.program_id(0)

        @pl.when(i == 0)
        def _():
            for ref in (gwo_ref, gbo_ref, gwp_ref, gbp_ref, gps_ref):
                ref[...] = jnp.zeros_like(ref)

        dx1 = dx1_ref[...]
        dx1b = dx1.astype(BF16)
        dcat = _dot_nt(dx1b, wo_ref[...])
        dattn_ref[...] = dcat[:, :ATTN_WIDTH].astype(BF16)
        for g, mixed in enumerate(_pool_mixed(sc_ref, u_ref, halo_ref, i, tm)):
            lanes = slice(LANES * g, LANES * (g + 1))
            mixed_b = mixed.astype(BF16)
            lin = _dot(mixed_b, wp_ref[g]) + bp_ref[:, lanes]
            dpool = dcat[:, ATTN_WIDTH + LANES * g:ATTN_WIDTH + LANES * (g + 1)]
            gps_ref[:, lanes] += jnp.sum(dpool * lin, axis=0, keepdims=True)
            dy = dpool * ps_ref[:, lanes]
            gbp_ref[:, lanes] += jnp.sum(dy, axis=0, keepdims=True)
            dy_b = dy.astype(BF16)
            gwp_ref[g] += _dot_tn(mixed_b, dy_b)
            dmixed_ref[:, lanes] = _dot_nt(dy_b, wp_ref[g])
        gwo_ref[0:ATTN_WIDTH, :] += _dot_tn(a_ref[...], dx1b)
        gwo_ref[ATTN_WIDTH:, :] += _dot_tn(y_ref[...], dx1b)
        gbo_ref[...] += jnp.sum(dx1, axis=0, keepdims=True)

    row = lambda w: pl.BlockSpec((tm, w), lambda i: (i, 0))
    const = lambda shape: pl.BlockSpec(shape, lambda i: (0,) * len(shape))
    return _compute_call(
        body, (dx1, attn, pool_y, u, u, w_pool, b_pool, pool_scale, w_out), grid=(t // tm,),
        in_specs=[row(D_MODEL), row(ATTN_WIDTH), row(POOL_WIDTH), row(POOL_WIDTH), _halo_before(tm, POOL_WIDTH),
                  _resident((4, LANES, LANES)), _resident((1, POOL_WIDTH)), _resident((1, POOL_WIDTH)),
                  _resident((D_MODEL, D_MODEL))],
        out_specs=[row(ATTN_WIDTH), row(POOL_WIDTH), const((D_MODEL, D_MODEL)), const((1, D_MODEL)),
                   const((4, LANES, LANES)), const((1, POOL_WIDTH)), const((1, POOL_WIDTH))],
        out_shape=[jax.ShapeDtypeStruct((t, ATTN_WIDTH), BF16), jax.ShapeDtypeStruct((t, POOL_WIDTH), F32),
                   jax.ShapeDtypeStruct((D_MODEL, D_MODEL), F32), jax.ShapeDtypeStruct((1, D_MODEL), F32),
                   jax.ShapeDtypeStruct((4, LANES, LANES), F32), jax.ShapeDtypeStruct((1, POOL_WIDTH), F32),
                   jax.ShapeDtypeStruct((1, POOL_WIDTH), F32)],
        scratch_shapes=[pltpu.VMEM((tm + POOL_HALO, POOL_WIDTH), F32)],
        semantics=("arbitrary",), name="outproj_bwd", exchange=exchange)


def _attn_bwd_call(q, k, v, dout, sinks, cos_t, sin_t, bias, bias_first, exchange=None):
    t = q.shape[0]
    nb = t // BLOCK
    gb = _blocks_per_step(nb)
    qb = gb * BLOCK
    steps = nb // gb

    def body(sink_ref, q_ref, kc_ref, kp_ref, vc_ref, vp_ref, do_ref, cq_ref, sq_ref, ck_ref, sk_ref, b_ref, bf_ref,
             dq_ref, dk_ref, dv_ref, dsink_ref, newk_ref, newv_ref, carryk_ref, carryv_ref, headk_ref, headv_ref):
        n = pl.program_id(0)
        first = _first_half_mask((BLOCK, LANES))

        @pl.when(n == 0)
        def _():
            dsink_ref[...] = jnp.zeros_like(dsink_ref)

        @pl.when(n < steps)
        def _():
            head_row = lax.broadcasted_iota(jnp.int32, dsink_ref.shape, 0)
            lane2 = lax.broadcasted_iota(jnp.int32, (2 * BLOCK, LANES), 1)
            row_low = lax.broadcasted_iota(jnp.int32, (LANES, BLOCK), 0) < HEAD_DIM
            dsink = jnp.zeros(dsink_ref.shape, F32)
            for g in range(gb):
                r0 = g * BLOCK
                kband = _kv_band(kc_ref, kp_ref, g)
                vband = _kv_band(vc_ref, vp_ref, g)
                kband_t = kband.T
                bias_g = jnp.where(n == 0, bf_ref[...], b_ref[...]) if g == 0 else b_ref[...]
                cq, sq = cq_ref[r0:r0 + BLOCK, :], sq_ref[r0:r0 + BLOCK, :]
                dk_heads, dv_heads = [], []
                for kh in range(2):
                    k2 = _head_in_both_halves(kband, kh)
                    v2 = _head_in_both_halves(vband, kh)
                    kt_head = kband_t[HEAD_DIM * kh:HEAD_DIM * (kh + 1), :]
                    k2_t = jnp.concatenate([kt_head, kt_head], axis=0).astype(BF16)
                    qst = _stack_heads(q_ref, kh, r0)
                    dost = _stack_heads(do_ref, kh, r0)
                    s_t = _dot_nt(k2, qst) + bias_g
                    sink = _sink_row(sink_ref, kh)
                    m = jnp.maximum(jnp.max(s_t, axis=0, keepdims=True), sink)
                    e = jnp.exp(s_t - m)
                    e_sink = jnp.exp(sink - m)
                    inv = 1.0 / (jnp.sum(e, axis=0, keepdims=True) + e_sink)
                    p_t = e * inv
                    dp_t = _dot_nt(v2, dost)
                    delta = jnp.sum(p_t * dp_t, axis=0, keepdims=True)
                    ds_t = (p_t * (dp_t - delta)).astype(BF16)
                    sink_term = e_sink * inv * delta
                    for h in range(4):
                        val = -jnp.sum(sink_term[:, BLOCK * h:BLOCK * (h + 1)])
                        dsink = dsink + jnp.where(head_row == 4 * kh + h, val, 0.0)
                    dq_t = _dot(k2_t, ds_t)
                    for jj in range(2):
                        slab = jnp.where(row_low, dq_t[:, BLOCK * 2 * jj:BLOCK * (2 * jj + 1)],
                                         dq_t[:, BLOCK * (2 * jj + 1):BLOCK * (2 * jj + 2)]).T
                        dq_ref[r0:r0 + BLOCK, LANES * (2 * kh + jj):LANES * (2 * kh + jj + 1)] = (
                            (slab * cq + _rope_partner(slab * sq, first)) * Q_SCALE).astype(BF16)
                    dk2 = _dot(ds_t, qst)
                    dv2 = _dot(p_t.astype(BF16), dost)
                    dk_heads.append(dk2 + pltpu.roll(dk2, HEAD_DIM, 1))
                    dv_heads.append(dv2 + pltpu.roll(dv2, HEAD_DIM, 1))
                bandk = jnp.where(lane2 < HEAD_DIM, dk_heads[0], dk_heads[1])
                bandv = jnp.where(lane2 < HEAD_DIM, dv_heads[0], dv_heads[1])
                if g == 0:
                    headk_ref[...] = bandk[0:BLOCK]
                    headv_ref[...] = bandv[0:BLOCK]
                else:
                    newk_ref[r0 - BLOCK:r0, :] += bandk[0:BLOCK]
                    newv_ref[r0 - BLOCK:r0, :] += bandv[0:BLOCK]
                newk_ref[r0:r0 + BLOCK, :] = bandk[BLOCK:]
                newv_ref[r0:r0 + BLOCK, :] = bandv[BLOCK:]
            dsink_ref[...] += dsink

        @pl.when(n == steps)
        def _():
            headk_ref[...] = jnp.zeros_like(headk_ref)
            headv_ref[...] = jnp.zeros_like(headv_ref)

        @pl.when(n >= 1)
        def _():
            last = qb - BLOCK
            carryk_ref[last:, :] += headk_ref[...]
            carryv_ref[last:, :] += headv_ref[...]
            for g in range(gb):
                rows = slice(g * BLOCK, (g + 1) * BLOCK)
                dk = carryk_ref[rows, :]
                dk_ref[rows, :] = (dk * ck_ref[rows, :] + _rope_partner(dk * sk_ref[rows, :], first)).astype(BF16)
            dv_ref[...] = carryv_ref[...].astype(BF16)

        @pl.when(n < steps)
        def _():
            carryk_ref[...] = newk_ref[...]
            carryv_ref[...] = newv_ref[...]

    cur = lambda w: pl.BlockSpec((qb, w), lambda n: (jnp.minimum(n, steps - 1), 0))
    late = lambda w: pl.BlockSpec((qb, w), lambda n: (jnp.maximum(n - 1, 0), 0))
    prev = lambda w: pl.BlockSpec((BLOCK, w), lambda n: (jnp.maximum(jnp.minimum(n, steps - 1) * gb - 1, 0), 0))
    return _compute_call(
        body, (sinks, q, k, k, v, v, dout, cos_t, sin_t, cos_t, sin_t, bias, bias_first), grid=(steps + 1,),
        in_specs=[pl.BlockSpec(memory_space=pltpu.SMEM), cur(ATTN_WIDTH), cur(KV_WIDTH), prev(KV_WIDTH),
                  cur(KV_WIDTH), prev(KV_WIDTH), cur(ATTN_WIDTH), cur(LANES), cur(LANES), late(LANES), late(LANES),
                  _resident(bias.shape), _resident(bias.shape)],
        out_specs=[cur(ATTN_WIDTH), late(KV_WIDTH), late(KV_WIDTH), pl.BlockSpec((8, LANES), lambda n: (0, 0))],
        out_shape=[jax.ShapeDtypeStruct((t, ATTN_WIDTH), BF16), jax.ShapeDtypeStruct((t, KV_WIDTH), BF16),
                   jax.ShapeDtypeStruct((t, KV_WIDTH), BF16), jax.ShapeDtypeStruct((8, LANES), F32)],
        scratch_shapes=[pltpu.VMEM((qb, LANES), F32), pltpu.VMEM((qb, LANES), F32),
                        pltpu.VMEM((qb, LANES), F32), pltpu.VMEM((qb, LANES), F32),
                        pltpu.VMEM((BLOCK, LANES), F32), pltpu.VMEM((BLOCK, LANES), F32)],
        semantics=("arbitrary",), name="attn_bwd", exchange=exchange)


def _inproj_bwd_call(dq, dk, dv, dmixed, x, h0, dx1, w_in, g_mix, exchange=None):
    t = x.shape[0]
    tm = _token_tile(t)
    n_tiles = t // tm

    def body(dq_ref, dk_ref, dv_ref, dm_ref, halo_ref, x_ref, h_ref, dx1_ref, w_ref, g_ref,
             dx_ref, gw_ref, gb_ref, gg_ref, sc_ref):
        i = pl.program_id(0)

        @pl.when(i == 0)
        def _():
            for ref in (gw_ref, gb_ref, gg_ref):
                ref[...] = jnp.zeros_like(ref)

        tok = i * tm + lax.broadcasted_iota(jnp.int32, (tm, 1), 0)
        halo = jnp.where(i == n_tiles - 1, 0.0, halo_ref[...])
        du = []
        for g, size in enumerate(POOL_SIZES):
            lanes = slice(LANES * g, LANES * (g + 1))
            dm = dm_ref[:, lanes]
            sc_ref[0:tm, lanes] = dm / jnp.minimum(tok + 1, size).astype(F32)
            sc_ref[tm:, lanes] = halo[:, lanes] / float(size)
            acc = -dm
            for j in range(size):
                acc = acc + sc_ref[pl.ds(j, tm), lanes]
            du.append(acc)
        dz32 = jnp.concatenate([dq_ref[...].astype(F32), dk_ref[...].astype(F32), dv_ref[...].astype(F32)] + du,
                               axis=1)
        dz = dz32.astype(BF16)
        gb_ref[...] += jnp.sum(dz32, axis=0, keepdims=True)
        gw_ref[...] += _dot_tn(dz, h_ref[...])
        dh = _dot(dz, w_ref[...])
        xh, r = _rms_stats(x_ref[...])
        gg_ref[...] += jnp.sum(dh * xh, axis=0, keepdims=True)
        dx_ref[...] = dx1_ref[...] + _rms_bwd(dh, xh, r, g_ref[...])

    row = lambda w: pl.BlockSpec((tm, w), lambda i: (i, 0))
    const = lambda shape: pl.BlockSpec(shape, lambda i: (0,) * len(shape))
    last_halo = t // POOL_HALO - 1
    halo_after = pl.BlockSpec((POOL_HALO, POOL_WIDTH),
                              lambda i: (jnp.minimum((i + 1) * (tm // POOL_HALO), last_halo), 0))
    return _compute_call(
        body, (dq, dk, dv, dmixed, dmixed, x, h0, dx1, w_in, g_mix), grid=(n_tiles,),
        in_specs=[row(ATTN_WIDTH), row(KV_WIDTH), row(KV_WIDTH), row(POOL_WIDTH), halo_after, row(D_MODEL),
                  row(D_MODEL), row(D_MODEL), _resident((IN_WIDTH, D_MODEL)), _resident((1, D_MODEL))],
        out_specs=[row(D_MODEL), const((IN_WIDTH, D_MODEL)), const((1, IN_WIDTH)), const((1, D_MODEL))],
        out_shape=[jax.ShapeDtypeStruct((t, D_MODEL), F32), jax.ShapeDtypeStruct((IN_WIDTH, D_MODEL), F32),
                   jax.ShapeDtypeStruct((1, IN_WIDTH), F32), jax.ShapeDtypeStruct((1, D_MODEL), F32)],
        scratch_shapes=[pltpu.VMEM((tm + POOL_HALO, POOL_WIDTH), F32)],
        semantics=("arbitrary",), name="inproj_bwd", exchange=exchange)


def _reduce_to_owner(grads, place, hosts, wire_dtypes):
    me, c = place
    landed = hosts[0](_sibling_exchange(grads))
    partials = _pair_sum_call(grads, landed, c, wire_dtypes)
    slots = hosts[1](_chip_exchange(partials))
    return _chip_sum_call(partials, slots, me, c)


def _local_step(x, target, small, w_in, w_out, w_ffn, place=None):
    t = x.shape[0]
    on_mesh = place is not None
    cos_t, sin_t = _rope_tables(t)
    w_pool = small["w_pool"].astype(BF16)
    w_ffn = list(w_ffn)

    def gathering(i):
        return _gather_exchange([w_ffn[i]]) if on_mesh else None

    def gathered(i, results):
        if on_mesh:
            w_ffn[i] = results[-1]

    (h0,), got = _norm_call(x, small["g_mix"], exchange=_gather_exchange([w_in]) if on_mesh else None)
    if on_mesh:
        w_in = got[0].reshape(IN_WIDTH, D_MODEL)
    first = _gather_exchange([w_out, w_ffn[0]]) if on_mesh else None
    (q, k, v, u), got = _inproj_call(h0, w_in, small["b_in"], cos_t, sin_t, exchange=first)
    gathered(0, got)
    if on_mesh:
        w_out = got[0].reshape(D_MODEL, D_MODEL)
    bias_t, bias_first_t = _band_biases()
    (attn,), got = _attn_fwd_call(q, k, v, small["sinks"], bias_t, bias_first_t, exchange=gathering(1))
    gathered(1, got)
    (x1, h1, pool_y), got = _outproj_call(attn, u, x, w_pool, small["b_pool"], small["pool_scale"], w_out,
                                          small["b_out"], small["g_ffn"], exchange=gathering(2))
    gathered(2, got)
    w_gate, w_up, w_down = w_ffn
    gate, up, act, dx2, loss, g_g_final = _ffn_fwd_call(h1, x1, target, w_gate, w_up, w_down, small["g_final"])
    dgate, dup, dx1, dx2b, g_g_ffn = _ffn_bwd_act_call(dx2, gate, up, x1, w_gate, w_up, w_down, small["g_ffn"])
    ffn_grads = list(_ffn_bwd_weights_call(h1, dgate, dup, act, dx2b))
    kept = {}

    def outproj_bwd(exchange=None):
        kept["outproj"], results = _outproj_bwd_call(dx1, attn, pool_y, u, w_pool, small["b_pool"],
                                                     small["pool_scale"], w_out, exchange=exchange)
        return results

    def attn_bwd(exchange=None):
        kept["attn"], results = _attn_bwd_call(q, k, v, kept["outproj"][0], small["sinks"], cos_t, sin_t, bias_t,
                                               bias_first_t, exchange=exchange)
        return results

    def inproj_bwd(exchange=None):
        dq, dk, dv, _ = kept["attn"]
        kept["inproj"], results = _inproj_bwd_call(dq, dk, dv, kept["outproj"][1], x, h0, dx1, w_in, small["g_mix"],
                                                   exchange=exchange)
        return results

    if on_mesh:
        ffn_grads = _reduce_to_owner(ffn_grads, place, (outproj_bwd, attn_bwd), [BF16] * 3)
    else:
        outproj_bwd(), attn_bwd()
    inproj_bwd()
    _, _, g_w_out, g_b_out, g_w_pool, g_b_pool, g_pool_scale = kept["outproj"]
    g_sinks = kept["attn"][3]
    grad_x, g_w_in, g_b_in, g_g_mix = kept["inproj"]
    small_grads = dict(g_mix=g_g_mix, b_in=g_b_in, sinks=g_sinks[:, 0], w_pool=g_w_pool,
                       b_pool=g_b_pool, pool_scale=g_pool_scale, b_out=g_b_out, g_ffn=g_g_ffn, g_final=g_g_final)
    big_grads = dict(w_in=g_w_in, w_out=g_w_out, w_gate=ffn_grads[0], w_up=ffn_grads[1], w_down=ffn_grads[2])
    return loss, grad_x, small_grads, big_grads


def _cast_place_call(shards, me):
    na = len(shards)
    split = 2

    def body(me_ref, *refs):
        for a in range(na):
            refs[na + a][...] = refs[a][...].astype(BF16)

    in_specs = [pl.BlockSpec((s.shape[0] // split, s.shape[1]), lambda i, me_ref: (i, 0)) for s in shards]
    out_specs = [pl.BlockSpec((None, s.shape[0] // split, s.shape[1]), lambda i, me_ref: (me_ref[0], i, 0))
                 for s in shards]
    return pl.pallas_call(
        body,
        grid_spec=pltpu.PrefetchScalarGridSpec(num_scalar_prefetch=1, grid=(split,), in_specs=in_specs,
                                               out_specs=out_specs),
        out_shape=[jax.ShapeDtypeStruct((N_SHARDS,) + s.shape, BF16) for s in shards],
        compiler_params=_params("parallel"), name="cast_place_weights",
    )(me.reshape(1), *shards)


def _pair_sum_call(grads, landed, c, wire_dtypes):
    na = len(grads)
    split = 2

    def body(c_ref, *refs):
        ins, lands, outs = refs[:na], refs[na:2 * na], refs[2 * na:]
        for a in range(na):
            outs[a][...] = (ins[a][...] + lands[a][...]).astype(outs[a].dtype)

    in_specs, land_specs, out_specs, views = [], [], [], []
    for g in grads:
        ns, r, cols = g.shape
        rows = r // 2 // split
        views.append(g.reshape(ns, 2, r // 2, cols))
        in_specs.append(pl.BlockSpec((None, None, rows, cols), lambda s, i, c_ref: (s, c_ref[0], i, 0)))
        land_specs.append(pl.BlockSpec((None, rows, cols), lambda s, i, c_ref: (s, i, 0)))
        out_specs.append(pl.BlockSpec((None, rows, cols), lambda s, i, c_ref: (s, i, 0)))
    return pl.pallas_call(
        body,
        grid_spec=pltpu.PrefetchScalarGridSpec(num_scalar_prefetch=1, grid=(N_SHARDS, split),
                                               in_specs=in_specs + land_specs, out_specs=out_specs),
        out_shape=[jax.ShapeDtypeStruct(l.shape, dt) for l, dt in zip(landed, wire_dtypes)],
        compiler_params=_params("parallel", "parallel"), name="reduce_pair_sum",
    )(c.reshape(1), *views, *landed)


def _chip_sum_call(partials, slots, me, c):
    na = len(slots)
    split = 2

    def body(place_ref, *refs):
        mine, lands, outs = refs[:na], refs[na:2 * na], refs[2 * na:]
        for j in range(N_SHARDS):
            @pl.when(place_ref[0] == j)
            def _():
                for a in range(na):
                    terms = [mine[a][...] if k == j else lands[a][k] for k in range(N_SHARDS)]
                    t0, t1, t2, t3 = (term.astype(F32) for term in terms)
                    outs[a][...] = ((t0 + t1) + t2) + t3

    mine_specs, land_specs, out_specs = [], [], []
    for s in slots:
        ns, h, cols = s.shape
        mine_specs.append(pl.BlockSpec((None, h // split, cols), lambda i, place: (place[0], i, 0)))
        land_specs.append(pl.BlockSpec((ns, h // split, cols), lambda i, place: (0, i, 0)))
        out_specs.append(pl.BlockSpec((h // split, cols), lambda i, place: (place[1] * split + i, 0)))
    return pl.pallas_call(
        body,
        grid_spec=pltpu.PrefetchScalarGridSpec(num_scalar_prefetch=1, grid=(split,),
                                               in_specs=mine_specs + land_specs, out_specs=out_specs),
        out_shape=[jax.ShapeDtypeStruct((2 * s.shape[1], s.shape[2]), F32) for s in slots],
        compiler_params=_params("parallel"), name="reduce_chip_sum",
    )(jnp.stack([me, c]), *partials, *slots)


def _adamw(w, g, m, v):
    m = ADAM_B1 * m + (1.0 - ADAM_B1) * g
    v = ADAM_B2 * v + (1.0 - ADAM_B2) * jnp.square(g)
    m_hat = m / (1.0 - ADAM_B1 ** ADAM_STEP)
    v_hat = v / (1.0 - ADAM_B2 ** ADAM_STEP)
    return -ADAM_LR * (m_hat / (jnp.sqrt(v_hat) + ADAM_EPS) + ADAM_WD * w), m, v


def _adamw_call(ws, gs, ms, vs):
    na = len(ws)
    split = 4

    def body(*refs):
        w_refs, g_refs, m_refs, v_refs = (refs[k * na:(k + 1) * na] for k in range(4))
        d_refs, nm_refs, nv_refs = (refs[(4 + k) * na:(5 + k) * na] for k in range(3))
        for a in range(na):
            d_refs[a][...], nm_refs[a][...], nv_refs[a][...] = _adamw(
                w_refs[a][...], g_refs[a][...], m_refs[a][...], v_refs[a][...])

    specs = [pl.BlockSpec((w.shape[0] // split, w.shape[1]), lambda i: (i, 0)) for w in ws]
    shapes = [jax.ShapeDtypeStruct(w.shape, F32) for w in ws]
    outs = pl.pallas_call(
        body, grid=(split,), in_specs=specs * 4, out_specs=specs * 3, out_shape=shapes * 3,
        compiler_params=_params("parallel"), name="adamw",
    )(*ws, *gs, *ms, *vs)
    return outs[:na], outs[na:2 * na], outs[2 * na:]


def _small_rows():
    first, row = {}, 0
    for name, (r, c) in SMALL_VIEWS.items():
        first[name] = row
        row += r * (-(-c // LANES))
    return first, row, -(-(row + 1) // 64) * 64


def _pack_small(arrays, loss):
    _, used, rows = _small_rows()
    parts = []
    for name, (r, c) in SMALL_VIEWS.items():
        a = arrays[name].reshape(r, c)
        parts.append(jnp.pad(a, ((0, 0), (0, LANES - c))) if c < LANES else a.reshape(r * c // LANES, LANES))
    parts += [loss, jnp.zeros((rows - used - 1, LANES), F32)]
    return jnp.concatenate(parts, axis=0)


def _adamw_small_call(ws, packed_grads, ms, vs):
    names = list(SMALL_VIEWS)
    first, _, _ = _small_rows()
    n = len(names)

    def body(g_ref, *refs):
        w_refs, m_refs, v_refs = refs[:n], refs[n:2 * n], refs[2 * n:3 * n]
        outs = refs[3 * n:]
        for a, name in enumerate(names):
            r, c = SMALL_VIEWS[name]
            r0 = first[name]
            if c == LANES:
                g = g_ref[r0:r0 + r, :]
            elif c < LANES:
                g = g_ref[r0:r0 + 1, 0:c]
            else:
                g = jnp.concatenate([g_ref[r0 + j:r0 + j + 1, :] for j in range(c // LANES)], axis=1)
            delta, m, v = _adamw(w_refs[a][...], g, m_refs[a][...], v_refs[a][...])
            for k, val in enumerate((g, delta, m, v)):
                outs[k * n + a][...] = val

    shapes = [jax.ShapeDtypeStruct(SMALL_VIEWS[name], F32) for name in names]
    outs = pl.pallas_call(
        body, out_shape=shapes * 4, compiler_params=pltpu.CompilerParams(vmem_limit_bytes=VMEM_LIMIT),
        name="adamw_small",
    )(packed_grads, *[ws[k] for k in names], *[ms[k] for k in names], *[vs[k] for k in names])
    return [dict(zip(names, outs[k * n:(k + 1) * n])) for k in range(4)]


def kernel(x, g_mix, w_in, b_in, sinks, w_pool, b_pool, pool_scale, w_out, b_out, g_ffn, w_gate, w_up, w_down, g_final, loss_target, m_g_mix, m_w_in, m_b_in, m_sinks, m_w_pool, m_b_pool, m_pool_scale, m_w_out, m_b_out, m_g_ffn, m_w_gate, m_w_up, m_w_down, m_g_final, v_g_mix, v_w_in, v_b_in, v_sinks, v_w_pool, v_b_pool, v_pool_scale, v_w_out, v_b_out, v_g_ffn, v_w_gate, v_w_up, v_w_down, v_g_final):
    given = dict(locals())
    big_names = ("w_in", "w_out", "w_gate", "w_up", "w_down")
    out_shapes = {n: given[n].shape for n in SMALL_NAMES + big_names}
    c = lax.axis_index("c")

    me = 2 * lax.axis_index("x") + lax.axis_index("y")
    transposed = ("w_in", "w_gate", "w_up")

    def shard_view(a, name):
        return a[0].T if name in transposed else a[0]

    placed = _cast_place_call([shard_view(given[n], n) for n in big_names], me)

    small = dict(g_mix=g_mix, b_in=b_in, sinks=sinks, w_pool=w_pool[0], b_pool=b_pool.reshape(1, POOL_WIDTH),
                 pool_scale=pool_scale.reshape(1, POOL_WIDTH), b_out=b_out, g_ffn=g_ffn,
                 g_final=g_final.reshape(1, D_MODEL))
    loss, grad_x, small_grads, big_grads = _local_step(
        x[0], loss_target[0], small, placed[0], placed[1], placed[2:], place=(me, c))

    packed = _pack_small(small_grads, loss)
    grads = [
        big_grads["w_in"].reshape(N_SHARDS, IN_WIDTH // N_SHARDS, D_MODEL),
        big_grads["w_out"].reshape(N_SHARDS, D_MODEL // N_SHARDS, D_MODEL),
        jnp.broadcast_to(packed[None], (N_SHARDS,) + packed.shape),
    ]
    alone = [functools.partial(_exchange_call, name=name) for name in
             ("reduce_sibling_exchange", "reduce_chip_exchange")]
    h_in, h_out, h_small = _reduce_to_owner(grads, (me, c), alone, [BF16, BF16, F32])
    totals = _exchange_call(_sibling_allgather(
        [h_in, h_out, big_grads["w_gate"], big_grads["w_up"], big_grads["w_down"], h_small]), "reduce_sibling_allgather")
    loss = totals[-1][_small_rows()[1], 0]

    def owned(prefix):
        return [shard_view(given[prefix + n], n) for n in big_names]

    def owned_small(prefix):
        return {n: given[prefix + n].reshape(SMALL_VIEWS[n]) for n in SMALL_NAMES}

    big = (totals[:-1],) + _adamw_call(owned(""), totals[:-1], owned("m_"), owned("v_"))
    small_groups = _adamw_small_call(owned_small(""), totals[-1], owned_small("m_"), owned_small("v_"))

    order = ("g_mix", "w_in", "b_in", "sinks", "w_pool", "b_pool", "pool_scale", "w_out", "b_out", "g_ffn",
             "w_gate", "w_up", "w_down", "g_final")
    results = [loss, grad_x.reshape(x.shape)]
    for group, small_group in zip(big, small_groups):
        named = {n: (a.T if n in transposed else a) for n, a in zip(big_names, group)}
        named.update(small_group)
        results += [named[n].reshape(out_shapes[n]) for n in order]
    return tuple(results)
```

```python
import functools
from typing import Any, Callable, Mapping, NamedTuple, Sequence

import jax
import jax.numpy as jnp
from jax import lax
from jax.experimental import pallas as pl
from jax.experimental.pallas import tpu as pltpu

F32 = jnp.float32
BF16 = jnp.bfloat16

D_MODEL = 1024
ATTN_WIDTH = 512
KV_WIDTH = 128
POOL_WIDTH = 512
IN_WIDTH = ATTN_WIDTH + 2 * KV_WIDTH + POOL_WIDTH
HEAD_DIM = 64
N_Q_HEADS = 8
BLOCK = 128
POOL_SIZES = (2, 4, 8, 16)
POOL_HALO = 16
ROPE_THETA = 10000.0
RMS_EPS = 1e-5
Q_SCALE = HEAD_DIM ** -0.5

ADAM_LR = 0.001
ADAM_B1 = 0.9
ADAM_B2 = 0.999
ADAM_EPS = 1e-08
ADAM_WD = 0.01
ADAM_STEP = 10

N_SHARDS = 4
LANES = 128
VMEM_LIMIT = 56 * 1024 * 1024
FFN_TOKEN_TILE = 256

MESH = pl.DeviceIdType.MESH
HBM_SPEC = pl.BlockSpec(memory_space=pltpu.HBM)

SMALL_VIEWS = dict(w_pool=(4 * LANES, LANES), b_pool=(4, LANES), pool_scale=(4, LANES), g_mix=(1, D_MODEL),
                   b_in=(1, IN_WIDTH), b_out=(1, D_MODEL), g_ffn=(1, D_MODEL), g_final=(1, D_MODEL),
                   sinks=(1, N_Q_HEADS))
SMALL_NAMES = tuple(SMALL_VIEWS)


def _dot(a, b):
    return jnp.dot(a, b, preferred_element_type=F32)


def _dot_nt(a, b):
    return lax.dot_general(a, b, (((1,), (1,)), ((), ())), preferred_element_type=F32)


def _dot_tn(a, b):
    return lax.dot_general(a, b, (((0,), (0,)), ((), ())), preferred_element_type=F32)


def _params(*semantics):
    return pltpu.CompilerParams(dimension_semantics=semantics, vmem_limit_bytes=VMEM_LIMIT)


def _resident(shape):
    return pl.BlockSpec(shape, lambda *_: (0,) * len(shape), pipeline_mode=pl.Buffered(1))


def _token_tile(t, largest=512):
    for tm in (1024, 512, 256, 128):
        if tm <= largest and t % tm == 0:
            return tm
    raise ValueError(f"sequence length {t} is not a multiple of 128")


def _rms_stats(xf):
    r = lax.rsqrt(jnp.mean(xf * xf, axis=-1, keepdims=True) + RMS_EPS)
    return xf * r, r


def _rms_bwd(dy, xh, r, g):
    dxh = dy * g
    return r * (dxh - xh * jnp.mean(dxh * xh, axis=-1, keepdims=True))


def _rope_partner(t, first):
    return jnp.where(first, pltpu.roll(t, LANES - HEAD_DIM // 2, 1), pltpu.roll(t, HEAD_DIM // 2, 1))


def _first_half_mask(shape):
    lane = lax.broadcasted_iota(jnp.int32, shape, 1)
    return (lane % HEAD_DIM) < (HEAD_DIM // 2)


def _rope_tables(t):
    inv_freq = 1.0 / (ROPE_THETA ** (jnp.arange(0, HEAD_DIM, 2, dtype=F32) / HEAD_DIM))
    ang = jnp.arange(t, dtype=F32)[:, None] * inv_freq[None, :]
    cos, sin = jnp.cos(ang), jnp.sin(ang)
    cos_t = jnp.tile(jnp.concatenate([cos, cos], axis=-1), (1, LANES // HEAD_DIM))
    sin_t = jnp.tile(jnp.concatenate([-sin, sin], axis=-1), (1, LANES // HEAD_DIM))
    return cos_t, sin_t


class _Exchange(NamedTuple):
    inputs: Sequence[Any]
    out_shapes: Sequence[Any]
    aliases: Mapping[int, int]
    sems: Sequence[Any]
    start: Callable[..., None]
    finish: Callable[..., None]


def _compute_call(body, operands, *, grid, in_specs, out_specs, out_shape, scratch_shapes=(), semantics, name,
                  exchange=None):
    n_in, n_out, n_scr = len(in_specs), len(out_specs), len(scratch_shapes)
    if exchange is None:
        return pl.pallas_call(body, grid=grid, in_specs=in_specs, out_specs=out_specs, out_shape=out_shape,
                              scratch_shapes=scratch_shapes, compiler_params=_params(*semantics),
                              name=name)(*operands), []
    k_in, k_out = len(exchange.inputs), len(exchange.out_shapes)

    def hosting(*refs):
        ins, rest = refs[:n_in], refs[n_in:]
        ex_ins, rest = rest[:k_in], rest[k_in:]
        outs, rest = rest[:n_out], rest[n_out:]
        ex_outs, rest = rest[:k_out], rest[k_out:]
        scratch, sems = rest[:n_scr], rest[n_scr:]
        steps = [pl.program_id(d) for d in range(len(grid))]
        is_first = functools.reduce(jnp.logical_and, [s == 0 for s in steps])
        is_last = functools.reduce(jnp.logical_and, [s == g - 1 for s, g in zip(steps, grid)])

        @pl.when(is_first)
        def _():
            exchange.start(ex_ins, ex_outs, sems)

        body(*ins, *outs, *scratch)

        @pl.when(is_last)
        def _():
            exchange.finish(ex_ins, ex_outs, sems)

    results = pl.pallas_call(
        hosting, grid=grid,
        in_specs=list(in_specs) + [HBM_SPEC] * k_in, out_specs=list(out_specs) + [HBM_SPEC] * k_out,
        out_shape=list(out_shape) + list(exchange.out_shapes),
        scratch_shapes=list(scratch_shapes) + list(exchange.sems),
        input_output_aliases={n_in + i: n_out + j for i, j in exchange.aliases.items()},
        compiler_params=_params(*semantics), name=name,
    )(*operands, *exchange.inputs)
    return results[:n_out], results[n_out:]


def _exchange_call(exchange, name):
    n_in, n_out = len(exchange.inputs), len(exchange.out_shapes)

    def body(*refs):
        ins, outs, sems = refs[:n_in], refs[n_in:n_in + n_out], refs[n_in + n_out:]
        exchange.start(ins, outs, sems)
        exchange.finish(ins, outs, sems)

    return pl.pallas_call(
        body, in_specs=[HBM_SPEC] * n_in, out_specs=[HBM_SPEC] * n_out, out_shape=list(exchange.out_shapes),
        input_output_aliases=dict(exchange.aliases), scratch_shapes=list(exchange.sems),
        compiler_params=pltpu.CompilerParams(has_side_effects=True), name=name,
    )(*exchange.inputs)


def _mesh_place():
    x, y, c = lax.axis_index("x"), lax.axis_index("y"), lax.axis_index("c")
    other_chips = [(1 - x, y), (x, 1 - y), (1 - x, 1 - y)]
    return x, y, c, other_chips


def _half(ref, c, rows):
    return ref.at[pl.ds(pl.multiple_of(c * rows, 16), rows), :]


def _remote(src, dst, send_sems, recv_sems, idx, to):
    return functools.partial(pltpu.make_async_remote_copy, src_ref=src, dst_ref=dst, send_sem=send_sems.at[idx],
                             recv_sem=recv_sems.at[idx], device_id=to, device_id_type=MESH)


def _gather_exchange(bufs):
    na = len(bufs)

    def copies(outs, sems):
        send_sems, recv_sems = sems
        x, y, c, other_chips = _mesh_place()
        me, sibling = 2 * x + y, (x, y, 1 - c)
        ici_out, ici_in, fwd_out, fwd_in = [], [], [], []
        for j, (cx, cy) in enumerate(other_chips):
            for a in range(na):
                rows = outs[a].shape[1] // 2
                mine = _half(outs[a].at[me], c, rows)
                landed = _half(outs[a].at[2 * cx + cy], c, rows)
                passed = _half(outs[a].at[2 * cx + cy], 1 - c, rows)
                ici_out.append(_remote(mine, mine, send_sems, recv_sems, j * na + a, (cx, cy, c)))
                ici_in.append(_remote(landed, landed, send_sems, recv_sems, j * na + a, (cx, cy, c)))
                fwd_out.append(_remote(landed, landed, send_sems, recv_sems, (3 + j) * na + a, sibling))
                fwd_in.append(_remote(passed, passed, send_sems, recv_sems, (3 + j) * na + a, sibling))
        return ici_out, ici_in, fwd_out, fwd_in

    def start(ins, outs, sems):
        for cp in copies(outs, sems)[0]:
            cp().start()

    def finish(ins, outs, sems):
        ici_out, ici_in, fwd_out, fwd_in = copies(outs, sems)
        for arrived, forward in zip(ici_in, fwd_out):
            arrived().wait_recv()
            forward().start()
        for cp in fwd_in:
            cp().wait_recv()
        for cp in ici_out + fwd_out:
            cp().wait_send()

    return _Exchange(inputs=bufs, out_shapes=[jax.ShapeDtypeStruct(b.shape, b.dtype) for b in bufs],
                     aliases={a: a for a in range(na)},
                     sems=[pltpu.SemaphoreType.DMA((6 * na,)), pltpu.SemaphoreType.DMA((6 * na,))],
                     start=start, finish=finish)


def _sibling_exchange(grads):
    na = len(grads)

    def copies(ins, outs, sems):
        x, y, c, _ = _mesh_place()
        out = []
        for a in range(na):
            rows = ins[a].shape[1] // 2
            src = ins[a].at[:, pl.ds(pl.multiple_of((1 - c) * rows, 8), rows), :]
            out.append(_remote(src, outs[a], sems[0], sems[1], a, (x, y, 1 - c)))
        return out

    def start(ins, outs, sems):
        for cp in copies(ins, outs, sems):
            cp().start()

    def finish(ins, outs, sems):
        for cp in copies(ins, outs, sems):
            cp().wait_recv()
            cp().wait_send()

    return _Exchange(inputs=grads,
                     out_shapes=[jax.ShapeDtypeStruct((g.shape[0], g.shape[1] // 2, g.shape[2]), g.dtype)
                                 for g in grads],
                     aliases={}, sems=[pltpu.SemaphoreType.DMA((na,)), pltpu.SemaphoreType.DMA((na,))],
                     start=start, finish=finish)


def _chip_exchange(partials):
    na = len(partials)

    def copies(ins, outs, sems):
        x, y, c, other_chips = _mesh_place()
        me = 2 * x + y
        sends, arrivals = [], []
        for j, (cx, cy) in enumerate(other_chips):
            for a in range(na):
                slot = outs[a].at[2 * cx + cy]
                sends.append(_remote(ins[a].at[2 * cx + cy], outs[a].at[me], sems[0], sems[1], j * na + a, (cx, cy, c)))
                arrivals.append(_remote(slot, slot, sems[0], sems[1], j * na + a, (cx, cy, c)))
        return sends, arrivals

    def start(ins, outs, sems):
        for cp in copies(ins, outs, sems)[0]:
            cp().start()

    def finish(ins, outs, sems):
        sends, arrivals = copies(ins, outs, sems)
        for cp in arrivals:
            cp().wait_recv()
        for cp in sends:
            cp().wait_send()

    return _Exchange(inputs=partials, out_shapes=[jax.ShapeDtypeStruct(p.shape, p.dtype) for p in partials],
                     aliases={}, sems=[pltpu.SemaphoreType.DMA((3 * na,)), pltpu.SemaphoreType.DMA((3 * na,))],
                     start=start, finish=finish)


def _sibling_allgather(bufs):
    na = len(bufs)

    def piece(outs, sems, a, half):
        x, y, c, _ = _mesh_place()
        rows = outs[a].shape[0] // 2
        rows_ref = outs[a].at[pl.ds(pl.multiple_of(half * rows, 8), rows), :]
        return _remote(rows_ref, rows_ref, sems[0], sems[1], a, (x, y, 1 - c))

    def start(ins, outs, sems):
        c = lax.axis_index("c")
        for a in range(na):
            piece(outs, sems, a, c)().start()

    def finish(ins, outs, sems):
        c = lax.axis_index("c")
        for a in range(na):
            piece(outs, sems, a, 1 - c)().wait_recv()
        for a in range(na):
            piece(outs, sems, a, c)().wait_send()

    return _Exchange(inputs=bufs, out_shapes=[jax.ShapeDtypeStruct(b.shape, b.dtype) for b in bufs],
                     aliases={a: a for a in range(na)},
                     sems=[pltpu.SemaphoreType.DMA((na,)), pltpu.SemaphoreType.DMA((na,))],
                     start=start, finish=finish)


def _norm_call(x, g_mix, exchange=None):
    t = x.shape[0]
    tm = _token_tile(t)

    def body(x_ref, g_ref, h_ref):
        xh, _ = _rms_stats(x_ref[...])
        h_ref[...] = (xh * g_ref[...]).astype(BF16)

    row = pl.BlockSpec((tm, D_MODEL), lambda i: (i, 0))
    return _compute_call(body, (x, g_mix), grid=(t // tm,), in_specs=[row, _resident((1, D_MODEL))],
                         out_specs=[row], out_shape=[jax.ShapeDtypeStruct((t, D_MODEL), BF16)],
                         semantics=("parallel",), name="norm_fwd", exchange=exchange)


def _inproj_call(h0, w_in, b_in, cos_t, sin_t, exchange=None):
    t = h0.shape[0]
    tm = _token_tile(t, 1024)

    def body(h_ref, w_ref, b_ref, c_ref, s_ref, q_ref, k_ref, v_ref, u_ref):
        z = _dot_nt(h_ref[...], w_ref[...]) + b_ref[...]
        cos, sin = c_ref[...], s_ref[...]
        first = _first_half_mask((tm, LANES))
        for j in range(ATTN_WIDTH // LANES):
            zj = z[:, LANES * j:LANES * (j + 1)]
            q_ref[:, LANES * j:LANES * (j + 1)] = ((zj * cos + _rope_partner(zj, first) * sin) * Q_SCALE).astype(BF16)
        zk = z[:, ATTN_WIDTH:ATTN_WIDTH + KV_WIDTH]
        k_ref[...] = (zk * cos + _rope_partner(zk, first) * sin).astype(BF16)
        v_ref[...] = z[:, ATTN_WIDTH + KV_WIDTH:ATTN_WIDTH + 2 * KV_WIDTH].astype(BF16)
        u_ref[...] = z[:, ATTN_WIDTH + 2 * KV_WIDTH:]

    row = lambda w: pl.BlockSpec((tm, w), lambda i: (i, 0))
    return _compute_call(
        body, (h0, w_in, b_in, cos_t, sin_t), grid=(t // tm,),
        in_specs=[row(D_MODEL), _resident((IN_WIDTH, D_MODEL)), _resident((1, IN_WIDTH)), row(LANES), row(LANES)],
        out_specs=[row(ATTN_WIDTH), row(KV_WIDTH), row(KV_WIDTH), row(POOL_WIDTH)],
        out_shape=[jax.ShapeDtypeStruct((t, ATTN_WIDTH), BF16),
                   jax.ShapeDtypeStruct((t, KV_WIDTH), BF16), jax.ShapeDtypeStruct((t, KV_WIDTH), BF16),
                   jax.ShapeDtypeStruct((t, POOL_WIDTH), F32)],
        semantics=("parallel",), name="inproj_fwd", exchange=exchange)


def _head_in_both_halves(band, kv_head):
    lane = lax.broadcasted_iota(jnp.int32, band.shape, 1)
    own = (lane < HEAD_DIM) if kv_head == 0 else (lane >= HEAD_DIM)
    return jnp.where(own, band, pltpu.roll(band, HEAD_DIM, 1)).astype(BF16)


def _stack_heads(ref, kv_head, row0):
    low = lax.broadcasted_iota(jnp.int32, (BLOCK, LANES), 1) < HEAD_DIM
    parts = []
    for jj in range(2):
        slab = ref[row0:row0 + BLOCK, LANES * (2 * kv_head + jj):LANES * (2 * kv_head + jj + 1)]
        zero = jnp.zeros_like(slab)
        parts += [jnp.where(low, slab, zero), jnp.where(low, zero, slab)]
    return jnp.concatenate(parts, axis=0)


def _band_biases():
    c = lax.broadcasted_iota(jnp.int32, (2 * BLOCK, 4 * BLOCK), 0)
    i = lax.broadcasted_iota(jnp.int32, (2 * BLOCK, 4 * BLOCK), 1) % BLOCK
    vis = (c > i) & (c <= i + BLOCK)
    return jnp.where(vis, 0.0, -jnp.inf).astype(F32), jnp.where(vis & (c >= BLOCK), 0.0, -jnp.inf).astype(F32)


def _blocks_per_step(nb):
    for g in (4, 2, 1):
        if nb % g == 0:
            return g


def _sink_row(sink_ref, kv_head):
    return jnp.concatenate([jnp.full((1, BLOCK), sink_ref[0, 4 * kv_head + g], F32) for g in range(4)], axis=1)


def _kv_band(cur_ref, prev_ref, g):
    r0 = g * BLOCK
    prev = prev_ref[...] if g == 0 else cur_ref[r0 - BLOCK:r0, :]
    return jnp.concatenate([prev, cur_ref[r0:r0 + BLOCK, :]], axis=0).astype(F32)


def _attn_fwd_call(q, k, v, sinks, bias, bias_first, exchange=None):
    t = q.shape[0]
    nb = t // BLOCK
    gb = _blocks_per_step(nb)
    qb = gb * BLOCK

    def body(sink_ref, q_ref, kc_ref, kp_ref, vc_ref, vp_ref, b_ref, bf_ref, o_ref):
        n = pl.program_id(0)
        row_low = lax.broadcasted_iota(jnp.int32, (LANES, BLOCK), 0) < HEAD_DIM
        for g in range(gb):
            r0 = g * BLOCK
            kband = _kv_band(kc_ref, kp_ref, g)
            vband_t = _kv_band(vc_ref, vp_ref, g).T
            bias_g = jnp.where(n == 0, bf_ref[...], b_ref[...]) if g == 0 else b_ref[...]
            for kh in range(2):
                k2 = _head_in_both_halves(kband, kh)
                vt_head = vband_t[HEAD_DIM * kh:HEAD_DIM * (kh + 1), :]
                v2_t = jnp.concatenate([vt_head, vt_head], axis=0).astype(BF16)
                s_t = _dot_nt(k2, _stack_heads(q_ref, kh, r0)) + bias_g
                sink = _sink_row(sink_ref, kh)
                m = jnp.maximum(jnp.max(s_t, axis=0, keepdims=True), sink)
                e = jnp.exp(s_t - m)
                inv = 1.0 / (jnp.sum(e, axis=0, keepdims=True) + jnp.exp(sink - m))
                o_t = _dot(v2_t, (e * inv).astype(BF16))
                for jj in range(2):
                    slab = jnp.where(row_low, o_t[:, BLOCK * 2 * jj:BLOCK * (2 * jj + 1)],
                                     o_t[:, BLOCK * (2 * jj + 1):BLOCK * (2 * jj + 2)]).T
                    o_ref[r0:r0 + BLOCK, LANES * (2 * kh + jj):LANES * (2 * kh + jj + 1)] = slab.astype(BF16)

    cur = lambda w: pl.BlockSpec((qb, w), lambda n: (n, 0))
    prev = lambda w: pl.BlockSpec((BLOCK, w), lambda n: (jnp.maximum(n * gb - 1, 0), 0))
    return _compute_call(
        body, (sinks, q, k, k, v, v, bias, bias_first), grid=(nb // gb,),
        in_specs=[pl.BlockSpec(memory_space=pltpu.SMEM), cur(ATTN_WIDTH), cur(KV_WIDTH), prev(KV_WIDTH),
                  cur(KV_WIDTH), prev(KV_WIDTH), _resident(bias.shape), _resident(bias.shape)],
        out_specs=[cur(ATTN_WIDTH)],
        out_shape=[jax.ShapeDtypeStruct((t, ATTN_WIDTH), BF16)],
        semantics=("parallel",), name="attn_fwd", exchange=exchange)


def _pool_mixed(sc_ref, u_ref, halo_ref, i, tm):
    sc_ref[0:POOL_HALO, :] = jnp.where(i == 0, 0.0, halo_ref[...])
    sc_ref[POOL_HALO:, :] = u_ref[...]
    tok = i * tm + lax.broadcasted_iota(jnp.int32, (tm, 1), 0)
    mixed = []
    for g, size in enumerate(POOL_SIZES):
        lanes = slice(LANES * g, LANES * (g + 1))
        cur = sc_ref[pl.ds(POOL_HALO, tm), lanes]
        acc = cur
        for j in range(1, size):
            acc = acc + sc_ref[pl.ds(POOL_HALO - j, tm), lanes]
        count = jnp.minimum(tok + 1, size).astype(F32)
        mixed.append(acc / count - cur)
    return mixed


def _halo_before(tm, width):
    return pl.BlockSpec((POOL_HALO, width), lambda i: (jnp.maximum(i * (tm // POOL_HALO) - 1, 0), 0))


def _outproj_call(attn, u, x, w_pool, b_pool, pool_scale, w_out, b_out, g_ffn, exchange=None):
    t = x.shape[0]
    tm = _token_tile(t)

    def body(a_ref, u_ref, halo_ref, x_ref, wp_ref, bp_ref, ps_ref, wo_ref, bo_ref, g_ref,
             x1_ref, h1_ref, y_ref, sc_ref):
        i = pl.program_id(0)
        for g, mixed in enumerate(_pool_mixed(sc_ref, u_ref, halo_ref, i, tm)):
            lanes = slice(LANES * g, LANES * (g + 1))
            lin = _dot(mixed.astype(BF16), wp_ref[g]) + bp_ref[:, lanes]
            y_ref[:, lanes] = (lin * ps_ref[:, lanes]).astype(BF16)
        x1 = (x_ref[...] + _dot(a_ref[...], wo_ref[0:ATTN_WIDTH, :]) + _dot(y_ref[...], wo_ref[ATTN_WIDTH:, :])
              + bo_ref[...])
        x1_ref[...] = x1
        xh, _ = _rms_stats(x1)
        h1_ref[...] = (xh * g_ref[...]).astype(BF16)

    row = lambda w: pl.BlockSpec((tm, w), lambda i: (i, 0))
    return _compute_call(
        body, (attn, u, u, x, w_pool, b_pool, pool_scale, w_out, b_out, g_ffn), grid=(t // tm,),
        in_specs=[row(ATTN_WIDTH), row(POOL_WIDTH), _halo_before(tm, POOL_WIDTH), row(D_MODEL),
                  _resident((4, LANES, LANES)), _resident((1, POOL_WIDTH)), _resident((1, POOL_WIDTH)),
                  _resident((D_MODEL, D_MODEL)), _resident((1, D_MODEL)), _resident((1, D_MODEL))],
        out_specs=[row(D_MODEL), row(D_MODEL), row(POOL_WIDTH)],
        out_shape=[jax.ShapeDtypeStruct((t, D_MODEL), F32), jax.ShapeDtypeStruct((t, D_MODEL), BF16),
                   jax.ShapeDtypeStruct((t, POOL_WIDTH), BF16)],
        scratch_shapes=[pltpu.VMEM((tm + POOL_HALO, POOL_WIDTH), F32)],
        semantics=("parallel",), name="outproj_fwd", exchange=exchange)


def _ffn_fwd_call(h1, x1, target, w_gate, w_up, w_down, g_final):
    t = x1.shape[0]
    tm = _token_tile(t, 2 * FFN_TOKEN_TILE)
    ns, fs, _ = w_gate.shape

    def body(h_ref, x1_ref, tgt_ref, wg_ref, wu_ref, wd_ref, gf_ref,
             gate_ref, up_ref, act_ref, dx2_ref, loss_ref, dgf_ref):
        @pl.when(pl.program_id(0) == 0)
        def _():
            loss_ref[...] = jnp.zeros_like(loss_ref)
            dgf_ref[...] = jnp.zeros_like(dgf_ref)

        h = h_ref[...]
        x2 = x1_ref[...]
        for s in range(ns):
            gate = _dot_nt(h, wg_ref[s])
            up = _dot_nt(h, wu_ref[s])
            act = (gate * jax.nn.sigmoid(gate) * up).astype(BF16)
            gate_ref[s] = gate.astype(BF16)
            up_ref[s] = up.astype(BF16)
            act_ref[s] = act
            x2 = x2 + _dot(act, wd_ref[s])
        xh, r = _rms_stats(x2)
        gf = gf_ref[...]
        err = xh * gf - tgt_ref[...]
        loss_ref[...] += 0.5 * jnp.sum(err * err) / D_MODEL
        dy = err / D_MODEL
        dgf_ref[...] += jnp.sum(dy * xh, axis=0, keepdims=True)
        dx2_ref[...] = _rms_bwd(dy, xh, r, gf)

    row = lambda w: pl.BlockSpec((tm, w), lambda i: (i, 0))
    shard_act = pl.BlockSpec((ns, tm, fs), lambda i: (0, i, 0))
    act_shape = jax.ShapeDtypeStruct((ns, t, fs), BF16)
    return pl.pallas_call(
        body, grid=(t // tm,),
        in_specs=[row(D_MODEL), row(D_MODEL), row(D_MODEL), _resident((ns, fs, D_MODEL)),
                  _resident((ns, fs, D_MODEL)), _resident((ns, fs, D_MODEL)), _resident((1, D_MODEL))],
        out_specs=[shard_act, shard_act, shard_act, row(D_MODEL),
                   pl.BlockSpec((1, LANES), lambda i: (0, 0)), pl.BlockSpec((1, D_MODEL), lambda i: (0, 0))],
        out_shape=[act_shape, act_shape, act_shape, jax.ShapeDtypeStruct((t, D_MODEL), F32),
                   jax.ShapeDtypeStruct((1, LANES), F32), jax.ShapeDtypeStruct((1, D_MODEL), F32)],
        compiler_params=_params("arbitrary"), name="ffn_fwd",
    )(h1, x1, target, w_gate, w_up, w_down, g_final)


def _ffn_bwd_act_call(dx2, gate, up, x1, w_gate, w_up, w_down, g_ffn):
    t = x1.shape[0]
    tm = _token_tile(t, FFN_TOKEN_TILE)
    ns, fs, _ = w_gate.shape

    def body(dx2_ref, gate_ref, up_ref, x1_ref, wg_ref, wu_ref, wd_ref, g_ref,
             dgate_ref, dup_ref, dx1_ref, dx2b_ref, dg_ref):
        @pl.when(pl.program_id(0) == 0)
        def _():
            dg_ref[...] = jnp.zeros_like(dg_ref)

        dx2 = dx2_ref[...]
        dx2b = dx2.astype(BF16)
        dx2b_ref[...] = dx2b
        dh = None
        for s in range(ns):
            dact = _dot_nt(dx2b, wd_ref[s])
            gate = gate_ref[s].astype(F32)
            sg = jax.nn.sigmoid(gate)
            dup = (dact * (gate * sg)).astype(BF16)
            dgate = (dact * up_ref[s].astype(F32) * (sg * (1.0 + gate * (1.0 - sg)))).astype(BF16)
            dgate_ref[s] = dgate
            dup_ref[s] = dup
            part = _dot(dgate, wg_ref[s]) + _dot(dup, wu_ref[s])
            dh = part if dh is None else dh + part
        xh, r = _rms_stats(x1_ref[...])
        dg_ref[...] += jnp.sum(dh * xh, axis=0, keepdims=True)
        dx1_ref[...] = dx2 + _rms_bwd(dh, xh, r, g_ref[...])

    row = lambda w: pl.BlockSpec((tm, w), lambda i: (i, 0))
    shard_act = pl.BlockSpec((ns, tm, fs), lambda i: (0, i, 0))
    act_shape = jax.ShapeDtypeStruct((ns, t, fs), BF16)
    return pl.pallas_call(
        body, grid=(t // tm,),
        in_specs=[row(D_MODEL), shard_act, shard_act, row(D_MODEL), _resident((ns, fs, D_MODEL)),
                  _resident((ns, fs, D_MODEL)), _resident((ns, fs, D_MODEL)), _resident((1, D_MODEL))],
        out_specs=[shard_act, shard_act, row(D_MODEL), row(D_MODEL), pl.BlockSpec((1, D_MODEL), lambda i: (0, 0))],
        out_shape=[act_shape, act_shape, jax.ShapeDtypeStruct((t, D_MODEL), F32),
                   jax.ShapeDtypeStruct((t, D_MODEL), BF16), jax.ShapeDtypeStruct((1, D_MODEL), F32)],
        compiler_params=_params("arbitrary"), name="ffn_bwd_act",
    )(dx2, gate, up, x1, w_gate, w_up, w_down, g_ffn)


def _ffn_bwd_weights_call(h1, dgate, dup, act, dx2):
    t = h1.shape[0]
    tm = _token_tile(t, 1024)
    ns, _, fs = dgate.shape

    def body(h_ref, dgate_ref, dup_ref, act_ref, dx2_ref, gg_ref, gu_ref, gd_ref):
        @pl.when(pl.program_id(1) == 0)
        def _():
            gg_ref[...] = jnp.zeros_like(gg_ref)
            gu_ref[...] = jnp.zeros_like(gu_ref)
            gd_ref[...] = jnp.zeros_like(gd_ref)

        h = h_ref[...]
        gg_ref[...] += _dot_tn(dgate_ref[...], h)
        gu_ref[...] += _dot_tn(dup_ref[...], h)
        gd_ref[...] += _dot_tn(act_ref[...], dx2_ref[...])

    row = lambda w: pl.BlockSpec((tm, w), lambda s, i: (i, 0))
    shard_act = pl.BlockSpec((None, tm, fs), lambda s, i: (s, i, 0))
    return pl.pallas_call(
        body, grid=(ns, t // tm),
        in_specs=[row(D_MODEL), shard_act, shard_act, shard_act, row(D_MODEL)],
        out_specs=[pl.BlockSpec((None, fs, D_MODEL), lambda s, i: (s, 0, 0))] * 3,
        out_shape=[jax.ShapeDtypeStruct((ns, fs, D_MODEL), F32)] * 3,
        compiler_params=_params("parallel", "arbitrary"), name="ffn_bwd_weights",
    )(h1, dgate, dup, act, dx2)


def _outproj_bwd_call(dx1, attn, pool_y, u, w_pool, b_pool, pool_scale, w_out, exchange=None):
    t = dx1.shape[0]
    tm = _token_tile(t, 1024)

    def body(dx1_ref, a_ref, y_ref, u_ref, halo_ref, wp_ref, bp_ref, ps_ref, wo_ref,
             dattn_ref, dmixed_ref, gwo_ref, gbo_ref, gwp_ref, gbp_ref, gps_ref, sc_ref):
        i = pl.program_id(0)

        @pl.when(i == 0)
        def _():
            for ref in (gwo_ref, gbo_ref, gwp_ref, gbp_ref, gps_ref):
                ref[...] = jnp.zeros_like(ref)

        dx1 = dx1_ref[...]
        dx1b = dx1.astype(BF16)
        dcat = _dot_nt(dx1b, wo_ref[...])
        dattn_ref[...] = dcat[:, :ATTN_WIDTH].astype(BF16)
        for g, mixed in enumerate(_pool_mixed(sc_ref, u_ref, halo_ref, i, tm)):
            lanes = slice(LANES * g, LANES * (g + 1))
            mixed_b = mixed.astype(BF16)
            lin = _dot(mixed_b, wp_ref[g]) + bp_ref[:, lanes]
            dpool = dcat[:, ATTN_WIDTH + LANES * g:ATTN_WIDTH + LANES * (g + 1)]
            gps_ref[:, lanes] += jnp.sum(dpool * lin, axis=0, keepdims=True)
            dy = dpool * ps_ref[:, lanes]
            gbp_ref[:, lanes] += jnp.sum(dy, axis=0, keepdims=True)
            dy_b = dy.astype(BF16)
            gwp_ref[g] += _dot_tn(mixed_b, dy_b)
            dmixed_ref[:, lanes] = _dot_nt(dy_b, wp_ref[g])
        gwo_ref[0:ATTN_WIDTH, :] += _dot_tn(a_ref[...], dx1b)
        gwo_ref[ATTN_WIDTH:, :] += _dot_tn(y_ref[...], dx1b)
        gbo_ref[...] += jnp.sum(dx1, axis=0, keepdims=True)

    row = lambda w: pl.BlockSpec((tm, w), lambda i: (i, 0))
    const = lambda shape: pl.BlockSpec(shape, lambda i: (0,) * len(shape))
    return _compute_call(
        body, (dx1, attn, pool_y, u, u, w_pool, b_pool, pool_scale, w_out), grid=(t // tm,),
        in_specs=[row(D_MODEL), row(ATTN_WIDTH), row(POOL_WIDTH), row(POOL_WIDTH), _halo_before(tm, POOL_WIDTH),
                  _resident((4, LANES, LANES)), _resident((1, POOL_WIDTH)), _resident((1, POOL_WIDTH)),
                  _resident((D_MODEL, D_MODEL))],
        out_specs=[row(ATTN_WIDTH), row(POOL_WIDTH), const((D_MODEL, D_MODEL)), const((1, D_MODEL)),
                   const((4, LANES, LANES)), const((1, POOL_WIDTH)), const((1, POOL_WIDTH))],
        out_shape=[jax.ShapeDtypeStruct((t, ATTN_WIDTH), BF16), jax.ShapeDtypeStruct((t, POOL_WIDTH), F32),
                   jax.ShapeDtypeStruct((D_MODEL, D_MODEL), F32), jax.ShapeDtypeStruct((1, D_MODEL), F32),
                   jax.ShapeDtypeStruct((4, LANES, LANES), F32), jax.ShapeDtypeStruct((1, POOL_WIDTH), F32),
                   jax.ShapeDtypeStruct((1, POOL_WIDTH), F32)],
        scratch_shapes=[pltpu.VMEM((tm + POOL_HALO, POOL_WIDTH), F32)],
        semantics=("arbitrary",), name="outproj_bwd", exchange=exchange)


def _attn_bwd_call(q, k, v, dout, sinks, cos_t, sin_t, bias, bias_first, exchange=None):
    t = q.shape[0]
    nb = t // BLOCK
    gb = _blocks_per_step(nb)
    qb = gb * BLOCK
    steps = nb // gb

    def body(sink_ref, q_ref, kc_ref, kp_ref, vc_ref, vp_ref, do_ref, cq_ref, sq_ref, ck_ref, sk_ref, b_ref, bf_ref,
             dq_ref, dk_ref, dv_ref, dsink_ref, newk_ref, newv_ref, carryk_ref, carryv_ref, headk_ref, headv_ref):
        n = pl.program_id(0)
        first = _first_half_mask((BLOCK, LANES))

        @pl.when(n == 0)
        def _():
            dsink_ref[...] = jnp.zeros_like(dsink_ref)

        @pl.when(n < steps)
        def _():
            head_row = lax.broadcasted_iota(jnp.int32, dsink_ref.shape, 0)
            lane2 = lax.broadcasted_iota(jnp.int32, (2 * BLOCK, LANES), 1)
            row_low = lax.broadcasted_iota(jnp.int32, (LANES, BLOCK), 0) < HEAD_DIM
            dsink = jnp.zeros(dsink_ref.shape, F32)
            for g in range(gb):
                r0 = g * BLOCK
                kband = _kv_band(kc_ref, kp_ref, g)
                vband = _kv_band(vc_ref, vp_ref, g)
                kband_t = kband.T
                bias_g = jnp.where(n == 0, bf_ref[...], b_ref[...]) if g == 0 else b_ref[...]
                cq, sq = cq_ref[r0:r0 + BLOCK, :], sq_ref[r0:r0 + BLOCK, :]
                dk_heads, dv_heads = [], []
                for kh in range(2):
                    k2 = _head_in_both_halves(kband, kh)
                    v2 = _head_in_both_halves(vband, kh)
                    kt_head = kband_t[HEAD_DIM * kh:HEAD_DIM * (kh + 1), :]
                    k2_t = jnp.concatenate([kt_head, kt_head], axis=0).astype(BF16)
                    qst = _stack_heads(q_ref, kh, r0)
                    dost = _stack_heads(do_ref, kh, r0)
                    s_t = _dot_nt(k2, qst) + bias_g
                    sink = _sink_row(sink_ref, kh)
                    m = jnp.maximum(jnp.max(s_t, axis=0, keepdims=True), sink)
                    e = jnp.exp(s_t - m)
                    e_sink = jnp.exp(sink - m)
                    inv = 1.0 / (jnp.sum(e, axis=0, keepdims=True) + e_sink)
                    p_t = e * inv
                    dp_t = _dot_nt(v2, dost)
                    delta = jnp.sum(p_t * dp_t, axis=0, keepdims=True)
                    ds_t = (p_t * (dp_t - delta)).astype(BF16)
                    sink_term = e_sink * inv * delta
                    for h in range(4):
                        val = -jnp.sum(sink_term[:, BLOCK * h:BLOCK * (h + 1)])
                        dsink = dsink + jnp.where(head_row == 4 * kh + h, val, 0.0)
                    dq_t = _dot(k2_t, ds_t)
                    for jj in range(2):
                        slab = jnp.where(row_low, dq_t[:, BLOCK * 2 * jj:BLOCK * (2 * jj + 1)],
                                         dq_t[:, BLOCK * (2 * jj + 1):BLOCK * (2 * jj + 2)]).T
                        dq_ref[r0:r0 + BLOCK, LANES * (2 * kh + jj):LANES * (2 * kh + jj + 1)] = (
                            (slab * cq + _rope_partner(slab * sq, first)) * Q_SCALE).astype(BF16)
                    dk2 = _dot(ds_t, qst)
                    dv2 = _dot(p_t.astype(BF16), dost)
                    dk_heads.append(dk2 + pltpu.roll(dk2, HEAD_DIM, 1))
                    dv_heads.append(dv2 + pltpu.roll(dv2, HEAD_DIM, 1))
                bandk = jnp.where(lane2 < HEAD_DIM, dk_heads[0], dk_heads[1])
                bandv = jnp.where(lane2 < HEAD_DIM, dv_heads[0], dv_heads[1])
                if g == 0:
                    headk_ref[...] = bandk[0:BLOCK]
                    headv_ref[...] = bandv[0:BLOCK]
                else:
                    newk_ref[r0 - BLOCK:r0, :] += bandk[0:BLOCK]
                    newv_ref[r0 - BLOCK:r0, :] += bandv[0:BLOCK]
                newk_ref[r0:r0 + BLOCK, :] = bandk[BLOCK:]
                newv_ref[r0:r0 + BLOCK, :] = bandv[BLOCK:]
            dsink_ref[...] += dsink

        @pl.when(n == steps)
        def _():
            headk_ref[...] = jnp.zeros_like(headk_ref)
            headv_ref[...] = jnp.zeros_like(headv_ref)

        @pl.when(n >= 1)
        def _():
            last = qb - BLOCK
            carryk_ref[last:, :] += headk_ref[...]
            carryv_ref[last:, :] += headv_ref[...]
            for g in range(gb):
                rows = slice(g * BLOCK, (g + 1) * BLOCK)
                dk = carryk_ref[rows, :]
                dk_ref[rows, :] = (dk * ck_ref[rows, :] + _rope_partner(dk * sk_ref[rows, :], first)).astype(BF16)
            dv_ref[...] = carryv_ref[...].astype(BF16)

        @pl.when(n < steps)
        def _():
            carryk_ref[...] = newk_ref[...]
            carryv_ref[...] = newv_ref[...]

    cur = lambda w: pl.BlockSpec((qb, w), lambda n: (jnp.minimum(n, steps - 1), 0))
    late = lambda w: pl.BlockSpec((qb, w), lambda n: (jnp.maximum(n - 1, 0), 0))
    prev = lambda w: pl.BlockSpec((BLOCK, w), lambda n: (jnp.maximum(jnp.minimum(n, steps - 1) * gb - 1, 0), 0))
    return _compute_call(
        body, (sinks, q, k, k, v, v, dout, cos_t, sin_t, cos_t, sin_t, bias, bias_first), grid=(steps + 1,),
        in_specs=[pl.BlockSpec(memory_space=pltpu.SMEM), cur(ATTN_WIDTH), cur(KV_WIDTH), prev(KV_WIDTH),
                  cur(KV_WIDTH), prev(KV_WIDTH), cur(ATTN_WIDTH), cur(LANES), cur(LANES), late(LANES), late(LANES),
                  _resident(bias.shape), _resident(bias.shape)],
        out_specs=[cur(ATTN_WIDTH), late(KV_WIDTH), late(KV_WIDTH), pl.BlockSpec((8, LANES), lambda n: (0, 0))],
        out_shape=[jax.ShapeDtypeStruct((t, ATTN_WIDTH), BF16), jax.ShapeDtypeStruct((t, KV_WIDTH), BF16),
                   jax.ShapeDtypeStruct((t, KV_WIDTH), BF16), jax.ShapeDtypeStruct((8, LANES), F32)],
        scratch_shapes=[pltpu.VMEM((qb, LANES), F32), pltpu.VMEM((qb, LANES), F32),
                        pltpu.VMEM((qb, LANES), F32), pltpu.VMEM((qb, LANES), F32),
                        pltpu.VMEM((BLOCK, LANES), F32), pltpu.VMEM((BLOCK, LANES), F32)],
        semantics=("arbitrary",), name="attn_bwd", exchange=exchange)


def _inproj_bwd_call(dq, dk, dv, dmixed, x, h0, dx1, w_in, g_mix, exchange=None):
    t = x.shape[0]
    tm = _token_tile(t, 1024)
    n_tiles = t // tm

    def body(dq_ref, dk_ref, dv_ref, dm_ref, halo_ref, x_ref, h_ref, dx1_ref, w_ref, g_ref,
             dx_ref, gw_ref, gb_ref, gg_ref, sc_ref):
        i = pl.program_id(0)

        @pl.when(i == 0)
        def _():
            for ref in (gw_ref, gb_ref, gg_ref):
                ref[...] = jnp.zeros_like(ref)

        tok = i * tm + lax.broadcasted_iota(jnp.int32, (tm, 1), 0)
        halo = jnp.where(i == n_tiles - 1, 0.0, halo_ref[...])
        du = []
        for g, size in enumerate(POOL_SIZES):
            lanes = slice(LANES * g, LANES * (g + 1))
            dm = dm_ref[:, lanes]
            sc_ref[0:tm, lanes] = dm / jnp.minimum(tok + 1, size).astype(F32)
            sc_ref[tm:, lanes] = halo[:, lanes] / float(size)
            acc = -dm
            for j in range(size):
                acc = acc + sc_ref[pl.ds(j, tm), lanes]
            du.append(acc)
        dz32 = jnp.concatenate([dq_ref[...].astype(F32), dk_ref[...].astype(F32), dv_ref[...].astype(F32)] + du,
                               axis=1)
        dz = dz32.astype(BF16)
        gb_ref[...] += jnp.sum(dz32, axis=0, keepdims=True)
        gw_ref[...] += _dot_tn(dz, h_ref[...])
        dh = _dot(dz, w_ref[...])
        xh, r = _rms_stats(x_ref[...])
        gg_ref[...] += jnp.sum(dh * xh, axis=0, keepdims=True)
        dx_ref[...] = dx1_ref[...] + _rms_bwd(dh, xh, r, g_ref[...])

    row = lambda w: pl.BlockSpec((tm, w), lambda i: (i, 0))
    const = lambda shape: pl.BlockSpec(shape, lambda i: (0,) * len(shape))
    last_halo = t // POOL_HALO - 1
    halo_after = pl.BlockSpec((POOL_HALO, POOL_WIDTH),
                              lambda i: (jnp.minimum((i + 1) * (tm // POOL_HALO), last_halo), 0))
    return _compute_call(
        body, (dq, dk, dv, dmixed, dmixed, x, h0, dx1, w_in, g_mix), grid=(n_tiles,),
        in_specs=[row(ATTN_WIDTH), row(KV_WIDTH), row(KV_WIDTH), row(POOL_WIDTH), halo_after, row(D_MODEL),
                  row(D_MODEL), row(D_MODEL), _resident((IN_WIDTH, D_MODEL)), _resident((1, D_MODEL))],
        out_specs=[row(D_MODEL), const((IN_WIDTH, D_MODEL)), const((1, IN_WIDTH)), const((1, D_MODEL))],
        out_shape=[jax.ShapeDtypeStruct((t, D_MODEL), F32), jax.ShapeDtypeStruct((IN_WIDTH, D_MODEL), F32),
                   jax.ShapeDtypeStruct((1, IN_WIDTH), F32), jax.ShapeDtypeStruct((1, D_MODEL), F32)],
        scratch_shapes=[pltpu.VMEM((tm + POOL_HALO, POOL_WIDTH), F32)],
        semantics=("arbitrary",), name="inproj_bwd", exchange=exchange)


def _reduce_to_owner(grads, place, hosts, wire_dtypes):
    me, c = place
    landed = hosts[0](_sibling_exchange(grads))
    partials = _pair_sum_call(grads, landed, c, wire_dtypes)
    slots = hosts[1](_chip_exchange(partials))
    return _chip_sum_call(partials, slots, me, c)


def _local_step(x, target, small, w_in, w_out, w_ffn, place=None):
    t = x.shape[0]
    on_mesh = place is not None
    cos_t, sin_t = _rope_tables(t)
    w_pool = small["w_pool"].astype(BF16)
    w_ffn = list(w_ffn)

    def gathering(bufs):
        return _gather_exchange(bufs) if on_mesh else None

    (h0,), got = _norm_call(x, small["g_mix"], exchange=gathering([w_in]))
    if on_mesh:
        w_in = got[0].reshape(IN_WIDTH, D_MODEL)
    (q, k, v, u), got = _inproj_call(h0, w_in, small["b_in"], cos_t, sin_t, exchange=gathering([w_out]))
    if on_mesh:
        w_out = got[0].reshape(D_MODEL, D_MODEL)
    bias_t, bias_first_t = _band_biases()
    (attn,), got = _attn_fwd_call(q, k, v, small["sinks"], bias_t, bias_first_t, exchange=gathering(w_ffn[:2]))
    if on_mesh:
        w_ffn[:2] = got
    (x1, h1, pool_y), got = _outproj_call(attn, u, x, w_pool, small["b_pool"], small["pool_scale"], w_out,
                                          small["b_out"], small["g_ffn"], exchange=gathering(w_ffn[2:]))
    if on_mesh:
        w_ffn[2:] = got
    w_gate, w_up, w_down = w_ffn
    gate, up, act, dx2, loss, g_g_final = _ffn_fwd_call(h1, x1, target, w_gate, w_up, w_down, small["g_final"])
    dgate, dup, dx1, dx2b, g_g_ffn = _ffn_bwd_act_call(dx2, gate, up, x1, w_gate, w_up, w_down, small["g_ffn"])
    ffn_grads = list(_ffn_bwd_weights_call(h1, dgate, dup, act, dx2b))
    kept = {}

    def outproj_bwd(exchange=None):
        kept["outproj"], results = _outproj_bwd_call(dx1, attn, pool_y, u, w_pool, small["b_pool"],
                                                     small["pool_scale"], w_out, exchange=exchange)
        return results

    def attn_bwd(exchange=None):
        kept["attn"], results = _attn_bwd_call(q, k, v, kept["outproj"][0], small["sinks"], cos_t, sin_t, bias_t,
                                               bias_first_t, exchange=exchange)
        return results

    def inproj_bwd(exchange=None):
        dq, dk, dv, _ = kept["attn"]
        kept["inproj"], results = _inproj_bwd_call(dq, dk, dv, kept["outproj"][1], x, h0, dx1, w_in, small["g_mix"],
                                                   exchange=exchange)
        return results

    if on_mesh:
        ffn_grads = _reduce_to_owner(ffn_grads, place, (outproj_bwd, attn_bwd), [BF16] * 3)
    else:
        outproj_bwd(), attn_bwd()
    inproj_bwd()
    _, _, g_w_out, g_b_out, g_w_pool, g_b_pool, g_pool_scale = kept["outproj"]
    g_sinks = kept["attn"][3]
    grad_x, g_w_in, g_b_in, g_g_mix = kept["inproj"]
    small_grads = dict(g_mix=g_g_mix, b_in=g_b_in, sinks=g_sinks[:, 0], w_pool=g_w_pool,
                       b_pool=g_b_pool, pool_scale=g_pool_scale, b_out=g_b_out, g_ffn=g_g_ffn, g_final=g_g_final)
    big_grads = dict(w_in=g_w_in, w_out=g_w_out, w_gate=ffn_grads[0], w_up=ffn_grads[1], w_down=ffn_grads[2])
    return loss, grad_x, small_grads, big_grads


def _cast_place_call(shards, me):
    na = len(shards)
    split = 2

    def body(me_ref, *refs):
        for a in range(na):
            refs[na + a][...] = refs[a][...].astype(BF16)

    in_specs = [pl.BlockSpec((s.shape[0] // split, s.shape[1]), lambda i, me_ref: (i, 0)) for s in shards]
    out_specs = [pl.BlockSpec((None, s.shape[0] // split, s.shape[1]), lambda i, me_ref: (me_ref[0], i, 0))
                 for s in shards]
    return pl.pallas_call(
        body,
        grid_spec=pltpu.PrefetchScalarGridSpec(num_scalar_prefetch=1, grid=(split,), in_specs=in_specs,
                                               out_specs=out_specs),
        out_shape=[jax.ShapeDtypeStruct((N_SHARDS,) + s.shape, BF16) for s in shards],
        compiler_params=_params("parallel"), name="cast_place_weights",
    )(me.reshape(1), *shards)


def _pair_sum_call(grads, landed, c, wire_dtypes):
    na = len(grads)
    split = 2

    def body(c_ref, *refs):
        ins, lands, outs = refs[:na], refs[na:2 * na], refs[2 * na:]
        for a in range(na):
            outs[a][...] = (ins[a][...] + lands[a][...]).astype(outs[a].dtype)

    in_specs, land_specs, out_specs, views = [], [], [], []
    for g in grads:
        ns, r, cols = g.shape
        rows = r // 2 // split
        views.append(g.reshape(ns, 2, r // 2, cols))
        in_specs.append(pl.BlockSpec((None, None, rows, cols), lambda s, i, c_ref: (s, c_ref[0], i, 0)))
        land_specs.append(pl.BlockSpec((None, rows, cols), lambda s, i, c_ref: (s, i, 0)))
        out_specs.append(pl.BlockSpec((None, rows, cols), lambda s, i, c_ref: (s, i, 0)))
    return pl.pallas_call(
        body,
        grid_spec=pltpu.PrefetchScalarGridSpec(num_scalar_prefetch=1, grid=(N_SHARDS, split),
                                               in_specs=in_specs + land_specs, out_specs=out_specs),
        out_shape=[jax.ShapeDtypeStruct(l.shape, dt) for l, dt in zip(landed, wire_dtypes)],
        compiler_params=_params("parallel", "parallel"), name="reduce_pair_sum",
    )(c.reshape(1), *views, *landed)


def _chip_sum_call(partials, slots, me, c):
    na = len(slots)
    split = 2

    def body(place_ref, *refs):
        mine, lands, outs = refs[:na], refs[na:2 * na], refs[2 * na:]
        for j in range(N_SHARDS):
            @pl.when(place_ref[0] == j)
            def _():
                for a in range(na):
                    terms = [mine[a][...] if k == j else lands[a][k] for k in range(N_SHARDS)]
                    t0, t1, t2, t3 = (term.astype(F32) for term in terms)
                    outs[a][...] = ((t0 + t1) + t2) + t3

    mine_specs, land_specs, out_specs = [], [], []
    for s in slots:
        ns, h, cols = s.shape
        mine_specs.append(pl.BlockSpec((None, h // split, cols), lambda i, place: (place[0], i, 0)))
        land_specs.append(pl.BlockSpec((ns, h // split, cols), lambda i, place: (0, i, 0)))
        out_specs.append(pl.BlockSpec((h // split, cols), lambda i, place: (place[1] * split + i, 0)))
    return pl.pallas_call(
        body,
        grid_spec=pltpu.PrefetchScalarGridSpec(num_scalar_prefetch=1, grid=(split,),
                                               in_specs=mine_specs + land_specs, out_specs=out_specs),
        out_shape=[jax.ShapeDtypeStruct((2 * s.shape[1], s.shape[2]), F32) for s in slots],
        compiler_params=_params("parallel"), name="reduce_chip_sum",
    )(jnp.stack([me, c]), *partials, *slots)


def _adamw(w, g, m, v):
    m = ADAM_B1 * m + (1.0 - ADAM_B1) * g
    v = ADAM_B2 * v + (1.0 - ADAM_B2) * jnp.square(g)
    m_hat = m / (1.0 - ADAM_B1 ** ADAM_STEP)
    v_hat = v / (1.0 - ADAM_B2 ** ADAM_STEP)
    return -ADAM_LR * (m_hat / (jnp.sqrt(v_hat) + ADAM_EPS) + ADAM_WD * w), m, v


def _adamw_call(ws, gs, ms, vs):
    na = len(ws)
    split = 4

    def body(*refs):
        w_refs, g_refs, m_refs, v_refs = (refs[k * na:(k + 1) * na] for k in range(4))
        d_refs, nm_refs, nv_refs = (refs[(4 + k) * na:(5 + k) * na] for k in range(3))
        for a in range(na):
            d_refs[a][...], nm_refs[a][...], nv_refs[a][...] = _adamw(
                w_refs[a][...], g_refs[a][...], m_refs[a][...], v_refs[a][...])

    specs = [pl.BlockSpec((w.shape[0] // split, w.shape[1]), lambda i: (i, 0)) for w in ws]
    shapes = [jax.ShapeDtypeStruct(w.shape, F32) for w in ws]
    outs = pl.pallas_call(
        body, grid=(split,), in_specs=specs * 4, out_specs=specs * 3, out_shape=shapes * 3,
        compiler_params=_params("parallel"), name="adamw",
    )(*ws, *gs, *ms, *vs)
    return outs[:na], outs[na:2 * na], outs[2 * na:]


def _small_rows():
    first, row = {}, 0
    for name, (r, c) in SMALL_VIEWS.items():
        first[name] = row
        row += r * (-(-c // LANES))
    return first, row, -(-(row + 1) // 64) * 64


def _pack_small(arrays, loss):
    _, used, rows = _small_rows()
    parts = []
    for name, (r, c) in SMALL_VIEWS.items():
        a = arrays[name].reshape(r, c)
        parts.append(jnp.pad(a, ((0, 0), (0, LANES - c))) if c < LANES else a.reshape(r * c // LANES, LANES))
    parts += [loss, jnp.zeros((rows - used - 1, LANES), F32)]
    return jnp.concatenate(parts, axis=0)


def _adamw_small_call(ws, packed_grads, ms, vs):
    names = list(SMALL_VIEWS)
    first, _, _ = _small_rows()
    n = len(names)

    def body(g_ref, *refs):
        w_refs, m_refs, v_refs = refs[:n], refs[n:2 * n], refs[2 * n:3 * n]
        outs = refs[3 * n:]
        for a, name in enumerate(names):
            r, c = SMALL_VIEWS[name]
            r0 = first[name]
            if c == LANES:
                g = g_ref[r0:r0 + r, :]
            elif c < LANES:
                g = g_ref[r0:r0 + 1, 0:c]
            else:
                g = jnp.concatenate([g_ref[r0 + j:r0 + j + 1, :] for j in range(c // LANES)], axis=1)
            delta, m, v = _adamw(w_refs[a][...], g, m_refs[a][...], v_refs[a][...])
            for k, val in enumerate((g, delta, m, v)):
                outs[k * n + a][...] = val

    shapes = [jax.ShapeDtypeStruct(SMALL_VIEWS[name], F32) for name in names]
    outs = pl.pallas_call(
        body, out_shape=shapes * 4, compiler_params=pltpu.CompilerParams(vmem_limit_bytes=VMEM_LIMIT),
        name="adamw_small",
    )(packed_grads, *[ws[k] for k in names], *[ms[k] for k in names], *[vs[k] for k in names])
    return [dict(zip(names, outs[k * n:(k + 1) * n])) for k in range(4)]


def kernel(x, g_mix, w_in, b_in, sinks, w_pool, b_pool, pool_scale, w_out, b_out, g_ffn, w_gate, w_up, w_down, g_final, loss_target, m_g_mix, m_w_in, m_b_in, m_sinks, m_w_pool, m_b_pool, m_pool_scale, m_w_out, m_b_out, m_g_ffn, m_w_gate, m_w_up, m_w_down, m_g_final, v_g_mix, v_w_in, v_b_in, v_sinks, v_w_pool, v_b_pool, v_pool_scale, v_w_out, v_b_out, v_g_ffn, v_w_gate, v_w_up, v_w_down, v_g_final):
    given = dict(locals())
    big_names = ("w_in", "w_out", "w_gate", "w_up", "w_down")
    out_shapes = {n: given[n].shape for n in SMALL_NAMES + big_names}
    c = lax.axis_index("c")

    me = 2 * lax.axis_index("x") + lax.axis_index("y")
    transposed = ("w_in", "w_gate", "w_up")

    def shard_view(a, name):
        return a[0].T if name in transposed else a[0]

    placed = _cast_place_call([shard_view(given[n], n) for n in big_names], me)

    small = dict(g_mix=g_mix, b_in=b_in, sinks=sinks, w_pool=w_pool[0], b_pool=b_pool.reshape(1, POOL_WIDTH),
                 pool_scale=pool_scale.reshape(1, POOL_WIDTH), b_out=b_out, g_ffn=g_ffn,
                 g_final=g_final.reshape(1, D_MODEL))
    loss, grad_x, small_grads, big_grads = _local_step(
        x[0], loss_target[0], small, placed[0], placed[1], placed[2:], place=(me, c))

    packed = _pack_small(small_grads, loss)
    grads = [
        big_grads["w_in"].reshape(N_SHARDS, IN_WIDTH // N_SHARDS, D_MODEL),
        big_grads["w_out"].reshape(N_SHARDS, D_MODEL // N_SHARDS, D_MODEL),
        jnp.broadcast_to(packed[None], (N_SHARDS,) + packed.shape),
    ]
    alone = [functools.partial(_exchange_call, name=name) for name in
             ("reduce_sibling_exchange", "reduce_chip_exchange")]
    h_in, h_out, h_small = _reduce_to_owner(grads, (me, c), alone, [BF16, BF16, F32])
    totals = _exchange_call(_sibling_allgather(
        [h_in, h_out, big_grads["w_gate"], big_grads["w_up"], big_grads["w_down"], h_small]), "reduce_sibling_allgather")
    loss = totals[-1][_small_rows()[1], 0]

    def owned(prefix):
        return [shard_view(given[prefix + n], n) for n in big_names]

    def owned_small(prefix):
        return {n: given[prefix + n].reshape(SMALL_VIEWS[n]) for n in SMALL_NAMES}

    big = (totals[:-1],) + _adamw_call(owned(""), totals[:-1], owned("m_"), owned("v_"))
    small_groups = _adamw_small_call(owned_small(""), totals[-1], owned_small("m_"), owned_small("v_"))

    order = ("g_mix", "w_in", "b_in", "sinks", "w_pool", "b_pool", "pool_scale", "w_out", "b_out", "g_ffn",
             "w_gate", "w_up", "w_down", "g_final")
    results = [loss, grad_x.reshape(x.shape)]
    for group, small_group in zip(big, small_groups):
        named = {n: (a.T if n in transposed else a) for n, a in zip(big_names, group)}
        named.update(small_group)
        results += [named[n].reshape(out_shapes[n]) for n in order]
    return tuple(results)
```

```python
import functools
from typing import Any, Callable, Mapping, NamedTuple, Sequence

import jax
import jax.numpy as jnp
from jax import lax
from jax.experimental import pallas as pl
from jax.experimental.pallas import tpu as pltpu

F32 = jnp.float32
BF16 = jnp.bfloat16

D_MODEL = 1024
ATTN_WIDTH = 512
KV_WIDTH = 128
POOL_WIDTH = 512
IN_WIDTH = ATTN_WIDTH + 2 * KV_WIDTH + POOL_WIDTH
HEAD_DIM = 64
N_Q_HEADS = 8
BLOCK = 128
POOL_SIZES = (2, 4, 8, 16)
POOL_HALO = 16
ROPE_THETA = 10000.0
RMS_EPS = 1e-5
Q_SCALE = HEAD_DIM ** -0.5

ADAM_LR = 0.001
ADAM_B1 = 0.9
ADAM_B2 = 0.999
ADAM_EPS = 1e-08
ADAM_WD = 0.01
ADAM_STEP = 10

N_SHARDS = 4
LANES = 128
VMEM_LIMIT = 56 * 1024 * 1024
FFN_TOKEN_TILE = 256

MESH = pl.DeviceIdType.MESH
HBM_SPEC = pl.BlockSpec(memory_space=pltpu.HBM)

SMALL_VIEWS = dict(w_pool=(4 * LANES, LANES), b_pool=(4, LANES), pool_scale=(4, LANES), g_mix=(1, D_MODEL),
                   b_in=(1, IN_WIDTH), b_out=(1, D_MODEL), g_ffn=(1, D_MODEL), g_final=(1, D_MODEL),
                   sinks=(1, N_Q_HEADS))
SMALL_NAMES = tuple(SMALL_VIEWS)


def _dot(a, b):
    return jnp.dot(a, b, preferred_element_type=F32)


def _dot_nt(a, b):
    return lax.dot_general(a, b, (((1,), (1,)), ((), ())), preferred_element_type=F32)


def _dot_tn(a, b):
    return lax.dot_general(a, b, (((0,), (0,)), ((), ())), preferred_element_type=F32)


def _params(*semantics):
    return pltpu.CompilerParams(dimension_semantics=semantics, vmem_limit_bytes=VMEM_LIMIT)


def _resident(shape):
    return pl.BlockSpec(shape, lambda *_: (0,) * len(shape), pipeline_mode=pl.Buffered(1))


def _token_tile(t, largest=512):
    for tm in (1024, 512, 256, 128):
        if tm <= largest and t % tm == 0:
            return tm
    raise ValueError(f"sequence length {t} is not a multiple of 128")


def _rms_stats(xf):
    r = lax.rsqrt(jnp.mean(xf * xf, axis=-1, keepdims=True) + RMS_EPS)
    return xf * r, r


def _rms_bwd(dy, xh, r, g):
    dxh = dy * g
    return r * (dxh - xh * jnp.mean(dxh * xh, axis=-1, keepdims=True))


def _rope_partner(t, first):
    return jnp.where(first, pltpu.roll(t, LANES - HEAD_DIM // 2, 1), pltpu.roll(t, HEAD_DIM // 2, 1))


def _first_half_mask(shape):
    lane = lax.broadcasted_iota(jnp.int32, shape, 1)
    return (lane % HEAD_DIM) < (HEAD_DIM // 2)


def _rope_tables(t):
    inv_freq = 1.0 / (ROPE_THETA ** (jnp.arange(0, HEAD_DIM, 2, dtype=F32) / HEAD_DIM))
    ang = jnp.arange(t, dtype=F32)[:, None] * inv_freq[None, :]
    cos, sin = jnp.cos(ang), jnp.sin(ang)
    cos_t = jnp.tile(jnp.concatenate([cos, cos], axis=-1), (1, LANES // HEAD_DIM))
    sin_t = jnp.tile(jnp.concatenate([-sin, sin], axis=-1), (1, LANES // HEAD_DIM))
    return cos_t, sin_t


class _Exchange(NamedTuple):
    inputs: Sequence[Any]
    out_shapes: Sequence[Any]
    aliases: Mapping[int, int]
    sems: Sequence[Any]
    start: Callable[..., None]
    finish: Callable[..., None]


def _compute_call(body, operands, *, grid, in_specs, out_specs, out_shape, scratch_shapes=(), semantics, name,
                  exchange=None):
    n_in, n_out, n_scr = len(in_specs), len(out_specs), len(scratch_shapes)
    if exchange is None:
        return pl.pallas_call(body, grid=grid, in_specs=in_specs, out_specs=out_specs, out_shape=out_shape,
                              scratch_shapes=scratch_shapes, compiler_params=_params(*semantics),
                              name=name)(*operands), []
    k_in, k_out = len(exchange.inputs), len(exchange.out_shapes)

    def hosting(*refs):
        ins, rest = refs[:n_in], refs[n_in:]
        ex_ins, rest = rest[:k_in], rest[k_in:]
        outs, rest = rest[:n_out], rest[n_out:]
        ex_outs, rest = rest[:k_out], rest[k_out:]
        scratch, sems = rest[:n_scr], rest[n_scr:]
        steps = [pl.program_id(d) for d in range(len(grid))]
        is_first = functools.reduce(jnp.logical_and, [s == 0 for s in steps])
        is_last = functools.reduce(jnp.logical_and, [s == g - 1 for s, g in zip(steps, grid)])

        @pl.when(is_first)
        def _():
            exchange.start(ex_ins, ex_outs, sems)

        body(*ins, *outs, *scratch)

        @pl.when(is_last)
        def _():
            exchange.finish(ex_ins, ex_outs, sems)

    results = pl.pallas_call(
        hosting, grid=grid,
        in_specs=list(in_specs) + [HBM_SPEC] * k_in, out_specs=list(out_specs) + [HBM_SPEC] * k_out,
        out_shape=list(out_shape) + list(exchange.out_shapes),
        scratch_shapes=list(scratch_shapes) + list(exchange.sems),
        input_output_aliases={n_in + i: n_out + j for i, j in exchange.aliases.items()},
        compiler_params=_params(*semantics), name=name,
    )(*operands, *exchange.inputs)
    return results[:n_out], results[n_out:]


def _exchange_call(exchange, name):
    n_in, n_out = len(exchange.inputs), len(exchange.out_shapes)

    def body(*refs):
        ins, outs, sems = refs[:n_in], refs[n_in:n_in + n_out], refs[n_in + n_out:]
        exchange.start(ins, outs, sems)
        exchange.finish(ins, outs, sems)

    return pl.pallas_call(
        body, in_specs=[HBM_SPEC] * n_in, out_specs=[HBM_SPEC] * n_out, out_shape=list(exchange.out_shapes),
        input_output_aliases=dict(exchange.aliases), scratch_shapes=list(exchange.sems),
        compiler_params=pltpu.CompilerParams(has_side_effects=True), name=name,
    )(*exchange.inputs)


def _mesh_place():
    x, y, c = lax.axis_index("x"), lax.axis_index("y"), lax.axis_index("c")
    other_chips = [(1 - x, y), (x, 1 - y), (1 - x, 1 - y)]
    return x, y, c, other_chips


def _half(ref, c, rows):
    return ref.at[pl.ds(pl.multiple_of(c * rows, 16), rows), :]


def _remote(src, dst, send_sems, recv_sems, idx, to):
    return functools.partial(pltpu.make_async_remote_copy, src_ref=src, dst_ref=dst, send_sem=send_sems.at[idx],
                             recv_sem=recv_sems.at[idx], device_id=to, device_id_type=MESH)


def _gather_exchange(bufs):
    na = len(bufs)

    def copies(outs, sems):
        send_sems, recv_sems = sems
        x, y, c, other_chips = _mesh_place()
        me, sibling = 2 * x + y, (x, y, 1 - c)
        ici_out, ici_in, fwd_out, fwd_in = [], [], [], []
        for j, (cx, cy) in enumerate(other_chips):
            for a in range(na):
                rows = outs[a].shape[1] // 2
                mine = _half(outs[a].at[me], c, rows)
                landed = _half(outs[a].at[2 * cx + cy], c, rows)
                passed = _half(outs[a].at[2 * cx + cy], 1 - c, rows)
                ici_out.append(_remote(mine, mine, send_sems, recv_sems, j * na + a, (cx, cy, c)))
                ici_in.append(_remote(landed, landed, send_sems, recv_sems, j * na + a, (cx, cy, c)))
                fwd_out.append(_remote(landed, landed, send_sems, recv_sems, (3 + j) * na + a, sibling))
                fwd_in.append(_remote(passed, passed, send_sems, recv_sems, (3 + j) * na + a, sibling))
        return ici_out, ici_in, fwd_out, fwd_in

    def start(ins, outs, sems):
        for cp in copies(outs, sems)[0]:
            cp().start()

    def finish(ins, outs, sems):
        ici_out, ici_in, fwd_out, fwd_in = copies(outs, sems)
        for arrived, forward in zip(ici_in, fwd_out):
            arrived().wait_recv()
            forward().start()
        for cp in fwd_in:
            cp().wait_recv()
        for cp in ici_out + fwd_out:
            cp().wait_send()

    return _Exchange(inputs=bufs, out_shapes=[jax.ShapeDtypeStruct(b.shape, b.dtype) for b in bufs],
                     aliases={a: a for a in range(na)},
                     sems=[pltpu.SemaphoreType.DMA((6 * na,)), pltpu.SemaphoreType.DMA((6 * na,))],
                     start=start, finish=finish)


def _sibling_exchange(grads):
    na = len(grads)

    def copies(ins, outs, sems):
        x, y, c, _ = _mesh_place()
        out = []
        for a in range(na):
            rows = ins[a].shape[1] // 2
            src = ins[a].at[:, pl.ds(pl.multiple_of((1 - c) * rows, 8), rows), :]
            out.append(_remote(src, outs[a], sems[0], sems[1], a, (x, y, 1 - c)))
        return out

    def start(ins, outs, sems):
        for cp in copies(ins, outs, sems):
            cp().start()

    def finish(ins, outs, sems):
        for cp in copies(ins, outs, sems):
            cp().wait_recv()
            cp().wait_send()

    return _Exchange(inputs=grads,
                     out_shapes=[jax.ShapeDtypeStruct((g.shape[0], g.shape[1] // 2, g.shape[2]), g.dtype)
                                 for g in grads],
                     aliases={}, sems=[pltpu.SemaphoreType.DMA((na,)), pltpu.SemaphoreType.DMA((na,))],
                     start=start, finish=finish)


def _chip_exchange(partials):
    na = len(partials)

    def copies(ins, outs, sems):
        x, y, c, other_chips = _mesh_place()
        me = 2 * x + y
        sends, arrivals = [], []
        for j, (cx, cy) in enumerate(other_chips):
            for a in range(na):
                slot = outs[a].at[2 * cx + cy]
                sends.append(_remote(ins[a].at[2 * cx + cy], outs[a].at[me], sems[0], sems[1], j * na + a, (cx, cy, c)))
                arrivals.append(_remote(slot, slot, sems[0], sems[1], j * na + a, (cx, cy, c)))
        return sends, arrivals

    def start(ins, outs, sems):
        for cp in copies(ins, outs, sems)[0]:
            cp().start()

    def finish(ins, outs, sems):
        sends, arrivals = copies(ins, outs, sems)
        for cp in arrivals:
            cp().wait_recv()
        for cp in sends:
            cp().wait_send()

    return _Exchange(inputs=partials, out_shapes=[jax.ShapeDtypeStruct(p.shape, p.dtype) for p in partials],
                     aliases={}, sems=[pltpu.SemaphoreType.DMA((3 * na,)), pltpu.SemaphoreType.DMA((3 * na,))],
                     start=start, finish=finish)


def _sibling_allgather(bufs):
    na = len(bufs)

    def piece(outs, sems, a, half):
        x, y, c, _ = _mesh_place()
        rows = outs[a].shape[0] // 2
        rows_ref = outs[a].at[pl.ds(pl.multiple_of(half * rows, 8), rows), :]
        return _remote(rows_ref, rows_ref, sems[0], sems[1], a, (x, y, 1 - c))

    def start(ins, outs, sems):
        c = lax.axis_index("c")
        for a in range(na):
            piece(outs, sems, a, c)().start()

    def finish(ins, outs, sems):
        c = lax.axis_index("c")
        for a in range(na):
            piece(outs, sems, a, 1 - c)().wait_recv()
        for a in range(na):
            piece(outs, sems, a, c)().wait_send()

    return _Exchange(inputs=bufs, out_shapes=[jax.ShapeDtypeStruct(b.shape, b.dtype) for b in bufs],
                     aliases={a: a for a in range(na)},
                     sems=[pltpu.SemaphoreType.DMA((na,)), pltpu.SemaphoreType.DMA((na,))],
                     start=start, finish=finish)


def _norm_call(x, g_mix, exchange=None):
    t = x.shape[0]
    tm = _token_tile(t)

    def body(x_ref, g_ref, h_ref):
        xh, _ = _rms_stats(x_ref[...])
        h_ref[...] = (xh * g_ref[...]).astype(BF16)

    row = pl.BlockSpec((tm, D_MODEL), lambda i: (i, 0))
    return _compute_call(body, (x, g_mix), grid=(t // tm,), in_specs=[row, _resident((1, D_MODEL))],
                         out_specs=[row], out_shape=[jax.ShapeDtypeStruct((t, D_MODEL), BF16)],
                         semantics=("parallel",), name="norm_fwd", exchange=exchange)


def _inproj_call(h0, w_in, b_in, cos_t, sin_t, exchange=None):
    t = h0.shape[0]
    tm = _token_tile(t, 1024)

    def body(h_ref, w_ref, b_ref, c_ref, s_ref, q_ref, k_ref, v_ref, u_ref):
        z = _dot_nt(h_ref[...], w_ref[...]) + b_ref[...]
        cos, sin = c_ref[...], s_ref[...]
        first = _first_half_mask((tm, LANES))
        for j in range(ATTN_WIDTH // LANES):
            zj = z[:, LANES * j:LANES * (j + 1)]
            q_ref[:, LANES * j:LANES * (j + 1)] = ((zj * cos + _rope_partner(zj, first) * sin) * Q_SCALE).astype(BF16)
        zk = z[:, ATTN_WIDTH:ATTN_WIDTH + KV_WIDTH]
        k_ref[...] = (zk * cos + _rope_partner(zk, first) * sin).astype(BF16)
        v_ref[...] = z[:, ATTN_WIDTH + KV_WIDTH:ATTN_WIDTH + 2 * KV_WIDTH].astype(BF16)
        u_ref[...] = z[:, ATTN_WIDTH + 2 * KV_WIDTH:]

    row = lambda w: pl.BlockSpec((tm, w), lambda i: (i, 0))
    return _compute_call(
        body, (h0, w_in, b_in, cos_t, sin_t), grid=(t // tm,),
        in_specs=[row(D_MODEL), _resident((IN_WIDTH, D_MODEL)), _resident((1, IN_WIDTH)), row(LANES), row(LANES)],
        out_specs=[row(ATTN_WIDTH), row(KV_WIDTH), row(KV_WIDTH), row(POOL_WIDTH)],
        out_shape=[jax.ShapeDtypeStruct((t, ATTN_WIDTH), BF16),
                   jax.ShapeDtypeStruct((t, KV_WIDTH), BF16), jax.ShapeDtypeStruct((t, KV_WIDTH), BF16),
                   jax.ShapeDtypeStruct((t, POOL_WIDTH), F32)],
        semantics=("parallel",), name="inproj_fwd", exchange=exchange)


def _head_in_both_halves(band, kv_head):
    lane = lax.broadcasted_iota(jnp.int32, band.shape, 1)
    own = (lane < HEAD_DIM) if kv_head == 0 else (lane >= HEAD_DIM)
    return jnp.where(own, band, pltpu.roll(band, HEAD_DIM, 1)).astype(BF16)


def _stack_heads(ref, kv_head, row0):
    low = lax.broadcasted_iota(jnp.int32, (BLOCK, LANES), 1) < HEAD_DIM
    parts = []
    for jj in range(2):
        slab = ref[row0:row0 + BLOCK, LANES * (2 * kv_head + jj):LANES * (2 * kv_head + jj + 1)]
        zero = jnp.zeros_like(slab)
        parts += [jnp.where(low, slab, zero), jnp.where(low, zero, slab)]
    return jnp.concatenate(parts, axis=0)


def _band_biases():
    c = lax.broadcasted_iota(jnp.int32, (2 * BLOCK, 4 * BLOCK), 0)
    i = lax.broadcasted_iota(jnp.int32, (2 * BLOCK, 4 * BLOCK), 1) % BLOCK
    vis = (c > i) & (c <= i + BLOCK)
    return jnp.where(vis, 0.0, -jnp.inf).astype(F32), jnp.where(vis & (c >= BLOCK), 0.0, -jnp.inf).astype(F32)


def _blocks_per_step(nb):
    for g in (4, 2, 1):
        if nb % g == 0:
            return g


def _sink_row(sink_ref, kv_head):
    return jnp.concatenate([jnp.full((1, BLOCK), sink_ref[0, 4 * kv_head + g], F32) for g in range(4)], axis=1)


def _kv_band(cur_ref, prev_ref, g):
    r0 = g * BLOCK
    prev = prev_ref[...] if g == 0 else cur_ref[r0 - BLOCK:r0, :]
    return jnp.concatenate([prev, cur_ref[r0:r0 + BLOCK, :]], axis=0).astype(F32)


def _attn_fwd_call(q, k, v, sinks, bias, bias_first, exchange=None):
    t = q.shape[0]
    nb = t // BLOCK
    gb = _blocks_per_step(nb)
    qb = gb * BLOCK

    def body(sink_ref, q_ref, kc_ref, kp_ref, vc_ref, vp_ref, b_ref, bf_ref, o_ref):
        n = pl.program_id(0)
        row_low = lax.broadcasted_iota(jnp.int32, (LANES, BLOCK), 0) < HEAD_DIM
        for g in range(gb):
            r0 = g * BLOCK
            kband = _kv_band(kc_ref, kp_ref, g)
            vband_t = _kv_band(vc_ref, vp_ref, g).T
            bias_g = jnp.where(n == 0, bf_ref[...], b_ref[...]) if g == 0 else b_ref[...]
            for kh in range(2):
                k2 = _head_in_both_halves(kband, kh)
                vt_head = vband_t[HEAD_DIM * kh:HEAD_DIM * (kh + 1), :]
                v2_t = jnp.concatenate([vt_head, vt_head], axis=0).astype(BF16)
                s_t = _dot_nt(k2, _stack_heads(q_ref, kh, r0)) + bias_g
                sink = _sink_row(sink_ref, kh)
                m = jnp.maximum(jnp.max(s_t, axis=0, keepdims=True), sink)
                e = jnp.exp(s_t - m)
                inv = 1.0 / (jnp.sum(e, axis=0, keepdims=True) + jnp.exp(sink - m))
                o_t = _dot(v2_t, (e * inv).astype(BF16))
                for jj in range(2):
                    slab = jnp.where(row_low, o_t[:, BLOCK * 2 * jj:BLOCK * (2 * jj + 1)],
                                     o_t[:, BLOCK * (2 * jj + 1):BLOCK * (2 * jj + 2)]).T
                    o_ref[r0:r0 + BLOCK, LANES * (2 * kh + jj):LANES * (2 * kh + jj + 1)] = slab.astype(BF16)

    cur = lambda w: pl.BlockSpec((qb, w), lambda n: (n, 0))
    prev = lambda w: pl.BlockSpec((BLOCK, w), lambda n: (jnp.maximum(n * gb - 1, 0), 0))
    return _compute_call(
        body, (sinks, q, k, k, v, v, bias, bias_first), grid=(nb // gb,),
        in_specs=[pl.BlockSpec(memory_space=pltpu.SMEM), cur(ATTN_WIDTH), cur(KV_WIDTH), prev(KV_WIDTH),
                  cur(KV_WIDTH), prev(KV_WIDTH), _resident(bias.shape), _resident(bias.shape)],
        out_specs=[cur(ATTN_WIDTH)],
        out_shape=[jax.ShapeDtypeStruct((t, ATTN_WIDTH), BF16)],
        semantics=("parallel",), name="attn_fwd", exchange=exchange)


def _pool_mixed(sc_ref, u_ref, halo_ref, i, tm):
    sc_ref[0:POOL_HALO, :] = jnp.where(i == 0, 0.0, halo_ref[...])
    sc_ref[POOL_HALO:, :] = u_ref[...]
    tok = i * tm + lax.broadcasted_iota(jnp.int32, (tm, 1), 0)
    mixed = []
    for g, size in enumerate(POOL_SIZES):
        lanes = slice(LANES * g, LANES * (g + 1))
        cur = sc_ref[pl.ds(POOL_HALO, tm), lanes]
        acc = cur
        for j in range(1, size):
            acc = acc + sc_ref[pl.ds(POOL_HALO - j, tm), lanes]
        count = jnp.minimum(tok + 1, size).astype(F32)
        mixed.append(acc / count - cur)
    return mixed


def _halo_before(tm, width):
    return pl.BlockSpec((POOL_HALO, width), lambda i: (jnp.maximum(i * (tm // POOL_HALO) - 1, 0), 0))


def _outproj_call(attn, u, x, w_pool, b_pool, pool_scale, w_out, b_out, g_ffn, exchange=None):
    t = x.shape[0]
    tm = _token_tile(t)

    def body(a_ref, u_ref, halo_ref, x_ref, wp_ref, bp_ref, ps_ref, wo_ref, bo_ref, g_ref,
             x1_ref, h1_ref, sc_ref, y_ref):
        i = pl.program_id(0)
        for g, mixed in enumerate(_pool_mixed(sc_ref, u_ref, halo_ref, i, tm)):
            lanes = slice(LANES * g, LANES * (g + 1))
            lin = _dot(mixed.astype(BF16), wp_ref[g]) + bp_ref[:, lanes]
            y_ref[:, lanes] = (lin * ps_ref[:, lanes]).astype(BF16)
        x1 = (x_ref[...] + _dot(a_ref[...], wo_ref[0:ATTN_WIDTH, :]) + _dot(y_ref[...], wo_ref[ATTN_WIDTH:, :])
              + bo_ref[...])
        x1_ref[...] = x1
        xh, _ = _rms_stats(x1)
        h1_ref[...] = (xh * g_ref[...]).astype(BF16)

    row = lambda w: pl.BlockSpec((tm, w), lambda i: (i, 0))
    return _compute_call(
        body, (attn, u, u, x, w_pool, b_pool, pool_scale, w_out, b_out, g_ffn), grid=(t // tm,),
        in_specs=[row(ATTN_WIDTH), row(POOL_WIDTH), _halo_before(tm, POOL_WIDTH), row(D_MODEL),
                  _resident((4, LANES, LANES)), _resident((1, POOL_WIDTH)), _resident((1, POOL_WIDTH)),
                  _resident((D_MODEL, D_MODEL)), _resident((1, D_MODEL)), _resident((1, D_MODEL))],
        out_specs=[row(D_MODEL), row(D_MODEL)],
        out_shape=[jax.ShapeDtypeStruct((t, D_MODEL), F32), jax.ShapeDtypeStruct((t, D_MODEL), BF16)],
        scratch_shapes=[pltpu.VMEM((tm + POOL_HALO, POOL_WIDTH), F32), pltpu.VMEM((tm, POOL_WIDTH), BF16)],
        semantics=("parallel",), name="outproj_fwd", exchange=exchange)


def _ffn_call(h1, x1, target, w_gate, w_up, w_down, g_ffn, g_final):
    t = x1.shape[0]
    tm = _token_tile(t, FFN_TOKEN_TILE)
    ns, fs, _ = w_gate.shape

    def body(h_ref, x1_ref, tgt_ref, wg_ref, wu_ref, wd_ref, g_ref, gf_ref,
             act_ref, dgate_ref, dup_ref, dx1_ref, dx2b_ref, loss_ref, dgf_ref, dg_ref, gate_ref, up_ref):
        @pl.when(pl.program_id(0) == 0)
        def _():
            loss_ref[...] = jnp.zeros_like(loss_ref)
            dgf_ref[...] = jnp.zeros_like(dgf_ref)
            dg_ref[...] = jnp.zeros_like(dg_ref)

        h = h_ref[...]
        x1 = x1_ref[...]
        x2 = x1
        for s in range(ns):
            gate = _dot_nt(h, wg_ref[s])
            up = _dot_nt(h, wu_ref[s])
            act = (gate * jax.nn.sigmoid(gate) * up).astype(BF16)
            gate_ref[s] = gate
            up_ref[s] = up
            act_ref[s] = act
            x2 = x2 + _dot(act, wd_ref[s])
        xh2, r2 = _rms_stats(x2)
        gf = gf_ref[...]
        err = xh2 * gf - tgt_ref[...]
        loss_ref[...] += 0.5 * jnp.sum(err * err) / D_MODEL
        dy = err / D_MODEL
        dgf_ref[...] += jnp.sum(dy * xh2, axis=0, keepdims=True)
        dx2 = _rms_bwd(dy, xh2, r2, gf)
        dx2b = dx2.astype(BF16)
        dx2b_ref[...] = dx2b
        dh = None
        for s in range(ns):
            dact = _dot_nt(dx2b, wd_ref[s])
            gate = gate_ref[s]
            sg = jax.nn.sigmoid(gate)
            dup = (dact * (gate * sg)).astype(BF16)
            dgate = (dact * up_ref[s] * (sg * (1.0 + gate * (1.0 - sg)))).astype(BF16)
            dgate_ref[s] = dgate
            dup_ref[s] = dup
            part = _dot(dgate, wg_ref[s]) + _dot(dup, wu_ref[s])
            dh = part if dh is None else dh + part
        xh1, r1 = _rms_stats(x1)
        dg_ref[...] += jnp.sum(dh * xh1, axis=0, keepdims=True)
        dx1_ref[...] = dx2 + _rms_bwd(dh, xh1, r1, g_ref[...])

    row = lambda w: pl.BlockSpec((tm, w), lambda i: (i, 0))
    shard_act = pl.BlockSpec((ns, tm, fs), lambda i: (0, i, 0))
    act_shape = jax.ShapeDtypeStruct((ns, t, fs), BF16)
    vec = pl.BlockSpec((1, D_MODEL), lambda i: (0, 0))
    return pl.pallas_call(
        body, grid=(t // tm,),
        in_specs=[row(D_MODEL), row(D_MODEL), row(D_MODEL), _resident((ns, fs, D_MODEL)),
                  _resident((ns, fs, D_MODEL)), _resident((ns, fs, D_MODEL)), _resident((1, D_MODEL)),
                  _resident((1, D_MODEL))],
        out_specs=[shard_act, shard_act, shard_act, row(D_MODEL), row(D_MODEL),
                   pl.BlockSpec((1, LANES), lambda i: (0, 0)), vec, vec],
        out_shape=[act_shape, act_shape, act_shape, jax.ShapeDtypeStruct((t, D_MODEL), F32),
                   jax.ShapeDtypeStruct((t, D_MODEL), BF16), jax.ShapeDtypeStruct((1, LANES), F32),
                   jax.ShapeDtypeStruct((1, D_MODEL), F32), jax.ShapeDtypeStruct((1, D_MODEL), F32)],
        scratch_shapes=[pltpu.VMEM((ns, tm, fs), F32), pltpu.VMEM((ns, tm, fs), F32)],
        compiler_params=_params("arbitrary"), name="ffn_fwd_bwd",
    )(h1, x1, target, w_gate, w_up, w_down, g_ffn, g_final)


def _ffn_bwd_weights_call(h1, dgate, dup, act, dx2):
    t = h1.shape[0]
    tm = _token_tile(t, 1024)
    ns, _, fs = dgate.shape

    def body(h_ref, dgate_ref, dup_ref, act_ref, dx2_ref, gg_ref, gu_ref, gd_ref):
        @pl.when(pl.program_id(1) == 0)
        def _():
            gg_ref[...] = jnp.zeros_like(gg_ref)
            gu_ref[...] = jnp.zeros_like(gu_ref)
            gd_ref[...] = jnp.zeros_like(gd_ref)

        h = h_ref[...]
        gg_ref[...] += _dot_tn(dgate_ref[...], h)
        gu_ref[...] += _dot_tn(dup_ref[...], h)
        gd_ref[...] += _dot_tn(act_ref[...], dx2_ref[...])

    row = lambda w: pl.BlockSpec((tm, w), lambda s, i: (i, 0))
    shard_act = pl.BlockSpec((None, tm, fs), lambda s, i: (s, i, 0))
    return pl.pallas_call(
        body, grid=(ns, t // tm),
        in_specs=[row(D_MODEL), shard_act, shard_act, shard_act, row(D_MODEL)],
        out_specs=[pl.BlockSpec((None, fs, D_MODEL), lambda s, i: (s, 0, 0))] * 3,
        out_shape=[jax.ShapeDtypeStruct((ns, fs, D_MODEL), F32)] * 3,
        compiler_params=_params("parallel", "arbitrary"), name="ffn_bwd_weights",
    )(h1, dgate, dup, act, dx2)


def _outproj_bwd_call(dx1, attn, u, w_pool, b_pool, pool_scale, w_out, exchange=None):
    t = dx1.shape[0]
    tm = _token_tile(t, 1024)

    def body(dx1_ref, a_ref, u_ref, halo_ref, wp_ref, bp_ref, ps_ref, wo_ref,
             dattn_ref, dmixed_ref, gwo_ref, gbo_ref, gwp_ref, gbp_ref, gps_ref, sc_ref, y_ref):
        i = pl.program_id(0)

        @pl.when(i == 0)
        def _():
            for ref in (gwo_ref, gbo_ref, gwp_ref, gbp_ref, gps_ref):
                ref[...] = jnp.zeros_like(ref)

        dx1 = dx1_ref[...]
        dx1b = dx1.astype(BF16)
        dcat = _dot_nt(dx1b, wo_ref[...])
        dattn_ref[...] = dcat[:, :ATTN_WIDTH].astype(BF16)
        for g, mixed in enumerate(_pool_mixed(sc_ref, u_ref, halo_ref, i, tm)):
            lanes = slice(LANES * g, LANES * (g + 1))
            mixed_b = mixed.astype(BF16)
            lin = _dot(mixed_b, wp_ref[g]) + bp_ref[:, lanes]
            y_ref[:, lanes] = (lin * ps_ref[:, lanes]).astype(BF16)
            dpool = dcat[:, ATTN_WIDTH + LANES * g:ATTN_WIDTH + LANES * (g + 1)]
            gps_ref[:, lanes] += jnp.sum(dpool * lin, axis=0, keepdims=True)
            dy = dpool * ps_ref[:, lanes]
            gbp_ref[:, lanes] += jnp.sum(dy, axis=0, keepdims=True)
            dy_b = dy.astype(BF16)
            gwp_ref[g] += _dot_tn(mixed_b, dy_b)
            dmixed_ref[:, lanes] = _dot_nt(dy_b, wp_ref[g])
        gwo_ref[0:ATTN_WIDTH, :] += _dot_tn(a_ref[...], dx1b)
        gwo_ref[ATTN_WIDTH:, :] += _dot_tn(y_ref[...], dx1b)
        gbo_ref[...] += jnp.sum(dx1, axis=0, keepdims=True)

    row = lambda w: pl.BlockSpec((tm, w), lambda i: (i, 0))
    const = lambda shape: pl.BlockSpec(shape, lambda i: (0,) * len(shape))
    return _compute_call(
        body, (dx1, attn, u, u, w_pool, b_pool, pool_scale, w_out), grid=(t // tm,),
        in_specs=[row(D_MODEL), row(ATTN_WIDTH), row(POOL_WIDTH), _halo_before(tm, POOL_WIDTH),
                  _resident((4, LANES, LANES)), _resident((1, POOL_WIDTH)), _resident((1, POOL_WIDTH)),
                  _resident((D_MODEL, D_MODEL))],
        out_specs=[row(ATTN_WIDTH), row(POOL_WIDTH), const((D_MODEL, D_MODEL)), const((1, D_MODEL)),
                   const((4, LANES, LANES)), const((1, POOL_WIDTH)), const((1, POOL_WIDTH))],
        out_shape=[jax.ShapeDtypeStruct((t, ATTN_WIDTH), BF16), jax.ShapeDtypeStruct((t, POOL_WIDTH), F32),
                   jax.ShapeDtypeStruct((D_MODEL, D_MODEL), F32), jax.ShapeDtypeStruct((1, D_MODEL), F32),
                   jax.ShapeDtypeStruct((4, LANES, LANES), F32), jax.ShapeDtypeStruct((1, POOL_WIDTH), F32),
                   jax.ShapeDtypeStruct((1, POOL_WIDTH), F32)],
        scratch_shapes=[pltpu.VMEM((tm + POOL_HALO, POOL_WIDTH), F32), pltpu.VMEM((tm, POOL_WIDTH), BF16)],
        semantics=("arbitrary",), name="outproj_bwd", exchange=exchange)


def _attn_bwd_call(q, k, v, dout, sinks, cos_t, sin_t, bias, bias_first, exchange=None):
    t = q.shape[0]
    nb = t // BLOCK
    gb = _blocks_per_step(nb)
    qb = gb * BLOCK
    steps = nb // gb

    def body(sink_ref, q_ref, kc_ref, kp_ref, vc_ref, vp_ref, do_ref, cq_ref, sq_ref, ck_ref, sk_ref, b_ref, bf_ref,
             dq_ref, dk_ref, dv_ref, dsink_ref, newk_ref, newv_ref, carryk_ref, carryv_ref, headk_ref, headv_ref):
        n = pl.program_id(0)
        first = _first_half_mask((BLOCK, LANES))

        @pl.when(n == 0)
        def _():
            dsink_ref[...] = jnp.zeros_like(dsink_ref)

        @pl.when(n < steps)
        def _():
            head_row = lax.broadcasted_iota(jnp.int32, dsink_ref.shape, 0)
            lane2 = lax.broadcasted_iota(jnp.int32, (2 * BLOCK, LANES), 1)
            row_low = lax.broadcasted_iota(jnp.int32, (LANES, BLOCK), 0) < HEAD_DIM
            dsink = jnp.zeros(dsink_ref.shape, F32)
            for g in range(gb):
                r0 = g * BLOCK
                kband = _kv_band(kc_ref, kp_ref, g)
                vband = _kv_band(vc_ref, vp_ref, g)
                kband_t = kband.T
                bias_g = jnp.where(n == 0, bf_ref[...], b_ref[...]) if g == 0 else b_ref[...]
                cq, sq = cq_ref[r0:r0 + BLOCK, :], sq_ref[r0:r0 + BLOCK, :]
                dk_heads, dv_heads = [], []
                for kh in range(2):
                    k2 = _head_in_both_halves(kband, kh)
                    v2 = _head_in_both_halves(vband, kh)
                    kt_head = kband_t[HEAD_DIM * kh:HEAD_DIM * (kh + 1), :]
                    k2_t = jnp.concatenate([kt_head, kt_head], axis=0).astype(BF16)
                    qst = _stack_heads(q_ref, kh, r0)
                    dost = _stack_heads(do_ref, kh, r0)
                    s_t = _dot_nt(k2, qst) + bias_g
                    sink = _sink_row(sink_ref, kh)
                    m = jnp.maximum(jnp.max(s_t, axis=0, keepdims=True), sink)
                    e = jnp.exp(s_t - m)
                    e_sink = jnp.exp(sink - m)
                    inv = 1.0 / (jnp.sum(e, axis=0, keepdims=True) + e_sink)
                    p_t = e * inv
                    dp_t = _dot_nt(v2, dost)
                    delta = jnp.sum(p_t * dp_t, axis=0, keepdims=True)
                    ds_t = (p_t * (dp_t - delta)).astype(BF16)
                    sink_term = e_sink * inv * delta
                    for h in range(4):
                        val = -jnp.sum(sink_term[:, BLOCK * h:BLOCK * (h + 1)])
                        dsink = dsink + jnp.where(head_row == 4 * kh + h, val, 0.0)
                    dq_t = _dot(k2_t, ds_t)
                    for jj in range(2):
                        slab = jnp.where(row_low, dq_t[:, BLOCK * 2 * jj:BLOCK * (2 * jj + 1)],
                                         dq_t[:, BLOCK * (2 * jj + 1):BLOCK * (2 * jj + 2)]).T
                        dq_ref[r0:r0 + BLOCK, LANES * (2 * kh + jj):LANES * (2 * kh + jj + 1)] = (
                            (slab * cq + _rope_partner(slab * sq, first)) * Q_SCALE).astype(BF16)
                    dk2 = _dot(ds_t, qst)
                    dv2 = _dot(p_t.astype(BF16), dost)
                    dk_heads.append(dk2 + pltpu.roll(dk2, HEAD_DIM, 1))
                    dv_heads.append(dv2 + pltpu.roll(dv2, HEAD_DIM, 1))
                bandk = jnp.where(lane2 < HEAD_DIM, dk_heads[0], dk_heads[1])
                bandv = jnp.where(lane2 < HEAD_DIM, dv_heads[0], dv_heads[1])
                if g == 0:
                    headk_ref[...] = bandk[0:BLOCK]
                    headv_ref[...] = bandv[0:BLOCK]
                else:
                    newk_ref[r0 - BLOCK:r0, :] += bandk[0:BLOCK]
                    newv_ref[r0 - BLOCK:r0, :] += bandv[0:BLOCK]
                newk_ref[r0:r0 + BLOCK, :] = bandk[BLOCK:]
                newv_ref[r0:r0 + BLOCK, :] = bandv[BLOCK:]
            dsink_ref[...] += dsink

        @pl.when(n == steps)
        def _():
            headk_ref[...] = jnp.zeros_like(headk_ref)
            headv_ref[...] = jnp.zeros_like(headv_ref)

        @pl.when(n >= 1)
        def _():
            last = qb - BLOCK
            carryk_ref[last:, :] += headk_ref[...]
            carryv_ref[last:, :] += headv_ref[...]
            for g in range(gb):
                rows = slice(g * BLOCK, (g + 1) * BLOCK)
                dk = carryk_ref[rows, :]
                dk_ref[rows, :] = (dk * ck_ref[rows, :] + _rope_partner(dk * sk_ref[rows, :], first)).astype(BF16)
            dv_ref[...] = carryv_ref[...].astype(BF16)

        @pl.when(n < steps)
        def _():
            carryk_ref[...] = newk_ref[...]
            carryv_ref[...] = newv_ref[...]

    cur = lambda w: pl.BlockSpec((qb, w), lambda n: (jnp.minimum(n, steps - 1), 0))
    late = lambda w: pl.BlockSpec((qb, w), lambda n: (jnp.maximum(n - 1, 0), 0))
    prev = lambda w: pl.BlockSpec((BLOCK, w), lambda n: (jnp.maximum(jnp.minimum(n, steps - 1) * gb - 1, 0), 0))
    return _compute_call(
        body, (sinks, q, k, k, v, v, dout, cos_t, sin_t, cos_t, sin_t, bias, bias_first), grid=(steps + 1,),
        in_specs=[pl.BlockSpec(memory_space=pltpu.SMEM), cur(ATTN_WIDTH), cur(KV_WIDTH), prev(KV_WIDTH),
                  cur(KV_WIDTH), prev(KV_WIDTH), cur(ATTN_WIDTH), cur(LANES), cur(LANES), late(LANES), late(LANES),
                  _resident(bias.shape), _resident(bias.shape)],
        out_specs=[cur(ATTN_WIDTH), late(KV_WIDTH), late(KV_WIDTH), pl.BlockSpec((8, LANES), lambda n: (0, 0))],
        out_shape=[jax.ShapeDtypeStruct((t, ATTN_WIDTH), BF16), jax.ShapeDtypeStruct((t, KV_WIDTH), BF16),
                   jax.ShapeDtypeStruct((t, KV_WIDTH), BF16), jax.ShapeDtypeStruct((8, LANES), F32)],
        scratch_shapes=[pltpu.VMEM((qb, LANES), F32), pltpu.VMEM((qb, LANES), F32),
                        pltpu.VMEM((qb, LANES), F32), pltpu.VMEM((qb, LANES), F32),
                        pltpu.VMEM((BLOCK, LANES), F32), pltpu.VMEM((BLOCK, LANES), F32)],
        semantics=("arbitrary",), name="attn_bwd", exchange=exchange)


def _inproj_bwd_call(dq, dk, dv, dmixed, x, dx1, w_in, g_mix, exchange=None):
    t = x.shape[0]
    tm = _token_tile(t, 1024)
    n_tiles = t // tm

    def body(dq_ref, dk_ref, dv_ref, dm_ref, halo_ref, x_ref, dx1_ref, w_ref, g_ref,
             dx_ref, gw_ref, gb_ref, gg_ref, sc_ref):
        i = pl.program_id(0)

        @pl.when(i == 0)
        def _():
            for ref in (gw_ref, gb_ref, gg_ref):
                ref[...] = jnp.zeros_like(ref)

        tok = i * tm + lax.broadcasted_iota(jnp.int32, (tm, 1), 0)
        halo = jnp.where(i == n_tiles - 1, 0.0, halo_ref[...])
        du = []
        for g, size in enumerate(POOL_SIZES):
            lanes = slice(LANES * g, LANES * (g + 1))
            dm = dm_ref[:, lanes]
            sc_ref[0:tm, lanes] = dm / jnp.minimum(tok + 1, size).astype(F32)
            sc_ref[tm:, lanes] = halo[:, lanes] / float(size)
            acc = -dm
            for j in range(size):
                acc = acc + sc_ref[pl.ds(j, tm), lanes]
            du.append(acc)
        dz32 = jnp.concatenate([dq_ref[...].astype(F32), dk_ref[...].astype(F32), dv_ref[...].astype(F32)] + du,
                               axis=1)
        dz = dz32.astype(BF16)
        gb_ref[...] += jnp.sum(dz32, axis=0, keepdims=True)
        xh, r = _rms_stats(x_ref[...])
        g = g_ref[...]
        gw_ref[...] += _dot_tn(dz, (xh * g).astype(BF16))
        dh = _dot(dz, w_ref[...])
        gg_ref[...] += jnp.sum(dh * xh, axis=0, keepdims=True)
        dx_ref[...] = dx1_ref[...] + _rms_bwd(dh, xh, r, g)

    row = lambda w: pl.BlockSpec((tm, w), lambda i: (i, 0))
    const = lambda shape: pl.BlockSpec(shape, lambda i: (0,) * len(shape))
    last_halo = t // POOL_HALO - 1
    halo_after = pl.BlockSpec((POOL_HALO, POOL_WIDTH),
                              lambda i: (jnp.minimum((i + 1) * (tm // POOL_HALO), last_halo), 0))
    return _compute_call(
        body, (dq, dk, dv, dmixed, dmixed, x, dx1, w_in, g_mix), grid=(n_tiles,),
        in_specs=[row(ATTN_WIDTH), row(KV_WIDTH), row(KV_WIDTH), row(POOL_WIDTH), halo_after, row(D_MODEL),
                  row(D_MODEL), _resident((IN_WIDTH, D_MODEL)), _resident((1, D_MODEL))],
        out_specs=[row(D_MODEL), const((IN_WIDTH, D_MODEL)), const((1, IN_WIDTH)), const((1, D_MODEL))],
        out_shape=[jax.ShapeDtypeStruct((t, D_MODEL), F32), jax.ShapeDtypeStruct((IN_WIDTH, D_MODEL), F32),
                   jax.ShapeDtypeStruct((1, IN_WIDTH), F32), jax.ShapeDtypeStruct((1, D_MODEL), F32)],
        scratch_shapes=[pltpu.VMEM((tm + POOL_HALO, POOL_WIDTH), F32)],
        semantics=("arbitrary",), name="inproj_bwd", exchange=exchange)


def _reduce_to_owner(grads, place, hosts, wire_dtypes):
    me, c = place
    landed = hosts[0](_sibling_exchange(grads))
    partials = _pair_sum_call(grads, landed, c, wire_dtypes)
    slots = hosts[1](_chip_exchange(partials))
    return _chip_sum_call(partials, slots, me, c)


def _local_step(x, target, small, w_in, w_out, w_ffn, place=None):
    t = x.shape[0]
    on_mesh = place is not None
    cos_t, sin_t = _rope_tables(t)
    w_pool = small["w_pool"].astype(BF16)
    w_ffn = list(w_ffn)

    def gathering(bufs):
        return _gather_exchange(bufs) if on_mesh else None

    (h0,), got = _norm_call(x, small["g_mix"], exchange=gathering([w_in]))
    if on_mesh:
        w_in = got[0].reshape(IN_WIDTH, D_MODEL)
    (q, k, v, u), got = _inproj_call(h0, w_in, small["b_in"], cos_t, sin_t, exchange=gathering([w_out]))
    if on_mesh:
        w_out = got[0].reshape(D_MODEL, D_MODEL)
    bias_t, bias_first_t = _band_biases()
    (attn,), got = _attn_fwd_call(q, k, v, small["sinks"], bias_t, bias_first_t, exchange=gathering(w_ffn[:2]))
    if on_mesh:
        w_ffn[:2] = got
    (x1, h1), got = _outproj_call(attn, u, x, w_pool, small["b_pool"], small["pool_scale"], w_out,
                                          small["b_out"], small["g_ffn"], exchange=gathering(w_ffn[2:]))
    if on_mesh:
        w_ffn[2:] = got
    w_gate, w_up, w_down = w_ffn
    act, dgate, dup, dx1, dx2b, loss, g_g_final, g_g_ffn = _ffn_call(h1, x1, target, w_gate, w_up, w_down,
                                                                      small["g_ffn"], small["g_final"])
    ffn_grads = list(_ffn_bwd_weights_call(h1, dgate, dup, act, dx2b))
    kept = {}

    def outproj_bwd(exchange=None):
        kept["outproj"], results = _outproj_bwd_call(dx1, attn, u, w_pool, small["b_pool"], small["pool_scale"],
                                                     w_out, exchange=exchange)
        return results

    def attn_bwd(exchange=None):
        kept["attn"], results = _attn_bwd_call(q, k, v, kept["outproj"][0], small["sinks"], cos_t, sin_t, bias_t,
                                               bias_first_t, exchange=exchange)
        return results

    def inproj_bwd(exchange=None):
        dq, dk, dv, _ = kept["attn"]
        kept["inproj"], results = _inproj_bwd_call(dq, dk, dv, kept["outproj"][1], x, dx1, w_in, small["g_mix"],
                                                   exchange=exchange)
        return results

    if on_mesh:
        ffn_grads = _reduce_to_owner(ffn_grads, place, (outproj_bwd, attn_bwd), [BF16] * 3)
    else:
        outproj_bwd(), attn_bwd()
    inproj_bwd()
    _, _, g_w_out, g_b_out, g_w_pool, g_b_pool, g_pool_scale = kept["outproj"]
    g_sinks = kept["attn"][3]
    grad_x, g_w_in, g_b_in, g_g_mix = kept["inproj"]
    small_grads = dict(g_mix=g_g_mix, b_in=g_b_in, sinks=g_sinks[:, 0], w_pool=g_w_pool,
                       b_pool=g_b_pool, pool_scale=g_pool_scale, b_out=g_b_out, g_ffn=g_g_ffn, g_final=g_g_final)
    big_grads = dict(w_in=g_w_in, w_out=g_w_out, w_gate=ffn_grads[0], w_up=ffn_grads[1], w_down=ffn_grads[2])
    return loss, grad_x, small_grads, big_grads


def _cast_place_call(shards, me):
    na = len(shards)
    split = 2

    def body(me_ref, *refs):
        for a in range(na):
            refs[na + a][...] = refs[a][...].astype(BF16)

    in_specs = [pl.BlockSpec((s.shape[0] // split, s.shape[1]), lambda i, me_ref: (i, 0)) for s in shards]
    out_specs = [pl.BlockSpec((None, s.shape[0] // split, s.shape[1]), lambda i, me_ref: (me_ref[0], i, 0))
                 for s in shards]
    return pl.pallas_call(
        body,
        grid_spec=pltpu.PrefetchScalarGridSpec(num_scalar_prefetch=1, grid=(split,), in_specs=in_specs,
                                               out_specs=out_specs),
        out_shape=[jax.ShapeDtypeStruct((N_SHARDS,) + s.shape, BF16) for s in shards],
        compiler_params=_params("parallel"), name="cast_place_weights",
    )(me.reshape(1), *shards)


def _pair_sum_call(grads, landed, c, wire_dtypes):
    na = len(grads)
    split = 2

    def body(c_ref, *refs):
        ins, lands, outs = refs[:na], refs[na:2 * na], refs[2 * na:]
        for a in range(na):
            outs[a][...] = (ins[a][...] + lands[a][...]).astype(outs[a].dtype)

    in_specs, land_specs, out_specs, views = [], [], [], []
    for g in grads:
        ns, r, cols = g.shape
        rows = r // 2 // split
        views.append(g.reshape(ns, 2, r // 2, cols))
        in_specs.append(pl.BlockSpec((None, None, rows, cols), lambda s, i, c_ref: (s, c_ref[0], i, 0)))
        land_specs.append(pl.BlockSpec((None, rows, cols), lambda s, i, c_ref: (s, i, 0)))
        out_specs.append(pl.BlockSpec((None, rows, cols), lambda s, i, c_ref: (s, i, 0)))
    return pl.pallas_call(
        body,
        grid_spec=pltpu.PrefetchScalarGridSpec(num_scalar_prefetch=1, grid=(N_SHARDS, split),
                                               in_specs=in_specs + land_specs, out_specs=out_specs),
        out_shape=[jax.ShapeDtypeStruct(l.shape, dt) for l, dt in zip(landed, wire_dtypes)],
        compiler_params=_params("parallel", "parallel"), name="reduce_pair_sum",
    )(c.reshape(1), *views, *landed)


def _chip_sum_call(partials, slots, me, c):
    na = len(slots)
    split = 2

    def body(place_ref, *refs):
        mine, lands, outs = refs[:na], refs[na:2 * na], refs[2 * na:]
        for j in range(N_SHARDS):
            @pl.when(place_ref[0] == j)
            def _():
                for a in range(na):
                    terms = [mine[a][...] if k == j else lands[a][k] for k in range(N_SHARDS)]
                    t0, t1, t2, t3 = (term.astype(F32) for term in terms)
                    outs[a][...] = ((t0 + t1) + t2) + t3

    mine_specs, land_specs, out_specs = [], [], []
    for s in slots:
        ns, h, cols = s.shape
        mine_specs.append(pl.BlockSpec((None, h // split, cols), lambda i, place: (place[0], i, 0)))
        land_specs.append(pl.BlockSpec((ns, h // split, cols), lambda i, place: (0, i, 0)))
        out_specs.append(pl.BlockSpec((h // split, cols), lambda i, place: (place[1] * split + i, 0)))
    return pl.pallas_call(
        body,
        grid_spec=pltpu.PrefetchScalarGridSpec(num_scalar_prefetch=1, grid=(split,),
                                               in_specs=mine_specs + land_specs, out_specs=out_specs),
        out_shape=[jax.ShapeDtypeStruct((2 * s.shape[1], s.shape[2]), F32) for s in slots],
        compiler_params=_params("parallel"), name="reduce_chip_sum",
    )(jnp.stack([me, c]), *partials, *slots)


def _adamw(w, g, m, v):
    m = ADAM_B1 * m + (1.0 - ADAM_B1) * g
    v = ADAM_B2 * v + (1.0 - ADAM_B2) * jnp.square(g)
    m_hat = m / (1.0 - ADAM_B1 ** ADAM_STEP)
    v_hat = v / (1.0 - ADAM_B2 ** ADAM_STEP)
    return -ADAM_LR * (m_hat / (jnp.sqrt(v_hat) + ADAM_EPS) + ADAM_WD * w), m, v


def _adamw_call(ws, gs, ms, vs):
    na = len(ws)
    split = 4

    def body(*refs):
        w_refs, g_refs, m_refs, v_refs = (refs[k * na:(k + 1) * na] for k in range(4))
        d_refs, nm_refs, nv_refs = (refs[(4 + k) * na:(5 + k) * na] for k in range(3))
        for a in range(na):
            d_refs[a][...], nm_refs[a][...], nv_refs[a][...] = _adamw(
                w_refs[a][...], g_refs[a][...], m_refs[a][...], v_refs[a][...])

    specs = [pl.BlockSpec((w.shape[0] // split, w.shape[1]), lambda i: (i, 0)) for w in ws]
    shapes = [jax.ShapeDtypeStruct(w.shape, F32) for w in ws]
    outs = pl.pallas_call(
        body, grid=(split,), in_specs=specs * 4, out_specs=specs * 3, out_shape=shapes * 3,
        compiler_params=_params("parallel"), name="adamw",
    )(*ws, *gs, *ms, *vs)
    return outs[:na], outs[na:2 * na], outs[2 * na:]


def _small_rows():
    first, row = {}, 0
    for name, (r, c) in SMALL_VIEWS.items():
        first[name] = row
        row += r * (-(-c // LANES))
    return first, row, -(-(row + 1) // 64) * 64


def _pack_small(arrays, loss):
    _, used, rows = _small_rows()
    parts = []
    for name, (r, c) in SMALL_VIEWS.items():
        a = arrays[name].reshape(r, c)
        parts.append(jnp.pad(a, ((0, 0), (0, LANES - c))) if c < LANES else a.reshape(r * c // LANES, LANES))
    parts += [loss, jnp.zeros((rows - used - 1, LANES), F32)]
    return jnp.concatenate(parts, axis=0)


def _adamw_small_call(ws, packed_grads, ms, vs):
    names = list(SMALL_VIEWS)
    first, _, _ = _small_rows()
    n = len(names)

    def body(g_ref, *refs):
        w_refs, m_refs, v_refs = refs[:n], refs[n:2 * n], refs[2 * n:3 * n]
        outs = refs[3 * n:]
        for a, name in enumerate(names):
            r, c = SMALL_VIEWS[name]
            r0 = first[name]
            if c == LANES:
                g = g_ref[r0:r0 + r, :]
            elif c < LANES:
                g = g_ref[r0:r0 + 1, 0:c]
            else:
                g = jnp.concatenate([g_ref[r0 + j:r0 + j + 1, :] for j in range(c // LANES)], axis=1)
            delta, m, v = _adamw(w_refs[a][...], g, m_refs[a][...], v_refs[a][...])
            for k, val in enumerate((g, delta, m, v)):
                outs[k * n + a][...] = val

    shapes = [jax.ShapeDtypeStruct(SMALL_VIEWS[name], F32) for name in names]
    outs = pl.pallas_call(
        body, out_shape=shapes * 4, compiler_params=pltpu.CompilerParams(vmem_limit_bytes=VMEM_LIMIT),
        name="adamw_small",
    )(packed_grads, *[ws[k] for k in names], *[ms[k] for k in names], *[vs[k] for k in names])
    return [dict(zip(names, outs[k * n:(k + 1) * n])) for k in range(4)]


def kernel(x, g_mix, w_in, b_in, sinks, w_pool, b_pool, pool_scale, w_out, b_out, g_ffn, w_gate, w_up, w_down, g_final, loss_target, m_g_mix, m_w_in, m_b_in, m_sinks, m_w_pool, m_b_pool, m_pool_scale, m_w_out, m_b_out, m_g_ffn, m_w_gate, m_w_up, m_w_down, m_g_final, v_g_mix, v_w_in, v_b_in, v_sinks, v_w_pool, v_b_pool, v_pool_scale, v_w_out, v_b_out, v_g_ffn, v_w_gate, v_w_up, v_w_down, v_g_final):
    given = dict(locals())
    big_names = ("w_in", "w_out", "w_gate", "w_up", "w_down")
    out_shapes = {n: given[n].shape for n in SMALL_NAMES + big_names}
    c = lax.axis_index("c")

    me = 2 * lax.axis_index("x") + lax.axis_index("y")
    transposed = ("w_in", "w_gate", "w_up")

    def shard_view(a, name):
        return a[0].T if name in transposed else a[0]

    placed = _cast_place_call([shard_view(given[n], n) for n in big_names], me)

    small = dict(g_mix=g_mix, b_in=b_in, sinks=sinks, w_pool=w_pool[0], b_pool=b_pool.reshape(1, POOL_WIDTH),
                 pool_scale=pool_scale.reshape(1, POOL_WIDTH), b_out=b_out, g_ffn=g_ffn,
                 g_final=g_final.reshape(1, D_MODEL))
    loss, grad_x, small_grads, big_grads = _local_step(
        x[0], loss_target[0], small, placed[0], placed[1], placed[2:], place=(me, c))

    packed = _pack_small(small_grads, loss)
    grads = [
        big_grads["w_in"].reshape(N_SHARDS, IN_WIDTH // N_SHARDS, D_MODEL),
        big_grads["w_out"].reshape(N_SHARDS, D_MODEL // N_SHARDS, D_MODEL),
        jnp.broadcast_to(packed[None], (N_SHARDS,) + packed.shape),
    ]
    alone = [functools.partial(_exchange_call, name=name) for name in
             ("reduce_sibling_exchange", "reduce_chip_exchange")]
    h_in, h_out, h_small = _reduce_to_owner(grads, (me, c), alone, [BF16, BF16, F32])
    totals = _exchange_call(_sibling_allgather(
        [h_in, h_out, big_grads["w_gate"], big_grads["w_up"], big_grads["w_down"], h_small]), "reduce_sibling_allgather")
    loss = totals[-1][_small_rows()[1], 0]

    def owned(prefix):
        return [shard_view(given[prefix + n], n) for n in big_names]

    def owned_small(prefix):
        return {n: given[prefix + n].reshape(SMALL_VIEWS[n]) for n in SMALL_NAMES}

    big = (totals[:-1],) + _adamw_call(owned(""), totals[:-1], owned("m_"), owned("v_"))
    small_groups = _adamw_small_call(owned_small(""), totals[-1], owned_small("m_"), owned_small("v_"))

    order = ("g_mix", "w_in", "b_in", "sinks", "w_pool", "b_pool", "pool_scale", "w_out", "b_out", "g_ffn",
             "w_gate", "w_up", "w_down", "g_final")
    results = [loss, grad_x.reshape(x.shape)]
    for group, small_group in zip(big, small_groups):
        named = {n: (a.T if n in transposed else a) for n, a in zip(big_names, group)}
        named.update(small_group)
        results += [named[n].reshape(out_shapes[n]) for n in order]
    return tuple(results)
```

```python
import functools
from typing import Any, Callable, Mapping, NamedTuple, Sequence

import jax
import jax.numpy as jnp
from jax import lax
from jax.experimental import pallas as pl
from jax.experimental.pallas import tpu as pltpu

F32 = jnp.float32
BF16 = jnp.bfloat16

D_MODEL = 1024
ATTN_WIDTH = 512
KV_WIDTH = 128
POOL_WIDTH = 512
IN_WIDTH = ATTN_WIDTH + 2 * KV_WIDTH + POOL_WIDTH
HEAD_DIM = 64
N_Q_HEADS = 8
BLOCK = 128
POOL_SIZES = (2, 4, 8, 16)
POOL_HALO = 16
ROPE_THETA = 10000.0
RMS_EPS = 1e-5
Q_SCALE = HEAD_DIM ** -0.5

ADAM_LR = 0.001
ADAM_B1 = 0.9
ADAM_B2 = 0.999
ADAM_EPS = 1e-08
ADAM_WD = 0.01
ADAM_STEP = 10

N_SHARDS = 4
LANES = 128
VMEM_LIMIT = 56 * 1024 * 1024
FFN_TOKEN_TILE = 256

MESH = pl.DeviceIdType.MESH
HBM_SPEC = pl.BlockSpec(memory_space=pltpu.HBM)

SMALL_VIEWS = dict(w_pool=(4 * LANES, LANES), b_pool=(4, LANES), pool_scale=(4, LANES), g_mix=(1, D_MODEL),
                   b_in=(1, IN_WIDTH), b_out=(1, D_MODEL), g_ffn=(1, D_MODEL), g_final=(1, D_MODEL),
                   sinks=(1, N_Q_HEADS))
SMALL_NAMES = tuple(SMALL_VIEWS)


def _dot(a, b):
    return jnp.dot(a, b, preferred_element_type=F32)


def _dot_nt(a, b):
    return lax.dot_general(a, b, (((1,), (1,)), ((), ())), preferred_element_type=F32)


def _dot_tn(a, b):
    return lax.dot_general(a, b, (((0,), (0,)), ((), ())), preferred_element_type=F32)


def _params(*semantics):
    return pltpu.CompilerParams(dimension_semantics=semantics, vmem_limit_bytes=VMEM_LIMIT)


def _resident(shape):
    return pl.BlockSpec(shape, lambda *_: (0,) * len(shape), pipeline_mode=pl.Buffered(1))


def _token_tile(t, largest=512):
    for tm in (1024, 512, 256, 128):
        if tm <= largest and t % tm == 0:
            return tm
    raise ValueError(f"sequence length {t} is not a multiple of 128")


def _rms_stats(xf):
    r = lax.rsqrt(jnp.mean(xf * xf, axis=-1, keepdims=True) + RMS_EPS)
    return xf * r, r


def _rms_bwd(dy, xh, r, g):
    dxh = dy * g
    return r * (dxh - xh * jnp.mean(dxh * xh, axis=-1, keepdims=True))


def _rope_partner(t, first):
    return jnp.where(first, pltpu.roll(t, LANES - HEAD_DIM // 2, 1), pltpu.roll(t, HEAD_DIM // 2, 1))


def _first_half_mask(shape):
    lane = lax.broadcasted_iota(jnp.int32, shape, 1)
    return (lane % HEAD_DIM) < (HEAD_DIM // 2)


def _rope_tables(t):
    inv_freq = 1.0 / (ROPE_THETA ** (jnp.arange(0, HEAD_DIM, 2, dtype=F32) / HEAD_DIM))
    ang = jnp.arange(t, dtype=F32)[:, None] * inv_freq[None, :]
    cos, sin = jnp.cos(ang), jnp.sin(ang)
    cos_t = jnp.tile(jnp.concatenate([cos, cos], axis=-1), (1, LANES // HEAD_DIM))
    sin_t = jnp.tile(jnp.concatenate([-sin, sin], axis=-1), (1, LANES // HEAD_DIM))
    return cos_t, sin_t


class _Exchange(NamedTuple):
    inputs: Sequence[Any]
    out_shapes: Sequence[Any]
    aliases: Mapping[int, int]
    sems: Sequence[Any]
    start: Callable[..., None]
    finish: Callable[..., None]


def _compute_call(body, operands, *, grid, in_specs, out_specs, out_shape, scratch_shapes=(), semantics, name,
                  exchange=None):
    n_in, n_out, n_scr = len(in_specs), len(out_specs), len(scratch_shapes)
    if exchange is None:
        return pl.pallas_call(body, grid=grid, in_specs=in_specs, out_specs=out_specs, out_shape=out_shape,
                              scratch_shapes=scratch_shapes, compiler_params=_params(*semantics),
                              name=name)(*operands), []
    k_in, k_out = len(exchange.inputs), len(exchange.out_shapes)

    def hosting(*refs):
        ins, rest = refs[:n_in], refs[n_in:]
        ex_ins, rest = rest[:k_in], rest[k_in:]
        outs, rest = rest[:n_out], rest[n_out:]
        ex_outs, rest = rest[:k_out], rest[k_out:]
        scratch, sems = rest[:n_scr], rest[n_scr:]
        steps = [pl.program_id(d) for d in range(len(grid))]
        is_first = functools.reduce(jnp.logical_and, [s == 0 for s in steps])
        is_last = functools.reduce(jnp.logical_and, [s == g - 1 for s, g in zip(steps, grid)])

        @pl.when(is_first)
        def _():
            exchange.start(ex_ins, ex_outs, sems)

        body(*ins, *outs, *scratch)

        @pl.when(is_last)
        def _():
            exchange.finish(ex_ins, ex_outs, sems)

    results = pl.pallas_call(
        hosting, grid=grid,
        in_specs=list(in_specs) + [HBM_SPEC] * k_in, out_specs=list(out_specs) + [HBM_SPEC] * k_out,
        out_shape=list(out_shape) + list(exchange.out_shapes),
        scratch_shapes=list(scratch_shapes) + list(exchange.sems),
        input_output_aliases={n_in + i: n_out + j for i, j in exchange.aliases.items()},
        compiler_params=_params(*semantics), name=name,
    )(*operands, *exchange.inputs)
    return results[:n_out], results[n_out:]


def _exchange_call(exchange, name):
    n_in, n_out = len(exchange.inputs), len(exchange.out_shapes)

    def body(*refs):
        ins, outs, sems = refs[:n_in], refs[n_in:n_in + n_out], refs[n_in + n_out:]
        exchange.start(ins, outs, sems)
        exchange.finish(ins, outs, sems)

    return pl.pallas_call(
        body, in_specs=[HBM_SPEC] * n_in, out_specs=[HBM_SPEC] * n_out, out_shape=list(exchange.out_shapes),
        input_output_aliases=dict(exchange.aliases), scratch_shapes=list(exchange.sems),
        compiler_params=pltpu.CompilerParams(has_side_effects=True), name=name,
    )(*exchange.inputs)


def _mesh_place():
    x, y, c = lax.axis_index("x"), lax.axis_index("y"), lax.axis_index("c")
    other_chips = [(1 - x, y), (x, 1 - y), (1 - x, 1 - y)]
    return x, y, c, other_chips


def _half(ref, c, rows):
    return ref.at[pl.ds(pl.multiple_of(c * rows, 16), rows), :]


def _remote(src, dst, send_sems, recv_sems, idx, to):
    return functools.partial(pltpu.make_async_remote_copy, src_ref=src, dst_ref=dst, send_sem=send_sems.at[idx],
                             recv_sem=recv_sems.at[idx], device_id=to, device_id_type=MESH)


def _gather_exchange(bufs):
    na = len(bufs)

    def copies(outs, sems):
        send_sems, recv_sems = sems
        x, y, c, other_chips = _mesh_place()
        me, sibling = 2 * x + y, (x, y, 1 - c)
        ici_out, ici_in, fwd_out, fwd_in = [], [], [], []
        for j, (cx, cy) in enumerate(other_chips):
            for a in range(na):
                rows = outs[a].shape[1] // 2
                mine = _half(outs[a].at[me], c, rows)
                landed = _half(outs[a].at[2 * cx + cy], c, rows)
                passed = _half(outs[a].at[2 * cx + cy], 1 - c, rows)
                ici_out.append(_remote(mine, mine, send_sems, recv_sems, j * na + a, (cx, cy, c)))
                ici_in.append(_remote(landed, landed, send_sems, recv_sems, j * na + a, (cx, cy, c)))
                fwd_out.append(_remote(landed, landed, send_sems, recv_sems, (3 + j) * na + a, sibling))
                fwd_in.append(_remote(passed, passed, send_sems, recv_sems, (3 + j) * na + a, sibling))
        return ici_out, ici_in, fwd_out, fwd_in

    def start(ins, outs, sems):
        for cp in copies(outs, sems)[0]:
            cp().start()

    def finish(ins, outs, sems):
        ici_out, ici_in, fwd_out, fwd_in = copies(outs, sems)
        for arrived, forward in zip(ici_in, fwd_out):
            arrived().wait_recv()
            forward().start()
        for cp in fwd_in:
            cp().wait_recv()
        for cp in ici_out + fwd_out:
            cp().wait_send()

    return _Exchange(inputs=bufs, out_shapes=[jax.ShapeDtypeStruct(b.shape, b.dtype) for b in bufs],
                     aliases={a: a for a in range(na)},
                     sems=[pltpu.SemaphoreType.DMA((6 * na,)), pltpu.SemaphoreType.DMA((6 * na,))],
                     start=start, finish=finish)


def _sibling_exchange(grads):
    na = len(grads)

    def copies(ins, outs, sems):
        x, y, c, _ = _mesh_place()
        out = []
        for a in range(na):
            rows = ins[a].shape[1] // 2
            src = ins[a].at[:, pl.ds(pl.multiple_of((1 - c) * rows, 8), rows), :]
            out.append(_remote(src, outs[a], sems[0], sems[1], a, (x, y, 1 - c)))
        return out

    def start(ins, outs, sems):
        for cp in copies(ins, outs, sems):
            cp().start()

    def finish(ins, outs, sems):
        for cp in copies(ins, outs, sems):
            cp().wait_recv()
            cp().wait_send()

    return _Exchange(inputs=grads,
                     out_shapes=[jax.ShapeDtypeStruct((g.shape[0], g.shape[1] // 2, g.shape[2]), g.dtype)
                                 for g in grads],
                     aliases={}, sems=[pltpu.SemaphoreType.DMA((na,)), pltpu.SemaphoreType.DMA((na,))],
                     start=start, finish=finish)


def _chip_exchange(partials):
    na = len(partials)

    def copies(ins, outs, sems):
        x, y, c, other_chips = _mesh_place()
        me = 2 * x + y
        sends, arrivals = [], []
        for j, (cx, cy) in enumerate(other_chips):
            for a in range(na):
                slot = outs[a].at[2 * cx + cy]
                sends.append(_remote(ins[a].at[2 * cx + cy], outs[a].at[me], sems[0], sems[1], j * na + a, (cx, cy, c)))
                arrivals.append(_remote(slot, slot, sems[0], sems[1], j * na + a, (cx, cy, c)))
        return sends, arrivals

    def start(ins, outs, sems):
        for cp in copies(ins, outs, sems)[0]:
            cp().start()

    def finish(ins, outs, sems):
        sends, arrivals = copies(ins, outs, sems)
        for cp in arrivals:
            cp().wait_recv()
        for cp in sends:
            cp().wait_send()

    return _Exchange(inputs=partials, out_shapes=[jax.ShapeDtypeStruct(p.shape, p.dtype) for p in partials],
                     aliases={}, sems=[pltpu.SemaphoreType.DMA((3 * na,)), pltpu.SemaphoreType.DMA((3 * na,))],
                     start=start, finish=finish)


def _sibling_allgather(bufs):
    na = len(bufs)

    def piece(outs, sems, a, half):
        x, y, c, _ = _mesh_place()
        rows = outs[a].shape[0] // 2
        rows_ref = outs[a].at[pl.ds(pl.multiple_of(half * rows, 8), rows), :]
        return _remote(rows_ref, rows_ref, sems[0], sems[1], a, (x, y, 1 - c))

    def start(ins, outs, sems):
        c = lax.axis_index("c")
        for a in range(na):
            piece(outs, sems, a, c)().start()

    def finish(ins, outs, sems):
        c = lax.axis_index("c")
        for a in range(na):
            piece(outs, sems, a, 1 - c)().wait_recv()
        for a in range(na):
            piece(outs, sems, a, c)().wait_send()

    return _Exchange(inputs=bufs, out_shapes=[jax.ShapeDtypeStruct(b.shape, b.dtype) for b in bufs],
                     aliases={a: a for a in range(na)},
                     sems=[pltpu.SemaphoreType.DMA((na,)), pltpu.SemaphoreType.DMA((na,))],
                     start=start, finish=finish)


def _norm_call(x, g_mix, exchange=None):
    t = x.shape[0]
    tm = _token_tile(t)

    def body(x_ref, g_ref, h_ref):
        xh, _ = _rms_stats(x_ref[...])
        h_ref[...] = (xh * g_ref[...]).astype(BF16)

    row = pl.BlockSpec((tm, D_MODEL), lambda i: (i, 0))
    return _compute_call(body, (x, g_mix), grid=(t // tm,), in_specs=[row, _resident((1, D_MODEL))],
                         out_specs=[row], out_shape=[jax.ShapeDtypeStruct((t, D_MODEL), BF16)],
                         semantics=("parallel",), name="norm_fwd", exchange=exchange)


def _inproj_call(h0, w_in, b_in, cos_t, sin_t, exchange=None):
    t = h0.shape[0]
    tm = _token_tile(t, 1024)

    def body(h_ref, w_ref, b_ref, c_ref, s_ref, q_ref, k_ref, v_ref, u_ref):
        z = _dot_nt(h_ref[...], w_ref[...]) + b_ref[...]
        cos, sin = c_ref[...], s_ref[...]
        first = _first_half_mask((tm, LANES))
        for j in range(ATTN_WIDTH // LANES):
            zj = z[:, LANES * j:LANES * (j + 1)]
            q_ref[:, LANES * j:LANES * (j + 1)] = ((zj * cos + _rope_partner(zj, first) * sin) * Q_SCALE).astype(BF16)
        zk = z[:, ATTN_WIDTH:ATTN_WIDTH + KV_WIDTH]
        k_ref[...] = (zk * cos + _rope_partner(zk, first) * sin).astype(BF16)
        v_ref[...] = z[:, ATTN_WIDTH + KV_WIDTH:ATTN_WIDTH + 2 * KV_WIDTH].astype(BF16)
        u_ref[...] = z[:, ATTN_WIDTH + 2 * KV_WIDTH:].astype(BF16)

    row = lambda w: pl.BlockSpec((tm, w), lambda i: (i, 0))
    return _compute_call(
        body, (h0, w_in, b_in, cos_t, sin_t), grid=(t // tm,),
        in_specs=[row(D_MODEL), _resident((IN_WIDTH, D_MODEL)), _resident((1, IN_WIDTH)), row(LANES), row(LANES)],
        out_specs=[row(ATTN_WIDTH), row(KV_WIDTH), row(KV_WIDTH), row(POOL_WIDTH)],
        out_shape=[jax.ShapeDtypeStruct((t, ATTN_WIDTH), BF16),
                   jax.ShapeDtypeStruct((t, KV_WIDTH), BF16), jax.ShapeDtypeStruct((t, KV_WIDTH), BF16),
                   jax.ShapeDtypeStruct((t, POOL_WIDTH), BF16)],
        semantics=("parallel",), name="inproj_fwd", exchange=exchange)


def _head_in_both_halves(band, kv_head):
    lane = lax.broadcasted_iota(jnp.int32, band.shape, 1)
    own = (lane < HEAD_DIM) if kv_head == 0 else (lane >= HEAD_DIM)
    return jnp.where(own, band, pltpu.roll(band, HEAD_DIM, 1)).astype(BF16)


def _stack_heads(ref, kv_head, row0):
    low = lax.broadcasted_iota(jnp.int32, (BLOCK, LANES), 1) < HEAD_DIM
    parts = []
    for jj in range(2):
        slab = ref[row0:row0 + BLOCK, LANES * (2 * kv_head + jj):LANES * (2 * kv_head + jj + 1)]
        zero = jnp.zeros_like(slab)
        parts += [jnp.where(low, slab, zero), jnp.where(low, zero, slab)]
    return jnp.concatenate(parts, axis=0)


def _band_biases():
    c = lax.broadcasted_iota(jnp.int32, (2 * BLOCK, 4 * BLOCK), 0)
    i = lax.broadcasted_iota(jnp.int32, (2 * BLOCK, 4 * BLOCK), 1) % BLOCK
    vis = (c > i) & (c <= i + BLOCK)
    return jnp.where(vis, 0.0, -jnp.inf).astype(F32), jnp.where(vis & (c >= BLOCK), 0.0, -jnp.inf).astype(F32)


def _blocks_per_step(nb):
    for g in (4, 2, 1):
        if nb % g == 0:
            return g


def _sink_row(sink_ref, kv_head):
    return jnp.concatenate([jnp.full((1, BLOCK), sink_ref[0, 4 * kv_head + g], F32) for g in range(4)], axis=1)


def _kv_band(cur_ref, prev_ref, g):
    r0 = g * BLOCK
    prev = prev_ref[...] if g == 0 else cur_ref[r0 - BLOCK:r0, :]
    return jnp.concatenate([prev, cur_ref[r0:r0 + BLOCK, :]], axis=0).astype(F32)


def _attn_fwd_call(q, k, v, sinks, bias, bias_first, exchange=None):
    t = q.shape[0]
    nb = t // BLOCK
    gb = _blocks_per_step(nb)
    qb = gb * BLOCK

    def body(sink_ref, q_ref, kc_ref, kp_ref, vc_ref, vp_ref, b_ref, bf_ref, o_ref):
        n = pl.program_id(0)
        row_low = lax.broadcasted_iota(jnp.int32, (LANES, BLOCK), 0) < HEAD_DIM
        for g in range(gb):
            r0 = g * BLOCK
            kband = _kv_band(kc_ref, kp_ref, g)
            vband_t = _kv_band(vc_ref, vp_ref, g).T
            bias_g = jnp.where(n == 0, bf_ref[...], b_ref[...]) if g == 0 else b_ref[...]
            for kh in range(2):
                k2 = _head_in_both_halves(kband, kh)
                vt_head = vband_t[HEAD_DIM * kh:HEAD_DIM * (kh + 1), :]
                v2_t = jnp.concatenate([vt_head, vt_head], axis=0).astype(BF16)
                s_t = _dot_nt(k2, _stack_heads(q_ref, kh, r0)) + bias_g
                sink = _sink_row(sink_ref, kh)
                m = jnp.maximum(jnp.max(s_t, axis=0, keepdims=True), sink)
                e = jnp.exp(s_t - m)
                inv = 1.0 / (jnp.sum(e, axis=0, keepdims=True) + jnp.exp(sink - m))
                o_t = _dot(v2_t, (e * inv).astype(BF16))
                for jj in range(2):
                    slab = jnp.where(row_low, o_t[:, BLOCK * 2 * jj:BLOCK * (2 * jj + 1)],
                                     o_t[:, BLOCK * (2 * jj + 1):BLOCK * (2 * jj + 2)]).T
                    o_ref[r0:r0 + BLOCK, LANES * (2 * kh + jj):LANES * (2 * kh + jj + 1)] = slab.astype(BF16)

    cur = lambda w: pl.BlockSpec((qb, w), lambda n: (n, 0))
    prev = lambda w: pl.BlockSpec((BLOCK, w), lambda n: (jnp.maximum(n * gb - 1, 0), 0))
    return _compute_call(
        body, (sinks, q, k, k, v, v, bias, bias_first), grid=(nb // gb,),
        in_specs=[pl.BlockSpec(memory_space=pltpu.SMEM), cur(ATTN_WIDTH), cur(KV_WIDTH), prev(KV_WIDTH),
                  cur(KV_WIDTH), prev(KV_WIDTH), _resident(bias.shape), _resident(bias.shape)],
        out_specs=[cur(ATTN_WIDTH)],
        out_shape=[jax.ShapeDtypeStruct((t, ATTN_WIDTH), BF16)],
        semantics=("parallel",), name="attn_fwd", exchange=exchange)


def _pool_mixed(sc_ref, u_ref, halo_ref, i, tm):
    sc_ref[0:POOL_HALO, :] = jnp.where(i == 0, 0.0, halo_ref[...].astype(F32))
    sc_ref[POOL_HALO:, :] = u_ref[...].astype(F32)
    tok = i * tm + lax.broadcasted_iota(jnp.int32, (tm, 1), 0)
    mixed = []
    for g, size in enumerate(POOL_SIZES):
        lanes = slice(LANES * g, LANES * (g + 1))
        cur = sc_ref[pl.ds(POOL_HALO, tm), lanes]
        acc = cur
        for j in range(1, size):
            acc = acc + sc_ref[pl.ds(POOL_HALO - j, tm), lanes]
        count = jnp.minimum(tok + 1, size).astype(F32)
        mixed.append(acc / count - cur)
    return mixed


def _halo_before(tm, width):
    return pl.BlockSpec((POOL_HALO, width), lambda i: (jnp.maximum(i * (tm // POOL_HALO) - 1, 0), 0))


def _outproj_call(attn, u, x, w_pool, b_pool, pool_scale, w_out, b_out, g_ffn, exchange=None):
    t = x.shape[0]
    tm = _token_tile(t)

    def body(a_ref, u_ref, halo_ref, x_ref, wp_ref, bp_ref, ps_ref, wo_ref, bo_ref, g_ref,
             x1_ref, h1_ref, sc_ref, y_ref):
        i = pl.program_id(0)
        for g, mixed in enumerate(_pool_mixed(sc_ref, u_ref, halo_ref, i, tm)):
            lanes = slice(LANES * g, LANES * (g + 1))
            lin = _dot(mixed.astype(BF16), wp_ref[g]) + bp_ref[:, lanes]
            y_ref[:, lanes] = (lin * ps_ref[:, lanes]).astype(BF16)
        x1 = (x_ref[...] + _dot(a_ref[...], wo_ref[0:ATTN_WIDTH, :]) + _dot(y_ref[...], wo_ref[ATTN_WIDTH:, :])
              + bo_ref[...])
        x1_ref[...] = x1
        xh, _ = _rms_stats(x1)
        h1_ref[...] = (xh * g_ref[...]).astype(BF16)

    row = lambda w: pl.BlockSpec((tm, w), lambda i: (i, 0))
    return _compute_call(
        body, (attn, u, u, x, w_pool, b_pool, pool_scale, w_out, b_out, g_ffn), grid=(t // tm,),
        in_specs=[row(ATTN_WIDTH), row(POOL_WIDTH), _halo_before(tm, POOL_WIDTH), row(D_MODEL),
                  _resident((4, LANES, LANES)), _resident((1, POOL_WIDTH)), _resident((1, POOL_WIDTH)),
                  _resident((D_MODEL, D_MODEL)), _resident((1, D_MODEL)), _resident((1, D_MODEL))],
        out_specs=[row(D_MODEL), row(D_MODEL)],
        out_shape=[jax.ShapeDtypeStruct((t, D_MODEL), F32), jax.ShapeDtypeStruct((t, D_MODEL), BF16)],
        scratch_shapes=[pltpu.VMEM((tm + POOL_HALO, POOL_WIDTH), F32), pltpu.VMEM((tm, POOL_WIDTH), BF16)],
        semantics=("parallel",), name="outproj_fwd", exchange=exchange)


def _ffn_call(h1, x1, target, w_gate, w_up, w_down, g_ffn, g_final):
    t = x1.shape[0]
    tm = _token_tile(t, FFN_TOKEN_TILE)
    ns, fs, _ = w_gate.shape

    def body(h_ref, x1_ref, tgt_ref, wg_ref, wu_ref, wd_ref, g_ref, gf_ref,
             act_ref, dgate_ref, dup_ref, dx1_ref, dx2b_ref, loss_ref, dgf_ref, dg_ref, gate_ref, up_ref):
        @pl.when(pl.program_id(0) == 0)
        def _():
            loss_ref[...] = jnp.zeros_like(loss_ref)
            dgf_ref[...] = jnp.zeros_like(dgf_ref)
            dg_ref[...] = jnp.zeros_like(dg_ref)

        h = h_ref[...]
        x1 = x1_ref[...]
        x2 = x1
        for s in range(ns):
            gate = _dot_nt(h, wg_ref[s])
            up = _dot_nt(h, wu_ref[s])
            act = (gate * jax.nn.sigmoid(gate) * up).astype(BF16)
            gate_ref[s] = gate
            up_ref[s] = up
            act_ref[s] = act
            x2 = x2 + _dot(act, wd_ref[s])
        xh2, r2 = _rms_stats(x2)
        gf = gf_ref[...]
        err = xh2 * gf - tgt_ref[...]
        loss_ref[...] += 0.5 * jnp.sum(err * err) / D_MODEL
        dy = err / D_MODEL
        dgf_ref[...] += jnp.sum(dy * xh2, axis=0, keepdims=True)
        dx2 = _rms_bwd(dy, xh2, r2, gf)
        dx2b = dx2.astype(BF16)
        dx2b_ref[...] = dx2b
        dh = None
        for s in range(ns):
            dact = _dot_nt(dx2b, wd_ref[s])
            gate = gate_ref[s]
            sg = jax.nn.sigmoid(gate)
            dup = (dact * (gate * sg)).astype(BF16)
            dgate = (dact * up_ref[s] * (sg * (1.0 + gate * (1.0 - sg)))).astype(BF16)
            dgate_ref[s] = dgate
            dup_ref[s] = dup
            part = _dot(dgate, wg_ref[s]) + _dot(dup, wu_ref[s])
            dh = part if dh is None else dh + part
        xh1, r1 = _rms_stats(x1)
        dg_ref[...] += jnp.sum(dh * xh1, axis=0, keepdims=True)
        dx1_ref[...] = dx2 + _rms_bwd(dh, xh1, r1, g_ref[...])

    row = lambda w: pl.BlockSpec((tm, w), lambda i: (i, 0))
    shard_act = pl.BlockSpec((ns, tm, fs), lambda i: (0, i, 0))
    act_shape = jax.ShapeDtypeStruct((ns, t, fs), BF16)
    vec = pl.BlockSpec((1, D_MODEL), lambda i: (0, 0))
    return pl.pallas_call(
        body, grid=(t // tm,),
        in_specs=[row(D_MODEL), row(D_MODEL), row(D_MODEL), _resident((ns, fs, D_MODEL)),
                  _resident((ns, fs, D_MODEL)), _resident((ns, fs, D_MODEL)), _resident((1, D_MODEL)),
                  _resident((1, D_MODEL))],
        out_specs=[shard_act, shard_act, shard_act, row(D_MODEL), row(D_MODEL),
                   pl.BlockSpec((1, LANES), lambda i: (0, 0)), vec, vec],
        out_shape=[act_shape, act_shape, act_shape, jax.ShapeDtypeStruct((t, D_MODEL), F32),
                   jax.ShapeDtypeStruct((t, D_MODEL), BF16), jax.ShapeDtypeStruct((1, LANES), F32),
                   jax.ShapeDtypeStruct((1, D_MODEL), F32), jax.ShapeDtypeStruct((1, D_MODEL), F32)],
        scratch_shapes=[pltpu.VMEM((ns, tm, fs), F32), pltpu.VMEM((ns, tm, fs), F32)],
        compiler_params=_params("arbitrary"), name="ffn_fwd_bwd",
    )(h1, x1, target, w_gate, w_up, w_down, g_ffn, g_final)


def _ffn_bwd_weights_call(h1, dgate, dup, act, dx2):
    t = h1.shape[0]
    tm = _token_tile(t, 1024)
    ns, _, fs = dgate.shape

    def body(h_ref, dgate_ref, dup_ref, act_ref, dx2_ref, gg_ref, gu_ref, gd_ref):
        @pl.when(pl.program_id(1) == 0)
        def _():
            gg_ref[...] = jnp.zeros_like(gg_ref)
            gu_ref[...] = jnp.zeros_like(gu_ref)
            gd_ref[...] = jnp.zeros_like(gd_ref)

        h = h_ref[...]
        gg_ref[...] += _dot_tn(dgate_ref[...], h)
        gu_ref[...] += _dot_tn(dup_ref[...], h)
        gd_ref[...] += _dot_tn(act_ref[...], dx2_ref[...])

    row = lambda w: pl.BlockSpec((tm, w), lambda s, i: (i, 0))
    shard_act = pl.BlockSpec((None, tm, fs), lambda s, i: (s, i, 0))
    return pl.pallas_call(
        body, grid=(ns, t // tm),
        in_specs=[row(D_MODEL), shard_act, shard_act, shard_act, row(D_MODEL)],
        out_specs=[pl.BlockSpec((None, fs, D_MODEL), lambda s, i: (s, 0, 0))] * 3,
        out_shape=[jax.ShapeDtypeStruct((ns, fs, D_MODEL), F32)] * 3,
        compiler_params=_params("parallel", "arbitrary"), name="ffn_bwd_weights",
    )(h1, dgate, dup, act, dx2)


def _outproj_bwd_call(dx1, attn, u, w_pool, b_pool, pool_scale, w_out, exchange=None):
    t = dx1.shape[0]
    tm = _token_tile(t, 1024)

    def body(dx1_ref, a_ref, u_ref, halo_ref, wp_ref, bp_ref, ps_ref, wo_ref,
             dattn_ref, dmixed_ref, gwo_ref, gbo_ref, gwp_ref, gbp_ref, gps_ref, sc_ref, y_ref):
        i = pl.program_id(0)

        @pl.when(i == 0)
        def _():
            for ref in (gwo_ref, gbo_ref, gwp_ref, gbp_ref, gps_ref):
                ref[...] = jnp.zeros_like(ref)

        dx1 = dx1_ref[...]
        dx1b = dx1.astype(BF16)
        dcat = _dot_nt(dx1b, wo_ref[...])
        dattn_ref[...] = dcat[:, :ATTN_WIDTH].astype(BF16)
        for g, mixed in enumerate(_pool_mixed(sc_ref, u_ref, halo_ref, i, tm)):
            lanes = slice(LANES * g, LANES * (g + 1))
            mixed_b = mixed.astype(BF16)
            lin = _dot(mixed_b, wp_ref[g]) + bp_ref[:, lanes]
            y_ref[:, lanes] = (lin * ps_ref[:, lanes]).astype(BF16)
            dpool = dcat[:, ATTN_WIDTH + LANES * g:ATTN_WIDTH + LANES * (g + 1)]
            gps_ref[:, lanes] += jnp.sum(dpool * lin, axis=0, keepdims=True)
            dy = dpool * ps_ref[:, lanes]
            gbp_ref[:, lanes] += jnp.sum(dy, axis=0, keepdims=True)
            dy_b = dy.astype(BF16)
            gwp_ref[g] += _dot_tn(mixed_b, dy_b)
            dmixed_ref[:, lanes] = _dot_nt(dy_b, wp_ref[g]).astype(BF16)
        gwo_ref[0:ATTN_WIDTH, :] += _dot_tn(a_ref[...], dx1b)
        gwo_ref[ATTN_WIDTH:, :] += _dot_tn(y_ref[...], dx1b)
        gbo_ref[...] += jnp.sum(dx1, axis=0, keepdims=True)

    row = lambda w: pl.BlockSpec((tm, w), lambda i: (i, 0))
    const = lambda shape: pl.BlockSpec(shape, lambda i: (0,) * len(shape))
    return _compute_call(
        body, (dx1, attn, u, u, w_pool, b_pool, pool_scale, w_out), grid=(t // tm,),
        in_specs=[row(D_MODEL), row(ATTN_WIDTH), row(POOL_WIDTH), _halo_before(tm, POOL_WIDTH),
                  _resident((4, LANES, LANES)), _resident((1, POOL_WIDTH)), _resident((1, POOL_WIDTH)),
                  _resident((D_MODEL, D_MODEL))],
        out_specs=[row(ATTN_WIDTH), row(POOL_WIDTH), const((D_MODEL, D_MODEL)), const((1, D_MODEL)),
                   const((4, LANES, LANES)), const((1, POOL_WIDTH)), const((1, POOL_WIDTH))],
        out_shape=[jax.ShapeDtypeStruct((t, ATTN_WIDTH), BF16), jax.ShapeDtypeStruct((t, POOL_WIDTH), BF16),
                   jax.ShapeDtypeStruct((D_MODEL, D_MODEL), F32), jax.ShapeDtypeStruct((1, D_MODEL), F32),
                   jax.ShapeDtypeStruct((4, LANES, LANES), F32), jax.ShapeDtypeStruct((1, POOL_WIDTH), F32),
                   jax.ShapeDtypeStruct((1, POOL_WIDTH), F32)],
        scratch_shapes=[pltpu.VMEM((tm + POOL_HALO, POOL_WIDTH), F32), pltpu.VMEM((tm, POOL_WIDTH), BF16)],
        semantics=("arbitrary",), name="outproj_bwd", exchange=exchange)


def _attn_bwd_call(q, k, v, dout, sinks, cos_t, sin_t, bias, bias_first, exchange=None):
    t = q.shape[0]
    nb = t // BLOCK
    gb = _blocks_per_step(nb)
    qb = gb * BLOCK
    steps = nb // gb

    def body(sink_ref, q_ref, kc_ref, kp_ref, vc_ref, vp_ref, do_ref, cq_ref, sq_ref, ck_ref, sk_ref, b_ref, bf_ref,
             dq_ref, dk_ref, dv_ref, dsink_ref, newk_ref, newv_ref, carryk_ref, carryv_ref, headk_ref, headv_ref):
        n = pl.program_id(0)
        first = _first_half_mask((BLOCK, LANES))

        @pl.when(n == 0)
        def _():
            dsink_ref[...] = jnp.zeros_like(dsink_ref)

        @pl.when(n < steps)
        def _():
            head_row = lax.broadcasted_iota(jnp.int32, dsink_ref.shape, 0)
            lane2 = lax.broadcasted_iota(jnp.int32, (2 * BLOCK, LANES), 1)
            row_low = lax.broadcasted_iota(jnp.int32, (LANES, BLOCK), 0) < HEAD_DIM
            dsink = jnp.zeros(dsink_ref.shape, F32)
            for g in range(gb):
                r0 = g * BLOCK
                kband = _kv_band(kc_ref, kp_ref, g)
                vband = _kv_band(vc_ref, vp_ref, g)
                kband_t = kband.T
                bias_g = jnp.where(n == 0, bf_ref[...], b_ref[...]) if g == 0 else b_ref[...]
                cq, sq = cq_ref[r0:r0 + BLOCK, :], sq_ref[r0:r0 + BLOCK, :]
                dk_heads, dv_heads = [], []
                for kh in range(2):
                    k2 = _head_in_both_halves(kband, kh)
                    v2 = _head_in_both_halves(vband, kh)
                    kt_head = kband_t[HEAD_DIM * kh:HEAD_DIM * (kh + 1), :]
                    k2_t = jnp.concatenate([kt_head, kt_head], axis=0).astype(BF16)
                    qst = _stack_heads(q_ref, kh, r0)
                    dost = _stack_heads(do_ref, kh, r0)
                    s_t = _dot_nt(k2, qst) + bias_g
                    sink = _sink_row(sink_ref, kh)
                    m = jnp.maximum(jnp.max(s_t, axis=0, keepdims=True), sink)
                    e = jnp.exp(s_t - m)
                    e_sink = jnp.exp(sink - m)
                    inv = 1.0 / (jnp.sum(e, axis=0, keepdims=True) + e_sink)
                    p_t = e * inv
                    dp_t = _dot_nt(v2, dost)
                    delta = jnp.sum(p_t * dp_t, axis=0, keepdims=True)
                    ds_t = (p_t * (dp_t - delta)).astype(BF16)
                    sink_term = e_sink * inv * delta
                    for h in range(4):
                        val = -jnp.sum(sink_term[:, BLOCK * h:BLOCK * (h + 1)])
                        dsink = dsink + jnp.where(head_row == 4 * kh + h, val, 0.0)
                    dq_t = _dot(k2_t, ds_t)
                    for jj in range(2):
                        slab = jnp.where(row_low, dq_t[:, BLOCK * 2 * jj:BLOCK * (2 * jj + 1)],
                                         dq_t[:, BLOCK * (2 * jj + 1):BLOCK * (2 * jj + 2)]).T
                        dq_ref[r0:r0 + BLOCK, LANES * (2 * kh + jj):LANES * (2 * kh + jj + 1)] = (
                            (slab * cq + _rope_partner(slab * sq, first)) * Q_SCALE).astype(BF16)
                    dk2 = _dot(ds_t, qst)
                    dv2 = _dot(p_t.astype(BF16), dost)
                    dk_heads.append(dk2 + pltpu.roll(dk2, HEAD_DIM, 1))
                    dv_heads.append(dv2 + pltpu.roll(dv2, HEAD_DIM, 1))
                bandk = jnp.where(lane2 < HEAD_DIM, dk_heads[0], dk_heads[1])
                bandv = jnp.where(lane2 < HEAD_DIM, dv_heads[0], dv_heads[1])
                if g == 0:
                    headk_ref[...] = bandk[0:BLOCK]
                    headv_ref[...] = bandv[0:BLOCK]
                else:
                    newk_ref[r0 - BLOCK:r0, :] += bandk[0:BLOCK]
                    newv_ref[r0 - BLOCK:r0, :] += bandv[0:BLOCK]
                newk_ref[r0:r0 + BLOCK, :] = bandk[BLOCK:]
                newv_ref[r0:r0 + BLOCK, :] = bandv[BLOCK:]
            dsink_ref[...] += dsink

        @pl.when(n == steps)
        def _():
            headk_ref[...] = jnp.zeros_like(headk_ref)
            headv_ref[...] = jnp.zeros_like(headv_ref)

        @pl.when(n >= 1)
        def _():
            last = qb - BLOCK
            carryk_ref[last:, :] += headk_ref[...]
            carryv_ref[last:, :] += headv_ref[...]
            for g in range(gb):
                rows = slice(g * BLOCK, (g + 1) * BLOCK)
                dk = carryk_ref[rows, :]
                dk_ref[rows, :] = (dk * ck_ref[rows, :] + _rope_partner(dk * sk_ref[rows, :], first)).astype(BF16)
            dv_ref[...] = carryv_ref[...].astype(BF16)

        @pl.when(n < steps)
        def _():
            carryk_ref[...] = newk_ref[...]
            carryv_ref[...] = newv_ref[...]

    cur = lambda w: pl.BlockSpec((qb, w), lambda n: (jnp.minimum(n, steps - 1), 0))
    late = lambda w: pl.BlockSpec((qb, w), lambda n: (jnp.maximum(n - 1, 0), 0))
    prev = lambda w: pl.BlockSpec((BLOCK, w), lambda n: (jnp.maximum(jnp.minimum(n, steps - 1) * gb - 1, 0), 0))
    return _compute_call(
        body, (sinks, q, k, k, v, v, dout, cos_t, sin_t, cos_t, sin_t, bias, bias_first), grid=(steps + 1,),
        in_specs=[pl.BlockSpec(memory_space=pltpu.SMEM), cur(ATTN_WIDTH), cur(KV_WIDTH), prev(KV_WIDTH),
                  cur(KV_WIDTH), prev(KV_WIDTH), cur(ATTN_WIDTH), cur(LANES), cur(LANES), late(LANES), late(LANES),
                  _resident(bias.shape), _resident(bias.shape)],
        out_specs=[cur(ATTN_WIDTH), late(KV_WIDTH), late(KV_WIDTH), pl.BlockSpec((8, LANES), lambda n: (0, 0))],
        out_shape=[jax.ShapeDtypeStruct((t, ATTN_WIDTH), BF16), jax.ShapeDtypeStruct((t, KV_WIDTH), BF16),
                   jax.ShapeDtypeStruct((t, KV_WIDTH), BF16), jax.ShapeDtypeStruct((8, LANES), F32)],
        scratch_shapes=[pltpu.VMEM((qb, LANES), F32), pltpu.VMEM((qb, LANES), F32),
                        pltpu.VMEM((qb, LANES), F32), pltpu.VMEM((qb, LANES), F32),
                        pltpu.VMEM((BLOCK, LANES), F32), pltpu.VMEM((BLOCK, LANES), F32)],
        semantics=("arbitrary",), name="attn_bwd", exchange=exchange)


def _inproj_bwd_call(dq, dk, dv, dmixed, x, dx1, w_in, g_mix, exchange=None):
    t = x.shape[0]
    tm = _token_tile(t, 1024)
    n_tiles = t // tm

    def body(dq_ref, dk_ref, dv_ref, dm_ref, halo_ref, x_ref, dx1_ref, w_ref, g_ref,
             dx_ref, gw_ref, gb_ref, gg_ref, sc_ref):
        i = pl.program_id(0)

        @pl.when(i == 0)
        def _():
            for ref in (gw_ref, gb_ref, gg_ref):
                ref[...] = jnp.zeros_like(ref)

        tok = i * tm + lax.broadcasted_iota(jnp.int32, (tm, 1), 0)
        halo = jnp.where(i == n_tiles - 1, 0.0, halo_ref[...].astype(F32))
        du = []
        for g, size in enumerate(POOL_SIZES):
            lanes = slice(LANES * g, LANES * (g + 1))
            dm = dm_ref[:, lanes].astype(F32)
            sc_ref[0:tm, lanes] = dm / jnp.minimum(tok + 1, size).astype(F32)
            sc_ref[tm:, lanes] = halo[:, lanes] / float(size)
            acc = -dm
            for j in range(size):
                acc = acc + sc_ref[pl.ds(j, tm), lanes]
            du.append(acc)
        dz32 = jnp.concatenate([dq_ref[...].astype(F32), dk_ref[...].astype(F32), dv_ref[...].astype(F32)] + du,
                               axis=1)
        dz = dz32.astype(BF16)
        gb_ref[...] += jnp.sum(dz32, axis=0, keepdims=True)
        xh, r = _rms_stats(x_ref[...])
        g = g_ref[...]
        gw_ref[...] += _dot_tn(dz, (xh * g).astype(BF16))
        dh = _dot(dz, w_ref[...])
        gg_ref[...] += jnp.sum(dh * xh, axis=0, keepdims=True)
        dx_ref[...] = dx1_ref[...] + _rms_bwd(dh, xh, r, g)

    row = lambda w: pl.BlockSpec((tm, w), lambda i: (i, 0))
    const = lambda shape: pl.BlockSpec(shape, lambda i: (0,) * len(shape))
    last_halo = t // POOL_HALO - 1
    halo_after = pl.BlockSpec((POOL_HALO, POOL_WIDTH),
                              lambda i: (jnp.minimum((i + 1) * (tm // POOL_HALO), last_halo), 0))
    return _compute_call(
        body, (dq, dk, dv, dmixed, dmixed, x, dx1, w_in, g_mix), grid=(n_tiles,),
        in_specs=[row(ATTN_WIDTH), row(KV_WIDTH), row(KV_WIDTH), row(POOL_WIDTH), halo_after, row(D_MODEL),
                  row(D_MODEL), _resident((IN_WIDTH, D_MODEL)), _resident((1, D_MODEL))],
        out_specs=[row(D_MODEL), const((IN_WIDTH, D_MODEL)), const((1, IN_WIDTH)), const((1, D_MODEL))],
        out_shape=[jax.ShapeDtypeStruct((t, D_MODEL), F32), jax.ShapeDtypeStruct((IN_WIDTH, D_MODEL), F32),
                   jax.ShapeDtypeStruct((1, IN_WIDTH), F32), jax.ShapeDtypeStruct((1, D_MODEL), F32)],
        scratch_shapes=[pltpu.VMEM((tm + POOL_HALO, POOL_WIDTH), F32)],
        semantics=("arbitrary",), name="inproj_bwd", exchange=exchange)


def _reduce_to_owner(grads, place, hosts, wire_dtypes):
    me, c = place
    landed = hosts[0](_sibling_exchange(grads))
    partials = _pair_sum_call(grads, landed, c, wire_dtypes)
    slots = hosts[1](_chip_exchange(partials))
    return _chip_sum_call(partials, slots, me, c)


def _local_step(x, target, small, w_in, w_out, w_ffn, place=None):
    t = x.shape[0]
    on_mesh = place is not None
    cos_t, sin_t = _rope_tables(t)
    w_pool = small["w_pool"].astype(BF16)
    w_ffn = list(w_ffn)

    def gathering(bufs):
        return _gather_exchange(bufs) if on_mesh else None

    (h0,), got = _norm_call(x, small["g_mix"], exchange=gathering([w_in]))
    if on_mesh:
        w_in = got[0].reshape(IN_WIDTH, D_MODEL)
    (q, k, v, u), got = _inproj_call(h0, w_in, small["b_in"], cos_t, sin_t, exchange=gathering([w_out]))
    if on_mesh:
        w_out = got[0].reshape(D_MODEL, D_MODEL)
    bias_t, bias_first_t = _band_biases()
    (attn,), got = _attn_fwd_call(q, k, v, small["sinks"], bias_t, bias_first_t, exchange=gathering(w_ffn[:2]))
    if on_mesh:
        w_ffn[:2] = got
    (x1, h1), got = _outproj_call(attn, u, x, w_pool, small["b_pool"], small["pool_scale"], w_out,
                                          small["b_out"], small["g_ffn"], exchange=gathering(w_ffn[2:]))
    if on_mesh:
        w_ffn[2:] = got
    w_gate, w_up, w_down = w_ffn
    act, dgate, dup, dx1, dx2b, loss, g_g_final, g_g_ffn = _ffn_call(h1, x1, target, w_gate, w_up, w_down,
                                                                      small["g_ffn"], small["g_final"])
    ffn_grads = list(_ffn_bwd_weights_call(h1, dgate, dup, act, dx2b))
    kept = {}

    def outproj_bwd(exchange=None):
        kept["outproj"], results = _outproj_bwd_call(dx1, attn, u, w_pool, small["b_pool"], small["pool_scale"],
                                                     w_out, exchange=exchange)
        return results

    def attn_bwd(exchange=None):
        kept["attn"], results = _attn_bwd_call(q, k, v, kept["outproj"][0], small["sinks"], cos_t, sin_t, bias_t,
                                               bias_first_t, exchange=exchange)
        return results

    def inproj_bwd(exchange=None):
        dq, dk, dv, _ = kept["attn"]
        kept["inproj"], results = _inproj_bwd_call(dq, dk, dv, kept["outproj"][1], x, dx1, w_in, small["g_mix"],
                                                   exchange=exchange)
        return results

    if on_mesh:
        ffn_grads = _reduce_to_owner(ffn_grads, place, (outproj_bwd, attn_bwd), [BF16] * 3)
    else:
        outproj_bwd(), attn_bwd()
    inproj_bwd()
    _, _, g_w_out, g_b_out, g_w_pool, g_b_pool, g_pool_scale = kept["outproj"]
    g_sinks = kept["attn"][3]
    grad_x, g_w_in, g_b_in, g_g_mix = kept["inproj"]
    small_grads = dict(g_mix=g_g_mix, b_in=g_b_in, sinks=g_sinks[:, 0], w_pool=g_w_pool,
                       b_pool=g_b_pool, pool_scale=g_pool_scale, b_out=g_b_out, g_ffn=g_g_ffn, g_final=g_g_final)
    big_grads = dict(w_in=g_w_in, w_out=g_w_out, w_gate=ffn_grads[0], w_up=ffn_grads[1], w_down=ffn_grads[2])
    return loss, grad_x, small_grads, big_grads


def _cast_place_call(shards, me):
    na = len(shards)
    split = 2

    def body(me_ref, *refs):
        for a in range(na):
            refs[na + a][...] = refs[a][...].astype(BF16)

    in_specs = [pl.BlockSpec((s.shape[0] // split, s.shape[1]), lambda i, me_ref: (i, 0)) for s in shards]
    out_specs = [pl.BlockSpec((None, s.shape[0] // split, s.shape[1]), lambda i, me_ref: (me_ref[0], i, 0))
                 for s in shards]
    return pl.pallas_call(
        body,
        grid_spec=pltpu.PrefetchScalarGridSpec(num_scalar_prefetch=1, grid=(split,), in_specs=in_specs,
                                               out_specs=out_specs),
        out_shape=[jax.ShapeDtypeStruct((N_SHARDS,) + s.shape, BF16) for s in shards],
        compiler_params=_params("parallel"), name="cast_place_weights",
    )(me.reshape(1), *shards)


def _pair_sum_call(grads, landed, c, wire_dtypes):
    na = len(grads)
    split = 2

    def body(c_ref, *refs):
        ins, lands, outs = refs[:na], refs[na:2 * na], refs[2 * na:]
        for a in range(na):
            outs[a][...] = (ins[a][...] + lands[a][...]).astype(outs[a].dtype)

    in_specs, land_specs, out_specs, views = [], [], [], []
    for g in grads:
        ns, r, cols = g.shape
        rows = r // 2 // split
        views.append(g.reshape(ns, 2, r // 2, cols))
        in_specs.append(pl.BlockSpec((None, None, rows, cols), lambda s, i, c_ref: (s, c_ref[0], i, 0)))
        land_specs.append(pl.BlockSpec((None, rows, cols), lambda s, i, c_ref: (s, i, 0)))
        out_specs.append(pl.BlockSpec((None, rows, cols), lambda s, i, c_ref: (s, i, 0)))
    return pl.pallas_call(
        body,
        grid_spec=pltpu.PrefetchScalarGridSpec(num_scalar_prefetch=1, grid=(N_SHARDS, split),
                                               in_specs=in_specs + land_specs, out_specs=out_specs),
        out_shape=[jax.ShapeDtypeStruct(l.shape, dt) for l, dt in zip(landed, wire_dtypes)],
        compiler_params=_params("parallel", "parallel"), name="reduce_pair_sum",
    )(c.reshape(1), *views, *landed)


def _chip_sum_call(partials, slots, me, c):
    na = len(slots)
    split = 2

    def body(place_ref, *refs):
        mine, lands, outs = refs[:na], refs[na:2 * na], refs[2 * na:]
        for j in range(N_SHARDS):
            @pl.when(place_ref[0] == j)
            def _():
                for a in range(na):
                    terms = [mine[a][...] if k == j else lands[a][k] for k in range(N_SHARDS)]
                    t0, t1, t2, t3 = (term.astype(F32) for term in terms)
                    outs[a][...] = ((t0 + t1) + t2) + t3

    mine_specs, land_specs, out_specs = [], [], []
    for s in slots:
        ns, h, cols = s.shape
        mine_specs.append(pl.BlockSpec((None, h // split, cols), lambda i, place: (place[0], i, 0)))
        land_specs.append(pl.BlockSpec((ns, h // split, cols), lambda i, place: (0, i, 0)))
        out_specs.append(pl.BlockSpec((h // split, cols), lambda i, place: (place[1] * split + i, 0)))
    return pl.pallas_call(
        body,
        grid_spec=pltpu.PrefetchScalarGridSpec(num_scalar_prefetch=1, grid=(split,),
                                               in_specs=mine_specs + land_specs, out_specs=out_specs),
        out_shape=[jax.ShapeDtypeStruct((2 * s.shape[1], s.shape[2]), F32) for s in slots],
        compiler_params=_params("parallel"), name="reduce_chip_sum",
    )(jnp.stack([me, c]), *partials, *slots)


def _adamw(w, g, m, v):
    m = ADAM_B1 * m + (1.0 - ADAM_B1) * g
    v = ADAM_B2 * v + (1.0 - ADAM_B2) * jnp.square(g)
    m_hat = m / (1.0 - ADAM_B1 ** ADAM_STEP)
    v_hat = v / (1.0 - ADAM_B2 ** ADAM_STEP)
    return -ADAM_LR * (m_hat / (jnp.sqrt(v_hat) + ADAM_EPS) + ADAM_WD * w), m, v


def _adamw_call(ws, gs, ms, vs):
    na = len(ws)
    split = 4

    def body(*refs):
        w_refs, g_refs, m_refs, v_refs = (refs[k * na:(k + 1) * na] for k in range(4))
        d_refs, nm_refs, nv_refs = (refs[(4 + k) * na:(5 + k) * na] for k in range(3))
        for a in range(na):
            d_refs[a][...], nm_refs[a][...], nv_refs[a][...] = _adamw(
                w_refs[a][...], g_refs[a][...], m_refs[a][...], v_refs[a][...])

    specs = [pl.BlockSpec((w.shape[0] // split, w.shape[1]), lambda i: (i, 0)) for w in ws]
    shapes = [jax.ShapeDtypeStruct(w.shape, F32) for w in ws]
    outs = pl.pallas_call(
        body, grid=(split,), in_specs=specs * 4, out_specs=specs * 3, out_shape=shapes * 3,
        compiler_params=_params("parallel"), name="adamw",
    )(*ws, *gs, *ms, *vs)
    return outs[:na], outs[na:2 * na], outs[2 * na:]


def _small_rows():
    first, row = {}, 0
    for name, (r, c) in SMALL_VIEWS.items():
        first[name] = row
        row += r * (-(-c // LANES))
    return first, row, -(-(row + 1) // 64) * 64


def _pack_small(arrays, loss):
    _, used, rows = _small_rows()
    parts = []
    for name, (r, c) in SMALL_VIEWS.items():
        a = arrays[name].reshape(r, c)
        parts.append(jnp.pad(a, ((0, 0), (0, LANES - c))) if c < LANES else a.reshape(r * c // LANES, LANES))
    parts += [loss, jnp.zeros((rows - used - 1, LANES), F32)]
    return jnp.concatenate(parts, axis=0)


def _adamw_small_call(ws, packed_grads, ms, vs):
    names = list(SMALL_VIEWS)
    first, _, _ = _small_rows()
    n = len(names)

    def body(g_ref, *refs):
        w_refs, m_refs, v_refs = refs[:n], refs[n:2 * n], refs[2 * n:3 * n]
        outs = refs[3 * n:]
        for a, name in enumerate(names):
            r, c = SMALL_VIEWS[name]
            r0 = first[name]
            if c == LANES:
                g = g_ref[r0:r0 + r, :]
            elif c < LANES:
                g = g_ref[r0:r0 + 1, 0:c]
            else:
                g = jnp.concatenate([g_ref[r0 + j:r0 + j + 1, :] for j in range(c // LANES)], axis=1)
            delta, m, v = _adamw(w_refs[a][...], g, m_refs[a][...], v_refs[a][...])
            for k, val in enumerate((g, delta, m, v)):
                outs[k * n + a][...] = val

    shapes = [jax.ShapeDtypeStruct(SMALL_VIEWS[name], F32) for name in names]
    outs = pl.pallas_call(
        body, out_shape=shapes * 4, compiler_params=pltpu.CompilerParams(vmem_limit_bytes=VMEM_LIMIT),
        name="adamw_small",
    )(packed_grads, *[ws[k] for k in names], *[ms[k] for k in names], *[vs[k] for k in names])
    return [dict(zip(names, outs[k * n:(k + 1) * n])) for k in range(4)]


def kernel(x, g_mix, w_in, b_in, sinks, w_pool, b_pool, pool_scale, w_out, b_out, g_ffn, w_gate, w_up, w_down, g_final, loss_target, m_g_mix, m_w_in, m_b_in, m_sinks, m_w_pool, m_b_pool, m_pool_scale, m_w_out, m_b_out, m_g_ffn, m_w_gate, m_w_up, m_w_down, m_g_final, v_g_mix, v_w_in, v_b_in, v_sinks, v_w_pool, v_b_pool, v_pool_scale, v_w_out, v_b_out, v_g_ffn, v_w_gate, v_w_up, v_w_down, v_g_final):
    given = dict(locals())
    big_names = ("w_in", "w_out", "w_gate", "w_up", "w_down")
    out_shapes = {n: given[n].shape for n in SMALL_NAMES + big_names}
    c = lax.axis_index("c")

    me = 2 * lax.axis_index("x") + lax.axis_index("y")
    transposed = ("w_in", "w_gate", "w_up")

    def shard_view(a, name):
        return a[0].T if name in transposed else a[0]

    placed = _cast_place_call([shard_view(given[n], n) for n in big_names], me)

    small = dict(g_mix=g_mix, b_in=b_in, sinks=sinks, w_pool=w_pool[0], b_pool=b_pool.reshape(1, POOL_WIDTH),
                 pool_scale=pool_scale.reshape(1, POOL_WIDTH), b_out=b_out, g_ffn=g_ffn,
                 g_final=g_final.reshape(1, D_MODEL))
    loss, grad_x, small_grads, big_grads = _local_step(
        x[0], loss_target[0], small, placed[0], placed[1], placed[2:], place=(me, c))

    packed = _pack_small(small_grads, loss)
    grads = [
        big_grads["w_in"].reshape(N_SHARDS, IN_WIDTH // N_SHARDS, D_MODEL),
        big_grads["w_out"].reshape(N_SHARDS, D_MODEL // N_SHARDS, D_MODEL),
        jnp.broadcast_to(packed[None], (N_SHARDS,) + packed.shape),
    ]
    alone = [functools.partial(_exchange_call, name=name) for name in
             ("reduce_sibling_exchange", "reduce_chip_exchange")]
    h_in, h_out, h_small = _reduce_to_owner(grads, (me, c), alone, [BF16, BF16, F32])
    totals = _exchange_call(_sibling_allgather(
        [h_in, h_out, big_grads["w_gate"], big_grads["w_up"], big_grads["w_down"], h_small]), "reduce_sibling_allgather")
    loss = totals[-1][_small_rows()[1], 0]

    def owned(prefix):
        return [shard_view(given[prefix + n], n) for n in big_names]

    def owned_small(prefix):
        return {n: given[prefix + n].reshape(SMALL_VIEWS[n]) for n in SMALL_NAMES}

    big = (totals[:-1],) + _adamw_call(owned(""), totals[:-1], owned("m_"), owned("v_"))
    small_groups = _adamw_small_call(owned_small(""), totals[-1], owned_small("m_"), owned_small("v_"))

    order = ("g_mix", "w_in", "b_in", "sinks", "w_pool", "b_pool", "pool_scale", "w_out", "b_out", "g_ffn",
             "w_gate", "w_up", "w_down", "g_final")
    results = [loss, grad_x.reshape(x.shape)]
    for group, small_group in zip(big, small_groups):
        named = {n: (a.T if n in transposed else a) for n, a in zip(big_names, group)}
        named.update(small_group)
        results += [named[n].reshape(out_shapes[n]) for n in order]
    return tuple(results)
```

```python
import functools
from typing import Any, Callable, Mapping, NamedTuple, Sequence

import jax
import jax.numpy as jnp
from jax import lax
from jax.experimental import pallas as pl
from jax.experimental.pallas import tpu as pltpu

F32 = jnp.float32
BF16 = jnp.bfloat16

D_MODEL = 1024
ATTN_WIDTH = 512
KV_WIDTH = 128
POOL_WIDTH = 512
IN_WIDTH = ATTN_WIDTH + 2 * KV_WIDTH + POOL_WIDTH
HEAD_DIM = 64
N_Q_HEADS = 8
BLOCK = 128
POOL_SIZES = (2, 4, 8, 16)
POOL_HALO = 16
ROPE_THETA = 10000.0
RMS_EPS = 1e-5
Q_SCALE = HEAD_DIM ** -0.5

ADAM_LR = 0.001
ADAM_B1 = 0.9
ADAM_B2 = 0.999
ADAM_EPS = 1e-08
ADAM_WD = 0.01
ADAM_STEP = 10

N_SHARDS = 4
LANES = 128
VMEM_LIMIT = 56 * 1024 * 1024
FFN_TOKEN_TILE = 256

MESH = pl.DeviceIdType.MESH
HBM_SPEC = pl.BlockSpec(memory_space=pltpu.HBM)

SMALL_VIEWS = dict(w_pool=(4 * LANES, LANES), b_pool=(4, LANES), pool_scale=(4, LANES), g_mix=(1, D_MODEL),
                   b_in=(1, IN_WIDTH), b_out=(1, D_MODEL), g_ffn=(1, D_MODEL), g_final=(1, D_MODEL),
                   sinks=(1, N_Q_HEADS))
SMALL_NAMES = tuple(SMALL_VIEWS)


def _dot(a, b):
    return jnp.dot(a, b, preferred_element_type=F32)


def _dot_nt(a, b):
    return lax.dot_general(a, b, (((1,), (1,)), ((), ())), preferred_element_type=F32)


def _dot_tn(a, b):
    return lax.dot_general(a, b, (((0,), (0,)), ((), ())), preferred_element_type=F32)


def _params(*semantics):
    return pltpu.CompilerParams(dimension_semantics=semantics, vmem_limit_bytes=VMEM_LIMIT)


def _resident(shape):
    return pl.BlockSpec(shape, lambda *_: (0,) * len(shape), pipeline_mode=pl.Buffered(1))


def _token_tile(t, largest=512):
    for tm in (1024, 512, 256, 128):
        if tm <= largest and t % tm == 0:
            return tm
    raise ValueError(f"sequence length {t} is not a multiple of 128")


def _rms_stats(xf):
    r = lax.rsqrt(jnp.mean(xf * xf, axis=-1, keepdims=True) + RMS_EPS)
    return xf * r, r


def _rms_bwd(dy, xh, r, g):
    dxh = dy * g
    return r * (dxh - xh * jnp.mean(dxh * xh, axis=-1, keepdims=True))


def _rope_partner(t, first):
    return jnp.where(first, pltpu.roll(t, LANES - HEAD_DIM // 2, 1), pltpu.roll(t, HEAD_DIM // 2, 1))


def _first_half_mask(shape):
    lane = lax.broadcasted_iota(jnp.int32, shape, 1)
    return (lane % HEAD_DIM) < (HEAD_DIM // 2)


def _rope_tables(t):
    inv_freq = 1.0 / (ROPE_THETA ** (jnp.arange(0, HEAD_DIM, 2, dtype=F32) / HEAD_DIM))
    ang = jnp.arange(t, dtype=F32)[:, None] * inv_freq[None, :]
    cos, sin = jnp.cos(ang), jnp.sin(ang)
    cos_t = jnp.tile(jnp.concatenate([cos, cos], axis=-1), (1, LANES // HEAD_DIM))
    sin_t = jnp.tile(jnp.concatenate([-sin, sin], axis=-1), (1, LANES // HEAD_DIM))
    return cos_t, sin_t


class _Exchange(NamedTuple):
    inputs: Sequence[Any]
    out_shapes: Sequence[Any]
    aliases: Mapping[int, int]
    sems: Sequence[Any]
    start: Callable[..., None]
    finish: Callable[..., None]


def _compute_call(body, operands, *, grid, in_specs, out_specs, out_shape, scratch_shapes=(), semantics, name,
                  exchange=None):
    n_in, n_out, n_scr = len(in_specs), len(out_specs), len(scratch_shapes)
    if exchange is None:
        return pl.pallas_call(body, grid=grid, in_specs=in_specs, out_specs=out_specs, out_shape=out_shape,
                              scratch_shapes=scratch_shapes, compiler_params=_params(*semantics),
                              name=name)(*operands), []
    k_in, k_out = len(exchange.inputs), len(exchange.out_shapes)

    def hosting(*refs):
        ins, rest = refs[:n_in], refs[n_in:]
        ex_ins, rest = rest[:k_in], rest[k_in:]
        outs, rest = rest[:n_out], rest[n_out:]
        ex_outs, rest = rest[:k_out], rest[k_out:]
        scratch, sems = rest[:n_scr], rest[n_scr:]
        steps = [pl.program_id(d) for d in range(len(grid))]
        is_first = functools.reduce(jnp.logical_and, [s == 0 for s in steps])
        is_last = functools.reduce(jnp.logical_and, [s == g - 1 for s, g in zip(steps, grid)])

        @pl.when(is_first)
        def _():
            exchange.start(ex_ins, ex_outs, sems)

        body(*ins, *outs, *scratch)

        @pl.when(is_last)
        def _():
            exchange.finish(ex_ins, ex_outs, sems)

    results = pl.pallas_call(
        hosting, grid=grid,
        in_specs=list(in_specs) + [HBM_SPEC] * k_in, out_specs=list(out_specs) + [HBM_SPEC] * k_out,
        out_shape=list(out_shape) + list(exchange.out_shapes),
        scratch_shapes=list(scratch_shapes) + list(exchange.sems),
        input_output_aliases={n_in + i: n_out + j for i, j in exchange.aliases.items()},
        compiler_params=_params(*semantics), name=name,
    )(*operands, *exchange.inputs)
    return results[:n_out], results[n_out:]


def _exchange_call(exchange, name):
    n_in, n_out = len(exchange.inputs), len(exchange.out_shapes)

    def body(*refs):
        ins, outs, sems = refs[:n_in], refs[n_in:n_in + n_out], refs[n_in + n_out:]
        exchange.start(ins, outs, sems)
        exchange.finish(ins, outs, sems)

    return pl.pallas_call(
        body, in_specs=[HBM_SPEC] * n_in, out_specs=[HBM_SPEC] * n_out, out_shape=list(exchange.out_shapes),
        input_output_aliases=dict(exchange.aliases), scratch_shapes=list(exchange.sems),
        compiler_params=pltpu.CompilerParams(has_side_effects=True), name=name,
    )(*exchange.inputs)


def _mesh_place():
    x, y, c = lax.axis_index("x"), lax.axis_index("y"), lax.axis_index("c")
    other_chips = [(1 - x, y), (x, 1 - y), (1 - x, 1 - y)]
    return x, y, c, other_chips


def _half(ref, c, rows):
    return ref.at[pl.ds(pl.multiple_of(c * rows, 16), rows), :]


def _remote(src, dst, send_sems, recv_sems, idx, to):
    return functools.partial(pltpu.make_async_remote_copy, src_ref=src, dst_ref=dst, send_sem=send_sems.at[idx],
                             recv_sem=recv_sems.at[idx], device_id=to, device_id_type=MESH)


def _gather_exchange(bufs):
    na = len(bufs)

    def copies(outs, sems):
        send_sems, recv_sems = sems
        x, y, c, other_chips = _mesh_place()
        me, sibling = 2 * x + y, (x, y, 1 - c)
        ici_out, ici_in, fwd_out, fwd_in = [], [], [], []
        for j, (cx, cy) in enumerate(other_chips):
            for a in range(na):
                rows = outs[a].shape[1] // 2
                mine = _half(outs[a].at[me], c, rows)
                landed = _half(outs[a].at[2 * cx + cy], c, rows)
                passed = _half(outs[a].at[2 * cx + cy], 1 - c, rows)
                ici_out.append(_remote(mine, mine, send_sems, recv_sems, j * na + a, (cx, cy, c)))
                ici_in.append(_remote(landed, landed, send_sems, recv_sems, j * na + a, (cx, cy, c)))
                fwd_out.append(_remote(landed, landed, send_sems, recv_sems, (3 + j) * na + a, sibling))
                fwd_in.append(_remote(passed, passed, send_sems, recv_sems, (3 + j) * na + a, sibling))
        return ici_out, ici_in, fwd_out, fwd_in

    def start(ins, outs, sems):
        for cp in copies(outs, sems)[0]:
            cp().start()

    def finish(ins, outs, sems):
        ici_out, ici_in, fwd_out, fwd_in = copies(outs, sems)
        for arrived, forward in zip(ici_in, fwd_out):
            arrived().wait_recv()
            forward().start()
        for cp in fwd_in:
            cp().wait_recv()
        for cp in ici_out + fwd_out:
            cp().wait_send()

    return _Exchange(inputs=bufs, out_shapes=[jax.ShapeDtypeStruct(b.shape, b.dtype) for b in bufs],
                     aliases={a: a for a in range(na)},
                     sems=[pltpu.SemaphoreType.DMA((6 * na,)), pltpu.SemaphoreType.DMA((6 * na,))],
                     start=start, finish=finish)


def _sibling_exchange(grads):
    na = len(grads)

    def copies(ins, outs, sems):
        x, y, c, _ = _mesh_place()
        out = []
        for a in range(na):
            rows = ins[a].shape[1] // 2
            src = ins[a].at[:, pl.ds(pl.multiple_of((1 - c) * rows, 8), rows), :]
            out.append(_remote(src, outs[a], sems[0], sems[1], a, (x, y, 1 - c)))
        return out

    def start(ins, outs, sems):
        for cp in copies(ins, outs, sems):
            cp().start()

    def finish(ins, outs, sems):
        for cp in copies(ins, outs, sems):
            cp().wait_recv()
            cp().wait_send()

    return _Exchange(inputs=grads,
                     out_shapes=[jax.ShapeDtypeStruct((g.shape[0], g.shape[1] // 2, g.shape[2]), g.dtype)
                                 for g in grads],
                     aliases={}, sems=[pltpu.SemaphoreType.DMA((na,)), pltpu.SemaphoreType.DMA((na,))],
                     start=start, finish=finish)


def _chip_exchange(partials):
    na = len(partials)

    def copies(ins, outs, sems):
        x, y, c, other_chips = _mesh_place()
        me = 2 * x + y
        sends, arrivals = [], []
        for j, (cx, cy) in enumerate(other_chips):
            for a in range(na):
                slot = outs[a].at[2 * cx + cy]
                sends.append(_remote(ins[a].at[2 * cx + cy], outs[a].at[me], sems[0], sems[1], j * na + a, (cx, cy, c)))
                arrivals.append(_remote(slot, slot, sems[0], sems[1], j * na + a, (cx, cy, c)))
        return sends, arrivals

    def start(ins, outs, sems):
        for cp in copies(ins, outs, sems)[0]:
            cp().start()

    def finish(ins, outs, sems):
        sends, arrivals = copies(ins, outs, sems)
        for cp in arrivals:
            cp().wait_recv()
        for cp in sends:
            cp().wait_send()

    return _Exchange(inputs=partials, out_shapes=[jax.ShapeDtypeStruct(p.shape, p.dtype) for p in partials],
                     aliases={}, sems=[pltpu.SemaphoreType.DMA((3 * na,)), pltpu.SemaphoreType.DMA((3 * na,))],
                     start=start, finish=finish)


def _sibling_allgather(bufs):
    na = len(bufs)

    def piece(outs, sems, a, half):
        x, y, c, _ = _mesh_place()
        rows = outs[a].shape[0] // 2
        rows_ref = outs[a].at[pl.ds(pl.multiple_of(half * rows, 8), rows), :]
        return _remote(rows_ref, rows_ref, sems[0], sems[1], a, (x, y, 1 - c))

    def start(ins, outs, sems):
        c = lax.axis_index("c")
        for a in range(na):
            piece(outs, sems, a, c)().start()

    def finish(ins, outs, sems):
        c = lax.axis_index("c")
        for a in range(na):
            piece(outs, sems, a, 1 - c)().wait_recv()
        for a in range(na):
            piece(outs, sems, a, c)().wait_send()

    return _Exchange(inputs=bufs, out_shapes=[jax.ShapeDtypeStruct(b.shape, b.dtype) for b in bufs],
                     aliases={a: a for a in range(na)},
                     sems=[pltpu.SemaphoreType.DMA((na,)), pltpu.SemaphoreType.DMA((na,))],
                     start=start, finish=finish)


def _norm_call(x, g_mix, exchange=None):
    t = x.shape[0]
    tm = _token_tile(t)

    def body(x_ref, g_ref, h_ref):
        xh, _ = _rms_stats(x_ref[...])
        h_ref[...] = (xh * g_ref[...]).astype(BF16)

    row = pl.BlockSpec((tm, D_MODEL), lambda i: (i, 0))
    return _compute_call(body, (x, g_mix), grid=(t // tm,), in_specs=[row, _resident((1, D_MODEL))],
                         out_specs=[row], out_shape=[jax.ShapeDtypeStruct((t, D_MODEL), BF16)],
                         semantics=("parallel",), name="norm_fwd", exchange=exchange)


def _inproj_call(h0, w_in, b_in, cos_t, sin_t, exchange=None):
    t = h0.shape[0]
    tm = _token_tile(t, 1024)

    def body(h_ref, w_ref, b_ref, c_ref, s_ref, q_ref, k_ref, v_ref, u_ref):
        z = _dot_nt(h_ref[...], w_ref[...]) + b_ref[...]
        cos, sin = c_ref[...], s_ref[...]
        first = _first_half_mask((tm, LANES))
        for j in range(ATTN_WIDTH // LANES):
            zj = z[:, LANES * j:LANES * (j + 1)]
            q_ref[:, LANES * j:LANES * (j + 1)] = ((zj * cos + _rope_partner(zj, first) * sin) * Q_SCALE).astype(BF16)
        zk = z[:, ATTN_WIDTH:ATTN_WIDTH + KV_WIDTH]
        k_ref[...] = (zk * cos + _rope_partner(zk, first) * sin).astype(BF16)
        v_ref[...] = z[:, ATTN_WIDTH + KV_WIDTH:ATTN_WIDTH + 2 * KV_WIDTH].astype(BF16)
        u_ref[...] = z[:, ATTN_WIDTH + 2 * KV_WIDTH:]

    row = lambda w: pl.BlockSpec((tm, w), lambda i: (i, 0))
    return _compute_call(
        body, (h0, w_in, b_in, cos_t, sin_t), grid=(t // tm,),
        in_specs=[row(D_MODEL), _resident((IN_WIDTH, D_MODEL)), _resident((1, IN_WIDTH)), row(LANES), row(LANES)],
        out_specs=[row(ATTN_WIDTH), row(KV_WIDTH), row(KV_WIDTH), row(POOL_WIDTH)],
        out_shape=[jax.ShapeDtypeStruct((t, ATTN_WIDTH), BF16),
                   jax.ShapeDtypeStruct((t, KV_WIDTH), BF16), jax.ShapeDtypeStruct((t, KV_WIDTH), BF16),
                   jax.ShapeDtypeStruct((t, POOL_WIDTH), F32)],
        semantics=("parallel",), name="inproj_fwd", exchange=exchange)


def _head_in_both_halves(band, kv_head):
    lane = lax.broadcasted_iota(jnp.int32, band.shape, 1)
    own = (lane < HEAD_DIM) if kv_head == 0 else (lane >= HEAD_DIM)
    return jnp.where(own, band, pltpu.roll(band, HEAD_DIM, 1)).astype(BF16)


def _stack_heads(ref, kv_head, row0):
    low = lax.broadcasted_iota(jnp.int32, (BLOCK, LANES), 1) < HEAD_DIM
    parts = []
    for jj in range(2):
        slab = ref[row0:row0 + BLOCK, LANES * (2 * kv_head + jj):LANES * (2 * kv_head + jj + 1)]
        zero = jnp.zeros_like(slab)
        parts += [jnp.where(low, slab, zero), jnp.where(low, zero, slab)]
    return jnp.concatenate(parts, axis=0)


def _band_biases():
    c = lax.broadcasted_iota(jnp.int32, (2 * BLOCK, 4 * BLOCK), 0)
    i = lax.broadcasted_iota(jnp.int32, (2 * BLOCK, 4 * BLOCK), 1) % BLOCK
    vis = (c > i) & (c <= i + BLOCK)
    return jnp.where(vis, 0.0, -jnp.inf).astype(F32), jnp.where(vis & (c >= BLOCK), 0.0, -jnp.inf).astype(F32)


def _blocks_per_step(nb):
    for g in (4, 2, 1):
        if nb % g == 0:
            return g


def _sink_row(sink_ref, kv_head):
    return jnp.concatenate([jnp.full((1, BLOCK), sink_ref[0, 4 * kv_head + g], F32) for g in range(4)], axis=1)


def _kv_band(cur_ref, prev_ref, g):
    r0 = g * BLOCK
    prev = prev_ref[...] if g == 0 else cur_ref[r0 - BLOCK:r0, :]
    return jnp.concatenate([prev, cur_ref[r0:r0 + BLOCK, :]], axis=0).astype(F32)


def _attn_fwd_call(q, k, v, sinks, bias, bias_first, exchange=None):
    t = q.shape[0]
    nb = t // BLOCK
    gb = _blocks_per_step(nb)
    qb = gb * BLOCK

    def body(sink_ref, q_ref, kc_ref, kp_ref, vc_ref, vp_ref, b_ref, bf_ref, o_ref, p_ref, psink_ref):
        n = pl.program_id(0)
        row_low = lax.broadcasted_iota(jnp.int32, (LANES, BLOCK), 0) < HEAD_DIM
        for g in range(gb):
            r0 = g * BLOCK
            kband = _kv_band(kc_ref, kp_ref, g)
            vband_t = _kv_band(vc_ref, vp_ref, g).T
            bias_g = jnp.where(n == 0, bf_ref[...], b_ref[...]) if g == 0 else b_ref[...]
            for kh in range(2):
                k2 = _head_in_both_halves(kband, kh)
                vt_head = vband_t[HEAD_DIM * kh:HEAD_DIM * (kh + 1), :]
                v2_t = jnp.concatenate([vt_head, vt_head], axis=0).astype(BF16)
                s_t = _dot_nt(k2, _stack_heads(q_ref, kh, r0)) + bias_g
                sink = _sink_row(sink_ref, kh)
                m = jnp.maximum(jnp.max(s_t, axis=0, keepdims=True), sink)
                e = jnp.exp(s_t - m)
                e_sink = jnp.exp(sink - m)
                inv = 1.0 / (jnp.sum(e, axis=0, keepdims=True) + e_sink)
                p_t = (e * inv).astype(BF16)
                p_ref[2 * g + kh] = p_t
                psink_ref[2 * g + kh] = e_sink * inv
                o_t = _dot(v2_t, p_t)
                for jj in range(2):
                    slab = jnp.where(row_low, o_t[:, BLOCK * 2 * jj:BLOCK * (2 * jj + 1)],
                                     o_t[:, BLOCK * (2 * jj + 1):BLOCK * (2 * jj + 2)]).T
                    o_ref[r0:r0 + BLOCK, LANES * (2 * kh + jj):LANES * (2 * kh + jj + 1)] = slab.astype(BF16)

    cur = lambda w: pl.BlockSpec((qb, w), lambda n: (n, 0))
    prev = lambda w: pl.BlockSpec((BLOCK, w), lambda n: (jnp.maximum(n * gb - 1, 0), 0))
    return _compute_call(
        body, (sinks, q, k, k, v, v, bias, bias_first), grid=(nb // gb,),
        in_specs=[pl.BlockSpec(memory_space=pltpu.SMEM), cur(ATTN_WIDTH), cur(KV_WIDTH), prev(KV_WIDTH),
                  cur(KV_WIDTH), prev(KV_WIDTH), _resident(bias.shape), _resident(bias.shape)],
        out_specs=[cur(ATTN_WIDTH), pl.BlockSpec((2 * gb, 2 * BLOCK, 4 * BLOCK), lambda n: (n, 0, 0)),
                   pl.BlockSpec((2 * gb, 1, 4 * BLOCK), lambda n: (n, 0, 0))],
        out_shape=[jax.ShapeDtypeStruct((t, ATTN_WIDTH), BF16),
                   jax.ShapeDtypeStruct((2 * nb, 2 * BLOCK, 4 * BLOCK), BF16),
                   jax.ShapeDtypeStruct((2 * nb, 1, 4 * BLOCK), F32)],
        semantics=("parallel",), name="attn_fwd", exchange=exchange)


def _pool_mixed(sc_ref, u_ref, halo_ref, i, tm):
    sc_ref[0:POOL_HALO, :] = jnp.where(i == 0, 0.0, halo_ref[...])
    sc_ref[POOL_HALO:, :] = u_ref[...]
    tok = i * tm + lax.broadcasted_iota(jnp.int32, (tm, 1), 0)
    mixed = []
    for g, size in enumerate(POOL_SIZES):
        lanes = slice(LANES * g, LANES * (g + 1))
        cur = sc_ref[pl.ds(POOL_HALO, tm), lanes]
        acc = cur
        for j in range(1, size):
            acc = acc + sc_ref[pl.ds(POOL_HALO - j, tm), lanes]
        count = jnp.minimum(tok + 1, size).astype(F32)
        mixed.append(acc / count - cur)
    return mixed


def _halo_before(tm, width):
    return pl.BlockSpec((POOL_HALO, width), lambda i: (jnp.maximum(i * (tm // POOL_HALO) - 1, 0), 0))


def _outproj_call(attn, u, x, w_pool, b_pool, pool_scale, w_out, b_out, g_ffn, exchange=None):
    t = x.shape[0]
    tm = _token_tile(t)

    def body(a_ref, u_ref, halo_ref, x_ref, wp_ref, bp_ref, ps_ref, wo_ref, bo_ref, g_ref,
             x1_ref, h1_ref, sc_ref, y_ref):
        i = pl.program_id(0)
        for g, mixed in enumerate(_pool_mixed(sc_ref, u_ref, halo_ref, i, tm)):
            lanes = slice(LANES * g, LANES * (g + 1))
            lin = _dot(mixed.astype(BF16), wp_ref[g]) + bp_ref[:, lanes]
            y_ref[:, lanes] = (lin * ps_ref[:, lanes]).astype(BF16)
        x1 = (x_ref[...] + _dot(a_ref[...], wo_ref[0:ATTN_WIDTH, :]) + _dot(y_ref[...], wo_ref[ATTN_WIDTH:, :])
              + bo_ref[...])
        x1_ref[...] = x1
        xh, _ = _rms_stats(x1)
        h1_ref[...] = (xh * g_ref[...]).astype(BF16)

    row = lambda w: pl.BlockSpec((tm, w), lambda i: (i, 0))
    return _compute_call(
        body, (attn, u, u, x, w_pool, b_pool, pool_scale, w_out, b_out, g_ffn), grid=(t // tm,),
        in_specs=[row(ATTN_WIDTH), row(POOL_WIDTH), _halo_before(tm, POOL_WIDTH), row(D_MODEL),
                  _resident((4, LANES, LANES)), _resident((1, POOL_WIDTH)), _resident((1, POOL_WIDTH)),
                  _resident((D_MODEL, D_MODEL)), _resident((1, D_MODEL)), _resident((1, D_MODEL))],
        out_specs=[row(D_MODEL), row(D_MODEL)],
        out_shape=[jax.ShapeDtypeStruct((t, D_MODEL), F32), jax.ShapeDtypeStruct((t, D_MODEL), BF16)],
        scratch_shapes=[pltpu.VMEM((tm + POOL_HALO, POOL_WIDTH), F32), pltpu.VMEM((tm, POOL_WIDTH), BF16)],
        semantics=("parallel",), name="outproj_fwd", exchange=exchange)


def _ffn_call(h1, x1, target, w_gate, w_up, w_down, g_ffn, g_final):
    t = x1.shape[0]
    tm = _token_tile(t, FFN_TOKEN_TILE)
    ns, fs, _ = w_gate.shape

    def body(h_ref, x1_ref, tgt_ref, wg_ref, wu_ref, wd_ref, g_ref, gf_ref,
             act_ref, dgate_ref, dup_ref, dx1_ref, dx2b_ref, loss_ref, dgf_ref, dg_ref, gate_ref, up_ref):
        @pl.when(pl.program_id(0) == 0)
        def _():
            loss_ref[...] = jnp.zeros_like(loss_ref)
            dgf_ref[...] = jnp.zeros_like(dgf_ref)
            dg_ref[...] = jnp.zeros_like(dg_ref)

        h = h_ref[...]
        x1 = x1_ref[...]
        x2 = x1
        for s in range(ns):
            gate = _dot_nt(h, wg_ref[s])
            up = _dot_nt(h, wu_ref[s])
            act = (gate * jax.nn.sigmoid(gate) * up).astype(BF16)
            gate_ref[s] = gate
            up_ref[s] = up
            act_ref[s] = act
            x2 = x2 + _dot(act, wd_ref[s])
        xh2, r2 = _rms_stats(x2)
        gf = gf_ref[...]
        err = xh2 * gf - tgt_ref[...]
        loss_ref[...] += 0.5 * jnp.sum(err * err) / D_MODEL
        dy = err / D_MODEL
        dgf_ref[...] += jnp.sum(dy * xh2, axis=0, keepdims=True)
        dx2 = _rms_bwd(dy, xh2, r2, gf)
        dx2b = dx2.astype(BF16)
        dx2b_ref[...] = dx2b
        dh = None
        for s in range(ns):
            dact = _dot_nt(dx2b, wd_ref[s])
            gate = gate_ref[s]
            sg = jax.nn.sigmoid(gate)
            dup = (dact * (gate * sg)).astype(BF16)
            dgate = (dact * up_ref[s] * (sg * (1.0 + gate * (1.0 - sg)))).astype(BF16)
            dgate_ref[s] = dgate
            dup_ref[s] = dup
            part = _dot(dgate, wg_ref[s]) + _dot(dup, wu_ref[s])
            dh = part if dh is None else dh + part
        xh1, r1 = _rms_stats(x1)
        dg_ref[...] += jnp.sum(dh * xh1, axis=0, keepdims=True)
        dx1_ref[...] = dx2 + _rms_bwd(dh, xh1, r1, g_ref[...])

    row = lambda w: pl.BlockSpec((tm, w), lambda i: (i, 0))
    shard_act = pl.BlockSpec((ns, tm, fs), lambda i: (0, i, 0))
    act_shape = jax.ShapeDtypeStruct((ns, t, fs), BF16)
    vec = pl.BlockSpec((1, D_MODEL), lambda i: (0, 0))
    return pl.pallas_call(
        body, grid=(t // tm,),
        in_specs=[row(D_MODEL), row(D_MODEL), row(D_MODEL), _resident((ns, fs, D_MODEL)),
                  _resident((ns, fs, D_MODEL)), _resident((ns, fs, D_MODEL)), _resident((1, D_MODEL)),
                  _resident((1, D_MODEL))],
        out_specs=[shard_act, shard_act, shard_act, row(D_MODEL), row(D_MODEL),
                   pl.BlockSpec((1, LANES), lambda i: (0, 0)), vec, vec],
        out_shape=[act_shape, act_shape, act_shape, jax.ShapeDtypeStruct((t, D_MODEL), F32),
                   jax.ShapeDtypeStruct((t, D_MODEL), BF16), jax.ShapeDtypeStruct((1, LANES), F32),
                   jax.ShapeDtypeStruct((1, D_MODEL), F32), jax.ShapeDtypeStruct((1, D_MODEL), F32)],
        scratch_shapes=[pltpu.VMEM((ns, tm, fs), F32), pltpu.VMEM((ns, tm, fs), F32)],
        compiler_params=_params("arbitrary"), name="ffn_fwd_bwd",
    )(h1, x1, target, w_gate, w_up, w_down, g_ffn, g_final)


def _ffn_bwd_weights_call(h1, dgate, dup, act, dx2):
    t = h1.shape[0]
    tm = _token_tile(t, 1024)
    ns, _, fs = dgate.shape

    def body(h_ref, dgate_ref, dup_ref, act_ref, dx2_ref, gg_ref, gu_ref, gd_ref):
        @pl.when(pl.program_id(1) == 0)
        def _():
            gg_ref[...] = jnp.zeros_like(gg_ref)
            gu_ref[...] = jnp.zeros_like(gu_ref)
            gd_ref[...] = jnp.zeros_like(gd_ref)

        h = h_ref[...]
        gg_ref[...] += _dot_tn(dgate_ref[...], h)
        gu_ref[...] += _dot_tn(dup_ref[...], h)
        gd_ref[...] += _dot_tn(act_ref[...], dx2_ref[...])

    row = lambda w: pl.BlockSpec((tm, w), lambda s, i: (i, 0))
    shard_act = pl.BlockSpec((None, tm, fs), lambda s, i: (s, i, 0))
    return pl.pallas_call(
        body, grid=(ns, t // tm),
        in_specs=[row(D_MODEL), shard_act, shard_act, shard_act, row(D_MODEL)],
        out_specs=[pl.BlockSpec((None, fs, D_MODEL), lambda s, i: (s, 0, 0))] * 3,
        out_shape=[jax.ShapeDtypeStruct((ns, fs, D_MODEL), F32)] * 3,
        compiler_params=_params("parallel", "arbitrary"), name="ffn_bwd_weights",
    )(h1, dgate, dup, act, dx2)


def _outproj_bwd_call(dx1, attn, u, w_pool, b_pool, pool_scale, w_out, exchange=None):
    t = dx1.shape[0]
    tm = _token_tile(t, 1024)

    def body(dx1_ref, a_ref, u_ref, halo_ref, wp_ref, bp_ref, ps_ref, wo_ref,
             dattn_ref, dmixed_ref, gwo_ref, gbo_ref, gwp_ref, gbp_ref, gps_ref, sc_ref, y_ref):
        i = pl.program_id(0)

        @pl.when(i == 0)
        def _():
            for ref in (gwo_ref, gbo_ref, gwp_ref, gbp_ref, gps_ref):
                ref[...] = jnp.zeros_like(ref)

        dx1 = dx1_ref[...]
        dx1b = dx1.astype(BF16)
        dcat = _dot_nt(dx1b, wo_ref[...])
        dattn_ref[...] = dcat[:, :ATTN_WIDTH].astype(BF16)
        for g, mixed in enumerate(_pool_mixed(sc_ref, u_ref, halo_ref, i, tm)):
            lanes = slice(LANES * g, LANES * (g + 1))
            mixed_b = mixed.astype(BF16)
            lin = _dot(mixed_b, wp_ref[g]) + bp_ref[:, lanes]
            y_ref[:, lanes] = (lin * ps_ref[:, lanes]).astype(BF16)
            dpool = dcat[:, ATTN_WIDTH + LANES * g:ATTN_WIDTH + LANES * (g + 1)]
            gps_ref[:, lanes] += jnp.sum(dpool * lin, axis=0, keepdims=True)
            dy = dpool * ps_ref[:, lanes]
            gbp_ref[:, lanes] += jnp.sum(dy, axis=0, keepdims=True)
            dy_b = dy.astype(BF16)
            gwp_ref[g] += _dot_tn(mixed_b, dy_b)
            dmixed_ref[:, lanes] = _dot_nt(dy_b, wp_ref[g])
        gwo_ref[0:ATTN_WIDTH, :] += _dot_tn(a_ref[...], dx1b)
        gwo_ref[ATTN_WIDTH:, :] += _dot_tn(y_ref[...], dx1b)
        gbo_ref[...] += jnp.sum(dx1, axis=0, keepdims=True)

    row = lambda w: pl.BlockSpec((tm, w), lambda i: (i, 0))
    const = lambda shape: pl.BlockSpec(shape, lambda i: (0,) * len(shape))
    return _compute_call(
        body, (dx1, attn, u, u, w_pool, b_pool, pool_scale, w_out), grid=(t // tm,),
        in_specs=[row(D_MODEL), row(ATTN_WIDTH), row(POOL_WIDTH), _halo_before(tm, POOL_WIDTH),
                  _resident((4, LANES, LANES)), _resident((1, POOL_WIDTH)), _resident((1, POOL_WIDTH)),
                  _resident((D_MODEL, D_MODEL))],
        out_specs=[row(ATTN_WIDTH), row(POOL_WIDTH), const((D_MODEL, D_MODEL)), const((1, D_MODEL)),
                   const((4, LANES, LANES)), const((1, POOL_WIDTH)), const((1, POOL_WIDTH))],
        out_shape=[jax.ShapeDtypeStruct((t, ATTN_WIDTH), BF16), jax.ShapeDtypeStruct((t, POOL_WIDTH), F32),
                   jax.ShapeDtypeStruct((D_MODEL, D_MODEL), F32), jax.ShapeDtypeStruct((1, D_MODEL), F32),
                   jax.ShapeDtypeStruct((4, LANES, LANES), F32), jax.ShapeDtypeStruct((1, POOL_WIDTH), F32),
                   jax.ShapeDtypeStruct((1, POOL_WIDTH), F32)],
        scratch_shapes=[pltpu.VMEM((tm + POOL_HALO, POOL_WIDTH), F32), pltpu.VMEM((tm, POOL_WIDTH), BF16)],
        semantics=("arbitrary",), name="outproj_bwd", exchange=exchange)


def _attn_bwd_call(q, k, v, dout, probs, p_sinks, cos_t, sin_t, exchange=None):
    t = q.shape[0]
    nb = t // BLOCK
    gb = _blocks_per_step(nb)
    qb = gb * BLOCK
    steps = nb // gb

    def body(q_ref, kc_ref, kp_ref, vc_ref, vp_ref, do_ref, p_ref, psink_ref, cq_ref, sq_ref, ck_ref, sk_ref,
             dq_ref, dk_ref, dv_ref, dsink_ref, newk_ref, newv_ref, carryk_ref, carryv_ref, headk_ref, headv_ref):
        n = pl.program_id(0)
        first = _first_half_mask((BLOCK, LANES))

        @pl.when(n == 0)
        def _():
            dsink_ref[...] = jnp.zeros_like(dsink_ref)

        @pl.when(n < steps)
        def _():
            head_row = lax.broadcasted_iota(jnp.int32, dsink_ref.shape, 0)
            lane2 = lax.broadcasted_iota(jnp.int32, (2 * BLOCK, LANES), 1)
            row_low = lax.broadcasted_iota(jnp.int32, (LANES, BLOCK), 0) < HEAD_DIM
            dsink = jnp.zeros(dsink_ref.shape, F32)
            for g in range(gb):
                r0 = g * BLOCK
                kband = _kv_band(kc_ref, kp_ref, g)
                vband = _kv_band(vc_ref, vp_ref, g)
                kband_t = kband.T
                cq, sq = cq_ref[r0:r0 + BLOCK, :], sq_ref[r0:r0 + BLOCK, :]
                dk_heads, dv_heads = [], []
                for kh in range(2):
                    v2 = _head_in_both_halves(vband, kh)
                    kt_head = kband_t[HEAD_DIM * kh:HEAD_DIM * (kh + 1), :]
                    k2_t = jnp.concatenate([kt_head, kt_head], axis=0).astype(BF16)
                    qst = _stack_heads(q_ref, kh, r0)
                    dost = _stack_heads(do_ref, kh, r0)
                    p_b = p_ref[2 * g + kh]
                    p_t = p_b.astype(F32)
                    dp_t = _dot_nt(v2, dost)
                    delta = jnp.sum(p_t * dp_t, axis=0, keepdims=True)
                    ds_t = (p_t * (dp_t - delta)).astype(BF16)
                    sink_term = psink_ref[2 * g + kh] * delta
                    for h in range(4):
                        val = -jnp.sum(sink_term[:, BLOCK * h:BLOCK * (h + 1)])
                        dsink = dsink + jnp.where(head_row == 4 * kh + h, val, 0.0)
                    dq_t = _dot(k2_t, ds_t)
                    for jj in range(2):
                        slab = jnp.where(row_low, dq_t[:, BLOCK * 2 * jj:BLOCK * (2 * jj + 1)],
                                         dq_t[:, BLOCK * (2 * jj + 1):BLOCK * (2 * jj + 2)]).T
                        dq_ref[r0:r0 + BLOCK, LANES * (2 * kh + jj):LANES * (2 * kh + jj + 1)] = (
                            (slab * cq + _rope_partner(slab * sq, first)) * Q_SCALE).astype(BF16)
                    dk2 = _dot(ds_t, qst)
                    dv2 = _dot(p_b, dost)
                    dk_heads.append(dk2 + pltpu.roll(dk2, HEAD_DIM, 1))
                    dv_heads.append(dv2 + pltpu.roll(dv2, HEAD_DIM, 1))
                bandk = jnp.where(lane2 < HEAD_DIM, dk_heads[0], dk_heads[1])
                bandv = jnp.where(lane2 < HEAD_DIM, dv_heads[0], dv_heads[1])
                if g == 0:
                    headk_ref[...] = bandk[0:BLOCK]
                    headv_ref[...] = bandv[0:BLOCK]
                else:
                    newk_ref[r0 - BLOCK:r0, :] += bandk[0:BLOCK]
                    newv_ref[r0 - BLOCK:r0, :] += bandv[0:BLOCK]
                newk_ref[r0:r0 + BLOCK, :] = bandk[BLOCK:]
                newv_ref[r0:r0 + BLOCK, :] = bandv[BLOCK:]
            dsink_ref[...] += dsink

        @pl.when(n == steps)
        def _():
            headk_ref[...] = jnp.zeros_like(headk_ref)
            headv_ref[...] = jnp.zeros_like(headv_ref)

        @pl.when(n >= 1)
        def _():
            last = qb - BLOCK
            carryk_ref[last:, :] += headk_ref[...]
            carryv_ref[last:, :] += headv_ref[...]
            for g in range(gb):
                rows = slice(g * BLOCK, (g + 1) * BLOCK)
                dk = carryk_ref[rows, :]
                dk_ref[rows, :] = (dk * ck_ref[rows, :] + _rope_partner(dk * sk_ref[rows, :], first)).astype(BF16)
            dv_ref[...] = carryv_ref[...].astype(BF16)

        @pl.when(n < steps)
        def _():
            carryk_ref[...] = newk_ref[...]
            carryv_ref[...] = newv_ref[...]

    cur = lambda w: pl.BlockSpec((qb, w), lambda n: (jnp.minimum(n, steps - 1), 0))
    late = lambda w: pl.BlockSpec((qb, w), lambda n: (jnp.maximum(n - 1, 0), 0))
    prev = lambda w: pl.BlockSpec((BLOCK, w), lambda n: (jnp.maximum(jnp.minimum(n, steps - 1) * gb - 1, 0), 0))
    return _compute_call(
        body, (q, k, k, v, v, dout, probs, p_sinks, cos_t, sin_t, cos_t, sin_t), grid=(steps + 1,),
        in_specs=[cur(ATTN_WIDTH), cur(KV_WIDTH), prev(KV_WIDTH), cur(KV_WIDTH), prev(KV_WIDTH), cur(ATTN_WIDTH),
                  pl.BlockSpec((2 * gb, 2 * BLOCK, 4 * BLOCK), lambda n: (jnp.minimum(n, steps - 1), 0, 0)),
                  pl.BlockSpec((2 * gb, 1, 4 * BLOCK), lambda n: (jnp.minimum(n, steps - 1), 0, 0)),
                  cur(LANES), cur(LANES), late(LANES), late(LANES)],
        out_specs=[cur(ATTN_WIDTH), late(KV_WIDTH), late(KV_WIDTH), pl.BlockSpec((8, LANES), lambda n: (0, 0))],
        out_shape=[jax.ShapeDtypeStruct((t, ATTN_WIDTH), BF16), jax.ShapeDtypeStruct((t, KV_WIDTH), BF16),
                   jax.ShapeDtypeStruct((t, KV_WIDTH), BF16), jax.ShapeDtypeStruct((8, LANES), F32)],
        scratch_shapes=[pltpu.VMEM((qb, LANES), F32), pltpu.VMEM((qb, LANES), F32),
                        pltpu.VMEM((qb, LANES), F32), pltpu.VMEM((qb, LANES), F32),
                        pltpu.VMEM((BLOCK, LANES), F32), pltpu.VMEM((BLOCK, LANES), F32)],
        semantics=("arbitrary",), name="attn_bwd", exchange=exchange)


def _inproj_bwd_call(dq, dk, dv, dmixed, x, dx1, w_in, g_mix, exchange=None):
    t = x.shape[0]
    tm = _token_tile(t, 1024)
    n_tiles = t // tm

    def body(dq_ref, dk_ref, dv_ref, dm_ref, halo_ref, x_ref, dx1_ref, w_ref, g_ref,
             dx_ref, gw_ref, gb_ref, gg_ref, sc_ref):
        i = pl.program_id(0)

        @pl.when(i == 0)
        def _():
            for ref in (gw_ref, gb_ref, gg_ref):
                ref[...] = jnp.zeros_like(ref)

        tok = i * tm + lax.broadcasted_iota(jnp.int32, (tm, 1), 0)
        halo = jnp.where(i == n_tiles - 1, 0.0, halo_ref[...])
        du = []
        for g, size in enumerate(POOL_SIZES):
            lanes = slice(LANES * g, LANES * (g + 1))
            dm = dm_ref[:, lanes]
            sc_ref[0:tm, lanes] = dm / jnp.minimum(tok + 1, size).astype(F32)
            sc_ref[tm:, lanes] = halo[:, lanes] / float(size)
            acc = -dm
            for j in range(size):
                acc = acc + sc_ref[pl.ds(j, tm), lanes]
            du.append(acc)
        dz32 = jnp.concatenate([dq_ref[...].astype(F32), dk_ref[...].astype(F32), dv_ref[...].astype(F32)] + du,
                               axis=1)
        dz = dz32.astype(BF16)
        gb_ref[...] += jnp.sum(dz32, axis=0, keepdims=True)
        xh, r = _rms_stats(x_ref[...])
        g = g_ref[...]
        gw_ref[...] += _dot_tn(dz, (xh * g).astype(BF16))
        dh = _dot(dz, w_ref[...])
        gg_ref[...] += jnp.sum(dh * xh, axis=0, keepdims=True)
        dx_ref[...] = dx1_ref[...] + _rms_bwd(dh, xh, r, g)

    row = lambda w: pl.BlockSpec((tm, w), lambda i: (i, 0))
    const = lambda shape: pl.BlockSpec(shape, lambda i: (0,) * len(shape))
    last_halo = t // POOL_HALO - 1
    halo_after = pl.BlockSpec((POOL_HALO, POOL_WIDTH),
                              lambda i: (jnp.minimum((i + 1) * (tm // POOL_HALO), last_halo), 0))
    return _compute_call(
        body, (dq, dk, dv, dmixed, dmixed, x, dx1, w_in, g_mix), grid=(n_tiles,),
        in_specs=[row(ATTN_WIDTH), row(KV_WIDTH), row(KV_WIDTH), row(POOL_WIDTH), halo_after, row(D_MODEL),
                  row(D_MODEL), _resident((IN_WIDTH, D_MODEL)), _resident((1, D_MODEL))],
        out_specs=[row(D_MODEL), const((IN_WIDTH, D_MODEL)), const((1, IN_WIDTH)), const((1, D_MODEL))],
        out_shape=[jax.ShapeDtypeStruct((t, D_MODEL), F32), jax.ShapeDtypeStruct((IN_WIDTH, D_MODEL), F32),
                   jax.ShapeDtypeStruct((1, IN_WIDTH), F32), jax.ShapeDtypeStruct((1, D_MODEL), F32)],
        scratch_shapes=[pltpu.VMEM((tm + POOL_HALO, POOL_WIDTH), F32)],
        semantics=("arbitrary",), name="inproj_bwd", exchange=exchange)


def _reduce_to_owner(grads, place, hosts, wire_dtypes):
    me, c = place
    landed = hosts[0](_sibling_exchange(grads))
    partials = _pair_sum_call(grads, landed, c, wire_dtypes)
    slots = hosts[1](_chip_exchange(partials))
    return _chip_sum_call(partials, slots, me, c)


def _local_step(x, target, small, w_in, w_out, w_ffn, place=None):
    t = x.shape[0]
    on_mesh = place is not None
    cos_t, sin_t = _rope_tables(t)
    w_pool = small["w_pool"].astype(BF16)
    w_ffn = list(w_ffn)

    def gathering(bufs):
        return _gather_exchange(bufs) if on_mesh else None

    (h0,), got = _norm_call(x, small["g_mix"], exchange=gathering([w_in]))
    if on_mesh:
        w_in = got[0].reshape(IN_WIDTH, D_MODEL)
    (q, k, v, u), got = _inproj_call(h0, w_in, small["b_in"], cos_t, sin_t, exchange=gathering([w_out]))
    if on_mesh:
        w_out = got[0].reshape(D_MODEL, D_MODEL)
    bias_t, bias_first_t = _band_biases()
    (attn, probs, p_sinks), got = _attn_fwd_call(q, k, v, small["sinks"], bias_t, bias_first_t, exchange=gathering(w_ffn[:2]))
    if on_mesh:
        w_ffn[:2] = got
    (x1, h1), got = _outproj_call(attn, u, x, w_pool, small["b_pool"], small["pool_scale"], w_out,
                                          small["b_out"], small["g_ffn"], exchange=gathering(w_ffn[2:]))
    if on_mesh:
        w_ffn[2:] = got
    w_gate, w_up, w_down = w_ffn
    act, dgate, dup, dx1, dx2b, loss, g_g_final, g_g_ffn = _ffn_call(h1, x1, target, w_gate, w_up, w_down,
                                                                      small["g_ffn"], small["g_final"])
    ffn_grads = list(_ffn_bwd_weights_call(h1, dgate, dup, act, dx2b))
    kept = {}

    def outproj_bwd(exchange=None):
        kept["outproj"], results = _outproj_bwd_call(dx1, attn, u, w_pool, small["b_pool"], small["pool_scale"],
                                                     w_out, exchange=exchange)
        return results

    def attn_bwd(exchange=None):
        kept["attn"], results = _attn_bwd_call(q, k, v, kept["outproj"][0], probs, p_sinks, cos_t, sin_t,
                                               exchange=exchange)
        return results

    def inproj_bwd(exchange=None):
        dq, dk, dv, _ = kept["attn"]
        kept["inproj"], results = _inproj_bwd_call(dq, dk, dv, kept["outproj"][1], x, dx1, w_in, small["g_mix"],
                                                   exchange=exchange)
        return results

    if on_mesh:
        ffn_grads = _reduce_to_owner(ffn_grads, place, (outproj_bwd, attn_bwd), [BF16] * 3)
    else:
        outproj_bwd(), attn_bwd()
    inproj_bwd()
    _, _, g_w_out, g_b_out, g_w_pool, g_b_pool, g_pool_scale = kept["outproj"]
    g_sinks = kept["attn"][3]
    grad_x, g_w_in, g_b_in, g_g_mix = kept["inproj"]
    small_grads = dict(g_mix=g_g_mix, b_in=g_b_in, sinks=g_sinks[:, 0], w_pool=g_w_pool,
                       b_pool=g_b_pool, pool_scale=g_pool_scale, b_out=g_b_out, g_ffn=g_g_ffn, g_final=g_g_final)
    big_grads = dict(w_in=g_w_in, w_out=g_w_out, w_gate=ffn_grads[0], w_up=ffn_grads[1], w_down=ffn_grads[2])
    return loss, grad_x, small_grads, big_grads


def _cast_place_call(shards, me):
    na = len(shards)
    split = 2

    def body(me_ref, *refs):
        for a in range(na):
            refs[na + a][...] = refs[a][...].astype(BF16)

    in_specs = [pl.BlockSpec((s.shape[0] // split, s.shape[1]), lambda i, me_ref: (i, 0)) for s in shards]
    out_specs = [pl.BlockSpec((None, s.shape[0] // split, s.shape[1]), lambda i, me_ref: (me_ref[0], i, 0))
                 for s in shards]
    return pl.pallas_call(
        body,
        grid_spec=pltpu.PrefetchScalarGridSpec(num_scalar_prefetch=1, grid=(split,), in_specs=in_specs,
                                               out_specs=out_specs),
        out_shape=[jax.ShapeDtypeStruct((N_SHARDS,) + s.shape, BF16) for s in shards],
        compiler_params=_params("parallel"), name="cast_place_weights",
    )(me.reshape(1), *shards)


def _pair_sum_call(grads, landed, c, wire_dtypes):
    na = len(grads)
    split = 2

    def body(c_ref, *refs):
        ins, lands, outs = refs[:na], refs[na:2 * na], refs[2 * na:]
        for a in range(na):
            outs[a][...] = (ins[a][...] + lands[a][...]).astype(outs[a].dtype)

    in_specs, land_specs, out_specs, views = [], [], [], []
    for g in grads:
        ns, r, cols = g.shape
        rows = r // 2 // split
        views.append(g.reshape(ns, 2, r // 2, cols))
        in_specs.append(pl.BlockSpec((None, None, rows, cols), lambda s, i, c_ref: (s, c_ref[0], i, 0)))
        land_specs.append(pl.BlockSpec((None, rows, cols), lambda s, i, c_ref: (s, i, 0)))
        out_specs.append(pl.BlockSpec((None, rows, cols), lambda s, i, c_ref: (s, i, 0)))
    return pl.pallas_call(
        body,
        grid_spec=pltpu.PrefetchScalarGridSpec(num_scalar_prefetch=1, grid=(N_SHARDS, split),
                                               in_specs=in_specs + land_specs, out_specs=out_specs),
        out_shape=[jax.ShapeDtypeStruct(l.shape, dt) for l, dt in zip(landed, wire_dtypes)],
        compiler_params=_params("parallel", "parallel"), name="reduce_pair_sum",
    )(c.reshape(1), *views, *landed)


def _chip_sum_call(partials, slots, me, c):
    na = len(slots)
    split = 2

    def body(place_ref, *refs):
        mine, lands, outs = refs[:na], refs[na:2 * na], refs[2 * na:]
        for j in range(N_SHARDS):
            @pl.when(place_ref[0] == j)
            def _():
                for a in range(na):
                    terms = [mine[a][...] if k == j else lands[a][k] for k in range(N_SHARDS)]
                    t0, t1, t2, t3 = (term.astype(F32) for term in terms)
                    outs[a][...] = ((t0 + t1) + t2) + t3

    mine_specs, land_specs, out_specs = [], [], []
    for s in slots:
        ns, h, cols = s.shape
        mine_specs.append(pl.BlockSpec((None, h // split, cols), lambda i, place: (place[0], i, 0)))
        land_specs.append(pl.BlockSpec((ns, h // split, cols), lambda i, place: (0, i, 0)))
        out_specs.append(pl.BlockSpec((h // split, cols), lambda i, place: (place[1] * split + i, 0)))
    return pl.pallas_call(
        body,
        grid_spec=pltpu.PrefetchScalarGridSpec(num_scalar_prefetch=1, grid=(split,),
                                               in_specs=mine_specs + land_specs, out_specs=out_specs),
        out_shape=[jax.ShapeDtypeStruct((2 * s.shape[1], s.shape[2]), F32) for s in slots],
        compiler_params=_params("parallel"), name="reduce_chip_sum",
    )(jnp.stack([me, c]), *partials, *slots)


def _adamw(w, g, m, v):
    m = ADAM_B1 * m + (1.0 - ADAM_B1) * g
    v = ADAM_B2 * v + (1.0 - ADAM_B2) * jnp.square(g)
    m_hat = m / (1.0 - ADAM_B1 ** ADAM_STEP)
    v_hat = v / (1.0 - ADAM_B2 ** ADAM_STEP)
    return -ADAM_LR * (m_hat / (jnp.sqrt(v_hat) + ADAM_EPS) + ADAM_WD * w), m, v


def _adamw_call(ws, gs, ms, vs):
    na = len(ws)
    split = 4

    def body(*refs):
        w_refs, g_refs, m_refs, v_refs = (refs[k * na:(k + 1) * na] for k in range(4))
        d_refs, nm_refs, nv_refs = (refs[(4 + k) * na:(5 + k) * na] for k in range(3))
        for a in range(na):
            d_refs[a][...], nm_refs[a][...], nv_refs[a][...] = _adamw(
                w_refs[a][...], g_refs[a][...], m_refs[a][...], v_refs[a][...])

    specs = [pl.BlockSpec((w.shape[0] // split, w.shape[1]), lambda i: (i, 0)) for w in ws]
    shapes = [jax.ShapeDtypeStruct(w.shape, F32) for w in ws]
    outs = pl.pallas_call(
        body, grid=(split,), in_specs=specs * 4, out_specs=specs * 3, out_shape=shapes * 3,
        compiler_params=_params("parallel"), name="adamw",
    )(*ws, *gs, *ms, *vs)
    return outs[:na], outs[na:2 * na], outs[2 * na:]


def _small_rows():
    first, row = {}, 0
    for name, (r, c) in SMALL_VIEWS.items():
        first[name] = row
        row += r * (-(-c // LANES))
    return first, row, -(-(row + 1) // 64) * 64


def _pack_small(arrays, loss):
    _, used, rows = _small_rows()
    parts = []
    for name, (r, c) in SMALL_VIEWS.items():
        a = arrays[name].reshape(r, c)
        parts.append(jnp.pad(a, ((0, 0), (0, LANES - c))) if c < LANES else a.reshape(r * c // LANES, LANES))
    parts += [loss, jnp.zeros((rows - used - 1, LANES), F32)]
    return jnp.concatenate(parts, axis=0)


def _adamw_small_call(ws, packed_grads, ms, vs):
    names = list(SMALL_VIEWS)
    first, _, _ = _small_rows()
    n = len(names)

    def body(g_ref, *refs):
        w_refs, m_refs, v_refs = refs[:n], refs[n:2 * n], refs[2 * n:3 * n]
        outs = refs[3 * n:]
        for a, name in enumerate(names):
            r, c = SMALL_VIEWS[name]
            r0 = first[name]
            if c == LANES:
                g = g_ref[r0:r0 + r, :]
            elif c < LANES:
                g = g_ref[r0:r0 + 1, 0:c]
            else:
                g = jnp.concatenate([g_ref[r0 + j:r0 + j + 1, :] for j in range(c // LANES)], axis=1)
            delta, m, v = _adamw(w_refs[a][...], g, m_refs[a][...], v_refs[a][...])
            for k, val in enumerate((g, delta, m, v)):
                outs[k * n + a][...] = val

    shapes = [jax.ShapeDtypeStruct(SMALL_VIEWS[name], F32) for name in names]
    outs = pl.pallas_call(
        body, out_shape=shapes * 4, compiler_params=pltpu.CompilerParams(vmem_limit_bytes=VMEM_LIMIT),
        name="adamw_small",
    )(packed_grads, *[ws[k] for k in names], *[ms[k] for k in names], *[vs[k] for k in names])
    return [dict(zip(names, outs[k * n:(k + 1) * n])) for k in range(4)]


def kernel(x, g_mix, w_in, b_in, sinks, w_pool, b_pool, pool_scale, w_out, b_out, g_ffn, w_gate, w_up, w_down, g_final, loss_target, m_g_mix, m_w_in, m_b_in, m_sinks, m_w_pool, m_b_pool, m_pool_scale, m_w_out, m_b_out, m_g_ffn, m_w_gate, m_w_up, m_w_down, m_g_final, v_g_mix, v_w_in, v_b_in, v_sinks, v_w_pool, v_b_pool, v_pool_scale, v_w_out, v_b_out, v_g_ffn, v_w_gate, v_w_up, v_w_down, v_g_final):
    given = dict(locals())
    big_names = ("w_in", "w_out", "w_gate", "w_up", "w_down")
    out_shapes = {n: given[n].shape for n in SMALL_NAMES + big_names}
    c = lax.axis_index("c")

    me = 2 * lax.axis_index("x") + lax.axis_index("y")
    transposed = ("w_in", "w_gate", "w_up")

    def shard_view(a, name):
        return a[0].T if name in transposed else a[0]

    placed = _cast_place_call([shard_view(given[n], n) for n in big_names], me)

    small = dict(g_mix=g_mix, b_in=b_in, sinks=sinks, w_pool=w_pool[0], b_pool=b_pool.reshape(1, POOL_WIDTH),
                 pool_scale=pool_scale.reshape(1, POOL_WIDTH), b_out=b_out, g_ffn=g_ffn,
                 g_final=g_final.reshape(1, D_MODEL))
    loss, grad_x, small_grads, big_grads = _local_step(
        x[0], loss_target[0], small, placed[0], placed[1], placed[2:], place=(me, c))

    packed = _pack_small(small_grads, loss)
    grads = [
        big_grads["w_in"].reshape(N_SHARDS, IN_WIDTH // N_SHARDS, D_MODEL),
        big_grads["w_out"].reshape(N_SHARDS, D_MODEL // N_SHARDS, D_MODEL),
        jnp.broadcast_to(packed[None], (N_SHARDS,) + packed.shape),
    ]
    alone = [functools.partial(_exchange_call, name=name) for name in
             ("reduce_sibling_exchange", "reduce_chip_exchange")]
    h_in, h_out, h_small = _reduce_to_owner(grads, (me, c), alone, [BF16, BF16, F32])
    totals = _exchange_call(_sibling_allgather(
        [h_in, h_out, big_grads["w_gate"], big_grads["w_up"], big_grads["w_down"], h_small]), "reduce_sibling_allgather")
    loss = totals[-1][_small_rows()[1], 0]

    def owned(prefix):
        return [shard_view(given[prefix + n], n) for n in big_names]

    def owned_small(prefix):
        return {n: given[prefix + n].reshape(SMALL_VIEWS[n]) for n in SMALL_NAMES}

    big = (totals[:-1],) + _adamw_call(owned(""), totals[:-1], owned("m_"), owned("v_"))
    small_groups = _adamw_small_call(owned_small(""), totals[-1], owned_small("m_"), owned_small("v_"))

    order = ("g_mix", "w_in", "b_in", "sinks", "w_pool", "b_pool", "pool_scale", "w_out", "b_out", "g_ffn",
             "w_gate", "w_up", "w_down", "g_final")
    results = [loss, grad_x.reshape(x.shape)]
    for group, small_group in zip(big, small_groups):
        named = {n: (a.T if n in transposed else a) for n, a in zip(big_names, group)}
        named.update(small_group)
        results += [named[n].reshape(out_shapes[n]) for n in order]
    return tuple(results)
```

```python
import functools
from typing import Any, Callable, Mapping, NamedTuple, Sequence

import jax
import jax.numpy as jnp
from jax import lax
from jax.experimental import pallas as pl
from jax.experimental.pallas import tpu as pltpu

F32 = jnp.float32
BF16 = jnp.bfloat16

D_MODEL = 1024
ATTN_WIDTH = 512
KV_WIDTH = 128
POOL_WIDTH = 512
IN_WIDTH = ATTN_WIDTH + 2 * KV_WIDTH + POOL_WIDTH
HEAD_DIM = 64
N_Q_HEADS = 8
BLOCK = 128
POOL_SIZES = (2, 4, 8, 16)
POOL_HALO = 16
ROPE_THETA = 10000.0
RMS_EPS = 1e-5
Q_SCALE = HEAD_DIM ** -0.5

ADAM_LR = 0.001
ADAM_B1 = 0.9
ADAM_B2 = 0.999
ADAM_EPS = 1e-08
ADAM_WD = 0.01
ADAM_STEP = 10

N_SHARDS = 4
LANES = 128
VMEM_LIMIT = 56 * 1024 * 1024
FFN_TOKEN_TILE = 256

MESH = pl.DeviceIdType.MESH
HBM_SPEC = pl.BlockSpec(memory_space=pltpu.HBM)

SMALL_VIEWS = dict(w_pool=(4 * LANES, LANES), b_pool=(4, LANES), pool_scale=(4, LANES), g_mix=(1, D_MODEL),
                   b_in=(1, IN_WIDTH), b_out=(1, D_MODEL), g_ffn=(1, D_MODEL), g_final=(1, D_MODEL),
                   sinks=(1, N_Q_HEADS))
SMALL_NAMES = tuple(SMALL_VIEWS)


def _dot(a, b):
    return jnp.dot(a, b, preferred_element_type=F32)


def _dot_nt(a, b):
    return lax.dot_general(a, b, (((1,), (1,)), ((), ())), preferred_element_type=F32)


def _dot_tn(a, b):
    return lax.dot_general(a, b, (((0,), (0,)), ((), ())), preferred_element_type=F32)


def _params(*semantics):
    return pltpu.CompilerParams(dimension_semantics=semantics, vmem_limit_bytes=VMEM_LIMIT)


def _resident(shape):
    return pl.BlockSpec(shape, lambda *_: (0,) * len(shape), pipeline_mode=pl.Buffered(1))


def _token_tile(t, largest=512):
    for tm in (2048, 1024, 512, 256, 128):
        if tm <= largest and t % tm == 0:
            return tm
    raise ValueError(f"sequence length {t} is not a multiple of 128")


def _rms_stats(xf):
    r = lax.rsqrt(jnp.mean(xf * xf, axis=-1, keepdims=True) + RMS_EPS)
    return xf * r, r


def _rms_bwd(dy, xh, r, g):
    dxh = dy * g
    return r * (dxh - xh * jnp.mean(dxh * xh, axis=-1, keepdims=True))


def _rope_partner(t, first):
    return jnp.where(first, pltpu.roll(t, LANES - HEAD_DIM // 2, 1), pltpu.roll(t, HEAD_DIM // 2, 1))


def _first_half_mask(shape):
    lane = lax.broadcasted_iota(jnp.int32, shape, 1)
    return (lane % HEAD_DIM) < (HEAD_DIM // 2)


def _rope_tables(t):
    inv_freq = 1.0 / (ROPE_THETA ** (jnp.arange(0, HEAD_DIM, 2, dtype=F32) / HEAD_DIM))
    ang = jnp.arange(t, dtype=F32)[:, None] * inv_freq[None, :]
    cos, sin = jnp.cos(ang), jnp.sin(ang)
    cos_t = jnp.tile(jnp.concatenate([cos, cos], axis=-1), (1, LANES // HEAD_DIM))
    sin_t = jnp.tile(jnp.concatenate([-sin, sin], axis=-1), (1, LANES // HEAD_DIM))
    return cos_t, sin_t


class _Exchange(NamedTuple):
    inputs: Sequence[Any]
    out_shapes: Sequence[Any]
    aliases: Mapping[int, int]
    sems: Sequence[Any]
    start: Callable[..., None]
    finish: Callable[..., None]


def _compute_call(body, operands, *, grid, in_specs, out_specs, out_shape, scratch_shapes=(), semantics, name,
                  exchange=None):
    n_in, n_out, n_scr = len(in_specs), len(out_specs), len(scratch_shapes)
    if exchange is None:
        return pl.pallas_call(body, grid=grid, in_specs=in_specs, out_specs=out_specs, out_shape=out_shape,
                              scratch_shapes=scratch_shapes, compiler_params=_params(*semantics),
                              name=name)(*operands), []
    k_in, k_out = len(exchange.inputs), len(exchange.out_shapes)

    def hosting(*refs):
        ins, rest = refs[:n_in], refs[n_in:]
        ex_ins, rest = rest[:k_in], rest[k_in:]
        outs, rest = rest[:n_out], rest[n_out:]
        ex_outs, rest = rest[:k_out], rest[k_out:]
        scratch, sems = rest[:n_scr], rest[n_scr:]
        steps = [pl.program_id(d) for d in range(len(grid))]
        is_first = functools.reduce(jnp.logical_and, [s == 0 for s in steps])
        is_last = functools.reduce(jnp.logical_and, [s == g - 1 for s, g in zip(steps, grid)])

        @pl.when(is_first)
        def _():
            exchange.start(ex_ins, ex_outs, sems)

        body(*ins, *outs, *scratch)

        @pl.when(is_last)
        def _():
            exchange.finish(ex_ins, ex_outs, sems)

    results = pl.pallas_call(
        hosting, grid=grid,
        in_specs=list(in_specs) + [HBM_SPEC] * k_in, out_specs=list(out_specs) + [HBM_SPEC] * k_out,
        out_shape=list(out_shape) + list(exchange.out_shapes),
        scratch_shapes=list(scratch_shapes) + list(exchange.sems),
        input_output_aliases={n_in + i: n_out + j for i, j in exchange.aliases.items()},
        compiler_params=_params(*semantics), name=name,
    )(*operands, *exchange.inputs)
    return results[:n_out], results[n_out:]


def _exchange_call(exchange, name):
    n_in, n_out = len(exchange.inputs), len(exchange.out_shapes)

    def body(*refs):
        ins, outs, sems = refs[:n_in], refs[n_in:n_in + n_out], refs[n_in + n_out:]
        exchange.start(ins, outs, sems)
        exchange.finish(ins, outs, sems)

    return pl.pallas_call(
        body, in_specs=[HBM_SPEC] * n_in, out_specs=[HBM_SPEC] * n_out, out_shape=list(exchange.out_shapes),
        input_output_aliases=dict(exchange.aliases), scratch_shapes=list(exchange.sems),
        compiler_params=pltpu.CompilerParams(has_side_effects=True), name=name,
    )(*exchange.inputs)


def _mesh_place():
    x, y, c = lax.axis_index("x"), lax.axis_index("y"), lax.axis_index("c")
    other_chips = [(1 - x, y), (x, 1 - y), (1 - x, 1 - y)]
    return x, y, c, other_chips


def _half(ref, c, rows):
    return ref.at[pl.ds(pl.multiple_of(c * rows, 16), rows), :]


def _remote(src, dst, send_sems, recv_sems, idx, to):
    return functools.partial(pltpu.make_async_remote_copy, src_ref=src, dst_ref=dst, send_sem=send_sems.at[idx],
                             recv_sem=recv_sems.at[idx], device_id=to, device_id_type=MESH)


def _gather_exchange(bufs):
    na = len(bufs)

    def copies(outs, sems):
        send_sems, recv_sems = sems
        x, y, c, other_chips = _mesh_place()
        me, sibling = 2 * x + y, (x, y, 1 - c)
        ici_out, ici_in, fwd_out, fwd_in = [], [], [], []
        for j, (cx, cy) in enumerate(other_chips):
            for a in range(na):
                rows = outs[a].shape[1] // 2
                mine = _half(outs[a].at[me], c, rows)
                landed = _half(outs[a].at[2 * cx + cy], c, rows)
                passed = _half(outs[a].at[2 * cx + cy], 1 - c, rows)
                ici_out.append(_remote(mine, mine, send_sems, recv_sems, j * na + a, (cx, cy, c)))
                ici_in.append(_remote(landed, landed, send_sems, recv_sems, j * na + a, (cx, cy, c)))
                fwd_out.append(_remote(landed, landed, send_sems, recv_sems, (3 + j) * na + a, sibling))
                fwd_in.append(_remote(passed, passed, send_sems, recv_sems, (3 + j) * na + a, sibling))
        return ici_out, ici_in, fwd_out, fwd_in

    def start(ins, outs, sems):
        for cp in copies(outs, sems)[0]:
            cp().start()

    def finish(ins, outs, sems):
        ici_out, ici_in, fwd_out, fwd_in = copies(outs, sems)
        for arrived, forward in zip(ici_in, fwd_out):
            arrived().wait_recv()
            forward().start()
        for cp in fwd_in:
            cp().wait_recv()
        for cp in ici_out + fwd_out:
            cp().wait_send()

    return _Exchange(inputs=bufs, out_shapes=[jax.ShapeDtypeStruct(b.shape, b.dtype) for b in bufs],
                     aliases={a: a for a in range(na)},
                     sems=[pltpu.SemaphoreType.DMA((6 * na,)), pltpu.SemaphoreType.DMA((6 * na,))],
                     start=start, finish=finish)


def _sibling_exchange(grads):
    na = len(grads)

    def copies(ins, outs, sems):
        x, y, c, _ = _mesh_place()
        out = []
        for a in range(na):
            rows = ins[a].shape[1] // 2
            src = ins[a].at[:, pl.ds(pl.multiple_of((1 - c) * rows, 8), rows), :]
            out.append(_remote(src, outs[a], sems[0], sems[1], a, (x, y, 1 - c)))
        return out

    def start(ins, outs, sems):
        for cp in copies(ins, outs, sems):
            cp().start()

    def finish(ins, outs, sems):
        for cp in copies(ins, outs, sems):
            cp().wait_recv()
            cp().wait_send()

    return _Exchange(inputs=grads,
                     out_shapes=[jax.ShapeDtypeStruct((g.shape[0], g.shape[1] // 2, g.shape[2]), g.dtype)
                                 for g in grads],
                     aliases={}, sems=[pltpu.SemaphoreType.DMA((na,)), pltpu.SemaphoreType.DMA((na,))],
                     start=start, finish=finish)


def _chip_exchange(partials):
    na = len(partials)

    def copies(ins, outs, sems):
        x, y, c, other_chips = _mesh_place()
        me = 2 * x + y
        sends, arrivals = [], []
        for j, (cx, cy) in enumerate(other_chips):
            for a in range(na):
                slot = outs[a].at[2 * cx + cy]
                sends.append(_remote(ins[a].at[2 * cx + cy], outs[a].at[me], sems[0], sems[1], j * na + a, (cx, cy, c)))
                arrivals.append(_remote(slot, slot, sems[0], sems[1], j * na + a, (cx, cy, c)))
        return sends, arrivals

    def start(ins, outs, sems):
        for cp in copies(ins, outs, sems)[0]:
            cp().start()

    def finish(ins, outs, sems):
        sends, arrivals = copies(ins, outs, sems)
        for cp in arrivals:
            cp().wait_recv()
        for cp in sends:
            cp().wait_send()

    return _Exchange(inputs=partials, out_shapes=[jax.ShapeDtypeStruct(p.shape, p.dtype) for p in partials],
                     aliases={}, sems=[pltpu.SemaphoreType.DMA((3 * na,)), pltpu.SemaphoreType.DMA((3 * na,))],
                     start=start, finish=finish)


def _sibling_allgather(bufs):
    na = len(bufs)

    def piece(outs, sems, a, half):
        x, y, c, _ = _mesh_place()
        rows = outs[a].shape[0] // 2
        rows_ref = outs[a].at[pl.ds(pl.multiple_of(half * rows, 8), rows), :]
        return _remote(rows_ref, rows_ref, sems[0], sems[1], a, (x, y, 1 - c))

    def start(ins, outs, sems):
        c = lax.axis_index("c")
        for a in range(na):
            piece(outs, sems, a, c)().start()

    def finish(ins, outs, sems):
        c = lax.axis_index("c")
        for a in range(na):
            piece(outs, sems, a, 1 - c)().wait_recv()
        for a in range(na):
            piece(outs, sems, a, c)().wait_send()

    return _Exchange(inputs=bufs, out_shapes=[jax.ShapeDtypeStruct(b.shape, b.dtype) for b in bufs],
                     aliases={a: a for a in range(na)},
                     sems=[pltpu.SemaphoreType.DMA((na,)), pltpu.SemaphoreType.DMA((na,))],
                     start=start, finish=finish)


def _norm_call(x, g_mix, exchange=None):
    t = x.shape[0]
    tm = _token_tile(t)

    def body(x_ref, g_ref, h_ref):
        xh, _ = _rms_stats(x_ref[...])
        h_ref[...] = (xh * g_ref[...]).astype(BF16)

    row = pl.BlockSpec((tm, D_MODEL), lambda i: (i, 0))
    return _compute_call(body, (x, g_mix), grid=(t // tm,), in_specs=[row, _resident((1, D_MODEL))],
                         out_specs=[row], out_shape=[jax.ShapeDtypeStruct((t, D_MODEL), BF16)],
                         semantics=("parallel",), name="norm_fwd", exchange=exchange)


def _inproj_call(h0, w_in, b_in, cos_t, sin_t, exchange=None):
    t = h0.shape[0]
    tm = _token_tile(t, 1024)

    def body(h_ref, w_ref, b_ref, c_ref, s_ref, q_ref, k_ref, v_ref, u_ref):
        z = _dot_nt(h_ref[...], w_ref[...]) + b_ref[...]
        cos, sin = c_ref[...], s_ref[...]
        first = _first_half_mask((tm, LANES))
        for j in range(ATTN_WIDTH // LANES):
            zj = z[:, LANES * j:LANES * (j + 1)]
            q_ref[:, LANES * j:LANES * (j + 1)] = ((zj * cos + _rope_partner(zj, first) * sin) * Q_SCALE).astype(BF16)
        zk = z[:, ATTN_WIDTH:ATTN_WIDTH + KV_WIDTH]
        k_ref[...] = (zk * cos + _rope_partner(zk, first) * sin).astype(BF16)
        v_ref[...] = z[:, ATTN_WIDTH + KV_WIDTH:ATTN_WIDTH + 2 * KV_WIDTH].astype(BF16)
        u_ref[...] = z[:, ATTN_WIDTH + 2 * KV_WIDTH:]

    row = lambda w: pl.BlockSpec((tm, w), lambda i: (i, 0))
    return _compute_call(
        body, (h0, w_in, b_in, cos_t, sin_t), grid=(t // tm,),
        in_specs=[row(D_MODEL), _resident((IN_WIDTH, D_MODEL)), _resident((1, IN_WIDTH)), row(LANES), row(LANES)],
        out_specs=[row(ATTN_WIDTH), row(KV_WIDTH), row(KV_WIDTH), row(POOL_WIDTH)],
        out_shape=[jax.ShapeDtypeStruct((t, ATTN_WIDTH), BF16),
                   jax.ShapeDtypeStruct((t, KV_WIDTH), BF16), jax.ShapeDtypeStruct((t, KV_WIDTH), BF16),
                   jax.ShapeDtypeStruct((t, POOL_WIDTH), F32)],
        semantics=("parallel",), name="inproj_fwd", exchange=exchange)


def _head_in_both_halves(band, kv_head):
    lane = lax.broadcasted_iota(jnp.int32, band.shape, 1)
    own = (lane < HEAD_DIM) if kv_head == 0 else (lane >= HEAD_DIM)
    return jnp.where(own, band, pltpu.roll(band, HEAD_DIM, 1)).astype(BF16)


def _stack_heads(ref, kv_head, row0):
    low = lax.broadcasted_iota(jnp.int32, (BLOCK, LANES), 1) < HEAD_DIM
    parts = []
    for jj in range(2):
        slab = ref[row0:row0 + BLOCK, LANES * (2 * kv_head + jj):LANES * (2 * kv_head + jj + 1)]
        zero = jnp.zeros_like(slab)
        parts += [jnp.where(low, slab, zero), jnp.where(low, zero, slab)]
    return jnp.concatenate(parts, axis=0)


def _band_biases():
    c = lax.broadcasted_iota(jnp.int32, (2 * BLOCK, 4 * BLOCK), 0)
    i = lax.broadcasted_iota(jnp.int32, (2 * BLOCK, 4 * BLOCK), 1) % BLOCK
    vis = (c > i) & (c <= i + BLOCK)
    return jnp.where(vis, 0.0, -jnp.inf).astype(F32), jnp.where(vis & (c >= BLOCK), 0.0, -jnp.inf).astype(F32)


def _blocks_per_step(nb):
    for g in (4, 2, 1):
        if nb % g == 0:
            return g


def _sink_row(sink_ref, kv_head):
    return jnp.concatenate([jnp.full((1, BLOCK), sink_ref[0, 4 * kv_head + g], F32) for g in range(4)], axis=1)


def _kv_band(cur_ref, prev_ref, g):
    r0 = g * BLOCK
    prev = prev_ref[...] if g == 0 else cur_ref[r0 - BLOCK:r0, :]
    return jnp.concatenate([prev, cur_ref[r0:r0 + BLOCK, :]], axis=0).astype(F32)


def _attn_fwd_call(q, k, v, sinks, bias, bias_first, exchange=None):
    t = q.shape[0]
    nb = t // BLOCK
    gb = _blocks_per_step(nb)
    qb = gb * BLOCK

    def body(sink_ref, q_ref, kc_ref, kp_ref, vc_ref, vp_ref, b_ref, bf_ref, o_ref, p_ref, psink_ref):
        n = pl.program_id(0)
        row_low = lax.broadcasted_iota(jnp.int32, (LANES, BLOCK), 0) < HEAD_DIM
        for g in range(gb):
            r0 = g * BLOCK
            kband = _kv_band(kc_ref, kp_ref, g)
            vband_t = _kv_band(vc_ref, vp_ref, g).T
            bias_g = jnp.where(n == 0, bf_ref[...], b_ref[...]) if g == 0 else b_ref[...]
            for kh in range(2):
                k2 = _head_in_both_halves(kband, kh)
                vt_head = vband_t[HEAD_DIM * kh:HEAD_DIM * (kh + 1), :]
                v2_t = jnp.concatenate([vt_head, vt_head], axis=0).astype(BF16)
                s_t = _dot_nt(k2, _stack_heads(q_ref, kh, r0)) + bias_g
                sink = _sink_row(sink_ref, kh)
                m = jnp.maximum(jnp.max(s_t, axis=0, keepdims=True), sink)
                e = jnp.exp(s_t - m)
                e_sink = jnp.exp(sink - m)
                inv = 1.0 / (jnp.sum(e, axis=0, keepdims=True) + e_sink)
                p_t = (e * inv).astype(BF16)
                p_ref[2 * g + kh] = p_t
                psink_ref[2 * g + kh] = e_sink * inv
                o_t = _dot(v2_t, p_t)
                for jj in range(2):
                    slab = jnp.where(row_low, o_t[:, BLOCK * 2 * jj:BLOCK * (2 * jj + 1)],
                                     o_t[:, BLOCK * (2 * jj + 1):BLOCK * (2 * jj + 2)]).T
                    o_ref[r0:r0 + BLOCK, LANES * (2 * kh + jj):LANES * (2 * kh + jj + 1)] = slab.astype(BF16)

    cur = lambda w: pl.BlockSpec((qb, w), lambda n: (n, 0))
    prev = lambda w: pl.BlockSpec((BLOCK, w), lambda n: (jnp.maximum(n * gb - 1, 0), 0))
    return _compute_call(
        body, (sinks, q, k, k, v, v, bias, bias_first), grid=(nb // gb,),
        in_specs=[pl.BlockSpec(memory_space=pltpu.SMEM), cur(ATTN_WIDTH), cur(KV_WIDTH), prev(KV_WIDTH),
                  cur(KV_WIDTH), prev(KV_WIDTH), _resident(bias.shape), _resident(bias.shape)],
        out_specs=[cur(ATTN_WIDTH), pl.BlockSpec((2 * gb, 2 * BLOCK, 4 * BLOCK), lambda n: (n, 0, 0)),
                   pl.BlockSpec((2 * gb, 1, 4 * BLOCK), lambda n: (n, 0, 0))],
        out_shape=[jax.ShapeDtypeStruct((t, ATTN_WIDTH), BF16),
                   jax.ShapeDtypeStruct((2 * nb, 2 * BLOCK, 4 * BLOCK), BF16),
                   jax.ShapeDtypeStruct((2 * nb, 1, 4 * BLOCK), F32)],
        semantics=("parallel",), name="attn_fwd", exchange=exchange)


def _pool_mixed(sc_ref, u_ref, halo_ref, i, tm):
    sc_ref[0:POOL_HALO, :] = jnp.where(i == 0, 0.0, halo_ref[...])
    sc_ref[POOL_HALO:, :] = u_ref[...]
    tok = i * tm + lax.broadcasted_iota(jnp.int32, (tm, 1), 0)
    mixed = []
    for g, size in enumerate(POOL_SIZES):
        lanes = slice(LANES * g, LANES * (g + 1))
        cur = sc_ref[pl.ds(POOL_HALO, tm), lanes]
        acc = cur
        for j in range(1, size):
            acc = acc + sc_ref[pl.ds(POOL_HALO - j, tm), lanes]
        count = jnp.minimum(tok + 1, size).astype(F32)
        mixed.append(acc / count - cur)
    return mixed


def _halo_before(tm, width):
    return pl.BlockSpec((POOL_HALO, width), lambda i: (jnp.maximum(i * (tm // POOL_HALO) - 1, 0), 0))


def _outproj_call(attn, u, x, w_pool, b_pool, pool_scale, w_out, b_out, g_ffn, exchange=None):
    t = x.shape[0]
    tm = _token_tile(t)

    def body(a_ref, u_ref, halo_ref, x_ref, wp_ref, bp_ref, ps_ref, wo_ref, bo_ref, g_ref,
             x1_ref, h1_ref, sc_ref, y_ref):
        i = pl.program_id(0)
        for g, mixed in enumerate(_pool_mixed(sc_ref, u_ref, halo_ref, i, tm)):
            lanes = slice(LANES * g, LANES * (g + 1))
            lin = _dot(mixed.astype(BF16), wp_ref[g]) + bp_ref[:, lanes]
            y_ref[:, lanes] = (lin * ps_ref[:, lanes]).astype(BF16)
        x1 = (x_ref[...] + _dot(a_ref[...], wo_ref[0:ATTN_WIDTH, :]) + _dot(y_ref[...], wo_ref[ATTN_WIDTH:, :])
              + bo_ref[...])
        x1_ref[...] = x1
        xh, _ = _rms_stats(x1)
        h1_ref[...] = (xh * g_ref[...]).astype(BF16)

    row = lambda w: pl.BlockSpec((tm, w), lambda i: (i, 0))
    return _compute_call(
        body, (attn, u, u, x, w_pool, b_pool, pool_scale, w_out, b_out, g_ffn), grid=(t // tm,),
        in_specs=[row(ATTN_WIDTH), row(POOL_WIDTH), _halo_before(tm, POOL_WIDTH), row(D_MODEL),
                  _resident((4, LANES, LANES)), _resident((1, POOL_WIDTH)), _resident((1, POOL_WIDTH)),
                  _resident((D_MODEL, D_MODEL)), _resident((1, D_MODEL)), _resident((1, D_MODEL))],
        out_specs=[row(D_MODEL), row(D_MODEL)],
        out_shape=[jax.ShapeDtypeStruct((t, D_MODEL), F32), jax.ShapeDtypeStruct((t, D_MODEL), BF16)],
        scratch_shapes=[pltpu.VMEM((tm + POOL_HALO, POOL_WIDTH), F32), pltpu.VMEM((tm, POOL_WIDTH), BF16)],
        semantics=("parallel",), name="outproj_fwd", exchange=exchange)


def _ffn_call(h1, x1, target, w_gate, w_up, w_down, g_ffn, g_final):
    t = x1.shape[0]
    tm = _token_tile(t, FFN_TOKEN_TILE)
    ns, fs, _ = w_gate.shape

    def body(h_ref, x1_ref, tgt_ref, wg_ref, wu_ref, wd_ref, g_ref, gf_ref,
             act_ref, dgate_ref, dup_ref, dx1_ref, dx2b_ref, loss_ref, dgf_ref, dg_ref, gate_ref, up_ref):
        @pl.when(pl.program_id(0) == 0)
        def _():
            loss_ref[...] = jnp.zeros_like(loss_ref)
            dgf_ref[...] = jnp.zeros_like(dgf_ref)
            dg_ref[...] = jnp.zeros_like(dg_ref)

        h = h_ref[...]
        x1 = x1_ref[...]
        x2 = x1
        for s in range(ns):
            gate = _dot_nt(h, wg_ref[s])
            up = _dot_nt(h, wu_ref[s])
            act = (gate * jax.nn.sigmoid(gate) * up).astype(BF16)
            gate_ref[s] = gate
            up_ref[s] = up
            act_ref[s] = act
            x2 = x2 + _dot(act, wd_ref[s])
        xh2, r2 = _rms_stats(x2)
        gf = gf_ref[...]
        err = xh2 * gf - tgt_ref[...]
        loss_ref[...] += 0.5 * jnp.sum(err * err) / D_MODEL
        dy = err / D_MODEL
        dgf_ref[...] += jnp.sum(dy * xh2, axis=0, keepdims=True)
        dx2 = _rms_bwd(dy, xh2, r2, gf)
        dx2b = dx2.astype(BF16)
        dx2b_ref[...] = dx2b
        dh = None
        for s in range(ns):
            dact = _dot_nt(dx2b, wd_ref[s])
            gate = gate_ref[s]
            sg = jax.nn.sigmoid(gate)
            dup = (dact * (gate * sg)).astype(BF16)
            dgate = (dact * up_ref[s] * (sg * (1.0 + gate * (1.0 - sg)))).astype(BF16)
            dgate_ref[s] = dgate
            dup_ref[s] = dup
            part = _dot(dgate, wg_ref[s]) + _dot(dup, wu_ref[s])
            dh = part if dh is None else dh + part
        xh1, r1 = _rms_stats(x1)
        dg_ref[...] += jnp.sum(dh * xh1, axis=0, keepdims=True)
        dx1_ref[...] = dx2 + _rms_bwd(dh, xh1, r1, g_ref[...])

    row = lambda w: pl.BlockSpec((tm, w), lambda i: (i, 0))
    shard_act = pl.BlockSpec((ns, tm, fs), lambda i: (0, i, 0))
    act_shape = jax.ShapeDtypeStruct((ns, t, fs), BF16)
    vec = pl.BlockSpec((1, D_MODEL), lambda i: (0, 0))
    return pl.pallas_call(
        body, grid=(t // tm,),
        in_specs=[row(D_MODEL), row(D_MODEL), row(D_MODEL), _resident((ns, fs, D_MODEL)),
                  _resident((ns, fs, D_MODEL)), _resident((ns, fs, D_MODEL)), _resident((1, D_MODEL)),
                  _resident((1, D_MODEL))],
        out_specs=[shard_act, shard_act, shard_act, row(D_MODEL), row(D_MODEL),
                   pl.BlockSpec((1, LANES), lambda i: (0, 0)), vec, vec],
        out_shape=[act_shape, act_shape, act_shape, jax.ShapeDtypeStruct((t, D_MODEL), F32),
                   jax.ShapeDtypeStruct((t, D_MODEL), BF16), jax.ShapeDtypeStruct((1, LANES), F32),
                   jax.ShapeDtypeStruct((1, D_MODEL), F32), jax.ShapeDtypeStruct((1, D_MODEL), F32)],
        scratch_shapes=[pltpu.VMEM((ns, tm, fs), F32), pltpu.VMEM((ns, tm, fs), F32)],
        compiler_params=_params("arbitrary"), name="ffn_fwd_bwd",
    )(h1, x1, target, w_gate, w_up, w_down, g_ffn, g_final)


def _ffn_bwd_weights_call(h1, dgate, dup, act, dx2):
    t = h1.shape[0]
    tm = _token_tile(t, 2048)
    ns, _, fs = dgate.shape

    def body(h_ref, dgate_ref, dup_ref, act_ref, dx2_ref, gg_ref, gu_ref, gd_ref):
        @pl.when(pl.program_id(1) == 0)
        def _():
            gg_ref[...] = jnp.zeros_like(gg_ref)
            gu_ref[...] = jnp.zeros_like(gu_ref)
            gd_ref[...] = jnp.zeros_like(gd_ref)

        h = h_ref[...]
        gg_ref[...] += _dot_tn(dgate_ref[...], h)
        gu_ref[...] += _dot_tn(dup_ref[...], h)
        gd_ref[...] += _dot_tn(act_ref[...], dx2_ref[...])

    row = lambda w: pl.BlockSpec((tm, w), lambda s, i: (i, 0))
    shard_act = pl.BlockSpec((None, tm, fs), lambda s, i: (s, i, 0))
    return pl.pallas_call(
        body, grid=(ns, t // tm),
        in_specs=[row(D_MODEL), shard_act, shard_act, shard_act, row(D_MODEL)],
        out_specs=[pl.BlockSpec((None, fs, D_MODEL), lambda s, i: (s, 0, 0))] * 3,
        out_shape=[jax.ShapeDtypeStruct((ns, fs, D_MODEL), F32)] * 3,
        compiler_params=_params("parallel", "arbitrary"), name="ffn_bwd_weights",
    )(h1, dgate, dup, act, dx2)


def _outproj_bwd_call(dx1, attn, u, w_pool, b_pool, pool_scale, w_out, exchange=None):
    t = dx1.shape[0]
    tm = _token_tile(t, 1024)

    def body(dx1_ref, a_ref, u_ref, halo_ref, wp_ref, bp_ref, ps_ref, wo_ref,
             dattn_ref, dmixed_ref, gwo_ref, gbo_ref, gwp_ref, gbp_ref, gps_ref, sc_ref, y_ref):
        i = pl.program_id(0)

        @pl.when(i == 0)
        def _():
            for ref in (gwo_ref, gbo_ref, gwp_ref, gbp_ref, gps_ref):
                ref[...] = jnp.zeros_like(ref)

        dx1 = dx1_ref[...]
        dx1b = dx1.astype(BF16)
        dcat = _dot_nt(dx1b, wo_ref[...])
        dattn_ref[...] = dcat[:, :ATTN_WIDTH].astype(BF16)
        for g, mixed in enumerate(_pool_mixed(sc_ref, u_ref, halo_ref, i, tm)):
            lanes = slice(LANES * g, LANES * (g + 1))
            mixed_b = mixed.astype(BF16)
            lin = _dot(mixed_b, wp_ref[g]) + bp_ref[:, lanes]
            y_ref[:, lanes] = (lin * ps_ref[:, lanes]).astype(BF16)
            dpool = dcat[:, ATTN_WIDTH + LANES * g:ATTN_WIDTH + LANES * (g + 1)]
            gps_ref[:, lanes] += jnp.sum(dpool * lin, axis=0, keepdims=True)
            dy = dpool * ps_ref[:, lanes]
            gbp_ref[:, lanes] += jnp.sum(dy, axis=0, keepdims=True)
            dy_b = dy.astype(BF16)
            gwp_ref[g] += _dot_tn(mixed_b, dy_b)
            dmixed_ref[:, lanes] = _dot_nt(dy_b, wp_ref[g])
        gwo_ref[0:ATTN_WIDTH, :] += _dot_tn(a_ref[...], dx1b)
        gwo_ref[ATTN_WIDTH:, :] += _dot_tn(y_ref[...], dx1b)
        gbo_ref[...] += jnp.sum(dx1, axis=0, keepdims=True)

    row = lambda w: pl.BlockSpec((tm, w), lambda i: (i, 0))
    const = lambda shape: pl.BlockSpec(shape, lambda i: (0,) * len(shape))
    return _compute_call(
        body, (dx1, attn, u, u, w_pool, b_pool, pool_scale, w_out), grid=(t // tm,),
        in_specs=[row(D_MODEL), row(ATTN_WIDTH), row(POOL_WIDTH), _halo_before(tm, POOL_WIDTH),
                  _resident((4, LANES, LANES)), _resident((1, POOL_WIDTH)), _resident((1, POOL_WIDTH)),
                  _resident((D_MODEL, D_MODEL))],
        out_specs=[row(ATTN_WIDTH), row(POOL_WIDTH), const((D_MODEL, D_MODEL)), const((1, D_MODEL)),
                   const((4, LANES, LANES)), const((1, POOL_WIDTH)), const((1, POOL_WIDTH))],
        out_shape=[jax.ShapeDtypeStruct((t, ATTN_WIDTH), BF16), jax.ShapeDtypeStruct((t, POOL_WIDTH), F32),
                   jax.ShapeDtypeStruct((D_MODEL, D_MODEL), F32), jax.ShapeDtypeStruct((1, D_MODEL), F32),
                   jax.ShapeDtypeStruct((4, LANES, LANES), F32), jax.ShapeDtypeStruct((1, POOL_WIDTH), F32),
                   jax.ShapeDtypeStruct((1, POOL_WIDTH), F32)],
        scratch_shapes=[pltpu.VMEM((tm + POOL_HALO, POOL_WIDTH), F32), pltpu.VMEM((tm, POOL_WIDTH), BF16)],
        semantics=("arbitrary",), name="outproj_bwd", exchange=exchange)


def _attn_bwd_call(q, k, v, dout, probs, p_sinks, cos_t, sin_t, exchange=None):
    t = q.shape[0]
    nb = t // BLOCK
    gb = _blocks_per_step(nb)
    qb = gb * BLOCK
    steps = nb // gb

    def body(q_ref, kc_ref, kp_ref, vc_ref, vp_ref, do_ref, p_ref, psink_ref, cq_ref, sq_ref, ck_ref, sk_ref,
             dq_ref, dk_ref, dv_ref, dsink_ref, newk_ref, newv_ref, carryk_ref, carryv_ref, headk_ref, headv_ref):
        n = pl.program_id(0)
        first = _first_half_mask((BLOCK, LANES))

        @pl.when(n == 0)
        def _():
            dsink_ref[...] = jnp.zeros_like(dsink_ref)

        @pl.when(n < steps)
        def _():
            head_row = lax.broadcasted_iota(jnp.int32, dsink_ref.shape, 0)
            lane2 = lax.broadcasted_iota(jnp.int32, (2 * BLOCK, LANES), 1)
            row_low = lax.broadcasted_iota(jnp.int32, (LANES, BLOCK), 0) < HEAD_DIM
            dsink = jnp.zeros(dsink_ref.shape, F32)
            for g in range(gb):
                r0 = g * BLOCK
                kband = _kv_band(kc_ref, kp_ref, g)
                vband = _kv_band(vc_ref, vp_ref, g)
                kband_t = kband.T
                cq, sq = cq_ref[r0:r0 + BLOCK, :], sq_ref[r0:r0 + BLOCK, :]
                dk_heads, dv_heads = [], []
                for kh in range(2):
                    v2 = _head_in_both_halves(vband, kh)
                    kt_head = kband_t[HEAD_DIM * kh:HEAD_DIM * (kh + 1), :]
                    k2_t = jnp.concatenate([kt_head, kt_head], axis=0).astype(BF16)
                    qst = _stack_heads(q_ref, kh, r0)
                    dost = _stack_heads(do_ref, kh, r0)
                    p_b = p_ref[2 * g + kh]
                    p_t = p_b.astype(F32)
                    dp_t = _dot_nt(v2, dost)
                    delta = jnp.sum(p_t * dp_t, axis=0, keepdims=True)
                    ds_t = (p_t * (dp_t - delta)).astype(BF16)
                    sink_term = psink_ref[2 * g + kh] * delta
                    for h in range(4):
                        val = -jnp.sum(sink_term[:, BLOCK * h:BLOCK * (h + 1)])
                        dsink = dsink + jnp.where(head_row == 4 * kh + h, val, 0.0)
                    dq_t = _dot(k2_t, ds_t)
                    for jj in range(2):
                        slab = jnp.where(row_low, dq_t[:, BLOCK * 2 * jj:BLOCK * (2 * jj + 1)],
                                         dq_t[:, BLOCK * (2 * jj + 1):BLOCK * (2 * jj + 2)]).T
                        dq_ref[r0:r0 + BLOCK, LANES * (2 * kh + jj):LANES * (2 * kh + jj + 1)] = (
                            (slab * cq + _rope_partner(slab * sq, first)) * Q_SCALE).astype(BF16)
                    dk2 = _dot(ds_t, qst)
                    dv2 = _dot(p_b, dost)
                    dk_heads.append(dk2 + pltpu.roll(dk2, HEAD_DIM, 1))
                    dv_heads.append(dv2 + pltpu.roll(dv2, HEAD_DIM, 1))
                bandk = jnp.where(lane2 < HEAD_DIM, dk_heads[0], dk_heads[1])
                bandv = jnp.where(lane2 < HEAD_DIM, dv_heads[0], dv_heads[1])
                if g == 0:
                    headk_ref[...] = bandk[0:BLOCK]
                    headv_ref[...] = bandv[0:BLOCK]
                else:
                    newk_ref[r0 - BLOCK:r0, :] += bandk[0:BLOCK]
                    newv_ref[r0 - BLOCK:r0, :] += bandv[0:BLOCK]
                newk_ref[r0:r0 + BLOCK, :] = bandk[BLOCK:]
                newv_ref[r0:r0 + BLOCK, :] = bandv[BLOCK:]
            dsink_ref[...] += dsink

        @pl.when(n == steps)
        def _():
            headk_ref[...] = jnp.zeros_like(headk_ref)
            headv_ref[...] = jnp.zeros_like(headv_ref)

        @pl.when(n >= 1)
        def _():
            last = qb - BLOCK
            carryk_ref[last:, :] += headk_ref[...]
            carryv_ref[last:, :] += headv_ref[...]
            for g in range(gb):
                rows = slice(g * BLOCK, (g + 1) * BLOCK)
                dk = carryk_ref[rows, :]
                dk_ref[rows, :] = (dk * ck_ref[rows, :] + _rope_partner(dk * sk_ref[rows, :], first)).astype(BF16)
            dv_ref[...] = carryv_ref[...].astype(BF16)

        @pl.when(n < steps)
        def _():
            carryk_ref[...] = newk_ref[...]
            carryv_ref[...] = newv_ref[...]

    cur = lambda w: pl.BlockSpec((qb, w), lambda n: (jnp.minimum(n, steps - 1), 0))
    late = lambda w: pl.BlockSpec((qb, w), lambda n: (jnp.maximum(n - 1, 0), 0))
    prev = lambda w: pl.BlockSpec((BLOCK, w), lambda n: (jnp.maximum(jnp.minimum(n, steps - 1) * gb - 1, 0), 0))
    return _compute_call(
        body, (q, k, k, v, v, dout, probs, p_sinks, cos_t, sin_t, cos_t, sin_t), grid=(steps + 1,),
        in_specs=[cur(ATTN_WIDTH), cur(KV_WIDTH), prev(KV_WIDTH), cur(KV_WIDTH), prev(KV_WIDTH), cur(ATTN_WIDTH),
                  pl.BlockSpec((2 * gb, 2 * BLOCK, 4 * BLOCK), lambda n: (jnp.minimum(n, steps - 1), 0, 0)),
                  pl.BlockSpec((2 * gb, 1, 4 * BLOCK), lambda n: (jnp.minimum(n, steps - 1), 0, 0)),
                  cur(LANES), cur(LANES), late(LANES), late(LANES)],
        out_specs=[cur(ATTN_WIDTH), late(KV_WIDTH), late(KV_WIDTH), pl.BlockSpec((8, LANES), lambda n: (0, 0))],
        out_shape=[jax.ShapeDtypeStruct((t, ATTN_WIDTH), BF16), jax.ShapeDtypeStruct((t, KV_WIDTH), BF16),
                   jax.ShapeDtypeStruct((t, KV_WIDTH), BF16), jax.ShapeDtypeStruct((8, LANES), F32)],
        scratch_shapes=[pltpu.VMEM((qb, LANES), F32), pltpu.VMEM((qb, LANES), F32),
                        pltpu.VMEM((qb, LANES), F32), pltpu.VMEM((qb, LANES), F32),
                        pltpu.VMEM((BLOCK, LANES), F32), pltpu.VMEM((BLOCK, LANES), F32)],
        semantics=("arbitrary",), name="attn_bwd", exchange=exchange)


def _inproj_bwd_call(dq, dk, dv, dmixed, x, dx1, w_in, g_mix, exchange=None):
    t = x.shape[0]
    tm = _token_tile(t, 1024)
    n_tiles = t // tm

    def body(dq_ref, dk_ref, dv_ref, dm_ref, halo_ref, x_ref, dx1_ref, w_ref, g_ref,
             dx_ref, gw_ref, gb_ref, gg_ref, sc_ref):
        i = pl.program_id(0)

        @pl.when(i == 0)
        def _():
            for ref in (gw_ref, gb_ref, gg_ref):
                ref[...] = jnp.zeros_like(ref)

        tok = i * tm + lax.broadcasted_iota(jnp.int32, (tm, 1), 0)
        halo = jnp.where(i == n_tiles - 1, 0.0, halo_ref[...])
        du = []
        for g, size in enumerate(POOL_SIZES):
            lanes = slice(LANES * g, LANES * (g + 1))
            dm = dm_ref[:, lanes]
            sc_ref[0:tm, lanes] = dm / jnp.minimum(tok + 1, size).astype(F32)
            sc_ref[tm:, lanes] = halo[:, lanes] / float(size)
            acc = -dm
            for j in range(size):
                acc = acc + sc_ref[pl.ds(j, tm), lanes]
            du.append(acc)
        dz32 = jnp.concatenate([dq_ref[...].astype(F32), dk_ref[...].astype(F32), dv_ref[...].astype(F32)] + du,
                               axis=1)
        dz = dz32.astype(BF16)
        gb_ref[...] += jnp.sum(dz32, axis=0, keepdims=True)
        xh, r = _rms_stats(x_ref[...])
        g = g_ref[...]
        gw_ref[...] += _dot_tn(dz, (xh * g).astype(BF16))
        dh = _dot(dz, w_ref[...])
        gg_ref[...] += jnp.sum(dh * xh, axis=0, keepdims=True)
        dx_ref[...] = dx1_ref[...] + _rms_bwd(dh, xh, r, g)

    row = lambda w: pl.BlockSpec((tm, w), lambda i: (i, 0))
    const = lambda shape: pl.BlockSpec(shape, lambda i: (0,) * len(shape))
    last_halo = t // POOL_HALO - 1
    halo_after = pl.BlockSpec((POOL_HALO, POOL_WIDTH),
                              lambda i: (jnp.minimum((i + 1) * (tm // POOL_HALO), last_halo), 0))
    return _compute_call(
        body, (dq, dk, dv, dmixed, dmixed, x, dx1, w_in, g_mix), grid=(n_tiles,),
        in_specs=[row(ATTN_WIDTH), row(KV_WIDTH), row(KV_WIDTH), row(POOL_WIDTH), halo_after, row(D_MODEL),
                  row(D_MODEL), _resident((IN_WIDTH, D_MODEL)), _resident((1, D_MODEL))],
        out_specs=[row(D_MODEL), const((IN_WIDTH, D_MODEL)), const((1, IN_WIDTH)), const((1, D_MODEL))],
        out_shape=[jax.ShapeDtypeStruct((t, D_MODEL), F32), jax.ShapeDtypeStruct((IN_WIDTH, D_MODEL), F32),
                   jax.ShapeDtypeStruct((1, IN_WIDTH), F32), jax.ShapeDtypeStruct((1, D_MODEL), F32)],
        scratch_shapes=[pltpu.VMEM((tm + POOL_HALO, POOL_WIDTH), F32)],
        semantics=("arbitrary",), name="inproj_bwd", exchange=exchange)


def _reduce_to_owner(grads, place, hosts, wire_dtypes):
    me, c = place
    landed = hosts[0](_sibling_exchange(grads))
    partials = _pair_sum_call(grads, landed, c, wire_dtypes)
    slots = hosts[1](_chip_exchange(partials))
    return _chip_sum_call(partials, slots, me, c)


def _local_step(x, target, small, w_in, w_out, w_ffn, place=None):
    t = x.shape[0]
    on_mesh = place is not None
    cos_t, sin_t = _rope_tables(t)
    w_pool = small["w_pool"].astype(BF16)
    w_ffn = list(w_ffn)

    def gathering(bufs):
        return _gather_exchange(bufs) if on_mesh else None

    (h0,), got = _norm_call(x, small["g_mix"], exchange=gathering([w_in]))
    if on_mesh:
        w_in = got[0].reshape(IN_WIDTH, D_MODEL)
    (q, k, v, u), got = _inproj_call(h0, w_in, small["b_in"], cos_t, sin_t, exchange=gathering([w_out]))
    if on_mesh:
        w_out = got[0].reshape(D_MODEL, D_MODEL)
    bias_t, bias_first_t = _band_biases()
    (attn, probs, p_sinks), got = _attn_fwd_call(q, k, v, small["sinks"], bias_t, bias_first_t, exchange=gathering(w_ffn[:2]))
    if on_mesh:
        w_ffn[:2] = got
    (x1, h1), got = _outproj_call(attn, u, x, w_pool, small["b_pool"], small["pool_scale"], w_out,
                                          small["b_out"], small["g_ffn"], exchange=gathering(w_ffn[2:]))
    if on_mesh:
        w_ffn[2:] = got
    w_gate, w_up, w_down = w_ffn
    act, dgate, dup, dx1, dx2b, loss, g_g_final, g_g_ffn = _ffn_call(h1, x1, target, w_gate, w_up, w_down,
                                                                      small["g_ffn"], small["g_final"])
    ffn_grads = list(_ffn_bwd_weights_call(h1, dgate, dup, act, dx2b))
    kept = {}

    def outproj_bwd(exchange=None):
        kept["outproj"], results = _outproj_bwd_call(dx1, attn, u, w_pool, small["b_pool"], small["pool_scale"],
                                                     w_out, exchange=exchange)
        return results

    def attn_bwd(exchange=None):
        kept["attn"], results = _attn_bwd_call(q, k, v, kept["outproj"][0], probs, p_sinks, cos_t, sin_t,
                                               exchange=exchange)
        return results

    def inproj_bwd(exchange=None):
        dq, dk, dv, _ = kept["attn"]
        kept["inproj"], results = _inproj_bwd_call(dq, dk, dv, kept["outproj"][1], x, dx1, w_in, small["g_mix"],
                                                   exchange=exchange)
        return results

    if on_mesh:
        ffn_grads = _reduce_to_owner(ffn_grads, place, (outproj_bwd, attn_bwd), [BF16] * 3)
    else:
        outproj_bwd(), attn_bwd()
    inproj_bwd()
    _, _, g_w_out, g_b_out, g_w_pool, g_b_pool, g_pool_scale = kept["outproj"]
    g_sinks = kept["attn"][3]
    grad_x, g_w_in, g_b_in, g_g_mix = kept["inproj"]
    small_grads = dict(g_mix=g_g_mix, b_in=g_b_in, sinks=g_sinks[:, 0], w_pool=g_w_pool,
                       b_pool=g_b_pool, pool_scale=g_pool_scale, b_out=g_b_out, g_ffn=g_g_ffn, g_final=g_g_final)
    big_grads = dict(w_in=g_w_in, w_out=g_w_out, w_gate=ffn_grads[0], w_up=ffn_grads[1], w_down=ffn_grads[2])
    return loss, grad_x, small_grads, big_grads


def _cast_place_call(shards, me):
    na = len(shards)
    split = 2

    def body(me_ref, *refs):
        for a in range(na):
            refs[na + a][...] = refs[a][...].astype(BF16)

    in_specs = [pl.BlockSpec((s.shape[0] // split, s.shape[1]), lambda i, me_ref: (i, 0)) for s in shards]
    out_specs = [pl.BlockSpec((None, s.shape[0] // split, s.shape[1]), lambda i, me_ref: (me_ref[0], i, 0))
                 for s in shards]
    return pl.pallas_call(
        body,
        grid_spec=pltpu.PrefetchScalarGridSpec(num_scalar_prefetch=1, grid=(split,), in_specs=in_specs,
                                               out_specs=out_specs),
        out_shape=[jax.ShapeDtypeStruct((N_SHARDS,) + s.shape, BF16) for s in shards],
        compiler_params=_params("parallel"), name="cast_place_weights",
    )(me.reshape(1), *shards)


def _pair_sum_call(grads, landed, c, wire_dtypes):
    na = len(grads)
    split = 2

    def body(c_ref, *refs):
        ins, lands, outs = refs[:na], refs[na:2 * na], refs[2 * na:]
        for a in range(na):
            outs[a][...] = (ins[a][...] + lands[a][...]).astype(outs[a].dtype)

    in_specs, land_specs, out_specs, views = [], [], [], []
    for g in grads:
        ns, r, cols = g.shape
        rows = r // 2 // split
        views.append(g.reshape(ns, 2, r // 2, cols))
        in_specs.append(pl.BlockSpec((None, None, rows, cols), lambda s, i, c_ref: (s, c_ref[0], i, 0)))
        land_specs.append(pl.BlockSpec((None, rows, cols), lambda s, i, c_ref: (s, i, 0)))
        out_specs.append(pl.BlockSpec((None, rows, cols), lambda s, i, c_ref: (s, i, 0)))
    return pl.pallas_call(
        body,
        grid_spec=pltpu.PrefetchScalarGridSpec(num_scalar_prefetch=1, grid=(N_SHARDS, split),
                                               in_specs=in_specs + land_specs, out_specs=out_specs),
        out_shape=[jax.ShapeDtypeStruct(l.shape, dt) for l, dt in zip(landed, wire_dtypes)],
        compiler_params=_params("parallel", "parallel"), name="reduce_pair_sum",
    )(c.reshape(1), *views, *landed)


def _chip_sum_call(partials, slots, me, c):
    na = len(slots)
    split = 2

    def body(place_ref, *refs):
        mine, lands, outs = refs[:na], refs[na:2 * na], refs[2 * na:]
        for j in range(N_SHARDS):
            @pl.when(place_ref[0] == j)
            def _():
                for a in range(na):
                    terms = [mine[a][...] if k == j else lands[a][k] for k in range(N_SHARDS)]
                    t0, t1, t2, t3 = (term.astype(F32) for term in terms)
                    outs[a][...] = ((t0 + t1) + t2) + t3

    mine_specs, land_specs, out_specs = [], [], []
    for s in slots:
        ns, h, cols = s.shape
        mine_specs.append(pl.BlockSpec((None, h // split, cols), lambda i, place: (place[0], i, 0)))
        land_specs.append(pl.BlockSpec((ns, h // split, cols), lambda i, place: (0, i, 0)))
        out_specs.append(pl.BlockSpec((h // split, cols), lambda i, place: (place[1] * split + i, 0)))
    return pl.pallas_call(
        body,
        grid_spec=pltpu.PrefetchScalarGridSpec(num_scalar_prefetch=1, grid=(split,),
                                               in_specs=mine_specs + land_specs, out_specs=out_specs),
        out_shape=[jax.ShapeDtypeStruct((2 * s.shape[1], s.shape[2]), F32) for s in slots],
        compiler_params=_params("parallel"), name="reduce_chip_sum",
    )(jnp.stack([me, c]), *partials, *slots)


def _adamw(w, g, m, v):
    m = ADAM_B1 * m + (1.0 - ADAM_B1) * g
    v = ADAM_B2 * v + (1.0 - ADAM_B2) * jnp.square(g)
    m_hat = m / (1.0 - ADAM_B1 ** ADAM_STEP)
    v_hat = v / (1.0 - ADAM_B2 ** ADAM_STEP)
    return -ADAM_LR * (m_hat / (jnp.sqrt(v_hat) + ADAM_EPS) + ADAM_WD * w), m, v


def _adamw_call(ws, gs, ms, vs):
    na = len(ws)
    split = 4

    def body(*refs):
        w_refs, g_refs, m_refs, v_refs = (refs[k * na:(k + 1) * na] for k in range(4))
        d_refs, nm_refs, nv_refs = (refs[(4 + k) * na:(5 + k) * na] for k in range(3))
        for a in range(na):
            d_refs[a][...], nm_refs[a][...], nv_refs[a][...] = _adamw(
                w_refs[a][...], g_refs[a][...], m_refs[a][...], v_refs[a][...])

    specs = [pl.BlockSpec((w.shape[0] // split, w.shape[1]), lambda i: (i, 0)) for w in ws]
    shapes = [jax.ShapeDtypeStruct(w.shape, F32) for w in ws]
    outs = pl.pallas_call(
        body, grid=(split,), in_specs=specs * 4, out_specs=specs * 3, out_shape=shapes * 3,
        compiler_params=_params("parallel"), name="adamw",
    )(*ws, *gs, *ms, *vs)
    return outs[:na], outs[na:2 * na], outs[2 * na:]


def _small_rows():
    first, row = {}, 0
    for name, (r, c) in SMALL_VIEWS.items():
        first[name] = row
        row += r * (-(-c // LANES))
    return first, row, -(-(row + 1) // 64) * 64


def _pack_small(arrays, loss):
    _, used, rows = _small_rows()
    parts = []
    for name, (r, c) in SMALL_VIEWS.items():
        a = arrays[name].reshape(r, c)
        parts.append(jnp.pad(a, ((0, 0), (0, LANES - c))) if c < LANES else a.reshape(r * c // LANES, LANES))
    parts += [loss, jnp.zeros((rows - used - 1, LANES), F32)]
    return jnp.concatenate(parts, axis=0)


def _adamw_small_call(ws, packed_grads, ms, vs):
    names = list(SMALL_VIEWS)
    first, _, _ = _small_rows()
    n = len(names)

    def body(g_ref, *refs):
        w_refs, m_refs, v_refs = refs[:n], refs[n:2 * n], refs[2 * n:3 * n]
        outs = refs[3 * n:]
        for a, name in enumerate(names):
            r, c = SMALL_VIEWS[name]
            r0 = first[name]
            if c == LANES:
                g = g_ref[r0:r0 + r, :]
            elif c < LANES:
                g = g_ref[r0:r0 + 1, 0:c]
            else:
                g = jnp.concatenate([g_ref[r0 + j:r0 + j + 1, :] for j in range(c // LANES)], axis=1)
            delta, m, v = _adamw(w_refs[a][...], g, m_refs[a][...], v_refs[a][...])
            for k, val in enumerate((g, delta, m, v)):
                outs[k * n + a][...] = val

    shapes = [jax.ShapeDtypeStruct(SMALL_VIEWS[name], F32) for name in names]
    outs = pl.pallas_call(
        body, out_shape=shapes * 4, compiler_params=pltpu.CompilerParams(vmem_limit_bytes=VMEM_LIMIT),
        name="adamw_small",
    )(packed_grads, *[ws[k] for k in names], *[ms[k] for k in names], *[vs[k] for k in names])
    return [dict(zip(names, outs[k * n:(k + 1) * n])) for k in range(4)]


def kernel(x, g_mix, w_in, b_in, sinks, w_pool, b_pool, pool_scale, w_out, b_out, g_ffn, w_gate, w_up, w_down, g_final, loss_target, m_g_mix, m_w_in, m_b_in, m_sinks, m_w_pool, m_b_pool, m_pool_scale, m_w_out, m_b_out, m_g_ffn, m_w_gate, m_w_up, m_w_down, m_g_final, v_g_mix, v_w_in, v_b_in, v_sinks, v_w_pool, v_b_pool, v_pool_scale, v_w_out, v_b_out, v_g_ffn, v_w_gate, v_w_up, v_w_down, v_g_final):
    given = dict(locals())
    big_names = ("w_in", "w_out", "w_gate", "w_up", "w_down")
    out_shapes = {n: given[n].shape for n in SMALL_NAMES + big_names}
    c = lax.axis_index("c")

    me = 2 * lax.axis_index("x") + lax.axis_index("y")
    transposed = ("w_in", "w_gate", "w_up")

    def shard_view(a, name):
        return a[0].T if name in transposed else a[0]

    placed = _cast_place_call([shard_view(given[n], n) for n in big_names], me)

    small = dict(g_mix=g_mix, b_in=b_in, sinks=sinks, w_pool=w_pool[0], b_pool=b_pool.reshape(1, POOL_WIDTH),
                 pool_scale=pool_scale.reshape(1, POOL_WIDTH), b_out=b_out, g_ffn=g_ffn,
                 g_final=g_final.reshape(1, D_MODEL))
    loss, grad_x, small_grads, big_grads = _local_step(
        x[0], loss_target[0], small, placed[0], placed[1], placed[2:], place=(me, c))

    packed = _pack_small(small_grads, loss)
    grads = [
        big_grads["w_in"].reshape(N_SHARDS, IN_WIDTH // N_SHARDS, D_MODEL),
        big_grads["w_out"].reshape(N_SHARDS, D_MODEL // N_SHARDS, D_MODEL),
        jnp.broadcast_to(packed[None], (N_SHARDS,) + packed.shape),
    ]
    alone = [functools.partial(_exchange_call, name=name) for name in
             ("reduce_sibling_exchange", "reduce_chip_exchange")]
    h_in, h_out, h_small = _reduce_to_owner(grads, (me, c), alone, [BF16, BF16, F32])
    totals = _exchange_call(_sibling_allgather(
        [h_in, h_out, big_grads["w_gate"], big_grads["w_up"], big_grads["w_down"], h_small]), "reduce_sibling_allgather")
    loss = totals[-1][_small_rows()[1], 0]

    def owned(prefix):
        return [shard_view(given[prefix + n], n) for n in big_names]

    def owned_small(prefix):
        return {n: given[prefix + n].reshape(SMALL_VIEWS[n]) for n in SMALL_NAMES}

    big = (totals[:-1],) + _adamw_call(owned(""), totals[:-1], owned("m_"), owned("v_"))
    small_groups = _adamw_small_call(owned_small(""), totals[-1], owned_small("m_"), owned_small("v_"))

    order = ("g_mix", "w_in", "b_in", "sinks", "w_pool", "b_pool", "pool_scale", "w_out", "b_out", "g_ffn",
             "w_gate", "w_up", "w_down", "g_final")
    results = [loss, grad_x.reshape(x.shape)]
    for group, small_group in zip(big, small_groups):
        named = {n: (a.T if n in transposed else a) for n, a in zip(big_names, group)}
        named.update(small_group)
        results += [named[n].reshape(out_shapes[n]) for n in order]
    return tuple(results)
```

```python
import functools
from typing import Any, Callable, Mapping, NamedTuple, Sequence

import jax
import jax.numpy as jnp
from jax import lax
from jax.experimental import pallas as pl
from jax.experimental.pallas import tpu as pltpu

F32 = jnp.float32
BF16 = jnp.bfloat16

D_MODEL = 1024
ATTN_WIDTH = 512
KV_WIDTH = 128
POOL_WIDTH = 512
IN_WIDTH = ATTN_WIDTH + 2 * KV_WIDTH + POOL_WIDTH
HEAD_DIM = 64
N_Q_HEADS = 8
BLOCK = 128
POOL_SIZES = (2, 4, 8, 16)
POOL_HALO = 16
ROPE_THETA = 10000.0
RMS_EPS = 1e-5
Q_SCALE = HEAD_DIM ** -0.5

ADAM_LR = 0.001
ADAM_B1 = 0.9
ADAM_B2 = 0.999
ADAM_EPS = 1e-08
ADAM_WD = 0.01
ADAM_STEP = 10

N_SHARDS = 4
LANES = 128
VMEM_LIMIT = 56 * 1024 * 1024
FFN_TOKEN_TILE = 256

MESH = pl.DeviceIdType.MESH
HBM_SPEC = pl.BlockSpec(memory_space=pltpu.HBM)

SMALL_VIEWS = dict(w_pool=(4 * LANES, LANES), b_pool=(4, LANES), pool_scale=(4, LANES), g_mix=(1, D_MODEL),
                   b_in=(1, IN_WIDTH), b_out=(1, D_MODEL), g_ffn=(1, D_MODEL), g_final=(1, D_MODEL),
                   sinks=(1, N_Q_HEADS))
SMALL_NAMES = tuple(SMALL_VIEWS)


def _dot(a, b):
    return jnp.dot(a, b, preferred_element_type=F32)


def _dot_nt(a, b):
    return lax.dot_general(a, b, (((1,), (1,)), ((), ())), preferred_element_type=F32)


def _dot_tn(a, b):
    return lax.dot_general(a, b, (((0,), (0,)), ((), ())), preferred_element_type=F32)


def _params(*semantics):
    return pltpu.CompilerParams(dimension_semantics=semantics, vmem_limit_bytes=VMEM_LIMIT)


def _resident(shape):
    return pl.BlockSpec(shape, lambda *_: (0,) * len(shape), pipeline_mode=pl.Buffered(1))


def _token_tile(t, largest=512):
    for tm in (2048, 1024, 512, 256, 128):
        if tm <= largest and t % tm == 0:
            return tm
    raise ValueError(f"sequence length {t} is not a multiple of 128")


def _rms_stats(xf):
    r = lax.rsqrt(jnp.mean(xf * xf, axis=-1, keepdims=True) + RMS_EPS)
    return xf * r, r


def _rms_bwd(dy, xh, r, g):
    dxh = dy * g
    return r * (dxh - xh * jnp.mean(dxh * xh, axis=-1, keepdims=True))


def _rope_partner(t, first):
    return jnp.where(first, pltpu.roll(t, LANES - HEAD_DIM // 2, 1), pltpu.roll(t, HEAD_DIM // 2, 1))


def _first_half_mask(shape):
    lane = lax.broadcasted_iota(jnp.int32, shape, 1)
    return (lane % HEAD_DIM) < (HEAD_DIM // 2)


def _rope_tables(t):
    inv_freq = 1.0 / (ROPE_THETA ** (jnp.arange(0, HEAD_DIM, 2, dtype=F32) / HEAD_DIM))
    ang = jnp.arange(t, dtype=F32)[:, None] * inv_freq[None, :]
    cos, sin = jnp.cos(ang), jnp.sin(ang)
    cos_t = jnp.tile(jnp.concatenate([cos, cos], axis=-1), (1, LANES // HEAD_DIM))
    sin_t = jnp.tile(jnp.concatenate([-sin, sin], axis=-1), (1, LANES // HEAD_DIM))
    return cos_t, sin_t


class _Exchange(NamedTuple):
    inputs: Sequence[Any]
    out_shapes: Sequence[Any]
    aliases: Mapping[int, int]
    sems: Sequence[Any]
    start: Callable[..., None]
    finish: Callable[..., None]
    relay: Callable[..., None] = None
    relay_early: int = 0


def _compute_call(body, operands, *, grid, in_specs, out_specs, out_shape, scratch_shapes=(), semantics, name,
                  exchange=None):
    n_in, n_out, n_scr = len(in_specs), len(out_specs), len(scratch_shapes)
    if exchange is None:
        return pl.pallas_call(body, grid=grid, in_specs=in_specs, out_specs=out_specs, out_shape=out_shape,
                              scratch_shapes=scratch_shapes, compiler_params=_params(*semantics),
                              name=name)(*operands), []
    k_in, k_out = len(exchange.inputs), len(exchange.out_shapes)

    def hosting(*refs):
        ins, rest = refs[:n_in], refs[n_in:]
        ex_ins, rest = rest[:k_in], rest[k_in:]
        outs, rest = rest[:n_out], rest[n_out:]
        ex_outs, rest = rest[:k_out], rest[k_out:]
        scratch, sems = rest[:n_scr], rest[n_scr:]
        steps = [pl.program_id(d) for d in range(len(grid))]
        is_first = functools.reduce(jnp.logical_and, [s == 0 for s in steps])
        is_last = functools.reduce(jnp.logical_and, [s == g - 1 for s, g in zip(steps, grid)])
        is_relay = functools.reduce(jnp.logical_and, [s == max(g - 1 - exchange.relay_early, 0)
                                                      for s, g in zip(steps, grid)])

        @pl.when(is_first)
        def _():
            exchange.start(ex_ins, ex_outs, sems)

        body(*ins, *outs, *scratch)

        if exchange.relay is not None:
            @pl.when(is_relay)
            def _():
                exchange.relay(ex_ins, ex_outs, sems)

        @pl.when(is_last)
        def _():
            exchange.finish(ex_ins, ex_outs, sems)

    results = pl.pallas_call(
        hosting, grid=grid,
        in_specs=list(in_specs) + [HBM_SPEC] * k_in, out_specs=list(out_specs) + [HBM_SPEC] * k_out,
        out_shape=list(out_shape) + list(exchange.out_shapes),
        scratch_shapes=list(scratch_shapes) + list(exchange.sems),
        input_output_aliases={n_in + i: n_out + j for i, j in exchange.aliases.items()},
        compiler_params=_params(*semantics), name=name,
    )(*operands, *exchange.inputs)
    return results[:n_out], results[n_out:]


def _exchange_call(exchange, name):
    n_in, n_out = len(exchange.inputs), len(exchange.out_shapes)

    def body(*refs):
        ins, outs, sems = refs[:n_in], refs[n_in:n_in + n_out], refs[n_in + n_out:]
        exchange.start(ins, outs, sems)
        if exchange.relay is not None:
            exchange.relay(ins, outs, sems)
        exchange.finish(ins, outs, sems)

    return pl.pallas_call(
        body, in_specs=[HBM_SPEC] * n_in, out_specs=[HBM_SPEC] * n_out, out_shape=list(exchange.out_shapes),
        input_output_aliases=dict(exchange.aliases), scratch_shapes=list(exchange.sems),
        compiler_params=pltpu.CompilerParams(has_side_effects=True), name=name,
    )(*exchange.inputs)


def _mesh_place():
    x, y, c = lax.axis_index("x"), lax.axis_index("y"), lax.axis_index("c")
    other_chips = [(1 - x, y), (x, 1 - y), (1 - x, 1 - y)]
    return x, y, c, other_chips


def _half(ref, c, rows):
    return ref.at[pl.ds(pl.multiple_of(c * rows, 16), rows), :]


def _remote(src, dst, send_sems, recv_sems, idx, to):
    return functools.partial(pltpu.make_async_remote_copy, src_ref=src, dst_ref=dst, send_sem=send_sems.at[idx],
                             recv_sem=recv_sems.at[idx], device_id=to, device_id_type=MESH)


def _gather_exchange(bufs, relay_early=0):
    na = len(bufs)

    def copies(outs, sems):
        send_sems, recv_sems = sems
        x, y, c, other_chips = _mesh_place()
        me, sibling = 2 * x + y, (x, y, 1 - c)
        ici_out, ici_in, fwd_out, fwd_in = [], [], [], []
        for j, (cx, cy) in enumerate(other_chips):
            for a in range(na):
                rows = outs[a].shape[1] // 2
                mine = _half(outs[a].at[me], c, rows)
                landed = _half(outs[a].at[2 * cx + cy], c, rows)
                passed = _half(outs[a].at[2 * cx + cy], 1 - c, rows)
                ici_out.append(_remote(mine, mine, send_sems, recv_sems, j * na + a, (cx, cy, c)))
                ici_in.append(_remote(landed, landed, send_sems, recv_sems, j * na + a, (cx, cy, c)))
                fwd_out.append(_remote(landed, landed, send_sems, recv_sems, (3 + j) * na + a, sibling))
                fwd_in.append(_remote(passed, passed, send_sems, recv_sems, (3 + j) * na + a, sibling))
        return ici_out, ici_in, fwd_out, fwd_in

    def start(ins, outs, sems):
        for cp in copies(outs, sems)[0]:
            cp().start()

    def relay(ins, outs, sems):
        _, ici_in, fwd_out, _ = copies(outs, sems)
        for arrived, forward in zip(ici_in, fwd_out):
            arrived().wait_recv()
            forward().start()

    def finish(ins, outs, sems):
        ici_out, _, fwd_out, fwd_in = copies(outs, sems)
        for cp in fwd_in:
            cp().wait_recv()
        for cp in ici_out + fwd_out:
            cp().wait_send()

    return _Exchange(inputs=bufs, out_shapes=[jax.ShapeDtypeStruct(b.shape, b.dtype) for b in bufs],
                     aliases={a: a for a in range(na)},
                     sems=[pltpu.SemaphoreType.DMA((6 * na,)), pltpu.SemaphoreType.DMA((6 * na,))],
                     start=start, finish=finish, relay=relay, relay_early=relay_early)


def _sibling_exchange(grads):
    na = len(grads)

    def copies(ins, outs, sems):
        x, y, c, _ = _mesh_place()
        out = []
        for a in range(na):
            rows = ins[a].shape[1] // 2
            src = ins[a].at[:, pl.ds(pl.multiple_of((1 - c) * rows, 8), rows), :]
            out.append(_remote(src, outs[a], sems[0], sems[1], a, (x, y, 1 - c)))
        return out

    def start(ins, outs, sems):
        for cp in copies(ins, outs, sems):
            cp().start()

    def finish(ins, outs, sems):
        for cp in copies(ins, outs, sems):
            cp().wait_recv()
            cp().wait_send()

    return _Exchange(inputs=grads,
                     out_shapes=[jax.ShapeDtypeStruct((g.shape[0], g.shape[1] // 2, g.shape[2]), g.dtype)
                                 for g in grads],
                     aliases={}, sems=[pltpu.SemaphoreType.DMA((na,)), pltpu.SemaphoreType.DMA((na,))],
                     start=start, finish=finish)


def _chip_exchange(partials):
    na = len(partials)

    def copies(ins, outs, sems):
        x, y, c, other_chips = _mesh_place()
        me = 2 * x + y
        sends, arrivals = [], []
        for j, (cx, cy) in enumerate(other_chips):
            for a in range(na):
                slot = outs[a].at[2 * cx + cy]
                sends.append(_remote(ins[a].at[2 * cx + cy], outs[a].at[me], sems[0], sems[1], j * na + a, (cx, cy, c)))
                arrivals.append(_remote(slot, slot, sems[0], sems[1], j * na + a, (cx, cy, c)))
        return sends, arrivals

    def start(ins, outs, sems):
        for cp in copies(ins, outs, sems)[0]:
            cp().start()

    def finish(ins, outs, sems):
        sends, arrivals = copies(ins, outs, sems)
        for cp in arrivals:
            cp().wait_recv()
        for cp in sends:
            cp().wait_send()

    return _Exchange(inputs=partials, out_shapes=[jax.ShapeDtypeStruct(p.shape, p.dtype) for p in partials],
                     aliases={}, sems=[pltpu.SemaphoreType.DMA((3 * na,)), pltpu.SemaphoreType.DMA((3 * na,))],
                     start=start, finish=finish)


def _sibling_allgather(bufs):
    na = len(bufs)

    def piece(outs, sems, a, half):
        x, y, c, _ = _mesh_place()
        rows = outs[a].shape[0] // 2
        rows_ref = outs[a].at[pl.ds(pl.multiple_of(half * rows, 8), rows), :]
        return _remote(rows_ref, rows_ref, sems[0], sems[1], a, (x, y, 1 - c))

    def start(ins, outs, sems):
        c = lax.axis_index("c")
        for a in range(na):
            piece(outs, sems, a, c)().start()

    def finish(ins, outs, sems):
        c = lax.axis_index("c")
        for a in range(na):
            piece(outs, sems, a, 1 - c)().wait_recv()
        for a in range(na):
            piece(outs, sems, a, c)().wait_send()

    return _Exchange(inputs=bufs, out_shapes=[jax.ShapeDtypeStruct(b.shape, b.dtype) for b in bufs],
                     aliases={a: a for a in range(na)},
                     sems=[pltpu.SemaphoreType.DMA((na,)), pltpu.SemaphoreType.DMA((na,))],
                     start=start, finish=finish)


def _norm_call(x, g_mix, exchange=None):
    t = x.shape[0]
    tm = _token_tile(t)

    def body(x_ref, g_ref, h_ref):
        xh, _ = _rms_stats(x_ref[...])
        h_ref[...] = (xh * g_ref[...]).astype(BF16)

    row = pl.BlockSpec((tm, D_MODEL), lambda i: (i, 0))
    return _compute_call(body, (x, g_mix), grid=(t // tm,), in_specs=[row, _resident((1, D_MODEL))],
                         out_specs=[row], out_shape=[jax.ShapeDtypeStruct((t, D_MODEL), BF16)],
                         semantics=("parallel",), name="norm_fwd", exchange=exchange)


def _inproj_call(h0, w_in, b_in, cos_t, sin_t, exchange=None):
    t = h0.shape[0]
    tm = _token_tile(t, 1024)

    def body(h_ref, w_ref, b_ref, c_ref, s_ref, q_ref, k_ref, v_ref, u_ref):
        z = _dot_nt(h_ref[...], w_ref[...]) + b_ref[...]
        cos, sin = c_ref[...], s_ref[...]
        first = _first_half_mask((tm, LANES))
        for j in range(ATTN_WIDTH // LANES):
            zj = z[:, LANES * j:LANES * (j + 1)]
            q_ref[:, LANES * j:LANES * (j + 1)] = ((zj * cos + _rope_partner(zj, first) * sin) * Q_SCALE).astype(BF16)
        zk = z[:, ATTN_WIDTH:ATTN_WIDTH + KV_WIDTH]
        k_ref[...] = (zk * cos + _rope_partner(zk, first) * sin).astype(BF16)
        v_ref[...] = z[:, ATTN_WIDTH + KV_WIDTH:ATTN_WIDTH + 2 * KV_WIDTH].astype(BF16)
        u_ref[...] = z[:, ATTN_WIDTH + 2 * KV_WIDTH:]

    row = lambda w: pl.BlockSpec((tm, w), lambda i: (i, 0))
    return _compute_call(
        body, (h0, w_in, b_in, cos_t, sin_t), grid=(t // tm,),
        in_specs=[row(D_MODEL), _resident((IN_WIDTH, D_MODEL)), _resident((1, IN_WIDTH)), row(LANES), row(LANES)],
        out_specs=[row(ATTN_WIDTH), row(KV_WIDTH), row(KV_WIDTH), row(POOL_WIDTH)],
        out_shape=[jax.ShapeDtypeStruct((t, ATTN_WIDTH), BF16),
                   jax.ShapeDtypeStruct((t, KV_WIDTH), BF16), jax.ShapeDtypeStruct((t, KV_WIDTH), BF16),
                   jax.ShapeDtypeStruct((t, POOL_WIDTH), F32)],
        semantics=("parallel",), name="inproj_fwd", exchange=exchange)


def _head_in_both_halves(band, kv_head):
    lane = lax.broadcasted_iota(jnp.int32, band.shape, 1)
    own = (lane < HEAD_DIM) if kv_head == 0 else (lane >= HEAD_DIM)
    return jnp.where(own, band, pltpu.roll(band, HEAD_DIM, 1)).astype(BF16)


def _stack_heads(ref, kv_head, row0):
    low = lax.broadcasted_iota(jnp.int32, (BLOCK, LANES), 1) < HEAD_DIM
    parts = []
    for jj in range(2):
        slab = ref[row0:row0 + BLOCK, LANES * (2 * kv_head + jj):LANES * (2 * kv_head + jj + 1)]
        zero = jnp.zeros_like(slab)
        parts += [jnp.where(low, slab, zero), jnp.where(low, zero, slab)]
    return jnp.concatenate(parts, axis=0)


def _band_biases():
    c = lax.broadcasted_iota(jnp.int32, (2 * BLOCK, 4 * BLOCK), 0)
    i = lax.broadcasted_iota(jnp.int32, (2 * BLOCK, 4 * BLOCK), 1) % BLOCK
    vis = (c > i) & (c <= i + BLOCK)
    return jnp.where(vis, 0.0, -jnp.inf).astype(F32), jnp.where(vis & (c >= BLOCK), 0.0, -jnp.inf).astype(F32)


def _blocks_per_step(nb):
    for g in (8, 4, 2, 1):
        if nb % g == 0:
            return g


def _sink_row(sink_ref, kv_head):
    return jnp.concatenate([jnp.full((1, BLOCK), sink_ref[0, 4 * kv_head + g], F32) for g in range(4)], axis=1)


def _kv_band(cur_ref, prev_ref, g):
    r0 = g * BLOCK
    prev = prev_ref[...] if g == 0 else cur_ref[r0 - BLOCK:r0, :]
    return jnp.concatenate([prev, cur_ref[r0:r0 + BLOCK, :]], axis=0).astype(F32)


def _attn_fwd_call(q, k, v, sinks, bias, bias_first, exchange=None):
    t = q.shape[0]
    nb = t // BLOCK
    gb = _blocks_per_step(nb)
    qb = gb * BLOCK

    def body(sink_ref, q_ref, kc_ref, kp_ref, vc_ref, vp_ref, b_ref, bf_ref, o_ref, p_ref, psink_ref):
        n = pl.program_id(0)
        row_low = lax.broadcasted_iota(jnp.int32, (LANES, BLOCK), 0) < HEAD_DIM
        for g in range(gb):
            r0 = g * BLOCK
            kband = _kv_band(kc_ref, kp_ref, g)
            vband_t = _kv_band(vc_ref, vp_ref, g).T
            bias_g = jnp.where(n == 0, bf_ref[...], b_ref[...]) if g == 0 else b_ref[...]
            for kh in range(2):
                k2 = _head_in_both_halves(kband, kh)
                vt_head = vband_t[HEAD_DIM * kh:HEAD_DIM * (kh + 1), :]
                v2_t = jnp.concatenate([vt_head, vt_head], axis=0).astype(BF16)
                s_t = _dot_nt(k2, _stack_heads(q_ref, kh, r0)) + bias_g
                sink = _sink_row(sink_ref, kh)
                m = jnp.maximum(jnp.max(s_t, axis=0, keepdims=True), sink)
                e = jnp.exp(s_t - m)
                e_sink = jnp.exp(sink - m)
                inv = 1.0 / (jnp.sum(e, axis=0, keepdims=True) + e_sink)
                p_t = (e * inv).astype(BF16)
                p_ref[2 * g + kh] = p_t
                psink_ref[2 * g + kh] = e_sink * inv
                o_t = _dot(v2_t, p_t)
                for jj in range(2):
                    slab = jnp.where(row_low, o_t[:, BLOCK * 2 * jj:BLOCK * (2 * jj + 1)],
                                     o_t[:, BLOCK * (2 * jj + 1):BLOCK * (2 * jj + 2)]).T
                    o_ref[r0:r0 + BLOCK, LANES * (2 * kh + jj):LANES * (2 * kh + jj + 1)] = slab.astype(BF16)

    cur = lambda w: pl.BlockSpec((qb, w), lambda n: (n, 0))
    prev = lambda w: pl.BlockSpec((BLOCK, w), lambda n: (jnp.maximum(n * gb - 1, 0), 0))
    return _compute_call(
        body, (sinks, q, k, k, v, v, bias, bias_first), grid=(nb // gb,),
        in_specs=[pl.BlockSpec(memory_space=pltpu.SMEM), cur(ATTN_WIDTH), cur(KV_WIDTH), prev(KV_WIDTH),
                  cur(KV_WIDTH), prev(KV_WIDTH), _resident(bias.shape), _resident(bias.shape)],
        out_specs=[cur(ATTN_WIDTH), pl.BlockSpec((2 * gb, 2 * BLOCK, 4 * BLOCK), lambda n: (n, 0, 0)),
                   pl.BlockSpec((2 * gb, 1, 4 * BLOCK), lambda n: (n, 0, 0))],
        out_shape=[jax.ShapeDtypeStruct((t, ATTN_WIDTH), BF16),
                   jax.ShapeDtypeStruct((2 * nb, 2 * BLOCK, 4 * BLOCK), BF16),
                   jax.ShapeDtypeStruct((2 * nb, 1, 4 * BLOCK), F32)],
        semantics=("parallel",), name="attn_fwd", exchange=exchange)


def _pool_mixed(sc_ref, u_ref, halo_ref, i, tm):
    sc_ref[0:POOL_HALO, :] = jnp.where(i == 0, 0.0, halo_ref[...])
    sc_ref[POOL_HALO:, :] = u_ref[...]
    tok = i * tm + lax.broadcasted_iota(jnp.int32, (tm, 1), 0)
    mixed = []
    for g, size in enumerate(POOL_SIZES):
        lanes = slice(LANES * g, LANES * (g + 1))
        cur = sc_ref[pl.ds(POOL_HALO, tm), lanes]
        acc = cur
        for j in range(1, size):
            acc = acc + sc_ref[pl.ds(POOL_HALO - j, tm), lanes]
        count = jnp.minimum(tok + 1, size).astype(F32)
        mixed.append(acc / count - cur)
    return mixed


def _halo_before(tm, width):
    return pl.BlockSpec((POOL_HALO, width), lambda i: (jnp.maximum(i * (tm // POOL_HALO) - 1, 0), 0))


def _outproj_call(attn, u, x, w_pool, b_pool, pool_scale, w_out, b_out, g_ffn, exchange=None):
    t = x.shape[0]
    tm = _token_tile(t)

    def body(a_ref, u_ref, halo_ref, x_ref, wp_ref, bp_ref, ps_ref, wo_ref, bo_ref, g_ref,
             x1_ref, h1_ref, sc_ref, y_ref):
        i = pl.program_id(0)
        for g, mixed in enumerate(_pool_mixed(sc_ref, u_ref, halo_ref, i, tm)):
            lanes = slice(LANES * g, LANES * (g + 1))
            lin = _dot(mixed.astype(BF16), wp_ref[g]) + bp_ref[:, lanes]
            y_ref[:, lanes] = (lin * ps_ref[:, lanes]).astype(BF16)
        x1 = (x_ref[...] + _dot(a_ref[...], wo_ref[0:ATTN_WIDTH, :]) + _dot(y_ref[...], wo_ref[ATTN_WIDTH:, :])
              + bo_ref[...])
        x1_ref[...] = x1
        xh, _ = _rms_stats(x1)
        h1_ref[...] = (xh * g_ref[...]).astype(BF16)

    row = lambda w: pl.BlockSpec((tm, w), lambda i: (i, 0))
    return _compute_call(
        body, (attn, u, u, x, w_pool, b_pool, pool_scale, w_out, b_out, g_ffn), grid=(t // tm,),
        in_specs=[row(ATTN_WIDTH), row(POOL_WIDTH), _halo_before(tm, POOL_WIDTH), row(D_MODEL),
                  _resident((4, LANES, LANES)), _resident((1, POOL_WIDTH)), _resident((1, POOL_WIDTH)),
                  _resident((D_MODEL, D_MODEL)), _resident((1, D_MODEL)), _resident((1, D_MODEL))],
        out_specs=[row(D_MODEL), row(D_MODEL)],
        out_shape=[jax.ShapeDtypeStruct((t, D_MODEL), F32), jax.ShapeDtypeStruct((t, D_MODEL), BF16)],
        scratch_shapes=[pltpu.VMEM((tm + POOL_HALO, POOL_WIDTH), F32), pltpu.VMEM((tm, POOL_WIDTH), BF16)],
        semantics=("parallel",), name="outproj_fwd", exchange=exchange)


def _ffn_call(h1, x1, target, w_gate, w_up, w_down, g_ffn, g_final):
    t = x1.shape[0]
    tm = _token_tile(t, FFN_TOKEN_TILE)
    ns, fs, _ = w_gate.shape

    def body(h_ref, x1_ref, tgt_ref, wg_ref, wu_ref, wd_ref, g_ref, gf_ref,
             act_ref, dgate_ref, dup_ref, dx1_ref, dx2b_ref, loss_ref, dgf_ref, dg_ref, gate_ref, up_ref):
        @pl.when(pl.program_id(0) == 0)
        def _():
            loss_ref[...] = jnp.zeros_like(loss_ref)
            dgf_ref[...] = jnp.zeros_like(dgf_ref)
            dg_ref[...] = jnp.zeros_like(dg_ref)

        h = h_ref[...]
        x1 = x1_ref[...]
        x2 = x1
        for s in range(ns):
            gate = _dot_nt(h, wg_ref[s])
            up = _dot_nt(h, wu_ref[s])
            act = (gate * jax.nn.sigmoid(gate) * up).astype(BF16)
            gate_ref[s] = gate
            up_ref[s] = up
            act_ref[s] = act
            x2 = x2 + _dot(act, wd_ref[s])
        xh2, r2 = _rms_stats(x2)
        gf = gf_ref[...]
        err = xh2 * gf - tgt_ref[...]
        loss_ref[...] += 0.5 * jnp.sum(err * err) / D_MODEL
        dy = err / D_MODEL
        dgf_ref[...] += jnp.sum(dy * xh2, axis=0, keepdims=True)
        dx2 = _rms_bwd(dy, xh2, r2, gf)
        dx2b = dx2.astype(BF16)
        dx2b_ref[...] = dx2b
        dh = None
        for s in range(ns):
            dact = _dot_nt(dx2b, wd_ref[s])
            gate = gate_ref[s]
            sg = jax.nn.sigmoid(gate)
            dup = (dact * (gate * sg)).astype(BF16)
            dgate = (dact * up_ref[s] * (sg * (1.0 + gate * (1.0 - sg)))).astype(BF16)
            dgate_ref[s] = dgate
            dup_ref[s] = dup
            part = _dot(dgate, wg_ref[s]) + _dot(dup, wu_ref[s])
            dh = part if dh is None else dh + part
        xh1, r1 = _rms_stats(x1)
        dg_ref[...] += jnp.sum(dh * xh1, axis=0, keepdims=True)
        dx1_ref[...] = dx2 + _rms_bwd(dh, xh1, r1, g_ref[...])

    row = lambda w: pl.BlockSpec((tm, w), lambda i: (i, 0))
    shard_act = pl.BlockSpec((ns, tm, fs), lambda i: (0, i, 0))
    act_shape = jax.ShapeDtypeStruct((ns, t, fs), BF16)
    vec = pl.BlockSpec((1, D_MODEL), lambda i: (0, 0))
    return pl.pallas_call(
        body, grid=(t // tm,),
        in_specs=[row(D_MODEL), row(D_MODEL), row(D_MODEL), _resident((ns, fs, D_MODEL)),
                  _resident((ns, fs, D_MODEL)), _resident((ns, fs, D_MODEL)), _resident((1, D_MODEL)),
                  _resident((1, D_MODEL))],
        out_specs=[shard_act, shard_act, shard_act, row(D_MODEL), row(D_MODEL),
                   pl.BlockSpec((1, LANES), lambda i: (0, 0)), vec, vec],
        out_shape=[act_shape, act_shape, act_shape, jax.ShapeDtypeStruct((t, D_MODEL), F32),
                   jax.ShapeDtypeStruct((t, D_MODEL), BF16), jax.ShapeDtypeStruct((1, LANES), F32),
                   jax.ShapeDtypeStruct((1, D_MODEL), F32), jax.ShapeDtypeStruct((1, D_MODEL), F32)],
        scratch_shapes=[pltpu.VMEM((ns, tm, fs), F32), pltpu.VMEM((ns, tm, fs), F32)],
        compiler_params=_params("arbitrary"), name="ffn_fwd_bwd",
    )(h1, x1, target, w_gate, w_up, w_down, g_ffn, g_final)


def _ffn_bwd_weights_call(h1, dgate, dup, act, dx2):
    t = h1.shape[0]
    tm = _token_tile(t, 2048)
    ns, _, fs = dgate.shape

    def body(h_ref, dgate_ref, dup_ref, act_ref, dx2_ref, gg_ref, gu_ref, gd_ref):
        @pl.when(pl.program_id(1) == 0)
        def _():
            gg_ref[...] = jnp.zeros_like(gg_ref)
            gu_ref[...] = jnp.zeros_like(gu_ref)
            gd_ref[...] = jnp.zeros_like(gd_ref)

        h = h_ref[...]
        gg_ref[...] += _dot_tn(dgate_ref[...], h)
        gu_ref[...] += _dot_tn(dup_ref[...], h)
        gd_ref[...] += _dot_tn(act_ref[...], dx2_ref[...])

    row = lambda w: pl.BlockSpec((tm, w), lambda s, i: (i, 0))
    shard_act = pl.BlockSpec((None, tm, fs), lambda s, i: (s, i, 0))
    return pl.pallas_call(
        body, grid=(ns, t // tm),
        in_specs=[row(D_MODEL), shard_act, shard_act, shard_act, row(D_MODEL)],
        out_specs=[pl.BlockSpec((None, fs, D_MODEL), lambda s, i: (s, 0, 0))] * 3,
        out_shape=[jax.ShapeDtypeStruct((ns, fs, D_MODEL), F32)] * 3,
        compiler_params=_params("parallel", "arbitrary"), name="ffn_bwd_weights",
    )(h1, dgate, dup, act, dx2)


def _outproj_bwd_call(dx1, attn, u, w_pool, b_pool, pool_scale, w_out, exchange=None):
    t = dx1.shape[0]
    tm = _token_tile(t, 1024)

    def body(dx1_ref, a_ref, u_ref, halo_ref, wp_ref, bp_ref, ps_ref, wo_ref,
             dattn_ref, dmixed_ref, gwo_ref, gbo_ref, gwp_ref, gbp_ref, gps_ref, sc_ref, y_ref):
        i = pl.program_id(0)

        @pl.when(i == 0)
        def _():
            for ref in (gwo_ref, gbo_ref, gwp_ref, gbp_ref, gps_ref):
                ref[...] = jnp.zeros_like(ref)

        dx1 = dx1_ref[...]
        dx1b = dx1.astype(BF16)
        dcat = _dot_nt(dx1b, wo_ref[...])
        dattn_ref[...] = dcat[:, :ATTN_WIDTH].astype(BF16)
        for g, mixed in enumerate(_pool_mixed(sc_ref, u_ref, halo_ref, i, tm)):
            lanes = slice(LANES * g, LANES * (g + 1))
            mixed_b = mixed.astype(BF16)
            lin = _dot(mixed_b, wp_ref[g]) + bp_ref[:, lanes]
            y_ref[:, lanes] = (lin * ps_ref[:, lanes]).astype(BF16)
            dpool = dcat[:, ATTN_WIDTH + LANES * g:ATTN_WIDTH + LANES * (g + 1)]
            gps_ref[:, lanes] += jnp.sum(dpool * lin, axis=0, keepdims=True)
            dy = dpool * ps_ref[:, lanes]
            gbp_ref[:, lanes] += jnp.sum(dy, axis=0, keepdims=True)
            dy_b = dy.astype(BF16)
            gwp_ref[g] += _dot_tn(mixed_b, dy_b)
            dmixed_ref[:, lanes] = _dot_nt(dy_b, wp_ref[g])
        gwo_ref[0:ATTN_WIDTH, :] += _dot_tn(a_ref[...], dx1b)
        gwo_ref[ATTN_WIDTH:, :] += _dot_tn(y_ref[...], dx1b)
        gbo_ref[...] += jnp.sum(dx1, axis=0, keepdims=True)

    row = lambda w: pl.BlockSpec((tm, w), lambda i: (i, 0))
    const = lambda shape: pl.BlockSpec(shape, lambda i: (0,) * len(shape))
    return _compute_call(
        body, (dx1, attn, u, u, w_pool, b_pool, pool_scale, w_out), grid=(t // tm,),
        in_specs=[row(D_MODEL), row(ATTN_WIDTH), row(POOL_WIDTH), _halo_before(tm, POOL_WIDTH),
                  _resident((4, LANES, LANES)), _resident((1, POOL_WIDTH)), _resident((1, POOL_WIDTH)),
                  _resident((D_MODEL, D_MODEL))],
        out_specs=[row(ATTN_WIDTH), row(POOL_WIDTH), const((D_MODEL, D_MODEL)), const((1, D_MODEL)),
                   const((4, LANES, LANES)), const((1, POOL_WIDTH)), const((1, POOL_WIDTH))],
        out_shape=[jax.ShapeDtypeStruct((t, ATTN_WIDTH), BF16), jax.ShapeDtypeStruct((t, POOL_WIDTH), F32),
                   jax.ShapeDtypeStruct((D_MODEL, D_MODEL), F32), jax.ShapeDtypeStruct((1, D_MODEL), F32),
                   jax.ShapeDtypeStruct((4, LANES, LANES), F32), jax.ShapeDtypeStruct((1, POOL_WIDTH), F32),
                   jax.ShapeDtypeStruct((1, POOL_WIDTH), F32)],
        scratch_shapes=[pltpu.VMEM((tm + POOL_HALO, POOL_WIDTH), F32), pltpu.VMEM((tm, POOL_WIDTH), BF16)],
        semantics=("arbitrary",), name="outproj_bwd", exchange=exchange)


def _attn_bwd_call(q, k, v, dout, probs, p_sinks, cos_t, sin_t, exchange=None):
    t = q.shape[0]
    nb = t // BLOCK
    gb = _blocks_per_step(nb)
    qb = gb * BLOCK
    steps = nb // gb

    def body(q_ref, kc_ref, kp_ref, vc_ref, vp_ref, do_ref, p_ref, psink_ref, cq_ref, sq_ref, ck_ref, sk_ref,
             dq_ref, dk_ref, dv_ref, dsink_ref, newk_ref, newv_ref, carryk_ref, carryv_ref, headk_ref, headv_ref):
        n = pl.program_id(0)
        first = _first_half_mask((BLOCK, LANES))

        @pl.when(n == 0)
        def _():
            dsink_ref[...] = jnp.zeros_like(dsink_ref)

        @pl.when(n < steps)
        def _():
            head_row = lax.broadcasted_iota(jnp.int32, dsink_ref.shape, 0)
            lane2 = lax.broadcasted_iota(jnp.int32, (2 * BLOCK, LANES), 1)
            row_low = lax.broadcasted_iota(jnp.int32, (LANES, BLOCK), 0) < HEAD_DIM
            dsink = jnp.zeros(dsink_ref.shape, F32)
            for g in range(gb):
                r0 = g * BLOCK
                kband = _kv_band(kc_ref, kp_ref, g)
                vband = _kv_band(vc_ref, vp_ref, g)
                kband_t = kband.T
                cq, sq = cq_ref[r0:r0 + BLOCK, :], sq_ref[r0:r0 + BLOCK, :]
                dk_heads, dv_heads = [], []
                for kh in range(2):
                    v2 = _head_in_both_halves(vband, kh)
                    kt_head = kband_t[HEAD_DIM * kh:HEAD_DIM * (kh + 1), :]
                    k2_t = jnp.concatenate([kt_head, kt_head], axis=0).astype(BF16)
                    qst = _stack_heads(q_ref, kh, r0)
                    dost = _stack_heads(do_ref, kh, r0)
                    p_b = p_ref[2 * g + kh]
                    p_t = p_b.astype(F32)
                    dp_t = _dot_nt(v2, dost)
                    delta = jnp.sum(p_t * dp_t, axis=0, keepdims=True)
                    ds_t = (p_t * (dp_t - delta)).astype(BF16)
                    sink_term = psink_ref[2 * g + kh] * delta
                    for h in range(4):
                        val = -jnp.sum(sink_term[:, BLOCK * h:BLOCK * (h + 1)])
                        dsink = dsink + jnp.where(head_row == 4 * kh + h, val, 0.0)
                    dq_t = _dot(k2_t, ds_t)
                    for jj in range(2):
                        slab = jnp.where(row_low, dq_t[:, BLOCK * 2 * jj:BLOCK * (2 * jj + 1)],
                                         dq_t[:, BLOCK * (2 * jj + 1):BLOCK * (2 * jj + 2)]).T
                        dq_ref[r0:r0 + BLOCK, LANES * (2 * kh + jj):LANES * (2 * kh + jj + 1)] = (
                            (slab * cq + _rope_partner(slab * sq, first)) * Q_SCALE).astype(BF16)
                    dk2 = _dot(ds_t, qst)
                    dv2 = _dot(p_b, dost)
                    dk_heads.append(dk2 + pltpu.roll(dk2, HEAD_DIM, 1))
                    dv_heads.append(dv2 + pltpu.roll(dv2, HEAD_DIM, 1))
                bandk = jnp.where(lane2 < HEAD_DIM, dk_heads[0], dk_heads[1])
                bandv = jnp.where(lane2 < HEAD_DIM, dv_heads[0], dv_heads[1])
                if g == 0:
                    headk_ref[...] = bandk[0:BLOCK]
                    headv_ref[...] = bandv[0:BLOCK]
                else:
                    newk_ref[r0 - BLOCK:r0, :] += bandk[0:BLOCK]
                    newv_ref[r0 - BLOCK:r0, :] += bandv[0:BLOCK]
                newk_ref[r0:r0 + BLOCK, :] = bandk[BLOCK:]
                newv_ref[r0:r0 + BLOCK, :] = bandv[BLOCK:]
            dsink_ref[...] += dsink

        @pl.when(n == steps)
        def _():
            headk_ref[...] = jnp.zeros_like(headk_ref)
            headv_ref[...] = jnp.zeros_like(headv_ref)

        @pl.when(n >= 1)
        def _():
            last = qb - BLOCK
            carryk_ref[last:, :] += headk_ref[...]
            carryv_ref[last:, :] += headv_ref[...]
            for g in range(gb):
                rows = slice(g * BLOCK, (g + 1) * BLOCK)
                dk = carryk_ref[rows, :]
                dk_ref[rows, :] = (dk * ck_ref[rows, :] + _rope_partner(dk * sk_ref[rows, :], first)).astype(BF16)
            dv_ref[...] = carryv_ref[...].astype(BF16)

        @pl.when(n < steps)
        def _():
            carryk_ref[...] = newk_ref[...]
            carryv_ref[...] = newv_ref[...]

    cur = lambda w: pl.BlockSpec((qb, w), lambda n: (jnp.minimum(n, steps - 1), 0))
    late = lambda w: pl.BlockSpec((qb, w), lambda n: (jnp.maximum(n - 1, 0), 0))
    prev = lambda w: pl.BlockSpec((BLOCK, w), lambda n: (jnp.maximum(jnp.minimum(n, steps - 1) * gb - 1, 0), 0))
    return _compute_call(
        body, (q, k, k, v, v, dout, probs, p_sinks, cos_t, sin_t, cos_t, sin_t), grid=(steps + 1,),
        in_specs=[cur(ATTN_WIDTH), cur(KV_WIDTH), prev(KV_WIDTH), cur(KV_WIDTH), prev(KV_WIDTH), cur(ATTN_WIDTH),
                  pl.BlockSpec((2 * gb, 2 * BLOCK, 4 * BLOCK), lambda n: (jnp.minimum(n, steps - 1), 0, 0)),
                  pl.BlockSpec((2 * gb, 1, 4 * BLOCK), lambda n: (jnp.minimum(n, steps - 1), 0, 0)),
                  cur(LANES), cur(LANES), late(LANES), late(LANES)],
        out_specs=[cur(ATTN_WIDTH), late(KV_WIDTH), late(KV_WIDTH), pl.BlockSpec((8, LANES), lambda n: (0, 0))],
        out_shape=[jax.ShapeDtypeStruct((t, ATTN_WIDTH), BF16), jax.ShapeDtypeStruct((t, KV_WIDTH), BF16),
                   jax.ShapeDtypeStruct((t, KV_WIDTH), BF16), jax.ShapeDtypeStruct((8, LANES), F32)],
        scratch_shapes=[pltpu.VMEM((qb, LANES), F32), pltpu.VMEM((qb, LANES), F32),
                        pltpu.VMEM((qb, LANES), F32), pltpu.VMEM((qb, LANES), F32),
                        pltpu.VMEM((BLOCK, LANES), F32), pltpu.VMEM((BLOCK, LANES), F32)],
        semantics=("arbitrary",), name="attn_bwd", exchange=exchange)


def _inproj_bwd_call(dq, dk, dv, dmixed, x, dx1, w_in, g_mix, exchange=None):
    t = x.shape[0]
    tm = _token_tile(t, 1024)
    n_tiles = t // tm

    def body(dq_ref, dk_ref, dv_ref, dm_ref, halo_ref, x_ref, dx1_ref, w_ref, g_ref,
             dx_ref, gw_ref, gb_ref, gg_ref, sc_ref):
        i = pl.program_id(0)

        @pl.when(i == 0)
        def _():
            for ref in (gw_ref, gb_ref, gg_ref):
                ref[...] = jnp.zeros_like(ref)

        tok = i * tm + lax.broadcasted_iota(jnp.int32, (tm, 1), 0)
        halo = jnp.where(i == n_tiles - 1, 0.0, halo_ref[...])
        du = []
        for g, size in enumerate(POOL_SIZES):
            lanes = slice(LANES * g, LANES * (g + 1))
            dm = dm_ref[:, lanes]
            sc_ref[0:tm, lanes] = dm / jnp.minimum(tok + 1, size).astype(F32)
            sc_ref[tm:, lanes] = halo[:, lanes] / float(size)
            acc = -dm
            for j in range(size):
                acc = acc + sc_ref[pl.ds(j, tm), lanes]
            du.append(acc)
        dz32 = jnp.concatenate([dq_ref[...].astype(F32), dk_ref[...].astype(F32), dv_ref[...].astype(F32)] + du,
                               axis=1)
        dz = dz32.astype(BF16)
        gb_ref[...] += jnp.sum(dz32, axis=0, keepdims=True)
        xh, r = _rms_stats(x_ref[...])
        g = g_ref[...]
        gw_ref[...] += _dot_tn(dz, (xh * g).astype(BF16))
        dh = _dot(dz, w_ref[...])
        gg_ref[...] += jnp.sum(dh * xh, axis=0, keepdims=True)
        dx_ref[...] = dx1_ref[...] + _rms_bwd(dh, xh, r, g)

    row = lambda w: pl.BlockSpec((tm, w), lambda i: (i, 0))
    const = lambda shape: pl.BlockSpec(shape, lambda i: (0,) * len(shape))
    last_halo = t // POOL_HALO - 1
    halo_after = pl.BlockSpec((POOL_HALO, POOL_WIDTH),
                              lambda i: (jnp.minimum((i + 1) * (tm // POOL_HALO), last_halo), 0))
    return _compute_call(
        body, (dq, dk, dv, dmixed, dmixed, x, dx1, w_in, g_mix), grid=(n_tiles,),
        in_specs=[row(ATTN_WIDTH), row(KV_WIDTH), row(KV_WIDTH), row(POOL_WIDTH), halo_after, row(D_MODEL),
                  row(D_MODEL), _resident((IN_WIDTH, D_MODEL)), _resident((1, D_MODEL))],
        out_specs=[row(D_MODEL), const((IN_WIDTH, D_MODEL)), const((1, IN_WIDTH)), const((1, D_MODEL))],
        out_shape=[jax.ShapeDtypeStruct((t, D_MODEL), F32), jax.ShapeDtypeStruct((IN_WIDTH, D_MODEL), F32),
                   jax.ShapeDtypeStruct((1, IN_WIDTH), F32), jax.ShapeDtypeStruct((1, D_MODEL), F32)],
        scratch_shapes=[pltpu.VMEM((tm + POOL_HALO, POOL_WIDTH), F32)],
        semantics=("arbitrary",), name="inproj_bwd", exchange=exchange)


def _reduce_to_owner(grads, place, hosts, wire_dtypes):
    me, c = place
    landed = hosts[0](_sibling_exchange(grads))
    partials = _pair_sum_call(grads, landed, c, wire_dtypes)
    slots = hosts[1](_chip_exchange(partials))
    return _chip_sum_call(partials, slots, me, c)


def _local_step(x, target, small, w_in, w_out, w_ffn, place=None):
    t = x.shape[0]
    on_mesh = place is not None
    cos_t, sin_t = _rope_tables(t)
    w_pool = small["w_pool"].astype(BF16)
    w_ffn = list(w_ffn)

    def gathering(bufs, relay_early=0):
        return _gather_exchange(bufs, relay_early) if on_mesh else None

    (h0,), got = _norm_call(x, small["g_mix"], exchange=gathering([w_in]))
    if on_mesh:
        w_in = got[0].reshape(IN_WIDTH, D_MODEL)
    (q, k, v, u), got = _inproj_call(h0, w_in, small["b_in"], cos_t, sin_t, exchange=gathering([w_out], 3))
    if on_mesh:
        w_out = got[0].reshape(D_MODEL, D_MODEL)
    bias_t, bias_first_t = _band_biases()
    (attn, probs, p_sinks), got = _attn_fwd_call(q, k, v, small["sinks"], bias_t, bias_first_t, exchange=gathering(w_ffn[:2]))
    if on_mesh:
        w_ffn[:2] = got
    (x1, h1), got = _outproj_call(attn, u, x, w_pool, small["b_pool"], small["pool_scale"], w_out,
                                          small["b_out"], small["g_ffn"], exchange=gathering(w_ffn[2:], 4))
    if on_mesh:
        w_ffn[2:] = got
    w_gate, w_up, w_down = w_ffn
    act, dgate, dup, dx1, dx2b, loss, g_g_final, g_g_ffn = _ffn_call(h1, x1, target, w_gate, w_up, w_down,
                                                                      small["g_ffn"], small["g_final"])
    ffn_grads = list(_ffn_bwd_weights_call(h1, dgate, dup, act, dx2b))
    kept = {}

    def outproj_bwd(exchange=None):
        kept["outproj"], results = _outproj_bwd_call(dx1, attn, u, w_pool, small["b_pool"], small["pool_scale"],
                                                     w_out, exchange=exchange)
        return results

    def attn_bwd(exchange=None):
        kept["attn"], results = _attn_bwd_call(q, k, v, kept["outproj"][0], probs, p_sinks, cos_t, sin_t,
                                               exchange=exchange)
        return results

    def inproj_bwd(exchange=None):
        dq, dk, dv, _ = kept["attn"]
        kept["inproj"], results = _inproj_bwd_call(dq, dk, dv, kept["outproj"][1], x, dx1, w_in, small["g_mix"],
                                                   exchange=exchange)
        return results

    if on_mesh:
        ffn_grads = _reduce_to_owner(ffn_grads, place, (outproj_bwd, attn_bwd), [BF16] * 3)
    else:
        outproj_bwd(), attn_bwd()
    inproj_bwd()
    _, _, g_w_out, g_b_out, g_w_pool, g_b_pool, g_pool_scale = kept["outproj"]
    g_sinks = kept["attn"][3]
    grad_x, g_w_in, g_b_in, g_g_mix = kept["inproj"]
    small_grads = dict(g_mix=g_g_mix, b_in=g_b_in, sinks=g_sinks[:, 0], w_pool=g_w_pool,
                       b_pool=g_b_pool, pool_scale=g_pool_scale, b_out=g_b_out, g_ffn=g_g_ffn, g_final=g_g_final)
    big_grads = dict(w_in=g_w_in, w_out=g_w_out, w_gate=ffn_grads[0], w_up=ffn_grads[1], w_down=ffn_grads[2])
    return loss, grad_x, small_grads, big_grads


def _cast_place_call(shards, me):
    na = len(shards)
    split = 2

    def body(me_ref, *refs):
        for a in range(na):
            refs[na + a][...] = refs[a][...].astype(BF16)

    in_specs = [pl.BlockSpec((s.shape[0] // split, s.shape[1]), lambda i, me_ref: (i, 0)) for s in shards]
    out_specs = [pl.BlockSpec((None, s.shape[0] // split, s.shape[1]), lambda i, me_ref: (me_ref[0], i, 0))
                 for s in shards]
    return pl.pallas_call(
        body,
        grid_spec=pltpu.PrefetchScalarGridSpec(num_scalar_prefetch=1, grid=(split,), in_specs=in_specs,
                                               out_specs=out_specs),
        out_shape=[jax.ShapeDtypeStruct((N_SHARDS,) + s.shape, BF16) for s in shards],
        compiler_params=_params("parallel"), name="cast_place_weights",
    )(me.reshape(1), *shards)


def _pair_sum_call(grads, landed, c, wire_dtypes):
    na = len(grads)
    split = 2

    def body(c_ref, *refs):
        ins, lands, outs = refs[:na], refs[na:2 * na], refs[2 * na:]
        for a in range(na):
            outs[a][...] = (ins[a][...] + lands[a][...]).astype(outs[a].dtype)

    in_specs, land_specs, out_specs, views = [], [], [], []
    for g in grads:
        ns, r, cols = g.shape
        rows = r // 2 // split
        views.append(g.reshape(ns, 2, r // 2, cols))
        in_specs.append(pl.BlockSpec((None, None, rows, cols), lambda s, i, c_ref: (s, c_ref[0], i, 0)))
        land_specs.append(pl.BlockSpec((None, rows, cols), lambda s, i, c_ref: (s, i, 0)))
        out_specs.append(pl.BlockSpec((None, rows, cols), lambda s, i, c_ref: (s, i, 0)))
    return pl.pallas_call(
        body,
        grid_spec=pltpu.PrefetchScalarGridSpec(num_scalar_prefetch=1, grid=(N_SHARDS, split),
                                               in_specs=in_specs + land_specs, out_specs=out_specs),
        out_shape=[jax.ShapeDtypeStruct(l.shape, dt) for l, dt in zip(landed, wire_dtypes)],
        compiler_params=_params("parallel", "parallel"), name="reduce_pair_sum",
    )(c.reshape(1), *views, *landed)


def _chip_sum_call(partials, slots, me, c):
    na = len(slots)
    split = 2

    def body(place_ref, *refs):
        mine, lands, outs = refs[:na], refs[na:2 * na], refs[2 * na:]
        for j in range(N_SHARDS):
            @pl.when(place_ref[0] == j)
            def _():
                for a in range(na):
                    terms = [mine[a][...] if k == j else lands[a][k] for k in range(N_SHARDS)]
                    t0, t1, t2, t3 = (term.astype(F32) for term in terms)
                    outs[a][...] = ((t0 + t1) + t2) + t3

    mine_specs, land_specs, out_specs = [], [], []
    for s in slots:
        ns, h, cols = s.shape
        mine_specs.append(pl.BlockSpec((None, h // split, cols), lambda i, place: (place[0], i, 0)))
        land_specs.append(pl.BlockSpec((ns, h // split, cols), lambda i, place: (0, i, 0)))
        out_specs.append(pl.BlockSpec((h // split, cols), lambda i, place: (place[1] * split + i, 0)))
    return pl.pallas_call(
        body,
        grid_spec=pltpu.PrefetchScalarGridSpec(num_scalar_prefetch=1, grid=(split,),
                                               in_specs=mine_specs + land_specs, out_specs=out_specs),
        out_shape=[jax.ShapeDtypeStruct((2 * s.shape[1], s.shape[2]), F32) for s in slots],
        compiler_params=_params("parallel"), name="reduce_chip_sum",
    )(jnp.stack([me, c]), *partials, *slots)


def _adamw(w, g, m, v):
    m = ADAM_B1 * m + (1.0 - ADAM_B1) * g
    v = ADAM_B2 * v + (1.0 - ADAM_B2) * jnp.square(g)
    m_hat = m / (1.0 - ADAM_B1 ** ADAM_STEP)
    v_hat = v / (1.0 - ADAM_B2 ** ADAM_STEP)
    return -ADAM_LR * (m_hat / (jnp.sqrt(v_hat) + ADAM_EPS) + ADAM_WD * w), m, v


def _adamw_call(ws, gs, ms, vs):
    na = len(ws)
    split = 4

    def body(*refs):
        w_refs, g_refs, m_refs, v_refs = (refs[k * na:(k + 1) * na] for k in range(4))
        d_refs, nm_refs, nv_refs = (refs[(4 + k) * na:(5 + k) * na] for k in range(3))
        for a in range(na):
            d_refs[a][...], nm_refs[a][...], nv_refs[a][...] = _adamw(
                w_refs[a][...], g_refs[a][...], m_refs[a][...], v_refs[a][...])

    specs = [pl.BlockSpec((w.shape[0] // split, w.shape[1]), lambda i: (i, 0)) for w in ws]
    shapes = [jax.ShapeDtypeStruct(w.shape, F32) for w in ws]
    outs = pl.pallas_call(
        body, grid=(split,), in_specs=specs * 4, out_specs=specs * 3, out_shape=shapes * 3,
        compiler_params=_params("parallel"), name="adamw",
    )(*ws, *gs, *ms, *vs)
    return outs[:na], outs[na:2 * na], outs[2 * na:]


def _small_rows():
    first, row = {}, 0
    for name, (r, c) in SMALL_VIEWS.items():
        first[name] = row
        row += r * (-(-c // LANES))
    return first, row, -(-(row + 1) // 64) * 64


def _pack_small(arrays, loss):
    _, used, rows = _small_rows()
    parts = []
    for name, (r, c) in SMALL_VIEWS.items():
        a = arrays[name].reshape(r, c)
        parts.append(jnp.pad(a, ((0, 0), (0, LANES - c))) if c < LANES else a.reshape(r * c // LANES, LANES))
    parts += [loss, jnp.zeros((rows - used - 1, LANES), F32)]
    return jnp.concatenate(parts, axis=0)


def _adamw_small_call(ws, packed_grads, ms, vs):
    names = list(SMALL_VIEWS)
    first, _, _ = _small_rows()
    n = len(names)

    def body(g_ref, *refs):
        w_refs, m_refs, v_refs = refs[:n], refs[n:2 * n], refs[2 * n:3 * n]
        outs = refs[3 * n:]
        for a, name in enumerate(names):
            r, c = SMALL_VIEWS[name]
            r0 = first[name]
            if c == LANES:
                g = g_ref[r0:r0 + r, :]
            elif c < LANES:
                g = g_ref[r0:r0 + 1, 0:c]
            else:
                g = jnp.concatenate([g_ref[r0 + j:r0 + j + 1, :] for j in range(c // LANES)], axis=1)
            delta, m, v = _adamw(w_refs[a][...], g, m_refs[a][...], v_refs[a][...])
            for k, val in enumerate((g, delta, m, v)):
                outs[k * n + a][...] = val

    shapes = [jax.ShapeDtypeStruct(SMALL_VIEWS[name], F32) for name in names]
    outs = pl.pallas_call(
        body, out_shape=shapes * 4, compiler_params=pltpu.CompilerParams(vmem_limit_bytes=VMEM_LIMIT),
        name="adamw_small",
    )(packed_grads, *[ws[k] for k in names], *[ms[k] for k in names], *[vs[k] for k in names])
    return [dict(zip(names, outs[k * n:(k + 1) * n])) for k in range(4)]


def kernel(x, g_mix, w_in, b_in, sinks, w_pool, b_pool, pool_scale, w_out, b_out, g_ffn, w_gate, w_up, w_down, g_final, loss_target, m_g_mix, m_w_in, m_b_in, m_sinks, m_w_pool, m_b_pool, m_pool_scale, m_w_out, m_b_out, m_g_ffn, m_w_gate, m_w_up, m_w_down, m_g_final, v_g_mix, v_w_in, v_b_in, v_sinks, v_w_pool, v_b_pool, v_pool_scale, v_w_out, v_b_out, v_g_ffn, v_w_gate, v_w_up, v_w_down, v_g_final):
    given = dict(locals())
    big_names = ("w_in", "w_out", "w_gate", "w_up", "w_down")
    out_shapes = {n: given[n].shape for n in SMALL_NAMES + big_names}
    c = lax.axis_index("c")

    me = 2 * lax.axis_index("x") + lax.axis_index("y")
    transposed = ("w_in", "w_gate", "w_up")

    def shard_view(a, name):
        return a[0].T if name in transposed else a[0]

    placed = _cast_place_call([shard_view(given[n], n) for n in big_names], me)

    small = dict(g_mix=g_mix, b_in=b_in, sinks=sinks, w_pool=w_pool[0], b_pool=b_pool.reshape(1, POOL_WIDTH),
                 pool_scale=pool_scale.reshape(1, POOL_WIDTH), b_out=b_out, g_ffn=g_ffn,
                 g_final=g_final.reshape(1, D_MODEL))
    loss, grad_x, small_grads, big_grads = _local_step(
        x[0], loss_target[0], small, placed[0], placed[1], placed[2:], place=(me, c))

    packed = _pack_small(small_grads, loss)
    grads = [
        big_grads["w_in"].reshape(N_SHARDS, IN_WIDTH // N_SHARDS, D_MODEL),
        big_grads["w_out"].reshape(N_SHARDS, D_MODEL // N_SHARDS, D_MODEL),
        jnp.broadcast_to(packed[None], (N_SHARDS,) + packed.shape),
    ]
    alone = [functools.partial(_exchange_call, name=name) for name in
             ("reduce_sibling_exchange", "reduce_chip_exchange")]
    h_in, h_out, h_small = _reduce_to_owner(grads, (me, c), alone, [BF16, BF16, F32])
    totals = _exchange_call(_sibling_allgather(
        [h_in, h_out, big_grads["w_gate"], big_grads["w_up"], big_grads["w_down"], h_small]), "reduce_sibling_allgather")
    loss = totals[-1][_small_rows()[1], 0]

    def owned(prefix):
        return [shard_view(given[prefix + n], n) for n in big_names]

    def owned_small(prefix):
        return {n: given[prefix + n].reshape(SMALL_VIEWS[n]) for n in SMALL_NAMES}

    big = (totals[:-1],) + _adamw_call(owned(""), totals[:-1], owned("m_"), owned("v_"))
    small_groups = _adamw_small_call(owned_small(""), totals[-1], owned_small("m_"), owned_small("v_"))

    order = ("g_mix", "w_in", "b_in", "sinks", "w_pool", "b_pool", "pool_scale", "w_out", "b_out", "g_ffn",
             "w_gate", "w_up", "w_down", "g_final")
    results = [loss, grad_x.reshape(x.shape)]
    for group, small_group in zip(big, small_groups):
        named = {n: (a.T if n in transposed else a) for n, a in zip(big_names, group)}
        named.update(small_group)
        results += [named[n].reshape(out_shapes[n]) for n in order]
    return tuple(results)
```

```python
import functools
from typing import Any, Callable, Mapping, NamedTuple, Sequence

import jax
import jax.numpy as jnp
from jax import lax
from jax.experimental import pallas as pl
from jax.experimental.pallas import tpu as pltpu

F32 = jnp.float32
BF16 = jnp.bfloat16

D_MODEL = 1024
ATTN_WIDTH = 512
KV_WIDTH = 128
POOL_WIDTH = 512
IN_WIDTH = ATTN_WIDTH + 2 * KV_WIDTH + POOL_WIDTH
HEAD_DIM = 64
N_Q_HEADS = 8
BLOCK = 128
POOL_SIZES = (2, 4, 8, 16)
POOL_HALO = 16
ROPE_THETA = 10000.0
RMS_EPS = 1e-5
Q_SCALE = HEAD_DIM ** -0.5

ADAM_LR = 0.001
ADAM_B1 = 0.9
ADAM_B2 = 0.999
ADAM_EPS = 1e-08
ADAM_WD = 0.01
ADAM_STEP = 10

N_SHARDS = 4
LANES = 128
VMEM_LIMIT = 56 * 1024 * 1024
FFN_TOKEN_TILE = 256

MESH = pl.DeviceIdType.MESH
HBM_SPEC = pl.BlockSpec(memory_space=pltpu.HBM)

SMALL_VIEWS = dict(w_pool=(4 * LANES, LANES), b_pool=(4, LANES), pool_scale=(4, LANES), g_mix=(1, D_MODEL),
                   b_in=(1, IN_WIDTH), b_out=(1, D_MODEL), g_ffn=(1, D_MODEL), g_final=(1, D_MODEL),
                   sinks=(1, N_Q_HEADS))
SMALL_NAMES = tuple(SMALL_VIEWS)


def _dot(a, b):
    return jnp.dot(a, b, preferred_element_type=F32)


def _dot_nt(a, b):
    return lax.dot_general(a, b, (((1,), (1,)), ((), ())), preferred_element_type=F32)


def _dot_tn(a, b):
    return lax.dot_general(a, b, (((0,), (0,)), ((), ())), preferred_element_type=F32)


def _params(*semantics):
    return pltpu.CompilerParams(dimension_semantics=semantics, vmem_limit_bytes=VMEM_LIMIT)


def _resident(shape):
    return pl.BlockSpec(shape, lambda *_: (0,) * len(shape), pipeline_mode=pl.Buffered(1))


def _token_tile(t, largest=512):
    for tm in (2048, 1024, 512, 256, 128):
        if tm <= largest and t % tm == 0:
            return tm
    raise ValueError(f"sequence length {t} is not a multiple of 128")


def _rms_stats(xf):
    r = lax.rsqrt(jnp.mean(xf * xf, axis=-1, keepdims=True) + RMS_EPS)
    return xf * r, r


def _rms_bwd(dy, xh, r, g):
    dxh = dy * g
    return r * (dxh - xh * jnp.mean(dxh * xh, axis=-1, keepdims=True))


def _rope_partner(t, first):
    return jnp.where(first, pltpu.roll(t, LANES - HEAD_DIM // 2, 1), pltpu.roll(t, HEAD_DIM // 2, 1))


def _first_half_mask(shape):
    lane = lax.broadcasted_iota(jnp.int32, shape, 1)
    return (lane % HEAD_DIM) < (HEAD_DIM // 2)


def _rope_tables(t):
    inv_freq = 1.0 / (ROPE_THETA ** (jnp.arange(0, HEAD_DIM, 2, dtype=F32) / HEAD_DIM))
    ang = jnp.arange(t, dtype=F32)[:, None] * inv_freq[None, :]
    cos, sin = jnp.cos(ang), jnp.sin(ang)
    cos_t = jnp.tile(jnp.concatenate([cos, cos], axis=-1), (1, LANES // HEAD_DIM))
    sin_t = jnp.tile(jnp.concatenate([-sin, sin], axis=-1), (1, LANES // HEAD_DIM))
    return cos_t, sin_t


class _Exchange(NamedTuple):
    inputs: Sequence[Any]
    out_shapes: Sequence[Any]
    aliases: Mapping[int, int]
    sems: Sequence[Any]
    start: Callable[..., None]
    finish: Callable[..., None]
    relay: Callable[..., None] = None
    relay_early: int = 0


def _compute_call(body, operands, *, grid, in_specs, out_specs, out_shape, scratch_shapes=(), semantics, name,
                  exchange=None):
    n_in, n_out, n_scr = len(in_specs), len(out_specs), len(scratch_shapes)
    if exchange is None:
        return pl.pallas_call(body, grid=grid, in_specs=in_specs, out_specs=out_specs, out_shape=out_shape,
                              scratch_shapes=scratch_shapes, compiler_params=_params(*semantics),
                              name=name)(*operands), []
    k_in, k_out = len(exchange.inputs), len(exchange.out_shapes)

    def hosting(*refs):
        ins, rest = refs[:n_in], refs[n_in:]
        ex_ins, rest = rest[:k_in], rest[k_in:]
        outs, rest = rest[:n_out], rest[n_out:]
        ex_outs, rest = rest[:k_out], rest[k_out:]
        scratch, sems = rest[:n_scr], rest[n_scr:]
        steps = [pl.program_id(d) for d in range(len(grid))]
        is_first = functools.reduce(jnp.logical_and, [s == 0 for s in steps])
        is_last = functools.reduce(jnp.logical_and, [s == g - 1 for s, g in zip(steps, grid)])
        is_relay = functools.reduce(jnp.logical_and, [s == max(g - 1 - exchange.relay_early, 0)
                                                      for s, g in zip(steps, grid)])

        @pl.when(is_first)
        def _():
            exchange.start(ex_ins, ex_outs, sems)

        body(*ins, *outs, *scratch)

        if exchange.relay is not None:
            @pl.when(is_relay)
            def _():
                exchange.relay(ex_ins, ex_outs, sems)

        @pl.when(is_last)
        def _():
            exchange.finish(ex_ins, ex_outs, sems)

    results = pl.pallas_call(
        hosting, grid=grid,
        in_specs=list(in_specs) + [HBM_SPEC] * k_in, out_specs=list(out_specs) + [HBM_SPEC] * k_out,
        out_shape=list(out_shape) + list(exchange.out_shapes),
        scratch_shapes=list(scratch_shapes) + list(exchange.sems),
        input_output_aliases={n_in + i: n_out + j for i, j in exchange.aliases.items()},
        compiler_params=_params(*semantics), name=name,
    )(*operands, *exchange.inputs)
    return results[:n_out], results[n_out:]


def _both(first, second):
    n_in, n_out, n_sem = len(first.inputs), len(first.out_shapes), len(first.sems)
    aliases = dict(first.aliases)
    aliases.update({n_in + i: n_out + j for i, j in second.aliases.items()})

    def start(ins, outs, sems):
        first.start(ins[:n_in], outs[:n_out], sems[:n_sem])
        second.start(ins[n_in:], outs[n_out:], sems[n_sem:])

    def finish(ins, outs, sems):
        first.finish(ins[:n_in], outs[:n_out], sems[:n_sem])
        second.finish(ins[n_in:], outs[n_out:], sems[n_sem:])

    assert first.relay is None and second.relay is None
    return _Exchange(inputs=list(first.inputs) + list(second.inputs),
                     out_shapes=list(first.out_shapes) + list(second.out_shapes), aliases=aliases,
                     sems=list(first.sems) + list(second.sems), start=start, finish=finish)


def _exchange_call(exchange, name):
    n_in, n_out = len(exchange.inputs), len(exchange.out_shapes)

    def body(*refs):
        ins, outs, sems = refs[:n_in], refs[n_in:n_in + n_out], refs[n_in + n_out:]
        exchange.start(ins, outs, sems)
        if exchange.relay is not None:
            exchange.relay(ins, outs, sems)
        exchange.finish(ins, outs, sems)

    return pl.pallas_call(
        body, in_specs=[HBM_SPEC] * n_in, out_specs=[HBM_SPEC] * n_out, out_shape=list(exchange.out_shapes),
        input_output_aliases=dict(exchange.aliases), scratch_shapes=list(exchange.sems),
        compiler_params=pltpu.CompilerParams(has_side_effects=True), name=name,
    )(*exchange.inputs)


def _mesh_place():
    x, y, c = lax.axis_index("x"), lax.axis_index("y"), lax.axis_index("c")
    other_chips = [(1 - x, y), (x, 1 - y), (1 - x, 1 - y)]
    return x, y, c, other_chips


def _half(ref, c, rows):
    return ref.at[pl.ds(pl.multiple_of(c * rows, 16), rows), :]


def _remote(src, dst, send_sems, recv_sems, idx, to):
    return functools.partial(pltpu.make_async_remote_copy, src_ref=src, dst_ref=dst, send_sem=send_sems.at[idx],
                             recv_sem=recv_sems.at[idx], device_id=to, device_id_type=MESH)


def _gather_exchange(bufs, relay_early=0):
    na = len(bufs)

    def copies(outs, sems):
        send_sems, recv_sems = sems
        x, y, c, other_chips = _mesh_place()
        me, sibling = 2 * x + y, (x, y, 1 - c)
        ici_out, ici_in, fwd_out, fwd_in = [], [], [], []
        for j, (cx, cy) in enumerate(other_chips):
            for a in range(na):
                rows = outs[a].shape[1] // 2
                mine = _half(outs[a].at[me], c, rows)
                landed = _half(outs[a].at[2 * cx + cy], c, rows)
                passed = _half(outs[a].at[2 * cx + cy], 1 - c, rows)
                ici_out.append(_remote(mine, mine, send_sems, recv_sems, j * na + a, (cx, cy, c)))
                ici_in.append(_remote(landed, landed, send_sems, recv_sems, j * na + a, (cx, cy, c)))
                fwd_out.append(_remote(landed, landed, send_sems, recv_sems, (3 + j) * na + a, sibling))
                fwd_in.append(_remote(passed, passed, send_sems, recv_sems, (3 + j) * na + a, sibling))
        return ici_out, ici_in, fwd_out, fwd_in

    def start(ins, outs, sems):
        for cp in copies(outs, sems)[0]:
            cp().start()

    def relay(ins, outs, sems):
        _, ici_in, fwd_out, _ = copies(outs, sems)
        for arrived, forward in zip(ici_in, fwd_out):
            arrived().wait_recv()
            forward().start()

    def finish(ins, outs, sems):
        ici_out, _, fwd_out, fwd_in = copies(outs, sems)
        for cp in fwd_in:
            cp().wait_recv()
        for cp in ici_out + fwd_out:
            cp().wait_send()

    return _Exchange(inputs=bufs, out_shapes=[jax.ShapeDtypeStruct(b.shape, b.dtype) for b in bufs],
                     aliases={a: a for a in range(na)},
                     sems=[pltpu.SemaphoreType.DMA((6 * na,)), pltpu.SemaphoreType.DMA((6 * na,))],
                     start=start, finish=finish, relay=relay, relay_early=relay_early)


def _sibling_exchange(grads):
    na = len(grads)

    def copies(ins, outs, sems):
        x, y, c, _ = _mesh_place()
        out = []
        for a in range(na):
            rows = ins[a].shape[1] // 2
            src = ins[a].at[:, pl.ds(pl.multiple_of((1 - c) * rows, 8), rows), :]
            out.append(_remote(src, outs[a], sems[0], sems[1], a, (x, y, 1 - c)))
        return out

    def start(ins, outs, sems):
        for cp in copies(ins, outs, sems):
            cp().start()

    def finish(ins, outs, sems):
        for cp in copies(ins, outs, sems):
            cp().wait_recv()
            cp().wait_send()

    return _Exchange(inputs=grads,
                     out_shapes=[jax.ShapeDtypeStruct((g.shape[0], g.shape[1] // 2, g.shape[2]), g.dtype)
                                 for g in grads],
                     aliases={}, sems=[pltpu.SemaphoreType.DMA((na,)), pltpu.SemaphoreType.DMA((na,))],
                     start=start, finish=finish)


def _chip_exchange(partials):
    na = len(partials)

    def copies(ins, outs, sems):
        x, y, c, other_chips = _mesh_place()
        me = 2 * x + y
        sends, arrivals = [], []
        for j, (cx, cy) in enumerate(other_chips):
            for a in range(na):
                slot = outs[a].at[2 * cx + cy]
                sends.append(_remote(ins[a].at[2 * cx + cy], outs[a].at[me], sems[0], sems[1], j * na + a, (cx, cy, c)))
                arrivals.append(_remote(slot, slot, sems[0], sems[1], j * na + a, (cx, cy, c)))
        return sends, arrivals

    def start(ins, outs, sems):
        for cp in copies(ins, outs, sems)[0]:
            cp().start()

    def finish(ins, outs, sems):
        sends, arrivals = copies(ins, outs, sems)
        for cp in arrivals:
            cp().wait_recv()
        for cp in sends:
            cp().wait_send()

    return _Exchange(inputs=partials, out_shapes=[jax.ShapeDtypeStruct(p.shape, p.dtype) for p in partials],
                     aliases={}, sems=[pltpu.SemaphoreType.DMA((3 * na,)), pltpu.SemaphoreType.DMA((3 * na,))],
                     start=start, finish=finish)


def _sibling_allgather(bufs):
    na = len(bufs)

    def piece(outs, sems, a, half):
        x, y, c, _ = _mesh_place()
        rows = outs[a].shape[0] // 2
        rows_ref = outs[a].at[pl.ds(pl.multiple_of(half * rows, 8), rows), :]
        return _remote(rows_ref, rows_ref, sems[0], sems[1], a, (x, y, 1 - c))

    def start(ins, outs, sems):
        c = lax.axis_index("c")
        for a in range(na):
            piece(outs, sems, a, c)().start()

    def finish(ins, outs, sems):
        c = lax.axis_index("c")
        for a in range(na):
            piece(outs, sems, a, 1 - c)().wait_recv()
        for a in range(na):
            piece(outs, sems, a, c)().wait_send()

    return _Exchange(inputs=bufs, out_shapes=[jax.ShapeDtypeStruct(b.shape, b.dtype) for b in bufs],
                     aliases={a: a for a in range(na)},
                     sems=[pltpu.SemaphoreType.DMA((na,)), pltpu.SemaphoreType.DMA((na,))],
                     start=start, finish=finish)


def _norm_call(x, g_mix, exchange=None):
    t = x.shape[0]
    tm = _token_tile(t)

    def body(x_ref, g_ref, h_ref):
        xh, _ = _rms_stats(x_ref[...])
        h_ref[...] = (xh * g_ref[...]).astype(BF16)

    row = pl.BlockSpec((tm, D_MODEL), lambda i: (i, 0))
    return _compute_call(body, (x, g_mix), grid=(t // tm,), in_specs=[row, _resident((1, D_MODEL))],
                         out_specs=[row], out_shape=[jax.ShapeDtypeStruct((t, D_MODEL), BF16)],
                         semantics=("parallel",), name="norm_fwd", exchange=exchange)


def _inproj_call(h0, w_in, b_in, cos_t, sin_t, exchange=None):
    t = h0.shape[0]
    tm = _token_tile(t, 1024)

    def body(h_ref, w_ref, b_ref, c_ref, s_ref, q_ref, k_ref, v_ref, u_ref):
        z = _dot_nt(h_ref[...], w_ref[...]) + b_ref[...]
        cos, sin = c_ref[...], s_ref[...]
        first = _first_half_mask((tm, LANES))
        for j in range(ATTN_WIDTH // LANES):
            zj = z[:, LANES * j:LANES * (j + 1)]
            q_ref[:, LANES * j:LANES * (j + 1)] = ((zj * cos + _rope_partner(zj, first) * sin) * Q_SCALE).astype(BF16)
        zk = z[:, ATTN_WIDTH:ATTN_WIDTH + KV_WIDTH]
        k_ref[...] = (zk * cos + _rope_partner(zk, first) * sin).astype(BF16)
        v_ref[...] = z[:, ATTN_WIDTH + KV_WIDTH:ATTN_WIDTH + 2 * KV_WIDTH].astype(BF16)
        u_ref[...] = z[:, ATTN_WIDTH + 2 * KV_WIDTH:]

    row = lambda w: pl.BlockSpec((tm, w), lambda i: (i, 0))
    return _compute_call(
        body, (h0, w_in, b_in, cos_t, sin_t), grid=(t // tm,),
        in_specs=[row(D_MODEL), _resident((IN_WIDTH, D_MODEL)), _resident((1, IN_WIDTH)), row(LANES), row(LANES)],
        out_specs=[row(ATTN_WIDTH), row(KV_WIDTH), row(KV_WIDTH), row(POOL_WIDTH)],
        out_shape=[jax.ShapeDtypeStruct((t, ATTN_WIDTH), BF16),
                   jax.ShapeDtypeStruct((t, KV_WIDTH), BF16), jax.ShapeDtypeStruct((t, KV_WIDTH), BF16),
                   jax.ShapeDtypeStruct((t, POOL_WIDTH), F32)],
        semantics=("parallel",), name="inproj_fwd", exchange=exchange)


def _head_in_both_halves(band, kv_head):
    lane = lax.broadcasted_iota(jnp.int32, band.shape, 1)
    own = (lane < HEAD_DIM) if kv_head == 0 else (lane >= HEAD_DIM)
    return jnp.where(own, band, pltpu.roll(band, HEAD_DIM, 1)).astype(BF16)


def _stack_heads(ref, kv_head, row0):
    low = lax.broadcasted_iota(jnp.int32, (BLOCK, LANES), 1) < HEAD_DIM
    parts = []
    for jj in range(2):
        slab = ref[row0:row0 + BLOCK, LANES * (2 * kv_head + jj):LANES * (2 * kv_head + jj + 1)]
        zero = jnp.zeros_like(slab)
        parts += [jnp.where(low, slab, zero), jnp.where(low, zero, slab)]
    return jnp.concatenate(parts, axis=0)


def _band_biases():
    c = lax.broadcasted_iota(jnp.int32, (2 * BLOCK, 4 * BLOCK), 0)
    i = lax.broadcasted_iota(jnp.int32, (2 * BLOCK, 4 * BLOCK), 1) % BLOCK
    vis = (c > i) & (c <= i + BLOCK)
    return jnp.where(vis, 0.0, -jnp.inf).astype(F32), jnp.where(vis & (c >= BLOCK), 0.0, -jnp.inf).astype(F32)


def _blocks_per_step(nb):
    for g in (8, 4, 2, 1):
        if nb % g == 0:
            return g


def _sink_row(sink_ref, kv_head):
    return jnp.concatenate([jnp.full((1, BLOCK), sink_ref[0, 4 * kv_head + g], F32) for g in range(4)], axis=1)


def _kv_band(cur_ref, prev_ref, g):
    r0 = g * BLOCK
    prev = prev_ref[...] if g == 0 else cur_ref[r0 - BLOCK:r0, :]
    return jnp.concatenate([prev, cur_ref[r0:r0 + BLOCK, :]], axis=0).astype(F32)


def _attn_fwd_call(q, k, v, sinks, bias, bias_first, exchange=None):
    t = q.shape[0]
    nb = t // BLOCK
    gb = _blocks_per_step(nb)
    qb = gb * BLOCK

    def body(sink_ref, q_ref, kc_ref, kp_ref, vc_ref, vp_ref, b_ref, bf_ref, o_ref, p_ref, psink_ref):
        n = pl.program_id(0)
        row_low = lax.broadcasted_iota(jnp.int32, (LANES, BLOCK), 0) < HEAD_DIM
        for g in range(gb):
            r0 = g * BLOCK
            kband = _kv_band(kc_ref, kp_ref, g)
            vband_t = _kv_band(vc_ref, vp_ref, g).T
            bias_g = jnp.where(n == 0, bf_ref[...], b_ref[...]) if g == 0 else b_ref[...]
            for kh in range(2):
                k2 = _head_in_both_halves(kband, kh)
                vt_head = vband_t[HEAD_DIM * kh:HEAD_DIM * (kh + 1), :]
                v2_t = jnp.concatenate([vt_head, vt_head], axis=0).astype(BF16)
                s_t = _dot_nt(k2, _stack_heads(q_ref, kh, r0)) + bias_g
                sink = _sink_row(sink_ref, kh)
                m = jnp.maximum(jnp.max(s_t, axis=0, keepdims=True), sink)
                e = jnp.exp(s_t - m)
                e_sink = jnp.exp(sink - m)
                inv = 1.0 / (jnp.sum(e, axis=0, keepdims=True) + e_sink)
                p_t = (e * inv).astype(BF16)
                p_ref[2 * g + kh] = p_t
                psink_ref[2 * g + kh] = e_sink * inv
                o_t = _dot(v2_t, p_t)
                for jj in range(2):
                    slab = jnp.where(row_low, o_t[:, BLOCK * 2 * jj:BLOCK * (2 * jj + 1)],
                                     o_t[:, BLOCK * (2 * jj + 1):BLOCK * (2 * jj + 2)]).T
                    o_ref[r0:r0 + BLOCK, LANES * (2 * kh + jj):LANES * (2 * kh + jj + 1)] = slab.astype(BF16)

    cur = lambda w: pl.BlockSpec((qb, w), lambda n: (n, 0))
    prev = lambda w: pl.BlockSpec((BLOCK, w), lambda n: (jnp.maximum(n * gb - 1, 0), 0))
    return _compute_call(
        body, (sinks, q, k, k, v, v, bias, bias_first), grid=(nb // gb,),
        in_specs=[pl.BlockSpec(memory_space=pltpu.SMEM), cur(ATTN_WIDTH), cur(KV_WIDTH), prev(KV_WIDTH),
                  cur(KV_WIDTH), prev(KV_WIDTH), _resident(bias.shape), _resident(bias.shape)],
        out_specs=[cur(ATTN_WIDTH), pl.BlockSpec((2 * gb, 2 * BLOCK, 4 * BLOCK), lambda n: (n, 0, 0)),
                   pl.BlockSpec((2 * gb, 1, 4 * BLOCK), lambda n: (n, 0, 0))],
        out_shape=[jax.ShapeDtypeStruct((t, ATTN_WIDTH), BF16),
                   jax.ShapeDtypeStruct((2 * nb, 2 * BLOCK, 4 * BLOCK), BF16),
                   jax.ShapeDtypeStruct((2 * nb, 1, 4 * BLOCK), F32)],
        semantics=("parallel",), name="attn_fwd", exchange=exchange)


def _pool_mixed(sc_ref, u_ref, halo_ref, i, tm):
    sc_ref[0:POOL_HALO, :] = jnp.where(i == 0, 0.0, halo_ref[...])
    sc_ref[POOL_HALO:, :] = u_ref[...]
    tok = i * tm + lax.broadcasted_iota(jnp.int32, (tm, 1), 0)
    mixed = []
    for g, size in enumerate(POOL_SIZES):
        lanes = slice(LANES * g, LANES * (g + 1))
        cur = sc_ref[pl.ds(POOL_HALO, tm), lanes]
        acc = cur
        for j in range(1, size):
            acc = acc + sc_ref[pl.ds(POOL_HALO - j, tm), lanes]
        count = jnp.minimum(tok + 1, size).astype(F32)
        mixed.append(acc / count - cur)
    return mixed


def _halo_before(tm, width):
    return pl.BlockSpec((POOL_HALO, width), lambda i: (jnp.maximum(i * (tm // POOL_HALO) - 1, 0), 0))


def _outproj_call(attn, u, x, w_pool, b_pool, pool_scale, w_out, b_out, g_ffn, exchange=None):
    t = x.shape[0]
    tm = _token_tile(t)

    def body(a_ref, u_ref, halo_ref, x_ref, wp_ref, bp_ref, ps_ref, wo_ref, bo_ref, g_ref,
             x1_ref, h1_ref, sc_ref, y_ref):
        i = pl.program_id(0)
        for g, mixed in enumerate(_pool_mixed(sc_ref, u_ref, halo_ref, i, tm)):
            lanes = slice(LANES * g, LANES * (g + 1))
            lin = _dot(mixed.astype(BF16), wp_ref[g]) + bp_ref[:, lanes]
            y_ref[:, lanes] = (lin * ps_ref[:, lanes]).astype(BF16)
        x1 = (x_ref[...] + _dot(a_ref[...], wo_ref[0:ATTN_WIDTH, :]) + _dot(y_ref[...], wo_ref[ATTN_WIDTH:, :])
              + bo_ref[...])
        x1_ref[...] = x1
        xh, _ = _rms_stats(x1)
        h1_ref[...] = (xh * g_ref[...]).astype(BF16)

    row = lambda w: pl.BlockSpec((tm, w), lambda i: (i, 0))
    return _compute_call(
        body, (attn, u, u, x, w_pool, b_pool, pool_scale, w_out, b_out, g_ffn), grid=(t // tm,),
        in_specs=[row(ATTN_WIDTH), row(POOL_WIDTH), _halo_before(tm, POOL_WIDTH), row(D_MODEL),
                  _resident((4, LANES, LANES)), _resident((1, POOL_WIDTH)), _resident((1, POOL_WIDTH)),
                  _resident((D_MODEL, D_MODEL)), _resident((1, D_MODEL)), _resident((1, D_MODEL))],
        out_specs=[row(D_MODEL), row(D_MODEL)],
        out_shape=[jax.ShapeDtypeStruct((t, D_MODEL), F32), jax.ShapeDtypeStruct((t, D_MODEL), BF16)],
        scratch_shapes=[pltpu.VMEM((tm + POOL_HALO, POOL_WIDTH), F32), pltpu.VMEM((tm, POOL_WIDTH), BF16)],
        semantics=("parallel",), name="outproj_fwd", exchange=exchange)


def _ffn_call(h1, x1, target, w_gate, w_up, w_down, g_ffn, g_final):
    t = x1.shape[0]
    tm = _token_tile(t, FFN_TOKEN_TILE)
    ns, fs, _ = w_gate.shape

    def body(h_ref, x1_ref, tgt_ref, wg_ref, wu_ref, wd_ref, g_ref, gf_ref,
             act_ref, dgate_ref, dup_ref, dx1_ref, dx2b_ref, loss_ref, dgf_ref, dg_ref, gate_ref, up_ref):
        @pl.when(pl.program_id(0) == 0)
        def _():
            loss_ref[...] = jnp.zeros_like(loss_ref)
            dgf_ref[...] = jnp.zeros_like(dgf_ref)
            dg_ref[...] = jnp.zeros_like(dg_ref)

        h = h_ref[...]
        x1 = x1_ref[...]
        x2 = x1
        for s in range(ns):
            gate = _dot_nt(h, wg_ref[s])
            up = _dot_nt(h, wu_ref[s])
            act = (gate * jax.nn.sigmoid(gate) * up).astype(BF16)
            gate_ref[s] = gate
            up_ref[s] = up
            act_ref[s] = act
            x2 = x2 + _dot(act, wd_ref[s])
        xh2, r2 = _rms_stats(x2)
        gf = gf_ref[...]
        err = xh2 * gf - tgt_ref[...]
        loss_ref[...] += 0.5 * jnp.sum(err * err) / D_MODEL
        dy = err / D_MODEL
        dgf_ref[...] += jnp.sum(dy * xh2, axis=0, keepdims=True)
        dx2 = _rms_bwd(dy, xh2, r2, gf)
        dx2b = dx2.astype(BF16)
        dx2b_ref[...] = dx2b
        dh = None
        for s in range(ns):
            dact = _dot_nt(dx2b, wd_ref[s])
            gate = gate_ref[s]
            sg = jax.nn.sigmoid(gate)
            dup = (dact * (gate * sg)).astype(BF16)
            dgate = (dact * up_ref[s] * (sg * (1.0 + gate * (1.0 - sg)))).astype(BF16)
            dgate_ref[s] = dgate
            dup_ref[s] = dup
            part = _dot(dgate, wg_ref[s]) + _dot(dup, wu_ref[s])
            dh = part if dh is None else dh + part
        xh1, r1 = _rms_stats(x1)
        dg_ref[...] += jnp.sum(dh * xh1, axis=0, keepdims=True)
        dx1_ref[...] = dx2 + _rms_bwd(dh, xh1, r1, g_ref[...])

    row = lambda w: pl.BlockSpec((tm, w), lambda i: (i, 0))
    shard_act = pl.BlockSpec((ns, tm, fs), lambda i: (0, i, 0))
    act_shape = jax.ShapeDtypeStruct((ns, t, fs), BF16)
    vec = pl.BlockSpec((1, D_MODEL), lambda i: (0, 0))
    return pl.pallas_call(
        body, grid=(t // tm,),
        in_specs=[row(D_MODEL), row(D_MODEL), row(D_MODEL), _resident((ns, fs, D_MODEL)),
                  _resident((ns, fs, D_MODEL)), _resident((ns, fs, D_MODEL)), _resident((1, D_MODEL)),
                  _resident((1, D_MODEL))],
        out_specs=[shard_act, shard_act, shard_act, row(D_MODEL), row(D_MODEL),
                   pl.BlockSpec((1, LANES), lambda i: (0, 0)), vec, vec],
        out_shape=[act_shape, act_shape, act_shape, jax.ShapeDtypeStruct((t, D_MODEL), F32),
                   jax.ShapeDtypeStruct((t, D_MODEL), BF16), jax.ShapeDtypeStruct((1, LANES), F32),
                   jax.ShapeDtypeStruct((1, D_MODEL), F32), jax.ShapeDtypeStruct((1, D_MODEL), F32)],
        scratch_shapes=[pltpu.VMEM((ns, tm, fs), F32), pltpu.VMEM((ns, tm, fs), F32)],
        compiler_params=_params("arbitrary"), name="ffn_fwd_bwd",
    )(h1, x1, target, w_gate, w_up, w_down, g_ffn, g_final)


def _ffn_bwd_weights_call(lhs, rhs, name, exchange=None):
    t = rhs.shape[0]
    tm = _token_tile(t, 2048)
    ns, _, fs = lhs[0].shape
    n = len(lhs)

    def body(rhs_ref, *refs):
        lhs_refs, out_refs = refs[:n], refs[n:]

        @pl.when(pl.program_id(1) == 0)
        def _():
            for ref in out_refs:
                ref[...] = jnp.zeros_like(ref)

        r = rhs_ref[...]
        for lhs_ref, out_ref in zip(lhs_refs, out_refs):
            out_ref[...] += _dot_tn(lhs_ref[...], r)

    row = pl.BlockSpec((tm, D_MODEL), lambda s, i: (i, 0))
    shard_act = pl.BlockSpec((None, tm, fs), lambda s, i: (s, i, 0))
    return _compute_call(
        body, (rhs, *lhs), grid=(ns, t // tm), in_specs=[row] + [shard_act] * n,
        out_specs=[pl.BlockSpec((None, fs, D_MODEL), lambda s, i: (s, 0, 0))] * n,
        out_shape=[jax.ShapeDtypeStruct((ns, fs, D_MODEL), F32)] * n,
        semantics=("parallel", "arbitrary"), name=name, exchange=exchange)


def _outproj_bwd_call(dx1, attn, u, w_pool, b_pool, pool_scale, w_out, exchange=None):
    t = dx1.shape[0]
    tm = _token_tile(t, 1024)

    def body(dx1_ref, a_ref, u_ref, halo_ref, wp_ref, bp_ref, ps_ref, wo_ref,
             dattn_ref, dmixed_ref, gwo_ref, gbo_ref, gwp_ref, gbp_ref, gps_ref, sc_ref, y_ref):
        i = pl.program_id(0)

        @pl.when(i == 0)
        def _():
            for ref in (gwo_ref, gbo_ref, gwp_ref, gbp_ref, gps_ref):
                ref[...] = jnp.zeros_like(ref)

        dx1 = dx1_ref[...]
        dx1b = dx1.astype(BF16)
        dcat = _dot_nt(dx1b, wo_ref[...])
        dattn_ref[...] = dcat[:, :ATTN_WIDTH].astype(BF16)
        for g, mixed in enumerate(_pool_mixed(sc_ref, u_ref, halo_ref, i, tm)):
            lanes = slice(LANES * g, LANES * (g + 1))
            mixed_b = mixed.astype(BF16)
            lin = _dot(mixed_b, wp_ref[g]) + bp_ref[:, lanes]
            y_ref[:, lanes] = (lin * ps_ref[:, lanes]).astype(BF16)
            dpool = dcat[:, ATTN_WIDTH + LANES * g:ATTN_WIDTH + LANES * (g + 1)]
            gps_ref[:, lanes] += jnp.sum(dpool * lin, axis=0, keepdims=True)
            dy = dpool * ps_ref[:, lanes]
            gbp_ref[:, lanes] += jnp.sum(dy, axis=0, keepdims=True)
            dy_b = dy.astype(BF16)
            gwp_ref[g] += _dot_tn(mixed_b, dy_b)
            dmixed_ref[:, lanes] = _dot_nt(dy_b, wp_ref[g])
        gwo_ref[0:ATTN_WIDTH, :] += _dot_tn(a_ref[...], dx1b)
        gwo_ref[ATTN_WIDTH:, :] += _dot_tn(y_ref[...], dx1b)
        gbo_ref[...] += jnp.sum(dx1, axis=0, keepdims=True)

    row = lambda w: pl.BlockSpec((tm, w), lambda i: (i, 0))
    const = lambda shape: pl.BlockSpec(shape, lambda i: (0,) * len(shape))
    return _compute_call(
        body, (dx1, attn, u, u, w_pool, b_pool, pool_scale, w_out), grid=(t // tm,),
        in_specs=[row(D_MODEL), row(ATTN_WIDTH), row(POOL_WIDTH), _halo_before(tm, POOL_WIDTH),
                  _resident((4, LANES, LANES)), _resident((1, POOL_WIDTH)), _resident((1, POOL_WIDTH)),
                  _resident((D_MODEL, D_MODEL))],
        out_specs=[row(ATTN_WIDTH), row(POOL_WIDTH), const((D_MODEL, D_MODEL)), const((1, D_MODEL)),
                   const((4, LANES, LANES)), const((1, POOL_WIDTH)), const((1, POOL_WIDTH))],
        out_shape=[jax.ShapeDtypeStruct((t, ATTN_WIDTH), BF16), jax.ShapeDtypeStruct((t, POOL_WIDTH), F32),
                   jax.ShapeDtypeStruct((D_MODEL, D_MODEL), F32), jax.ShapeDtypeStruct((1, D_MODEL), F32),
                   jax.ShapeDtypeStruct((4, LANES, LANES), F32), jax.ShapeDtypeStruct((1, POOL_WIDTH), F32),
                   jax.ShapeDtypeStruct((1, POOL_WIDTH), F32)],
        scratch_shapes=[pltpu.VMEM((tm + POOL_HALO, POOL_WIDTH), F32), pltpu.VMEM((tm, POOL_WIDTH), BF16)],
        semantics=("arbitrary",), name="outproj_bwd", exchange=exchange)


def _attn_bwd_call(q, k, v, dout, probs, p_sinks, cos_t, sin_t, exchange=None):
    t = q.shape[0]
    nb = t // BLOCK
    gb = _blocks_per_step(nb)
    qb = gb * BLOCK
    steps = nb // gb

    def body(q_ref, kc_ref, kp_ref, vc_ref, vp_ref, do_ref, p_ref, psink_ref, cq_ref, sq_ref, ck_ref, sk_ref,
             dq_ref, dk_ref, dv_ref, dsink_ref, newk_ref, newv_ref, carryk_ref, carryv_ref, headk_ref, headv_ref):
        n = pl.program_id(0)
        first = _first_half_mask((BLOCK, LANES))

        @pl.when(n == 0)
        def _():
            dsink_ref[...] = jnp.zeros_like(dsink_ref)

        @pl.when(n < steps)
        def _():
            head_row = lax.broadcasted_iota(jnp.int32, dsink_ref.shape, 0)
            lane2 = lax.broadcasted_iota(jnp.int32, (2 * BLOCK, LANES), 1)
            row_low = lax.broadcasted_iota(jnp.int32, (LANES, BLOCK), 0) < HEAD_DIM
            dsink = jnp.zeros(dsink_ref.shape, F32)
            for g in range(gb):
                r0 = g * BLOCK
                kband = _kv_band(kc_ref, kp_ref, g)
                vband = _kv_band(vc_ref, vp_ref, g)
                kband_t = kband.T
                cq, sq = cq_ref[r0:r0 + BLOCK, :], sq_ref[r0:r0 + BLOCK, :]
                dk_heads, dv_heads = [], []
                for kh in range(2):
                    v2 = _head_in_both_halves(vband, kh)
                    kt_head = kband_t[HEAD_DIM * kh:HEAD_DIM * (kh + 1), :]
                    k2_t = jnp.concatenate([kt_head, kt_head], axis=0).astype(BF16)
                    qst = _stack_heads(q_ref, kh, r0)
                    dost = _stack_heads(do_ref, kh, r0)
                    p_b = p_ref[2 * g + kh]
                    p_t = p_b.astype(F32)
                    dp_t = _dot_nt(v2, dost)
                    delta = jnp.sum(p_t * dp_t, axis=0, keepdims=True)
                    ds_t = (p_t * (dp_t - delta)).astype(BF16)
                    sink_term = psink_ref[2 * g + kh] * delta
                    for h in range(4):
                        val = -jnp.sum(sink_term[:, BLOCK * h:BLOCK * (h + 1)])
                        dsink = dsink + jnp.where(head_row == 4 * kh + h, val, 0.0)
                    dq_t = _dot(k2_t, ds_t)
                    for jj in range(2):
                        slab = jnp.where(row_low, dq_t[:, BLOCK * 2 * jj:BLOCK * (2 * jj + 1)],
                                         dq_t[:, BLOCK * (2 * jj + 1):BLOCK * (2 * jj + 2)]).T
                        dq_ref[r0:r0 + BLOCK, LANES * (2 * kh + jj):LANES * (2 * kh + jj + 1)] = (
                            (slab * cq + _rope_partner(slab * sq, first)) * Q_SCALE).astype(BF16)
                    dk2 = _dot(ds_t, qst)
                    dv2 = _dot(p_b, dost)
                    dk_heads.append(dk2 + pltpu.roll(dk2, HEAD_DIM, 1))
                    dv_heads.append(dv2 + pltpu.roll(dv2, HEAD_DIM, 1))
                bandk = jnp.where(lane2 < HEAD_DIM, dk_heads[0], dk_heads[1])
                bandv = jnp.where(lane2 < HEAD_DIM, dv_heads[0], dv_heads[1])
                if g == 0:
                    headk_ref[...] = bandk[0:BLOCK]
                    headv_ref[...] = bandv[0:BLOCK]
                else:
                    newk_ref[r0 - BLOCK:r0, :] += bandk[0:BLOCK]
                    newv_ref[r0 - BLOCK:r0, :] += bandv[0:BLOCK]
                newk_ref[r0:r0 + BLOCK, :] = bandk[BLOCK:]
                newv_ref[r0:r0 + BLOCK, :] = bandv[BLOCK:]
            dsink_ref[...] += dsink

        @pl.when(n == steps)
        def _():
            headk_ref[...] = jnp.zeros_like(headk_ref)
            headv_ref[...] = jnp.zeros_like(headv_ref)

        @pl.when(n >= 1)
        def _():
            last = qb - BLOCK
            carryk_ref[last:, :] += headk_ref[...]
            carryv_ref[last:, :] += headv_ref[...]
            for g in range(gb):
                rows = slice(g * BLOCK, (g + 1) * BLOCK)
                dk = carryk_ref[rows, :]
                dk_ref[rows, :] = (dk * ck_ref[rows, :] + _rope_partner(dk * sk_ref[rows, :], first)).astype(BF16)
            dv_ref[...] = carryv_ref[...].astype(BF16)

        @pl.when(n < steps)
        def _():
            carryk_ref[...] = newk_ref[...]
            carryv_ref[...] = newv_ref[...]

    cur = lambda w: pl.BlockSpec((qb, w), lambda n: (jnp.minimum(n, steps - 1), 0))
    late = lambda w: pl.BlockSpec((qb, w), lambda n: (jnp.maximum(n - 1, 0), 0))
    prev = lambda w: pl.BlockSpec((BLOCK, w), lambda n: (jnp.maximum(jnp.minimum(n, steps - 1) * gb - 1, 0), 0))
    return _compute_call(
        body, (q, k, k, v, v, dout, probs, p_sinks, cos_t, sin_t, cos_t, sin_t), grid=(steps + 1,),
        in_specs=[cur(ATTN_WIDTH), cur(KV_WIDTH), prev(KV_WIDTH), cur(KV_WIDTH), prev(KV_WIDTH), cur(ATTN_WIDTH),
                  pl.BlockSpec((2 * gb, 2 * BLOCK, 4 * BLOCK), lambda n: (jnp.minimum(n, steps - 1), 0, 0)),
                  pl.BlockSpec((2 * gb, 1, 4 * BLOCK), lambda n: (jnp.minimum(n, steps - 1), 0, 0)),
                  cur(LANES), cur(LANES), late(LANES), late(LANES)],
        out_specs=[cur(ATTN_WIDTH), late(KV_WIDTH), late(KV_WIDTH), pl.BlockSpec((8, LANES), lambda n: (0, 0))],
        out_shape=[jax.ShapeDtypeStruct((t, ATTN_WIDTH), BF16), jax.ShapeDtypeStruct((t, KV_WIDTH), BF16),
                   jax.ShapeDtypeStruct((t, KV_WIDTH), BF16), jax.ShapeDtypeStruct((8, LANES), F32)],
        scratch_shapes=[pltpu.VMEM((qb, LANES), F32), pltpu.VMEM((qb, LANES), F32),
                        pltpu.VMEM((qb, LANES), F32), pltpu.VMEM((qb, LANES), F32),
                        pltpu.VMEM((BLOCK, LANES), F32), pltpu.VMEM((BLOCK, LANES), F32)],
        semantics=("arbitrary",), name="attn_bwd", exchange=exchange)


def _inproj_bwd_call(dq, dk, dv, dmixed, x, dx1, w_in, g_mix, exchange=None):
    t = x.shape[0]
    tm = _token_tile(t, 1024)
    n_tiles = t // tm

    def body(dq_ref, dk_ref, dv_ref, dm_ref, halo_ref, x_ref, dx1_ref, w_ref, g_ref,
             dx_ref, gw_ref, gb_ref, gg_ref, sc_ref):
        i = pl.program_id(0)

        @pl.when(i == 0)
        def _():
            for ref in (gw_ref, gb_ref, gg_ref):
                ref[...] = jnp.zeros_like(ref)

        tok = i * tm + lax.broadcasted_iota(jnp.int32, (tm, 1), 0)
        halo = jnp.where(i == n_tiles - 1, 0.0, halo_ref[...])
        du = []
        for g, size in enumerate(POOL_SIZES):
            lanes = slice(LANES * g, LANES * (g + 1))
            dm = dm_ref[:, lanes]
            sc_ref[0:tm, lanes] = dm / jnp.minimum(tok + 1, size).astype(F32)
            sc_ref[tm:, lanes] = halo[:, lanes] / float(size)
            acc = -dm
            for j in range(size):
                acc = acc + sc_ref[pl.ds(j, tm), lanes]
            du.append(acc)
        dz32 = jnp.concatenate([dq_ref[...].astype(F32), dk_ref[...].astype(F32), dv_ref[...].astype(F32)] + du,
                               axis=1)
        dz = dz32.astype(BF16)
        gb_ref[...] += jnp.sum(dz32, axis=0, keepdims=True)
        xh, r = _rms_stats(x_ref[...])
        g = g_ref[...]
        gw_ref[...] += _dot_tn(dz, (xh * g).astype(BF16))
        dh = _dot(dz, w_ref[...])
        gg_ref[...] += jnp.sum(dh * xh, axis=0, keepdims=True)
        dx_ref[...] = dx1_ref[...] + _rms_bwd(dh, xh, r, g)

    row = lambda w: pl.BlockSpec((tm, w), lambda i: (i, 0))
    const = lambda shape: pl.BlockSpec(shape, lambda i: (0,) * len(shape))
    last_halo = t // POOL_HALO - 1
    halo_after = pl.BlockSpec((POOL_HALO, POOL_WIDTH),
                              lambda i: (jnp.minimum((i + 1) * (tm // POOL_HALO), last_halo), 0))
    return _compute_call(
        body, (dq, dk, dv, dmixed, dmixed, x, dx1, w_in, g_mix), grid=(n_tiles,),
        in_specs=[row(ATTN_WIDTH), row(KV_WIDTH), row(KV_WIDTH), row(POOL_WIDTH), halo_after, row(D_MODEL),
                  row(D_MODEL), _resident((IN_WIDTH, D_MODEL)), _resident((1, D_MODEL))],
        out_specs=[row(D_MODEL), const((IN_WIDTH, D_MODEL)), const((1, IN_WIDTH)), const((1, D_MODEL))],
        out_shape=[jax.ShapeDtypeStruct((t, D_MODEL), F32), jax.ShapeDtypeStruct((IN_WIDTH, D_MODEL), F32),
                   jax.ShapeDtypeStruct((1, IN_WIDTH), F32), jax.ShapeDtypeStruct((1, D_MODEL), F32)],
        scratch_shapes=[pltpu.VMEM((tm + POOL_HALO, POOL_WIDTH), F32)],
        semantics=("arbitrary",), name="inproj_bwd", exchange=exchange)


def _reduce_to_owner(grads, place, hosts, wire_dtypes):
    me, c = place
    landed = hosts[0](_sibling_exchange(grads))
    partials = _pair_sum_call(grads, landed, c, wire_dtypes)
    slots = hosts[1](_chip_exchange(partials))
    return _chip_sum_call(partials, slots, me, c)


def _local_step(x, target, small, w_in, w_out, w_ffn, place=None):
    t = x.shape[0]
    on_mesh = place is not None
    cos_t, sin_t = _rope_tables(t)
    w_pool = small["w_pool"].astype(BF16)
    w_ffn = list(w_ffn)

    def gathering(bufs, relay_early=0):
        return _gather_exchange(bufs, relay_early) if on_mesh else None

    (h0,), got = _norm_call(x, small["g_mix"], exchange=gathering([w_in]))
    if on_mesh:
        w_in = got[0].reshape(IN_WIDTH, D_MODEL)
    (q, k, v, u), got = _inproj_call(h0, w_in, small["b_in"], cos_t, sin_t, exchange=gathering([w_out], 3))
    if on_mesh:
        w_out = got[0].reshape(D_MODEL, D_MODEL)
    bias_t, bias_first_t = _band_biases()
    (attn, probs, p_sinks), got = _attn_fwd_call(q, k, v, small["sinks"], bias_t, bias_first_t, exchange=gathering(w_ffn[:2]))
    if on_mesh:
        w_ffn[:2] = got
    (x1, h1), got = _outproj_call(attn, u, x, w_pool, small["b_pool"], small["pool_scale"], w_out,
                                          small["b_out"], small["g_ffn"], exchange=gathering(w_ffn[2:], 4))
    if on_mesh:
        w_ffn[2:] = got
    w_gate, w_up, w_down = w_ffn
    act, dgate, dup, dx1, dx2b, loss, g_g_final, g_g_ffn = _ffn_call(h1, x1, target, w_gate, w_up, w_down,
                                                                      small["g_ffn"], small["g_final"])
    kept = {}

    def outproj_bwd(exchange=None):
        kept["outproj"], results = _outproj_bwd_call(dx1, attn, u, w_pool, small["b_pool"], small["pool_scale"],
                                                     w_out, exchange=exchange)
        return results

    def attn_bwd(exchange=None):
        kept["attn"], results = _attn_bwd_call(q, k, v, kept["outproj"][0], probs, p_sinks, cos_t, sin_t,
                                               exchange=exchange)
        return results

    if on_mesh:
        me, c = place
        wave1, _ = _ffn_bwd_weights_call([dgate, dup], h1, "ffn_bwd_weights_gate_up")
        wave2, landed1 = _ffn_bwd_weights_call([act], dx2b, "ffn_bwd_weights_down",
                                               exchange=_sibling_exchange(wave1))
        partials1 = _pair_sum_call(wave1, landed1, c, [BF16] * 2)
        hosted = outproj_bwd(_both(_chip_exchange(partials1), _sibling_exchange(wave2)))
        slots1, landed2 = hosted[:2], hosted[2:]
        partials2 = _pair_sum_call(wave2, landed2, c, [BF16])
        halves1 = _chip_sum_call(partials1, slots1, me, c)
        slots2 = attn_bwd(_chip_exchange(partials2))
        ffn_grads = list(halves1) + list(_chip_sum_call(partials2, slots2, me, c))
    else:
        ffn_grads = (list(_ffn_bwd_weights_call([dgate, dup], h1, "ffn_bwd_weights_gate_up")[0])
                     + list(_ffn_bwd_weights_call([act], dx2b, "ffn_bwd_weights_down")[0]))
        outproj_bwd(), attn_bwd()
    dq, dk, dv, _ = kept["attn"]
    kept["inproj"], _ = _inproj_bwd_call(dq, dk, dv, kept["outproj"][1], x, dx1, w_in, small["g_mix"])
    _, _, g_w_out, g_b_out, g_w_pool, g_b_pool, g_pool_scale = kept["outproj"]
    g_sinks = kept["attn"][3]
    grad_x, g_w_in, g_b_in, g_g_mix = kept["inproj"]
    small_grads = dict(g_mix=g_g_mix, b_in=g_b_in, sinks=g_sinks[:, 0], w_pool=g_w_pool,
                       b_pool=g_b_pool, pool_scale=g_pool_scale, b_out=g_b_out, g_ffn=g_g_ffn, g_final=g_g_final)
    big_grads = dict(w_in=g_w_in, w_out=g_w_out, w_gate=ffn_grads[0], w_up=ffn_grads[1], w_down=ffn_grads[2])
    return loss, grad_x, small_grads, big_grads


def _cast_place_call(shards, me):
    na = len(shards)
    split = 2

    def body(me_ref, *refs):
        for a in range(na):
            refs[na + a][...] = refs[a][...].astype(BF16)

    in_specs = [pl.BlockSpec((s.shape[0] // split, s.shape[1]), lambda i, me_ref: (i, 0)) for s in shards]
    out_specs = [pl.BlockSpec((None, s.shape[0] // split, s.shape[1]), lambda i, me_ref: (me_ref[0], i, 0))
                 for s in shards]
    return pl.pallas_call(
        body,
        grid_spec=pltpu.PrefetchScalarGridSpec(num_scalar_prefetch=1, grid=(split,), in_specs=in_specs,
                                               out_specs=out_specs),
        out_shape=[jax.ShapeDtypeStruct((N_SHARDS,) + s.shape, BF16) for s in shards],
        compiler_params=_params("parallel"), name="cast_place_weights",
    )(me.reshape(1), *shards)


def _pair_sum_call(grads, landed, c, wire_dtypes):
    na = len(grads)
    split = 2

    def body(c_ref, *refs):
        ins, lands, outs = refs[:na], refs[na:2 * na], refs[2 * na:]
        for a in range(na):
            outs[a][...] = (ins[a][...] + lands[a][...]).astype(outs[a].dtype)

    in_specs, land_specs, out_specs, views = [], [], [], []
    for g in grads:
        ns, r, cols = g.shape
        rows = r // 2 // split
        views.append(g.reshape(ns, 2, r // 2, cols))
        in_specs.append(pl.BlockSpec((None, None, rows, cols), lambda s, i, c_ref: (s, c_ref[0], i, 0)))
        land_specs.append(pl.BlockSpec((None, rows, cols), lambda s, i, c_ref: (s, i, 0)))
        out_specs.append(pl.BlockSpec((None, rows, cols), lambda s, i, c_ref: (s, i, 0)))
    return pl.pallas_call(
        body,
        grid_spec=pltpu.PrefetchScalarGridSpec(num_scalar_prefetch=1, grid=(N_SHARDS, split),
                                               in_specs=in_specs + land_specs, out_specs=out_specs),
        out_shape=[jax.ShapeDtypeStruct(l.shape, dt) for l, dt in zip(landed, wire_dtypes)],
        compiler_params=_params("parallel", "parallel"), name="reduce_pair_sum",
    )(c.reshape(1), *views, *landed)


def _chip_sum_call(partials, slots, me, c):
    na = len(slots)
    split = 2

    def body(place_ref, *refs):
        mine, lands, outs = refs[:na], refs[na:2 * na], refs[2 * na:]
        for j in range(N_SHARDS):
            @pl.when(place_ref[0] == j)
            def _():
                for a in range(na):
                    terms = [mine[a][...] if k == j else lands[a][k] for k in range(N_SHARDS)]
                    t0, t1, t2, t3 = (term.astype(F32) for term in terms)
                    outs[a][...] = ((t0 + t1) + t2) + t3

    mine_specs, land_specs, out_specs = [], [], []
    for s in slots:
        ns, h, cols = s.shape
        mine_specs.append(pl.BlockSpec((None, h // split, cols), lambda i, place: (place[0], i, 0)))
        land_specs.append(pl.BlockSpec((ns, h // split, cols), lambda i, place: (0, i, 0)))
        out_specs.append(pl.BlockSpec((h // split, cols), lambda i, place: (place[1] * split + i, 0)))
    return pl.pallas_call(
        body,
        grid_spec=pltpu.PrefetchScalarGridSpec(num_scalar_prefetch=1, grid=(split,),
                                               in_specs=mine_specs + land_specs, out_specs=out_specs),
        out_shape=[jax.ShapeDtypeStruct((2 * s.shape[1], s.shape[2]), F32) for s in slots],
        compiler_params=_params("parallel"), name="reduce_chip_sum",
    )(jnp.stack([me, c]), *partials, *slots)


def _adamw(w, g, m, v):
    m = ADAM_B1 * m + (1.0 - ADAM_B1) * g
    v = ADAM_B2 * v + (1.0 - ADAM_B2) * jnp.square(g)
    m_hat = m / (1.0 - ADAM_B1 ** ADAM_STEP)
    v_hat = v / (1.0 - ADAM_B2 ** ADAM_STEP)
    return -ADAM_LR * (m_hat / (jnp.sqrt(v_hat) + ADAM_EPS) + ADAM_WD * w), m, v


def _adamw_call(ws, gs, ms, vs):
    na = len(ws)
    split = 4

    def body(*refs):
        w_refs, g_refs, m_refs, v_refs = (refs[k * na:(k + 1) * na] for k in range(4))
        d_refs, nm_refs, nv_refs = (refs[(4 + k) * na:(5 + k) * na] for k in range(3))
        for a in range(na):
            d_refs[a][...], nm_refs[a][...], nv_refs[a][...] = _adamw(
                w_refs[a][...], g_refs[a][...], m_refs[a][...], v_refs[a][...])

    specs = [pl.BlockSpec((w.shape[0] // split, w.shape[1]), lambda i: (i, 0)) for w in ws]
    shapes = [jax.ShapeDtypeStruct(w.shape, F32) for w in ws]
    outs = pl.pallas_call(
        body, grid=(split,), in_specs=specs * 4, out_specs=specs * 3, out_shape=shapes * 3,
        compiler_params=_params("parallel"), name="adamw",
    )(*ws, *gs, *ms, *vs)
    return outs[:na], outs[na:2 * na], outs[2 * na:]


def _small_rows():
    first, row = {}, 0
    for name, (r, c) in SMALL_VIEWS.items():
        first[name] = row
        row += r * (-(-c // LANES))
    return first, row, -(-(row + 1) // 64) * 64


def _pack_small(arrays, loss):
    _, used, rows = _small_rows()
    parts = []
    for name, (r, c) in SMALL_VIEWS.items():
        a = arrays[name].reshape(r, c)
        parts.append(jnp.pad(a, ((0, 0), (0, LANES - c))) if c < LANES else a.reshape(r * c // LANES, LANES))
    parts += [loss, jnp.zeros((rows - used - 1, LANES), F32)]
    return jnp.concatenate(parts, axis=0)


def _adamw_small_call(ws, packed_grads, ms, vs):
    names = list(SMALL_VIEWS)
    first, _, _ = _small_rows()
    n = len(names)

    def body(g_ref, *refs):
        w_refs, m_refs, v_refs = refs[:n], refs[n:2 * n], refs[2 * n:3 * n]
        outs = refs[3 * n:]
        for a, name in enumerate(names):
            r, c = SMALL_VIEWS[name]
            r0 = first[name]
            if c == LANES:
                g = g_ref[r0:r0 + r, :]
            elif c < LANES:
                g = g_ref[r0:r0 + 1, 0:c]
            else:
                g = jnp.concatenate([g_ref[r0 + j:r0 + j + 1, :] for j in range(c // LANES)], axis=1)
            delta, m, v = _adamw(w_refs[a][...], g, m_refs[a][...], v_refs[a][...])
            for k, val in enumerate((g, delta, m, v)):
                outs[k * n + a][...] = val

    shapes = [jax.ShapeDtypeStruct(SMALL_VIEWS[name], F32) for name in names]
    outs = pl.pallas_call(
        body, out_shape=shapes * 4, compiler_params=pltpu.CompilerParams(vmem_limit_bytes=VMEM_LIMIT),
        name="adamw_small",
    )(packed_grads, *[ws[k] for k in names], *[ms[k] for k in names], *[vs[k] for k in names])
    return [dict(zip(names, outs[k * n:(k + 1) * n])) for k in range(4)]


def kernel(x, g_mix, w_in, b_in, sinks, w_pool, b_pool, pool_scale, w_out, b_out, g_ffn, w_gate, w_up, w_down, g_final, loss_target, m_g_mix, m_w_in, m_b_in, m_sinks, m_w_pool, m_b_pool, m_pool_scale, m_w_out, m_b_out, m_g_ffn, m_w_gate, m_w_up, m_w_down, m_g_final, v_g_mix, v_w_in, v_b_in, v_sinks, v_w_pool, v_b_pool, v_pool_scale, v_w_out, v_b_out, v_g_ffn, v_w_gate, v_w_up, v_w_down, v_g_final):
    given = dict(locals())
    big_names = ("w_in", "w_out", "w_gate", "w_up", "w_down")
    out_shapes = {n: given[n].shape for n in SMALL_NAMES + big_names}
    c = lax.axis_index("c")

    me = 2 * lax.axis_index("x") + lax.axis_index("y")
    transposed = ("w_in", "w_gate", "w_up")

    def shard_view(a, name):
        return a[0].T if name in transposed else a[0]

    placed = _cast_place_call([shard_view(given[n], n) for n in big_names], me)

    small = dict(g_mix=g_mix, b_in=b_in, sinks=sinks, w_pool=w_pool[0], b_pool=b_pool.reshape(1, POOL_WIDTH),
                 pool_scale=pool_scale.reshape(1, POOL_WIDTH), b_out=b_out, g_ffn=g_ffn,
                 g_final=g_final.reshape(1, D_MODEL))
    loss, grad_x, small_grads, big_grads = _local_step(
        x[0], loss_target[0], small, placed[0], placed[1], placed[2:], place=(me, c))

    packed = _pack_small(small_grads, loss)
    grads = [
        big_grads["w_in"].reshape(N_SHARDS, IN_WIDTH // N_SHARDS, D_MODEL),
        big_grads["w_out"].reshape(N_SHARDS, D_MODEL // N_SHARDS, D_MODEL),
        jnp.broadcast_to(packed[None], (N_SHARDS,) + packed.shape),
    ]
    alone = [functools.partial(_exchange_call, name=name) for name in
             ("reduce_sibling_exchange", "reduce_chip_exchange")]
    h_in, h_out, h_small = _reduce_to_owner(grads, (me, c), alone, [BF16, BF16, F32])
    totals = _exchange_call(_sibling_allgather(
        [h_in, h_out, big_grads["w_gate"], big_grads["w_up"], big_grads["w_down"], h_small]), "reduce_sibling_allgather")
    loss = totals[-1][_small_rows()[1], 0]

    def owned(prefix):
        return [shard_view(given[prefix + n], n) for n in big_names]

    def owned_small(prefix):
        return {n: given[prefix + n].reshape(SMALL_VIEWS[n]) for n in SMALL_NAMES}

    big = (totals[:-1],) + _adamw_call(owned(""), totals[:-1], owned("m_"), owned("v_"))
    small_groups = _adamw_small_call(owned_small(""), totals[-1], owned_small("m_"), owned_small("v_"))

    order = ("g_mix", "w_in", "b_in", "sinks", "w_pool", "b_pool", "pool_scale", "w_out", "b_out", "g_ffn",
             "w_gate", "w_up", "w_down", "g_final")
    results = [loss, grad_x.reshape(x.shape)]
    for group, small_group in zip(big, small_groups):
        named = {n: (a.T if n in transposed else a) for n, a in zip(big_names, group)}
        named.update(small_group)
        results += [named[n].reshape(out_shapes[n]) for n in order]
    return tuple(results)
```

```python
import functools
from typing import Any, Callable, Mapping, NamedTuple, Sequence

import jax
import jax.numpy as jnp
from jax import lax
from jax.experimental import pallas as pl
from jax.experimental.pallas import tpu as pltpu

F32 = jnp.float32
BF16 = jnp.bfloat16

D_MODEL = 1024
ATTN_WIDTH = 512
KV_WIDTH = 128
POOL_WIDTH = 512
IN_WIDTH = ATTN_WIDTH + 2 * KV_WIDTH + POOL_WIDTH
HEAD_DIM = 64
N_Q_HEADS = 8
BLOCK = 128
POOL_SIZES = (2, 4, 8, 16)
POOL_HALO = 16
ROPE_THETA = 10000.0
RMS_EPS = 1e-5
Q_SCALE = HEAD_DIM ** -0.5

ADAM_LR = 0.001
ADAM_B1 = 0.9
ADAM_B2 = 0.999
ADAM_EPS = 1e-08
ADAM_WD = 0.01
ADAM_STEP = 10

N_SHARDS = 4
LANES = 128
VMEM_LIMIT = 56 * 1024 * 1024
RING_SLOTS = 3
FFN_TOKEN_TILE = 256

MESH = pl.DeviceIdType.MESH
HBM_SPEC = pl.BlockSpec(memory_space=pltpu.HBM)

SMALL_VIEWS = dict(w_pool=(4 * LANES, LANES), b_pool=(4, LANES), pool_scale=(4, LANES), g_mix=(1, D_MODEL),
                   b_in=(1, IN_WIDTH), b_out=(1, D_MODEL), g_ffn=(1, D_MODEL), g_final=(1, D_MODEL),
                   sinks=(1, N_Q_HEADS))
SMALL_NAMES = tuple(SMALL_VIEWS)


def _dot(a, b):
    return jnp.dot(a, b, preferred_element_type=F32)


def _dot_nt(a, b):
    return lax.dot_general(a, b, (((1,), (1,)), ((), ())), preferred_element_type=F32)


def _dot_tn(a, b):
    return lax.dot_general(a, b, (((0,), (0,)), ((), ())), preferred_element_type=F32)


def _params(*semantics):
    return pltpu.CompilerParams(dimension_semantics=semantics, vmem_limit_bytes=VMEM_LIMIT)


def _resident(shape):
    return pl.BlockSpec(shape, lambda *_: (0,) * len(shape), pipeline_mode=pl.Buffered(1))


def _token_tile(t, largest=512):
    for tm in (2048, 1024, 512, 256, 128):
        if tm <= largest and t % tm == 0:
            return tm
    raise ValueError(f"sequence length {t} is not a multiple of 128")


def _rms_stats(xf):
    r = lax.rsqrt(jnp.mean(xf * xf, axis=-1, keepdims=True) + RMS_EPS)
    return xf * r, r


def _rms_bwd(dy, xh, r, g):
    dxh = dy * g
    return r * (dxh - xh * jnp.mean(dxh * xh, axis=-1, keepdims=True))


def _rope_partner(t, first):
    return jnp.where(first, pltpu.roll(t, LANES - HEAD_DIM // 2, 1), pltpu.roll(t, HEAD_DIM // 2, 1))


def _first_half_mask(shape):
    lane = lax.broadcasted_iota(jnp.int32, shape, 1)
    return (lane % HEAD_DIM) < (HEAD_DIM // 2)


def _rope_tables(t):
    inv_freq = 1.0 / (ROPE_THETA ** (jnp.arange(0, HEAD_DIM, 2, dtype=F32) / HEAD_DIM))
    ang = jnp.arange(t, dtype=F32)[:, None] * inv_freq[None, :]
    cos, sin = jnp.cos(ang), jnp.sin(ang)
    cos_t = jnp.tile(jnp.concatenate([cos, cos], axis=-1), (1, LANES // HEAD_DIM))
    sin_t = jnp.tile(jnp.concatenate([-sin, sin], axis=-1), (1, LANES // HEAD_DIM))
    return cos_t, sin_t


class _Exchange(NamedTuple):
    inputs: Sequence[Any]
    out_shapes: Sequence[Any]
    aliases: Mapping[int, int]
    sems: Sequence[Any]
    start: Callable[..., None]
    finish: Callable[..., None]
    relay: Callable[..., None] = None
    relay_early: int = 0


def _compute_call(body, operands, *, grid, in_specs, out_specs, out_shape, scratch_shapes=(), semantics, name,
                  exchange=None):
    n_in, n_out, n_scr = len(in_specs), len(out_specs), len(scratch_shapes)
    if exchange is None:
        return pl.pallas_call(body, grid=grid, in_specs=in_specs, out_specs=out_specs, out_shape=out_shape,
                              scratch_shapes=scratch_shapes, compiler_params=_params(*semantics),
                              name=name)(*operands), []
    k_in, k_out = len(exchange.inputs), len(exchange.out_shapes)

    def hosting(*refs):
        ins, rest = refs[:n_in], refs[n_in:]
        ex_ins, rest = rest[:k_in], rest[k_in:]
        outs, rest = rest[:n_out], rest[n_out:]
        ex_outs, rest = rest[:k_out], rest[k_out:]
        scratch, sems = rest[:n_scr], rest[n_scr:]
        steps = [pl.program_id(d) for d in range(len(grid))]
        is_first = functools.reduce(jnp.logical_and, [s == 0 for s in steps])
        is_last = functools.reduce(jnp.logical_and, [s == g - 1 for s, g in zip(steps, grid)])
        is_relay = functools.reduce(jnp.logical_and, [s == max(g - 1 - exchange.relay_early, 0)
                                                      for s, g in zip(steps, grid)])

        @pl.when(is_first)
        def _():
            exchange.start(ex_ins, ex_outs, sems)

        body(*ins, *outs, *scratch)

        if exchange.relay is not None:
            @pl.when(is_relay)
            def _():
                exchange.relay(ex_ins, ex_outs, sems)

        @pl.when(is_last)
        def _():
            exchange.finish(ex_ins, ex_outs, sems)

    results = pl.pallas_call(
        hosting, grid=grid,
        in_specs=list(in_specs) + [HBM_SPEC] * k_in, out_specs=list(out_specs) + [HBM_SPEC] * k_out,
        out_shape=list(out_shape) + list(exchange.out_shapes),
        scratch_shapes=list(scratch_shapes) + list(exchange.sems),
        input_output_aliases={n_in + i: n_out + j for i, j in exchange.aliases.items()},
        compiler_params=_params(*semantics), name=name,
    )(*operands, *exchange.inputs)
    return results[:n_out], results[n_out:]


def _exchange_call(exchange, name):
    n_in, n_out = len(exchange.inputs), len(exchange.out_shapes)

    def body(*refs):
        ins, outs, sems = refs[:n_in], refs[n_in:n_in + n_out], refs[n_in + n_out:]
        exchange.start(ins, outs, sems)
        if exchange.relay is not None:
            exchange.relay(ins, outs, sems)
        exchange.finish(ins, outs, sems)

    return pl.pallas_call(
        body, in_specs=[HBM_SPEC] * n_in, out_specs=[HBM_SPEC] * n_out, out_shape=list(exchange.out_shapes),
        input_output_aliases=dict(exchange.aliases), scratch_shapes=list(exchange.sems),
        compiler_params=pltpu.CompilerParams(has_side_effects=True), name=name,
    )(*exchange.inputs)


def _mesh_place():
    x, y, c = lax.axis_index("x"), lax.axis_index("y"), lax.axis_index("c")
    other_chips = [(1 - x, y), (x, 1 - y), (1 - x, 1 - y)]
    return x, y, c, other_chips


def _half(ref, c, rows):
    return ref.at[pl.ds(pl.multiple_of(c * rows, 16), rows), :]


def _remote(src, dst, send_sems, recv_sems, idx, to):
    return functools.partial(pltpu.make_async_remote_copy, src_ref=src, dst_ref=dst, send_sem=send_sems.at[idx],
                             recv_sem=recv_sems.at[idx], device_id=to, device_id_type=MESH)


def _gather_exchange(bufs, relay_early=0):
    na = len(bufs)

    def copies(outs, sems):
        send_sems, recv_sems = sems
        x, y, c, other_chips = _mesh_place()
        me, sibling = 2 * x + y, (x, y, 1 - c)
        ici_out, ici_in, fwd_out, fwd_in = [], [], [], []
        for j, (cx, cy) in enumerate(other_chips):
            for a in range(na):
                rows = outs[a].shape[1] // 2
                mine = _half(outs[a].at[me], c, rows)
                landed = _half(outs[a].at[2 * cx + cy], c, rows)
                passed = _half(outs[a].at[2 * cx + cy], 1 - c, rows)
                ici_out.append(_remote(mine, mine, send_sems, recv_sems, j * na + a, (cx, cy, c)))
                ici_in.append(_remote(landed, landed, send_sems, recv_sems, j * na + a, (cx, cy, c)))
                fwd_out.append(_remote(landed, landed, send_sems, recv_sems, (3 + j) * na + a, sibling))
                fwd_in.append(_remote(passed, passed, send_sems, recv_sems, (3 + j) * na + a, sibling))
        return ici_out, ici_in, fwd_out, fwd_in

    def start(ins, outs, sems):
        for cp in copies(outs, sems)[0]:
            cp().start()

    def relay(ins, outs, sems):
        _, ici_in, fwd_out, _ = copies(outs, sems)
        for arrived, forward in zip(ici_in, fwd_out):
            arrived().wait_recv()
            forward().start()

    def finish(ins, outs, sems):
        ici_out, _, fwd_out, fwd_in = copies(outs, sems)
        for cp in fwd_in:
            cp().wait_recv()
        for cp in ici_out + fwd_out:
            cp().wait_send()

    return _Exchange(inputs=bufs, out_shapes=[jax.ShapeDtypeStruct(b.shape, b.dtype) for b in bufs],
                     aliases={a: a for a in range(na)},
                     sems=[pltpu.SemaphoreType.DMA((6 * na,)), pltpu.SemaphoreType.DMA((6 * na,))],
                     start=start, finish=finish, relay=relay, relay_early=relay_early)


def _sibling_exchange(grads):
    na = len(grads)

    def copies(ins, outs, sems):
        x, y, c, _ = _mesh_place()
        out = []
        for a in range(na):
            rows = ins[a].shape[1] // 2
            src = ins[a].at[:, pl.ds(pl.multiple_of((1 - c) * rows, 8), rows), :]
            out.append(_remote(src, outs[a], sems[0], sems[1], a, (x, y, 1 - c)))
        return out

    def start(ins, outs, sems):
        for cp in copies(ins, outs, sems):
            cp().start()

    def finish(ins, outs, sems):
        for cp in copies(ins, outs, sems):
            cp().wait_recv()
            cp().wait_send()

    return _Exchange(inputs=grads,
                     out_shapes=[jax.ShapeDtypeStruct((g.shape[0], g.shape[1] // 2, g.shape[2]), g.dtype)
                                 for g in grads],
                     aliases={}, sems=[pltpu.SemaphoreType.DMA((na,)), pltpu.SemaphoreType.DMA((na,))],
                     start=start, finish=finish)


def _chip_exchange(partials):
    na = len(partials)

    def copies(ins, outs, sems):
        x, y, c, other_chips = _mesh_place()
        me = 2 * x + y
        sends, arrivals = [], []
        for j, (cx, cy) in enumerate(other_chips):
            for a in range(na):
                slot = outs[a].at[2 * cx + cy]
                sends.append(_remote(ins[a].at[2 * cx + cy], outs[a].at[me], sems[0], sems[1], j * na + a, (cx, cy, c)))
                arrivals.append(_remote(slot, slot, sems[0], sems[1], j * na + a, (cx, cy, c)))
        return sends, arrivals

    def start(ins, outs, sems):
        for cp in copies(ins, outs, sems)[0]:
            cp().start()

    def finish(ins, outs, sems):
        sends, arrivals = copies(ins, outs, sems)
        for cp in arrivals:
            cp().wait_recv()
        for cp in sends:
            cp().wait_send()

    return _Exchange(inputs=partials, out_shapes=[jax.ShapeDtypeStruct(p.shape, p.dtype) for p in partials],
                     aliases={}, sems=[pltpu.SemaphoreType.DMA((3 * na,)), pltpu.SemaphoreType.DMA((3 * na,))],
                     start=start, finish=finish)


def _sibling_allgather(bufs):
    na = len(bufs)

    def piece(outs, sems, a, half):
        x, y, c, _ = _mesh_place()
        rows = outs[a].shape[0] // 2
        rows_ref = outs[a].at[pl.ds(pl.multiple_of(half * rows, 8), rows), :]
        return _remote(rows_ref, rows_ref, sems[0], sems[1], a, (x, y, 1 - c))

    def start(ins, outs, sems):
        c = lax.axis_index("c")
        for a in range(na):
            piece(outs, sems, a, c)().start()

    def finish(ins, outs, sems):
        c = lax.axis_index("c")
        for a in range(na):
            piece(outs, sems, a, 1 - c)().wait_recv()
        for a in range(na):
            piece(outs, sems, a, c)().wait_send()

    return _Exchange(inputs=bufs, out_shapes=[jax.ShapeDtypeStruct(b.shape, b.dtype) for b in bufs],
                     aliases={a: a for a in range(na)},
                     sems=[pltpu.SemaphoreType.DMA((na,)), pltpu.SemaphoreType.DMA((na,))],
                     start=start, finish=finish)


def _norm_call(x, g_mix, exchange=None):
    t = x.shape[0]
    tm = _token_tile(t)

    def body(x_ref, g_ref, h_ref):
        xh, _ = _rms_stats(x_ref[...])
        h_ref[...] = (xh * g_ref[...]).astype(BF16)

    row = pl.BlockSpec((tm, D_MODEL), lambda i: (i, 0))
    return _compute_call(body, (x, g_mix), grid=(t // tm,), in_specs=[row, _resident((1, D_MODEL))],
                         out_specs=[row], out_shape=[jax.ShapeDtypeStruct((t, D_MODEL), BF16)],
                         semantics=("parallel",), name="norm_fwd", exchange=exchange)


def _inproj_call(h0, w_in, b_in, cos_t, sin_t, exchange=None):
    t = h0.shape[0]
    tm = _token_tile(t, 1024)

    def body(h_ref, w_ref, b_ref, c_ref, s_ref, q_ref, k_ref, v_ref, u_ref):
        z = _dot_nt(h_ref[...], w_ref[...]) + b_ref[...]
        cos, sin = c_ref[...], s_ref[...]
        first = _first_half_mask((tm, LANES))
        for j in range(ATTN_WIDTH // LANES):
            zj = z[:, LANES * j:LANES * (j + 1)]
            q_ref[:, LANES * j:LANES * (j + 1)] = ((zj * cos + _rope_partner(zj, first) * sin) * Q_SCALE).astype(BF16)
        zk = z[:, ATTN_WIDTH:ATTN_WIDTH + KV_WIDTH]
        k_ref[...] = (zk * cos + _rope_partner(zk, first) * sin).astype(BF16)
        v_ref[...] = z[:, ATTN_WIDTH + KV_WIDTH:ATTN_WIDTH + 2 * KV_WIDTH].astype(BF16)
        u_ref[...] = z[:, ATTN_WIDTH + 2 * KV_WIDTH:]

    row = lambda w: pl.BlockSpec((tm, w), lambda i: (i, 0))
    return _compute_call(
        body, (h0, w_in, b_in, cos_t, sin_t), grid=(t // tm,),
        in_specs=[row(D_MODEL), _resident((IN_WIDTH, D_MODEL)), _resident((1, IN_WIDTH)), row(LANES), row(LANES)],
        out_specs=[row(ATTN_WIDTH), row(KV_WIDTH), row(KV_WIDTH), row(POOL_WIDTH)],
        out_shape=[jax.ShapeDtypeStruct((t, ATTN_WIDTH), BF16),
                   jax.ShapeDtypeStruct((t, KV_WIDTH), BF16), jax.ShapeDtypeStruct((t, KV_WIDTH), BF16),
                   jax.ShapeDtypeStruct((t, POOL_WIDTH), F32)],
        semantics=("parallel",), name="inproj_fwd", exchange=exchange)


def _head_in_both_halves(band, kv_head):
    lane = lax.broadcasted_iota(jnp.int32, band.shape, 1)
    own = (lane < HEAD_DIM) if kv_head == 0 else (lane >= HEAD_DIM)
    return jnp.where(own, band, pltpu.roll(band, HEAD_DIM, 1)).astype(BF16)


def _stack_heads(ref, kv_head, row0):
    low = lax.broadcasted_iota(jnp.int32, (BLOCK, LANES), 1) < HEAD_DIM
    parts = []
    for jj in range(2):
        slab = ref[row0:row0 + BLOCK, LANES * (2 * kv_head + jj):LANES * (2 * kv_head + jj + 1)]
        zero = jnp.zeros_like(slab)
        parts += [jnp.where(low, slab, zero), jnp.where(low, zero, slab)]
    return jnp.concatenate(parts, axis=0)


def _band_biases():
    c = lax.broadcasted_iota(jnp.int32, (2 * BLOCK, 4 * BLOCK), 0)
    i = lax.broadcasted_iota(jnp.int32, (2 * BLOCK, 4 * BLOCK), 1) % BLOCK
    vis = (c > i) & (c <= i + BLOCK)
    return jnp.where(vis, 0.0, -jnp.inf).astype(F32), jnp.where(vis & (c >= BLOCK), 0.0, -jnp.inf).astype(F32)


def _blocks_per_step(nb):
    for g in (8, 4, 2, 1):
        if nb % g == 0:
            return g


def _sink_row(sink_ref, kv_head):
    return jnp.concatenate([jnp.full((1, BLOCK), sink_ref[0, 4 * kv_head + g], F32) for g in range(4)], axis=1)


def _kv_band(cur_ref, prev_ref, g):
    r0 = g * BLOCK
    prev = prev_ref[...] if g == 0 else cur_ref[r0 - BLOCK:r0, :]
    return jnp.concatenate([prev, cur_ref[r0:r0 + BLOCK, :]], axis=0).astype(F32)


def _attn_fwd_call(q, k, v, sinks, bias, bias_first, exchange=None):
    t = q.shape[0]
    nb = t // BLOCK
    gb = _blocks_per_step(nb)
    qb = gb * BLOCK

    def body(sink_ref, q_ref, kc_ref, kp_ref, vc_ref, vp_ref, b_ref, bf_ref, o_ref, p_ref, psink_ref):
        n = pl.program_id(0)
        row_low = lax.broadcasted_iota(jnp.int32, (LANES, BLOCK), 0) < HEAD_DIM
        for g in range(gb):
            r0 = g * BLOCK
            kband = _kv_band(kc_ref, kp_ref, g)
            vband_t = _kv_band(vc_ref, vp_ref, g).T
            bias_g = jnp.where(n == 0, bf_ref[...], b_ref[...]) if g == 0 else b_ref[...]
            for kh in range(2):
                k2 = _head_in_both_halves(kband, kh)
                vt_head = vband_t[HEAD_DIM * kh:HEAD_DIM * (kh + 1), :]
                v2_t = jnp.concatenate([vt_head, vt_head], axis=0).astype(BF16)
                s_t = _dot_nt(k2, _stack_heads(q_ref, kh, r0)) + bias_g
                sink = _sink_row(sink_ref, kh)
                m = jnp.maximum(jnp.max(s_t, axis=0, keepdims=True), sink)
                e = jnp.exp(s_t - m)
                e_sink = jnp.exp(sink - m)
                inv = 1.0 / (jnp.sum(e, axis=0, keepdims=True) + e_sink)
                p_t = (e * inv).astype(BF16)
                p_ref[2 * g + kh] = p_t
                psink_ref[2 * g + kh] = e_sink * inv
                o_t = _dot(v2_t, p_t)
                for jj in range(2):
                    slab = jnp.where(row_low, o_t[:, BLOCK * 2 * jj:BLOCK * (2 * jj + 1)],
                                     o_t[:, BLOCK * (2 * jj + 1):BLOCK * (2 * jj + 2)]).T
                    o_ref[r0:r0 + BLOCK, LANES * (2 * kh + jj):LANES * (2 * kh + jj + 1)] = slab.astype(BF16)

    cur = lambda w: pl.BlockSpec((qb, w), lambda n: (n, 0))
    prev = lambda w: pl.BlockSpec((BLOCK, w), lambda n: (jnp.maximum(n * gb - 1, 0), 0))
    return _compute_call(
        body, (sinks, q, k, k, v, v, bias, bias_first), grid=(nb // gb,),
        in_specs=[pl.BlockSpec(memory_space=pltpu.SMEM), cur(ATTN_WIDTH), cur(KV_WIDTH), prev(KV_WIDTH),
                  cur(KV_WIDTH), prev(KV_WIDTH), _resident(bias.shape), _resident(bias.shape)],
        out_specs=[cur(ATTN_WIDTH), pl.BlockSpec((2 * gb, 2 * BLOCK, 4 * BLOCK), lambda n: (n, 0, 0)),
                   pl.BlockSpec((2 * gb, 1, 4 * BLOCK), lambda n: (n, 0, 0))],
        out_shape=[jax.ShapeDtypeStruct((t, ATTN_WIDTH), BF16),
                   jax.ShapeDtypeStruct((2 * nb, 2 * BLOCK, 4 * BLOCK), BF16),
                   jax.ShapeDtypeStruct((2 * nb, 1, 4 * BLOCK), F32)],
        semantics=("parallel",), name="attn_fwd", exchange=exchange)


def _pool_mixed(sc_ref, u_ref, halo_ref, i, tm):
    sc_ref[0:POOL_HALO, :] = jnp.where(i == 0, 0.0, halo_ref[...])
    sc_ref[POOL_HALO:, :] = u_ref[...]
    tok = i * tm + lax.broadcasted_iota(jnp.int32, (tm, 1), 0)
    mixed = []
    for g, size in enumerate(POOL_SIZES):
        lanes = slice(LANES * g, LANES * (g + 1))
        cur = sc_ref[pl.ds(POOL_HALO, tm), lanes]
        acc = cur
        for j in range(1, size):
            acc = acc + sc_ref[pl.ds(POOL_HALO - j, tm), lanes]
        count = jnp.minimum(tok + 1, size).astype(F32)
        mixed.append(acc / count - cur)
    return mixed


def _halo_before(tm, width):
    return pl.BlockSpec((POOL_HALO, width), lambda i: (jnp.maximum(i * (tm // POOL_HALO) - 1, 0), 0))


def _outproj_call(attn, u, x, w_pool, b_pool, pool_scale, w_out, b_out, g_ffn, exchange=None):
    t = x.shape[0]
    tm = _token_tile(t)

    def body(a_ref, u_ref, halo_ref, x_ref, wp_ref, bp_ref, ps_ref, wo_ref, bo_ref, g_ref,
             x1_ref, h1_ref, sc_ref, y_ref):
        i = pl.program_id(0)
        for g, mixed in enumerate(_pool_mixed(sc_ref, u_ref, halo_ref, i, tm)):
            lanes = slice(LANES * g, LANES * (g + 1))
            lin = _dot(mixed.astype(BF16), wp_ref[g]) + bp_ref[:, lanes]
            y_ref[:, lanes] = (lin * ps_ref[:, lanes]).astype(BF16)
        x1 = (x_ref[...] + _dot(a_ref[...], wo_ref[0:ATTN_WIDTH, :]) + _dot(y_ref[...], wo_ref[ATTN_WIDTH:, :])
              + bo_ref[...])
        x1_ref[...] = x1
        xh, _ = _rms_stats(x1)
        h1_ref[...] = (xh * g_ref[...]).astype(BF16)

    row = lambda w: pl.BlockSpec((tm, w), lambda i: (i, 0))
    return _compute_call(
        body, (attn, u, u, x, w_pool, b_pool, pool_scale, w_out, b_out, g_ffn), grid=(t // tm,),
        in_specs=[row(ATTN_WIDTH), row(POOL_WIDTH), _halo_before(tm, POOL_WIDTH), row(D_MODEL),
                  _resident((4, LANES, LANES)), _resident((1, POOL_WIDTH)), _resident((1, POOL_WIDTH)),
                  _resident((D_MODEL, D_MODEL)), _resident((1, D_MODEL)), _resident((1, D_MODEL))],
        out_specs=[row(D_MODEL), row(D_MODEL)],
        out_shape=[jax.ShapeDtypeStruct((t, D_MODEL), F32), jax.ShapeDtypeStruct((t, D_MODEL), BF16)],
        scratch_shapes=[pltpu.VMEM((tm + POOL_HALO, POOL_WIDTH), F32), pltpu.VMEM((tm, POOL_WIDTH), BF16)],
        semantics=("parallel",), name="outproj_fwd", exchange=exchange)


def _ffn_call(h1, x1, target, w_gate, w_up, w_down, g_ffn, g_final):
    t = x1.shape[0]
    tm = _token_tile(t, FFN_TOKEN_TILE)
    ns, fs, _ = w_gate.shape

    def body(h_ref, x1_ref, tgt_ref, wg_ref, wu_ref, wd_ref, g_ref, gf_ref,
             act_ref, dgate_ref, dup_ref, dx1_ref, dx2b_ref, loss_ref, dgf_ref, dg_ref, gate_ref, up_ref):
        @pl.when(pl.program_id(0) == 0)
        def _():
            loss_ref[...] = jnp.zeros_like(loss_ref)
            dgf_ref[...] = jnp.zeros_like(dgf_ref)
            dg_ref[...] = jnp.zeros_like(dg_ref)

        h = h_ref[...]
        x1 = x1_ref[...]
        x2 = x1
        for s in range(ns):
            gate = _dot_nt(h, wg_ref[s])
            up = _dot_nt(h, wu_ref[s])
            act = (gate * jax.nn.sigmoid(gate) * up).astype(BF16)
            gate_ref[s] = gate
            up_ref[s] = up
            act_ref[s] = act
            x2 = x2 + _dot(act, wd_ref[s])
        xh2, r2 = _rms_stats(x2)
        gf = gf_ref[...]
        err = xh2 * gf - tgt_ref[...]
        loss_ref[...] += 0.5 * jnp.sum(err * err) / D_MODEL
        dy = err / D_MODEL
        dgf_ref[...] += jnp.sum(dy * xh2, axis=0, keepdims=True)
        dx2 = _rms_bwd(dy, xh2, r2, gf)
        dx2b = dx2.astype(BF16)
        dx2b_ref[...] = dx2b
        dh = None
        for s in range(ns):
            dact = _dot_nt(dx2b, wd_ref[s])
            gate = gate_ref[s]
            sg = jax.nn.sigmoid(gate)
            dup = (dact * (gate * sg)).astype(BF16)
            dgate = (dact * up_ref[s] * (sg * (1.0 + gate * (1.0 - sg)))).astype(BF16)
            dgate_ref[s] = dgate
            dup_ref[s] = dup
            part = _dot(dgate, wg_ref[s]) + _dot(dup, wu_ref[s])
            dh = part if dh is None else dh + part
        xh1, r1 = _rms_stats(x1)
        dg_ref[...] += jnp.sum(dh * xh1, axis=0, keepdims=True)
        dx1_ref[...] = dx2 + _rms_bwd(dh, xh1, r1, g_ref[...])

    row = lambda w: pl.BlockSpec((tm, w), lambda i: (i, 0))
    shard_act = pl.BlockSpec((ns, tm, fs), lambda i: (0, i, 0))
    act_shape = jax.ShapeDtypeStruct((ns, t, fs), BF16)
    vec = pl.BlockSpec((1, D_MODEL), lambda i: (0, 0))
    return pl.pallas_call(
        body, grid=(t // tm,),
        in_specs=[row(D_MODEL), row(D_MODEL), row(D_MODEL), _resident((ns, fs, D_MODEL)),
                  _resident((ns, fs, D_MODEL)), _resident((ns, fs, D_MODEL)), _resident((1, D_MODEL)),
                  _resident((1, D_MODEL))],
        out_specs=[shard_act, shard_act, shard_act, row(D_MODEL), row(D_MODEL),
                   pl.BlockSpec((1, LANES), lambda i: (0, 0)), vec, vec],
        out_shape=[act_shape, act_shape, act_shape, jax.ShapeDtypeStruct((t, D_MODEL), F32),
                   jax.ShapeDtypeStruct((t, D_MODEL), BF16), jax.ShapeDtypeStruct((1, LANES), F32),
                   jax.ShapeDtypeStruct((1, D_MODEL), F32), jax.ShapeDtypeStruct((1, D_MODEL), F32)],
        scratch_shapes=[pltpu.VMEM((ns, tm, fs), F32), pltpu.VMEM((ns, tm, fs), F32)],
        compiler_params=_params("arbitrary"), name="ffn_fwd_bwd",
    )(h1, x1, target, w_gate, w_up, w_down, g_ffn, g_final)


def _ffn_bwd_weights_call(h1, dgate, dup, act, dx2):
    t = h1.shape[0]
    tm = _token_tile(t, 2048)
    ns, _, fs = dgate.shape

    def body(h_ref, dgate_ref, dup_ref, act_ref, dx2_ref, gg_ref, gu_ref, gd_ref):
        @pl.when(pl.program_id(1) == 0)
        def _():
            gg_ref[...] = jnp.zeros_like(gg_ref)
            gu_ref[...] = jnp.zeros_like(gu_ref)
            gd_ref[...] = jnp.zeros_like(gd_ref)

        h = h_ref[...]
        gg_ref[...] += _dot_tn(dgate_ref[...], h)
        gu_ref[...] += _dot_tn(dup_ref[...], h)
        gd_ref[...] += _dot_tn(act_ref[...], dx2_ref[...])

    row = lambda w: pl.BlockSpec((tm, w), lambda s, i: (i, 0))
    shard_act = pl.BlockSpec((None, tm, fs), lambda s, i: (s, i, 0))
    return pl.pallas_call(
        body, grid=(ns, t // tm),
        in_specs=[row(D_MODEL), shard_act, shard_act, shard_act, row(D_MODEL)],
        out_specs=[pl.BlockSpec((None, fs, D_MODEL), lambda s, i: (s, 0, 0))] * 3,
        out_shape=[jax.ShapeDtypeStruct((ns, fs, D_MODEL), F32)] * 3,
        compiler_params=_params("parallel", "arbitrary"), name="ffn_bwd_weights",
    )(h1, dgate, dup, act, dx2)


def _outproj_bwd_call(dx1, attn, u, w_pool, b_pool, pool_scale, w_out, exchange=None):
    t = dx1.shape[0]
    tm = _token_tile(t, 1024)

    def body(dx1_ref, a_ref, u_ref, halo_ref, wp_ref, bp_ref, ps_ref, wo_ref,
             dattn_ref, dmixed_ref, gwo_ref, gbo_ref, gwp_ref, gbp_ref, gps_ref, sc_ref, y_ref):
        i = pl.program_id(0)

        @pl.when(i == 0)
        def _():
            for ref in (gwo_ref, gbo_ref, gwp_ref, gbp_ref, gps_ref):
                ref[...] = jnp.zeros_like(ref)

        dx1 = dx1_ref[...]
        dx1b = dx1.astype(BF16)
        dcat = _dot_nt(dx1b, wo_ref[...])
        dattn_ref[...] = dcat[:, :ATTN_WIDTH].astype(BF16)
        for g, mixed in enumerate(_pool_mixed(sc_ref, u_ref, halo_ref, i, tm)):
            lanes = slice(LANES * g, LANES * (g + 1))
            mixed_b = mixed.astype(BF16)
            lin = _dot(mixed_b, wp_ref[g]) + bp_ref[:, lanes]
            y_ref[:, lanes] = (lin * ps_ref[:, lanes]).astype(BF16)
            dpool = dcat[:, ATTN_WIDTH + LANES * g:ATTN_WIDTH + LANES * (g + 1)]
            gps_ref[:, lanes] += jnp.sum(dpool * lin, axis=0, keepdims=True)
            dy = dpool * ps_ref[:, lanes]
            gbp_ref[:, lanes] += jnp.sum(dy, axis=0, keepdims=True)
            dy_b = dy.astype(BF16)
            gwp_ref[g] += _dot_tn(mixed_b, dy_b)
            dmixed_ref[:, lanes] = _dot_nt(dy_b, wp_ref[g])
        gwo_ref[0:ATTN_WIDTH, :] += _dot_tn(a_ref[...], dx1b)
        gwo_ref[ATTN_WIDTH:, :] += _dot_tn(y_ref[...], dx1b)
        gbo_ref[...] += jnp.sum(dx1, axis=0, keepdims=True)

    row = lambda w: pl.BlockSpec((tm, w), lambda i: (i, 0))
    const = lambda shape: pl.BlockSpec(shape, lambda i: (0,) * len(shape))
    return _compute_call(
        body, (dx1, attn, u, u, w_pool, b_pool, pool_scale, w_out), grid=(t // tm,),
        in_specs=[row(D_MODEL), row(ATTN_WIDTH), row(POOL_WIDTH), _halo_before(tm, POOL_WIDTH),
                  _resident((4, LANES, LANES)), _resident((1, POOL_WIDTH)), _resident((1, POOL_WIDTH)),
                  _resident((D_MODEL, D_MODEL))],
        out_specs=[row(ATTN_WIDTH), row(POOL_WIDTH), const((D_MODEL, D_MODEL)), const((1, D_MODEL)),
                   const((4, LANES, LANES)), const((1, POOL_WIDTH)), const((1, POOL_WIDTH))],
        out_shape=[jax.ShapeDtypeStruct((t, ATTN_WIDTH), BF16), jax.ShapeDtypeStruct((t, POOL_WIDTH), F32),
                   jax.ShapeDtypeStruct((D_MODEL, D_MODEL), F32), jax.ShapeDtypeStruct((1, D_MODEL), F32),
                   jax.ShapeDtypeStruct((4, LANES, LANES), F32), jax.ShapeDtypeStruct((1, POOL_WIDTH), F32),
                   jax.ShapeDtypeStruct((1, POOL_WIDTH), F32)],
        scratch_shapes=[pltpu.VMEM((tm + POOL_HALO, POOL_WIDTH), F32), pltpu.VMEM((tm, POOL_WIDTH), BF16)],
        semantics=("arbitrary",), name="outproj_bwd", exchange=exchange)


def _attn_bwd_call(q, k, v, dout, probs, p_sinks, cos_t, sin_t, exchange=None):
    t = q.shape[0]
    nb = t // BLOCK
    gb = _blocks_per_step(nb)
    qb = gb * BLOCK
    steps = nb // gb

    def body(q_ref, kc_ref, kp_ref, vc_ref, vp_ref, do_ref, p_ref, psink_ref, cq_ref, sq_ref, ck_ref, sk_ref,
             dq_ref, dk_ref, dv_ref, dsink_ref, newk_ref, newv_ref, carryk_ref, carryv_ref, headk_ref, headv_ref):
        n = pl.program_id(0)
        first = _first_half_mask((BLOCK, LANES))

        @pl.when(n == 0)
        def _():
            dsink_ref[...] = jnp.zeros_like(dsink_ref)

        @pl.when(n < steps)
        def _():
            head_row = lax.broadcasted_iota(jnp.int32, dsink_ref.shape, 0)
            lane2 = lax.broadcasted_iota(jnp.int32, (2 * BLOCK, LANES), 1)
            row_low = lax.broadcasted_iota(jnp.int32, (LANES, BLOCK), 0) < HEAD_DIM
            dsink = jnp.zeros(dsink_ref.shape, F32)
            for g in range(gb):
                r0 = g * BLOCK
                kband = _kv_band(kc_ref, kp_ref, g)
                vband = _kv_band(vc_ref, vp_ref, g)
                kband_t = kband.T
                cq, sq = cq_ref[r0:r0 + BLOCK, :], sq_ref[r0:r0 + BLOCK, :]
                dk_heads, dv_heads = [], []
                for kh in range(2):
                    v2 = _head_in_both_halves(vband, kh)
                    kt_head = kband_t[HEAD_DIM * kh:HEAD_DIM * (kh + 1), :]
                    k2_t = jnp.concatenate([kt_head, kt_head], axis=0).astype(BF16)
                    qst = _stack_heads(q_ref, kh, r0)
                    dost = _stack_heads(do_ref, kh, r0)
                    p_b = p_ref[2 * g + kh]
                    p_t = p_b.astype(F32)
                    dp_t = _dot_nt(v2, dost)
                    delta = jnp.sum(p_t * dp_t, axis=0, keepdims=True)
                    ds_t = (p_t * (dp_t - delta)).astype(BF16)
                    sink_term = psink_ref[2 * g + kh] * delta
                    for h in range(4):
                        val = -jnp.sum(sink_term[:, BLOCK * h:BLOCK * (h + 1)])
                        dsink = dsink + jnp.where(head_row == 4 * kh + h, val, 0.0)
                    dq_t = _dot(k2_t, ds_t)
                    for jj in range(2):
                        slab = jnp.where(row_low, dq_t[:, BLOCK * 2 * jj:BLOCK * (2 * jj + 1)],
                                         dq_t[:, BLOCK * (2 * jj + 1):BLOCK * (2 * jj + 2)]).T
                        dq_ref[r0:r0 + BLOCK, LANES * (2 * kh + jj):LANES * (2 * kh + jj + 1)] = (
                            (slab * cq + _rope_partner(slab * sq, first)) * Q_SCALE).astype(BF16)
                    dk2 = _dot(ds_t, qst)
                    dv2 = _dot(p_b, dost)
                    dk_heads.append(dk2 + pltpu.roll(dk2, HEAD_DIM, 1))
                    dv_heads.append(dv2 + pltpu.roll(dv2, HEAD_DIM, 1))
                bandk = jnp.where(lane2 < HEAD_DIM, dk_heads[0], dk_heads[1])
                bandv = jnp.where(lane2 < HEAD_DIM, dv_heads[0], dv_heads[1])
                if g == 0:
                    headk_ref[...] = bandk[0:BLOCK]
                    headv_ref[...] = bandv[0:BLOCK]
                else:
                    newk_ref[r0 - BLOCK:r0, :] += bandk[0:BLOCK]
                    newv_ref[r0 - BLOCK:r0, :] += bandv[0:BLOCK]
                newk_ref[r0:r0 + BLOCK, :] = bandk[BLOCK:]
                newv_ref[r0:r0 + BLOCK, :] = bandv[BLOCK:]
            dsink_ref[...] += dsink

        @pl.when(n == steps)
        def _():
            headk_ref[...] = jnp.zeros_like(headk_ref)
            headv_ref[...] = jnp.zeros_like(headv_ref)

        @pl.when(n >= 1)
        def _():
            last = qb - BLOCK
            carryk_ref[last:, :] += headk_ref[...]
            carryv_ref[last:, :] += headv_ref[...]
            for g in range(gb):
                rows = slice(g * BLOCK, (g + 1) * BLOCK)
                dk = carryk_ref[rows, :]
                dk_ref[rows, :] = (dk * ck_ref[rows, :] + _rope_partner(dk * sk_ref[rows, :], first)).astype(BF16)
            dv_ref[...] = carryv_ref[...].astype(BF16)

        @pl.when(n < steps)
        def _():
            carryk_ref[...] = newk_ref[...]
            carryv_ref[...] = newv_ref[...]

    cur = lambda w: pl.BlockSpec((qb, w), lambda n: (jnp.minimum(n, steps - 1), 0))
    late = lambda w: pl.BlockSpec((qb, w), lambda n: (jnp.maximum(n - 1, 0), 0))
    prev = lambda w: pl.BlockSpec((BLOCK, w), lambda n: (jnp.maximum(jnp.minimum(n, steps - 1) * gb - 1, 0), 0))
    return _compute_call(
        body, (q, k, k, v, v, dout, probs, p_sinks, cos_t, sin_t, cos_t, sin_t), grid=(steps + 1,),
        in_specs=[cur(ATTN_WIDTH), cur(KV_WIDTH), prev(KV_WIDTH), cur(KV_WIDTH), prev(KV_WIDTH), cur(ATTN_WIDTH),
                  pl.BlockSpec((2 * gb, 2 * BLOCK, 4 * BLOCK), lambda n: (jnp.minimum(n, steps - 1), 0, 0)),
                  pl.BlockSpec((2 * gb, 1, 4 * BLOCK), lambda n: (jnp.minimum(n, steps - 1), 0, 0)),
                  cur(LANES), cur(LANES), late(LANES), late(LANES)],
        out_specs=[cur(ATTN_WIDTH), late(KV_WIDTH), late(KV_WIDTH), pl.BlockSpec((8, LANES), lambda n: (0, 0))],
        out_shape=[jax.ShapeDtypeStruct((t, ATTN_WIDTH), BF16), jax.ShapeDtypeStruct((t, KV_WIDTH), BF16),
                   jax.ShapeDtypeStruct((t, KV_WIDTH), BF16), jax.ShapeDtypeStruct((8, LANES), F32)],
        scratch_shapes=[pltpu.VMEM((qb, LANES), F32), pltpu.VMEM((qb, LANES), F32),
                        pltpu.VMEM((qb, LANES), F32), pltpu.VMEM((qb, LANES), F32),
                        pltpu.VMEM((BLOCK, LANES), F32), pltpu.VMEM((BLOCK, LANES), F32)],
        semantics=("arbitrary",), name="attn_bwd", exchange=exchange)


def _inproj_bwd_call(dq, dk, dv, dmixed, x, dx1, w_in, g_mix, exchange=None):
    t = x.shape[0]
    tm = _token_tile(t)
    n_tiles = t // tm

    def body(dq_ref, dk_ref, dv_ref, dm_ref, halo_ref, x_hbm, dx1_hbm, w_ref, g_ref,
             dx_ref, gw_ref, gb_ref, gg_ref, sc_ref, x_ring, dx1_ring, ring_sems):
        i = pl.program_id(0)

        def fetch(tile):
            rows = pl.ds(pl.multiple_of(tile * tm, tm), tm)
            slot = tile % RING_SLOTS
            return [pltpu.make_async_copy(hbm.at[rows, :], ring.at[slot], ring_sems.at[k, slot])
                    for k, (hbm, ring) in enumerate(((x_hbm, x_ring), (dx1_hbm, dx1_ring)))]

        @pl.when(i == 0)
        def _():
            for tile in range(min(RING_SLOTS - 1, n_tiles)):
                for cp in fetch(tile):
                    cp.start()

        @pl.when(i + RING_SLOTS - 1 < n_tiles)
        def _():
            for cp in fetch(i + RING_SLOTS - 1):
                cp.start()

        for cp in fetch(i):
            cp.wait()
        x_ref, dx1_ref = x_ring.at[i % RING_SLOTS], dx1_ring.at[i % RING_SLOTS]

        @pl.when(i == 0)
        def _():
            for ref in (gw_ref, gb_ref, gg_ref):
                ref[...] = jnp.zeros_like(ref)

        tok = i * tm + lax.broadcasted_iota(jnp.int32, (tm, 1), 0)
        halo = jnp.where(i == n_tiles - 1, 0.0, halo_ref[...])
        du = []
        for g, size in enumerate(POOL_SIZES):
            lanes = slice(LANES * g, LANES * (g + 1))
            dm = dm_ref[:, lanes]
            sc_ref[0:tm, lanes] = dm / jnp.minimum(tok + 1, size).astype(F32)
            sc_ref[tm:, lanes] = halo[:, lanes] / float(size)
            acc = -dm
            for j in range(size):
                acc = acc + sc_ref[pl.ds(j, tm), lanes]
            du.append(acc)
        dz32 = jnp.concatenate([dq_ref[...].astype(F32), dk_ref[...].astype(F32), dv_ref[...].astype(F32)] + du,
                               axis=1)
        dz = dz32.astype(BF16)
        gb_ref[...] += jnp.sum(dz32, axis=0, keepdims=True)
        xh, r = _rms_stats(x_ref[...])
        g = g_ref[...]
        gw_ref[...] += _dot_tn(dz, (xh * g).astype(BF16))
        dh = _dot(dz, w_ref[...])
        gg_ref[...] += jnp.sum(dh * xh, axis=0, keepdims=True)
        dx_ref[...] = dx1_ref[...] + _rms_bwd(dh, xh, r, g)

    row = lambda w: pl.BlockSpec((tm, w), lambda i: (i, 0))
    const = lambda shape: pl.BlockSpec(shape, lambda i: (0,) * len(shape))
    last_halo = t // POOL_HALO - 1
    halo_after = pl.BlockSpec((POOL_HALO, POOL_WIDTH),
                              lambda i: (jnp.minimum((i + 1) * (tm // POOL_HALO), last_halo), 0))
    return _compute_call(
        body, (dq, dk, dv, dmixed, dmixed, x, dx1, w_in, g_mix), grid=(n_tiles,),
        in_specs=[row(ATTN_WIDTH), row(KV_WIDTH), row(KV_WIDTH), row(POOL_WIDTH), halo_after, HBM_SPEC,
                  HBM_SPEC, _resident((IN_WIDTH, D_MODEL)), _resident((1, D_MODEL))],
        out_specs=[row(D_MODEL), const((IN_WIDTH, D_MODEL)), const((1, IN_WIDTH)), const((1, D_MODEL))],
        out_shape=[jax.ShapeDtypeStruct((t, D_MODEL), F32), jax.ShapeDtypeStruct((IN_WIDTH, D_MODEL), F32),
                   jax.ShapeDtypeStruct((1, IN_WIDTH), F32), jax.ShapeDtypeStruct((1, D_MODEL), F32)],
        scratch_shapes=[pltpu.VMEM((tm + POOL_HALO, POOL_WIDTH), F32), pltpu.VMEM((RING_SLOTS, tm, D_MODEL), F32),
                        pltpu.VMEM((RING_SLOTS, tm, D_MODEL), F32), pltpu.SemaphoreType.DMA((2, RING_SLOTS))],
        semantics=("arbitrary",), name="inproj_bwd", exchange=exchange)


def _reduce_to_owner(grads, place, hosts, wire_dtypes):
    me, c = place
    landed = hosts[0](_sibling_exchange(grads))
    partials = _pair_sum_call(grads, landed, c, wire_dtypes)
    slots = hosts[1](_chip_exchange(partials))
    return _chip_sum_call(partials, slots, me, c)


def _local_step(x, target, small, w_in, w_out, w_ffn, place=None):
    t = x.shape[0]
    on_mesh = place is not None
    cos_t, sin_t = _rope_tables(t)
    w_pool = small["w_pool"].astype(BF16)
    w_ffn = list(w_ffn)

    def gathering(bufs, relay_early=0):
        return _gather_exchange(bufs, relay_early) if on_mesh else None

    (h0,), got = _norm_call(x, small["g_mix"], exchange=gathering([w_in]))
    if on_mesh:
        w_in = got[0].reshape(IN_WIDTH, D_MODEL)
    (q, k, v, u), got = _inproj_call(h0, w_in, small["b_in"], cos_t, sin_t, exchange=gathering([w_out], 3))
    if on_mesh:
        w_out = got[0].reshape(D_MODEL, D_MODEL)
    bias_t, bias_first_t = _band_biases()
    (attn, probs, p_sinks), got = _attn_fwd_call(q, k, v, small["sinks"], bias_t, bias_first_t, exchange=gathering(w_ffn[:2]))
    if on_mesh:
        w_ffn[:2] = got
    (x1, h1), got = _outproj_call(attn, u, x, w_pool, small["b_pool"], small["pool_scale"], w_out,
                                          small["b_out"], small["g_ffn"], exchange=gathering(w_ffn[2:], 4))
    if on_mesh:
        w_ffn[2:] = got
    w_gate, w_up, w_down = w_ffn
    act, dgate, dup, dx1, dx2b, loss, g_g_final, g_g_ffn = _ffn_call(h1, x1, target, w_gate, w_up, w_down,
                                                                      small["g_ffn"], small["g_final"])
    ffn_grads = list(_ffn_bwd_weights_call(h1, dgate, dup, act, dx2b))
    kept = {}

    def outproj_bwd(exchange=None):
        kept["outproj"], results = _outproj_bwd_call(dx1, attn, u, w_pool, small["b_pool"], small["pool_scale"],
                                                     w_out, exchange=exchange)
        return results

    def attn_bwd(exchange=None):
        kept["attn"], results = _attn_bwd_call(q, k, v, kept["outproj"][0], probs, p_sinks, cos_t, sin_t,
                                               exchange=exchange)
        return results

    def inproj_bwd(exchange=None):
        dq, dk, dv, _ = kept["attn"]
        kept["inproj"], results = _inproj_bwd_call(dq, dk, dv, kept["outproj"][1], x, dx1, w_in, small["g_mix"],
                                                   exchange=exchange)
        return results

    if on_mesh:
        ffn_grads = _reduce_to_owner(ffn_grads, place, (outproj_bwd, attn_bwd), [BF16] * 3)
    else:
        outproj_bwd(), attn_bwd()
    inproj_bwd()
    _, _, g_w_out, g_b_out, g_w_pool, g_b_pool, g_pool_scale = kept["outproj"]
    g_sinks = kept["attn"][3]
    grad_x, g_w_in, g_b_in, g_g_mix = kept["inproj"]
    small_grads = dict(g_mix=g_g_mix, b_in=g_b_in, sinks=g_sinks[:, 0], w_pool=g_w_pool,
                       b_pool=g_b_pool, pool_scale=g_pool_scale, b_out=g_b_out, g_ffn=g_g_ffn, g_final=g_g_final)
    big_grads = dict(w_in=g_w_in, w_out=g_w_out, w_gate=ffn_grads[0], w_up=ffn_grads[1], w_down=ffn_grads[2])
    return loss, grad_x, small_grads, big_grads


def _cast_place_call(shards, me):
    na = len(shards)
    split = 2

    def body(me_ref, *refs):
        for a in range(na):
            refs[na + a][...] = refs[a][...].astype(BF16)

    in_specs = [pl.BlockSpec((s.shape[0] // split, s.shape[1]), lambda i, me_ref: (i, 0)) for s in shards]
    out_specs = [pl.BlockSpec((None, s.shape[0] // split, s.shape[1]), lambda i, me_ref: (me_ref[0], i, 0))
                 for s in shards]
    return pl.pallas_call(
        body,
        grid_spec=pltpu.PrefetchScalarGridSpec(num_scalar_prefetch=1, grid=(split,), in_specs=in_specs,
                                               out_specs=out_specs),
        out_shape=[jax.ShapeDtypeStruct((N_SHARDS,) + s.shape, BF16) for s in shards],
        compiler_params=_params("parallel"), name="cast_place_weights",
    )(me.reshape(1), *shards)


def _pair_sum_call(grads, landed, c, wire_dtypes):
    na = len(grads)
    split = 2

    def body(c_ref, *refs):
        ins, lands, outs = refs[:na], refs[na:2 * na], refs[2 * na:]
        for a in range(na):
            outs[a][...] = (ins[a][...] + lands[a][...]).astype(outs[a].dtype)

    in_specs, land_specs, out_specs, views = [], [], [], []
    for g in grads:
        ns, r, cols = g.shape
        rows = r // 2 // split
        views.append(g.reshape(ns, 2, r // 2, cols))
        in_specs.append(pl.BlockSpec((None, None, rows, cols), lambda s, i, c_ref: (s, c_ref[0], i, 0)))
        land_specs.append(pl.BlockSpec((None, rows, cols), lambda s, i, c_ref: (s, i, 0)))
        out_specs.append(pl.BlockSpec((None, rows, cols), lambda s, i, c_ref: (s, i, 0)))
    return pl.pallas_call(
        body,
        grid_spec=pltpu.PrefetchScalarGridSpec(num_scalar_prefetch=1, grid=(N_SHARDS, split),
                                               in_specs=in_specs + land_specs, out_specs=out_specs),
        out_shape=[jax.ShapeDtypeStruct(l.shape, dt) for l, dt in zip(landed, wire_dtypes)],
        compiler_params=_params("parallel", "parallel"), name="reduce_pair_sum",
    )(c.reshape(1), *views, *landed)


def _chip_sum_call(partials, slots, me, c):
    na = len(slots)
    split = 2

    def body(place_ref, *refs):
        mine, lands, outs = refs[:na], refs[na:2 * na], refs[2 * na:]
        for j in range(N_SHARDS):
            @pl.when(place_ref[0] == j)
            def _():
                for a in range(na):
                    terms = [mine[a][...] if k == j else lands[a][k] for k in range(N_SHARDS)]
                    t0, t1, t2, t3 = (term.astype(F32) for term in terms)
                    outs[a][...] = ((t0 + t1) + t2) + t3

    mine_specs, land_specs, out_specs = [], [], []
    for s in slots:
        ns, h, cols = s.shape
        mine_specs.append(pl.BlockSpec((None, h // split, cols), lambda i, place: (place[0], i, 0)))
        land_specs.append(pl.BlockSpec((ns, h // split, cols), lambda i, place: (0, i, 0)))
        out_specs.append(pl.BlockSpec((h // split, cols), lambda i, place: (place[1] * split + i, 0)))
    return pl.pallas_call(
        body,
        grid_spec=pltpu.PrefetchScalarGridSpec(num_scalar_prefetch=1, grid=(split,),
                                               in_specs=mine_specs + land_specs, out_specs=out_specs),
        out_shape=[jax.ShapeDtypeStruct((2 * s.shape[1], s.shape[2]), F32) for s in slots],
        compiler_params=_params("parallel"), name="reduce_chip_sum",
    )(jnp.stack([me, c]), *partials, *slots)


def _adamw(w, g, m, v):
    m = ADAM_B1 * m + (1.0 - ADAM_B1) * g
    v = ADAM_B2 * v + (1.0 - ADAM_B2) * jnp.square(g)
    m_hat = m / (1.0 - ADAM_B1 ** ADAM_STEP)
    v_hat = v / (1.0 - ADAM_B2 ** ADAM_STEP)
    return -ADAM_LR * (m_hat / (jnp.sqrt(v_hat) + ADAM_EPS) + ADAM_WD * w), m, v


def _adamw_call(ws, gs, ms, vs):
    na = len(ws)
    split = 4

    def body(*refs):
        w_refs, g_refs, m_refs, v_refs = (refs[k * na:(k + 1) * na] for k in range(4))
        d_refs, nm_refs, nv_refs = (refs[(4 + k) * na:(5 + k) * na] for k in range(3))
        for a in range(na):
            d_refs[a][...], nm_refs[a][...], nv_refs[a][...] = _adamw(
                w_refs[a][...], g_refs[a][...], m_refs[a][...], v_refs[a][...])

    specs = [pl.BlockSpec((w.shape[0] // split, w.shape[1]), lambda i: (i, 0)) for w in ws]
    shapes = [jax.ShapeDtypeStruct(w.shape, F32) for w in ws]
    outs = pl.pallas_call(
        body, grid=(split,), in_specs=specs * 4, out_specs=specs * 3, out_shape=shapes * 3,
        compiler_params=_params("parallel"), name="adamw",
    )(*ws, *gs, *ms, *vs)
    return outs[:na], outs[na:2 * na], outs[2 * na:]


def _small_rows():
    first, row = {}, 0
    for name, (r, c) in SMALL_VIEWS.items():
        first[name] = row
        row += r * (-(-c // LANES))
    return first, row, -(-(row + 1) // 64) * 64


def _pack_small(arrays, loss):
    _, used, rows = _small_rows()
    parts = []
    for name, (r, c) in SMALL_VIEWS.items():
        a = arrays[name].reshape(r, c)
        parts.append(jnp.pad(a, ((0, 0), (0, LANES - c))) if c < LANES else a.reshape(r * c // LANES, LANES))
    parts += [loss, jnp.zeros((rows - used - 1, LANES), F32)]
    return jnp.concatenate(parts, axis=0)


def _adamw_small_call(ws, packed_grads, ms, vs):
    names = list(SMALL_VIEWS)
    first, _, _ = _small_rows()
    n = len(names)

    def body(g_ref, *refs):
        w_refs, m_refs, v_refs = refs[:n], refs[n:2 * n], refs[2 * n:3 * n]
        outs = refs[3 * n:]
        for a, name in enumerate(names):
            r, c = SMALL_VIEWS[name]
            r0 = first[name]
            if c == LANES:
                g = g_ref[r0:r0 + r, :]
            elif c < LANES:
                g = g_ref[r0:r0 + 1, 0:c]
            else:
                g = jnp.concatenate([g_ref[r0 + j:r0 + j + 1, :] for j in range(c // LANES)], axis=1)
            delta, m, v = _adamw(w_refs[a][...], g, m_refs[a][...], v_refs[a][...])
            for k, val in enumerate((g, delta, m, v)):
                outs[k * n + a][...] = val

    shapes = [jax.ShapeDtypeStruct(SMALL_VIEWS[name], F32) for name in names]
    outs = pl.pallas_call(
        body, out_shape=shapes * 4, compiler_params=pltpu.CompilerParams(vmem_limit_bytes=VMEM_LIMIT),
        name="adamw_small",
    )(packed_grads, *[ws[k] for k in names], *[ms[k] for k in names], *[vs[k] for k in names])
    return [dict(zip(names, outs[k * n:(k + 1) * n])) for k in range(4)]


def kernel(x, g_mix, w_in, b_in, sinks, w_pool, b_pool, pool_scale, w_out, b_out, g_ffn, w_gate, w_up, w_down, g_final, loss_target, m_g_mix, m_w_in, m_b_in, m_sinks, m_w_pool, m_b_pool, m_pool_scale, m_w_out, m_b_out, m_g_ffn, m_w_gate, m_w_up, m_w_down, m_g_final, v_g_mix, v_w_in, v_b_in, v_sinks, v_w_pool, v_b_pool, v_pool_scale, v_w_out, v_b_out, v_g_ffn, v_w_gate, v_w_up, v_w_down, v_g_final):
    given = dict(locals())
    big_names = ("w_in", "w_out", "w_gate", "w_up", "w_down")
    out_shapes = {n: given[n].shape for n in SMALL_NAMES + big_names}
    c = lax.axis_index("c")

    me = 2 * lax.axis_index("x") + lax.axis_index("y")
    transposed = ("w_in", "w_gate", "w_up")

    def shard_view(a, name):
        return a[0].T if name in transposed else a[0]

    placed = _cast_place_call([shard_view(given[n], n) for n in big_names], me)

    small = dict(g_mix=g_mix, b_in=b_in, sinks=sinks, w_pool=w_pool[0], b_pool=b_pool.reshape(1, POOL_WIDTH),
                 pool_scale=pool_scale.reshape(1, POOL_WIDTH), b_out=b_out, g_ffn=g_ffn,
                 g_final=g_final.reshape(1, D_MODEL))
    loss, grad_x, small_grads, big_grads = _local_step(
        x[0], loss_target[0], small, placed[0], placed[1], placed[2:], place=(me, c))

    packed = _pack_small(small_grads, loss)
    grads = [
        big_grads["w_in"].reshape(N_SHARDS, IN_WIDTH // N_SHARDS, D_MODEL),
        big_grads["w_out"].reshape(N_SHARDS, D_MODEL // N_SHARDS, D_MODEL),
        jnp.broadcast_to(packed[None], (N_SHARDS,) + packed.shape),
    ]
    alone = [functools.partial(_exchange_call, name=name) for name in
             ("reduce_sibling_exchange", "reduce_chip_exchange")]
    h_in, h_out, h_small = _reduce_to_owner(grads, (me, c), alone, [BF16, BF16, F32])
    totals = _exchange_call(_sibling_allgather(
        [h_in, h_out, big_grads["w_gate"], big_grads["w_up"], big_grads["w_down"], h_small]), "reduce_sibling_allgather")
    loss = totals[-1][_small_rows()[1], 0]

    def owned(prefix):
        return [shard_view(given[prefix + n], n) for n in big_names]

    def owned_small(prefix):
        return {n: given[prefix + n].reshape(SMALL_VIEWS[n]) for n in SMALL_NAMES}

    big = (totals[:-1],) + _adamw_call(owned(""), totals[:-1], owned("m_"), owned("v_"))
    small_groups = _adamw_small_call(owned_small(""), totals[-1], owned_small("m_"), owned_small("v_"))

    order = ("g_mix", "w_in", "b_in", "sinks", "w_pool", "b_pool", "pool_scale", "w_out", "b_out", "g_ffn",
             "w_gate", "w_up", "w_down", "g_final")
    results = [loss, grad_x.reshape(x.shape)]
    for group, small_group in zip(big, small_groups):
        named = {n: (a.T if n in transposed else a) for n, a in zip(big_names, group)}
        named.update(small_group)
        results += [named[n].reshape(out_shapes[n]) for n in order]
    return tuple(results)
```

```python
import functools
from typing import Any, Callable, Mapping, NamedTuple, Sequence

import jax
import jax.numpy as jnp
from jax import lax
from jax.experimental import pallas as pl
from jax.experimental.pallas import tpu as pltpu

F32 = jnp.float32
BF16 = jnp.bfloat16

D_MODEL = 1024
ATTN_WIDTH = 512
KV_WIDTH = 128
POOL_WIDTH = 512
IN_WIDTH = ATTN_WIDTH + 2 * KV_WIDTH + POOL_WIDTH
HEAD_DIM = 64
N_Q_HEADS = 8
BLOCK = 128
POOL_SIZES = (2, 4, 8, 16)
POOL_HALO = 16
ROPE_THETA = 10000.0
RMS_EPS = 1e-5
Q_SCALE = HEAD_DIM ** -0.5

ADAM_LR = 0.001
ADAM_B1 = 0.9
ADAM_B2 = 0.999
ADAM_EPS = 1e-08
ADAM_WD = 0.01
ADAM_STEP = 10

N_SHARDS = 4
LANES = 128
VMEM_LIMIT = 56 * 1024 * 1024
FFN_TOKEN_TILE = 256

MESH = pl.DeviceIdType.MESH
HBM_SPEC = pl.BlockSpec(memory_space=pltpu.HBM)

SMALL_VIEWS = dict(w_pool=(4 * LANES, LANES), b_pool=(4, LANES), pool_scale=(4, LANES), g_mix=(1, D_MODEL),
                   b_in=(1, IN_WIDTH), b_out=(1, D_MODEL), g_ffn=(1, D_MODEL), g_final=(1, D_MODEL),
                   sinks=(1, N_Q_HEADS))
SMALL_NAMES = tuple(SMALL_VIEWS)


def _dot(a, b):
    return jnp.dot(a, b, preferred_element_type=F32)


def _dot_nt(a, b):
    return lax.dot_general(a, b, (((1,), (1,)), ((), ())), preferred_element_type=F32)


def _dot_tn(a, b):
    return lax.dot_general(a, b, (((0,), (0,)), ((), ())), preferred_element_type=F32)


def _params(*semantics):
    return pltpu.CompilerParams(dimension_semantics=semantics, vmem_limit_bytes=VMEM_LIMIT)


def _resident(shape):
    return pl.BlockSpec(shape, lambda *_: (0,) * len(shape), pipeline_mode=pl.Buffered(1))


def _token_tile(t, largest=512):
    for tm in (2048, 1024, 512, 256, 128):
        if tm <= largest and t % tm == 0:
            return tm
    raise ValueError(f"sequence length {t} is not a multiple of 128")


def _rms_stats(xf):
    r = lax.rsqrt(jnp.mean(xf * xf, axis=-1, keepdims=True) + RMS_EPS)
    return xf * r, r


def _rms_bwd(dy, xh, r, g):
    dxh = dy * g
    return r * (dxh - xh * jnp.mean(dxh * xh, axis=-1, keepdims=True))


def _rope_partner(t, first):
    return jnp.where(first, pltpu.roll(t, LANES - HEAD_DIM // 2, 1), pltpu.roll(t, HEAD_DIM // 2, 1))


def _first_half_mask(shape):
    lane = lax.broadcasted_iota(jnp.int32, shape, 1)
    return (lane % HEAD_DIM) < (HEAD_DIM // 2)


def _rope_tables(t):
    inv_freq = 1.0 / (ROPE_THETA ** (jnp.arange(0, HEAD_DIM, 2, dtype=F32) / HEAD_DIM))
    ang = jnp.arange(t, dtype=F32)[:, None] * inv_freq[None, :]
    cos, sin = jnp.cos(ang), jnp.sin(ang)
    cos_t = jnp.tile(jnp.concatenate([cos, cos], axis=-1), (1, LANES // HEAD_DIM))
    sin_t = jnp.tile(jnp.concatenate([-sin, sin], axis=-1), (1, LANES // HEAD_DIM))
    return cos_t, sin_t


class _Exchange(NamedTuple):
    inputs: Sequence[Any]
    out_shapes: Sequence[Any]
    aliases: Mapping[int, int]
    sems: Sequence[Any]
    start: Callable[..., None]
    finish: Callable[..., None]
    relay: Callable[..., None] = None
    relay_early: int = 0


def _compute_call(body, operands, *, grid, in_specs, out_specs, out_shape, scratch_shapes=(), semantics, name,
                  exchange=None):
    n_in, n_out, n_scr = len(in_specs), len(out_specs), len(scratch_shapes)
    if exchange is None:
        return pl.pallas_call(body, grid=grid, in_specs=in_specs, out_specs=out_specs, out_shape=out_shape,
                              scratch_shapes=scratch_shapes, compiler_params=_params(*semantics),
                              name=name)(*operands), []
    k_in, k_out = len(exchange.inputs), len(exchange.out_shapes)

    def hosting(*refs):
        ins, rest = refs[:n_in], refs[n_in:]
        ex_ins, rest = rest[:k_in], rest[k_in:]
        outs, rest = rest[:n_out], rest[n_out:]
        ex_outs, rest = rest[:k_out], rest[k_out:]
        scratch, sems = rest[:n_scr], rest[n_scr:]
        steps = [pl.program_id(d) for d in range(len(grid))]
        is_first = functools.reduce(jnp.logical_and, [s == 0 for s in steps])
        is_last = functools.reduce(jnp.logical_and, [s == g - 1 for s, g in zip(steps, grid)])
        is_relay = functools.reduce(jnp.logical_and, [s == max(g - 1 - exchange.relay_early, 0)
                                                      for s, g in zip(steps, grid)])

        @pl.when(is_first)
        def _():
            exchange.start(ex_ins, ex_outs, sems)

        body(*ins, *outs, *scratch)

        if exchange.relay is not None:
            @pl.when(is_relay)
            def _():
                exchange.relay(ex_ins, ex_outs, sems)

        @pl.when(is_last)
        def _():
            exchange.finish(ex_ins, ex_outs, sems)

    results = pl.pallas_call(
        hosting, grid=grid,
        in_specs=list(in_specs) + [HBM_SPEC] * k_in, out_specs=list(out_specs) + [HBM_SPEC] * k_out,
        out_shape=list(out_shape) + list(exchange.out_shapes),
        scratch_shapes=list(scratch_shapes) + list(exchange.sems),
        input_output_aliases={n_in + i: n_out + j for i, j in exchange.aliases.items()},
        compiler_params=_params(*semantics), name=name,
    )(*operands, *exchange.inputs)
    return results[:n_out], results[n_out:]


def _both(first, second):
    n_in, n_out, n_sem = len(first.inputs), len(first.out_shapes), len(first.sems)
    aliases = dict(first.aliases)
    aliases.update({n_in + i: n_out + j for i, j in second.aliases.items()})

    def start(ins, outs, sems):
        first.start(ins[:n_in], outs[:n_out], sems[:n_sem])
        second.start(ins[n_in:], outs[n_out:], sems[n_sem:])

    def finish(ins, outs, sems):
        first.finish(ins[:n_in], outs[:n_out], sems[:n_sem])
        second.finish(ins[n_in:], outs[n_out:], sems[n_sem:])

    assert first.relay is None and second.relay is None
    return _Exchange(inputs=list(first.inputs) + list(second.inputs),
                     out_shapes=list(first.out_shapes) + list(second.out_shapes), aliases=aliases,
                     sems=list(first.sems) + list(second.sems), start=start, finish=finish)


def _exchange_call(exchange, name):
    n_in, n_out = len(exchange.inputs), len(exchange.out_shapes)

    def body(*refs):
        ins, outs, sems = refs[:n_in], refs[n_in:n_in + n_out], refs[n_in + n_out:]
        exchange.start(ins, outs, sems)
        if exchange.relay is not None:
            exchange.relay(ins, outs, sems)
        exchange.finish(ins, outs, sems)

    return pl.pallas_call(
        body, in_specs=[HBM_SPEC] * n_in, out_specs=[HBM_SPEC] * n_out, out_shape=list(exchange.out_shapes),
        input_output_aliases=dict(exchange.aliases), scratch_shapes=list(exchange.sems),
        compiler_params=pltpu.CompilerParams(has_side_effects=True), name=name,
    )(*exchange.inputs)


def _mesh_place():
    x, y, c = lax.axis_index("x"), lax.axis_index("y"), lax.axis_index("c")
    other_chips = [(1 - x, y), (x, 1 - y), (1 - x, 1 - y)]
    return x, y, c, other_chips


def _half(ref, c, rows):
    return ref.at[pl.ds(pl.multiple_of(c * rows, 16), rows), :]


def _remote(src, dst, send_sems, recv_sems, idx, to):
    return functools.partial(pltpu.make_async_remote_copy, src_ref=src, dst_ref=dst, send_sem=send_sems.at[idx],
                             recv_sem=recv_sems.at[idx], device_id=to, device_id_type=MESH)


def _gather_exchange(bufs, relay_early=0):
    na = len(bufs)

    def copies(outs, sems):
        send_sems, recv_sems = sems
        x, y, c, other_chips = _mesh_place()
        me, sibling = 2 * x + y, (x, y, 1 - c)
        ici_out, ici_in, fwd_out, fwd_in = [], [], [], []
        for j, (cx, cy) in enumerate(other_chips):
            for a in range(na):
                rows = outs[a].shape[1] // 2
                mine = _half(outs[a].at[me], c, rows)
                landed = _half(outs[a].at[2 * cx + cy], c, rows)
                passed = _half(outs[a].at[2 * cx + cy], 1 - c, rows)
                ici_out.append(_remote(mine, mine, send_sems, recv_sems, j * na + a, (cx, cy, c)))
                ici_in.append(_remote(landed, landed, send_sems, recv_sems, j * na + a, (cx, cy, c)))
                fwd_out.append(_remote(landed, landed, send_sems, recv_sems, (3 + j) * na + a, sibling))
                fwd_in.append(_remote(passed, passed, send_sems, recv_sems, (3 + j) * na + a, sibling))
        return ici_out, ici_in, fwd_out, fwd_in

    def start(ins, outs, sems):
        for cp in copies(outs, sems)[0]:
            cp().start()

    def relay(ins, outs, sems):
        _, ici_in, fwd_out, _ = copies(outs, sems)
        for arrived, forward in zip(ici_in, fwd_out):
            arrived().wait_recv()
            forward().start()

    def finish(ins, outs, sems):
        ici_out, _, fwd_out, fwd_in = copies(outs, sems)
        for cp in fwd_in:
            cp().wait_recv()
        for cp in ici_out + fwd_out:
            cp().wait_send()

    return _Exchange(inputs=bufs, out_shapes=[jax.ShapeDtypeStruct(b.shape, b.dtype) for b in bufs],
                     aliases={a: a for a in range(na)},
                     sems=[pltpu.SemaphoreType.DMA((6 * na,)), pltpu.SemaphoreType.DMA((6 * na,))],
                     start=start, finish=finish, relay=relay, relay_early=relay_early)


def _sibling_exchange(grads):
    na = len(grads)

    def copies(ins, outs, sems):
        x, y, c, _ = _mesh_place()
        out = []
        for a in range(na):
            rows = ins[a].shape[1] // 2
            src = ins[a].at[:, pl.ds(pl.multiple_of((1 - c) * rows, 8), rows), :]
            out.append(_remote(src, outs[a], sems[0], sems[1], a, (x, y, 1 - c)))
        return out

    def start(ins, outs, sems):
        for cp in copies(ins, outs, sems):
            cp().start()

    def finish(ins, outs, sems):
        for cp in copies(ins, outs, sems):
            cp().wait_recv()
            cp().wait_send()

    return _Exchange(inputs=grads,
                     out_shapes=[jax.ShapeDtypeStruct((g.shape[0], g.shape[1] // 2, g.shape[2]), g.dtype)
                                 for g in grads],
                     aliases={}, sems=[pltpu.SemaphoreType.DMA((na,)), pltpu.SemaphoreType.DMA((na,))],
                     start=start, finish=finish)


def _chip_exchange(partials):
    na = len(partials)

    def copies(ins, outs, sems):
        x, y, c, other_chips = _mesh_place()
        me = 2 * x + y
        sends, arrivals = [], []
        for j, (cx, cy) in enumerate(other_chips):
            for a in range(na):
                slot = outs[a].at[2 * cx + cy]
                sends.append(_remote(ins[a].at[2 * cx + cy], outs[a].at[me], sems[0], sems[1], j * na + a, (cx, cy, c)))
                arrivals.append(_remote(slot, slot, sems[0], sems[1], j * na + a, (cx, cy, c)))
        return sends, arrivals

    def start(ins, outs, sems):
        for cp in copies(ins, outs, sems)[0]:
            cp().start()

    def finish(ins, outs, sems):
        sends, arrivals = copies(ins, outs, sems)
        for cp in arrivals:
            cp().wait_recv()
        for cp in sends:
            cp().wait_send()

    return _Exchange(inputs=partials, out_shapes=[jax.ShapeDtypeStruct(p.shape, p.dtype) for p in partials],
                     aliases={}, sems=[pltpu.SemaphoreType.DMA((3 * na,)), pltpu.SemaphoreType.DMA((3 * na,))],
                     start=start, finish=finish)


def _sibling_allgather(bufs):
    na = len(bufs)

    def piece(outs, sems, a, half):
        x, y, c, _ = _mesh_place()
        rows = outs[a].shape[0] // 2
        rows_ref = outs[a].at[pl.ds(pl.multiple_of(half * rows, 8), rows), :]
        return _remote(rows_ref, rows_ref, sems[0], sems[1], a, (x, y, 1 - c))

    def start(ins, outs, sems):
        c = lax.axis_index("c")
        for a in range(na):
            piece(outs, sems, a, c)().start()

    def finish(ins, outs, sems):
        c = lax.axis_index("c")
        for a in range(na):
            piece(outs, sems, a, 1 - c)().wait_recv()
        for a in range(na):
            piece(outs, sems, a, c)().wait_send()

    return _Exchange(inputs=bufs, out_shapes=[jax.ShapeDtypeStruct(b.shape, b.dtype) for b in bufs],
                     aliases={a: a for a in range(na)},
                     sems=[pltpu.SemaphoreType.DMA((na,)), pltpu.SemaphoreType.DMA((na,))],
                     start=start, finish=finish)


def _norm_call(x, g_mix, exchange=None):
    t = x.shape[0]
    tm = _token_tile(t)

    def body(x_ref, g_ref, h_ref):
        xh, _ = _rms_stats(x_ref[...])
        h_ref[...] = (xh * g_ref[...]).astype(BF16)

    row = pl.BlockSpec((tm, D_MODEL), lambda i: (i, 0))
    return _compute_call(body, (x, g_mix), grid=(t // tm,), in_specs=[row, _resident((1, D_MODEL))],
                         out_specs=[row], out_shape=[jax.ShapeDtypeStruct((t, D_MODEL), BF16)],
                         semantics=("parallel",), name="norm_fwd", exchange=exchange)


def _inproj_call(h0, w_in, b_in, cos_t, sin_t, exchange=None):
    t = h0.shape[0]
    tm = _token_tile(t, 1024)

    def body(h_ref, w_ref, b_ref, c_ref, s_ref, q_ref, k_ref, v_ref, u_ref):
        z = _dot_nt(h_ref[...], w_ref[...]) + b_ref[...]
        cos, sin = c_ref[...], s_ref[...]
        first = _first_half_mask((tm, LANES))
        for j in range(ATTN_WIDTH // LANES):
            zj = z[:, LANES * j:LANES * (j + 1)]
            q_ref[:, LANES * j:LANES * (j + 1)] = ((zj * cos + _rope_partner(zj, first) * sin) * Q_SCALE).astype(BF16)
        zk = z[:, ATTN_WIDTH:ATTN_WIDTH + KV_WIDTH]
        k_ref[...] = (zk * cos + _rope_partner(zk, first) * sin).astype(BF16)
        v_ref[...] = z[:, ATTN_WIDTH + KV_WIDTH:ATTN_WIDTH + 2 * KV_WIDTH].astype(BF16)
        u_ref[...] = z[:, ATTN_WIDTH + 2 * KV_WIDTH:]

    row = lambda w: pl.BlockSpec((tm, w), lambda i: (i, 0))
    return _compute_call(
        body, (h0, w_in, b_in, cos_t, sin_t), grid=(t // tm,),
        in_specs=[row(D_MODEL), _resident((IN_WIDTH, D_MODEL)), _resident((1, IN_WIDTH)), row(LANES), row(LANES)],
        out_specs=[row(ATTN_WIDTH), row(KV_WIDTH), row(KV_WIDTH), row(POOL_WIDTH)],
        out_shape=[jax.ShapeDtypeStruct((t, ATTN_WIDTH), BF16),
                   jax.ShapeDtypeStruct((t, KV_WIDTH), BF16), jax.ShapeDtypeStruct((t, KV_WIDTH), BF16),
                   jax.ShapeDtypeStruct((t, POOL_WIDTH), F32)],
        semantics=("parallel",), name="inproj_fwd", exchange=exchange)


def _head_in_both_halves(band, kv_head):
    lane = lax.broadcasted_iota(jnp.int32, band.shape, 1)
    own = (lane < HEAD_DIM) if kv_head == 0 else (lane >= HEAD_DIM)
    return jnp.where(own, band, pltpu.roll(band, HEAD_DIM, 1)).astype(BF16)


def _stack_heads(ref, kv_head, row0):
    low = lax.broadcasted_iota(jnp.int32, (BLOCK, LANES), 1) < HEAD_DIM
    parts = []
    for jj in range(2):
        slab = ref[row0:row0 + BLOCK, LANES * (2 * kv_head + jj):LANES * (2 * kv_head + jj + 1)]
        zero = jnp.zeros_like(slab)
        parts += [jnp.where(low, slab, zero), jnp.where(low, zero, slab)]
    return jnp.concatenate(parts, axis=0)


def _band_biases():
    c = lax.broadcasted_iota(jnp.int32, (2 * BLOCK, 4 * BLOCK), 0)
    i = lax.broadcasted_iota(jnp.int32, (2 * BLOCK, 4 * BLOCK), 1) % BLOCK
    vis = (c > i) & (c <= i + BLOCK)
    return jnp.where(vis, 0.0, -jnp.inf).astype(F32), jnp.where(vis & (c >= BLOCK), 0.0, -jnp.inf).astype(F32)


def _blocks_per_step(nb):
    for g in (8, 4, 2, 1):
        if nb % g == 0:
            return g


def _sink_row(sink_ref, kv_head):
    return jnp.concatenate([jnp.full((1, BLOCK), sink_ref[0, 4 * kv_head + g], F32) for g in range(4)], axis=1)


def _kv_band(cur_ref, prev_ref, g):
    r0 = g * BLOCK
    prev = prev_ref[...] if g == 0 else cur_ref[r0 - BLOCK:r0, :]
    return jnp.concatenate([prev, cur_ref[r0:r0 + BLOCK, :]], axis=0).astype(F32)


def _attn_fwd_call(q, k, v, sinks, bias, bias_first, exchange=None):
    t = q.shape[0]
    nb = t // BLOCK
    gb = _blocks_per_step(nb)
    qb = gb * BLOCK

    def body(sink_ref, q_ref, kc_ref, kp_ref, vc_ref, vp_ref, b_ref, bf_ref, o_ref, p_ref, psink_ref):
        n = pl.program_id(0)
        row_low = lax.broadcasted_iota(jnp.int32, (LANES, BLOCK), 0) < HEAD_DIM
        for g in range(gb):
            r0 = g * BLOCK
            kband = _kv_band(kc_ref, kp_ref, g)
            vband_t = _kv_band(vc_ref, vp_ref, g).T
            bias_g = jnp.where(n == 0, bf_ref[...], b_ref[...]) if g == 0 else b_ref[...]
            for kh in range(2):
                k2 = _head_in_both_halves(kband, kh)
                vt_head = vband_t[HEAD_DIM * kh:HEAD_DIM * (kh + 1), :]
                v2_t = jnp.concatenate([vt_head, vt_head], axis=0).astype(BF16)
                s_t = _dot_nt(k2, _stack_heads(q_ref, kh, r0)) + bias_g
                sink = _sink_row(sink_ref, kh)
                m = jnp.maximum(jnp.max(s_t, axis=0, keepdims=True), sink)
                e = jnp.exp(s_t - m)
                e_sink = jnp.exp(sink - m)
                inv = 1.0 / (jnp.sum(e, axis=0, keepdims=True) + e_sink)
                p_t = (e * inv).astype(BF16)
                p_ref[2 * g + kh] = p_t
                psink_ref[2 * g + kh] = e_sink * inv
                o_t = _dot(v2_t, p_t)
                for jj in range(2):
                    slab = jnp.where(row_low, o_t[:, BLOCK * 2 * jj:BLOCK * (2 * jj + 1)],
                                     o_t[:, BLOCK * (2 * jj + 1):BLOCK * (2 * jj + 2)]).T
                    o_ref[r0:r0 + BLOCK, LANES * (2 * kh + jj):LANES * (2 * kh + jj + 1)] = slab.astype(BF16)

    cur = lambda w: pl.BlockSpec((qb, w), lambda n: (n, 0))
    prev = lambda w: pl.BlockSpec((BLOCK, w), lambda n: (jnp.maximum(n * gb - 1, 0), 0))
    return _compute_call(
        body, (sinks, q, k, k, v, v, bias, bias_first), grid=(nb // gb,),
        in_specs=[pl.BlockSpec(memory_space=pltpu.SMEM), cur(ATTN_WIDTH), cur(KV_WIDTH), prev(KV_WIDTH),
                  cur(KV_WIDTH), prev(KV_WIDTH), _resident(bias.shape), _resident(bias.shape)],
        out_specs=[cur(ATTN_WIDTH), pl.BlockSpec((2 * gb, 2 * BLOCK, 4 * BLOCK), lambda n: (n, 0, 0)),
                   pl.BlockSpec((2 * gb, 1, 4 * BLOCK), lambda n: (n, 0, 0))],
        out_shape=[jax.ShapeDtypeStruct((t, ATTN_WIDTH), BF16),
                   jax.ShapeDtypeStruct((2 * nb, 2 * BLOCK, 4 * BLOCK), BF16),
                   jax.ShapeDtypeStruct((2 * nb, 1, 4 * BLOCK), F32)],
        semantics=("parallel",), name="attn_fwd", exchange=exchange)


def _pool_mixed(sc_ref, u_ref, halo_ref, i, tm):
    sc_ref[0:POOL_HALO, :] = jnp.where(i == 0, 0.0, halo_ref[...])
    sc_ref[POOL_HALO:, :] = u_ref[...]
    tok = i * tm + lax.broadcasted_iota(jnp.int32, (tm, 1), 0)
    mixed = []
    for g, size in enumerate(POOL_SIZES):
        lanes = slice(LANES * g, LANES * (g + 1))
        cur = sc_ref[pl.ds(POOL_HALO, tm), lanes]
        acc = cur
        for j in range(1, size):
            acc = acc + sc_ref[pl.ds(POOL_HALO - j, tm), lanes]
        count = jnp.minimum(tok + 1, size).astype(F32)
        mixed.append(acc / count - cur)
    return mixed


def _halo_before(tm, width):
    return pl.BlockSpec((POOL_HALO, width), lambda i: (jnp.maximum(i * (tm // POOL_HALO) - 1, 0), 0))


def _outproj_call(attn, u, x, w_pool, b_pool, pool_scale, w_out, b_out, g_ffn, exchange=None):
    t = x.shape[0]
    tm = _token_tile(t)

    def body(a_ref, u_ref, halo_ref, x_ref, wp_ref, bp_ref, ps_ref, wo_ref, bo_ref, g_ref,
             x1_ref, h1_ref, sc_ref, y_ref):
        i = pl.program_id(0)
        for g, mixed in enumerate(_pool_mixed(sc_ref, u_ref, halo_ref, i, tm)):
            lanes = slice(LANES * g, LANES * (g + 1))
            lin = _dot(mixed.astype(BF16), wp_ref[g]) + bp_ref[:, lanes]
            y_ref[:, lanes] = (lin * ps_ref[:, lanes]).astype(BF16)
        x1 = (x_ref[...] + _dot(a_ref[...], wo_ref[0:ATTN_WIDTH, :]) + _dot(y_ref[...], wo_ref[ATTN_WIDTH:, :])
              + bo_ref[...])
        x1_ref[...] = x1
        xh, _ = _rms_stats(x1)
        h1_ref[...] = (xh * g_ref[...]).astype(BF16)

    row = lambda w: pl.BlockSpec((tm, w), lambda i: (i, 0))
    return _compute_call(
        body, (attn, u, u, x, w_pool, b_pool, pool_scale, w_out, b_out, g_ffn), grid=(t // tm,),
        in_specs=[row(ATTN_WIDTH), row(POOL_WIDTH), _halo_before(tm, POOL_WIDTH), row(D_MODEL),
                  _resident((4, LANES, LANES)), _resident((1, POOL_WIDTH)), _resident((1, POOL_WIDTH)),
                  _resident((D_MODEL, D_MODEL)), _resident((1, D_MODEL)), _resident((1, D_MODEL))],
        out_specs=[row(D_MODEL), row(D_MODEL)],
        out_shape=[jax.ShapeDtypeStruct((t, D_MODEL), F32), jax.ShapeDtypeStruct((t, D_MODEL), BF16)],
        scratch_shapes=[pltpu.VMEM((tm + POOL_HALO, POOL_WIDTH), F32), pltpu.VMEM((tm, POOL_WIDTH), BF16)],
        semantics=("parallel",), name="outproj_fwd", exchange=exchange)


def _ffn_call(h1, x1, target, w_gate, w_up, w_down, g_ffn, g_final):
    t = x1.shape[0]
    tm = _token_tile(t, FFN_TOKEN_TILE)
    ns, fs, _ = w_gate.shape

    def body(h_ref, x1_ref, tgt_ref, wg_ref, wu_ref, wd_ref, g_ref, gf_ref,
             act_ref, dgate_ref, dup_ref, dx1_ref, dx2b_ref, loss_ref, dgf_ref, dg_ref, gate_ref, up_ref):
        @pl.when(pl.program_id(0) == 0)
        def _():
            loss_ref[...] = jnp.zeros_like(loss_ref)
            dgf_ref[...] = jnp.zeros_like(dgf_ref)
            dg_ref[...] = jnp.zeros_like(dg_ref)

        h = h_ref[...]
        x1 = x1_ref[...]
        x2 = x1
        for s in range(ns):
            gate = _dot_nt(h, wg_ref[s])
            up = _dot_nt(h, wu_ref[s])
            act = (gate * jax.nn.sigmoid(gate) * up).astype(BF16)
            gate_ref[s] = gate
            up_ref[s] = up
            act_ref[s] = act
            x2 = x2 + _dot(act, wd_ref[s])
        xh2, r2 = _rms_stats(x2)
        gf = gf_ref[...]
        err = xh2 * gf - tgt_ref[...]
        loss_ref[...] += 0.5 * jnp.sum(err * err) / D_MODEL
        dy = err / D_MODEL
        dgf_ref[...] += jnp.sum(dy * xh2, axis=0, keepdims=True)
        dx2 = _rms_bwd(dy, xh2, r2, gf)
        dx2b = dx2.astype(BF16)
        dx2b_ref[...] = dx2b
        dh = None
        for s in range(ns):
            dact = _dot_nt(dx2b, wd_ref[s])
            gate = gate_ref[s]
            sg = jax.nn.sigmoid(gate)
            dup = (dact * (gate * sg)).astype(BF16)
            dgate = (dact * up_ref[s] * (sg * (1.0 + gate * (1.0 - sg)))).astype(BF16)
            dgate_ref[s] = dgate
            dup_ref[s] = dup
            part = _dot(dgate, wg_ref[s]) + _dot(dup, wu_ref[s])
            dh = part if dh is None else dh + part
        xh1, r1 = _rms_stats(x1)
        dg_ref[...] += jnp.sum(dh * xh1, axis=0, keepdims=True)
        dx1_ref[...] = dx2 + _rms_bwd(dh, xh1, r1, g_ref[...])

    row = lambda w: pl.BlockSpec((tm, w), lambda i: (i, 0))
    shard_act = pl.BlockSpec((ns, tm, fs), lambda i: (0, i, 0))
    act_shape = jax.ShapeDtypeStruct((ns, t, fs), BF16)
    vec = pl.BlockSpec((1, D_MODEL), lambda i: (0, 0))
    return pl.pallas_call(
        body, grid=(t // tm,),
        in_specs=[row(D_MODEL), row(D_MODEL), row(D_MODEL), _resident((ns, fs, D_MODEL)),
                  _resident((ns, fs, D_MODEL)), _resident((ns, fs, D_MODEL)), _resident((1, D_MODEL)),
                  _resident((1, D_MODEL))],
        out_specs=[shard_act, shard_act, shard_act, row(D_MODEL), row(D_MODEL),
                   pl.BlockSpec((1, LANES), lambda i: (0, 0)), vec, vec],
        out_shape=[act_shape, act_shape, act_shape, jax.ShapeDtypeStruct((t, D_MODEL), F32),
                   jax.ShapeDtypeStruct((t, D_MODEL), BF16), jax.ShapeDtypeStruct((1, LANES), F32),
                   jax.ShapeDtypeStruct((1, D_MODEL), F32), jax.ShapeDtypeStruct((1, D_MODEL), F32)],
        scratch_shapes=[pltpu.VMEM((ns, tm, fs), F32), pltpu.VMEM((ns, tm, fs), F32)],
        compiler_params=_params("arbitrary"), name="ffn_fwd_bwd",
    )(h1, x1, target, w_gate, w_up, w_down, g_ffn, g_final)


def _ffn_bwd_weights_call(h1, dgate, dup, act, dx2):
    t = h1.shape[0]
    tm = _token_tile(t, 2048)
    ns, _, fs = dgate.shape

    def body(h_ref, dgate_ref, dup_ref, act_ref, dx2_ref, gg_ref, gu_ref, gd_ref):
        @pl.when(pl.program_id(1) == 0)
        def _():
            gg_ref[...] = jnp.zeros_like(gg_ref)
            gu_ref[...] = jnp.zeros_like(gu_ref)
            gd_ref[...] = jnp.zeros_like(gd_ref)

        h = h_ref[...]
        gg_ref[...] += _dot_tn(dgate_ref[...], h)
        gu_ref[...] += _dot_tn(dup_ref[...], h)
        gd_ref[...] += _dot_tn(act_ref[...], dx2_ref[...])

    row = lambda w: pl.BlockSpec((tm, w), lambda s, i: (i, 0))
    shard_act = pl.BlockSpec((None, tm, fs), lambda s, i: (s, i, 0))
    return pl.pallas_call(
        body, grid=(ns, t // tm),
        in_specs=[row(D_MODEL), shard_act, shard_act, shard_act, row(D_MODEL)],
        out_specs=[pl.BlockSpec((None, fs, D_MODEL), lambda s, i: (s, 0, 0))] * 3,
        out_shape=[jax.ShapeDtypeStruct((ns, fs, D_MODEL), F32)] * 3,
        compiler_params=_params("parallel", "arbitrary"), name="ffn_bwd_weights",
    )(h1, dgate, dup, act, dx2)


def _outproj_bwd_call(dx1, attn, u, w_pool, b_pool, pool_scale, w_out, exchange=None):
    t = dx1.shape[0]
    tm = _token_tile(t, 1024)

    def body(dx1_ref, a_ref, u_ref, halo_ref, wp_ref, bp_ref, ps_ref, wo_ref,
             dattn_ref, dmixed_ref, gwo_ref, gbo_ref, gwp_ref, gbp_ref, gps_ref, sc_ref, y_ref):
        i = pl.program_id(0)

        @pl.when(i == 0)
        def _():
            for ref in (gwo_ref, gbo_ref, gwp_ref, gbp_ref, gps_ref):
                ref[...] = jnp.zeros_like(ref)

        dx1 = dx1_ref[...]
        dx1b = dx1.astype(BF16)
        dcat = _dot_nt(dx1b, wo_ref[...])
        dattn_ref[...] = dcat[:, :ATTN_WIDTH].astype(BF16)
        for g, mixed in enumerate(_pool_mixed(sc_ref, u_ref, halo_ref, i, tm)):
            lanes = slice(LANES * g, LANES * (g + 1))
            mixed_b = mixed.astype(BF16)
            lin = _dot(mixed_b, wp_ref[g]) + bp_ref[:, lanes]
            y_ref[:, lanes] = (lin * ps_ref[:, lanes]).astype(BF16)
            dpool = dcat[:, ATTN_WIDTH + LANES * g:ATTN_WIDTH + LANES * (g + 1)]
            gps_ref[:, lanes] += jnp.sum(dpool * lin, axis=0, keepdims=True)
            dy = dpool * ps_ref[:, lanes]
            gbp_ref[:, lanes] += jnp.sum(dy, axis=0, keepdims=True)
            dy_b = dy.astype(BF16)
            gwp_ref[g] += _dot_tn(mixed_b, dy_b)
            dmixed_ref[:, lanes] = _dot_nt(dy_b, wp_ref[g])
        gwo_ref[0:ATTN_WIDTH, :] += _dot_tn(a_ref[...], dx1b)
        gwo_ref[ATTN_WIDTH:, :] += _dot_tn(y_ref[...], dx1b)
        gbo_ref[...] += jnp.sum(dx1, axis=0, keepdims=True)

    row = lambda w: pl.BlockSpec((tm, w), lambda i: (i, 0))
    const = lambda shape: pl.BlockSpec(shape, lambda i: (0,) * len(shape))
    return _compute_call(
        body, (dx1, attn, u, u, w_pool, b_pool, pool_scale, w_out), grid=(t // tm,),
        in_specs=[row(D_MODEL), row(ATTN_WIDTH), row(POOL_WIDTH), _halo_before(tm, POOL_WIDTH),
                  _resident((4, LANES, LANES)), _resident((1, POOL_WIDTH)), _resident((1, POOL_WIDTH)),
                  _resident((D_MODEL, D_MODEL))],
        out_specs=[row(ATTN_WIDTH), row(POOL_WIDTH), const((D_MODEL, D_MODEL)), const((1, D_MODEL)),
                   const((4, LANES, LANES)), const((1, POOL_WIDTH)), const((1, POOL_WIDTH))],
        out_shape=[jax.ShapeDtypeStruct((t, ATTN_WIDTH), BF16), jax.ShapeDtypeStruct((t, POOL_WIDTH), F32),
                   jax.ShapeDtypeStruct((D_MODEL, D_MODEL), F32), jax.ShapeDtypeStruct((1, D_MODEL), F32),
                   jax.ShapeDtypeStruct((4, LANES, LANES), F32), jax.ShapeDtypeStruct((1, POOL_WIDTH), F32),
                   jax.ShapeDtypeStruct((1, POOL_WIDTH), F32)],
        scratch_shapes=[pltpu.VMEM((tm + POOL_HALO, POOL_WIDTH), F32), pltpu.VMEM((tm, POOL_WIDTH), BF16)],
        semantics=("arbitrary",), name="outproj_bwd", exchange=exchange)


def _attn_bwd_call(q, k, v, dout, probs, p_sinks, cos_t, sin_t, exchange=None):
    t = q.shape[0]
    nb = t // BLOCK
    gb = _blocks_per_step(nb)
    qb = gb * BLOCK
    steps = nb // gb

    def body(q_ref, kc_ref, kp_ref, vc_ref, vp_ref, do_ref, p_ref, psink_ref, cq_ref, sq_ref, ck_ref, sk_ref,
             dq_ref, dk_ref, dv_ref, dsink_ref, newk_ref, newv_ref, carryk_ref, carryv_ref, headk_ref, headv_ref):
        n = pl.program_id(0)
        first = _first_half_mask((BLOCK, LANES))

        @pl.when(n == 0)
        def _():
            dsink_ref[...] = jnp.zeros_like(dsink_ref)

        @pl.when(n < steps)
        def _():
            head_row = lax.broadcasted_iota(jnp.int32, dsink_ref.shape, 0)
            lane2 = lax.broadcasted_iota(jnp.int32, (2 * BLOCK, LANES), 1)
            row_low = lax.broadcasted_iota(jnp.int32, (LANES, BLOCK), 0) < HEAD_DIM
            dsink = jnp.zeros(dsink_ref.shape, F32)
            for g in range(gb):
                r0 = g * BLOCK
                kband = _kv_band(kc_ref, kp_ref, g)
                vband = _kv_band(vc_ref, vp_ref, g)
                kband_t = kband.T
                cq, sq = cq_ref[r0:r0 + BLOCK, :], sq_ref[r0:r0 + BLOCK, :]
                dk_heads, dv_heads = [], []
                for kh in range(2):
                    v2 = _head_in_both_halves(vband, kh)
                    kt_head = kband_t[HEAD_DIM * kh:HEAD_DIM * (kh + 1), :]
                    k2_t = jnp.concatenate([kt_head, kt_head], axis=0).astype(BF16)
                    qst = _stack_heads(q_ref, kh, r0)
                    dost = _stack_heads(do_ref, kh, r0)
                    p_b = p_ref[2 * g + kh]
                    p_t = p_b.astype(F32)
                    dp_t = _dot_nt(v2, dost)
                    delta = jnp.sum(p_t * dp_t, axis=0, keepdims=True)
                    ds_t = (p_t * (dp_t - delta)).astype(BF16)
                    sink_term = psink_ref[2 * g + kh] * delta
                    for h in range(4):
                        val = -jnp.sum(sink_term[:, BLOCK * h:BLOCK * (h + 1)])
                        dsink = dsink + jnp.where(head_row == 4 * kh + h, val, 0.0)
                    dq_t = _dot(k2_t, ds_t)
                    for jj in range(2):
                        slab = jnp.where(row_low, dq_t[:, BLOCK * 2 * jj:BLOCK * (2 * jj + 1)],
                                         dq_t[:, BLOCK * (2 * jj + 1):BLOCK * (2 * jj + 2)]).T
                        dq_ref[r0:r0 + BLOCK, LANES * (2 * kh + jj):LANES * (2 * kh + jj + 1)] = (
                            (slab * cq + _rope_partner(slab * sq, first)) * Q_SCALE).astype(BF16)
                    dk2 = _dot(ds_t, qst)
                    dv2 = _dot(p_b, dost)
                    dk_heads.append(dk2 + pltpu.roll(dk2, HEAD_DIM, 1))
                    dv_heads.append(dv2 + pltpu.roll(dv2, HEAD_DIM, 1))
                bandk = jnp.where(lane2 < HEAD_DIM, dk_heads[0], dk_heads[1])
                bandv = jnp.where(lane2 < HEAD_DIM, dv_heads[0], dv_heads[1])
                if g == 0:
                    headk_ref[...] = bandk[0:BLOCK]
                    headv_ref[...] = bandv[0:BLOCK]
                else:
                    newk_ref[r0 - BLOCK:r0, :] += bandk[0:BLOCK]
                    newv_ref[r0 - BLOCK:r0, :] += bandv[0:BLOCK]
                newk_ref[r0:r0 + BLOCK, :] = bandk[BLOCK:]
                newv_ref[r0:r0 + BLOCK, :] = bandv[BLOCK:]
            dsink_ref[...] += dsink

        @pl.when(n == steps)
        def _():
            headk_ref[...] = jnp.zeros_like(headk_ref)
            headv_ref[...] = jnp.zeros_like(headv_ref)

        @pl.when(n >= 1)
        def _():
            last = qb - BLOCK
            carryk_ref[last:, :] += headk_ref[...]
            carryv_ref[last:, :] += headv_ref[...]
            for g in range(gb):
                rows = slice(g * BLOCK, (g + 1) * BLOCK)
                dk = carryk_ref[rows, :]
                dk_ref[rows, :] = (dk * ck_ref[rows, :] + _rope_partner(dk * sk_ref[rows, :], first)).astype(BF16)
            dv_ref[...] = carryv_ref[...].astype(BF16)

        @pl.when(n < steps)
        def _():
            carryk_ref[...] = newk_ref[...]
            carryv_ref[...] = newv_ref[...]

    cur = lambda w: pl.BlockSpec((qb, w), lambda n: (jnp.minimum(n, steps - 1), 0))
    late = lambda w: pl.BlockSpec((qb, w), lambda n: (jnp.maximum(n - 1, 0), 0))
    prev = lambda w: pl.BlockSpec((BLOCK, w), lambda n: (jnp.maximum(jnp.minimum(n, steps - 1) * gb - 1, 0), 0))
    return _compute_call(
        body, (q, k, k, v, v, dout, probs, p_sinks, cos_t, sin_t, cos_t, sin_t), grid=(steps + 1,),
        in_specs=[cur(ATTN_WIDTH), cur(KV_WIDTH), prev(KV_WIDTH), cur(KV_WIDTH), prev(KV_WIDTH), cur(ATTN_WIDTH),
                  pl.BlockSpec((2 * gb, 2 * BLOCK, 4 * BLOCK), lambda n: (jnp.minimum(n, steps - 1), 0, 0)),
                  pl.BlockSpec((2 * gb, 1, 4 * BLOCK), lambda n: (jnp.minimum(n, steps - 1), 0, 0)),
                  cur(LANES), cur(LANES), late(LANES), late(LANES)],
        out_specs=[cur(ATTN_WIDTH), late(KV_WIDTH), late(KV_WIDTH), pl.BlockSpec((8, LANES), lambda n: (0, 0))],
        out_shape=[jax.ShapeDtypeStruct((t, ATTN_WIDTH), BF16), jax.ShapeDtypeStruct((t, KV_WIDTH), BF16),
                   jax.ShapeDtypeStruct((t, KV_WIDTH), BF16), jax.ShapeDtypeStruct((8, LANES), F32)],
        scratch_shapes=[pltpu.VMEM((qb, LANES), F32), pltpu.VMEM((qb, LANES), F32),
                        pltpu.VMEM((qb, LANES), F32), pltpu.VMEM((qb, LANES), F32),
                        pltpu.VMEM((BLOCK, LANES), F32), pltpu.VMEM((BLOCK, LANES), F32)],
        semantics=("arbitrary",), name="attn_bwd", exchange=exchange)


def _inproj_bwd_call(dq, dk, dv, dmixed, x, dx1, w_in, g_mix, exchange=None):
    t = x.shape[0]
    tm = _token_tile(t, 1024)
    n_tiles = t // tm

    def body(dq_ref, dk_ref, dv_ref, dm_ref, halo_ref, x_ref, dx1_ref, w_ref, g_ref,
             dx_ref, gw_ref, gb_ref, gg_ref, sc_ref):
        i = pl.program_id(0)

        @pl.when(i == 0)
        def _():
            for ref in (gw_ref, gb_ref, gg_ref):
                ref[...] = jnp.zeros_like(ref)

        tok = i * tm + lax.broadcasted_iota(jnp.int32, (tm, 1), 0)
        halo = jnp.where(i == n_tiles - 1, 0.0, halo_ref[...])
        du = []
        for g, size in enumerate(POOL_SIZES):
            lanes = slice(LANES * g, LANES * (g + 1))
            dm = dm_ref[:, lanes]
            sc_ref[0:tm, lanes] = dm / jnp.minimum(tok + 1, size).astype(F32)
            sc_ref[tm:, lanes] = halo[:, lanes] / float(size)
            acc = -dm
            for j in range(size):
                acc = acc + sc_ref[pl.ds(j, tm), lanes]
            du.append(acc)
        dz32 = jnp.concatenate([dq_ref[...].astype(F32), dk_ref[...].astype(F32), dv_ref[...].astype(F32)] + du,
                               axis=1)
        dz = dz32.astype(BF16)
        gb_ref[...] += jnp.sum(dz32, axis=0, keepdims=True)
        xh, r = _rms_stats(x_ref[...])
        g = g_ref[...]
        gw_ref[...] += _dot_tn(dz, (xh * g).astype(BF16))
        dh = _dot(dz, w_ref[...])
        gg_ref[...] += jnp.sum(dh * xh, axis=0, keepdims=True)
        dx_ref[...] = dx1_ref[...] + _rms_bwd(dh, xh, r, g)

    row = lambda w: pl.BlockSpec((tm, w), lambda i: (i, 0))
    const = lambda shape: pl.BlockSpec(shape, lambda i: (0,) * len(shape))
    last_halo = t // POOL_HALO - 1
    halo_after = pl.BlockSpec((POOL_HALO, POOL_WIDTH),
                              lambda i: (jnp.minimum((i + 1) * (tm // POOL_HALO), last_halo), 0))
    return _compute_call(
        body, (dq, dk, dv, dmixed, dmixed, x, dx1, w_in, g_mix), grid=(n_tiles,),
        in_specs=[row(ATTN_WIDTH), row(KV_WIDTH), row(KV_WIDTH), row(POOL_WIDTH), halo_after, row(D_MODEL),
                  row(D_MODEL), _resident((IN_WIDTH, D_MODEL)), _resident((1, D_MODEL))],
        out_specs=[row(D_MODEL), const((IN_WIDTH, D_MODEL)), const((1, IN_WIDTH)), const((1, D_MODEL))],
        out_shape=[jax.ShapeDtypeStruct((t, D_MODEL), F32), jax.ShapeDtypeStruct((IN_WIDTH, D_MODEL), F32),
                   jax.ShapeDtypeStruct((1, IN_WIDTH), F32), jax.ShapeDtypeStruct((1, D_MODEL), F32)],
        scratch_shapes=[pltpu.VMEM((tm + POOL_HALO, POOL_WIDTH), F32)],
        semantics=("arbitrary",), name="inproj_bwd", exchange=exchange)


def _reduce_to_owner(grads, place, hosts, wire_dtypes):
    me, c = place
    landed = hosts[0](_sibling_exchange(grads))
    partials = _pair_sum_call(grads, landed, c, wire_dtypes)
    slots = hosts[1](_chip_exchange(partials))
    return _chip_sum_call(partials, slots, me, c)


def _local_step(x, target, small, w_in, w_out, w_ffn, place=None):
    t = x.shape[0]
    on_mesh = place is not None
    cos_t, sin_t = _rope_tables(t)
    w_pool = small["w_pool"].astype(BF16)
    w_ffn = list(w_ffn)

    def gathering(bufs, relay_early=0):
        return _gather_exchange(bufs, relay_early) if on_mesh else None

    (h0,), got = _norm_call(x, small["g_mix"], exchange=gathering([w_in]))
    if on_mesh:
        w_in = got[0].reshape(IN_WIDTH, D_MODEL)
    (q, k, v, u), got = _inproj_call(h0, w_in, small["b_in"], cos_t, sin_t, exchange=gathering([w_out], 3))
    if on_mesh:
        w_out = got[0].reshape(D_MODEL, D_MODEL)
    bias_t, bias_first_t = _band_biases()
    (attn, probs, p_sinks), got = _attn_fwd_call(q, k, v, small["sinks"], bias_t, bias_first_t, exchange=gathering(w_ffn[:2]))
    if on_mesh:
        w_ffn[:2] = got
    (x1, h1), got = _outproj_call(attn, u, x, w_pool, small["b_pool"], small["pool_scale"], w_out,
                                          small["b_out"], small["g_ffn"], exchange=gathering(w_ffn[2:], 4))
    if on_mesh:
        w_ffn[2:] = got
    w_gate, w_up, w_down = w_ffn
    act, dgate, dup, dx1, dx2b, loss, g_g_final, g_g_ffn = _ffn_call(h1, x1, target, w_gate, w_up, w_down,
                                                                      small["g_ffn"], small["g_final"])
    ffn_grads = list(_ffn_bwd_weights_call(h1, dgate, dup, act, dx2b))
    kept = {}

    def outproj_bwd(exchange=None):
        kept["outproj"], results = _outproj_bwd_call(dx1, attn, u, w_pool, small["b_pool"], small["pool_scale"],
                                                     w_out, exchange=exchange)
        return results

    def attn_bwd(exchange=None):
        kept["attn"], results = _attn_bwd_call(q, k, v, kept["outproj"][0], probs, p_sinks, cos_t, sin_t,
                                               exchange=exchange)
        return results

    def inproj_bwd(exchange=None):
        dq, dk, dv, _ = kept["attn"]
        kept["inproj"], results = _inproj_bwd_call(dq, dk, dv, kept["outproj"][1], x, dx1, w_in, small["g_mix"],
                                                   exchange=exchange)
        return results

    if on_mesh:
        ffn_grads = _reduce_to_owner(ffn_grads, place, (outproj_bwd, attn_bwd), [BF16] * 3)
    else:
        outproj_bwd(), attn_bwd()
    inproj_bwd()
    _, _, g_w_out, g_b_out, g_w_pool, g_b_pool, g_pool_scale = kept["outproj"]
    g_sinks = kept["attn"][3]
    grad_x, g_w_in, g_b_in, g_g_mix = kept["inproj"]
    small_grads = dict(g_mix=g_g_mix, b_in=g_b_in, sinks=g_sinks[:, 0], w_pool=g_w_pool,
                       b_pool=g_b_pool, pool_scale=g_pool_scale, b_out=g_b_out, g_ffn=g_g_ffn, g_final=g_g_final)
    big_grads = dict(w_in=g_w_in, w_out=g_w_out, w_gate=ffn_grads[0], w_up=ffn_grads[1], w_down=ffn_grads[2])
    return loss, grad_x, small_grads, big_grads


def _cast_place_call(shards, me):
    na = len(shards)
    split = 2

    def body(me_ref, *refs):
        for a in range(na):
            refs[na + a][...] = refs[a][...].astype(BF16)

    in_specs = [pl.BlockSpec((s.shape[0] // split, s.shape[1]), lambda i, me_ref: (i, 0)) for s in shards]
    out_specs = [pl.BlockSpec((None, s.shape[0] // split, s.shape[1]), lambda i, me_ref: (me_ref[0], i, 0))
                 for s in shards]
    return pl.pallas_call(
        body,
        grid_spec=pltpu.PrefetchScalarGridSpec(num_scalar_prefetch=1, grid=(split,), in_specs=in_specs,
                                               out_specs=out_specs),
        out_shape=[jax.ShapeDtypeStruct((N_SHARDS,) + s.shape, BF16) for s in shards],
        compiler_params=_params("parallel"), name="cast_place_weights",
    )(me.reshape(1), *shards)


def _pair_sum_call(grads, landed, c, wire_dtypes):
    na = len(grads)
    split = 2

    def body(c_ref, *refs):
        ins, lands, outs = refs[:na], refs[na:2 * na], refs[2 * na:]
        for a in range(na):
            outs[a][...] = (ins[a][...] + lands[a][...]).astype(outs[a].dtype)

    in_specs, land_specs, out_specs, views = [], [], [], []
    for g in grads:
        ns, r, cols = g.shape
        rows = r // 2 // split
        views.append(g.reshape(ns, 2, r // 2, cols))
        in_specs.append(pl.BlockSpec((None, None, rows, cols), lambda s, i, c_ref: (s, c_ref[0], i, 0)))
        land_specs.append(pl.BlockSpec((None, rows, cols), lambda s, i, c_ref: (s, i, 0)))
        out_specs.append(pl.BlockSpec((None, rows, cols), lambda s, i, c_ref: (s, i, 0)))
    return pl.pallas_call(
        body,
        grid_spec=pltpu.PrefetchScalarGridSpec(num_scalar_prefetch=1, grid=(N_SHARDS, split),
                                               in_specs=in_specs + land_specs, out_specs=out_specs),
        out_shape=[jax.ShapeDtypeStruct(l.shape, dt) for l, dt in zip(landed, wire_dtypes)],
        compiler_params=_params("parallel", "parallel"), name="reduce_pair_sum",
    )(c.reshape(1), *views, *landed)


def _chip_sum_call(partials, slots, me, c):
    na = len(slots)
    split = 2

    def body(place_ref, *refs):
        mine, lands, outs = refs[:na], refs[na:2 * na], refs[2 * na:]
        for j in range(N_SHARDS):
            @pl.when(place_ref[0] == j)
            def _():
                for a in range(na):
                    terms = [mine[a][...] if k == j else lands[a][k] for k in range(N_SHARDS)]
                    t0, t1, t2, t3 = (term.astype(F32) for term in terms)
                    outs[a][...] = ((t0 + t1) + t2) + t3

    mine_specs, land_specs, out_specs = [], [], []
    for s in slots:
        ns, h, cols = s.shape
        mine_specs.append(pl.BlockSpec((None, h // split, cols), lambda i, place: (place[0], i, 0)))
        land_specs.append(pl.BlockSpec((ns, h // split, cols), lambda i, place: (0, i, 0)))
        out_specs.append(pl.BlockSpec((h // split, cols), lambda i, place: (place[1] * split + i, 0)))
    return pl.pallas_call(
        body,
        grid_spec=pltpu.PrefetchScalarGridSpec(num_scalar_prefetch=1, grid=(split,),
                                               in_specs=mine_specs + land_specs, out_specs=out_specs),
        out_shape=[jax.ShapeDtypeStruct((2 * s.shape[1], s.shape[2]), F32) for s in slots],
        compiler_params=_params("parallel"), name="reduce_chip_sum",
    )(jnp.stack([me, c]), *partials, *slots)


def _adamw(w, g, m, v):
    m = ADAM_B1 * m + (1.0 - ADAM_B1) * g
    v = ADAM_B2 * v + (1.0 - ADAM_B2) * jnp.square(g)
    m_hat = m / (1.0 - ADAM_B1 ** ADAM_STEP)
    v_hat = v / (1.0 - ADAM_B2 ** ADAM_STEP)
    return -ADAM_LR * (m_hat / (jnp.sqrt(v_hat) + ADAM_EPS) + ADAM_WD * w), m, v


def _adamw_call(ws, gs, ms, vs, name, after=None):
    na = len(ws)
    split = 4

    def body(*refs):
        w_refs, g_refs, m_refs, v_refs = (refs[k * na:(k + 1) * na] for k in range(4))
        outs = refs[len(refs) - 3 * na:]
        d_refs, nm_refs, nv_refs = (outs[k * na:(k + 1) * na] for k in range(3))
        for a in range(na):
            d_refs[a][...], nm_refs[a][...], nv_refs[a][...] = _adamw(
                w_refs[a][...], g_refs[a][...], m_refs[a][...], v_refs[a][...])

    specs = [pl.BlockSpec((w.shape[0] // split, w.shape[1]), lambda i: (i, 0)) for w in ws]
    shapes = [jax.ShapeDtypeStruct(w.shape, F32) for w in ws]
    extra_specs = [] if after is None else [pl.BlockSpec(after.shape, lambda i: (0, 0))]
    extra = [] if after is None else [after]
    outs = pl.pallas_call(
        body, grid=(split,), in_specs=specs * 4 + extra_specs, out_specs=specs * 3, out_shape=shapes * 3,
        compiler_params=_params("parallel"), name=name,
    )(*ws, *gs, *ms, *vs, *extra)
    return outs[:na], outs[na:2 * na], outs[2 * na:]


def _chip_exchange_start_call(partials):
    na = len(partials)
    n_copies = 3 * na

    def body(*refs):
        ins, lands = refs[:na], refs[na:2 * na]
        sems = refs[2 * na:2 * na + 2 * n_copies]
        token = refs[-1]
        x, y, c, other_chips = _mesh_place()
        me = 2 * x + y
        for j, (cx, cy) in enumerate(other_chips):
            for a in range(na):
                k = j * na + a
                pltpu.make_async_remote_copy(
                    src_ref=ins[a].at[2 * cx + cy], dst_ref=lands[a].at[me], send_sem=sems[k],
                    recv_sem=sems[n_copies + k], device_id=(cx, cy, c), device_id_type=MESH).start()
        token[...] = jnp.zeros_like(token)

    sem_spec = pl.BlockSpec(memory_space=pltpu.SEMAPHORE)
    held = [pltpu.HBM(p.shape, p.dtype) for p in partials]
    results = pl.pallas_call(
        body, name="reduce_chip_exchange_start",
        out_shape=[pltpu.SemaphoreType.DMA(())] * (2 * n_copies) + held + held + [jax.ShapeDtypeStruct((8, LANES), F32)],
        in_specs=[HBM_SPEC] * (2 * na),
        out_specs=[sem_spec] * (2 * n_copies) + [HBM_SPEC] * (2 * na) + [pl.BlockSpec(memory_space=pltpu.VMEM)],
        input_output_aliases={a: 2 * n_copies + a for a in range(2 * na)},
        compiler_params=pltpu.CompilerParams(has_side_effects=pltpu.SideEffectType.DATAFLOW_SIDE_EFFECTING),
    )(*[pltpu.with_memory_space_constraint(p, pltpu.HBM) for p in partials],
      *[pltpu.with_memory_space_constraint(lax.empty(p.shape, p.dtype), pltpu.HBM) for p in partials])
    sems, rest = results[:2 * n_copies], results[2 * n_copies:]
    return sems, rest[:na], rest[na:2 * na], rest[-1]


def _chip_exchange_wait_call(sems, partials, lands, after):
    na = len(partials)
    n_copies = 3 * na

    def body(*refs):
        ins, slots = refs[:na], refs[na:2 * na]
        sem_refs = refs[2 * na:2 * na + 2 * n_copies]
        x, y, c, other_chips = _mesh_place()
        me = 2 * x + y
        for j, (cx, cy) in enumerate(other_chips):
            for a in range(na):
                k = j * na + a
                sent = pltpu.make_async_remote_copy(
                    src_ref=ins[a].at[2 * cx + cy], dst_ref=slots[a].at[me], send_sem=sem_refs[k],
                    recv_sem=sem_refs[n_copies + k], device_id=(cx, cy, c), device_id_type=MESH)
                slot = slots[a].at[2 * cx + cy]
                landed = pltpu.make_async_remote_copy(
                    src_ref=slot, dst_ref=slot, send_sem=sem_refs[k], recv_sem=sem_refs[n_copies + k],
                    device_id=(cx, cy, c), device_id_type=MESH)
                sent.wait_send()
                landed.wait_recv()

    sem_spec = pl.BlockSpec(memory_space=pltpu.SEMAPHORE)
    held = [pltpu.HBM(p.shape, p.dtype) for p in partials]
    results = pl.pallas_call(
        body, name="reduce_chip_exchange_wait", out_shape=held + held,
        in_specs=[HBM_SPEC] * (2 * na) + [sem_spec] * (2 * n_copies) + [pl.BlockSpec(memory_space=pl.ANY)],
        out_specs=[HBM_SPEC] * (2 * na), input_output_aliases={a: a for a in range(2 * na)},
        compiler_params=pltpu.CompilerParams(has_side_effects=pltpu.SideEffectType.DATAFLOW_SIDE_EFFECTING),
    )(*partials, *lands, *sems, after)
    return results[:na], results[na:]


def _small_rows():
    first, row = {}, 0
    for name, (r, c) in SMALL_VIEWS.items():
        first[name] = row
        row += r * (-(-c // LANES))
    return first, row, -(-(row + 1) // 64) * 64


def _pack_small(arrays, loss):
    _, used, rows = _small_rows()
    parts = []
    for name, (r, c) in SMALL_VIEWS.items():
        a = arrays[name].reshape(r, c)
        parts.append(jnp.pad(a, ((0, 0), (0, LANES - c))) if c < LANES else a.reshape(r * c // LANES, LANES))
    parts += [loss, jnp.zeros((rows - used - 1, LANES), F32)]
    return jnp.concatenate(parts, axis=0)


def _adamw_small_call(ws, packed_grads, ms, vs):
    names = list(SMALL_VIEWS)
    first, _, _ = _small_rows()
    n = len(names)

    def body(g_ref, *refs):
        w_refs, m_refs, v_refs = refs[:n], refs[n:2 * n], refs[2 * n:3 * n]
        outs = refs[3 * n:]
        for a, name in enumerate(names):
            r, c = SMALL_VIEWS[name]
            r0 = first[name]
            if c == LANES:
                g = g_ref[r0:r0 + r, :]
            elif c < LANES:
                g = g_ref[r0:r0 + 1, 0:c]
            else:
                g = jnp.concatenate([g_ref[r0 + j:r0 + j + 1, :] for j in range(c // LANES)], axis=1)
            delta, m, v = _adamw(w_refs[a][...], g, m_refs[a][...], v_refs[a][...])
            for k, val in enumerate((g, delta, m, v)):
                outs[k * n + a][...] = val

    shapes = [jax.ShapeDtypeStruct(SMALL_VIEWS[name], F32) for name in names]
    outs = pl.pallas_call(
        body, out_shape=shapes * 4, compiler_params=pltpu.CompilerParams(vmem_limit_bytes=VMEM_LIMIT),
        name="adamw_small",
    )(packed_grads, *[ws[k] for k in names], *[ms[k] for k in names], *[vs[k] for k in names])
    return [dict(zip(names, outs[k * n:(k + 1) * n])) for k in range(4)]


def kernel(x, g_mix, w_in, b_in, sinks, w_pool, b_pool, pool_scale, w_out, b_out, g_ffn, w_gate, w_up, w_down, g_final, loss_target, m_g_mix, m_w_in, m_b_in, m_sinks, m_w_pool, m_b_pool, m_pool_scale, m_w_out, m_b_out, m_g_ffn, m_w_gate, m_w_up, m_w_down, m_g_final, v_g_mix, v_w_in, v_b_in, v_sinks, v_w_pool, v_b_pool, v_pool_scale, v_w_out, v_b_out, v_g_ffn, v_w_gate, v_w_up, v_w_down, v_g_final):
    given = dict(locals())
    big_names = ("w_in", "w_out", "w_gate", "w_up", "w_down")
    out_shapes = {n: given[n].shape for n in SMALL_NAMES + big_names}
    c = lax.axis_index("c")

    me = 2 * lax.axis_index("x") + lax.axis_index("y")
    transposed = ("w_in", "w_gate", "w_up")

    def shard_view(a, name):
        return a[0].T if name in transposed else a[0]

    placed = _cast_place_call([shard_view(given[n], n) for n in big_names], me)

    small = dict(g_mix=g_mix, b_in=b_in, sinks=sinks, w_pool=w_pool[0], b_pool=b_pool.reshape(1, POOL_WIDTH),
                 pool_scale=pool_scale.reshape(1, POOL_WIDTH), b_out=b_out, g_ffn=g_ffn,
                 g_final=g_final.reshape(1, D_MODEL))
    loss, grad_x, small_grads, big_grads = _local_step(
        x[0], loss_target[0], small, placed[0], placed[1], placed[2:], place=(me, c))

    packed = _pack_small(small_grads, loss)
    grads = [
        big_grads["w_in"].reshape(N_SHARDS, IN_WIDTH // N_SHARDS, D_MODEL),
        big_grads["w_out"].reshape(N_SHARDS, D_MODEL // N_SHARDS, D_MODEL),
        jnp.broadcast_to(packed[None], (N_SHARDS,) + packed.shape),
    ]
    def owned(prefix, names):
        return [shard_view(given[prefix + n], n) for n in names]

    def owned_small(prefix):
        return {n: given[prefix + n].reshape(SMALL_VIEWS[n]) for n in SMALL_NAMES}

    ffn_names, rest_names = big_names[2:], big_names[:2]
    first = _exchange_call(_both(_sibling_exchange(grads), _sibling_allgather([big_grads[n] for n in ffn_names])),
                           "reduce_sibling_exchange")
    landed, ffn_totals = first[:len(grads)], first[len(grads):]
    partials = _pair_sum_call(grads, landed, c, [BF16, BF16, F32])
    sems, held, lands, token = _chip_exchange_start_call(partials)
    ffn_updates = _adamw_call(owned("", ffn_names), ffn_totals, owned("m_", ffn_names), owned("v_", ffn_names),
                              "adamw_ffn", after=token)
    held, slots = _chip_exchange_wait_call(sems, held, lands, ffn_updates[0][0])
    t_in, t_out, t_small = _exchange_call(_sibling_allgather(_chip_sum_call(held, slots, me, c)),
                                          "reduce_sibling_allgather")
    loss = t_small[_small_rows()[1], 0]

    rest_updates = _adamw_call(owned("", rest_names), [t_in, t_out], owned("m_", rest_names), owned("v_", rest_names),
                               "adamw_attention")
    big = [[t_in, t_out] + list(ffn_totals)] + [list(r) + list(f) for r, f in zip(rest_updates, ffn_updates)]
    small_groups = _adamw_small_call(owned_small(""), t_small, owned_small("m_"), owned_small("v_"))

    order = ("g_mix", "w_in", "b_in", "sinks", "w_pool", "b_pool", "pool_scale", "w_out", "b_out", "g_ffn",
             "w_gate", "w_up", "w_down", "g_final")
    results = [loss, grad_x.reshape(x.shape)]
    for group, small_group in zip(big, small_groups):
        named = {n: (a.T if n in transposed else a) for n, a in zip(big_names, group)}
        named.update(small_group)
        results += [named[n].reshape(out_shapes[n]) for n in order]
    return tuple(results)
```

```python
import functools
from typing import Any, Callable, Mapping, NamedTuple, Sequence

import jax
import jax.numpy as jnp
from jax import lax
from jax.experimental import pallas as pl
from jax.experimental.pallas import tpu as pltpu

F32 = jnp.float32
BF16 = jnp.bfloat16

D_MODEL = 1024
ATTN_WIDTH = 512
KV_WIDTH = 128
POOL_WIDTH = 512
IN_WIDTH = ATTN_WIDTH + 2 * KV_WIDTH + POOL_WIDTH
HEAD_DIM = 64
N_Q_HEADS = 8
BLOCK = 128
POOL_SIZES = (2, 4, 8, 16)
POOL_HALO = 16
ROPE_THETA = 10000.0
RMS_EPS = 1e-5
Q_SCALE = HEAD_DIM ** -0.5

ADAM_LR = 0.001
ADAM_B1 = 0.9
ADAM_B2 = 0.999
ADAM_EPS = 1e-08
ADAM_WD = 0.01
ADAM_STEP = 10

N_SHARDS = 4
LANES = 128
VMEM_LIMIT = 56 * 1024 * 1024
FFN_TOKEN_TILE = 256

MESH = pl.DeviceIdType.MESH
HBM_SPEC = pl.BlockSpec(memory_space=pltpu.HBM)

SMALL_VIEWS = dict(w_pool=(4 * LANES, LANES), b_pool=(4, LANES), pool_scale=(4, LANES), g_mix=(1, D_MODEL),
                   b_in=(1, IN_WIDTH), b_out=(1, D_MODEL), g_ffn=(1, D_MODEL), g_final=(1, D_MODEL),
                   sinks=(1, N_Q_HEADS))
SMALL_NAMES = tuple(SMALL_VIEWS)


def _dot(a, b):
    return jnp.dot(a, b, preferred_element_type=F32)


def _dot_nt(a, b):
    return lax.dot_general(a, b, (((1,), (1,)), ((), ())), preferred_element_type=F32)


def _dot_tn(a, b):
    return lax.dot_general(a, b, (((0,), (0,)), ((), ())), preferred_element_type=F32)


def _params(*semantics):
    return pltpu.CompilerParams(dimension_semantics=semantics, vmem_limit_bytes=VMEM_LIMIT)


def _resident(shape):
    return pl.BlockSpec(shape, lambda *_: (0,) * len(shape), pipeline_mode=pl.Buffered(1))


def _token_tile(t, largest=512):
    for tm in (2048, 1024, 512, 256, 128):
        if tm <= largest and t % tm == 0:
            return tm
    raise ValueError(f"sequence length {t} is not a multiple of 128")


def _rms_stats(xf):
    r = lax.rsqrt(jnp.mean(xf * xf, axis=-1, keepdims=True) + RMS_EPS)
    return xf * r, r


def _rms_bwd(dy, xh, r, g):
    dxh = dy * g
    return r * (dxh - xh * jnp.mean(dxh * xh, axis=-1, keepdims=True))


def _rope_partner(t, first):
    return jnp.where(first, pltpu.roll(t, LANES - HEAD_DIM // 2, 1), pltpu.roll(t, HEAD_DIM // 2, 1))


def _first_half_mask(shape):
    lane = lax.broadcasted_iota(jnp.int32, shape, 1)
    return (lane % HEAD_DIM) < (HEAD_DIM // 2)


def _rope_tables(t):
    inv_freq = 1.0 / (ROPE_THETA ** (jnp.arange(0, HEAD_DIM, 2, dtype=F32) / HEAD_DIM))
    ang = jnp.arange(t, dtype=F32)[:, None] * inv_freq[None, :]
    cos, sin = jnp.cos(ang), jnp.sin(ang)
    cos_t = jnp.tile(jnp.concatenate([cos, cos], axis=-1), (1, LANES // HEAD_DIM))
    sin_t = jnp.tile(jnp.concatenate([-sin, sin], axis=-1), (1, LANES // HEAD_DIM))
    return cos_t, sin_t


class _Exchange(NamedTuple):
    inputs: Sequence[Any]
    out_shapes: Sequence[Any]
    aliases: Mapping[int, int]
    sems: Sequence[Any]
    start: Callable[..., None]
    finish: Callable[..., None]
    relay: Callable[..., None] = None
    relay_early: int = 0


def _compute_call(body, operands, *, grid, in_specs, out_specs, out_shape, scratch_shapes=(), semantics, name,
                  exchange=None):
    n_in, n_out, n_scr = len(in_specs), len(out_specs), len(scratch_shapes)
    if exchange is None:
        return pl.pallas_call(body, grid=grid, in_specs=in_specs, out_specs=out_specs, out_shape=out_shape,
                              scratch_shapes=scratch_shapes, compiler_params=_params(*semantics),
                              name=name)(*operands), []
    k_in, k_out = len(exchange.inputs), len(exchange.out_shapes)

    def hosting(*refs):
        ins, rest = refs[:n_in], refs[n_in:]
        ex_ins, rest = rest[:k_in], rest[k_in:]
        outs, rest = rest[:n_out], rest[n_out:]
        ex_outs, rest = rest[:k_out], rest[k_out:]
        scratch, sems = rest[:n_scr], rest[n_scr:]
        steps = [pl.program_id(d) for d in range(len(grid))]
        is_first = functools.reduce(jnp.logical_and, [s == 0 for s in steps])
        is_last = functools.reduce(jnp.logical_and, [s == g - 1 for s, g in zip(steps, grid)])
        is_relay = functools.reduce(jnp.logical_and, [s == max(g - 1 - exchange.relay_early, 0)
                                                      for s, g in zip(steps, grid)])

        @pl.when(is_first)
        def _():
            exchange.start(ex_ins, ex_outs, sems)

        body(*ins, *outs, *scratch)

        if exchange.relay is not None:
            @pl.when(is_relay)
            def _():
                exchange.relay(ex_ins, ex_outs, sems)

        @pl.when(is_last)
        def _():
            exchange.finish(ex_ins, ex_outs, sems)

    results = pl.pallas_call(
        hosting, grid=grid,
        in_specs=list(in_specs) + [HBM_SPEC] * k_in, out_specs=list(out_specs) + [HBM_SPEC] * k_out,
        out_shape=list(out_shape) + list(exchange.out_shapes),
        scratch_shapes=list(scratch_shapes) + list(exchange.sems),
        input_output_aliases={n_in + i: n_out + j for i, j in exchange.aliases.items()},
        compiler_params=_params(*semantics), name=name,
    )(*operands, *exchange.inputs)
    return results[:n_out], results[n_out:]


def _both(first, second):
    n_in, n_out, n_sem = len(first.inputs), len(first.out_shapes), len(first.sems)
    aliases = dict(first.aliases)
    aliases.update({n_in + i: n_out + j for i, j in second.aliases.items()})

    def start(ins, outs, sems):
        first.start(ins[:n_in], outs[:n_out], sems[:n_sem])
        second.start(ins[n_in:], outs[n_out:], sems[n_sem:])

    def finish(ins, outs, sems):
        first.finish(ins[:n_in], outs[:n_out], sems[:n_sem])
        second.finish(ins[n_in:], outs[n_out:], sems[n_sem:])

    assert first.relay is None and second.relay is None
    return _Exchange(inputs=list(first.inputs) + list(second.inputs),
                     out_shapes=list(first.out_shapes) + list(second.out_shapes), aliases=aliases,
                     sems=list(first.sems) + list(second.sems), start=start, finish=finish)


def _exchange_call(exchange, name):
    n_in, n_out = len(exchange.inputs), len(exchange.out_shapes)

    def body(*refs):
        ins, outs, sems = refs[:n_in], refs[n_in:n_in + n_out], refs[n_in + n_out:]
        exchange.start(ins, outs, sems)
        if exchange.relay is not None:
            exchange.relay(ins, outs, sems)
        exchange.finish(ins, outs, sems)

    return pl.pallas_call(
        body, in_specs=[HBM_SPEC] * n_in, out_specs=[HBM_SPEC] * n_out, out_shape=list(exchange.out_shapes),
        input_output_aliases=dict(exchange.aliases), scratch_shapes=list(exchange.sems),
        compiler_params=pltpu.CompilerParams(has_side_effects=True), name=name,
    )(*exchange.inputs)


def _mesh_place():
    x, y, c = lax.axis_index("x"), lax.axis_index("y"), lax.axis_index("c")
    other_chips = [(1 - x, y), (x, 1 - y), (1 - x, 1 - y)]
    return x, y, c, other_chips


def _half(ref, c, rows):
    return ref.at[pl.ds(pl.multiple_of(c * rows, 16), rows), :]


def _remote(src, dst, send_sems, recv_sems, idx, to):
    return functools.partial(pltpu.make_async_remote_copy, src_ref=src, dst_ref=dst, send_sem=send_sems.at[idx],
                             recv_sem=recv_sems.at[idx], device_id=to, device_id_type=MESH)


def _gather_exchange(bufs, relay_early=0):
    na = len(bufs)

    def copies(outs, sems):
        send_sems, recv_sems = sems
        x, y, c, other_chips = _mesh_place()
        me, sibling = 2 * x + y, (x, y, 1 - c)
        ici_out, ici_in, fwd_out, fwd_in = [], [], [], []
        for j, (cx, cy) in enumerate(other_chips):
            for a in range(na):
                rows = outs[a].shape[1] // 2
                mine = _half(outs[a].at[me], c, rows)
                landed = _half(outs[a].at[2 * cx + cy], c, rows)
                passed = _half(outs[a].at[2 * cx + cy], 1 - c, rows)
                ici_out.append(_remote(mine, mine, send_sems, recv_sems, j * na + a, (cx, cy, c)))
                ici_in.append(_remote(landed, landed, send_sems, recv_sems, j * na + a, (cx, cy, c)))
                fwd_out.append(_remote(landed, landed, send_sems, recv_sems, (3 + j) * na + a, sibling))
                fwd_in.append(_remote(passed, passed, send_sems, recv_sems, (3 + j) * na + a, sibling))
        return ici_out, ici_in, fwd_out, fwd_in

    def start(ins, outs, sems):
        for cp in copies(outs, sems)[0]:
            cp().start()

    def relay(ins, outs, sems):
        _, ici_in, fwd_out, _ = copies(outs, sems)
        for arrived, forward in zip(ici_in, fwd_out):
            arrived().wait_recv()
            forward().start()

    def finish(ins, outs, sems):
        ici_out, _, fwd_out, fwd_in = copies(outs, sems)
        for cp in fwd_in:
            cp().wait_recv()
        for cp in ici_out + fwd_out:
            cp().wait_send()

    return _Exchange(inputs=bufs, out_shapes=[jax.ShapeDtypeStruct(b.shape, b.dtype) for b in bufs],
                     aliases={a: a for a in range(na)},
                     sems=[pltpu.SemaphoreType.DMA((6 * na,)), pltpu.SemaphoreType.DMA((6 * na,))],
                     start=start, finish=finish, relay=relay, relay_early=relay_early)


def _sibling_exchange(grads):
    na = len(grads)

    def copies(ins, outs, sems):
        x, y, c, _ = _mesh_place()
        out = []
        for a in range(na):
            rows = ins[a].shape[1] // 2
            src = ins[a].at[:, pl.ds(pl.multiple_of((1 - c) * rows, 8), rows), :]
            out.append(_remote(src, outs[a], sems[0], sems[1], a, (x, y, 1 - c)))
        return out

    def start(ins, outs, sems):
        for cp in copies(ins, outs, sems):
            cp().start()

    def finish(ins, outs, sems):
        for cp in copies(ins, outs, sems):
            cp().wait_recv()
            cp().wait_send()

    return _Exchange(inputs=grads,
                     out_shapes=[jax.ShapeDtypeStruct((g.shape[0], g.shape[1] // 2, g.shape[2]), g.dtype)
                                 for g in grads],
                     aliases={}, sems=[pltpu.SemaphoreType.DMA((na,)), pltpu.SemaphoreType.DMA((na,))],
                     start=start, finish=finish)


def _sibling_allgather(bufs):
    na = len(bufs)

    def piece(outs, sems, a, half):
        x, y, c, _ = _mesh_place()
        rows = outs[a].shape[0] // 2
        rows_ref = outs[a].at[pl.ds(pl.multiple_of(half * rows, 8), rows), :]
        return _remote(rows_ref, rows_ref, sems[0], sems[1], a, (x, y, 1 - c))

    def start(ins, outs, sems):
        c = lax.axis_index("c")
        for a in range(na):
            piece(outs, sems, a, c)().start()

    def finish(ins, outs, sems):
        c = lax.axis_index("c")
        for a in range(na):
            piece(outs, sems, a, 1 - c)().wait_recv()
        for a in range(na):
            piece(outs, sems, a, c)().wait_send()

    return _Exchange(inputs=bufs, out_shapes=[jax.ShapeDtypeStruct(b.shape, b.dtype) for b in bufs],
                     aliases={a: a for a in range(na)},
                     sems=[pltpu.SemaphoreType.DMA((na,)), pltpu.SemaphoreType.DMA((na,))],
                     start=start, finish=finish)


def _norm_call(x, g_mix, exchange=None):
    t = x.shape[0]
    tm = _token_tile(t)

    def body(x_ref, g_ref, h_ref):
        xh, _ = _rms_stats(x_ref[...])
        h_ref[...] = (xh * g_ref[...]).astype(BF16)

    row = pl.BlockSpec((tm, D_MODEL), lambda i: (i, 0))
    return _compute_call(body, (x, g_mix), grid=(t // tm,), in_specs=[row, _resident((1, D_MODEL))],
                         out_specs=[row], out_shape=[jax.ShapeDtypeStruct((t, D_MODEL), BF16)],
                         semantics=("parallel",), name="norm_fwd", exchange=exchange)


def _inproj_call(h0, w_in, b_in, cos_t, sin_t, exchange=None):
    t = h0.shape[0]
    tm = _token_tile(t, 1024)

    def body(h_ref, w_ref, b_ref, c_ref, s_ref, q_ref, k_ref, v_ref, u_ref):
        z = _dot_nt(h_ref[...], w_ref[...]) + b_ref[...]
        cos, sin = c_ref[...], s_ref[...]
        first = _first_half_mask((tm, LANES))
        for j in range(ATTN_WIDTH // LANES):
            zj = z[:, LANES * j:LANES * (j + 1)]
            q_ref[:, LANES * j:LANES * (j + 1)] = ((zj * cos + _rope_partner(zj, first) * sin) * Q_SCALE).astype(BF16)
        zk = z[:, ATTN_WIDTH:ATTN_WIDTH + KV_WIDTH]
        k_ref[...] = (zk * cos + _rope_partner(zk, first) * sin).astype(BF16)
        v_ref[...] = z[:, ATTN_WIDTH + KV_WIDTH:ATTN_WIDTH + 2 * KV_WIDTH].astype(BF16)
        u_ref[...] = z[:, ATTN_WIDTH + 2 * KV_WIDTH:]

    row = lambda w: pl.BlockSpec((tm, w), lambda i: (i, 0))
    return _compute_call(
        body, (h0, w_in, b_in, cos_t, sin_t), grid=(t // tm,),
        in_specs=[row(D_MODEL), _resident((IN_WIDTH, D_MODEL)), _resident((1, IN_WIDTH)), row(LANES), row(LANES)],
        out_specs=[row(ATTN_WIDTH), row(KV_WIDTH), row(KV_WIDTH), row(POOL_WIDTH)],
        out_shape=[jax.ShapeDtypeStruct((t, ATTN_WIDTH), BF16),
                   jax.ShapeDtypeStruct((t, KV_WIDTH), BF16), jax.ShapeDtypeStruct((t, KV_WIDTH), BF16),
                   jax.ShapeDtypeStruct((t, POOL_WIDTH), F32)],
        semantics=("parallel",), name="inproj_fwd", exchange=exchange)


def _head_in_both_halves(band, kv_head):
    lane = lax.broadcasted_iota(jnp.int32, band.shape, 1)
    own = (lane < HEAD_DIM) if kv_head == 0 else (lane >= HEAD_DIM)
    return jnp.where(own, band, pltpu.roll(band, HEAD_DIM, 1)).astype(BF16)


def _stack_heads(ref, kv_head, row0):
    low = lax.broadcasted_iota(jnp.int32, (BLOCK, LANES), 1) < HEAD_DIM
    parts = []
    for jj in range(2):
        slab = ref[row0:row0 + BLOCK, LANES * (2 * kv_head + jj):LANES * (2 * kv_head + jj + 1)]
        zero = jnp.zeros_like(slab)
        parts += [jnp.where(low, slab, zero), jnp.where(low, zero, slab)]
    return jnp.concatenate(parts, axis=0)


def _band_biases():
    c = lax.broadcasted_iota(jnp.int32, (2 * BLOCK, 4 * BLOCK), 0)
    i = lax.broadcasted_iota(jnp.int32, (2 * BLOCK, 4 * BLOCK), 1) % BLOCK
    vis = (c > i) & (c <= i + BLOCK)
    return jnp.where(vis, 0.0, -jnp.inf).astype(F32), jnp.where(vis & (c >= BLOCK), 0.0, -jnp.inf).astype(F32)


def _blocks_per_step(nb):
    for g in (8, 4, 2, 1):
        if nb % g == 0:
            return g


def _sink_row(sink_ref, kv_head):
    return jnp.concatenate([jnp.full((1, BLOCK), sink_ref[0, 4 * kv_head + g], F32) for g in range(4)], axis=1)


def _kv_band(cur_ref, prev_ref, g):
    r0 = g * BLOCK
    prev = prev_ref[...] if g == 0 else cur_ref[r0 - BLOCK:r0, :]
    return jnp.concatenate([prev, cur_ref[r0:r0 + BLOCK, :]], axis=0).astype(F32)


def _attn_fwd_call(q, k, v, sinks, bias, bias_first, exchange=None):
    t = q.shape[0]
    nb = t // BLOCK
    gb = _blocks_per_step(nb)
    qb = gb * BLOCK

    def body(sink_ref, q_ref, kc_ref, kp_ref, vc_ref, vp_ref, b_ref, bf_ref, o_ref, p_ref, psink_ref):
        n = pl.program_id(0)
        row_low = lax.broadcasted_iota(jnp.int32, (LANES, BLOCK), 0) < HEAD_DIM
        for g in range(gb):
            r0 = g * BLOCK
            kband = _kv_band(kc_ref, kp_ref, g)
            vband_t = _kv_band(vc_ref, vp_ref, g).T
            bias_g = jnp.where(n == 0, bf_ref[...], b_ref[...]) if g == 0 else b_ref[...]
            for kh in range(2):
                k2 = _head_in_both_halves(kband, kh)
                vt_head = vband_t[HEAD_DIM * kh:HEAD_DIM * (kh + 1), :]
                v2_t = jnp.concatenate([vt_head, vt_head], axis=0).astype(BF16)
                s_t = _dot_nt(k2, _stack_heads(q_ref, kh, r0)) + bias_g
                sink = _sink_row(sink_ref, kh)
                m = jnp.maximum(jnp.max(s_t, axis=0, keepdims=True), sink)
                e = jnp.exp(s_t - m)
                e_sink = jnp.exp(sink - m)
                inv = 1.0 / (jnp.sum(e, axis=0, keepdims=True) + e_sink)
                p_t = (e * inv).astype(BF16)
                p_ref[2 * g + kh] = p_t
                psink_ref[2 * g + kh] = e_sink * inv
                o_t = _dot(v2_t, p_t)
                for jj in range(2):
                    slab = jnp.where(row_low, o_t[:, BLOCK * 2 * jj:BLOCK * (2 * jj + 1)],
                                     o_t[:, BLOCK * (2 * jj + 1):BLOCK * (2 * jj + 2)]).T
                    o_ref[r0:r0 + BLOCK, LANES * (2 * kh + jj):LANES * (2 * kh + jj + 1)] = slab.astype(BF16)

    cur = lambda w: pl.BlockSpec((qb, w), lambda n: (n, 0))
    prev = lambda w: pl.BlockSpec((BLOCK, w), lambda n: (jnp.maximum(n * gb - 1, 0), 0))
    return _compute_call(
        body, (sinks, q, k, k, v, v, bias, bias_first), grid=(nb // gb,),
        in_specs=[pl.BlockSpec(memory_space=pltpu.SMEM), cur(ATTN_WIDTH), cur(KV_WIDTH), prev(KV_WIDTH),
                  cur(KV_WIDTH), prev(KV_WIDTH), _resident(bias.shape), _resident(bias.shape)],
        out_specs=[cur(ATTN_WIDTH), pl.BlockSpec((2 * gb, 2 * BLOCK, 4 * BLOCK), lambda n: (n, 0, 0)),
                   pl.BlockSpec((2 * gb, 1, 4 * BLOCK), lambda n: (n, 0, 0))],
        out_shape=[jax.ShapeDtypeStruct((t, ATTN_WIDTH), BF16),
                   jax.ShapeDtypeStruct((2 * nb, 2 * BLOCK, 4 * BLOCK), BF16),
                   jax.ShapeDtypeStruct((2 * nb, 1, 4 * BLOCK), F32)],
        semantics=("parallel",), name="attn_fwd", exchange=exchange)


def _pool_mixed(sc_ref, u_ref, halo_ref, i, tm):
    sc_ref[0:POOL_HALO, :] = jnp.where(i == 0, 0.0, halo_ref[...])
    sc_ref[POOL_HALO:, :] = u_ref[...]
    tok = i * tm + lax.broadcasted_iota(jnp.int32, (tm, 1), 0)
    mixed = []
    for g, size in enumerate(POOL_SIZES):
        lanes = slice(LANES * g, LANES * (g + 1))
        cur = sc_ref[pl.ds(POOL_HALO, tm), lanes]
        acc = cur
        for j in range(1, size):
            acc = acc + sc_ref[pl.ds(POOL_HALO - j, tm), lanes]
        count = jnp.minimum(tok + 1, size).astype(F32)
        mixed.append(acc / count - cur)
    return mixed


def _halo_before(tm, width):
    return pl.BlockSpec((POOL_HALO, width), lambda i: (jnp.maximum(i * (tm // POOL_HALO) - 1, 0), 0))


def _outproj_call(attn, u, x, w_pool, b_pool, pool_scale, w_out, b_out, g_ffn, exchange=None):
    t = x.shape[0]
    tm = _token_tile(t)

    def body(a_ref, u_ref, halo_ref, x_ref, wp_ref, bp_ref, ps_ref, wo_ref, bo_ref, g_ref,
             x1_ref, h1_ref, sc_ref, y_ref):
        i = pl.program_id(0)
        for g, mixed in enumerate(_pool_mixed(sc_ref, u_ref, halo_ref, i, tm)):
            lanes = slice(LANES * g, LANES * (g + 1))
            lin = _dot(mixed.astype(BF16), wp_ref[g]) + bp_ref[:, lanes]
            y_ref[:, lanes] = (lin * ps_ref[:, lanes]).astype(BF16)
        x1 = (x_ref[...] + _dot(a_ref[...], wo_ref[0:ATTN_WIDTH, :]) + _dot(y_ref[...], wo_ref[ATTN_WIDTH:, :])
              + bo_ref[...])
        x1_ref[...] = x1
        xh, _ = _rms_stats(x1)
        h1_ref[...] = (xh * g_ref[...]).astype(BF16)

    row = lambda w: pl.BlockSpec((tm, w), lambda i: (i, 0))
    return _compute_call(
        body, (attn, u, u, x, w_pool, b_pool, pool_scale, w_out, b_out, g_ffn), grid=(t // tm,),
        in_specs=[row(ATTN_WIDTH), row(POOL_WIDTH), _halo_before(tm, POOL_WIDTH), row(D_MODEL),
                  _resident((4, LANES, LANES)), _resident((1, POOL_WIDTH)), _resident((1, POOL_WIDTH)),
                  _resident((D_MODEL, D_MODEL)), _resident((1, D_MODEL)), _resident((1, D_MODEL))],
        out_specs=[row(D_MODEL), row(D_MODEL)],
        out_shape=[jax.ShapeDtypeStruct((t, D_MODEL), F32), jax.ShapeDtypeStruct((t, D_MODEL), BF16)],
        scratch_shapes=[pltpu.VMEM((tm + POOL_HALO, POOL_WIDTH), F32), pltpu.VMEM((tm, POOL_WIDTH), BF16)],
        semantics=("parallel",), name="outproj_fwd", exchange=exchange)


def _ffn_call(h1, x1, target, w_gate, w_up, w_down, g_ffn, g_final):
    t = x1.shape[0]
    tm = _token_tile(t, FFN_TOKEN_TILE)
    ns, fs, _ = w_gate.shape

    def body(h_ref, x1_ref, tgt_ref, wg_ref, wu_ref, wd_ref, g_ref, gf_ref,
             act_ref, dgate_ref, dup_ref, dx1_ref, dx2b_ref, loss_ref, dgf_ref, dg_ref, gate_ref, up_ref):
        @pl.when(pl.program_id(0) == 0)
        def _():
            loss_ref[...] = jnp.zeros_like(loss_ref)
            dgf_ref[...] = jnp.zeros_like(dgf_ref)
            dg_ref[...] = jnp.zeros_like(dg_ref)

        h = h_ref[...]
        x1 = x1_ref[...]
        x2 = x1
        for s in range(ns):
            gate = _dot_nt(h, wg_ref[s])
            up = _dot_nt(h, wu_ref[s])
            act = (gate * jax.nn.sigmoid(gate) * up).astype(BF16)
            gate_ref[s] = gate
            up_ref[s] = up
            act_ref[s] = act
            x2 = x2 + _dot(act, wd_ref[s])
        xh2, r2 = _rms_stats(x2)
        gf = gf_ref[...]
        err = xh2 * gf - tgt_ref[...]
        loss_ref[...] += 0.5 * jnp.sum(err * err) / D_MODEL
        dy = err / D_MODEL
        dgf_ref[...] += jnp.sum(dy * xh2, axis=0, keepdims=True)
        dx2 = _rms_bwd(dy, xh2, r2, gf)
        dx2b = dx2.astype(BF16)
        dx2b_ref[...] = dx2b
        dh = None
        for s in range(ns):
            dact = _dot_nt(dx2b, wd_ref[s])
            gate = gate_ref[s]
            sg = jax.nn.sigmoid(gate)
            dup = (dact * (gate * sg)).astype(BF16)
            dgate = (dact * up_ref[s] * (sg * (1.0 + gate * (1.0 - sg)))).astype(BF16)
            dgate_ref[s] = dgate
            dup_ref[s] = dup
            part = _dot(dgate, wg_ref[s]) + _dot(dup, wu_ref[s])
            dh = part if dh is None else dh + part
        xh1, r1 = _rms_stats(x1)
        dg_ref[...] += jnp.sum(dh * xh1, axis=0, keepdims=True)
        dx1_ref[...] = dx2 + _rms_bwd(dh, xh1, r1, g_ref[...])

    row = lambda w: pl.BlockSpec((tm, w), lambda i: (i, 0))
    shard_act = pl.BlockSpec((ns, tm, fs), lambda i: (0, i, 0))
    act_shape = jax.ShapeDtypeStruct((ns, t, fs), BF16)
    vec = pl.BlockSpec((1, D_MODEL), lambda i: (0, 0))
    return pl.pallas_call(
        body, grid=(t // tm,),
        in_specs=[row(D_MODEL), row(D_MODEL), row(D_MODEL), _resident((ns, fs, D_MODEL)),
                  _resident((ns, fs, D_MODEL)), _resident((ns, fs, D_MODEL)), _resident((1, D_MODEL)),
                  _resident((1, D_MODEL))],
        out_specs=[shard_act, shard_act, shard_act, row(D_MODEL), row(D_MODEL),
                   pl.BlockSpec((1, LANES), lambda i: (0, 0)), vec, vec],
        out_shape=[act_shape, act_shape, act_shape, jax.ShapeDtypeStruct((t, D_MODEL), F32),
                   jax.ShapeDtypeStruct((t, D_MODEL), BF16), jax.ShapeDtypeStruct((1, LANES), F32),
                   jax.ShapeDtypeStruct((1, D_MODEL), F32), jax.ShapeDtypeStruct((1, D_MODEL), F32)],
        scratch_shapes=[pltpu.VMEM((ns, tm, fs), F32), pltpu.VMEM((ns, tm, fs), F32)],
        compiler_params=_params("arbitrary"), name="ffn_fwd_bwd",
    )(h1, x1, target, w_gate, w_up, w_down, g_ffn, g_final)


def _ffn_bwd_weights_call(h1, dgate, dup, act, dx2):
    t = h1.shape[0]
    tm = _token_tile(t, 2048)
    ns, _, fs = dgate.shape

    def body(h_ref, dgate_ref, dup_ref, act_ref, dx2_ref, gg_ref, gu_ref, gd_ref):
        @pl.when(pl.program_id(1) == 0)
        def _():
            gg_ref[...] = jnp.zeros_like(gg_ref)
            gu_ref[...] = jnp.zeros_like(gu_ref)
            gd_ref[...] = jnp.zeros_like(gd_ref)

        h = h_ref[...]
        gg_ref[...] += _dot_tn(dgate_ref[...], h)
        gu_ref[...] += _dot_tn(dup_ref[...], h)
        gd_ref[...] += _dot_tn(act_ref[...], dx2_ref[...])

    row = lambda w: pl.BlockSpec((tm, w), lambda s, i: (i, 0))
    shard_act = pl.BlockSpec((None, tm, fs), lambda s, i: (s, i, 0))
    return pl.pallas_call(
        body, grid=(ns, t // tm),
        in_specs=[row(D_MODEL), shard_act, shard_act, shard_act, row(D_MODEL)],
        out_specs=[pl.BlockSpec((None, fs, D_MODEL), lambda s, i: (s, 0, 0))] * 3,
        out_shape=[jax.ShapeDtypeStruct((ns, fs, D_MODEL), F32)] * 3,
        compiler_params=_params("parallel", "arbitrary"), name="ffn_bwd_weights",
    )(h1, dgate, dup, act, dx2)


def _outproj_bwd_call(dx1, attn, u, w_pool, b_pool, pool_scale, w_out, exchange=None):
    t = dx1.shape[0]
    tm = _token_tile(t, 1024)

    def body(dx1_ref, a_ref, u_ref, halo_ref, wp_ref, bp_ref, ps_ref, wo_ref,
             dattn_ref, dmixed_ref, gwo_ref, gbo_ref, gwp_ref, gbp_ref, gps_ref, sc_ref, y_ref):
        i = pl.program_id(0)

        @pl.when(i == 0)
        def _():
            for ref in (gwo_ref, gbo_ref, gwp_ref, gbp_ref, gps_ref):
                ref[...] = jnp.zeros_like(ref)

        dx1 = dx1_ref[...]
        dx1b = dx1.astype(BF16)
        dcat = _dot_nt(dx1b, wo_ref[...])
        dattn_ref[...] = dcat[:, :ATTN_WIDTH].astype(BF16)
        for g, mixed in enumerate(_pool_mixed(sc_ref, u_ref, halo_ref, i, tm)):
            lanes = slice(LANES * g, LANES * (g + 1))
            mixed_b = mixed.astype(BF16)
            lin = _dot(mixed_b, wp_ref[g]) + bp_ref[:, lanes]
            y_ref[:, lanes] = (lin * ps_ref[:, lanes]).astype(BF16)
            dpool = dcat[:, ATTN_WIDTH + LANES * g:ATTN_WIDTH + LANES * (g + 1)]
            gps_ref[:, lanes] += jnp.sum(dpool * lin, axis=0, keepdims=True)
            dy = dpool * ps_ref[:, lanes]
            gbp_ref[:, lanes] += jnp.sum(dy, axis=0, keepdims=True)
            dy_b = dy.astype(BF16)
            gwp_ref[g] += _dot_tn(mixed_b, dy_b)
            dmixed_ref[:, lanes] = _dot_nt(dy_b, wp_ref[g])
        gwo_ref[0:ATTN_WIDTH, :] += _dot_tn(a_ref[...], dx1b)
        gwo_ref[ATTN_WIDTH:, :] += _dot_tn(y_ref[...], dx1b)
        gbo_ref[...] += jnp.sum(dx1, axis=0, keepdims=True)

    row = lambda w: pl.BlockSpec((tm, w), lambda i: (i, 0))
    const = lambda shape: pl.BlockSpec(shape, lambda i: (0,) * len(shape))
    return _compute_call(
        body, (dx1, attn, u, u, w_pool, b_pool, pool_scale, w_out), grid=(t // tm,),
        in_specs=[row(D_MODEL), row(ATTN_WIDTH), row(POOL_WIDTH), _halo_before(tm, POOL_WIDTH),
                  _resident((4, LANES, LANES)), _resident((1, POOL_WIDTH)), _resident((1, POOL_WIDTH)),
                  _resident((D_MODEL, D_MODEL))],
        out_specs=[row(ATTN_WIDTH), row(POOL_WIDTH), const((D_MODEL, D_MODEL)), const((1, D_MODEL)),
                   const((4, LANES, LANES)), const((1, POOL_WIDTH)), const((1, POOL_WIDTH))],
        out_shape=[jax.ShapeDtypeStruct((t, ATTN_WIDTH), BF16), jax.ShapeDtypeStruct((t, POOL_WIDTH), F32),
                   jax.ShapeDtypeStruct((D_MODEL, D_MODEL), F32), jax.ShapeDtypeStruct((1, D_MODEL), F32),
                   jax.ShapeDtypeStruct((4, LANES, LANES), F32), jax.ShapeDtypeStruct((1, POOL_WIDTH), F32),
                   jax.ShapeDtypeStruct((1, POOL_WIDTH), F32)],
        scratch_shapes=[pltpu.VMEM((tm + POOL_HALO, POOL_WIDTH), F32), pltpu.VMEM((tm, POOL_WIDTH), BF16)],
        semantics=("arbitrary",), name="outproj_bwd", exchange=exchange)


def _attn_bwd_call(q, k, v, dout, probs, p_sinks, cos_t, sin_t, exchange=None):
    t = q.shape[0]
    nb = t // BLOCK
    gb = _blocks_per_step(nb)
    qb = gb * BLOCK
    steps = nb // gb

    def body(q_ref, kc_ref, kp_ref, vc_ref, vp_ref, do_ref, p_ref, psink_ref, cq_ref, sq_ref, ck_ref, sk_ref,
             dq_ref, dk_ref, dv_ref, dsink_ref, newk_ref, newv_ref, carryk_ref, carryv_ref, headk_ref, headv_ref):
        n = pl.program_id(0)
        first = _first_half_mask((BLOCK, LANES))

        @pl.when(n == 0)
        def _():
            dsink_ref[...] = jnp.zeros_like(dsink_ref)

        @pl.when(n < steps)
        def _():
            head_row = lax.broadcasted_iota(jnp.int32, dsink_ref.shape, 0)
            lane2 = lax.broadcasted_iota(jnp.int32, (2 * BLOCK, LANES), 1)
            row_low = lax.broadcasted_iota(jnp.int32, (LANES, BLOCK), 0) < HEAD_DIM
            dsink = jnp.zeros(dsink_ref.shape, F32)
            for g in range(gb):
                r0 = g * BLOCK
                kband = _kv_band(kc_ref, kp_ref, g)
                vband = _kv_band(vc_ref, vp_ref, g)
                kband_t = kband.T
                cq, sq = cq_ref[r0:r0 + BLOCK, :], sq_ref[r0:r0 + BLOCK, :]
                dk_heads, dv_heads = [], []
                for kh in range(2):
                    v2 = _head_in_both_halves(vband, kh)
                    kt_head = kband_t[HEAD_DIM * kh:HEAD_DIM * (kh + 1), :]
                    k2_t = jnp.concatenate([kt_head, kt_head], axis=0).astype(BF16)
                    qst = _stack_heads(q_ref, kh, r0)
                    dost = _stack_heads(do_ref, kh, r0)
                    p_b = p_ref[2 * g + kh]
                    p_t = p_b.astype(F32)
                    dp_t = _dot_nt(v2, dost)
                    delta = jnp.sum(p_t * dp_t, axis=0, keepdims=True)
                    ds_t = (p_t * (dp_t - delta)).astype(BF16)
                    sink_term = psink_ref[2 * g + kh] * delta
                    for h in range(4):
                        val = -jnp.sum(sink_term[:, BLOCK * h:BLOCK * (h + 1)])
                        dsink = dsink + jnp.where(head_row == 4 * kh + h, val, 0.0)
                    dq_t = _dot(k2_t, ds_t)
                    for jj in range(2):
                        slab = jnp.where(row_low, dq_t[:, BLOCK * 2 * jj:BLOCK * (2 * jj + 1)],
                                         dq_t[:, BLOCK * (2 * jj + 1):BLOCK * (2 * jj + 2)]).T
                        dq_ref[r0:r0 + BLOCK, LANES * (2 * kh + jj):LANES * (2 * kh + jj + 1)] = (
                            (slab * cq + _rope_partner(slab * sq, first)) * Q_SCALE).astype(BF16)
                    dk2 = _dot(ds_t, qst)
                    dv2 = _dot(p_b, dost)
                    dk_heads.append(dk2 + pltpu.roll(dk2, HEAD_DIM, 1))
                    dv_heads.append(dv2 + pltpu.roll(dv2, HEAD_DIM, 1))
                bandk = jnp.where(lane2 < HEAD_DIM, dk_heads[0], dk_heads[1])
                bandv = jnp.where(lane2 < HEAD_DIM, dv_heads[0], dv_heads[1])
                if g == 0:
                    headk_ref[...] = bandk[0:BLOCK]
                    headv_ref[...] = bandv[0:BLOCK]
                else:
                    newk_ref[r0 - BLOCK:r0, :] += bandk[0:BLOCK]
                    newv_ref[r0 - BLOCK:r0, :] += bandv[0:BLOCK]
                newk_ref[r0:r0 + BLOCK, :] = bandk[BLOCK:]
                newv_ref[r0:r0 + BLOCK, :] = bandv[BLOCK:]
            dsink_ref[...] += dsink

        @pl.when(n == steps)
        def _():
            headk_ref[...] = jnp.zeros_like(headk_ref)
            headv_ref[...] = jnp.zeros_like(headv_ref)

        @pl.when(n >= 1)
        def _():
            last = qb - BLOCK
            carryk_ref[last:, :] += headk_ref[...]
            carryv_ref[last:, :] += headv_ref[...]
            for g in range(gb):
                rows = slice(g * BLOCK, (g + 1) * BLOCK)
                dk = carryk_ref[rows, :]
                dk_ref[rows, :] = (dk * ck_ref[rows, :] + _rope_partner(dk * sk_ref[rows, :], first)).astype(BF16)
            dv_ref[...] = carryv_ref[...].astype(BF16)

        @pl.when(n < steps)
        def _():
            carryk_ref[...] = newk_ref[...]
            carryv_ref[...] = newv_ref[...]

    cur = lambda w: pl.BlockSpec((qb, w), lambda n: (jnp.minimum(n, steps - 1), 0))
    late = lambda w: pl.BlockSpec((qb, w), lambda n: (jnp.maximum(n - 1, 0), 0))
    prev = lambda w: pl.BlockSpec((BLOCK, w), lambda n: (jnp.maximum(jnp.minimum(n, steps - 1) * gb - 1, 0), 0))
    return _compute_call(
        body, (q, k, k, v, v, dout, probs, p_sinks, cos_t, sin_t, cos_t, sin_t), grid=(steps + 1,),
        in_specs=[cur(ATTN_WIDTH), cur(KV_WIDTH), prev(KV_WIDTH), cur(KV_WIDTH), prev(KV_WIDTH), cur(ATTN_WIDTH),
                  pl.BlockSpec((2 * gb, 2 * BLOCK, 4 * BLOCK), lambda n: (jnp.minimum(n, steps - 1), 0, 0)),
                  pl.BlockSpec((2 * gb, 1, 4 * BLOCK), lambda n: (jnp.minimum(n, steps - 1), 0, 0)),
                  cur(LANES), cur(LANES), late(LANES), late(LANES)],
        out_specs=[cur(ATTN_WIDTH), late(KV_WIDTH), late(KV_WIDTH), pl.BlockSpec((8, LANES), lambda n: (0, 0))],
        out_shape=[jax.ShapeDtypeStruct((t, ATTN_WIDTH), BF16), jax.ShapeDtypeStruct((t, KV_WIDTH), BF16),
                   jax.ShapeDtypeStruct((t, KV_WIDTH), BF16), jax.ShapeDtypeStruct((8, LANES), F32)],
        scratch_shapes=[pltpu.VMEM((qb, LANES), F32), pltpu.VMEM((qb, LANES), F32),
                        pltpu.VMEM((qb, LANES), F32), pltpu.VMEM((qb, LANES), F32),
                        pltpu.VMEM((BLOCK, LANES), F32), pltpu.VMEM((BLOCK, LANES), F32)],
        semantics=("arbitrary",), name="attn_bwd", exchange=exchange)


def _inproj_bwd_call(dq, dk, dv, dmixed, x, dx1, w_in, g_mix, exchange=None):
    t = x.shape[0]
    tm = _token_tile(t, 1024)
    n_tiles = t // tm

    def body(dq_ref, dk_ref, dv_ref, dm_ref, halo_ref, x_ref, dx1_ref, w_ref, g_ref,
             dx_ref, gw_ref, gb_ref, gg_ref, sc_ref):
        i = pl.program_id(0)

        @pl.when(i == 0)
        def _():
            for ref in (gw_ref, gb_ref, gg_ref):
                ref[...] = jnp.zeros_like(ref)

        tok = i * tm + lax.broadcasted_iota(jnp.int32, (tm, 1), 0)
        halo = jnp.where(i == n_tiles - 1, 0.0, halo_ref[...])
        du = []
        for g, size in enumerate(POOL_SIZES):
            lanes = slice(LANES * g, LANES * (g + 1))
            dm = dm_ref[:, lanes]
            sc_ref[0:tm, lanes] = dm / jnp.minimum(tok + 1, size).astype(F32)
            sc_ref[tm:, lanes] = halo[:, lanes] / float(size)
            acc = -dm
            for j in range(size):
                acc = acc + sc_ref[pl.ds(j, tm), lanes]
            du.append(acc)
        dz32 = jnp.concatenate([dq_ref[...].astype(F32), dk_ref[...].astype(F32), dv_ref[...].astype(F32)] + du,
                               axis=1)
        dz = dz32.astype(BF16)
        gb_ref[...] += jnp.sum(dz32, axis=0, keepdims=True)
        xh, r = _rms_stats(x_ref[...])
        g = g_ref[...]
        gw_ref[...] += _dot_tn(dz, (xh * g).astype(BF16))
        dh = _dot(dz, w_ref[...])
        gg_ref[...] += jnp.sum(dh * xh, axis=0, keepdims=True)
        dx_ref[...] = dx1_ref[...] + _rms_bwd(dh, xh, r, g)

    row = lambda w: pl.BlockSpec((tm, w), lambda i: (i, 0))
    const = lambda shape: pl.BlockSpec(shape, lambda i: (0,) * len(shape))
    last_halo = t // POOL_HALO - 1
    halo_after = pl.BlockSpec((POOL_HALO, POOL_WIDTH),
                              lambda i: (jnp.minimum((i + 1) * (tm // POOL_HALO), last_halo), 0))
    return _compute_call(
        body, (dq, dk, dv, dmixed, dmixed, x, dx1, w_in, g_mix), grid=(n_tiles,),
        in_specs=[row(ATTN_WIDTH), row(KV_WIDTH), row(KV_WIDTH), row(POOL_WIDTH), halo_after, row(D_MODEL),
                  row(D_MODEL), _resident((IN_WIDTH, D_MODEL)), _resident((1, D_MODEL))],
        out_specs=[row(D_MODEL), const((IN_WIDTH, D_MODEL)), const((1, IN_WIDTH)), const((1, D_MODEL))],
        out_shape=[jax.ShapeDtypeStruct((t, D_MODEL), F32), jax.ShapeDtypeStruct((IN_WIDTH, D_MODEL), F32),
                   jax.ShapeDtypeStruct((1, IN_WIDTH), F32), jax.ShapeDtypeStruct((1, D_MODEL), F32)],
        scratch_shapes=[pltpu.VMEM((tm + POOL_HALO, POOL_WIDTH), F32)],
        semantics=("arbitrary",), name="inproj_bwd", exchange=exchange)


def _local_step(x, target, small, w_in, w_out, w_ffn, place=None):
    t = x.shape[0]
    on_mesh = place is not None
    cos_t, sin_t = _rope_tables(t)
    w_pool = small["w_pool"].astype(BF16)
    w_ffn = list(w_ffn)

    def gathering(bufs, relay_early=0):
        return _gather_exchange(bufs, relay_early) if on_mesh else None

    (h0,), got = _norm_call(x, small["g_mix"], exchange=gathering([w_in]))
    if on_mesh:
        w_in = got[0].reshape(IN_WIDTH, D_MODEL)
    (q, k, v, u), got = _inproj_call(h0, w_in, small["b_in"], cos_t, sin_t, exchange=gathering([w_out], 3))
    if on_mesh:
        w_out = got[0].reshape(D_MODEL, D_MODEL)
    bias_t, bias_first_t = _band_biases()
    (attn, probs, p_sinks), got = _attn_fwd_call(q, k, v, small["sinks"], bias_t, bias_first_t, exchange=gathering(w_ffn[:2]))
    if on_mesh:
        w_ffn[:2] = got
    (x1, h1), got = _outproj_call(attn, u, x, w_pool, small["b_pool"], small["pool_scale"], w_out,
                                          small["b_out"], small["g_ffn"], exchange=gathering(w_ffn[2:], 4))
    if on_mesh:
        w_ffn[2:] = got
    w_gate, w_up, w_down = w_ffn
    act, dgate, dup, dx1, dx2b, loss, g_g_final, g_g_ffn = _ffn_call(h1, x1, target, w_gate, w_up, w_down,
                                                                      small["g_ffn"], small["g_final"])
    ffn_grads = list(_ffn_bwd_weights_call(h1, dgate, dup, act, dx2b))
    kept = {}

    def outproj_bwd(exchange=None):
        kept["outproj"], results = _outproj_bwd_call(dx1, attn, u, w_pool, small["b_pool"], small["pool_scale"],
                                                     w_out, exchange=exchange)
        return results

    def attn_bwd(exchange=None):
        kept["attn"], results = _attn_bwd_call(q, k, v, kept["outproj"][0], probs, p_sinks, cos_t, sin_t,
                                               exchange=exchange)
        return results

    def inproj_bwd(exchange=None):
        dq, dk, dv, _ = kept["attn"]
        kept["inproj"], results = _inproj_bwd_call(dq, dk, dv, kept["outproj"][1], x, dx1, w_in, small["g_mix"],
                                                   exchange=exchange)
        return results

    if on_mesh:
        me, c = place
        landed = outproj_bwd(_sibling_exchange(ffn_grads))
        partials = _pair_sum_call(ffn_grads, landed, c, [BF16] * 3)
        sems, held, lands, _ = _chip_exchange_start_call(partials, "ffn_chip_exchange_start")
        attn_bwd()
        inproj_bwd()
        held, slots = _chip_exchange_wait_call(sems, held, lands, kept["inproj"][1], "ffn_chip_exchange_wait")
        ffn_grads = _chip_sum_call(held, slots, me, c)
    else:
        outproj_bwd(), attn_bwd()
        inproj_bwd()
    _, _, g_w_out, g_b_out, g_w_pool, g_b_pool, g_pool_scale = kept["outproj"]
    g_sinks = kept["attn"][3]
    grad_x, g_w_in, g_b_in, g_g_mix = kept["inproj"]
    small_grads = dict(g_mix=g_g_mix, b_in=g_b_in, sinks=g_sinks[:, 0], w_pool=g_w_pool,
                       b_pool=g_b_pool, pool_scale=g_pool_scale, b_out=g_b_out, g_ffn=g_g_ffn, g_final=g_g_final)
    big_grads = dict(w_in=g_w_in, w_out=g_w_out, w_gate=ffn_grads[0], w_up=ffn_grads[1], w_down=ffn_grads[2])
    return loss, grad_x, small_grads, big_grads


def _cast_place_call(shards, me):
    na = len(shards)
    split = 2

    def body(me_ref, *refs):
        for a in range(na):
            refs[na + a][...] = refs[a][...].astype(BF16)

    in_specs = [pl.BlockSpec((s.shape[0] // split, s.shape[1]), lambda i, me_ref: (i, 0)) for s in shards]
    out_specs = [pl.BlockSpec((None, s.shape[0] // split, s.shape[1]), lambda i, me_ref: (me_ref[0], i, 0))
                 for s in shards]
    return pl.pallas_call(
        body,
        grid_spec=pltpu.PrefetchScalarGridSpec(num_scalar_prefetch=1, grid=(split,), in_specs=in_specs,
                                               out_specs=out_specs),
        out_shape=[jax.ShapeDtypeStruct((N_SHARDS,) + s.shape, BF16) for s in shards],
        compiler_params=_params("parallel"), name="cast_place_weights",
    )(me.reshape(1), *shards)


def _pair_sum_call(grads, landed, c, wire_dtypes):
    na = len(grads)
    split = 2

    def body(c_ref, *refs):
        ins, lands, outs = refs[:na], refs[na:2 * na], refs[2 * na:]
        for a in range(na):
            outs[a][...] = (ins[a][...] + lands[a][...]).astype(outs[a].dtype)

    in_specs, land_specs, out_specs, views = [], [], [], []
    for g in grads:
        ns, r, cols = g.shape
        rows = r // 2 // split
        views.append(g.reshape(ns, 2, r // 2, cols))
        in_specs.append(pl.BlockSpec((None, None, rows, cols), lambda s, i, c_ref: (s, c_ref[0], i, 0)))
        land_specs.append(pl.BlockSpec((None, rows, cols), lambda s, i, c_ref: (s, i, 0)))
        out_specs.append(pl.BlockSpec((None, rows, cols), lambda s, i, c_ref: (s, i, 0)))
    return pl.pallas_call(
        body,
        grid_spec=pltpu.PrefetchScalarGridSpec(num_scalar_prefetch=1, grid=(N_SHARDS, split),
                                               in_specs=in_specs + land_specs, out_specs=out_specs),
        out_shape=[jax.ShapeDtypeStruct(l.shape, dt) for l, dt in zip(landed, wire_dtypes)],
        compiler_params=_params("parallel", "parallel"), name="reduce_pair_sum",
    )(c.reshape(1), *views, *landed)


def _chip_sum_call(partials, slots, me, c):
    na = len(slots)
    split = 2

    def body(place_ref, *refs):
        mine, lands, outs = refs[:na], refs[na:2 * na], refs[2 * na:]
        for j in range(N_SHARDS):
            @pl.when(place_ref[0] == j)
            def _():
                for a in range(na):
                    terms = [mine[a][...] if k == j else lands[a][k] for k in range(N_SHARDS)]
                    t0, t1, t2, t3 = (term.astype(F32) for term in terms)
                    outs[a][...] = ((t0 + t1) + t2) + t3

    mine_specs, land_specs, out_specs = [], [], []
    for s in slots:
        ns, h, cols = s.shape
        mine_specs.append(pl.BlockSpec((None, h // split, cols), lambda i, place: (place[0], i, 0)))
        land_specs.append(pl.BlockSpec((ns, h // split, cols), lambda i, place: (0, i, 0)))
        out_specs.append(pl.BlockSpec((h // split, cols), lambda i, place: (place[1] * split + i, 0)))
    return pl.pallas_call(
        body,
        grid_spec=pltpu.PrefetchScalarGridSpec(num_scalar_prefetch=1, grid=(split,),
                                               in_specs=mine_specs + land_specs, out_specs=out_specs),
        out_shape=[jax.ShapeDtypeStruct((2 * s.shape[1], s.shape[2]), F32) for s in slots],
        compiler_params=_params("parallel"), name="reduce_chip_sum",
    )(jnp.stack([me, c]), *partials, *slots)


def _adamw(w, g, m, v):
    m = ADAM_B1 * m + (1.0 - ADAM_B1) * g
    v = ADAM_B2 * v + (1.0 - ADAM_B2) * jnp.square(g)
    m_hat = m / (1.0 - ADAM_B1 ** ADAM_STEP)
    v_hat = v / (1.0 - ADAM_B2 ** ADAM_STEP)
    return -ADAM_LR * (m_hat / (jnp.sqrt(v_hat) + ADAM_EPS) + ADAM_WD * w), m, v


def _adamw_call(ws, gs, ms, vs, name, after=None):
    na = len(ws)
    split = 4

    def body(*refs):
        w_refs, g_refs, m_refs, v_refs = (refs[k * na:(k + 1) * na] for k in range(4))
        outs = refs[len(refs) - 3 * na:]
        d_refs, nm_refs, nv_refs = (outs[k * na:(k + 1) * na] for k in range(3))
        for a in range(na):
            d_refs[a][...], nm_refs[a][...], nv_refs[a][...] = _adamw(
                w_refs[a][...], g_refs[a][...], m_refs[a][...], v_refs[a][...])

    specs = [pl.BlockSpec((w.shape[0] // split, w.shape[1]), lambda i: (i, 0)) for w in ws]
    shapes = [jax.ShapeDtypeStruct(w.shape, F32) for w in ws]
    extra_specs = [] if after is None else [pl.BlockSpec(after.shape, lambda i: (0, 0))]
    extra = [] if after is None else [after]
    outs = pl.pallas_call(
        body, grid=(split,), in_specs=specs * 4 + extra_specs, out_specs=specs * 3, out_shape=shapes * 3,
        compiler_params=_params("parallel"), name=name,
    )(*ws, *gs, *ms, *vs, *extra)
    return outs[:na], outs[na:2 * na], outs[2 * na:]


def _chip_exchange_start_call(partials, name):
    na = len(partials)
    n_copies = 3 * na

    def body(*refs):
        ins, lands = refs[:na], refs[na:2 * na]
        sems = refs[2 * na:2 * na + 2 * n_copies]
        token = refs[-1]
        x, y, c, other_chips = _mesh_place()
        me = 2 * x + y
        for j, (cx, cy) in enumerate(other_chips):
            for a in range(na):
                k = j * na + a
                pltpu.make_async_remote_copy(
                    src_ref=ins[a].at[2 * cx + cy], dst_ref=lands[a].at[me], send_sem=sems[k],
                    recv_sem=sems[n_copies + k], device_id=(cx, cy, c), device_id_type=MESH).start()
        token[...] = jnp.zeros_like(token)

    sem_spec = pl.BlockSpec(memory_space=pltpu.SEMAPHORE)
    held = [pltpu.HBM(p.shape, p.dtype) for p in partials]
    results = pl.pallas_call(
        body, name=name,
        out_shape=[pltpu.SemaphoreType.DMA(())] * (2 * n_copies) + held + held + [jax.ShapeDtypeStruct((8, LANES), F32)],
        in_specs=[HBM_SPEC] * (2 * na),
        out_specs=[sem_spec] * (2 * n_copies) + [HBM_SPEC] * (2 * na) + [pl.BlockSpec(memory_space=pltpu.VMEM)],
        input_output_aliases={a: 2 * n_copies + a for a in range(2 * na)},
        compiler_params=pltpu.CompilerParams(has_side_effects=pltpu.SideEffectType.DATAFLOW_SIDE_EFFECTING),
    )(*[pltpu.with_memory_space_constraint(p, pltpu.HBM) for p in partials],
      *[pltpu.with_memory_space_constraint(lax.empty(p.shape, p.dtype), pltpu.HBM) for p in partials])
    sems, rest = results[:2 * n_copies], results[2 * n_copies:]
    return sems, rest[:na], rest[na:2 * na], rest[-1]


def _chip_exchange_wait_call(sems, partials, lands, after, name):
    na = len(partials)
    n_copies = 3 * na

    def body(*refs):
        ins, slots = refs[:na], refs[na:2 * na]
        sem_refs = refs[2 * na:2 * na + 2 * n_copies]
        x, y, c, other_chips = _mesh_place()
        me = 2 * x + y
        for j, (cx, cy) in enumerate(other_chips):
            for a in range(na):
                k = j * na + a
                sent = pltpu.make_async_remote_copy(
                    src_ref=ins[a].at[2 * cx + cy], dst_ref=slots[a].at[me], send_sem=sem_refs[k],
                    recv_sem=sem_refs[n_copies + k], device_id=(cx, cy, c), device_id_type=MESH)
                slot = slots[a].at[2 * cx + cy]
                landed = pltpu.make_async_remote_copy(
                    src_ref=slot, dst_ref=slot, send_sem=sem_refs[k], recv_sem=sem_refs[n_copies + k],
                    device_id=(cx, cy, c), device_id_type=MESH)
                sent.wait_send()
                landed.wait_recv()

    sem_spec = pl.BlockSpec(memory_space=pltpu.SEMAPHORE)
    held = [pltpu.HBM(p.shape, p.dtype) for p in partials]
    results = pl.pallas_call(
        body, name=name, out_shape=held + held,
        in_specs=[HBM_SPEC] * (2 * na) + [sem_spec] * (2 * n_copies) + [pl.BlockSpec(memory_space=pl.ANY)],
        out_specs=[HBM_SPEC] * (2 * na), input_output_aliases={a: a for a in range(2 * na)},
        compiler_params=pltpu.CompilerParams(has_side_effects=pltpu.SideEffectType.DATAFLOW_SIDE_EFFECTING),
    )(*partials, *lands, *sems, after)
    return results[:na], results[na:]


def _small_rows():
    first, row = {}, 0
    for name, (r, c) in SMALL_VIEWS.items():
        first[name] = row
        row += r * (-(-c // LANES))
    return first, row, -(-(row + 1) // 64) * 64


def _pack_small(arrays, loss):
    _, used, rows = _small_rows()
    parts = []
    for name, (r, c) in SMALL_VIEWS.items():
        a = arrays[name].reshape(r, c)
        parts.append(jnp.pad(a, ((0, 0), (0, LANES - c))) if c < LANES else a.reshape(r * c // LANES, LANES))
    parts += [loss, jnp.zeros((rows - used - 1, LANES), F32)]
    return jnp.concatenate(parts, axis=0)


def _adamw_small_call(ws, packed_grads, ms, vs):
    names = list(SMALL_VIEWS)
    first, _, _ = _small_rows()
    n = len(names)

    def body(g_ref, *refs):
        w_refs, m_refs, v_refs = refs[:n], refs[n:2 * n], refs[2 * n:3 * n]
        outs = refs[3 * n:]
        for a, name in enumerate(names):
            r, c = SMALL_VIEWS[name]
            r0 = first[name]
            if c == LANES:
                g = g_ref[r0:r0 + r, :]
            elif c < LANES:
                g = g_ref[r0:r0 + 1, 0:c]
            else:
                g = jnp.concatenate([g_ref[r0 + j:r0 + j + 1, :] for j in range(c // LANES)], axis=1)
            delta, m, v = _adamw(w_refs[a][...], g, m_refs[a][...], v_refs[a][...])
            for k, val in enumerate((g, delta, m, v)):
                outs[k * n + a][...] = val

    shapes = [jax.ShapeDtypeStruct(SMALL_VIEWS[name], F32) for name in names]
    outs = pl.pallas_call(
        body, out_shape=shapes * 4, compiler_params=pltpu.CompilerParams(vmem_limit_bytes=VMEM_LIMIT),
        name="adamw_small",
    )(packed_grads, *[ws[k] for k in names], *[ms[k] for k in names], *[vs[k] for k in names])
    return [dict(zip(names, outs[k * n:(k + 1) * n])) for k in range(4)]


def kernel(x, g_mix, w_in, b_in, sinks, w_pool, b_pool, pool_scale, w_out, b_out, g_ffn, w_gate, w_up, w_down, g_final, loss_target, m_g_mix, m_w_in, m_b_in, m_sinks, m_w_pool, m_b_pool, m_pool_scale, m_w_out, m_b_out, m_g_ffn, m_w_gate, m_w_up, m_w_down, m_g_final, v_g_mix, v_w_in, v_b_in, v_sinks, v_w_pool, v_b_pool, v_pool_scale, v_w_out, v_b_out, v_g_ffn, v_w_gate, v_w_up, v_w_down, v_g_final):
    given = dict(locals())
    big_names = ("w_in", "w_out", "w_gate", "w_up", "w_down")
    out_shapes = {n: given[n].shape for n in SMALL_NAMES + big_names}
    c = lax.axis_index("c")

    me = 2 * lax.axis_index("x") + lax.axis_index("y")
    transposed = ("w_in", "w_gate", "w_up")

    def shard_view(a, name):
        return a[0].T if name in transposed else a[0]

    placed = _cast_place_call([shard_view(given[n], n) for n in big_names], me)

    small = dict(g_mix=g_mix, b_in=b_in, sinks=sinks, w_pool=w_pool[0], b_pool=b_pool.reshape(1, POOL_WIDTH),
                 pool_scale=pool_scale.reshape(1, POOL_WIDTH), b_out=b_out, g_ffn=g_ffn,
                 g_final=g_final.reshape(1, D_MODEL))
    loss, grad_x, small_grads, big_grads = _local_step(
        x[0], loss_target[0], small, placed[0], placed[1], placed[2:], place=(me, c))

    packed = _pack_small(small_grads, loss)
    grads = [
        big_grads["w_in"].reshape(N_SHARDS, IN_WIDTH // N_SHARDS, D_MODEL),
        big_grads["w_out"].reshape(N_SHARDS, D_MODEL // N_SHARDS, D_MODEL),
        jnp.broadcast_to(packed[None], (N_SHARDS,) + packed.shape),
    ]
    def owned(prefix, names):
        return [shard_view(given[prefix + n], n) for n in names]

    def owned_small(prefix):
        return {n: given[prefix + n].reshape(SMALL_VIEWS[n]) for n in SMALL_NAMES}

    ffn_names, rest_names = big_names[2:], big_names[:2]
    first = _exchange_call(_both(_sibling_exchange(grads), _sibling_allgather([big_grads[n] for n in ffn_names])),
                           "reduce_sibling_exchange")
    landed, ffn_totals = first[:len(grads)], first[len(grads):]
    partials = _pair_sum_call(grads, landed, c, [BF16, BF16, F32])
    sems, held, lands, token = _chip_exchange_start_call(partials, "reduce_chip_exchange_start")
    ffn_updates = _adamw_call(owned("", ffn_names), ffn_totals, owned("m_", ffn_names), owned("v_", ffn_names),
                              "adamw_ffn", after=token)
    held, slots = _chip_exchange_wait_call(sems, held, lands, ffn_updates[0][0], "reduce_chip_exchange_wait")
    t_in, t_out, t_small = _exchange_call(_sibling_allgather(_chip_sum_call(held, slots, me, c)),
                                          "reduce_sibling_allgather")
    loss = t_small[_small_rows()[1], 0]

    rest_updates = _adamw_call(owned("", rest_names), [t_in, t_out], owned("m_", rest_names), owned("v_", rest_names),
                               "adamw_attention")
    big = [[t_in, t_out] + list(ffn_totals)] + [list(r) + list(f) for r, f in zip(rest_updates, ffn_updates)]
    small_groups = _adamw_small_call(owned_small(""), t_small, owned_small("m_"), owned_small("v_"))

    order = ("g_mix", "w_in", "b_in", "sinks", "w_pool", "b_pool", "pool_scale", "w_out", "b_out", "g_ffn",
             "w_gate", "w_up", "w_down", "g_final")
    results = [loss, grad_x.reshape(x.shape)]
    for group, small_group in zip(big, small_groups):
        named = {n: (a.T if n in transposed else a) for n, a in zip(big_names, group)}
        named.update(small_group)
        results += [named[n].reshape(out_shapes[n]) for n in order]
    return tuple(results)
```

```python
import functools
from typing import Any, Callable, Mapping, NamedTuple, Sequence

import jax
import jax.numpy as jnp
from jax import lax
from jax.experimental import pallas as pl
from jax.experimental.pallas import tpu as pltpu

F32 = jnp.float32
BF16 = jnp.bfloat16

D_MODEL = 1024
ATTN_WIDTH = 512
KV_WIDTH = 128
POOL_WIDTH = 512
IN_WIDTH = ATTN_WIDTH + 2 * KV_WIDTH + POOL_WIDTH
HEAD_DIM = 64
N_Q_HEADS = 8
BLOCK = 128
POOL_SIZES = (2, 4, 8, 16)
POOL_HALO = 16
ROPE_THETA = 10000.0
RMS_EPS = 1e-5
Q_SCALE = HEAD_DIM ** -0.5

ADAM_LR = 0.001
ADAM_B1 = 0.9
ADAM_B2 = 0.999
ADAM_EPS = 1e-08
ADAM_WD = 0.01
ADAM_STEP = 10

N_SHARDS = 4
LANES = 128
VMEM_LIMIT = 56 * 1024 * 1024
FFN_TOKEN_TILE = 256

MESH = pl.DeviceIdType.MESH
HBM_SPEC = pl.BlockSpec(memory_space=pltpu.HBM)

SMALL_VIEWS = dict(w_pool=(4 * LANES, LANES), b_pool=(4, LANES), pool_scale=(4, LANES), g_mix=(1, D_MODEL),
                   b_in=(1, IN_WIDTH), b_out=(1, D_MODEL), g_ffn=(1, D_MODEL), g_final=(1, D_MODEL),
                   sinks=(1, N_Q_HEADS))
SMALL_NAMES = tuple(SMALL_VIEWS)


def _dot(a, b):
    return jnp.dot(a, b, preferred_element_type=F32)


def _dot_nt(a, b):
    return lax.dot_general(a, b, (((1,), (1,)), ((), ())), preferred_element_type=F32)


def _dot_tn(a, b):
    return lax.dot_general(a, b, (((0,), (0,)), ((), ())), preferred_element_type=F32)


def _params(*semantics):
    return pltpu.CompilerParams(dimension_semantics=semantics, vmem_limit_bytes=VMEM_LIMIT)


def _resident(shape):
    return pl.BlockSpec(shape, lambda *_: (0,) * len(shape), pipeline_mode=pl.Buffered(1))


def _token_tile(t, largest=512):
    for tm in (2048, 1024, 512, 256, 128):
        if tm <= largest and t % tm == 0:
            return tm
    raise ValueError(f"sequence length {t} is not a multiple of 128")


def _rms_stats(xf):
    r = lax.rsqrt(jnp.mean(xf * xf, axis=-1, keepdims=True) + RMS_EPS)
    return xf * r, r


def _rms_bwd(dy, xh, r, g):
    dxh = dy * g
    return r * (dxh - xh * jnp.mean(dxh * xh, axis=-1, keepdims=True))


def _rope_partner(t, first):
    return jnp.where(first, pltpu.roll(t, LANES - HEAD_DIM // 2, 1), pltpu.roll(t, HEAD_DIM // 2, 1))


def _first_half_mask(shape):
    lane = lax.broadcasted_iota(jnp.int32, shape, 1)
    return (lane % HEAD_DIM) < (HEAD_DIM // 2)


def _rope_tables(t):
    inv_freq = 1.0 / (ROPE_THETA ** (jnp.arange(0, HEAD_DIM, 2, dtype=F32) / HEAD_DIM))
    ang = jnp.arange(t, dtype=F32)[:, None] * inv_freq[None, :]
    cos, sin = jnp.cos(ang), jnp.sin(ang)
    cos_t = jnp.tile(jnp.concatenate([cos, cos], axis=-1), (1, LANES // HEAD_DIM))
    sin_t = jnp.tile(jnp.concatenate([-sin, sin], axis=-1), (1, LANES // HEAD_DIM))
    return cos_t, sin_t


class _Exchange(NamedTuple):
    inputs: Sequence[Any]
    out_shapes: Sequence[Any]
    aliases: Mapping[int, int]
    sems: Sequence[Any]
    start: Callable[..., None]
    finish: Callable[..., None]
    relay: Callable[..., None] = None
    relay_early: int = 0


def _compute_call(body, operands, *, grid, in_specs, out_specs, out_shape, scratch_shapes=(), semantics, name,
                  exchange=None):
    n_in, n_out, n_scr = len(in_specs), len(out_specs), len(scratch_shapes)
    if exchange is None:
        return pl.pallas_call(body, grid=grid, in_specs=in_specs, out_specs=out_specs, out_shape=out_shape,
                              scratch_shapes=scratch_shapes, compiler_params=_params(*semantics),
                              name=name)(*operands), []
    k_in, k_out = len(exchange.inputs), len(exchange.out_shapes)

    def hosting(*refs):
        ins, rest = refs[:n_in], refs[n_in:]
        ex_ins, rest = rest[:k_in], rest[k_in:]
        outs, rest = rest[:n_out], rest[n_out:]
        ex_outs, rest = rest[:k_out], rest[k_out:]
        scratch, sems = rest[:n_scr], rest[n_scr:]
        steps = [pl.program_id(d) for d in range(len(grid))]
        is_first = functools.reduce(jnp.logical_and, [s == 0 for s in steps])
        is_last = functools.reduce(jnp.logical_and, [s == g - 1 for s, g in zip(steps, grid)])
        is_relay = functools.reduce(jnp.logical_and, [s == max(g - 1 - exchange.relay_early, 0)
                                                      for s, g in zip(steps, grid)])

        @pl.when(is_first)
        def _():
            exchange.start(ex_ins, ex_outs, sems)

        body(*ins, *outs, *scratch)

        if exchange.relay is not None:
            @pl.when(is_relay)
            def _():
                exchange.relay(ex_ins, ex_outs, sems)

        @pl.when(is_last)
        def _():
            exchange.finish(ex_ins, ex_outs, sems)

    results = pl.pallas_call(
        hosting, grid=grid,
        in_specs=list(in_specs) + [HBM_SPEC] * k_in, out_specs=list(out_specs) + [HBM_SPEC] * k_out,
        out_shape=list(out_shape) + list(exchange.out_shapes),
        scratch_shapes=list(scratch_shapes) + list(exchange.sems),
        input_output_aliases={n_in + i: n_out + j for i, j in exchange.aliases.items()},
        compiler_params=_params(*semantics), name=name,
    )(*operands, *exchange.inputs)
    return results[:n_out], results[n_out:]


def _exchange_call(exchange, name):
    n_in, n_out = len(exchange.inputs), len(exchange.out_shapes)

    def body(*refs):
        ins, outs, sems = refs[:n_in], refs[n_in:n_in + n_out], refs[n_in + n_out:]
        exchange.start(ins, outs, sems)
        if exchange.relay is not None:
            exchange.relay(ins, outs, sems)
        exchange.finish(ins, outs, sems)

    return pl.pallas_call(
        body, in_specs=[HBM_SPEC] * n_in, out_specs=[HBM_SPEC] * n_out, out_shape=list(exchange.out_shapes),
        input_output_aliases=dict(exchange.aliases), scratch_shapes=list(exchange.sems),
        compiler_params=pltpu.CompilerParams(has_side_effects=True), name=name,
    )(*exchange.inputs)


def _mesh_place():
    x, y, c = lax.axis_index("x"), lax.axis_index("y"), lax.axis_index("c")
    other_chips = [(1 - x, y), (x, 1 - y), (1 - x, 1 - y)]
    return x, y, c, other_chips


def _half(ref, c, rows):
    return ref.at[pl.ds(pl.multiple_of(c * rows, 16), rows), :]


def _remote(src, dst, send_sems, recv_sems, idx, to):
    return functools.partial(pltpu.make_async_remote_copy, src_ref=src, dst_ref=dst, send_sem=send_sems.at[idx],
                             recv_sem=recv_sems.at[idx], device_id=to, device_id_type=MESH)


def _gather_exchange(bufs, relay_early=0):
    na = len(bufs)

    def copies(outs, sems):
        send_sems, recv_sems = sems
        x, y, c, other_chips = _mesh_place()
        me, sibling = 2 * x + y, (x, y, 1 - c)
        ici_out, ici_in, fwd_out, fwd_in = [], [], [], []
        for j, (cx, cy) in enumerate(other_chips):
            for a in range(na):
                rows = outs[a].shape[1] // 2
                mine = _half(outs[a].at[me], c, rows)
                landed = _half(outs[a].at[2 * cx + cy], c, rows)
                passed = _half(outs[a].at[2 * cx + cy], 1 - c, rows)
                ici_out.append(_remote(mine, mine, send_sems, recv_sems, j * na + a, (cx, cy, c)))
                ici_in.append(_remote(landed, landed, send_sems, recv_sems, j * na + a, (cx, cy, c)))
                fwd_out.append(_remote(landed, landed, send_sems, recv_sems, (3 + j) * na + a, sibling))
                fwd_in.append(_remote(passed, passed, send_sems, recv_sems, (3 + j) * na + a, sibling))
        return ici_out, ici_in, fwd_out, fwd_in

    def start(ins, outs, sems):
        for cp in copies(outs, sems)[0]:
            cp().start()

    def relay(ins, outs, sems):
        _, ici_in, fwd_out, _ = copies(outs, sems)
        for arrived, forward in zip(ici_in, fwd_out):
            arrived().wait_recv()
            forward().start()

    def finish(ins, outs, sems):
        ici_out, _, fwd_out, fwd_in = copies(outs, sems)
        for cp in fwd_in:
            cp().wait_recv()
        for cp in ici_out + fwd_out:
            cp().wait_send()

    return _Exchange(inputs=bufs, out_shapes=[jax.ShapeDtypeStruct(b.shape, b.dtype) for b in bufs],
                     aliases={a: a for a in range(na)},
                     sems=[pltpu.SemaphoreType.DMA((6 * na,)), pltpu.SemaphoreType.DMA((6 * na,))],
                     start=start, finish=finish, relay=relay, relay_early=relay_early)


def _sibling_exchange(grads):
    na = len(grads)

    def copies(ins, outs, sems):
        x, y, c, _ = _mesh_place()
        out = []
        for a in range(na):
            rows = ins[a].shape[1] // 2
            src = ins[a].at[:, pl.ds(pl.multiple_of((1 - c) * rows, 8), rows), :]
            out.append(_remote(src, outs[a], sems[0], sems[1], a, (x, y, 1 - c)))
        return out

    def start(ins, outs, sems):
        for cp in copies(ins, outs, sems):
            cp().start()

    def finish(ins, outs, sems):
        for cp in copies(ins, outs, sems):
            cp().wait_recv()
            cp().wait_send()

    return _Exchange(inputs=grads,
                     out_shapes=[jax.ShapeDtypeStruct((g.shape[0], g.shape[1] // 2, g.shape[2]), g.dtype)
                                 for g in grads],
                     aliases={}, sems=[pltpu.SemaphoreType.DMA((na,)), pltpu.SemaphoreType.DMA((na,))],
                     start=start, finish=finish)


def _sibling_allgather(bufs):
    na = len(bufs)

    def piece(outs, sems, a, half):
        x, y, c, _ = _mesh_place()
        rows = outs[a].shape[0] // 2
        rows_ref = outs[a].at[pl.ds(pl.multiple_of(half * rows, 8), rows), :]
        return _remote(rows_ref, rows_ref, sems[0], sems[1], a, (x, y, 1 - c))

    def start(ins, outs, sems):
        c = lax.axis_index("c")
        for a in range(na):
            piece(outs, sems, a, c)().start()

    def finish(ins, outs, sems):
        c = lax.axis_index("c")
        for a in range(na):
            piece(outs, sems, a, 1 - c)().wait_recv()
        for a in range(na):
            piece(outs, sems, a, c)().wait_send()

    return _Exchange(inputs=bufs, out_shapes=[jax.ShapeDtypeStruct(b.shape, b.dtype) for b in bufs],
                     aliases={a: a for a in range(na)},
                     sems=[pltpu.SemaphoreType.DMA((na,)), pltpu.SemaphoreType.DMA((na,))],
                     start=start, finish=finish)


def _norm_call(x, g_mix, exchange=None):
    t = x.shape[0]
    tm = _token_tile(t)

    def body(x_ref, g_ref, h_ref):
        xh, _ = _rms_stats(x_ref[...])
        h_ref[...] = (xh * g_ref[...]).astype(BF16)

    row = pl.BlockSpec((tm, D_MODEL), lambda i: (i, 0))
    return _compute_call(body, (x, g_mix), grid=(t // tm,), in_specs=[row, _resident((1, D_MODEL))],
                         out_specs=[row], out_shape=[jax.ShapeDtypeStruct((t, D_MODEL), BF16)],
                         semantics=("parallel",), name="norm_fwd", exchange=exchange)


def _inproj_call(h0, w_in, b_in, cos_t, sin_t, exchange=None):
    t = h0.shape[0]
    tm = _token_tile(t, 1024)

    def body(h_ref, w_ref, b_ref, c_ref, s_ref, q_ref, k_ref, v_ref, u_ref):
        z = _dot_nt(h_ref[...], w_ref[...]) + b_ref[...]
        cos, sin = c_ref[...], s_ref[...]
        first = _first_half_mask((tm, LANES))
        for j in range(ATTN_WIDTH // LANES):
            zj = z[:, LANES * j:LANES * (j + 1)]
            q_ref[:, LANES * j:LANES * (j + 1)] = ((zj * cos + _rope_partner(zj, first) * sin) * Q_SCALE).astype(BF16)
        zk = z[:, ATTN_WIDTH:ATTN_WIDTH + KV_WIDTH]
        k_ref[...] = (zk * cos + _rope_partner(zk, first) * sin).astype(BF16)
        v_ref[...] = z[:, ATTN_WIDTH + KV_WIDTH:ATTN_WIDTH + 2 * KV_WIDTH].astype(BF16)
        u_ref[...] = z[:, ATTN_WIDTH + 2 * KV_WIDTH:]

    row = lambda w: pl.BlockSpec((tm, w), lambda i: (i, 0))
    return _compute_call(
        body, (h0, w_in, b_in, cos_t, sin_t), grid=(t // tm,),
        in_specs=[row(D_MODEL), _resident((IN_WIDTH, D_MODEL)), _resident((1, IN_WIDTH)), row(LANES), row(LANES)],
        out_specs=[row(ATTN_WIDTH), row(KV_WIDTH), row(KV_WIDTH), row(POOL_WIDTH)],
        out_shape=[jax.ShapeDtypeStruct((t, ATTN_WIDTH), BF16),
                   jax.ShapeDtypeStruct((t, KV_WIDTH), BF16), jax.ShapeDtypeStruct((t, KV_WIDTH), BF16),
                   jax.ShapeDtypeStruct((t, POOL_WIDTH), F32)],
        semantics=("parallel",), name="inproj_fwd", exchange=exchange)


def _head_in_both_halves(band, kv_head):
    lane = lax.broadcasted_iota(jnp.int32, band.shape, 1)
    own = (lane < HEAD_DIM) if kv_head == 0 else (lane >= HEAD_DIM)
    return jnp.where(own, band, pltpu.roll(band, HEAD_DIM, 1)).astype(BF16)


def _stack_heads(ref, kv_head, row0):
    low = lax.broadcasted_iota(jnp.int32, (BLOCK, LANES), 1) < HEAD_DIM
    parts = []
    for jj in range(2):
        slab = ref[row0:row0 + BLOCK, LANES * (2 * kv_head + jj):LANES * (2 * kv_head + jj + 1)]
        zero = jnp.zeros_like(slab)
        parts += [jnp.where(low, slab, zero), jnp.where(low, zero, slab)]
    return jnp.concatenate(parts, axis=0)


def _band_biases():
    c = lax.broadcasted_iota(jnp.int32, (2 * BLOCK, 4 * BLOCK), 0)
    i = lax.broadcasted_iota(jnp.int32, (2 * BLOCK, 4 * BLOCK), 1) % BLOCK
    vis = (c > i) & (c <= i + BLOCK)
    return jnp.where(vis, 0.0, -jnp.inf).astype(F32), jnp.where(vis & (c >= BLOCK), 0.0, -jnp.inf).astype(F32)


def _blocks_per_step(nb):
    for g in (8, 4, 2, 1):
        if nb % g == 0:
            return g


def _sink_row(sink_ref, kv_head):
    return jnp.concatenate([jnp.full((1, BLOCK), sink_ref[0, 4 * kv_head + g], F32) for g in range(4)], axis=1)


def _kv_band(cur_ref, prev_ref, g):
    r0 = g * BLOCK
    prev = prev_ref[...] if g == 0 else cur_ref[r0 - BLOCK:r0, :]
    return jnp.concatenate([prev, cur_ref[r0:r0 + BLOCK, :]], axis=0).astype(F32)


def _attn_fwd_call(q, k, v, sinks, bias, bias_first, exchange=None):
    t = q.shape[0]
    nb = t // BLOCK
    gb = _blocks_per_step(nb)
    qb = gb * BLOCK

    def body(sink_ref, q_ref, kc_ref, kp_ref, vc_ref, vp_ref, b_ref, bf_ref, o_ref, p_ref, psink_ref):
        n = pl.program_id(0)
        row_low = lax.broadcasted_iota(jnp.int32, (LANES, BLOCK), 0) < HEAD_DIM
        for g in range(gb):
            r0 = g * BLOCK
            kband = _kv_band(kc_ref, kp_ref, g)
            vband_t = _kv_band(vc_ref, vp_ref, g).T
            bias_g = jnp.where(n == 0, bf_ref[...], b_ref[...]) if g == 0 else b_ref[...]
            for kh in range(2):
                k2 = _head_in_both_halves(kband, kh)
                vt_head = vband_t[HEAD_DIM * kh:HEAD_DIM * (kh + 1), :]
                v2_t = jnp.concatenate([vt_head, vt_head], axis=0).astype(BF16)
                s_t = _dot_nt(k2, _stack_heads(q_ref, kh, r0)) + bias_g
                sink = _sink_row(sink_ref, kh)
                m = jnp.maximum(jnp.max(s_t, axis=0, keepdims=True), sink)
                e = jnp.exp(s_t - m)
                e_sink = jnp.exp(sink - m)
                inv = 1.0 / (jnp.sum(e, axis=0, keepdims=True) + e_sink)
                p_t = (e * inv).astype(BF16)
                p_ref[2 * g + kh] = p_t
                psink_ref[2 * g + kh] = e_sink * inv
                o_t = _dot(v2_t, p_t)
                for jj in range(2):
                    slab = jnp.where(row_low, o_t[:, BLOCK * 2 * jj:BLOCK * (2 * jj + 1)],
                                     o_t[:, BLOCK * (2 * jj + 1):BLOCK * (2 * jj + 2)]).T
                    o_ref[r0:r0 + BLOCK, LANES * (2 * kh + jj):LANES * (2 * kh + jj + 1)] = slab.astype(BF16)

    cur = lambda w: pl.BlockSpec((qb, w), lambda n: (n, 0))
    prev = lambda w: pl.BlockSpec((BLOCK, w), lambda n: (jnp.maximum(n * gb - 1, 0), 0))
    return _compute_call(
        body, (sinks, q, k, k, v, v, bias, bias_first), grid=(nb // gb,),
        in_specs=[pl.BlockSpec(memory_space=pltpu.SMEM), cur(ATTN_WIDTH), cur(KV_WIDTH), prev(KV_WIDTH),
                  cur(KV_WIDTH), prev(KV_WIDTH), _resident(bias.shape), _resident(bias.shape)],
        out_specs=[cur(ATTN_WIDTH), pl.BlockSpec((2 * gb, 2 * BLOCK, 4 * BLOCK), lambda n: (n, 0, 0)),
                   pl.BlockSpec((2 * gb, 1, 4 * BLOCK), lambda n: (n, 0, 0))],
        out_shape=[jax.ShapeDtypeStruct((t, ATTN_WIDTH), BF16),
                   jax.ShapeDtypeStruct((2 * nb, 2 * BLOCK, 4 * BLOCK), BF16),
                   jax.ShapeDtypeStruct((2 * nb, 1, 4 * BLOCK), F32)],
        semantics=("parallel",), name="attn_fwd", exchange=exchange)


def _pool_mixed(sc_ref, u_ref, halo_ref, i, tm):
    sc_ref[0:POOL_HALO, :] = jnp.where(i == 0, 0.0, halo_ref[...])
    sc_ref[POOL_HALO:, :] = u_ref[...]
    tok = i * tm + lax.broadcasted_iota(jnp.int32, (tm, 1), 0)
    mixed = []
    for g, size in enumerate(POOL_SIZES):
        lanes = slice(LANES * g, LANES * (g + 1))
        cur = sc_ref[pl.ds(POOL_HALO, tm), lanes]
        acc = cur
        for j in range(1, size):
            acc = acc + sc_ref[pl.ds(POOL_HALO - j, tm), lanes]
        count = jnp.minimum(tok + 1, size).astype(F32)
        mixed.append(acc / count - cur)
    return mixed


def _halo_before(tm, width):
    return pl.BlockSpec((POOL_HALO, width), lambda i: (jnp.maximum(i * (tm // POOL_HALO) - 1, 0), 0))


def _outproj_call(attn, u, x, w_pool, b_pool, pool_scale, w_out, b_out, g_ffn, exchange=None):
    t = x.shape[0]
    tm = _token_tile(t)

    def body(a_ref, u_ref, halo_ref, x_ref, wp_ref, bp_ref, ps_ref, wo_ref, bo_ref, g_ref,
             x1_ref, h1_ref, sc_ref, y_ref):
        i = pl.program_id(0)
        for g, mixed in enumerate(_pool_mixed(sc_ref, u_ref, halo_ref, i, tm)):
            lanes = slice(LANES * g, LANES * (g + 1))
            lin = _dot(mixed.astype(BF16), wp_ref[g]) + bp_ref[:, lanes]
            y_ref[:, lanes] = (lin * ps_ref[:, lanes]).astype(BF16)
        x1 = (x_ref[...] + _dot(a_ref[...], wo_ref[0:ATTN_WIDTH, :]) + _dot(y_ref[...], wo_ref[ATTN_WIDTH:, :])
              + bo_ref[...])
        x1_ref[...] = x1
        xh, _ = _rms_stats(x1)
        h1_ref[...] = (xh * g_ref[...]).astype(BF16)

    row = lambda w: pl.BlockSpec((tm, w), lambda i: (i, 0))
    return _compute_call(
        body, (attn, u, u, x, w_pool, b_pool, pool_scale, w_out, b_out, g_ffn), grid=(t // tm,),
        in_specs=[row(ATTN_WIDTH), row(POOL_WIDTH), _halo_before(tm, POOL_WIDTH), row(D_MODEL),
                  _resident((4, LANES, LANES)), _resident((1, POOL_WIDTH)), _resident((1, POOL_WIDTH)),
                  _resident((D_MODEL, D_MODEL)), _resident((1, D_MODEL)), _resident((1, D_MODEL))],
        out_specs=[row(D_MODEL), row(D_MODEL)],
        out_shape=[jax.ShapeDtypeStruct((t, D_MODEL), F32), jax.ShapeDtypeStruct((t, D_MODEL), BF16)],
        scratch_shapes=[pltpu.VMEM((tm + POOL_HALO, POOL_WIDTH), F32), pltpu.VMEM((tm, POOL_WIDTH), BF16)],
        semantics=("parallel",), name="outproj_fwd", exchange=exchange)


def _ffn_call(h1, x1, target, w_gate, w_up, w_down, g_ffn, g_final):
    t = x1.shape[0]
    tm = _token_tile(t, FFN_TOKEN_TILE)
    ns, fs, _ = w_gate.shape

    def body(h_ref, x1_ref, tgt_ref, wg_ref, wu_ref, wd_ref, g_ref, gf_ref,
             act_ref, dgate_ref, dup_ref, dx1_ref, dx2b_ref, loss_ref, dgf_ref, dg_ref, gate_ref, up_ref):
        @pl.when(pl.program_id(0) == 0)
        def _():
            loss_ref[...] = jnp.zeros_like(loss_ref)
            dgf_ref[...] = jnp.zeros_like(dgf_ref)
            dg_ref[...] = jnp.zeros_like(dg_ref)

        h = h_ref[...]
        x1 = x1_ref[...]
        x2 = x1
        for s in range(ns):
            gate = _dot_nt(h, wg_ref[s])
            up = _dot_nt(h, wu_ref[s])
            act = (gate * jax.nn.sigmoid(gate) * up).astype(BF16)
            gate_ref[s] = gate
            up_ref[s] = up
            act_ref[s] = act
            x2 = x2 + _dot(act, wd_ref[s])
        xh2, r2 = _rms_stats(x2)
        gf = gf_ref[...]
        err = xh2 * gf - tgt_ref[...]
        loss_ref[...] += 0.5 * jnp.sum(err * err) / D_MODEL
        dy = err / D_MODEL
        dgf_ref[...] += jnp.sum(dy * xh2, axis=0, keepdims=True)
        dx2 = _rms_bwd(dy, xh2, r2, gf)
        dx2b = dx2.astype(BF16)
        dx2b_ref[...] = dx2b
        dh = None
        for s in range(ns):
            dact = _dot_nt(dx2b, wd_ref[s])
            gate = gate_ref[s]
            sg = jax.nn.sigmoid(gate)
            dup = (dact * (gate * sg)).astype(BF16)
            dgate = (dact * up_ref[s] * (sg * (1.0 + gate * (1.0 - sg)))).astype(BF16)
            dgate_ref[s] = dgate
            dup_ref[s] = dup
            part = _dot(dgate, wg_ref[s]) + _dot(dup, wu_ref[s])
            dh = part if dh is None else dh + part
        xh1, r1 = _rms_stats(x1)
        dg_ref[...] += jnp.sum(dh * xh1, axis=0, keepdims=True)
        dx1_ref[...] = dx2 + _rms_bwd(dh, xh1, r1, g_ref[...])

    row = lambda w: pl.BlockSpec((tm, w), lambda i: (i, 0))
    shard_act = pl.BlockSpec((ns, tm, fs), lambda i: (0, i, 0))
    act_shape = jax.ShapeDtypeStruct((ns, t, fs), BF16)
    vec = pl.BlockSpec((1, D_MODEL), lambda i: (0, 0))
    return pl.pallas_call(
        body, grid=(t // tm,),
        in_specs=[row(D_MODEL), row(D_MODEL), row(D_MODEL), _resident((ns, fs, D_MODEL)),
                  _resident((ns, fs, D_MODEL)), _resident((ns, fs, D_MODEL)), _resident((1, D_MODEL)),
                  _resident((1, D_MODEL))],
        out_specs=[shard_act, shard_act, shard_act, row(D_MODEL), row(D_MODEL),
                   pl.BlockSpec((1, LANES), lambda i: (0, 0)), vec, vec],
        out_shape=[act_shape, act_shape, act_shape, jax.ShapeDtypeStruct((t, D_MODEL), F32),
                   jax.ShapeDtypeStruct((t, D_MODEL), BF16), jax.ShapeDtypeStruct((1, LANES), F32),
                   jax.ShapeDtypeStruct((1, D_MODEL), F32), jax.ShapeDtypeStruct((1, D_MODEL), F32)],
        scratch_shapes=[pltpu.VMEM((ns, tm, fs), F32), pltpu.VMEM((ns, tm, fs), F32)],
        compiler_params=_params("arbitrary"), name="ffn_fwd_bwd",
    )(h1, x1, target, w_gate, w_up, w_down, g_ffn, g_final)


def _ffn_bwd_weights_call(h1, dgate, dup, act, dx2):
    t = h1.shape[0]
    tm = _token_tile(t, 2048)
    ns, _, fs = dgate.shape

    def body(h_ref, dgate_ref, dup_ref, act_ref, dx2_ref, gg_ref, gu_ref, gd_ref):
        @pl.when(pl.program_id(1) == 0)
        def _():
            gg_ref[...] = jnp.zeros_like(gg_ref)
            gu_ref[...] = jnp.zeros_like(gu_ref)
            gd_ref[...] = jnp.zeros_like(gd_ref)

        h = h_ref[...]
        gg_ref[...] += _dot_tn(dgate_ref[...], h)
        gu_ref[...] += _dot_tn(dup_ref[...], h)
        gd_ref[...] += _dot_tn(act_ref[...], dx2_ref[...])

    row = lambda w: pl.BlockSpec((tm, w), lambda s, i: (i, 0))
    shard_act = pl.BlockSpec((None, tm, fs), lambda s, i: (s, i, 0))
    return pl.pallas_call(
        body, grid=(ns, t // tm),
        in_specs=[row(D_MODEL), shard_act, shard_act, shard_act, row(D_MODEL)],
        out_specs=[pl.BlockSpec((None, fs, D_MODEL), lambda s, i: (s, 0, 0))] * 3,
        out_shape=[jax.ShapeDtypeStruct((ns, fs, D_MODEL), F32)] * 3,
        compiler_params=_params("parallel", "arbitrary"), name="ffn_bwd_weights",
    )(h1, dgate, dup, act, dx2)


def _outproj_bwd_call(dx1, attn, u, w_pool, b_pool, pool_scale, w_out, exchange=None):
    t = dx1.shape[0]
    tm = _token_tile(t, 1024)

    def body(dx1_ref, a_ref, u_ref, halo_ref, wp_ref, bp_ref, ps_ref, wo_ref,
             dattn_ref, dmixed_ref, gwo_ref, gbo_ref, gwp_ref, gbp_ref, gps_ref, sc_ref, y_ref):
        i = pl.program_id(0)

        @pl.when(i == 0)
        def _():
            for ref in (gwo_ref, gbo_ref, gwp_ref, gbp_ref, gps_ref):
                ref[...] = jnp.zeros_like(ref)

        dx1 = dx1_ref[...]
        dx1b = dx1.astype(BF16)
        dcat = _dot_nt(dx1b, wo_ref[...])
        dattn_ref[...] = dcat[:, :ATTN_WIDTH].astype(BF16)
        for g, mixed in enumerate(_pool_mixed(sc_ref, u_ref, halo_ref, i, tm)):
            lanes = slice(LANES * g, LANES * (g + 1))
            mixed_b = mixed.astype(BF16)
            lin = _dot(mixed_b, wp_ref[g]) + bp_ref[:, lanes]
            y_ref[:, lanes] = (lin * ps_ref[:, lanes]).astype(BF16)
            dpool = dcat[:, ATTN_WIDTH + LANES * g:ATTN_WIDTH + LANES * (g + 1)]
            gps_ref[:, lanes] += jnp.sum(dpool * lin, axis=0, keepdims=True)
            dy = dpool * ps_ref[:, lanes]
            gbp_ref[:, lanes] += jnp.sum(dy, axis=0, keepdims=True)
            dy_b = dy.astype(BF16)
            gwp_ref[g] += _dot_tn(mixed_b, dy_b)
            dmixed_ref[:, lanes] = _dot_nt(dy_b, wp_ref[g])
        gwo_ref[0:ATTN_WIDTH, :] += _dot_tn(a_ref[...], dx1b)
        gwo_ref[ATTN_WIDTH:, :] += _dot_tn(y_ref[...], dx1b)
        gbo_ref[...] += jnp.sum(dx1, axis=0, keepdims=True)

    row = lambda w: pl.BlockSpec((tm, w), lambda i: (i, 0))
    const = lambda shape: pl.BlockSpec(shape, lambda i: (0,) * len(shape))
    return _compute_call(
        body, (dx1, attn, u, u, w_pool, b_pool, pool_scale, w_out), grid=(t // tm,),
        in_specs=[row(D_MODEL), row(ATTN_WIDTH), row(POOL_WIDTH), _halo_before(tm, POOL_WIDTH),
                  _resident((4, LANES, LANES)), _resident((1, POOL_WIDTH)), _resident((1, POOL_WIDTH)),
                  _resident((D_MODEL, D_MODEL))],
        out_specs=[row(ATTN_WIDTH), row(POOL_WIDTH), const((D_MODEL, D_MODEL)), const((1, D_MODEL)),
                   const((4, LANES, LANES)), const((1, POOL_WIDTH)), const((1, POOL_WIDTH))],
        out_shape=[jax.ShapeDtypeStruct((t, ATTN_WIDTH), BF16), jax.ShapeDtypeStruct((t, POOL_WIDTH), F32),
                   jax.ShapeDtypeStruct((D_MODEL, D_MODEL), F32), jax.ShapeDtypeStruct((1, D_MODEL), F32),
                   jax.ShapeDtypeStruct((4, LANES, LANES), F32), jax.ShapeDtypeStruct((1, POOL_WIDTH), F32),
                   jax.ShapeDtypeStruct((1, POOL_WIDTH), F32)],
        scratch_shapes=[pltpu.VMEM((tm + POOL_HALO, POOL_WIDTH), F32), pltpu.VMEM((tm, POOL_WIDTH), BF16)],
        semantics=("arbitrary",), name="outproj_bwd", exchange=exchange)


def _attn_bwd_call(q, k, v, dout, probs, p_sinks, cos_t, sin_t, exchange=None):
    t = q.shape[0]
    nb = t // BLOCK
    gb = _blocks_per_step(nb)
    qb = gb * BLOCK
    steps = nb // gb

    def body(q_ref, kc_ref, kp_ref, vc_ref, vp_ref, do_ref, p_ref, psink_ref, cq_ref, sq_ref, ck_ref, sk_ref,
             dq_ref, dk_ref, dv_ref, dsink_ref, newk_ref, newv_ref, carryk_ref, carryv_ref, headk_ref, headv_ref):
        n = pl.program_id(0)
        first = _first_half_mask((BLOCK, LANES))

        @pl.when(n == 0)
        def _():
            dsink_ref[...] = jnp.zeros_like(dsink_ref)

        @pl.when(n < steps)
        def _():
            head_row = lax.broadcasted_iota(jnp.int32, dsink_ref.shape, 0)
            lane2 = lax.broadcasted_iota(jnp.int32, (2 * BLOCK, LANES), 1)
            row_low = lax.broadcasted_iota(jnp.int32, (LANES, BLOCK), 0) < HEAD_DIM
            dsink = jnp.zeros(dsink_ref.shape, F32)
            for g in range(gb):
                r0 = g * BLOCK
                kband = _kv_band(kc_ref, kp_ref, g)
                vband = _kv_band(vc_ref, vp_ref, g)
                kband_t = kband.T
                cq, sq = cq_ref[r0:r0 + BLOCK, :], sq_ref[r0:r0 + BLOCK, :]
                dk_heads, dv_heads = [], []
                for kh in range(2):
                    v2 = _head_in_both_halves(vband, kh)
                    kt_head = kband_t[HEAD_DIM * kh:HEAD_DIM * (kh + 1), :]
                    k2_t = jnp.concatenate([kt_head, kt_head], axis=0).astype(BF16)
                    qst = _stack_heads(q_ref, kh, r0)
                    dost = _stack_heads(do_ref, kh, r0)
                    p_b = p_ref[2 * g + kh]
                    p_t = p_b.astype(F32)
                    dp_t = _dot_nt(v2, dost)
                    delta = jnp.sum(p_t * dp_t, axis=0, keepdims=True)
                    ds_t = (p_t * (dp_t - delta)).astype(BF16)
                    sink_term = psink_ref[2 * g + kh] * delta
                    for h in range(4):
                        val = -jnp.sum(sink_term[:, BLOCK * h:BLOCK * (h + 1)])
                        dsink = dsink + jnp.where(head_row == 4 * kh + h, val, 0.0)
                    dq_t = _dot(k2_t, ds_t)
                    for jj in range(2):
                        slab = jnp.where(row_low, dq_t[:, BLOCK * 2 * jj:BLOCK * (2 * jj + 1)],
                                         dq_t[:, BLOCK * (2 * jj + 1):BLOCK * (2 * jj + 2)]).T
                        dq_ref[r0:r0 + BLOCK, LANES * (2 * kh + jj):LANES * (2 * kh + jj + 1)] = (
                            (slab * cq + _rope_partner(slab * sq, first)) * Q_SCALE).astype(BF16)
                    dk2 = _dot(ds_t, qst)
                    dv2 = _dot(p_b, dost)
                    dk_heads.append(dk2 + pltpu.roll(dk2, HEAD_DIM, 1))
                    dv_heads.append(dv2 + pltpu.roll(dv2, HEAD_DIM, 1))
                bandk = jnp.where(lane2 < HEAD_DIM, dk_heads[0], dk_heads[1])
                bandv = jnp.where(lane2 < HEAD_DIM, dv_heads[0], dv_heads[1])
                if g == 0:
                    headk_ref[...] = bandk[0:BLOCK]
                    headv_ref[...] = bandv[0:BLOCK]
                else:
                    newk_ref[r0 - BLOCK:r0, :] += bandk[0:BLOCK]
                    newv_ref[r0 - BLOCK:r0, :] += bandv[0:BLOCK]
                newk_ref[r0:r0 + BLOCK, :] = bandk[BLOCK:]
                newv_ref[r0:r0 + BLOCK, :] = bandv[BLOCK:]
            dsink_ref[...] += dsink

        @pl.when(n == steps)
        def _():
            headk_ref[...] = jnp.zeros_like(headk_ref)
            headv_ref[...] = jnp.zeros_like(headv_ref)

        @pl.when(n >= 1)
        def _():
            last = qb - BLOCK
            carryk_ref[last:, :] += headk_ref[...]
            carryv_ref[last:, :] += headv_ref[...]
            for g in range(gb):
                rows = slice(g * BLOCK, (g + 1) * BLOCK)
                dk = carryk_ref[rows, :]
                dk_ref[rows, :] = (dk * ck_ref[rows, :] + _rope_partner(dk * sk_ref[rows, :], first)).astype(BF16)
            dv_ref[...] = carryv_ref[...].astype(BF16)

        @pl.when(n < steps)
        def _():
            carryk_ref[...] = newk_ref[...]
            carryv_ref[...] = newv_ref[...]

    cur = lambda w: pl.BlockSpec((qb, w), lambda n: (jnp.minimum(n, steps - 1), 0))
    late = lambda w: pl.BlockSpec((qb, w), lambda n: (jnp.maximum(n - 1, 0), 0))
    prev = lambda w: pl.BlockSpec((BLOCK, w), lambda n: (jnp.maximum(jnp.minimum(n, steps - 1) * gb - 1, 0), 0))
    return _compute_call(
        body, (q, k, k, v, v, dout, probs, p_sinks, cos_t, sin_t, cos_t, sin_t), grid=(steps + 1,),
        in_specs=[cur(ATTN_WIDTH), cur(KV_WIDTH), prev(KV_WIDTH), cur(KV_WIDTH), prev(KV_WIDTH), cur(ATTN_WIDTH),
                  pl.BlockSpec((2 * gb, 2 * BLOCK, 4 * BLOCK), lambda n: (jnp.minimum(n, steps - 1), 0, 0)),
                  pl.BlockSpec((2 * gb, 1, 4 * BLOCK), lambda n: (jnp.minimum(n, steps - 1), 0, 0)),
                  cur(LANES), cur(LANES), late(LANES), late(LANES)],
        out_specs=[cur(ATTN_WIDTH), late(KV_WIDTH), late(KV_WIDTH), pl.BlockSpec((8, LANES), lambda n: (0, 0))],
        out_shape=[jax.ShapeDtypeStruct((t, ATTN_WIDTH), BF16), jax.ShapeDtypeStruct((t, KV_WIDTH), BF16),
                   jax.ShapeDtypeStruct((t, KV_WIDTH), BF16), jax.ShapeDtypeStruct((8, LANES), F32)],
        scratch_shapes=[pltpu.VMEM((qb, LANES), F32), pltpu.VMEM((qb, LANES), F32),
                        pltpu.VMEM((qb, LANES), F32), pltpu.VMEM((qb, LANES), F32),
                        pltpu.VMEM((BLOCK, LANES), F32), pltpu.VMEM((BLOCK, LANES), F32)],
        semantics=("arbitrary",), name="attn_bwd", exchange=exchange)


def _inproj_bwd_call(dq, dk, dv, dmixed, x, dx1, w_in, g_mix, exchange=None):
    t = x.shape[0]
    tm = _token_tile(t, 1024)
    n_tiles = t // tm

    def body(dq_ref, dk_ref, dv_ref, dm_ref, halo_ref, x_ref, dx1_ref, w_ref, g_ref,
             dx_ref, gw_ref, gb_ref, gg_ref, sc_ref):
        i = pl.program_id(0)

        @pl.when(i == 0)
        def _():
            for ref in (gw_ref, gb_ref, gg_ref):
                ref[...] = jnp.zeros_like(ref)

        tok = i * tm + lax.broadcasted_iota(jnp.int32, (tm, 1), 0)
        halo = jnp.where(i == n_tiles - 1, 0.0, halo_ref[...])
        du = []
        for g, size in enumerate(POOL_SIZES):
            lanes = slice(LANES * g, LANES * (g + 1))
            dm = dm_ref[:, lanes]
            sc_ref[0:tm, lanes] = dm / jnp.minimum(tok + 1, size).astype(F32)
            sc_ref[tm:, lanes] = halo[:, lanes] / float(size)
            acc = -dm
            for j in range(size):
                acc = acc + sc_ref[pl.ds(j, tm), lanes]
            du.append(acc)
        dz32 = jnp.concatenate([dq_ref[...].astype(F32), dk_ref[...].astype(F32), dv_ref[...].astype(F32)] + du,
                               axis=1)
        dz = dz32.astype(BF16)
        gb_ref[...] += jnp.sum(dz32, axis=0, keepdims=True)
        xh, r = _rms_stats(x_ref[...])
        g = g_ref[...]
        gw_ref[...] += _dot_tn(dz, (xh * g).astype(BF16))
        dh = _dot(dz, w_ref[...])
        gg_ref[...] += jnp.sum(dh * xh, axis=0, keepdims=True)
        dx_ref[...] = dx1_ref[...] + _rms_bwd(dh, xh, r, g)

    row = lambda w: pl.BlockSpec((tm, w), lambda i: (i, 0))
    const = lambda shape: pl.BlockSpec(shape, lambda i: (0,) * len(shape))
    last_halo = t // POOL_HALO - 1
    halo_after = pl.BlockSpec((POOL_HALO, POOL_WIDTH),
                              lambda i: (jnp.minimum((i + 1) * (tm // POOL_HALO), last_halo), 0))
    return _compute_call(
        body, (dq, dk, dv, dmixed, dmixed, x, dx1, w_in, g_mix), grid=(n_tiles,),
        in_specs=[row(ATTN_WIDTH), row(KV_WIDTH), row(KV_WIDTH), row(POOL_WIDTH), halo_after, row(D_MODEL),
                  row(D_MODEL), _resident((IN_WIDTH, D_MODEL)), _resident((1, D_MODEL))],
        out_specs=[row(D_MODEL), const((IN_WIDTH, D_MODEL)), const((1, IN_WIDTH)), const((1, D_MODEL))],
        out_shape=[jax.ShapeDtypeStruct((t, D_MODEL), F32), jax.ShapeDtypeStruct((IN_WIDTH, D_MODEL), F32),
                   jax.ShapeDtypeStruct((1, IN_WIDTH), F32), jax.ShapeDtypeStruct((1, D_MODEL), F32)],
        scratch_shapes=[pltpu.VMEM((tm + POOL_HALO, POOL_WIDTH), F32)],
        semantics=("arbitrary",), name="inproj_bwd", exchange=exchange)


def _local_step(x, target, small, w_in, w_out, w_ffn, place=None):
    t = x.shape[0]
    on_mesh = place is not None
    cos_t, sin_t = _rope_tables(t)
    w_pool = small["w_pool"].astype(BF16)
    w_ffn = list(w_ffn)

    def gathering(bufs, relay_early=0):
        return _gather_exchange(bufs, relay_early) if on_mesh else None

    (h0,), got = _norm_call(x, small["g_mix"], exchange=gathering([w_in]))
    if on_mesh:
        w_in = got[0].reshape(IN_WIDTH, D_MODEL)
    (q, k, v, u), got = _inproj_call(h0, w_in, small["b_in"], cos_t, sin_t, exchange=gathering([w_out], 3))
    if on_mesh:
        w_out = got[0].reshape(D_MODEL, D_MODEL)
    bias_t, bias_first_t = _band_biases()
    (attn, probs, p_sinks), got = _attn_fwd_call(q, k, v, small["sinks"], bias_t, bias_first_t, exchange=gathering(w_ffn[:2]))
    if on_mesh:
        w_ffn[:2] = got
    (x1, h1), got = _outproj_call(attn, u, x, w_pool, small["b_pool"], small["pool_scale"], w_out,
                                          small["b_out"], small["g_ffn"], exchange=gathering(w_ffn[2:], 4))
    if on_mesh:
        w_ffn[2:] = got
    w_gate, w_up, w_down = w_ffn
    act, dgate, dup, dx1, dx2b, loss, g_g_final, g_g_ffn = _ffn_call(h1, x1, target, w_gate, w_up, w_down,
                                                                      small["g_ffn"], small["g_final"])
    ffn_grads = list(_ffn_bwd_weights_call(h1, dgate, dup, act, dx2b))
    kept = {}

    def outproj_bwd(exchange=None):
        kept["outproj"], results = _outproj_bwd_call(dx1, attn, u, w_pool, small["b_pool"], small["pool_scale"],
                                                     w_out, exchange=exchange)
        return results

    def attn_bwd(exchange=None):
        kept["attn"], results = _attn_bwd_call(q, k, v, kept["outproj"][0], probs, p_sinks, cos_t, sin_t,
                                               exchange=exchange)
        return results

    def inproj_bwd(exchange=None):
        dq, dk, dv, _ = kept["attn"]
        kept["inproj"], results = _inproj_bwd_call(dq, dk, dv, kept["outproj"][1], x, dx1, w_in, small["g_mix"],
                                                   exchange=exchange)
        return results

    if on_mesh:
        me, c = place
        landed = outproj_bwd(_sibling_exchange(ffn_grads))
        partials = _pair_sum_call(ffn_grads, landed, c, [BF16] * 3)
        sems, held, lands, _ = _chip_exchange_start_call(partials, "ffn_chip_exchange_start")
        attn_bwd()
        inproj_bwd()
        held, slots = _chip_exchange_wait_call(sems, held, lands, kept["inproj"][1], "ffn_chip_exchange_wait")
        ffn_grads = _chip_sum_call(held, slots, me, c)
    else:
        outproj_bwd(), attn_bwd()
        inproj_bwd()
    _, _, g_w_out, g_b_out, g_w_pool, g_b_pool, g_pool_scale = kept["outproj"]
    g_sinks = kept["attn"][3]
    grad_x, g_w_in, g_b_in, g_g_mix = kept["inproj"]
    small_grads = dict(g_mix=g_g_mix, b_in=g_b_in, sinks=g_sinks[:, 0], w_pool=g_w_pool,
                       b_pool=g_b_pool, pool_scale=g_pool_scale, b_out=g_b_out, g_ffn=g_g_ffn, g_final=g_g_final)
    big_grads = dict(w_in=g_w_in, w_out=g_w_out, w_gate=ffn_grads[0], w_up=ffn_grads[1], w_down=ffn_grads[2])
    return loss, grad_x, small_grads, big_grads


def _cast_place_call(shards, me):
    na = len(shards)
    split = 2

    def body(me_ref, *refs):
        for a in range(na):
            refs[na + a][...] = refs[a][...].astype(BF16)

    in_specs = [pl.BlockSpec((s.shape[0] // split, s.shape[1]), lambda i, me_ref: (i, 0)) for s in shards]
    out_specs = [pl.BlockSpec((None, s.shape[0] // split, s.shape[1]), lambda i, me_ref: (me_ref[0], i, 0))
                 for s in shards]
    return pl.pallas_call(
        body,
        grid_spec=pltpu.PrefetchScalarGridSpec(num_scalar_prefetch=1, grid=(split,), in_specs=in_specs,
                                               out_specs=out_specs),
        out_shape=[jax.ShapeDtypeStruct((N_SHARDS,) + s.shape, BF16) for s in shards],
        compiler_params=_params("parallel"), name="cast_place_weights",
    )(me.reshape(1), *shards)


def _pair_sum_call(grads, landed, c, wire_dtypes):
    na = len(grads)
    split = 2

    def body(c_ref, *refs):
        ins, lands, outs = refs[:na], refs[na:2 * na], refs[2 * na:]
        for a in range(na):
            outs[a][...] = (ins[a][...] + lands[a][...]).astype(outs[a].dtype)

    in_specs, land_specs, out_specs, views = [], [], [], []
    for g in grads:
        ns, r, cols = g.shape
        rows = r // 2 // split
        views.append(g.reshape(ns, 2, r // 2, cols))
        in_specs.append(pl.BlockSpec((None, None, rows, cols), lambda s, i, c_ref: (s, c_ref[0], i, 0)))
        land_specs.append(pl.BlockSpec((None, rows, cols), lambda s, i, c_ref: (s, i, 0)))
        out_specs.append(pl.BlockSpec((None, rows, cols), lambda s, i, c_ref: (s, i, 0)))
    return pl.pallas_call(
        body,
        grid_spec=pltpu.PrefetchScalarGridSpec(num_scalar_prefetch=1, grid=(N_SHARDS, split),
                                               in_specs=in_specs + land_specs, out_specs=out_specs),
        out_shape=[jax.ShapeDtypeStruct(l.shape, dt) for l, dt in zip(landed, wire_dtypes)],
        compiler_params=_params("parallel", "parallel"), name="reduce_pair_sum",
    )(c.reshape(1), *views, *landed)


def _chip_sum_call(partials, slots, me, c):
    na = len(slots)
    split = 2

    def body(place_ref, *refs):
        mine, lands, outs = refs[:na], refs[na:2 * na], refs[2 * na:]
        for j in range(N_SHARDS):
            @pl.when(place_ref[0] == j)
            def _():
                for a in range(na):
                    terms = [mine[a][...] if k == j else lands[a][k] for k in range(N_SHARDS)]
                    t0, t1, t2, t3 = (term.astype(F32) for term in terms)
                    outs[a][...] = ((t0 + t1) + t2) + t3

    mine_specs, land_specs, out_specs = [], [], []
    for s in slots:
        ns, h, cols = s.shape
        mine_specs.append(pl.BlockSpec((None, h // split, cols), lambda i, place: (place[0], i, 0)))
        land_specs.append(pl.BlockSpec((ns, h // split, cols), lambda i, place: (0, i, 0)))
        out_specs.append(pl.BlockSpec((h // split, cols), lambda i, place: (place[1] * split + i, 0)))
    return pl.pallas_call(
        body,
        grid_spec=pltpu.PrefetchScalarGridSpec(num_scalar_prefetch=1, grid=(split,),
                                               in_specs=mine_specs + land_specs, out_specs=out_specs),
        out_shape=[jax.ShapeDtypeStruct((2 * s.shape[1], s.shape[2]), F32) for s in slots],
        compiler_params=_params("parallel"), name="reduce_chip_sum",
    )(jnp.stack([me, c]), *partials, *slots)


def _adamw(w, g, m, v):
    m = ADAM_B1 * m + (1.0 - ADAM_B1) * g
    v = ADAM_B2 * v + (1.0 - ADAM_B2) * jnp.square(g)
    m_hat = m / (1.0 - ADAM_B1 ** ADAM_STEP)
    v_hat = v / (1.0 - ADAM_B2 ** ADAM_STEP)
    return -ADAM_LR * (m_hat / (jnp.sqrt(v_hat) + ADAM_EPS) + ADAM_WD * w), m, v


def _adamw_call(ws, gs, ms, vs, name, after=None):
    na = len(ws)
    split = 4

    def body(*refs):
        w_refs, g_refs, m_refs, v_refs = (refs[k * na:(k + 1) * na] for k in range(4))
        outs = refs[len(refs) - 3 * na:]
        d_refs, nm_refs, nv_refs = (outs[k * na:(k + 1) * na] for k in range(3))
        for a in range(na):
            d_refs[a][...], nm_refs[a][...], nv_refs[a][...] = _adamw(
                w_refs[a][...], g_refs[a][...], m_refs[a][...], v_refs[a][...])

    specs = [pl.BlockSpec((w.shape[0] // split, w.shape[1]), lambda i: (i, 0)) for w in ws]
    shapes = [jax.ShapeDtypeStruct(w.shape, F32) for w in ws]
    extra_specs = [] if after is None else [pl.BlockSpec(after.shape, lambda i: (0, 0))]
    extra = [] if after is None else [after]
    outs = pl.pallas_call(
        body, grid=(split,), in_specs=specs * 4 + extra_specs, out_specs=specs * 3, out_shape=shapes * 3,
        compiler_params=_params("parallel"), name=name,
    )(*ws, *gs, *ms, *vs, *extra)
    return outs[:na], outs[na:2 * na], outs[2 * na:]


def _chip_exchange_start_call(partials, name):
    na = len(partials)
    n_copies = 3 * na

    def body(*refs):
        ins, lands = refs[:na], refs[na:2 * na]
        sems = refs[2 * na:2 * na + 2 * n_copies]
        token = refs[-1]
        x, y, c, other_chips = _mesh_place()
        me = 2 * x + y
        for j, (cx, cy) in enumerate(other_chips):
            for a in range(na):
                k = j * na + a
                pltpu.make_async_remote_copy(
                    src_ref=ins[a].at[2 * cx + cy], dst_ref=lands[a].at[me], send_sem=sems[k],
                    recv_sem=sems[n_copies + k], device_id=(cx, cy, c), device_id_type=MESH).start()
        token[...] = jnp.zeros_like(token)

    sem_spec = pl.BlockSpec(memory_space=pltpu.SEMAPHORE)
    held = [pltpu.HBM(p.shape, p.dtype) for p in partials]
    results = pl.pallas_call(
        body, name=name,
        out_shape=[pltpu.SemaphoreType.DMA(())] * (2 * n_copies) + held + held + [jax.ShapeDtypeStruct((8, LANES), F32)],
        in_specs=[HBM_SPEC] * (2 * na),
        out_specs=[sem_spec] * (2 * n_copies) + [HBM_SPEC] * (2 * na) + [pl.BlockSpec(memory_space=pltpu.VMEM)],
        input_output_aliases={a: 2 * n_copies + a for a in range(2 * na)},
        compiler_params=pltpu.CompilerParams(has_side_effects=pltpu.SideEffectType.DATAFLOW_SIDE_EFFECTING),
    )(*[pltpu.with_memory_space_constraint(p, pltpu.HBM) for p in partials],
      *[pltpu.with_memory_space_constraint(lax.empty(p.shape, p.dtype), pltpu.HBM) for p in partials])
    sems, rest = results[:2 * n_copies], results[2 * n_copies:]
    return sems, rest[:na], rest[na:2 * na], rest[-1]


def _chip_exchange_wait_call(sems, partials, lands, after, name):
    na = len(partials)
    n_copies = 3 * na

    def body(*refs):
        ins, slots = refs[:na], refs[na:2 * na]
        sem_refs = refs[2 * na:2 * na + 2 * n_copies]
        x, y, c, other_chips = _mesh_place()
        me = 2 * x + y
        for j, (cx, cy) in enumerate(other_chips):
            for a in range(na):
                k = j * na + a
                sent = pltpu.make_async_remote_copy(
                    src_ref=ins[a].at[2 * cx + cy], dst_ref=slots[a].at[me], send_sem=sem_refs[k],
                    recv_sem=sem_refs[n_copies + k], device_id=(cx, cy, c), device_id_type=MESH)
                slot = slots[a].at[2 * cx + cy]
                landed = pltpu.make_async_remote_copy(
                    src_ref=slot, dst_ref=slot, send_sem=sem_refs[k], recv_sem=sem_refs[n_copies + k],
                    device_id=(cx, cy, c), device_id_type=MESH)
                sent.wait_send()
                landed.wait_recv()

    sem_spec = pl.BlockSpec(memory_space=pltpu.SEMAPHORE)
    held = [pltpu.HBM(p.shape, p.dtype) for p in partials]
    results = pl.pallas_call(
        body, name=name, out_shape=held + held,
        in_specs=[HBM_SPEC] * (2 * na) + [sem_spec] * (2 * n_copies) + [pl.BlockSpec(memory_space=pl.ANY)],
        out_specs=[HBM_SPEC] * (2 * na), input_output_aliases={a: a for a in range(2 * na)},
        compiler_params=pltpu.CompilerParams(has_side_effects=pltpu.SideEffectType.DATAFLOW_SIDE_EFFECTING),
    )(*partials, *lands, *sems, after)
    return results[:na], results[na:]


def _small_rows():
    first, row = {}, 0
    for name, (r, c) in SMALL_VIEWS.items():
        first[name] = row
        row += r * (-(-c // LANES))
    return first, row, -(-(row + 1) // 64) * 64


def _pack_small(arrays, loss):
    _, used, rows = _small_rows()
    parts = []
    for name, (r, c) in SMALL_VIEWS.items():
        a = arrays[name].reshape(r, c)
        parts.append(jnp.pad(a, ((0, 0), (0, LANES - c))) if c < LANES else a.reshape(r * c // LANES, LANES))
    parts += [loss, jnp.zeros((rows - used - 1, LANES), F32)]
    return jnp.concatenate(parts, axis=0)


def _adamw_small_call(ws, packed_grads, ms, vs):
    names = list(SMALL_VIEWS)
    first, _, _ = _small_rows()
    n = len(names)

    def body(g_ref, *refs):
        w_refs, m_refs, v_refs = refs[:n], refs[n:2 * n], refs[2 * n:3 * n]
        outs = refs[3 * n:]
        for a, name in enumerate(names):
            r, c = SMALL_VIEWS[name]
            r0 = first[name]
            if c == LANES:
                g = g_ref[r0:r0 + r, :]
            elif c < LANES:
                g = g_ref[r0:r0 + 1, 0:c]
            else:
                g = jnp.concatenate([g_ref[r0 + j:r0 + j + 1, :] for j in range(c // LANES)], axis=1)
            delta, m, v = _adamw(w_refs[a][...], g, m_refs[a][...], v_refs[a][...])
            for k, val in enumerate((g, delta, m, v)):
                outs[k * n + a][...] = val

    shapes = [jax.ShapeDtypeStruct(SMALL_VIEWS[name], F32) for name in names]
    outs = pl.pallas_call(
        body, out_shape=shapes * 4, compiler_params=pltpu.CompilerParams(vmem_limit_bytes=VMEM_LIMIT),
        name="adamw_small",
    )(packed_grads, *[ws[k] for k in names], *[ms[k] for k in names], *[vs[k] for k in names])
    return [dict(zip(names, outs[k * n:(k + 1) * n])) for k in range(4)]


def kernel(x, g_mix, w_in, b_in, sinks, w_pool, b_pool, pool_scale, w_out, b_out, g_ffn, w_gate, w_up, w_down, g_final, loss_target, m_g_mix, m_w_in, m_b_in, m_sinks, m_w_pool, m_b_pool, m_pool_scale, m_w_out, m_b_out, m_g_ffn, m_w_gate, m_w_up, m_w_down, m_g_final, v_g_mix, v_w_in, v_b_in, v_sinks, v_w_pool, v_b_pool, v_pool_scale, v_w_out, v_b_out, v_g_ffn, v_w_gate, v_w_up, v_w_down, v_g_final):
    given = dict(locals())
    big_names = ("w_in", "w_out", "w_gate", "w_up", "w_down")
    out_shapes = {n: given[n].shape for n in SMALL_NAMES + big_names}
    c = lax.axis_index("c")

    me = 2 * lax.axis_index("x") + lax.axis_index("y")
    transposed = ("w_in", "w_gate", "w_up")

    def shard_view(a, name):
        return a[0].T if name in transposed else a[0]

    placed = _cast_place_call([shard_view(given[n], n) for n in big_names], me)

    small = dict(g_mix=g_mix, b_in=b_in, sinks=sinks, w_pool=w_pool[0], b_pool=b_pool.reshape(1, POOL_WIDTH),
                 pool_scale=pool_scale.reshape(1, POOL_WIDTH), b_out=b_out, g_ffn=g_ffn,
                 g_final=g_final.reshape(1, D_MODEL))
    loss, grad_x, small_grads, big_grads = _local_step(
        x[0], loss_target[0], small, placed[0], placed[1], placed[2:], place=(me, c))

    packed = _pack_small(small_grads, loss)
    grads = [
        big_grads["w_in"].reshape(N_SHARDS, IN_WIDTH // N_SHARDS, D_MODEL),
        big_grads["w_out"].reshape(N_SHARDS, D_MODEL // N_SHARDS, D_MODEL),
        jnp.broadcast_to(packed[None], (N_SHARDS,) + packed.shape),
    ]
    def owned(prefix, names):
        return [shard_view(given[prefix + n], n) for n in names]

    def owned_small(prefix):
        return {n: given[prefix + n].reshape(SMALL_VIEWS[n]) for n in SMALL_NAMES}

    ffn_names, rest_names = big_names[2:], big_names[:2]
    landed = _exchange_call(_sibling_exchange(grads), "reduce_sibling_exchange")
    partials = _pair_sum_call(grads, landed, c, [BF16, BF16, F32])
    sems, held, lands, token = _chip_exchange_start_call(partials, "reduce_chip_exchange_start")
    ffn_totals = _exchange_call(_sibling_allgather([big_grads[n] for n in ffn_names]), "ffn_sibling_allgather")
    ffn_updates = _adamw_call(owned("", ffn_names), ffn_totals, owned("m_", ffn_names), owned("v_", ffn_names),
                              "adamw_ffn", after=token)
    held, slots = _chip_exchange_wait_call(sems, held, lands, ffn_updates[0][0], "reduce_chip_exchange_wait")
    t_in, t_out, t_small = _exchange_call(_sibling_allgather(_chip_sum_call(held, slots, me, c)),
                                          "reduce_sibling_allgather")
    loss = t_small[_small_rows()[1], 0]

    rest_updates = _adamw_call(owned("", rest_names), [t_in, t_out], owned("m_", rest_names), owned("v_", rest_names),
                               "adamw_attention")
    big = [[t_in, t_out] + list(ffn_totals)] + [list(r) + list(f) for r, f in zip(rest_updates, ffn_updates)]
    small_groups = _adamw_small_call(owned_small(""), t_small, owned_small("m_"), owned_small("v_"))

    order = ("g_mix", "w_in", "b_in", "sinks", "w_pool", "b_pool", "pool_scale", "w_out", "b_out", "g_ffn",
             "w_gate", "w_up", "w_down", "g_final")
    results = [loss, grad_x.reshape(x.shape)]
    for group, small_group in zip(big, small_groups):
        named = {n: (a.T if n in transposed else a) for n, a in zip(big_names, group)}
        named.update(small_group)
        results += [named[n].reshape(out_shapes[n]) for n in order]
    return tuple(results)
```
